```python
import math
import jax, jax.numpy as jnp
from jax import lax
import numpy as np

D_MODEL = 1024
BATCH = 32
SEQ = 2048
DEPTH = 1

CHUNK = 64
Q_BLOCK = 128

D_MIX = D_MODEL
SB_HEAD_DIM = 64
SB_HEADS = (D_MIX // 2) // SB_HEAD_DIM
SB_W = SB_HEADS * SB_HEAD_DIM
MLA_V_DIM = 64
MLA_HEADS = (D_MIX - SB_W) // MLA_V_DIM
MLA_W = MLA_HEADS * MLA_V_DIM
MLA_NOPE_DIM = 64
MLA_ROPE_DIM = 32
MLA_QK_DIM = MLA_NOPE_DIM + MLA_ROPE_DIM
MLA_Q_RANK = (3 * D_MODEL) // 8
MLA_KV_RANK = D_MODEL // 4
ROPE_BASE = 10000.0

IN_SPLITS = (SB_W, SB_W, SB_W, MLA_Q_RANK, MLA_KV_RANK, MLA_ROPE_DIM)
D_IN = sum(IN_SPLITS)

D_FF = 4 * D_MODEL

LN_EPS = 1e-5
RMS_EPS = 1e-6
DEEPNORM_ALPHA = (2.0 * DEPTH) ** 0.25
DEEPNORM_BETA = (8.0 * DEPTH) ** -0.25
N_MOD = 6

kernel_name = "hybrid_sb_mla_deepnorm_adaln_block"


def layer_norm(x, g, b):
    xf = x.astype(jnp.float32)
    mu = jnp.mean(xf, axis=-1, keepdims=True)
    var = jnp.mean(jnp.square(xf - mu), axis=-1, keepdims=True)
    return ((xf - mu) * lax.rsqrt(var + LN_EPS) * g.astype(jnp.float32) + b.astype(jnp.float32)).astype(x.dtype)


def rms_norm(x, g):
    xf = x.astype(jnp.float32)
    ms = jnp.mean(jnp.square(xf), axis=-1, keepdims=True)
    return (xf * lax.rsqrt(ms + RMS_EPS) * g.astype(jnp.float32)).astype(x.dtype)


def rope_tables(seq):
    inv_freq = 1.0 / (ROPE_BASE ** (jnp.arange(0, MLA_ROPE_DIM, 2, dtype=jnp.float32) / MLA_ROPE_DIM))
    ang = jnp.arange(seq, dtype=jnp.float32)[:, None] * inv_freq[None, :]
    return jnp.cos(ang), jnp.sin(ang)


def apply_rope(x, cos, sin):
    xf = x.astype(jnp.float32)
    x1, x2 = jnp.split(xf, 2, axis=-1)
    return jnp.concatenate([x1 * cos - x2 * sin, x2 * cos + x1 * sin], axis=-1).astype(x.dtype)


def stick_breaking_block(q_blk, k_pre, v_pre, q_start):
    qb = q_blk.shape[1]
    kl = k_pre.shape[1]
    z = jnp.einsum('bqhd,bkhd->bhqk', q_blk, k_pre).astype(jnp.float32) * (SB_HEAD_DIM ** -0.5)
    t_idx = q_start + jnp.arange(qb)
    s_idx = jnp.arange(kl)
    strict = s_idx[None, :] < t_idx[:, None]
    log_keep = jnp.where(strict, jax.nn.log_sigmoid(-z), 0.0)
    log_later = lax.cumsum(log_keep, axis=3, reverse=True) - log_keep
    w = jnp.where(strict, jnp.exp(jax.nn.log_sigmoid(z) + log_later), 0.0)
    return jnp.einsum('bhqk,bkhd->bqhd', w.astype(v_pre.dtype), v_pre)


def mla_block(qn_blk, qr_blk, kn_pre, kr_pre, v_pre, q_start):
    qb = qn_blk.shape[1]
    kl = kn_pre.shape[1]
    scores = (jnp.einsum('bqhd,bkhd->bhqk', qn_blk, kn_pre)
              + jnp.einsum('bqhr,bkr->bhqk', qr_blk, kr_pre)).astype(jnp.float32) * (MLA_QK_DIM ** -0.5)
    q_chunk = (q_start + jnp.arange(qb)) // CHUNK
    k_chunk = jnp.arange(kl) // CHUNK
    allowed = k_chunk[None, :] <= q_chunk[:, None]
    scores = jnp.where(allowed, scores, jnp.finfo(jnp.float32).min)
    p = jax.nn.softmax(scores, axis=-1)
    return jnp.einsum('bhqk,bkhd->bqhd', p.astype(v_pre.dtype), v_pre)


def _fwd_setup_inputs(seed: int = 0) -> dict:
    key = jax.random.key(seed)
    ks = jax.random.split(key, 24)
    f32 = jnp.float32
    nrm = lambda k, shape, s: jax.random.normal(k, shape, f32) * s
    L, D = DEPTH, D_MODEL
    x = jax.random.normal(ks[0], (BATCH, SEQ, D), f32)
    c = jax.random.normal(ks[1], (BATCH, D), f32)
    ln_in_g = 1.0 + nrm(ks[2], (D,), 0.02)
    ln_in_b = nrm(ks[3], (D,), 0.02)
    w_ada = nrm(ks[4], (L, D, N_MOD * D), 0.1 * D ** -0.5)
    b_ada = nrm(ks[5], (L, N_MOD * D), 0.02)
    w_in_qk = nrm(ks[6], (L, D, 2 * SB_W), D ** -0.5)
    w_in_v = nrm(ks[7], (L, D, SB_W), D ** -0.5) * DEEPNORM_BETA
    w_in_lat = nrm(ks[8], (L, D, MLA_Q_RANK + MLA_KV_RANK + MLA_ROPE_DIM), D ** -0.5)
    w_in = jnp.concatenate([w_in_qk, w_in_v, w_in_lat], axis=-1)
    q_norm_g = 1.0 + nrm(ks[9], (L, MLA_Q_RANK), 0.02)
    kv_norm_g = 1.0 + nrm(ks[10], (L, MLA_KV_RANK), 0.02)
    w_uq = nrm(ks[11], (L, MLA_Q_RANK, MLA_HEADS * MLA_QK_DIM), MLA_Q_RANK ** -0.5)
    w_uk = nrm(ks[12], (L, MLA_KV_RANK, MLA_HEADS * MLA_NOPE_DIM), MLA_KV_RANK ** -0.5)
    w_uv = nrm(ks[13], (L, MLA_KV_RANK, MLA_HEADS * MLA_V_DIM), MLA_KV_RANK ** -0.5) * DEEPNORM_BETA
    w_ukv = jnp.concatenate([w_uk, w_uv], axis=-1)
    w_o = nrm(ks[14], (L, D_MIX, D), D_MIX ** -0.5) * DEEPNORM_BETA
    ln1_g = 1.0 + nrm(ks[15], (L, D), 0.02)
    ln1_b = nrm(ks[16], (L, D), 0.02)
    w_up = nrm(ks[17], (L, D, D_FF), D ** -0.5)
    w_down = nrm(ks[18], (L, D_FF, D), D_FF ** -0.5) * DEEPNORM_BETA
    ln2_g = 1.0 + nrm(ks[19], (L, D), 0.02)
    ln2_b = nrm(ks[20], (L, D), 0.02)
    return {"x": x, "c": c, "ln_in_g": ln_in_g, "ln_in_b": ln_in_b,
            "w_ada": w_ada, "b_ada": b_ada, "w_in": w_in,
            "q_norm_g": q_norm_g, "kv_norm_g": kv_norm_g, "w_uq": w_uq, "w_ukv": w_ukv,
            "w_o": w_o, "ln1_g": ln1_g, "ln1_b": ln1_b,
            "w_up": w_up, "w_down": w_down, "ln2_g": ln2_g, "ln2_b": ln2_b}


def _fwd_reference(x, c, ln_in_g, ln_in_b, w_ada, b_ada, w_in, q_norm_g, kv_norm_g, w_uq, w_ukv,
              w_o, ln1_g, ln1_b, w_up, w_down, ln2_g, ln2_b):
    bsz, seq, _ = x.shape
    n_blocks = seq // Q_BLOCK
    cos, sin = rope_tables(seq)
    split_at = list(np.cumsum(IN_SPLITS)[:-1])
    c_act = jax.nn.silu(c)

    x = layer_norm(x, ln_in_g, ln_in_b)

    for l in range(DEPTH):
        mod = c_act @ w_ada[l] + b_ada[l]
        shift1, scale1, gate1, shift2, scale2, gate2 = [m[:, None, :] for m in jnp.split(mod, N_MOD, axis=-1)]

        h = x * (1.0 + scale1) + shift1
        proj = h @ w_in[l]
        sb_q, sb_k, sb_v, c_q, c_kv, k_rope = jnp.split(proj, split_at, axis=-1)
        sb_q = sb_q.reshape(bsz, seq, SB_HEADS, SB_HEAD_DIM)
        sb_k = sb_k.reshape(bsz, seq, SB_HEADS, SB_HEAD_DIM)
        sb_v = sb_v.reshape(bsz, seq, SB_HEADS, SB_HEAD_DIM)

        q_all = (rms_norm(c_q, q_norm_g[l]) @ w_uq[l]).reshape(bsz, seq, MLA_HEADS, MLA_QK_DIM)
        q_nope, q_rot = q_all[..., :MLA_NOPE_DIM], q_all[..., MLA_NOPE_DIM:]
        q_rot = apply_rope(q_rot, cos[None, :, None, :], sin[None, :, None, :])
        kv = rms_norm(c_kv, kv_norm_g[l]) @ w_ukv[l]
        k_nope = kv[..., :MLA_HEADS * MLA_NOPE_DIM].reshape(bsz, seq, MLA_HEADS, MLA_NOPE_DIM)
        mla_v = kv[..., MLA_HEADS * MLA_NOPE_DIM:].reshape(bsz, seq, MLA_HEADS, MLA_V_DIM)
        k_rot = apply_rope(k_rope, cos[None], sin[None])

        sb_out, mla_out = [], []
        for i in range(n_blocks):
            q0 = i * Q_BLOCK
            q1 = q0 + Q_BLOCK
            sb_out.append(stick_breaking_block(sb_q[:, q0:q1], sb_k[:, :q1], sb_v[:, :q1], q0))
            mla_out.append(mla_block(q_nope[:, q0:q1], q_rot[:, q0:q1], k_nope[:, :q1],
                                     k_rot[:, :q1], mla_v[:, :q1], q0))
        sb_y = jnp.concatenate(sb_out, axis=1).reshape(bsz, seq, SB_W)
        mla_y = jnp.concatenate(mla_out, axis=1).reshape(bsz, seq, MLA_W)
        mix = jnp.concatenate([sb_y, mla_y], axis=-1) @ w_o[l]
        x = layer_norm(DEEPNORM_ALPHA * x + (1.0 + gate1) * mix, ln1_g[l], ln1_b[l])

        h2 = x * (1.0 + scale2) + shift2
        ff = jnp.square(jax.nn.relu(h2 @ w_up[l])) @ w_down[l]
        x = layer_norm(DEEPNORM_ALPHA * x + (1.0 + gate2) * ff, ln2_g[l], ln2_b[l])

    return x


import jax as _jax
import jax.numpy as _jnp

TWIN_FORMAT = 'train_step'
FWD_PARAMS = ['x', 'c', 'ln_in_g', 'ln_in_b', 'w_ada', 'b_ada', 'w_in', 'q_norm_g', 'kv_norm_g', 'w_uq', 'w_ukv', 'w_o', 'ln1_g', 'ln1_b', 'w_up', 'w_down', 'ln2_g', 'ln2_b']
TWIN_WEIGHTS = ['ln_in_g', 'ln_in_b', 'w_ada', 'b_ada', 'w_in', 'q_norm_g', 'kv_norm_g', 'w_uq', 'w_ukv', 'w_o', 'ln1_g', 'ln1_b', 'w_up', 'w_down', 'ln2_g', 'ln2_b']
TWIN_DIFF_INPUT = 'x'
TWIN_INPUTS = ['x', 'c', 'ln_in_g', 'ln_in_b', 'w_ada', 'b_ada', 'w_in', 'q_norm_g', 'kv_norm_g', 'w_uq', 'w_ukv', 'w_o', 'ln1_g', 'ln1_b', 'w_up', 'w_down', 'ln2_g', 'ln2_b', 'loss_target', 'm_ln_in_g', 'm_ln_in_b', 'm_w_ada', 'm_b_ada', 'm_w_in', 'm_q_norm_g', 'm_kv_norm_g', 'm_w_uq', 'm_w_ukv', 'm_w_o', 'm_ln1_g', 'm_ln1_b', 'm_w_up', 'm_w_down', 'm_ln2_g', 'm_ln2_b', 'v_ln_in_g', 'v_ln_in_b', 'v_w_ada', 'v_b_ada', 'v_w_in', 'v_q_norm_g', 'v_kv_norm_g', 'v_w_uq', 'v_w_ukv', 'v_w_o', 'v_ln1_g', 'v_ln1_b', 'v_w_up', 'v_w_down', 'v_ln2_g', 'v_ln2_b']
TWIN_OUTPUTS = ['loss', 'grad_x', 'grad_ln_in_g', 'grad_ln_in_b', 'grad_w_ada', 'grad_b_ada', 'grad_w_in', 'grad_q_norm_g', 'grad_kv_norm_g', 'grad_w_uq', 'grad_w_ukv', 'grad_w_o', 'grad_ln1_g', 'grad_ln1_b', 'grad_w_up', 'grad_w_down', 'grad_ln2_g', 'grad_ln2_b', 'delta_ln_in_g', 'delta_ln_in_b', 'delta_w_ada', 'delta_b_ada', 'delta_w_in', 'delta_q_norm_g', 'delta_kv_norm_g', 'delta_w_uq', 'delta_w_ukv', 'delta_w_o', 'delta_ln1_g', 'delta_ln1_b', 'delta_w_up', 'delta_w_down', 'delta_ln2_g', 'delta_ln2_b', 'new_m_ln_in_g', 'new_m_ln_in_b', 'new_m_w_ada', 'new_m_b_ada', 'new_m_w_in', 'new_m_q_norm_g', 'new_m_kv_norm_g', 'new_m_w_uq', 'new_m_w_ukv', 'new_m_w_o', 'new_m_ln1_g', 'new_m_ln1_b', 'new_m_w_up', 'new_m_w_down', 'new_m_ln2_g', 'new_m_ln2_b', 'new_v_ln_in_g', 'new_v_ln_in_b', 'new_v_w_ada', 'new_v_b_ada', 'new_v_w_in', 'new_v_q_norm_g', 'new_v_kv_norm_g', 'new_v_w_uq', 'new_v_w_ukv', 'new_v_w_o', 'new_v_ln1_g', 'new_v_ln1_b', 'new_v_w_up', 'new_v_w_down', 'new_v_ln2_g', 'new_v_ln2_b']
TWIN_LEAF_KINDS = {'loss': 'loss', 'grad_x': 'grad_x', 'grad_ln_in_g': 'grad_w', 'grad_ln_in_b': 'grad_w', 'grad_w_ada': 'grad_w', 'grad_b_ada': 'grad_w', 'grad_w_in': 'grad_w', 'grad_q_norm_g': 'grad_w', 'grad_kv_norm_g': 'grad_w', 'grad_w_uq': 'grad_w', 'grad_w_ukv': 'grad_w', 'grad_w_o': 'grad_w', 'grad_ln1_g': 'grad_w', 'grad_ln1_b': 'grad_w', 'grad_w_up': 'grad_w', 'grad_w_down': 'grad_w', 'grad_ln2_g': 'grad_w', 'grad_ln2_b': 'grad_w', 'delta_ln_in_g': 'delta_w', 'delta_ln_in_b': 'delta_w', 'delta_w_ada': 'delta_w', 'delta_b_ada': 'delta_w', 'delta_w_in': 'delta_w', 'delta_q_norm_g': 'delta_w', 'delta_kv_norm_g': 'delta_w', 'delta_w_uq': 'delta_w', 'delta_w_ukv': 'delta_w', 'delta_w_o': 'delta_w', 'delta_ln1_g': 'delta_w', 'delta_ln1_b': 'delta_w', 'delta_w_up': 'delta_w', 'delta_w_down': 'delta_w', 'delta_ln2_g': 'delta_w', 'delta_ln2_b': 'delta_w', 'new_m_ln_in_g': 'new_m', 'new_m_ln_in_b': 'new_m', 'new_m_w_ada': 'new_m', 'new_m_b_ada': 'new_m', 'new_m_w_in': 'new_m', 'new_m_q_norm_g': 'new_m', 'new_m_kv_norm_g': 'new_m', 'new_m_w_uq': 'new_m', 'new_m_w_ukv': 'new_m', 'new_m_w_o': 'new_m', 'new_m_ln1_g': 'new_m', 'new_m_ln1_b': 'new_m', 'new_m_w_up': 'new_m', 'new_m_w_down': 'new_m', 'new_m_ln2_g': 'new_m', 'new_m_ln2_b': 'new_m', 'new_v_ln_in_g': 'new_v', 'new_v_ln_in_b': 'new_v', 'new_v_w_ada': 'new_v', 'new_v_b_ada': 'new_v', 'new_v_w_in': 'new_v', 'new_v_q_norm_g': 'new_v', 'new_v_kv_norm_g': 'new_v', 'new_v_w_uq': 'new_v', 'new_v_w_ukv': 'new_v', 'new_v_w_o': 'new_v', 'new_v_ln1_g': 'new_v', 'new_v_ln1_b': 'new_v', 'new_v_w_up': 'new_v', 'new_v_w_down': 'new_v', 'new_v_ln2_g': 'new_v', 'new_v_ln2_b': 'new_v'}


def _forward(args):
    return _fwd_reference(*[args[k] for k in FWD_PARAMS])


def _output_shape():
    out = _jax.eval_shape(lambda: _forward(_fwd_setup_inputs(0)))
    return out.shape, out.dtype

N_MICROBATCH = 1
ADAM_LR = 0.001
ADAM_B1 = 0.9
ADAM_B2 = 0.999
ADAM_EPS = 1e-08
ADAM_WD = 0.01
ADAM_STEP = 10
PER_EXAMPLE_BATCH_AXIS = {'x': 0, 'c': 0, 'loss_target': 0}
SHARED_INPUTS = []
_WEIGHT_DTYPES = {'ln_in_g': _jnp.float32, 'ln_in_b': _jnp.float32, 'w_ada': _jnp.float32, 'b_ada': _jnp.float32, 'w_in': _jnp.float32, 'q_norm_g': _jnp.float32, 'kv_norm_g': _jnp.float32, 'w_uq': _jnp.float32, 'w_ukv': _jnp.float32, 'w_o': _jnp.float32, 'ln1_g': _jnp.float32, 'ln1_b': _jnp.float32, 'w_up': _jnp.float32, 'w_down': _jnp.float32, 'ln2_g': _jnp.float32, 'ln2_b': _jnp.float32}
MOMENT_SCALE = {'ln_in_g': 1.154209e+00, 'ln_in_b': 8.048803e-01, 'w_ada': 9.007402e-02, 'b_ada': 2.206163e-01, 'w_in': 4.450041e-02, 'q_norm_g': 1.254431e-02, 'kv_norm_g': 2.418707e-02, 'w_uq': 9.371053e-03, 'w_ukv': 1.725076e-02, 'w_o': 6.161770e-02, 'ln1_g': 1.198897e+00, 'ln1_b': 7.589324e-01, 'w_up': 7.650014e-02, 'w_down': 3.080687e-01, 'ln2_g': 6.406267e+01, 'ln2_b': 1.366526e+01}


def _to_microbatches(a, axis):
    t = _jnp.moveaxis(a, axis, 0)
    t = t.reshape((N_MICROBATCH, t.shape[0] // N_MICROBATCH) + t.shape[1:])
    return _jnp.moveaxis(t, 1, axis + 1)


def setup_inputs(seed: int = 0) -> dict:
    inp = _fwd_setup_inputs(seed)
    key = _jax.random.fold_in(_jax.random.key(seed), 7919)
    shape, _ = _output_shape()
    out = dict(inp)
    out["loss_target"] = _jax.random.normal(_jax.random.fold_in(key, 0), shape, _jnp.float32)
    for i, name in enumerate(TWIN_WEIGHTS):
        w = inp[name].astype(_jnp.float32)
        if MOMENT_SCALE is None:
            s = _jnp.sqrt(_jnp.mean(_jnp.square(w)) + 1e-30)
        else:
            s = MOMENT_SCALE[name]
        km, kv = _jax.random.split(_jax.random.fold_in(key, i + 1))
        out[name] = w
        out["m_" + name] = s * _jax.random.normal(km, w.shape, _jnp.float32)
        out["v_" + name] = (s * s) * _jax.random.uniform(kv, w.shape, _jnp.float32, 0.5, 1.5)
    if N_MICROBATCH > 1:
        for name, axis in PER_EXAMPLE_BATCH_AXIS.items():
            out[name] = _to_microbatches(out[name], axis)
    return {'x': out['x'], 'c': out['c'], 'ln_in_g': out['ln_in_g'], 'ln_in_b': out['ln_in_b'], 'w_ada': out['w_ada'], 'b_ada': out['b_ada'], 'w_in': out['w_in'], 'q_norm_g': out['q_norm_g'], 'kv_norm_g': out['kv_norm_g'], 'w_uq': out['w_uq'], 'w_ukv': out['w_ukv'], 'w_o': out['w_o'], 'ln1_g': out['ln1_g'], 'ln1_b': out['ln1_b'], 'w_up': out['w_up'], 'w_down': out['w_down'], 'ln2_g': out['ln2_g'], 'ln2_b': out['ln2_b'], 'loss_target': out['loss_target'], 'm_ln_in_g': out['m_ln_in_g'], 'm_ln_in_b': out['m_ln_in_b'], 'm_w_ada': out['m_w_ada'], 'm_b_ada': out['m_b_ada'], 'm_w_in': out['m_w_in'], 'm_q_norm_g': out['m_q_norm_g'], 'm_kv_norm_g': out['m_kv_norm_g'], 'm_w_uq': out['m_w_uq'], 'm_w_ukv': out['m_w_ukv'], 'm_w_o': out['m_w_o'], 'm_ln1_g': out['m_ln1_g'], 'm_ln1_b': out['m_ln1_b'], 'm_w_up': out['m_w_up'], 'm_w_down': out['m_w_down'], 'm_ln2_g': out['m_ln2_g'], 'm_ln2_b': out['m_ln2_b'], 'v_ln_in_g': out['v_ln_in_g'], 'v_ln_in_b': out['v_ln_in_b'], 'v_w_ada': out['v_w_ada'], 'v_b_ada': out['v_b_ada'], 'v_w_in': out['v_w_in'], 'v_q_norm_g': out['v_q_norm_g'], 'v_kv_norm_g': out['v_kv_norm_g'], 'v_w_uq': out['v_w_uq'], 'v_w_ukv': out['v_w_ukv'], 'v_w_o': out['v_w_o'], 'v_ln1_g': out['v_ln1_g'], 'v_ln1_b': out['v_ln1_b'], 'v_w_up': out['v_w_up'], 'v_w_down': out['v_w_down'], 'v_ln2_g': out['v_ln2_g'], 'v_ln2_b': out['v_ln2_b']}


def _loss(weights, diff, rest, loss_target):
    with _jax.named_scope("forward"):
        args = {**rest, TWIN_DIFF_INPUT: diff, **{k: w.astype(_WEIGHT_DTYPES[k]) for k, w in weights.items()}}
        y = _forward(args)
    with _jax.named_scope("loss_head"):
        err = _jnp.square(y.astype(_jnp.float32) - loss_target)
        return 0.5 * _jnp.sum(_jnp.mean(err, axis=-1)) if err.ndim else 0.5 * err


def _adamw(w, g, m, v):
    m = ADAM_B1 * m + (1.0 - ADAM_B1) * g
    v = ADAM_B2 * v + (1.0 - ADAM_B2) * _jnp.square(g)
    m_hat = m / (1.0 - ADAM_B1 ** ADAM_STEP)
    v_hat = v / (1.0 - ADAM_B2 ** ADAM_STEP)
    delta = -ADAM_LR * (m_hat / (_jnp.sqrt(v_hat) + ADAM_EPS) + ADAM_WD * w)
    return delta, m, v


def reference(x, c, ln_in_g, ln_in_b, w_ada, b_ada, w_in, q_norm_g, kv_norm_g, w_uq, w_ukv, w_o, ln1_g, ln1_b, w_up, w_down, ln2_g, ln2_b, loss_target, m_ln_in_g, m_ln_in_b, m_w_ada, m_b_ada, m_w_in, m_q_norm_g, m_kv_norm_g, m_w_uq, m_w_ukv, m_w_o, m_ln1_g, m_ln1_b, m_w_up, m_w_down, m_ln2_g, m_ln2_b, v_ln_in_g, v_ln_in_b, v_w_ada, v_b_ada, v_w_in, v_q_norm_g, v_kv_norm_g, v_w_uq, v_w_ukv, v_w_o, v_ln1_g, v_ln1_b, v_w_up, v_w_down, v_ln2_g, v_ln2_b):
    given = dict(x=x, c=c, ln_in_g=ln_in_g, ln_in_b=ln_in_b, w_ada=w_ada, b_ada=b_ada, w_in=w_in, q_norm_g=q_norm_g, kv_norm_g=kv_norm_g, w_uq=w_uq, w_ukv=w_ukv, w_o=w_o, ln1_g=ln1_g, ln1_b=ln1_b, w_up=w_up, w_down=w_down, ln2_g=ln2_g, ln2_b=ln2_b, loss_target=loss_target, m_ln_in_g=m_ln_in_g, m_ln_in_b=m_ln_in_b, m_w_ada=m_w_ada, m_b_ada=m_b_ada, m_w_in=m_w_in, m_q_norm_g=m_q_norm_g, m_kv_norm_g=m_kv_norm_g, m_w_uq=m_w_uq, m_w_ukv=m_w_ukv, m_w_o=m_w_o, m_ln1_g=m_ln1_g, m_ln1_b=m_ln1_b, m_w_up=m_w_up, m_w_down=m_w_down, m_ln2_g=m_ln2_g, m_ln2_b=m_ln2_b, v_ln_in_g=v_ln_in_g, v_ln_in_b=v_ln_in_b, v_w_ada=v_w_ada, v_b_ada=v_b_ada, v_w_in=v_w_in, v_q_norm_g=v_q_norm_g, v_kv_norm_g=v_kv_norm_g, v_w_uq=v_w_uq, v_w_ukv=v_w_ukv, v_w_o=v_w_o, v_ln1_g=v_ln1_g, v_ln1_b=v_ln1_b, v_w_up=v_w_up, v_w_down=v_w_down, v_ln2_g=v_ln2_g, v_ln2_b=v_ln2_b)
    weights = {n: given[n] for n in TWIN_WEIGHTS}
    shared = {n: given[n] for n in SHARED_INPUTS}
    per_example = {n: given[n] for n in ['x', 'c']}
    grad_fn = _jax.value_and_grad(_loss, argnums=(0, 1))

    def one_microbatch(ex, loss_target):
        ex = dict(ex)
        diff = ex.pop(TWIN_DIFF_INPUT)
        return grad_fn(weights, diff, {**shared, **ex}, loss_target)

    if N_MICROBATCH == 1:
        loss, (grad_w, grad_x) = one_microbatch(per_example, given["loss_target"])
    else:
        def body(carry, xs):
            loss_sum, grad_sum = carry
            l_k, (gw_k, gx_k) = one_microbatch(xs[0], xs[1])
            with _jax.named_scope("update"):
                return (loss_sum + l_k, _jax.tree.map(_jnp.add, grad_sum, gw_k)), gx_k

        init = (_jnp.zeros((), _jnp.float32), _jax.tree.map(_jnp.zeros_like, weights))
        (loss, grad_w), grad_x = _jax.lax.scan(body, init, (per_example, given["loss_target"]))
    with _jax.named_scope("update"):
        delta_w, new_m, new_v = {}, {}, {}
        for n in TWIN_WEIGHTS:
            delta_w[n], new_m[n], new_v[n] = _adamw(weights[n], grad_w[n], given["m_" + n], given["v_" + n])
    return (loss, grad_x, *[grad_w[n] for n in TWIN_WEIGHTS], *[delta_w[n] for n in TWIN_WEIGHTS],
            *[new_m[n] for n in TWIN_WEIGHTS], *[new_v[n] for n in TWIN_WEIGHTS])
```

```python
import functools
import math

import jax
import jax.numpy as jnp
from jax import lax
from jax.experimental import pallas as pl
from jax.experimental.pallas import tpu as pltpu

F32 = jnp.float32
BF16 = jnp.bfloat16

SB_HD = 64
MLA_V = 64
MLA_NOPE = 64
MLA_ROPE = 32
HEAD_PAD = 128
CHUNK = 64
ROPE_BASE = 10000.0
LN_EPS = 1e-5
RMS_EPS = 1e-6
DEPTH = 1
ALPHA = (2.0 * DEPTH) ** 0.25
N_MOD = 6
ADAM_LR = 0.001
ADAM_B1 = 0.9
ADAM_B2 = 0.999
ADAM_EPS = 1e-08
ADAM_WD = 0.01
ADAM_STEP = 10
N_DEV = 8
LANES = 128
VMEM_LIMIT = 56 * 1024 * 1024
MESH = pl.DeviceIdType.MESH


def _dot(a, b):
    return jnp.dot(a, b, preferred_element_type=F32)


def _dot_nt(a, b):
    return lax.dot_general(a, b, (((1,), (1,)), ((), ())), preferred_element_type=F32)


def _dot_tn(a, b):
    return lax.dot_general(a, b, (((0,), (0,)), ((), ())), preferred_element_type=F32)


def _cparams(sem):
    return pltpu.CompilerParams(dimension_semantics=sem, vmem_limit_bytes=VMEM_LIMIT)


def _full(a):
    nd = a.ndim
    return pl.BlockSpec(a.shape, lambda *_: (0,) * nd, pipeline_mode=pl.Buffered(1))


def _tok(tm, w):
    return pl.BlockSpec((1, tm, w), lambda b, s: (b, s, 0))


def _perb(rows, w):
    return pl.BlockSpec((1, rows, w), lambda b, s: (b, 0, 0))


def _sds(shape, dtype):
    return jax.ShapeDtypeStruct(shape, dtype)


def _ln_fwd(x, g, b):
    mu = jnp.mean(x, axis=-1, keepdims=True)
    xc = x - mu
    var = jnp.mean(xc * xc, axis=-1, keepdims=True)
    rstd = lax.rsqrt(var + LN_EPS)
    xhat = xc * rstd
    return xhat * g + b, xhat, rstd


def _ln_bwd(dy, xhat, rstd, g):
    dxh = dy * g
    m1 = jnp.mean(dxh, axis=-1, keepdims=True)
    m2 = jnp.mean(dxh * xhat, axis=-1, keepdims=True)
    return rstd * (dxh - m1 - xhat * m2)


def _colsum(a):
    return jnp.sum(a, axis=0, keepdims=True)


def _rope(x, c, s1, s2):
    w = x.shape[-1]
    return x * c + pltpu.roll(x, w - 16, 1) * s1 + pltpu.roll(x, 16, 1) * s2


def _rope_t(x, c, s1, s2):
    w = x.shape[-1]
    return x * c - pltpu.roll(x, w - 16, 1) * s1 - pltpu.roll(x, 16, 1) * s2


def _adamw(w, g, m, v):
    m = ADAM_B1 * m + (1.0 - ADAM_B1) * g
    v = ADAM_B2 * v + (1.0 - ADAM_B2) * (g * g)
    m_hat = m / (1.0 - ADAM_B1 ** ADAM_STEP)
    v_hat = v / (1.0 - ADAM_B2 ** ADAM_STEP)
    delta = -ADAM_LR * (m_hat / (jnp.sqrt(v_hat) + ADAM_EPS) + ADAM_WD * w)
    return delta, m, v


def _my_place():
    return lax.axis_index("x"), lax.axis_index("y"), lax.axis_index("c")


def _chip_peers(mx, my):
    out = []
    for j in (1, 2, 3):
        px = 1 - mx if (j >> 1) else mx
        py = 1 - my if (j & 1) else my
        out.append((px, py, 2 * px + py))
    return out


def _chip_exchange(x, name, scatter):
    blk = x.shape[1:] if scatter else x.shape

    def body(x_ref, o_ref, ssem, rsem, lsem):
        mx, my, mc = _my_place()
        me = 2 * mx + my
        peers = _chip_peers(mx, my)

        def copy(j, src, slot, px, py):
            return pltpu.make_async_remote_copy(
                src_ref=src, dst_ref=o_ref.at[slot], send_sem=ssem.at[j], recv_sem=rsem.at[j],
                device_id=(px, py, mc), device_id_type=MESH)

        local = pltpu.make_async_copy(x_ref.at[me] if scatter else x_ref, o_ref.at[me], lsem)
        local.start()
        sends = []
        for j, (px, py, pk) in enumerate(peers):
            cp = copy(j, x_ref.at[pk] if scatter else x_ref, me, px, py)
            cp.start()
            sends.append(cp)
        for j, (px, py, pk) in enumerate(peers):
            copy(j, x_ref.at[pk] if scatter else x_ref, pk, px, py).wait_recv()
        for cp in sends:
            cp.wait_send()
        local.wait()

    return pl.pallas_call(
        body, name=name,
        out_shape=_sds((4,) + tuple(blk), x.dtype),
        in_specs=[pl.BlockSpec(memory_space=pl.ANY)],
        out_specs=pl.BlockSpec(memory_space=pl.ANY),
        scratch_shapes=[pltpu.SemaphoreType.DMA((3,)), pltpu.SemaphoreType.DMA((3,)),
                        pltpu.SemaphoreType.DMA(())],
    )(x)


def _core_gather(x, name):
    def body(x_ref, o_ref, ssem, rsem, lsem):
        mx, my, mc = _my_place()
        local = pltpu.make_async_copy(x_ref, o_ref.at[mc], lsem)
        local.start()
        send = pltpu.make_async_remote_copy(
            src_ref=x_ref, dst_ref=o_ref.at[mc], send_sem=ssem, recv_sem=rsem,
            device_id=(mx, my, 1 - mc), device_id_type=MESH)
        send.start()
        pltpu.make_async_remote_copy(
            src_ref=x_ref, dst_ref=o_ref.at[1 - mc], send_sem=ssem, recv_sem=rsem,
            device_id=(mx, my, 1 - mc), device_id_type=MESH).wait_recv()
        send.wait_send()
        local.wait()

    return pl.pallas_call(
        body, name=name,
        out_shape=_sds((2,) + tuple(x.shape), x.dtype),
        in_specs=[pl.BlockSpec(memory_space=pl.ANY)],
        out_specs=pl.BlockSpec(memory_space=pl.ANY),
        scratch_shapes=[pltpu.SemaphoreType.DMA(()), pltpu.SemaphoreType.DMA(()),
                        pltpu.SemaphoreType.DMA(())],
    )(x)


def _core_scatter(g, name):
    def body(g_ref, mine_ref, got_ref, ssem, rsem, lsem):
        mx, my, mc = _my_place()
        local = pltpu.make_async_copy(g_ref.at[mc], mine_ref, lsem)
        local.start()
        send = pltpu.make_async_remote_copy(
            src_ref=g_ref.at[1 - mc], dst_ref=got_ref, send_sem=ssem, recv_sem=rsem,
            device_id=(mx, my, 1 - mc), device_id_type=MESH)
        send.start()
        pltpu.make_async_remote_copy(
            src_ref=g_ref.at[1 - mc], dst_ref=got_ref, send_sem=ssem, recv_sem=rsem,
            device_id=(mx, my, 1 - mc), device_id_type=MESH).wait_recv()
        send.wait_send()
        local.wait()

    blk = _sds(tuple(g.shape[1:]), g.dtype)
    return pl.pallas_call(
        body, name=name,
        out_shape=(blk, blk),
        in_specs=[pl.BlockSpec(memory_space=pl.ANY)],
        out_specs=(pl.BlockSpec(memory_space=pl.ANY), pl.BlockSpec(memory_space=pl.ANY)),
        scratch_shapes=[pltpu.SemaphoreType.DMA(()), pltpu.SemaphoreType.DMA(()),
                        pltpu.SemaphoreType.DMA(())],
    )(g)


def _all_gather(x, name):
    by_chip = _chip_exchange(x, name + "_chips", scatter=False)
    both = _core_gather(by_chip, name + "_cores")
    return jnp.swapaxes(both, 0, 1).reshape((N_DEV,) + tuple(x.shape))


def _ada_partial(c_all, w_ada_loc, b_loc):
    def body(c_ref, w_ref, b_ref, act_ref, mod_ref):
        c = c_ref[...]
        act = c * (1.0 / (1.0 + jnp.exp(-c)))
        act_ref[...] = act
        mod_ref[...] = _dot(act.astype(BF16), w_ref[...].astype(BF16)) + b_ref[...]

    nb, d = c_all.shape
    return pl.pallas_call(
        body, name="ada_partial",
        out_shape=(_sds((nb, d), F32), _sds((nb, w_ada_loc.shape[1]), F32)),
        compiler_params=pltpu.CompilerParams(vmem_limit_bytes=VMEM_LIMIT),
    )(c_all, w_ada_loc, b_loc)


def _inproj_fwd(x, mod, ln_g, ln_b, w_in_p, w_uq_p, w_kv, gq, gkv, tc, ts1, ts2, dm):
    B, S, D = x.shape
    tm = dm["tm"]
    sbw, qr, kvr, nh = dm["sbw"], dm["qr"], dm["kvr"], dm["nh"]
    o_cq, o_ckv, o_kr = 3 * sbw, 3 * sbw + qr, 3 * sbw + qr + kvr
    qpw = nh * HEAD_PAD

    def body(x_ref, mod_ref, g_ref, b_ref, win_ref, wuq_ref, wkv_ref, gq_ref, gkv_ref, tc_ref, ts1_ref, ts2_ref,
             x0_ref, h_ref, q_ref, k_ref, v_ref, qp_ref, kp_ref, mv_ref, cq_ref, ckv_ref, qn_ref, kvn_ref):
        x0, _, _ = _ln_fwd(x_ref[0], g_ref[...], b_ref[...])
        x0_ref[0] = x0
        mod = mod_ref[0]
        h = (x0 * (1.0 + mod[1:2]) + mod[0:1]).astype(BF16)
        h_ref[0] = h
        proj = _dot(h, win_ref[...])
        q_ref[0] = (proj[:, 0:sbw] * (SB_HD ** -0.5)).astype(BF16)
        k_ref[0] = proj[:, sbw:2 * sbw].astype(BF16)
        v_ref[0] = proj[:, 2 * sbw:3 * sbw].astype(BF16)
        cq = proj[:, o_cq:o_cq + qr]
        ckv = proj[:, o_ckv:o_ckv + kvr]
        cq_ref[0] = cq
        ckv_ref[0] = ckv
        qn = (cq * lax.rsqrt(jnp.mean(cq * cq, axis=-1, keepdims=True) + RMS_EPS) * gq_ref[...]).astype(BF16)
        kvn = (ckv * lax.rsqrt(jnp.mean(ckv * ckv, axis=-1, keepdims=True) + RMS_EPS) * gkv_ref[...]).astype(BF16)
        qn_ref[0] = qn
        kvn_ref[0] = kvn
        c1, s1, s2 = tc_ref[...], ts1_ref[...], ts2_ref[...]
        c8, s18, s28 = jnp.tile(c1, (1, nh)), jnp.tile(s1, (1, nh)), jnp.tile(s2, (1, nh))
        qp_ref[0] = _rope(_dot(qn, wuq_ref[...]), c8, s18, s28).astype(BF16)
        kvo = _dot(kvn, wkv_ref[...])
        kr = pltpu.roll(proj[:, o_kr:o_kr + LANES], 64, 1)
        kr = _rope(kr, c1, s1, s2)
        kp_ref[0] = (kvo[:, 0:qpw] + jnp.tile(kr, (1, nh))).astype(BF16)
        mv_ref[0] = kvo[:, qpw:].astype(BF16)

    tab = pl.BlockSpec((tm, LANES), lambda b, s: (s, 0))
    outs = [(D, F32), (D, BF16), (sbw, BF16), (sbw, BF16), (sbw, BF16), (qpw, BF16), (qpw, BF16),
            (nh * MLA_V, BF16), (qr, F32), (kvr, F32), (qr, BF16), (kvr, BF16)]
    return pl.pallas_call(
        body, name="inproj_fwd", grid=(B, S // tm),
        in_specs=[_tok(tm, D), _perb(N_MOD, D), _full(ln_g), _full(ln_b), _full(w_in_p), _full(w_uq_p),
                  _full(w_kv), _full(gq), _full(gkv), tab, tab, tab],
        out_specs=[_tok(tm, w) for w, _ in outs],
        out_shape=[_sds((B, S, w), t) for w, t in outs],
        compiler_params=_cparams(("parallel", "parallel")),
    )(x, mod, ln_g, ln_b, w_in_p, w_uq_p, w_kv, gq, gkv, tc, ts1, ts2)


def _softplus_parts(z):
    e = jnp.exp(-jnp.abs(z))
    a = -(jnp.maximum(z, 0.0) + jnp.log(1.0 + e))
    return a, e


def _split_dot(a, u):
    hi = a.astype(BF16)
    lo = (a - hi.astype(F32)).astype(BF16)
    return _dot(hi, u) + _dot(lo, u)


def _sb_fwd(q, k, v, dm):
    B, S, W = q.shape
    tq = dm["tq"]
    nq = S // tq

    def body(q_ref, k_ref, v_ref, y_ref, tot_ref):
        qi = pl.program_id(2)
        q2 = q_ref[0]
        lane = lax.broadcasted_iota(jnp.int32, (tq, LANES), 1)
        qh = [jnp.where(lane < SB_HD, q2, 0).astype(BF16), jnp.where(lane >= SB_HD, q2, 0).astype(BF16)]
        row = lax.broadcasted_iota(jnp.int32, (tq, tq), 0)
        col = lax.broadcasted_iota(jnp.int32, (tq, tq), 1)
        later = (row > col).astype(BF16)
        strict = col < row

        def block(j, carry, masked):
            off = pl.multiple_of(j * tq, tq)
            k2 = k_ref[0, pl.ds(off, tq), :]
            v2 = v_ref[0, pl.ds(off, tq), :]
            new = []
            for h in range(2):
                acc, run = carry[2 * h], carry[2 * h + 1]
                z = _dot_nt(qh[h], k2)
                a, _ = _softplus_parts(z)
                if masked:
                    a = jnp.where(strict, a, 0.0)
                w = jnp.exp(z + a + _split_dot(a, later) + run)
                if masked:
                    w = jnp.where(strict, w, 0.0)
                new.append(acc + _dot(w.astype(BF16), v2))
                new.append(run + jnp.sum(a, axis=1, keepdims=True))
            return tuple(new)

        zero = jnp.zeros((tq, LANES), F32)
        zrun = jnp.zeros((tq, 1), F32)
        carry = block(qi, (zero, zrun, zero, zrun), True)
        carry = lax.fori_loop(0, qi, lambda jj, c: block(qi - 1 - jj, c, False), carry)
        y_ref[0] = jnp.where(lane < SB_HD, carry[0], carry[2]).astype(BF16)
        tot_ref[0] = jnp.where(lane < SB_HD, carry[1], carry[3])

    qspec = pl.BlockSpec((1, tq, LANES), lambda b, hp, i: (b, i, hp))
    kspec = pl.BlockSpec((1, S, LANES), lambda b, hp, i: (b, 0, hp))
    return pl.pallas_call(
        body, name="sb_fwd", grid=(B, W // LANES, nq),
        in_specs=[qspec, kspec, kspec],
        out_specs=[qspec, qspec],
        out_shape=[_sds((B, S, W), BF16), _sds((B, S, W), F32)],
        compiler_params=_cparams(("parallel", "parallel", "arbitrary")),
    )(q, k, v)


def _sb_bwd(q, k, v, tot, dy, dm):
    B, S, W = q.shape
    tq = dm["tq"]
    nq = S // tq

    def body(q_ref, k_ref, v_ref, tot_ref, dy_ref, dq_ref, dk_ref, dv_ref, dk_acc, dv_acc):
        qi = pl.program_id(2)

        @pl.when(qi == 0)
        def _():
            dk_acc[...] = jnp.zeros_like(dk_acc)
            dv_acc[...] = jnp.zeros_like(dv_acc)

        q2 = q_ref[0]
        dy2 = dy_ref[0]
        tot2 = tot_ref[0]
        lane = lax.broadcasted_iota(jnp.int32, (tq, LANES), 1)
        in_h = [lane < SB_HD, lane >= SB_HD]
        qh = [jnp.where(m, q2, 0).astype(BF16) for m in in_h]
        dyh = [jnp.where(m, dy2, 0).astype(BF16) for m in in_h]
        toth = [tot2[:, 0:1], tot2[:, SB_HD:SB_HD + 1]]
        row = lax.broadcasted_iota(jnp.int32, (tq, tq), 0)
        col = lax.broadcasted_iota(jnp.int32, (tq, tq), 1)
        upto = (row <= col).astype(BF16)
        before = (row < col).astype(BF16)
        strict = col < row

        def block(j, carry, masked):
            off = pl.multiple_of(j * tq, tq)
            k2 = k_ref[0, pl.ds(off, tq), :]
            v2 = v_ref[0, pl.ds(off, tq), :]
            new = []
            dk_blk = jnp.zeros((tq, LANES), F32)
            dv_blk = jnp.zeros((tq, LANES), F32)
            for h in range(2):
                dq, pa, pg = carry[3 * h], carry[3 * h + 1], carry[3 * h + 2]
                z = _dot_nt(qh[h], k2)
                a, e = _softplus_parts(z)
                if masked:
                    a = jnp.where(strict, a, 0.0)
                w = jnp.exp(z + a + (toth[h] - pa - _split_dot(a, upto)))
                if masked:
                    w = jnp.where(strict, w, 0.0)
                g = _dot_nt(dyh[h], v2) * w
                g_before = _dot(g.astype(BF16), before) + pg
                r = 1.0 / (1.0 + e)
                er = e * r
                pos = z >= 0.0
                sig = jnp.where(pos, r, er)
                dz = g * jnp.where(pos, er, r) - g_before * sig
                if masked:
                    dz = jnp.where(strict, dz, 0.0)
                dzb = dz.astype(BF16)
                dv_blk = dv_blk + _dot_tn(w.astype(BF16), dyh[h])
                dk_blk = dk_blk + _dot_tn(dzb, qh[h])
                new += [dq + _dot(dzb, k2), pa + jnp.sum(a, axis=1, keepdims=True),
                        pg + jnp.sum(g, axis=1, keepdims=True)]
            dk_acc[pl.ds(off, tq), :] += dk_blk
            dv_acc[pl.ds(off, tq), :] += dv_blk
            return tuple(new)

        zero = jnp.zeros((tq, LANES), F32)
        zrun = jnp.zeros((tq, 1), F32)
        carry = lax.fori_loop(0, qi, lambda j, c: block(j, c, False), (zero, zrun, zrun, zero, zrun, zrun))
        carry = block(qi, carry, True)
        dq_ref[0] = (jnp.where(in_h[0], carry[0], carry[3]) * (SB_HD ** -0.5)).astype(BF16)

        @pl.when(qi == nq - 1)
        def _():
            dk_ref[0] = dk_acc[...].astype(BF16)
            dv_ref[0] = dv_acc[...].astype(BF16)

    qspec = pl.BlockSpec((1, tq, LANES), lambda b, hp, i: (b, i, hp))
    kspec = pl.BlockSpec((1, S, LANES), lambda b, hp, i: (b, 0, hp))
    return pl.pallas_call(
        body, name="sb_bwd", grid=(B, W // LANES, nq),
        in_specs=[qspec, kspec, kspec, qspec, qspec],
        out_specs=[qspec, kspec, kspec],
        out_shape=[_sds((B, S, W), BF16)] * 3,
        scratch_shapes=[pltpu.VMEM((S, LANES), F32), pltpu.VMEM((S, LANES), F32)],
        compiler_params=_cparams(("parallel", "parallel", "arbitrary")),
    )(q, k, v, tot, dy)


def _chunk_mask(tq):
    row = lax.broadcasted_iota(jnp.int32, (tq, tq), 0)
    col = lax.broadcasted_iota(jnp.int32, (tq, tq), 1)
    return lax.shift_right_logical(col, 6) <= lax.shift_right_logical(row, 6)


def _mla_fwd(qp, kp, mv, dm):
    B, S, QW = qp.shape
    VW = mv.shape[2]
    tq = dm["tq"]
    nq = S // tq
    scale = (MLA_NOPE + MLA_ROPE) ** -0.5
    assert CHUNK == 64

    def body(q_ref, k_ref, v_ref, y_ref, lse_ref):
        qi = pl.program_id(2)
        q2 = q_ref[0]
        lane = lax.broadcasted_iota(jnp.int32, (tq, LANES), 1)
        allowed = _chunk_mask(tq)

        def block(j, carry, masked):
            off = pl.multiple_of(j * tq, tq)
            v2 = v_ref[0, pl.ds(off, tq), :]
            new = []
            for h in range(2):
                acc, m, l = carry[3 * h], carry[3 * h + 1], carry[3 * h + 2]
                kh = k_ref[0, pl.ds(off, tq), h * HEAD_PAD:(h + 1) * HEAD_PAD]
                s = _dot_nt(q2[:, h * HEAD_PAD:(h + 1) * HEAD_PAD], kh) * scale
                if masked:
                    s = jnp.where(allowed, s, -1e30)
                m_new = jnp.maximum(m, jnp.max(s, axis=1, keepdims=True))
                alpha = jnp.exp(m - m_new)
                p = jnp.exp(s - m_new)
                new += [alpha * acc + _dot(p.astype(BF16), v2), m_new,
                        alpha * l + jnp.sum(p, axis=1, keepdims=True)]
            return tuple(new)

        zero = jnp.zeros((tq, LANES), F32)
        m0 = jnp.full((tq, 1), -1e30, F32)
        l0 = jnp.zeros((tq, 1), F32)
        carry = block(qi, (zero, m0, l0, zero, m0, l0), True)
        carry = lax.fori_loop(0, qi, lambda j, c: block(j, c, False), carry)
        y0 = carry[0] / carry[2]
        y1 = carry[3] / carry[5]
        y_ref[0] = jnp.where(lane < MLA_V, y0, y1).astype(BF16)
        lse_ref[0] = jnp.where(lane < MLA_V, carry[1] + jnp.log(carry[2]), carry[4] + jnp.log(carry[5]))

    qspec = pl.BlockSpec((1, tq, 2 * HEAD_PAD), lambda b, hp, i: (b, i, hp))
    kspec = pl.BlockSpec((1, S, 2 * HEAD_PAD), lambda b, hp, i: (b, 0, hp))
    vspec = pl.BlockSpec((1, S, LANES), lambda b, hp, i: (b, 0, hp))
    yspec = pl.BlockSpec((1, tq, LANES), lambda b, hp, i: (b, i, hp))
    return pl.pallas_call(
        body, name="mla_fwd", grid=(B, VW // LANES, nq),
        in_specs=[qspec, kspec, vspec],
        out_specs=[yspec, yspec],
        out_shape=[_sds((B, S, VW), BF16), _sds((B, S, VW), F32)],
        compiler_params=_cparams(("parallel", "parallel", "arbitrary")),
    )(qp, kp, mv)


def _mla_bwd(qp, kp, mv, y, lse, dy, dm):
    B, S, QW = qp.shape
    VW = mv.shape[2]
    tq = dm["tq"]
    nq = S // tq
    scale = (MLA_NOPE + MLA_ROPE) ** -0.5

    def body(q_ref, k_ref, v_ref, y_ref, lse_ref, dy_ref, dq_ref, dk_ref, dv_ref, dk_acc, dv_acc):
        qi = pl.program_id(2)

        @pl.when(qi == 0)
        def _():
            dk_acc[...] = jnp.zeros_like(dk_acc)
            dv_acc[...] = jnp.zeros_like(dv_acc)

        q2 = q_ref[0]
        dy2 = dy_ref[0]
        lse2 = lse_ref[0]
        lane = lax.broadcasted_iota(jnp.int32, (tq, LANES), 1)
        in_h = [lane < MLA_V, lane >= MLA_V]
        prod = dy2.astype(F32) * y_ref[0].astype(F32)
        delta = [jnp.sum(jnp.where(m, prod, 0.0), axis=1, keepdims=True) for m in in_h]
        dyh = [jnp.where(m, dy2, 0).astype(BF16) for m in in_h]
        lseh = [lse2[:, 0:1], lse2[:, MLA_V:MLA_V + 1]]
        allowed = _chunk_mask(tq)

        def block(j, carry, masked):
            off = pl.multiple_of(j * tq, tq)
            v2 = v_ref[0, pl.ds(off, tq), :]
            new = []
            dv_blk = jnp.zeros((tq, LANES), F32)
            for h in range(2):
                sl = slice(h * HEAD_PAD, (h + 1) * HEAD_PAD)
                qhh = q2[:, sl]
                kh = k_ref[0, pl.ds(off, tq), sl]
                s = _dot_nt(qhh, kh) * scale
                if masked:
                    s = jnp.where(allowed, s, -1e30)
                p = jnp.exp(s - lseh[h])
                ds = (p * (_dot_nt(dyh[h], v2) - delta[h]) * scale).astype(BF16)
                dv_blk = dv_blk + _dot_tn(p.astype(BF16), dyh[h])
                dk_acc[pl.ds(off, tq), sl] += _dot_tn(ds, qhh)
                new.append(carry[h] + _dot(ds, kh))
            dv_acc[pl.ds(off, tq), :] += dv_blk
            return tuple(new)

        zero = jnp.zeros((tq, HEAD_PAD), F32)
        carry = lax.fori_loop(0, qi, lambda j, c: block(j, c, False), (zero, zero))
        carry = block(qi, carry, True)
        dq_ref[0] = jnp.concatenate([carry[0], carry[1]], axis=1).astype(BF16)

        @pl.when(qi == nq - 1)
        def _():
            dk_ref[0] = dk_acc[...].astype(BF16)
            dv_ref[0] = dv_acc[...].astype(BF16)

    qspec = pl.BlockSpec((1, tq, 2 * HEAD_PAD), lambda b, hp, i: (b, i, hp))
    kspec = pl.BlockSpec((1, S, 2 * HEAD_PAD), lambda b, hp, i: (b, 0, hp))
    vspec = pl.BlockSpec((1, S, LANES), lambda b, hp, i: (b, 0, hp))
    yspec = pl.BlockSpec((1, tq, LANES), lambda b, hp, i: (b, i, hp))
    return pl.pallas_call(
        body, name="mla_bwd", grid=(B, VW // LANES, nq),
        in_specs=[qspec, kspec, vspec, yspec, yspec, yspec],
        out_specs=[qspec, kspec, vspec],
        out_shape=[_sds((B, S, QW), BF16), _sds((B, S, QW), BF16), _sds((B, S, VW), BF16)],
        scratch_shapes=[pltpu.VMEM((S, 2 * HEAD_PAD), F32), pltpu.VMEM((S, LANES), F32)],
        compiler_params=_cparams(("parallel", "parallel", "arbitrary")),
    )(qp, kp, mv, y, lse, dy)


def _outproj_fwd(sb_y, mla_y, x0, mod, wo_a, wo_b, ln_g, ln_b, dm):
    B, S, D = x0.shape
    tm = dm["tm"]

    def body(ya_ref, yb_ref, x0_ref, mod_ref, wa_ref, wb_ref, g_ref, b_ref, mix_ref, x1_ref, h2_ref):
        mod = mod_ref[0]
        mix = _dot(ya_ref[0], wa_ref[...]) + _dot(yb_ref[0], wb_ref[...])
        mix_ref[0] = mix
        x1, _, _ = _ln_fwd(ALPHA * x0_ref[0] + (1.0 + mod[2:3]) * mix, g_ref[...], b_ref[...])
        x1_ref[0] = x1
        h2_ref[0] = (x1 * (1.0 + mod[4:5]) + mod[3:4]).astype(BF16)

    return pl.pallas_call(
        body, name="outproj_fwd", grid=(B, S // tm),
        in_specs=[_tok(tm, sb_y.shape[2]), _tok(tm, mla_y.shape[2]), _tok(tm, D), _perb(N_MOD, D),
                  _full(wo_a), _full(wo_b), _full(ln_g), _full(ln_b)],
        out_specs=[_tok(tm, D)] * 3,
        out_shape=[_sds((B, S, D), F32), _sds((B, S, D), F32), _sds((B, S, D), BF16)],
        compiler_params=_cparams(("parallel", "parallel")),
    )(sb_y, mla_y, x0, mod, wo_a, wo_b, ln_g, ln_b)


def _stat_specs(B, D):
    specs = [pl.BlockSpec((1, 8, D), lambda b, s: (b, 0, 0)), pl.BlockSpec((8, D), lambda b, s: (0, 0))]
    shapes = [_sds((B, 8, D), F32), _sds((8, D), F32)]
    return specs, shapes


def _stat_init(bst_ref, wst_ref):
    @pl.when(pl.program_id(1) == 0)
    def _():
        bst_ref[...] = jnp.zeros_like(bst_ref)

    @pl.when((pl.program_id(0) == 0) & (pl.program_id(1) == 0))
    def _():
        wst_ref[...] = jnp.zeros_like(wst_ref)


def _mlp_fwd(h2, x1, mod, target, w_up, w_down, ln_g, ln_b, dm):
    B, S, D = x1.shape
    tm = dm["tm"]
    dff = w_up.shape[1]
    ck = min(dff, 1024)

    def body(h2_ref, x1_ref, mod_ref, t_ref, wu_ref, wd_ref, g_ref, b_ref, u_ref, dr_ref, bst_ref, wst_ref):
        _stat_init(bst_ref, wst_ref)
        mod = mod_ref[0]
        h2 = h2_ref[0]
        ff = jnp.zeros((tm, D), F32)
        for c in range(dff // ck):
            u = _dot(h2, wu_ref[:, c * ck:(c + 1) * ck])
            u_ref[0, :, c * ck:(c + 1) * ck] = u.astype(BF16)
            act = jnp.square(jnp.maximum(u, 0.0)).astype(BF16)
            ff = ff + _dot(act, wd_ref[c * ck:(c + 1) * ck, :])
        g = g_ref[...]
        x2, xhat, rstd = _ln_fwd(ALPHA * x1_ref[0] + (1.0 + mod[5:6]) * ff, g, b_ref[...])
        err = x2 - t_ref[0]
        dy = err * (1.0 / D)
        dr = _ln_bwd(dy, xhat, rstd, g)
        dr_ref[0] = dr
        bst_ref[0, 0:1, :] += _colsum(dr * ff)
        wst_ref[0:1, :] += _colsum(dy * xhat)
        wst_ref[1:2, :] += _colsum(dy)
        wst_ref[2:3, :] += _colsum(err * err) * (0.5 / D)

    sspecs, sshapes = _stat_specs(B, D)
    return pl.pallas_call(
        body, name="mlp_fwd", grid=(B, S // tm),
        in_specs=[_tok(tm, D), _tok(tm, D), _perb(N_MOD, D), _tok(tm, D), _full(w_up), _full(w_down),
                  _full(ln_g), _full(ln_b)],
        out_specs=[_tok(tm, dff), _tok(tm, D)] + sspecs,
        out_shape=[_sds((B, S, dff), BF16), _sds((B, S, D), F32)] + sshapes,
        compiler_params=_cparams(("arbitrary", "arbitrary")),
    )(h2, x1, mod, target, w_up, w_down, ln_g, ln_b)


def _mlp_bwd(dr2, u, x1, x0, mix, mod, w_up, w_down, wo_a, wo_b, ln_g, dm):
    B, S, D = x1.shape
    tm = dm["tm_small"]
    dff = w_up.shape[1]
    ck = min(dff, 1024)

    def body(dr_ref, u_ref, x1_ref, x0_ref, mix_ref, mod_ref, wu_ref, wd_ref, wa_ref, wb_ref, g_ref,
             du_ref, dff_ref, dmix_ref, dx0_ref, dya_ref, dyb_ref, bst_ref, wst_ref):
        _stat_init(bst_ref, wst_ref)
        mod = mod_ref[0]
        dr2 = dr_ref[0]
        dffv = ((1.0 + mod[5:6]) * dr2).astype(BF16)
        dff_ref[0] = dffv
        dh2 = jnp.zeros((tm, D), F32)
        for c in range(dff // ck):
            sl = slice(c * ck, (c + 1) * ck)
            da = _dot_nt(dffv, wd_ref[sl, :])
            du = (da * (2.0 * jnp.maximum(u_ref[0, :, sl].astype(F32), 0.0))).astype(BF16)
            du_ref[0, :, sl] = du
            dh2 = dh2 + _dot_nt(du, wu_ref[:, sl])
        x1 = x1_ref[0]
        dx1 = ALPHA * dr2 + dh2 * (1.0 + mod[4:5])
        bst_ref[0, 0:1, :] += _colsum(dh2 * x1)
        bst_ref[0, 1:2, :] += _colsum(dh2)
        mix = mix_ref[0]
        g = g_ref[...]
        _, xhat, rstd = _ln_fwd(ALPHA * x0_ref[0] + (1.0 + mod[2:3]) * mix, g, 0.0)
        dr1 = _ln_bwd(dx1, xhat, rstd, g)
        wst_ref[0:1, :] += _colsum(dx1 * xhat)
        wst_ref[1:2, :] += _colsum(dx1)
        bst_ref[0, 2:3, :] += _colsum(dr1 * mix)
        dx0_ref[0] = ALPHA * dr1
        dmix = ((1.0 + mod[2:3]) * dr1).astype(BF16)
        dmix_ref[0] = dmix
        dya_ref[0] = _dot_nt(dmix, wa_ref[...]).astype(BF16)
        dyb_ref[0] = _dot_nt(dmix, wb_ref[...]).astype(BF16)

    sspecs, sshapes = _stat_specs(B, D)
    wa, wb = wo_a.shape[0], wo_b.shape[0]
    return pl.pallas_call(
        body, name="mlp_bwd", grid=(B, S // tm),
        in_specs=[_tok(tm, D), _tok(tm, dff), _tok(tm, D), _tok(tm, D), _tok(tm, D), _perb(N_MOD, D),
                  _full(w_up), _full(w_down), _full(wo_a), _full(wo_b), _full(ln_g)],
        out_specs=[_tok(tm, dff), _tok(tm, D), _tok(tm, D), _tok(tm, D), _tok(tm, wa), _tok(tm, wb)] + sspecs,
        out_shape=[_sds((B, S, dff), BF16), _sds((B, S, D), BF16), _sds((B, S, D), BF16), _sds((B, S, D), F32),
                   _sds((B, S, wa), BF16), _sds((B, S, wb), BF16)] + sshapes,
        compiler_params=_cparams(("arbitrary", "arbitrary")),
    )(dr2, u, x1, x0, mix, mod, w_up, w_down, wo_a, wo_b, ln_g)


def _inproj_bwd(x, x0, dx0a, mod, ln_g, dq, dk, dv, dqp, dkp, dmv, cq, ckv, w_in_p, w_uq_p, w_kv, gq, gkv,
                tc, ts1, ts2, dm):
    B, S, D = x.shape
    tm = dm["tm"]
    sbw, qr, kvr, nh = dm["sbw"], dm["qr"], dm["kvr"], dm["nh"]
    qpw = nh * HEAD_PAD
    dinp = w_in_p.shape[1]
    kvw = w_kv.shape[1]

    def body(x_ref, x0_ref, dx0a_ref, mod_ref, g_ref, dq_ref, dk_ref, dv_ref, dqp_ref, dkp_ref, dmv_ref,
             cq_ref, ckv_ref, win_ref, wuq_ref, wkv_ref, gq_ref, gkv_ref, tc_ref, ts1_ref, ts2_ref,
             gx_ref, dproj_ref, dqpre_ref, dkvo_ref, bst_ref, wst_ref):
        _stat_init(bst_ref, wst_ref)
        mod = mod_ref[0]
        c1, s1, s2 = tc_ref[...], ts1_ref[...], ts2_ref[...]
        c8, s18, s28 = jnp.tile(c1, (1, nh)), jnp.tile(s1, (1, nh)), jnp.tile(s2, (1, nh))
        dqpre = _rope_t(dqp_ref[0].astype(F32), c8, s18, s28).astype(BF16)
        dqpre_ref[0] = dqpre
        gq = gq_ref[...]
        cq = cq_ref[0]
        rq = lax.rsqrt(jnp.mean(cq * cq, axis=-1, keepdims=True) + RMS_EPS)
        dqn = _dot_nt(dqpre, wuq_ref[...])
        wst_ref[4:5, 0:qr] += _colsum(dqn * cq * rq)
        dqg = dqn * gq
        dcq = rq * dqg - cq * (rq * rq * rq) * jnp.mean(dqg * cq, axis=-1, keepdims=True)

        dkpre = _rope_t(dkp_ref[0].astype(F32), c8, s18, s28)
        dkr = dkpre[:, 0:HEAD_PAD]
        for h in range(1, nh):
            dkr = dkr + dkpre[:, h * HEAD_PAD:(h + 1) * HEAD_PAD]
        lane = lax.broadcasted_iota(jnp.int32, (tm, LANES), 1)
        dkr = jnp.where((lane >= MLA_NOPE) & (lane < MLA_NOPE + MLA_ROPE), dkr, 0.0)
        dkr = pltpu.roll(dkr, LANES - MLA_NOPE, 1)
        dkvo = jnp.concatenate([dkpre.astype(BF16), dmv_ref[0]], axis=1)
        dkvo_ref[0] = dkvo
        gkv = gkv_ref[...]
        ckv = ckv_ref[0]
        rkv = lax.rsqrt(jnp.mean(ckv * ckv, axis=-1, keepdims=True) + RMS_EPS)
        dkvn = _dot_nt(dkvo, wkv_ref[...])
        wst_ref[5:6, 0:kvr] += _colsum(dkvn * ckv * rkv)
        dkg = dkvn * gkv
        dckv = rkv * dkg - ckv * (rkv * rkv * rkv) * jnp.mean(dkg * ckv, axis=-1, keepdims=True)

        dproj = jnp.concatenate([dq_ref[0], dk_ref[0], dv_ref[0], dcq.astype(BF16), dckv.astype(BF16),
                                 dkr.astype(BF16)], axis=1)
        dproj_ref[0] = dproj
        dh = _dot_nt(dproj, win_ref[...])
        x0 = x0_ref[0]
        dx0 = dx0a_ref[0] + dh * (1.0 + mod[1:2])
        bst_ref[0, 0:1, :] += _colsum(dh * x0)
        bst_ref[0, 1:2, :] += _colsum(dh)
        g = g_ref[...]
        _, xhat, rstd = _ln_fwd(x_ref[0], g, 0.0)
        gx_ref[0] = _ln_bwd(dx0, xhat, rstd, g)
        wst_ref[0:1, :] += _colsum(dx0 * xhat)
        wst_ref[1:2, :] += _colsum(dx0)

    tab = pl.BlockSpec((tm, LANES), lambda b, s: (s, 0))
    sspecs, sshapes = _stat_specs(B, D)
    return pl.pallas_call(
        body, name="inproj_bwd", grid=(B, S // tm),
        in_specs=[_tok(tm, D), _tok(tm, D), _tok(tm, D), _perb(N_MOD, D), _full(ln_g),
                  _tok(tm, sbw), _tok(tm, sbw), _tok(tm, sbw), _tok(tm, qpw), _tok(tm, qpw), _tok(tm, nh * MLA_V),
                  _tok(tm, qr), _tok(tm, kvr), _full(w_in_p), _full(w_uq_p), _full(w_kv), _full(gq), _full(gkv),
                  tab, tab, tab],
        out_specs=[_tok(tm, D), _tok(tm, dinp), _tok(tm, qpw), _tok(tm, kvw)] + sspecs,
        out_shape=[_sds((B, S, D), F32), _sds((B, S, dinp), BF16), _sds((B, S, qpw), BF16),
                   _sds((B, S, kvw), BF16)] + sshapes,
        compiler_params=_cparams(("arbitrary", "arbitrary")),
    )(x, x0, dx0a, mod, ln_g, dq, dk, dv, dqp, dkp, dmv, cq, ckv, w_in_p, w_uq_p, w_kv, gq, gkv, tc, ts1, ts2)


def _tile_of(n, cap):
    if n <= cap:
        return n
    best = n
    for t in range(LANES, cap + 1, LANES):
        if n % t == 0:
            best = t
    return best


def _mm_tn(a, g, name, relu_sq=False):
    T, K = a.shape
    N = g.shape[1]
    tt = 512 if T % 512 == 0 else T
    tk = _tile_of(K, 1024)
    tn = _tile_of(N, 1280)

    def body(a_ref, g_ref, o_ref):
        @pl.when(pl.program_id(2) == 0)
        def _():
            o_ref[...] = jnp.zeros_like(o_ref)

        av = a_ref[...]
        if relu_sq:
            av = jnp.square(jnp.maximum(av.astype(F32), 0.0)).astype(BF16)
        o_ref[...] += _dot_tn(av, g_ref[...])

    return pl.pallas_call(
        body, name=name, grid=(K // tk, N // tn, T // tt),
        in_specs=[pl.BlockSpec((tt, tk), lambda i, j, t: (t, i)), pl.BlockSpec((tt, tn), lambda i, j, t: (t, j))],
        out_specs=pl.BlockSpec((tk, tn), lambda i, j, t: (i, j)),
        out_shape=_sds((K, N), F32),
        compiler_params=_cparams(("parallel", "parallel", "arbitrary")),
    )(a, g)


def _pair_sum(a, b):
    _, R, C = a.shape
    tr = 256

    def body(a_ref, b_ref, o_ref):
        o_ref[...] = (a_ref[...].astype(F32) + b_ref[...].astype(F32)).astype(BF16)

    spec = pl.BlockSpec((1, tr, C), lambda k, r: (k, r, 0))
    return pl.pallas_call(
        body, name="grad_pair_sum", grid=(4, R // tr),
        in_specs=[spec, spec], out_specs=spec, out_shape=_sds(a.shape, BF16),
        compiler_params=_cparams(("parallel", "parallel")),
    )(a, b)


def _reduce_adamw(parts, w, m, v):
    _, R, C = parts.shape
    tr = 256

    def body(p_ref, w_ref, m_ref, v_ref, g_ref, d_ref, nm_ref, nv_ref):
        g = p_ref[0].astype(F32)
        for k in range(1, 4):
            g = g + p_ref[k].astype(F32)
        g_ref[...] = g
        d_ref[...], nm_ref[...], nv_ref[...] = _adamw(w_ref[...], g, m_ref[...], v_ref[...])

    spec = pl.BlockSpec((tr, C), lambda r: (r, 0))
    return pl.pallas_call(
        body, name="grad_reduce_adamw", grid=(R // tr,),
        in_specs=[pl.BlockSpec((4, tr, C), lambda r: (0, r, 0)), spec, spec, spec],
        out_specs=[spec] * 4, out_shape=[_sds((R, C), F32)] * 4,
        compiler_params=_cparams(("parallel",)),
    )(parts, w, m, v)


def _finish(sm, dmod_all, dmod_my, cact_all, p_small, m_small, v_small, b_ada, m_b, v_b, w_ada, m_w, v_w):
    n0 = p_small.shape[1]
    n1 = sm.shape[1]
    d = cact_all.shape[1]

    def body(sm_ref, dma_ref, dmm_ref, ca_ref, p_ref, pm_ref, pv_ref, b_ref, bm_ref, bv_ref, w_ref, wm_ref, wv_ref,
             gs_ref, ds_ref, ms_ref, vs_ref, gb_ref, db_ref, mb_ref, vb_ref, gw_ref, dw_ref, mw_ref, vw_ref,
             loss_ref):
        gs = sm_ref[0:1, :]
        for k in range(1, N_DEV):
            gs = gs + sm_ref[k:k + 1, :]
        gs_ref[...] = gs
        ds_ref[...], ms_ref[...], vs_ref[...] = _adamw(p_ref[...], gs[:, 0:n0], pm_ref[...], pv_ref[...])
        loss_ref[...] = jnp.zeros((1, LANES), F32) + jnp.sum(gs[:, n1 - d:n1])
        gb = jnp.sum(dma_ref[...], axis=0, keepdims=True)
        gb_ref[...] = gb
        db_ref[...], mb_ref[...], vb_ref[...] = _adamw(b_ref[...], gb, bm_ref[...], bv_ref[...])
        gw = _dot_tn(ca_ref[...].astype(BF16), dmm_ref[...].astype(BF16))
        gw_ref[...] = gw
        dw_ref[...], mw_ref[...], vw_ref[...] = _adamw(w_ref[...], gw, wm_ref[...], wv_ref[...])

    s0 = _sds(p_small.shape, F32)
    sb = _sds(b_ada.shape, F32)
    sw = _sds(w_ada.shape, F32)
    return pl.pallas_call(
        body, name="finish_small",
        out_shape=[_sds((1, n1), F32), s0, s0, s0, sb, sb, sb, sb, sw, sw, sw, sw,
                   _sds((1, LANES), F32)],
        compiler_params=pltpu.CompilerParams(vmem_limit_bytes=VMEM_LIMIT),
    )(sm, dmod_all, dmod_my, cact_all, p_small, m_small, v_small, b_ada, m_b, v_b, w_ada, m_w, v_w)


def _pack(arrs, dtype, width):
    flat = jnp.concatenate([a.astype(dtype).reshape(-1) for a in arrs])
    rows = -(-flat.shape[0] // (256 * width)) * 256
    return jnp.pad(flat, (0, rows * width - flat.shape[0])).reshape(rows, width)


def _unpack(slab, shapes):
    flat = slab.reshape(-1)
    out, o = [], 0
    for s in shapes:
        n = math.prod(s)
        out.append(flat[o:o + n].reshape(s))
        o += n
    return out


def _rope_tables(S):
    inv_freq = 1.0 / (ROPE_BASE ** (jnp.arange(0, MLA_ROPE, 2, dtype=F32) / MLA_ROPE))
    ang = jnp.arange(S, dtype=F32)[:, None] * inv_freq[None, :]
    cos, sin = jnp.cos(ang), jnp.sin(ang)
    one = jnp.ones((S, MLA_NOPE), F32)
    z16 = jnp.zeros((S, 16), F32)
    z32 = jnp.zeros((S, 32), F32)
    z64 = jnp.zeros((S, MLA_NOPE), F32)
    tc = jnp.concatenate([one, cos, cos, jnp.ones((S, 32), F32)], axis=1)
    ts1 = jnp.concatenate([z64, -sin, z16, z32], axis=1)
    ts2 = jnp.concatenate([z64, z16, sin, z32], axis=1)
    return tc, ts1, ts2


def kernel(x, c, ln_in_g, ln_in_b, w_ada, b_ada, w_in, q_norm_g, kv_norm_g, w_uq, w_ukv, w_o, ln1_g, ln1_b, w_up, w_down, ln2_g, ln2_b, loss_target, m_ln_in_g, m_ln_in_b, m_w_ada, m_b_ada, m_w_in, m_q_norm_g, m_kv_norm_g, m_w_uq, m_w_ukv, m_w_o, m_ln1_g, m_ln1_b, m_w_up, m_w_down, m_ln2_g, m_ln2_b, v_ln_in_g, v_ln_in_b, v_w_ada, v_b_ada, v_w_in, v_q_norm_g, v_kv_norm_g, v_w_uq, v_w_ukv, v_w_o, v_ln1_g, v_ln1_b, v_w_up, v_w_down, v_ln2_g, v_ln2_b):
    B, S, D = x.shape
    sbw = D // 2
    mlw = D - sbw
    nh = mlw // MLA_V
    qr = w_uq.shape[1]
    kvr = w_ukv.shape[1]
    qk = MLA_NOPE + MLA_ROPE
    dff = w_up.shape[2] * N_DEV
    din = w_in.shape[2] * N_DEV
    tm = 512 if S % 512 == 0 else S
    tq = 256
    dm = dict(tm=tm, tm_small=min(tm, 256), tq=tq, sbw=sbw, qr=qr, kvr=kvr, nh=nh)
    width = 1024 if D >= 1024 else LANES
    dev = 4 * lax.axis_index("x") + 2 * lax.axis_index("y") + lax.axis_index("c")

    nada = w_ada.shape[2]
    c_all = _all_gather(c, "gather_c").reshape(N_DEV * B, D)
    b_loc = lax.dynamic_slice(b_ada, (0, dev * nada), (1, nada))
    cact_all, mod_part = _ada_partial(c_all, w_ada[0], b_loc)
    mod_all = _all_gather(mod_part, "gather_mod")
    mod = lax.dynamic_slice(mod_all, (0, dev * B, 0), (N_DEV, B, nada))
    mod = jnp.swapaxes(mod, 0, 1).reshape(B, N_MOD, D)

    big = [w_in[0], w_uq[0], w_ukv[0], w_o[0], w_up[0], w_down[0]]
    big_shapes = [a.shape for a in big]
    slabs = _all_gather(_pack(big, BF16, width), "gather_w")
    parts = [_unpack(slabs[d], big_shapes) for d in range(N_DEV)]
    cat = lambda i, ax: jnp.concatenate([p[i] for p in parts], axis=ax)
    w_in_f, w_uq_f, w_ukv_f, w_o_f, w_up_f, w_down_f = cat(0, 1), cat(1, 1), cat(2, 1), cat(3, 0), cat(4, 1), cat(5, 0)
    w_in_p = jnp.pad(w_in_f, ((0, 0), (0, LANES - MLA_ROPE)))
    zpad = jnp.zeros((qr, nh, HEAD_PAD - qk), BF16)
    w_uq_p = jnp.concatenate([w_uq_f.reshape(qr, nh, qk), zpad], axis=2).reshape(qr, nh * HEAD_PAD)
    w_uk = w_ukv_f[:, :nh * MLA_NOPE].reshape(kvr, nh, MLA_NOPE)
    w_uk_p = jnp.concatenate([w_uk, jnp.zeros((kvr, nh, HEAD_PAD - MLA_NOPE), BF16)], axis=2)
    w_kv = jnp.concatenate([w_uk_p.reshape(kvr, nh * HEAD_PAD), w_ukv_f[:, nh * MLA_NOPE:]], axis=1)
    wo_a, wo_b = w_o_f[:sbw], w_o_f[sbw:]

    tc, ts1, ts2 = _rope_tables(S)
    g_in, b_in = ln_in_g.reshape(1, D), ln_in_b.reshape(1, D)
    (x0, h, sq, sk, sv, qp, kp, mv, cq, ckv, qn, kvn) = _inproj_fwd(
        x, mod, g_in, b_in, w_in_p, w_uq_p, w_kv, q_norm_g, kv_norm_g, tc, ts1, ts2, dm)
    sb_y, sb_tot = _sb_fwd(sq, sk, sv, dm)
    mla_y, mla_lse = _mla_fwd(qp, kp, mv, dm)
    mix, x1, h2 = _outproj_fwd(sb_y, mla_y, x0, mod, wo_a, wo_b, ln1_g, ln1_b, dm)
    u, dr2, bst_c, wst_c = _mlp_fwd(h2, x1, mod, loss_target, w_up_f, w_down_f, ln2_g, ln2_b, dm)

    du, dffb, dmixb, dx0a, dsb_y, dmla_y, bst_b, wst_b = _mlp_bwd(
        dr2, u, x1, x0, mix, mod, w_up_f, w_down_f, wo_a, wo_b, ln1_g, dm)
    dsq, dsk, dsv = _sb_bwd(sq, sk, sv, sb_tot, dsb_y, dm)
    dqp, dkp, dmv = _mla_bwd(qp, kp, mv, mla_y, mla_lse, dmla_y, dm)
    grad_x, dproj, dqpre, dkvo, bst_a, wst_a = _inproj_bwd(
        x, x0, dx0a, mod, g_in, dsq, dsk, dsv, dqp, dkp, dmv, cq, ckv, w_in_p, w_uq_p, w_kv, q_norm_g, kv_norm_g,
        tc, ts1, ts2, dm)

    T = B * S
    r2 = lambda a: a.reshape(T, a.shape[2])
    g_in_p = _mm_tn(r2(h), r2(dproj), "grad_w_in")
    g_uq_p = _mm_tn(r2(qn), r2(dqpre), "grad_w_uq")
    g_kv = _mm_tn(r2(kvn), r2(dkvo), "grad_w_kv")
    g_o = jnp.concatenate([_mm_tn(r2(sb_y), r2(dmixb), "grad_w_o_sb"), _mm_tn(r2(mla_y), r2(dmixb), "grad_w_o_mla")], axis=0)
    g_up = _mm_tn(r2(h2), r2(du), "grad_w_up")
    g_down = _mm_tn(r2(u), r2(dffb), "grad_w_down", relu_sq=True)
    g_in_f = g_in_p[:, :din]
    g_uq_f = g_uq_p.reshape(qr, nh, HEAD_PAD)[:, :, :qk].reshape(qr, nh * qk)
    g_uk = g_kv[:, :nh * HEAD_PAD].reshape(kvr, nh, HEAD_PAD)[:, :, :MLA_NOPE].reshape(kvr, nh * MLA_NOPE)
    g_ukv_f = jnp.concatenate([g_uk, g_kv[:, nh * HEAD_PAD:]], axis=1)

    def shard(a, d, ax):
        n = a.shape[ax] // N_DEV
        return lax.slice_in_dim(a, d * n, (d + 1) * n, axis=ax)

    dest = [_pack([shard(g_in_f, d, 1), shard(g_uq_f, d, 1), shard(g_ukv_f, d, 1), shard(g_o, d, 0),
                   shard(g_up, d, 1), shard(g_down, d, 0)], BF16, width) for d in range(N_DEV)]
    R = dest[0].shape[0]
    by_core = jnp.stack(dest).reshape(4, 2, R, width).swapaxes(0, 1)
    mine, got = _core_scatter(by_core, "scatter_g_cores")
    chip_sum = _pair_sum(mine, got)
    quarter = _chip_exchange(chip_sum, "scatter_g_chips", scatter=True)
    mom = lambda arrs: _pack(arrs, F32, width)
    g_slab, d_slab, nm_slab, nv_slab = _reduce_adamw(
        quarter, mom(big), mom([m_w_in[0], m_w_uq[0], m_w_ukv[0], m_w_o[0], m_w_up[0], m_w_down[0]]),
        mom([v_w_in[0], v_w_uq[0], v_w_ukv[0], v_w_o[0], v_w_up[0], v_w_down[0]]))
    lead = lambda arrs: [a[None] for a in arrs]
    gb, db, nmb, nvb = (lead(_unpack(s, big_shapes)) for s in (g_slab, d_slab, nm_slab, nv_slab))

    dmod = jnp.concatenate([bst_a[:, 1], bst_a[:, 0], bst_b[:, 2], bst_b[:, 1], bst_b[:, 0], bst_c[:, 0]], axis=1)
    small = jnp.concatenate([wst_a[0], wst_a[1], wst_a[4, :qr], wst_a[5, :kvr], wst_b[0], wst_b[1],
                             wst_c[0], wst_c[1], wst_c[2]])
    n1 = small.shape[0]
    both = _all_gather(_pack([dmod, small], F32, LANES), "gather_small").reshape(N_DEV, -1)
    dmod_all = both[:, :B * N_MOD * D].reshape(N_DEV * B, N_MOD * D)
    sm = both[:, B * N_MOD * D:B * N_MOD * D + n1]
    dmod_my = lax.dynamic_slice(dmod_all, (0, dev * nada), (N_DEV * B, nada))
    row = lambda arrs: jnp.concatenate([a.reshape(1, -1) for a in arrs], axis=1)
    smalls = [ln_in_g, ln_in_b, q_norm_g, kv_norm_g, ln1_g, ln1_b, ln2_g, ln2_b]
    small_shapes = [a.shape for a in smalls]
    (gs, ds, nms, nvs, g_b, d_b, nm_b, nv_b, g_w, d_w, nm_w, nv_w, loss_v) = _finish(
        sm, dmod_all, dmod_my, cact_all, row(smalls),
        row([m_ln_in_g, m_ln_in_b, m_q_norm_g, m_kv_norm_g, m_ln1_g, m_ln1_b, m_ln2_g, m_ln2_b]),
        row([v_ln_in_g, v_ln_in_b, v_q_norm_g, v_kv_norm_g, v_ln1_g, v_ln1_b, v_ln2_g, v_ln2_b]),
        b_ada, m_b_ada, v_b_ada, w_ada[0], m_w_ada[0], v_w_ada[0])
    gsm, dsm, nmsm, nvsm = (_unpack(s, small_shapes) for s in (gs, ds, nms, nvs))

    def ordered(sm_l, w_l, ada_w, ada_b):
        return [sm_l[0], sm_l[1], ada_w[None], ada_b, w_l[0], sm_l[2], sm_l[3], w_l[1], w_l[2], w_l[3],
                sm_l[4], sm_l[5], w_l[4], w_l[5], sm_l[6], sm_l[7]]

    loss = loss_v[0, 0]
    return (loss, grad_x, *ordered(gsm, gb, g_w, g_b), *ordered(dsm, db, d_w, d_b),
            *ordered(nmsm, nmb, nm_w, nm_b), *ordered(nvsm, nvb, nv_w, nv_b))
```

```python
import functools
import math

import jax
import jax.numpy as jnp
from jax import lax
from jax.experimental import pallas as pl
from jax.experimental.pallas import tpu as pltpu

F32 = jnp.float32
BF16 = jnp.bfloat16

SB_HD = 64
MLA_V = 64
MLA_NOPE = 64
MLA_ROPE = 32
HEAD_PAD = 128
CHUNK = 64
ROPE_BASE = 10000.0
LN_EPS = 1e-5
RMS_EPS = 1e-6
DEPTH = 1
ALPHA = (2.0 * DEPTH) ** 0.25
N_MOD = 6
ADAM_LR = 0.001
ADAM_B1 = 0.9
ADAM_B2 = 0.999
ADAM_EPS = 1e-08
ADAM_WD = 0.01
ADAM_STEP = 10
N_DEV = 8
LANES = 128
CUMSUM_W = 256
VMEM_LIMIT = 56 * 1024 * 1024
MESH = pl.DeviceIdType.MESH


def _dot(a, b):
    return jnp.dot(a, b, preferred_element_type=F32)


def _dot_nt(a, b):
    return lax.dot_general(a, b, (((1,), (1,)), ((), ())), preferred_element_type=F32)


def _dot_tn(a, b):
    return lax.dot_general(a, b, (((0,), (0,)), ((), ())), preferred_element_type=F32)


def _cparams(sem):
    return pltpu.CompilerParams(dimension_semantics=sem, vmem_limit_bytes=VMEM_LIMIT)


def _full(a):
    nd = a.ndim
    return pl.BlockSpec(a.shape, lambda *_: (0,) * nd, pipeline_mode=pl.Buffered(1))


def _tok(tm, w):
    return pl.BlockSpec((1, tm, w), lambda b, s: (b, s, 0))


def _perb(rows, w):
    return pl.BlockSpec((1, rows, w), lambda b, s: (b, 0, 0))


def _sds(shape, dtype):
    return jax.ShapeDtypeStruct(shape, dtype)


def _ln_fwd(x, g, b):
    mu = jnp.mean(x, axis=-1, keepdims=True)
    xc = x - mu
    var = jnp.mean(xc * xc, axis=-1, keepdims=True)
    rstd = lax.rsqrt(var + LN_EPS)
    xhat = xc * rstd
    return xhat * g + b, xhat, rstd


def _ln_bwd(dy, xhat, rstd, g):
    dxh = dy * g
    m1 = jnp.mean(dxh, axis=-1, keepdims=True)
    m2 = jnp.mean(dxh * xhat, axis=-1, keepdims=True)
    return rstd * (dxh - m1 - xhat * m2)


def _colsum(a):
    return jnp.sum(a, axis=0, keepdims=True)


def _rope(x, c, s1, s2):
    w = x.shape[-1]
    return x * c + pltpu.roll(x, w - 16, 1) * s1 + pltpu.roll(x, 16, 1) * s2


def _rope_t(x, c, s1, s2):
    w = x.shape[-1]
    return x * c - pltpu.roll(x, w - 16, 1) * s1 - pltpu.roll(x, 16, 1) * s2


def _adamw(w, g, m, v):
    m = ADAM_B1 * m + (1.0 - ADAM_B1) * g
    v = ADAM_B2 * v + (1.0 - ADAM_B2) * (g * g)
    m_hat = m / (1.0 - ADAM_B1 ** ADAM_STEP)
    v_hat = v / (1.0 - ADAM_B2 ** ADAM_STEP)
    delta = -ADAM_LR * (m_hat / (jnp.sqrt(v_hat) + ADAM_EPS) + ADAM_WD * w)
    return delta, m, v


def _my_place():
    return lax.axis_index("x"), lax.axis_index("y"), lax.axis_index("c")


def _chip_peers(mx, my):
    out = []
    for j in (1, 2, 3):
        px = 1 - mx if (j >> 1) else mx
        py = 1 - my if (j & 1) else my
        out.append((px, py, 2 * px + py))
    return out


def _hbm_call(body, name, n_in, out_shape, sems):
    hbm = pl.BlockSpec(memory_space=pl.ANY)
    return pl.pallas_call(
        body, name=name, out_shape=out_shape,
        in_specs=[hbm] * n_in, out_specs=[hbm] * len(out_shape),
        scratch_shapes=[pltpu.SemaphoreType.DMA(s) for s in sems])


def _chip_exchange(xs, name, scatter):
    n = len(xs)

    def body(*refs):
        x_refs, o_refs = refs[:n], refs[n:2 * n]
        ssem, rsem, lsem = refs[2 * n:]
        mx, my, mc = _my_place()
        me = 2 * mx + my
        peers = _chip_peers(mx, my)

        def copy(i, j, src_slot, dst_slot):
            px, py, _ = peers[j]
            return pltpu.make_async_remote_copy(
                src_ref=x_refs[i].at[src_slot] if scatter else x_refs[i], dst_ref=o_refs[i].at[dst_slot],
                send_sem=ssem.at[i, j], recv_sem=rsem.at[i, j], device_id=(px, py, mc), device_id_type=MESH)

        local = [pltpu.make_async_copy(x_refs[i].at[me] if scatter else x_refs[i], o_refs[i].at[me], lsem.at[i])
                 for i in range(n)]
        sends = [copy(i, j, peers[j][2], me) for i in range(n) for j in range(3)]
        for cp in local + sends:
            cp.start()
        for i in range(n):
            for j in range(3):
                copy(i, j, peers[j][2], peers[j][2]).wait_recv()
        for cp in sends:
            cp.wait_send()
        for cp in local:
            cp.wait()

    out_shape = [_sds((4,) + tuple(x.shape[1:] if scatter else x.shape), x.dtype) for x in xs]
    return _hbm_call(body, name, n, out_shape, [(n, 3), (n, 3), (n,)])(*xs)


def _core_gather(xs, name):
    n = len(xs)

    def body(*refs):
        x_refs, o_refs = refs[:n], refs[n:2 * n]
        ssem, rsem, lsem = refs[2 * n:]
        mx, my, mc = _my_place()

        def copy(i, k, core):
            return pltpu.make_async_remote_copy(
                src_ref=x_refs[i].at[k], dst_ref=o_refs[i].at[k, core], send_sem=ssem.at[i, k],
                recv_sem=rsem.at[i, k], device_id=(mx, my, 1 - mc), device_id_type=MESH)

        pairs = [(i, k) for i in range(n) for k in range(4)]
        local = [pltpu.make_async_copy(x_refs[i].at[k], o_refs[i].at[k, mc], lsem.at[i, k]) for i, k in pairs]
        sends = [copy(i, k, mc) for i, k in pairs]
        for cp in local + sends:
            cp.start()
        for i, k in pairs:
            copy(i, k, 1 - mc).wait_recv()
        for cp in sends:
            cp.wait_send()
        for cp in local:
            cp.wait()

    out_shape = [_sds((4, 2) + tuple(x.shape[1:]), x.dtype) for x in xs]
    return _hbm_call(body, name, n, out_shape, [(n, 4), (n, 4), (n, 4)])(*xs)


def _core_scatter(gs, name):
    n = len(gs)

    def body(*refs):
        g_refs, mine_refs, got_refs = refs[:n], refs[n:2 * n], refs[2 * n:3 * n]
        ssem, rsem, lsem = refs[3 * n:]
        mx, my, mc = _my_place()
        pairs = [(i, k) for i in range(n) for k in range(4)]
        local = [pltpu.make_async_copy(g_refs[i].at[k, mc], mine_refs[i].at[k], lsem.at[i, k]) for i, k in pairs]
        sends = [pltpu.make_async_remote_copy(
            src_ref=g_refs[i].at[k, 1 - mc], dst_ref=got_refs[i].at[k], send_sem=ssem.at[i, k],
            recv_sem=rsem.at[i, k], device_id=(mx, my, 1 - mc), device_id_type=MESH) for i, k in pairs]
        for cp in local + sends:
            cp.start()
        for cp in sends:
            cp.wait_recv()
        for cp in sends:
            cp.wait_send()
        for cp in local:
            cp.wait()

    blk = [_sds((4,) + tuple(g.shape[2:]), g.dtype) for g in gs]
    res = _hbm_call(body, name, n, blk + blk, [(n, 4), (n, 4), (n, 4)])(*gs)
    return res[:n], res[n:]


def _all_gather(xs, name):
    by_chip = _chip_exchange(xs, name + "_chips", scatter=False)
    both = _core_gather(by_chip, name + "_cores")
    return [b.reshape((N_DEV,) + tuple(x.shape)) for b, x in zip(both, xs)]


def _ada_partial(c_all, w_ada_loc, b_loc):
    def body(c_ref, w_ref, b_ref, act_ref, mod_ref):
        c = c_ref[...]
        act = c * (1.0 / (1.0 + jnp.exp(-c)))
        act_ref[...] = act
        mod_ref[...] = _dot(act.astype(BF16), w_ref[...].astype(BF16)) + b_ref[...]

    nb, d = c_all.shape
    return pl.pallas_call(
        body, name="ada_partial",
        out_shape=(_sds((nb, d), F32), _sds((nb, w_ada_loc.shape[1]), F32)),
        compiler_params=pltpu.CompilerParams(vmem_limit_bytes=VMEM_LIMIT),
    )(c_all, w_ada_loc, b_loc)


def _inproj_fwd(x, mod, ln_g, ln_b, w_in_p, w_uq_p, w_kv, gq, gkv, tc, ts1, ts2, dm):
    B, S, D = x.shape
    tm = dm["tm"]
    sbw, qr, kvr, nh = dm["sbw"], dm["qr"], dm["kvr"], dm["nh"]
    o_cq, o_ckv, o_kr = 3 * sbw, 3 * sbw + qr, 3 * sbw + qr + kvr
    qpw = nh * HEAD_PAD

    def body(x_ref, mod_ref, g_ref, b_ref, win_ref, wuq_ref, wkv_ref, gq_ref, gkv_ref, tc_ref, ts1_ref, ts2_ref,
             x0_ref, h_ref, q_ref, k_ref, v_ref, qp_ref, kp_ref, mv_ref, cq_ref, ckv_ref, qn_ref, kvn_ref):
        x0, _, _ = _ln_fwd(x_ref[0], g_ref[...], b_ref[...])
        x0_ref[0] = x0
        mod = mod_ref[0]
        h = (x0 * (1.0 + mod[1:2]) + mod[0:1]).astype(BF16)
        h_ref[0] = h
        proj = _dot(h, win_ref[...])
        q_ref[0] = (proj[:, 0:sbw] * (SB_HD ** -0.5)).astype(BF16)
        k_ref[0] = proj[:, sbw:2 * sbw].astype(BF16)
        v_ref[0] = proj[:, 2 * sbw:3 * sbw].astype(BF16)
        cq = proj[:, o_cq:o_cq + qr]
        ckv = proj[:, o_ckv:o_ckv + kvr]
        cq_ref[0] = cq
        ckv_ref[0] = ckv
        qn = (cq * lax.rsqrt(jnp.mean(cq * cq, axis=-1, keepdims=True) + RMS_EPS) * gq_ref[...]).astype(BF16)
        kvn = (ckv * lax.rsqrt(jnp.mean(ckv * ckv, axis=-1, keepdims=True) + RMS_EPS) * gkv_ref[...]).astype(BF16)
        qn_ref[0] = qn
        kvn_ref[0] = kvn
        c1, s1, s2 = tc_ref[...], ts1_ref[...], ts2_ref[...]
        c8, s18, s28 = jnp.tile(c1, (1, nh)), jnp.tile(s1, (1, nh)), jnp.tile(s2, (1, nh))
        qp_ref[0] = _rope(_dot(qn, wuq_ref[...]), c8, s18, s28).astype(BF16)
        kvo = _dot(kvn, wkv_ref[...])
        kr = pltpu.roll(proj[:, o_kr:o_kr + LANES], 64, 1)
        kr = _rope(kr, c1, s1, s2)
        kp_ref[0] = (kvo[:, 0:qpw] + jnp.tile(kr, (1, nh))).astype(BF16)
        mv_ref[0] = kvo[:, qpw:].astype(BF16)

    tab = pl.BlockSpec((tm, LANES), lambda b, s: (s, 0))
    outs = [(D, F32), (D, BF16), (sbw, BF16), (sbw, BF16), (sbw, BF16), (qpw, BF16), (qpw, BF16),
            (nh * MLA_V, BF16), (qr, F32), (kvr, F32), (qr, BF16), (kvr, BF16)]
    return pl.pallas_call(
        body, name="inproj_fwd", grid=(B, S // tm),
        in_specs=[_tok(tm, D), _perb(N_MOD, D), _full(ln_g), _full(ln_b), _full(w_in_p), _full(w_uq_p),
                  _full(w_kv), _full(gq), _full(gkv), tab, tab, tab],
        out_specs=[_tok(tm, w) for w, _ in outs],
        out_shape=[_sds((B, S, w), t) for w, t in outs],
        compiler_params=_cparams(("parallel", "parallel")),
    )(x, mod, ln_g, ln_b, w_in_p, w_uq_p, w_kv, gq, gkv, tc, ts1, ts2)


def _softplus_parts(z):
    e = jnp.exp(-jnp.abs(z))
    a = -(jnp.maximum(z, 0.0) + jnp.log(1.0 + e))
    return a, e


def _split_dot(a, u):
    hi = a.astype(BF16)
    lo = (a - hi.astype(F32)).astype(BF16)
    return _dot(hi, u) + _dot(lo, u)


def _tri(n, rel):
    row = lax.broadcasted_iota(jnp.int32, (n, n), 0)
    col = lax.broadcasted_iota(jnp.int32, (n, n), 1)
    return rel(row, col).astype(BF16)


def _running_sum(a, tri, reverse, split):
    cs = tri.shape[0]
    n = a.shape[1] // cs
    out = [None] * n
    run = None
    for c in (reversed(range(n)) if reverse else range(n)):
        part = a[:, c * cs:(c + 1) * cs]
        loc = _split_dot(part, tri) if split else _dot(part.astype(BF16), tri)
        out[c] = loc if run is None else loc + run
        tot = jnp.sum(part, axis=1, keepdims=True)
        run = tot if run is None else run + tot
    return (out[0] if n == 1 else jnp.concatenate(out, axis=1)), run


def _sb_fwd(q, k, v, dm):
    B, S, W = q.shape
    tq = dm["tq"]
    nq = S // tq

    def body(q_ref, k_ref, v_ref, y_ref, tot_ref):
        qi = pl.program_id(2)
        q2 = q_ref[0]
        lane = lax.broadcasted_iota(jnp.int32, (tq, LANES), 1)
        qh = [jnp.where(lane < SB_HD, q2, 0).astype(BF16), jnp.where(lane >= SB_HD, q2, 0).astype(BF16)]
        row = lax.broadcasted_iota(jnp.int32, (tq, tq), 0)
        col = lax.broadcasted_iota(jnp.int32, (tq, tq), 1)
        later = _tri(min(tq, CUMSUM_W), lambda j, s: j > s)
        strict = col < row

        def block(j, carry, masked):
            off = pl.multiple_of(j * tq, tq)
            k2 = k_ref[0, pl.ds(off, tq), :]
            v2 = v_ref[0, pl.ds(off, tq), :]
            new = []
            for h in range(2):
                acc, run = carry[2 * h], carry[2 * h + 1]
                z = _dot_nt(qh[h], k2)
                a, _ = _softplus_parts(z)
                if masked:
                    a = jnp.where(strict, a, 0.0)
                a_later, a_tot = _running_sum(a, later, reverse=True, split=True)
                w = jnp.exp(z + a + a_later + run)
                if masked:
                    w = jnp.where(strict, w, 0.0)
                new.append(acc + _dot(w.astype(BF16), v2))
                new.append(run + a_tot)
            return tuple(new)

        zero = jnp.zeros((tq, LANES), F32)
        zrun = jnp.zeros((tq, 1), F32)
        carry = block(qi, (zero, zrun, zero, zrun), True)
        carry = lax.fori_loop(0, qi, lambda jj, c: block(qi - 1 - jj, c, False), carry)
        y_ref[0] = jnp.where(lane < SB_HD, carry[0], carry[2]).astype(BF16)
        tot_ref[0] = jnp.where(lane < SB_HD, carry[1], carry[3])

    qspec = pl.BlockSpec((1, tq, LANES), lambda b, hp, i: (b, i, hp))
    kspec = pl.BlockSpec((1, S, LANES), lambda b, hp, i: (b, 0, hp))
    return pl.pallas_call(
        body, name="sb_fwd", grid=(B, W // LANES, nq),
        in_specs=[qspec, kspec, kspec],
        out_specs=[qspec, qspec],
        out_shape=[_sds((B, S, W), BF16), _sds((B, S, W), F32)],
        compiler_params=_cparams(("parallel", "parallel", "arbitrary")),
    )(q, k, v)


def _sb_bwd(q, k, v, tot, dy, dm):
    B, S, W = q.shape
    tq = dm["tq"]
    nq = S // tq

    def body(q_ref, k_ref, v_ref, tot_ref, dy_ref, dq_ref, dk_ref, dv_ref, dk_acc, dv_acc):
        qi = pl.program_id(2)

        @pl.when(qi == 0)
        def _():
            dk_acc[...] = jnp.zeros_like(dk_acc)
            dv_acc[...] = jnp.zeros_like(dv_acc)

        q2 = q_ref[0]
        dy2 = dy_ref[0]
        tot2 = tot_ref[0]
        lane = lax.broadcasted_iota(jnp.int32, (tq, LANES), 1)
        in_h = [lane < SB_HD, lane >= SB_HD]
        qh = [jnp.where(m, q2, 0).astype(BF16) for m in in_h]
        dyh = [jnp.where(m, dy2, 0).astype(BF16) for m in in_h]
        toth = [tot2[:, 0:1], tot2[:, SB_HD:SB_HD + 1]]
        row = lax.broadcasted_iota(jnp.int32, (tq, tq), 0)
        col = lax.broadcasted_iota(jnp.int32, (tq, tq), 1)
        upto = _tri(min(tq, CUMSUM_W), lambda j, s: j <= s)
        before = _tri(min(tq, CUMSUM_W), lambda s, j: s < j)
        strict = col < row

        def block(j, carry, masked):
            off = pl.multiple_of(j * tq, tq)
            k2 = k_ref[0, pl.ds(off, tq), :]
            v2 = v_ref[0, pl.ds(off, tq), :]
            new = []
            dk_blk = jnp.zeros((tq, LANES), F32)
            dv_blk = jnp.zeros((tq, LANES), F32)
            for h in range(2):
                dq, pa, pg = carry[3 * h], carry[3 * h + 1], carry[3 * h + 2]
                z = _dot_nt(qh[h], k2)
                a, e = _softplus_parts(z)
                if masked:
                    a = jnp.where(strict, a, 0.0)
                a_upto, a_tot = _running_sum(a, upto, reverse=False, split=True)
                w = jnp.exp(z + a + (toth[h] - pa - a_upto))
                if masked:
                    w = jnp.where(strict, w, 0.0)
                g = _dot_nt(dyh[h], v2) * w
                g_before, g_tot = _running_sum(g, before, reverse=False, split=False)
                g_before = g_before + pg
                r = 1.0 / (1.0 + e)
                er = e * r
                pos = z >= 0.0
                sig = jnp.where(pos, r, er)
                dz = g * jnp.where(pos, er, r) - g_before * sig
                if masked:
                    dz = jnp.where(strict, dz, 0.0)
                dzb = dz.astype(BF16)
                dv_blk = dv_blk + _dot_tn(w.astype(BF16), dyh[h])
                dk_blk = dk_blk + _dot_tn(dzb, qh[h])
                new += [dq + _dot(dzb, k2), pa + a_tot, pg + g_tot]
            dk_acc[pl.ds(off, tq), :] += dk_blk
            dv_acc[pl.ds(off, tq), :] += dv_blk
            return tuple(new)

        zero = jnp.zeros((tq, LANES), F32)
        zrun = jnp.zeros((tq, 1), F32)
        carry = lax.fori_loop(0, qi, lambda j, c: block(j, c, False), (zero, zrun, zrun, zero, zrun, zrun))
        carry = block(qi, carry, True)
        dq_ref[0] = (jnp.where(in_h[0], carry[0], carry[3]) * (SB_HD ** -0.5)).astype(BF16)

        @pl.when(qi == nq - 1)
        def _():
            dk_ref[0] = dk_acc[...].astype(BF16)
            dv_ref[0] = dv_acc[...].astype(BF16)

    qspec = pl.BlockSpec((1, tq, LANES), lambda b, hp, i: (b, i, hp))
    kspec = pl.BlockSpec((1, S, LANES), lambda b, hp, i: (b, 0, hp))
    return pl.pallas_call(
        body, name="sb_bwd", grid=(B, W // LANES, nq),
        in_specs=[qspec, kspec, kspec, qspec, qspec],
        out_specs=[qspec, kspec, kspec],
        out_shape=[_sds((B, S, W), BF16)] * 3,
        scratch_shapes=[pltpu.VMEM((S, LANES), F32), pltpu.VMEM((S, LANES), F32)],
        compiler_params=_cparams(("parallel", "parallel", "arbitrary")),
    )(q, k, v, tot, dy)


def _chunk_mask(tq):
    row = lax.broadcasted_iota(jnp.int32, (tq, tq), 0)
    col = lax.broadcasted_iota(jnp.int32, (tq, tq), 1)
    return lax.shift_right_logical(col, 6) <= lax.shift_right_logical(row, 6)


def _mla_fwd(qp, kp, mv, dm):
    B, S, QW = qp.shape
    VW = mv.shape[2]
    tq = dm["tq"]
    nq = S // tq
    scale = (MLA_NOPE + MLA_ROPE) ** -0.5
    assert CHUNK == 64

    def body(q_ref, k_ref, v_ref, y_ref, lse_ref):
        qi = pl.program_id(2)
        q2 = q_ref[0]
        lane = lax.broadcasted_iota(jnp.int32, (tq, LANES), 1)
        allowed = _chunk_mask(tq)

        def block(j, carry, masked):
            off = pl.multiple_of(j * tq, tq)
            v2 = v_ref[0, pl.ds(off, tq), :]
            new = []
            for h in range(2):
                acc, m, l = carry[3 * h], carry[3 * h + 1], carry[3 * h + 2]
                kh = k_ref[0, pl.ds(off, tq), h * HEAD_PAD:(h + 1) * HEAD_PAD]
                s = _dot_nt(q2[:, h * HEAD_PAD:(h + 1) * HEAD_PAD], kh) * scale
                if masked:
                    s = jnp.where(allowed, s, -1e30)
                m_new = jnp.maximum(m, jnp.max(s, axis=1, keepdims=True))
                alpha = jnp.exp(m - m_new)
                p = jnp.exp(s - m_new)
                new += [alpha * acc + _dot(p.astype(BF16), v2), m_new,
                        alpha * l + jnp.sum(p, axis=1, keepdims=True)]
            return tuple(new)

        zero = jnp.zeros((tq, LANES), F32)
        m0 = jnp.full((tq, 1), -1e30, F32)
        l0 = jnp.zeros((tq, 1), F32)
        carry = block(qi, (zero, m0, l0, zero, m0, l0), True)
        carry = lax.fori_loop(0, qi, lambda j, c: block(j, c, False), carry)
        y0 = carry[0] / carry[2]
        y1 = carry[3] / carry[5]
        y_ref[0] = jnp.where(lane < MLA_V, y0, y1).astype(BF16)
        lse_ref[0] = jnp.where(lane < MLA_V, carry[1] + jnp.log(carry[2]), carry[4] + jnp.log(carry[5]))

    qspec = pl.BlockSpec((1, tq, 2 * HEAD_PAD), lambda b, hp, i: (b, i, hp))
    kspec = pl.BlockSpec((1, S, 2 * HEAD_PAD), lambda b, hp, i: (b, 0, hp))
    vspec = pl.BlockSpec((1, S, LANES), lambda b, hp, i: (b, 0, hp))
    yspec = pl.BlockSpec((1, tq, LANES), lambda b, hp, i: (b, i, hp))
    return pl.pallas_call(
        body, name="mla_fwd", grid=(B, VW // LANES, nq),
        in_specs=[qspec, kspec, vspec],
        out_specs=[yspec, yspec],
        out_shape=[_sds((B, S, VW), BF16), _sds((B, S, VW), F32)],
        compiler_params=_cparams(("parallel", "parallel", "arbitrary")),
    )(qp, kp, mv)


def _mla_bwd(qp, kp, mv, y, lse, dy, dm):
    B, S, QW = qp.shape
    VW = mv.shape[2]
    tq = dm["tq"]
    nq = S // tq
    scale = (MLA_NOPE + MLA_ROPE) ** -0.5

    def body(q_ref, k_ref, v_ref, y_ref, lse_ref, dy_ref, dq_ref, dk_ref, dv_ref, dk_acc, dv_acc):
        qi = pl.program_id(2)

        @pl.when(qi == 0)
        def _():
            dk_acc[...] = jnp.zeros_like(dk_acc)
            dv_acc[...] = jnp.zeros_like(dv_acc)

        q2 = q_ref[0]
        dy2 = dy_ref[0]
        lse2 = lse_ref[0]
        lane = lax.broadcasted_iota(jnp.int32, (tq, LANES), 1)
        in_h = [lane < MLA_V, lane >= MLA_V]
        prod = dy2.astype(F32) * y_ref[0].astype(F32)
        delta = [jnp.sum(jnp.where(m, prod, 0.0), axis=1, keepdims=True) for m in in_h]
        dyh = [jnp.where(m, dy2, 0).astype(BF16) for m in in_h]
        lseh = [lse2[:, 0:1], lse2[:, MLA_V:MLA_V + 1]]
        allowed = _chunk_mask(tq)

        def block(j, carry, masked):
            off = pl.multiple_of(j * tq, tq)
            v2 = v_ref[0, pl.ds(off, tq), :]
            new = []
            dv_blk = jnp.zeros((tq, LANES), F32)
            for h in range(2):
                sl = slice(h * HEAD_PAD, (h + 1) * HEAD_PAD)
                qhh = q2[:, sl]
                kh = k_ref[0, pl.ds(off, tq), sl]
                s = _dot_nt(qhh, kh) * scale
                if masked:
                    s = jnp.where(allowed, s, -1e30)
                p = jnp.exp(s - lseh[h])
                ds = (p * (_dot_nt(dyh[h], v2) - delta[h]) * scale).astype(BF16)
                dv_blk = dv_blk + _dot_tn(p.astype(BF16), dyh[h])
                dk_acc[pl.ds(off, tq), sl] += _dot_tn(ds, qhh)
                new.append(carry[h] + _dot(ds, kh))
            dv_acc[pl.ds(off, tq), :] += dv_blk
            return tuple(new)

        zero = jnp.zeros((tq, HEAD_PAD), F32)
        carry = lax.fori_loop(0, qi, lambda j, c: block(j, c, False), (zero, zero))
        carry = block(qi, carry, True)
        dq_ref[0] = jnp.concatenate([carry[0], carry[1]], axis=1).astype(BF16)

        @pl.when(qi == nq - 1)
        def _():
            dk_ref[0] = dk_acc[...].astype(BF16)
            dv_ref[0] = dv_acc[...].astype(BF16)

    qspec = pl.BlockSpec((1, tq, 2 * HEAD_PAD), lambda b, hp, i: (b, i, hp))
    kspec = pl.BlockSpec((1, S, 2 * HEAD_PAD), lambda b, hp, i: (b, 0, hp))
    vspec = pl.BlockSpec((1, S, LANES), lambda b, hp, i: (b, 0, hp))
    yspec = pl.BlockSpec((1, tq, LANES), lambda b, hp, i: (b, i, hp))
    return pl.pallas_call(
        body, name="mla_bwd", grid=(B, VW // LANES, nq),
        in_specs=[qspec, kspec, vspec, yspec, yspec, yspec],
        out_specs=[qspec, kspec, vspec],
        out_shape=[_sds((B, S, QW), BF16), _sds((B, S, QW), BF16), _sds((B, S, VW), BF16)],
        scratch_shapes=[pltpu.VMEM((S, 2 * HEAD_PAD), F32), pltpu.VMEM((S, LANES), F32)],
        compiler_params=_cparams(("parallel", "parallel", "arbitrary")),
    )(qp, kp, mv, y, lse, dy)


def _outproj_fwd(sb_y, mla_y, x0, mod, w_o, ln_g, ln_b, dm):
    B, S, D = x0.shape
    tm = dm["tm"]
    sbw = sb_y.shape[2]

    def body(ya_ref, yb_ref, x0_ref, mod_ref, wo_ref, g_ref, b_ref, mix_ref, x1_ref, h2_ref):
        mod = mod_ref[0]
        mix = _dot(ya_ref[0], wo_ref[0:sbw, :]) + _dot(yb_ref[0], wo_ref[sbw:, :])
        mix_ref[0] = mix
        x1, _, _ = _ln_fwd(ALPHA * x0_ref[0] + (1.0 + mod[2:3]) * mix, g_ref[...], b_ref[...])
        x1_ref[0] = x1
        h2_ref[0] = (x1 * (1.0 + mod[4:5]) + mod[3:4]).astype(BF16)

    return pl.pallas_call(
        body, name="outproj_fwd", grid=(B, S // tm),
        in_specs=[_tok(tm, sbw), _tok(tm, mla_y.shape[2]), _tok(tm, D), _perb(N_MOD, D),
                  _full(w_o), _full(ln_g), _full(ln_b)],
        out_specs=[_tok(tm, D)] * 3,
        out_shape=[_sds((B, S, D), F32), _sds((B, S, D), F32), _sds((B, S, D), BF16)],
        compiler_params=_cparams(("parallel", "parallel")),
    )(sb_y, mla_y, x0, mod, w_o, ln_g, ln_b)


def _stat_specs(B, D):
    specs = [pl.BlockSpec((1, 8, D), lambda b, s: (b, 0, 0)), pl.BlockSpec((8, D), lambda b, s: (0, 0))]
    shapes = [_sds((B, 8, D), F32), _sds((8, D), F32)]
    return specs, shapes


def _stat_init(bst_ref, wst_ref):
    @pl.when(pl.program_id(1) == 0)
    def _():
        bst_ref[...] = jnp.zeros_like(bst_ref)

    @pl.when((pl.program_id(0) == 0) & (pl.program_id(1) == 0))
    def _():
        wst_ref[...] = jnp.zeros_like(wst_ref)


def _mlp_fwd(h2, x1, mod, target, w_up, w_down, ln_g, ln_b, dm):
    B, S, D = x1.shape
    tm = dm["tm"]
    nck, _, ck = w_up.shape
    dff = nck * ck

    def body(h2_ref, x1_ref, mod_ref, t_ref, wu_ref, wd_ref, g_ref, b_ref, u_ref, dr_ref, bst_ref, wst_ref):
        _stat_init(bst_ref, wst_ref)
        mod = mod_ref[0]
        h2 = h2_ref[0]
        ff = jnp.zeros((tm, D), F32)
        for c in range(nck):
            u = _dot(h2, wu_ref[c])
            u_ref[0, :, c * ck:(c + 1) * ck] = u.astype(BF16)
            act = jnp.square(jnp.maximum(u, 0.0)).astype(BF16)
            ff = ff + _dot(act, wd_ref[c])
        g = g_ref[...]
        x2, xhat, rstd = _ln_fwd(ALPHA * x1_ref[0] + (1.0 + mod[5:6]) * ff, g, b_ref[...])
        err = x2 - t_ref[0]
        dy = err * (1.0 / D)
        dr = _ln_bwd(dy, xhat, rstd, g)
        dr_ref[0] = dr
        bst_ref[0, 0:1, :] += _colsum(dr * ff)
        wst_ref[0:1, :] += _colsum(dy * xhat)
        wst_ref[1:2, :] += _colsum(dy)
        wst_ref[2:3, :] += _colsum(err * err) * (0.5 / D)

    sspecs, sshapes = _stat_specs(B, D)
    return pl.pallas_call(
        body, name="mlp_fwd", grid=(B, S // tm),
        in_specs=[_tok(tm, D), _tok(tm, D), _perb(N_MOD, D), _tok(tm, D), _full(w_up), _full(w_down),
                  _full(ln_g), _full(ln_b)],
        out_specs=[_tok(tm, dff), _tok(tm, D)] + sspecs,
        out_shape=[_sds((B, S, dff), BF16), _sds((B, S, D), F32)] + sshapes,
        compiler_params=_cparams(("arbitrary", "arbitrary")),
    )(h2, x1, mod, target, w_up, w_down, ln_g, ln_b)


def _mlp_bwd(dr2, u, x1, x0, mix, mod, w_up, w_down, w_o, ln_g, dm):
    B, S, D = x1.shape
    tm = dm["tm_small"]
    sbw = dm["sbw"]
    nck, _, ck = w_up.shape
    dff = nck * ck

    def body(dr_ref, u_ref, x1_ref, x0_ref, mix_ref, mod_ref, wu_ref, wd_ref, wo_ref, g_ref,
             du_ref, dff_ref, dmix_ref, dx0_ref, dya_ref, dyb_ref, bst_ref, wst_ref):
        _stat_init(bst_ref, wst_ref)
        mod = mod_ref[0]
        dr2 = dr_ref[0]
        dffv = ((1.0 + mod[5:6]) * dr2).astype(BF16)
        dff_ref[0] = dffv
        dh2 = jnp.zeros((tm, D), F32)
        for c in range(nck):
            sl = slice(c * ck, (c + 1) * ck)
            da = _dot_nt(dffv, wd_ref[c])
            du = (da * (2.0 * jnp.maximum(u_ref[0, :, sl].astype(F32), 0.0))).astype(BF16)
            du_ref[0, :, sl] = du
            dh2 = dh2 + _dot_nt(du, wu_ref[c])
        x1 = x1_ref[0]
        dx1 = ALPHA * dr2 + dh2 * (1.0 + mod[4:5])
        bst_ref[0, 0:1, :] += _colsum(dh2 * x1)
        bst_ref[0, 1:2, :] += _colsum(dh2)
        mix = mix_ref[0]
        g = g_ref[...]
        _, xhat, rstd = _ln_fwd(ALPHA * x0_ref[0] + (1.0 + mod[2:3]) * mix, g, 0.0)
        dr1 = _ln_bwd(dx1, xhat, rstd, g)
        wst_ref[0:1, :] += _colsum(dx1 * xhat)
        wst_ref[1:2, :] += _colsum(dx1)
        bst_ref[0, 2:3, :] += _colsum(dr1 * mix)
        dx0_ref[0] = ALPHA * dr1
        dmix = ((1.0 + mod[2:3]) * dr1).astype(BF16)
        dmix_ref[0] = dmix
        dya_ref[0] = _dot_nt(dmix, wo_ref[0:sbw, :]).astype(BF16)
        dyb_ref[0] = _dot_nt(dmix, wo_ref[sbw:, :]).astype(BF16)

    sspecs, sshapes = _stat_specs(B, D)
    wa, wb = sbw, w_o.shape[0] - sbw
    return pl.pallas_call(
        body, name="mlp_bwd", grid=(B, S // tm),
        in_specs=[_tok(tm, D), _tok(tm, dff), _tok(tm, D), _tok(tm, D), _tok(tm, D), _perb(N_MOD, D),
                  _full(w_up), _full(w_down), _full(w_o), _full(ln_g)],
        out_specs=[_tok(tm, dff), _tok(tm, D), _tok(tm, D), _tok(tm, D), _tok(tm, wa), _tok(tm, wb)] + sspecs,
        out_shape=[_sds((B, S, dff), BF16), _sds((B, S, D), BF16), _sds((B, S, D), BF16), _sds((B, S, D), F32),
                   _sds((B, S, wa), BF16), _sds((B, S, wb), BF16)] + sshapes,
        compiler_params=_cparams(("arbitrary", "arbitrary")),
    )(dr2, u, x1, x0, mix, mod, w_up, w_down, w_o, ln_g)


def _inproj_bwd(x, x0, dx0a, mod, ln_g, dq, dk, dv, dqp, dkp, dmv, cq, ckv, w_in_p, w_uq_p, w_kv, gq, gkv,
                tc, ts1, ts2, dm):
    B, S, D = x.shape
    tm = dm["tm"]
    sbw, qr, kvr, nh = dm["sbw"], dm["qr"], dm["kvr"], dm["nh"]
    qpw = nh * HEAD_PAD
    dinp = w_in_p.shape[1]
    kvw = w_kv.shape[1]

    def body(x_ref, x0_ref, dx0a_ref, mod_ref, g_ref, dq_ref, dk_ref, dv_ref, dqp_ref, dkp_ref, dmv_ref,
             cq_ref, ckv_ref, win_ref, wuq_ref, wkv_ref, gq_ref, gkv_ref, tc_ref, ts1_ref, ts2_ref,
             gx_ref, dproj_ref, dqpre_ref, dkvo_ref, bst_ref, wst_ref):
        _stat_init(bst_ref, wst_ref)
        mod = mod_ref[0]
        c1, s1, s2 = tc_ref[...], ts1_ref[...], ts2_ref[...]
        c8, s18, s28 = jnp.tile(c1, (1, nh)), jnp.tile(s1, (1, nh)), jnp.tile(s2, (1, nh))
        dqpre = _rope_t(dqp_ref[0].astype(F32), c8, s18, s28).astype(BF16)
        dqpre_ref[0] = dqpre
        gq = gq_ref[...]
        cq = cq_ref[0]
        rq = lax.rsqrt(jnp.mean(cq * cq, axis=-1, keepdims=True) + RMS_EPS)
        dqn = _dot_nt(dqpre, wuq_ref[...])
        wst_ref[4:5, 0:qr] += _colsum(dqn * cq * rq)
        dqg = dqn * gq
        dcq = rq * dqg - cq * (rq * rq * rq) * jnp.mean(dqg * cq, axis=-1, keepdims=True)

        dkpre = _rope_t(dkp_ref[0].astype(F32), c8, s18, s28)
        dkr = dkpre[:, 0:HEAD_PAD]
        for h in range(1, nh):
            dkr = dkr + dkpre[:, h * HEAD_PAD:(h + 1) * HEAD_PAD]
        lane = lax.broadcasted_iota(jnp.int32, (tm, LANES), 1)
        dkr = jnp.where((lane >= MLA_NOPE) & (lane < MLA_NOPE + MLA_ROPE), dkr, 0.0)
        dkr = pltpu.roll(dkr, LANES - MLA_NOPE, 1)
        dkvo = jnp.concatenate([dkpre.astype(BF16), dmv_ref[0]], axis=1)
        dkvo_ref[0] = dkvo
        gkv = gkv_ref[...]
        ckv = ckv_ref[0]
        rkv = lax.rsqrt(jnp.mean(ckv * ckv, axis=-1, keepdims=True) + RMS_EPS)
        dkvn = _dot_nt(dkvo, wkv_ref[...])
        wst_ref[5:6, 0:kvr] += _colsum(dkvn * ckv * rkv)
        dkg = dkvn * gkv
        dckv = rkv * dkg - ckv * (rkv * rkv * rkv) * jnp.mean(dkg * ckv, axis=-1, keepdims=True)

        dproj = jnp.concatenate([dq_ref[0], dk_ref[0], dv_ref[0], dcq.astype(BF16), dckv.astype(BF16),
                                 dkr.astype(BF16)], axis=1)
        dproj_ref[0] = dproj
        dh = _dot_nt(dproj, win_ref[...])
        x0 = x0_ref[0]
        dx0 = dx0a_ref[0] + dh * (1.0 + mod[1:2])
        bst_ref[0, 0:1, :] += _colsum(dh * x0)
        bst_ref[0, 1:2, :] += _colsum(dh)
        g = g_ref[...]
        _, xhat, rstd = _ln_fwd(x_ref[0], g, 0.0)
        gx_ref[0] = _ln_bwd(dx0, xhat, rstd, g)
        wst_ref[0:1, :] += _colsum(dx0 * xhat)
        wst_ref[1:2, :] += _colsum(dx0)

    tab = pl.BlockSpec((tm, LANES), lambda b, s: (s, 0))
    sspecs, sshapes = _stat_specs(B, D)
    return pl.pallas_call(
        body, name="inproj_bwd", grid=(B, S // tm),
        in_specs=[_tok(tm, D), _tok(tm, D), _tok(tm, D), _perb(N_MOD, D), _full(ln_g),
                  _tok(tm, sbw), _tok(tm, sbw), _tok(tm, sbw), _tok(tm, qpw), _tok(tm, qpw), _tok(tm, nh * MLA_V),
                  _tok(tm, qr), _tok(tm, kvr), _full(w_in_p), _full(w_uq_p), _full(w_kv), _full(gq), _full(gkv),
                  tab, tab, tab],
        out_specs=[_tok(tm, D), _tok(tm, dinp), _tok(tm, qpw), _tok(tm, kvw)] + sspecs,
        out_shape=[_sds((B, S, D), F32), _sds((B, S, dinp), BF16), _sds((B, S, qpw), BF16),
                   _sds((B, S, kvw), BF16)] + sshapes,
        compiler_params=_cparams(("arbitrary", "arbitrary")),
    )(x, x0, dx0a, mod, ln_g, dq, dk, dv, dqp, dkp, dmv, cq, ckv, w_in_p, w_uq_p, w_kv, gq, gkv, tc, ts1, ts2)


def _tile_of(n, cap):
    if n <= cap:
        return n
    best = n
    for t in range(LANES, cap + 1, LANES):
        if n % t == 0:
            best = t
    return best


def _mm_tn(a, g, name, relu_sq=False, out_dtype=F32, col_blocks=None):
    T, K = a.shape
    N = g.shape[1]
    tt = 512 if T % 512 == 0 else T
    tk = _tile_of(K, 1024)
    tn = N // col_blocks if col_blocks else _tile_of(N, 1280)
    nt = T // tt

    def body(a_ref, g_ref, o_ref, acc_ref):
        @pl.when(pl.program_id(2) == 0)
        def _():
            acc_ref[...] = jnp.zeros_like(acc_ref)

        av = a_ref[...]
        if relu_sq:
            av = jnp.square(jnp.maximum(av.astype(F32), 0.0)).astype(BF16)
        acc_ref[...] += _dot_tn(av, g_ref[...])

        @pl.when(pl.program_id(2) == nt - 1)
        def _():
            o_ref[...] = acc_ref[...].astype(out_dtype).reshape(o_ref.shape)

    if col_blocks:
        out_spec = pl.BlockSpec((1, tk, tn), lambda i, j, t: (j, i, 0))
        out_shape = _sds((col_blocks, K, tn), out_dtype)
    else:
        out_spec = pl.BlockSpec((tk, tn), lambda i, j, t: (i, j))
        out_shape = _sds((K, N), out_dtype)
    return pl.pallas_call(
        body, name=name, grid=(K // tk, N // tn, nt),
        in_specs=[pl.BlockSpec((tt, tk), lambda i, j, t: (t, i)), pl.BlockSpec((tt, tn), lambda i, j, t: (t, j))],
        out_specs=out_spec, out_shape=out_shape,
        scratch_shapes=[pltpu.VMEM((tk, tn), F32)],
        compiler_params=_cparams(("parallel", "parallel", "arbitrary")),
    )(a, g)


def _pair_sum(a, b, name):
    blk = (1,) + tuple(a.shape[1:])

    def body(a_ref, b_ref, o_ref):
        o_ref[...] = (a_ref[...].astype(F32) + b_ref[...].astype(F32)).astype(BF16)

    spec = pl.BlockSpec(blk, lambda k: (k, 0, 0))
    return pl.pallas_call(
        body, name=name, grid=(4,),
        in_specs=[spec, spec], out_specs=spec, out_shape=_sds(a.shape, BF16),
        compiler_params=_cparams(("parallel",)),
    )(a, b)


def _reduce_adamw(parts, w, m, v, name):
    _, K, N = parts.shape
    tr = 256 if K % 256 == 0 else K

    def body(p_ref, w_ref, m_ref, v_ref, g_ref, d_ref, nm_ref, nv_ref):
        g = p_ref[0].astype(F32)
        for k in range(1, 4):
            g = g + p_ref[k].astype(F32)
        g_ref[0] = g
        d_ref[0], nm_ref[0], nv_ref[0] = _adamw(w_ref[0], g, m_ref[0], v_ref[0])

    spec = pl.BlockSpec((1, tr, N), lambda r: (0, r, 0))
    return pl.pallas_call(
        body, name=name, grid=(K // tr,),
        in_specs=[pl.BlockSpec((4, tr, N), lambda r: (0, r, 0)), spec, spec, spec],
        out_specs=[spec] * 4, out_shape=[_sds((1, K, N), F32)] * 4,
        compiler_params=_cparams(("parallel",)),
    )(parts, w, m, v)


def _finish(sm, dmod_all, dmod_my, cact_all, p_small, m_small, v_small, b_ada, m_b, v_b, w_ada, m_w, v_w):
    n0 = p_small.shape[1]
    n1 = sm.shape[1]
    d = cact_all.shape[1]

    def body(sm_ref, dma_ref, dmm_ref, ca_ref, p_ref, pm_ref, pv_ref, b_ref, bm_ref, bv_ref, w_ref, wm_ref, wv_ref,
             gs_ref, ds_ref, ms_ref, vs_ref, gb_ref, db_ref, mb_ref, vb_ref, gw_ref, dw_ref, mw_ref, vw_ref,
             loss_ref):
        gs = sm_ref[0:1, :]
        for k in range(1, N_DEV):
            gs = gs + sm_ref[k:k + 1, :]
        gs_ref[...] = gs
        ds_ref[...], ms_ref[...], vs_ref[...] = _adamw(p_ref[...], gs[:, 0:n0], pm_ref[...], pv_ref[...])
        loss_ref[...] = jnp.zeros((1, LANES), F32) + jnp.sum(gs[:, n1 - d:n1])
        gb = jnp.sum(dma_ref[...], axis=0, keepdims=True)
        gb_ref[...] = gb
        db_ref[...], mb_ref[...], vb_ref[...] = _adamw(b_ref[...], gb, bm_ref[...], bv_ref[...])
        gw = _dot_tn(ca_ref[...].astype(BF16), dmm_ref[...].astype(BF16))
        gw_ref[...] = gw
        dw_ref[...], mw_ref[...], vw_ref[...] = _adamw(w_ref[...], gw, wm_ref[...], wv_ref[...])

    s0 = _sds(p_small.shape, F32)
    sb = _sds(b_ada.shape, F32)
    sw = _sds(w_ada.shape, F32)
    return pl.pallas_call(
        body, name="finish_small",
        out_shape=[_sds((1, n1), F32), s0, s0, s0, sb, sb, sb, sb, sw, sw, sw, sw,
                   _sds((1, LANES), F32)],
        compiler_params=pltpu.CompilerParams(vmem_limit_bytes=VMEM_LIMIT),
    )(sm, dmod_all, dmod_my, cact_all, p_small, m_small, v_small, b_ada, m_b, v_b, w_ada, m_w, v_w)


def _pack(arrs, dtype, width):
    flat = jnp.concatenate([a.astype(dtype).reshape(-1) for a in arrs])
    rows = -(-flat.shape[0] // (256 * width)) * 256
    return jnp.pad(flat, (0, rows * width - flat.shape[0])).reshape(rows, width)


def _unpack(slab, shapes):
    flat = slab.reshape(-1)
    out, o = [], 0
    for s in shapes:
        n = math.prod(s)
        out.append(flat[o:o + n].reshape(s))
        o += n
    return out


def _rope_tables(S):
    inv_freq = 1.0 / (ROPE_BASE ** (jnp.arange(0, MLA_ROPE, 2, dtype=F32) / MLA_ROPE))
    ang = jnp.arange(S, dtype=F32)[:, None] * inv_freq[None, :]
    cos, sin = jnp.cos(ang), jnp.sin(ang)
    one = jnp.ones((S, MLA_NOPE), F32)
    z16 = jnp.zeros((S, 16), F32)
    z32 = jnp.zeros((S, 32), F32)
    z64 = jnp.zeros((S, MLA_NOPE), F32)
    tc = jnp.concatenate([one, cos, cos, jnp.ones((S, 32), F32)], axis=1)
    ts1 = jnp.concatenate([z64, -sin, z16, z32], axis=1)
    ts2 = jnp.concatenate([z64, z16, sin, z32], axis=1)
    return tc, ts1, ts2


def kernel(x, c, ln_in_g, ln_in_b, w_ada, b_ada, w_in, q_norm_g, kv_norm_g, w_uq, w_ukv, w_o, ln1_g, ln1_b, w_up, w_down, ln2_g, ln2_b, loss_target, m_ln_in_g, m_ln_in_b, m_w_ada, m_b_ada, m_w_in, m_q_norm_g, m_kv_norm_g, m_w_uq, m_w_ukv, m_w_o, m_ln1_g, m_ln1_b, m_w_up, m_w_down, m_ln2_g, m_ln2_b, v_ln_in_g, v_ln_in_b, v_w_ada, v_b_ada, v_w_in, v_q_norm_g, v_kv_norm_g, v_w_uq, v_w_ukv, v_w_o, v_ln1_g, v_ln1_b, v_w_up, v_w_down, v_ln2_g, v_ln2_b):
    B, S, D = x.shape
    sbw = D // 2
    mlw = D - sbw
    nh = mlw // MLA_V
    qr = w_uq.shape[1]
    kvr = w_ukv.shape[1]
    qk = MLA_NOPE + MLA_ROPE
    dff = w_up.shape[2] * N_DEV
    din = w_in.shape[2] * N_DEV
    tm = 512 if S % 512 == 0 else S
    tq = min(512, S // 2)
    dm = dict(tm=tm, tm_small=min(tm, 256), tq=tq, sbw=sbw, qr=qr, kvr=kvr, nh=nh)
    width = 1024 if D >= 1024 else LANES
    dev = 4 * lax.axis_index("x") + 2 * lax.axis_index("y") + lax.axis_index("c")

    nada = w_ada.shape[2]
    c_all = _all_gather([c], "gather_c")[0].reshape(N_DEV * B, D)
    b_loc = lax.dynamic_slice(b_ada, (0, dev * nada), (1, nada))
    cact_all, mod_part = _ada_partial(c_all, w_ada[0], b_loc)
    mod_all = _all_gather([mod_part], "gather_mod")[0]
    mod = lax.dynamic_slice(mod_all, (0, dev * B, 0), (N_DEV, B, nada))
    mod = jnp.swapaxes(mod, 0, 1).reshape(B, N_MOD, D)

    big = [w_in, w_uq, w_ukv, w_o, w_up, w_down]
    w_in8, w_uq8, w_ukv8, w_o8, w_up8, w_down8 = _all_gather([a[0].astype(BF16) for a in big], "gather_w")
    cols = lambda a8: jnp.swapaxes(a8, 0, 1).reshape(a8.shape[1], N_DEV * a8.shape[2])
    w_in_p = jnp.pad(cols(w_in8), ((0, 0), (0, LANES - MLA_ROPE)))
    zpad = jnp.zeros((qr, nh, HEAD_PAD - qk), BF16)
    w_uq_p = jnp.concatenate([cols(w_uq8).reshape(qr, nh, qk), zpad], axis=2).reshape(qr, nh * HEAD_PAD)
    w_ukv_f = cols(w_ukv8)
    w_uk = w_ukv_f[:, :nh * MLA_NOPE].reshape(kvr, nh, MLA_NOPE)
    w_uk_p = jnp.concatenate([w_uk, jnp.zeros((kvr, nh, HEAD_PAD - MLA_NOPE), BF16)], axis=2)
    w_kv = jnp.concatenate([w_uk_p.reshape(kvr, nh * HEAD_PAD), w_ukv_f[:, nh * MLA_NOPE:]], axis=1)
    w_o_f = w_o8.reshape(D, D)

    tc, ts1, ts2 = _rope_tables(S)
    g_in, b_in = ln_in_g.reshape(1, D), ln_in_b.reshape(1, D)
    (x0, h, sq, sk, sv, qp, kp, mv, cq, ckv, qn, kvn) = _inproj_fwd(
        x, mod, g_in, b_in, w_in_p, w_uq_p, w_kv, q_norm_g, kv_norm_g, tc, ts1, ts2, dm)
    sb_y, sb_tot = _sb_fwd(sq, sk, sv, dm)
    mla_y, mla_lse = _mla_fwd(qp, kp, mv, dm)
    mix, x1, h2 = _outproj_fwd(sb_y, mla_y, x0, mod, w_o_f, ln1_g, ln1_b, dm)
    u, dr2, bst_c, wst_c = _mlp_fwd(h2, x1, mod, loss_target, w_up8, w_down8, ln2_g, ln2_b, dm)

    du, dffb, dmixb, dx0a, dsb_y, dmla_y, bst_b, wst_b = _mlp_bwd(
        dr2, u, x1, x0, mix, mod, w_up8, w_down8, w_o_f, ln1_g, dm)
    dsq, dsk, dsv = _sb_bwd(sq, sk, sv, sb_tot, dsb_y, dm)
    dqp, dkp, dmv = _mla_bwd(qp, kp, mv, mla_y, mla_lse, dmla_y, dm)
    grad_x, dproj, dqpre, dkvo, bst_a, wst_a = _inproj_bwd(
        x, x0, dx0a, mod, g_in, dsq, dsk, dsv, dqp, dkp, dmv, cq, ckv, w_in_p, w_uq_p, w_kv, q_norm_g, kv_norm_g,
        tc, ts1, ts2, dm)

    T = B * S
    r2 = lambda a: a.reshape(T, a.shape[2])
    g_in_p = _mm_tn(r2(h), r2(dproj), "grad_w_in")
    g_uq_p = _mm_tn(r2(qn), r2(dqpre), "grad_w_uq")
    g_kv = _mm_tn(r2(kvn), r2(dkvo), "grad_w_kv")
    g_o = jnp.concatenate([_mm_tn(r2(sb_y), r2(dmixb), "grad_w_o_sb", out_dtype=BF16),
                           _mm_tn(r2(mla_y), r2(dmixb), "grad_w_o_mla", out_dtype=BF16)], axis=0)
    g_up8 = _mm_tn(r2(h2), r2(du), "grad_w_up", out_dtype=BF16, col_blocks=N_DEV)
    g_down = _mm_tn(r2(u), r2(dffb), "grad_w_down", relu_sq=True, out_dtype=BF16)
    g_uq_f = g_uq_p.reshape(qr, nh, HEAD_PAD)[:, :, :qk].reshape(qr, nh * qk)
    g_uk = g_kv[:, :nh * HEAD_PAD].reshape(kvr, nh, HEAD_PAD)[:, :, :MLA_NOPE].reshape(kvr, nh * MLA_NOPE)
    g_ukv_f = jnp.concatenate([g_uk, g_kv[:, nh * HEAD_PAD:]], axis=1)

    def by_dest_cols(a):
        k, n = a.shape[0], a.shape[1] // N_DEV
        return jnp.swapaxes(a.reshape(k, N_DEV, n), 0, 1).astype(BF16)

    by_dest = [by_dest_cols(g_in_p[:, :din]), by_dest_cols(g_uq_f), by_dest_cols(g_ukv_f),
               g_o.reshape(N_DEV, D // N_DEV, D), g_up8, g_down.reshape(N_DEV, dff // N_DEV, D)]
    mine, got = _core_scatter([a.reshape((4, 2) + a.shape[1:]) for a in by_dest], "scatter_g_cores")
    names = ["w_in", "w_uq", "w_ukv", "w_o", "w_up", "w_down"]
    chip_sum = [_pair_sum(a, b, "pair_sum_" + n) for a, b, n in zip(mine, got, names)]
    quarter = _chip_exchange(chip_sum, "scatter_g_chips", scatter=True)
    moms = [m_w_in, m_w_uq, m_w_ukv, m_w_o, m_w_up, m_w_down]
    vars_ = [v_w_in, v_w_uq, v_w_ukv, v_w_o, v_w_up, v_w_down]
    res = [_reduce_adamw(p, w, m, v, "adamw_" + n) for p, w, m, v, n in zip(quarter, big, moms, vars_, names)]
    gb, db, nmb, nvb = ([r[i] for r in res] for i in range(4))

    dmod = jnp.concatenate([bst_a[:, 1], bst_a[:, 0], bst_b[:, 2], bst_b[:, 1], bst_b[:, 0], bst_c[:, 0]], axis=1)
    small = jnp.concatenate([wst_a[0], wst_a[1], wst_a[4, :qr], wst_a[5, :kvr], wst_b[0], wst_b[1],
                             wst_c[0], wst_c[1], wst_c[2]])
    n1 = small.shape[0]
    both = _all_gather([_pack([dmod, small], F32, LANES)], "gather_small")[0].reshape(N_DEV, -1)
    dmod_all = both[:, :B * N_MOD * D].reshape(N_DEV * B, N_MOD * D)
    sm = both[:, B * N_MOD * D:B * N_MOD * D + n1]
    dmod_my = lax.dynamic_slice(dmod_all, (0, dev * nada), (N_DEV * B, nada))
    row = lambda arrs: jnp.concatenate([a.reshape(1, -1) for a in arrs], axis=1)
    smalls = [ln_in_g, ln_in_b, q_norm_g, kv_norm_g, ln1_g, ln1_b, ln2_g, ln2_b]
    small_shapes = [a.shape for a in smalls]
    (gs, ds, nms, nvs, g_b, d_b, nm_b, nv_b, g_w, d_w, nm_w, nv_w, loss_v) = _finish(
        sm, dmod_all, dmod_my, cact_all, row(smalls),
        row([m_ln_in_g, m_ln_in_b, m_q_norm_g, m_kv_norm_g, m_ln1_g, m_ln1_b, m_ln2_g, m_ln2_b]),
        row([v_ln_in_g, v_ln_in_b, v_q_norm_g, v_kv_norm_g, v_ln1_g, v_ln1_b, v_ln2_g, v_ln2_b]),
        b_ada, m_b_ada, v_b_ada, w_ada[0], m_w_ada[0], v_w_ada[0])
    gsm, dsm, nmsm, nvsm = (_unpack(s, small_shapes) for s in (gs, ds, nms, nvs))

    def ordered(sm_l, w_l, ada_w, ada_b):
        return [sm_l[0], sm_l[1], ada_w[None], ada_b, w_l[0], sm_l[2], sm_l[3], w_l[1], w_l[2], w_l[3],
                sm_l[4], sm_l[5], w_l[4], w_l[5], sm_l[6], sm_l[7]]

    loss = loss_v[0, 0]
    return (loss, grad_x, *ordered(gsm, gb, g_w, g_b), *ordered(dsm, db, d_w, d_b),
            *ordered(nmsm, nmb, nm_w, nm_b), *ordered(nvsm, nvb, nv_w, nv_b))
```

```python
import functools
import math

import jax
import jax.numpy as jnp
from jax import lax
from jax.experimental import pallas as pl
from jax.experimental.pallas import tpu as pltpu

F32 = jnp.float32
BF16 = jnp.bfloat16

SB_HD = 64
MLA_V = 64
MLA_NOPE = 64
MLA_ROPE = 32
HEAD_PAD = 128
CHUNK = 64
ROPE_BASE = 10000.0
LN_EPS = 1e-5
RMS_EPS = 1e-6
DEPTH = 1
ALPHA = (2.0 * DEPTH) ** 0.25
N_MOD = 6
ADAM_LR = 0.001
ADAM_B1 = 0.9
ADAM_B2 = 0.999
ADAM_EPS = 1e-08
ADAM_WD = 0.01
ADAM_STEP = 10
N_DEV = 8
LANES = 128
CUMSUM_W = 256
VMEM_LIMIT = 56 * 1024 * 1024
MESH = pl.DeviceIdType.MESH


def _dot(a, b):
    return jnp.dot(a, b, preferred_element_type=F32)


def _dot_nt(a, b):
    return lax.dot_general(a, b, (((1,), (1,)), ((), ())), preferred_element_type=F32)


def _dot_tn(a, b):
    return lax.dot_general(a, b, (((0,), (0,)), ((), ())), preferred_element_type=F32)


def _cparams(sem):
    return pltpu.CompilerParams(dimension_semantics=sem, vmem_limit_bytes=VMEM_LIMIT)


def _full(a):
    nd = a.ndim
    return pl.BlockSpec(a.shape, lambda *_: (0,) * nd, pipeline_mode=pl.Buffered(1))


def _tok(tm, w):
    return pl.BlockSpec((1, tm, w), lambda b, s: (b, s, 0))


def _perb(rows, w):
    return pl.BlockSpec((1, rows, w), lambda b, s: (b, 0, 0))


def _sds(shape, dtype):
    return jax.ShapeDtypeStruct(shape, dtype)


def _ln_fwd(x, g, b):
    mu = jnp.mean(x, axis=-1, keepdims=True)
    xc = x - mu
    var = jnp.mean(xc * xc, axis=-1, keepdims=True)
    rstd = lax.rsqrt(var + LN_EPS)
    xhat = xc * rstd
    return xhat * g + b, xhat, rstd


def _ln_bwd(dy, xhat, rstd, g):
    dxh = dy * g
    m1 = jnp.mean(dxh, axis=-1, keepdims=True)
    m2 = jnp.mean(dxh * xhat, axis=-1, keepdims=True)
    return rstd * (dxh - m1 - xhat * m2)


def _colsum(a):
    return jnp.sum(a, axis=0, keepdims=True)


def _rope(x, c, s1, s2):
    w = x.shape[-1]
    return x * c + pltpu.roll(x, w - 16, 1) * s1 + pltpu.roll(x, 16, 1) * s2


def _rope_t(x, c, s1, s2):
    w = x.shape[-1]
    return x * c - pltpu.roll(x, w - 16, 1) * s1 - pltpu.roll(x, 16, 1) * s2


def _adamw(w, g, m, v):
    m = ADAM_B1 * m + (1.0 - ADAM_B1) * g
    v = ADAM_B2 * v + (1.0 - ADAM_B2) * (g * g)
    m_hat = m / (1.0 - ADAM_B1 ** ADAM_STEP)
    v_hat = v / (1.0 - ADAM_B2 ** ADAM_STEP)
    delta = -ADAM_LR * (m_hat / (jnp.sqrt(v_hat) + ADAM_EPS) + ADAM_WD * w)
    return delta, m, v


def _my_place():
    return lax.axis_index("x"), lax.axis_index("y"), lax.axis_index("c")


def _chip_peers(mx, my):
    out = []
    for j in (1, 2, 3):
        px = 1 - mx if (j >> 1) else mx
        py = 1 - my if (j & 1) else my
        out.append((px, py, 2 * px + py))
    return out


def _hbm_call(body, name, n_in, out_shape, sems):
    hbm = pl.BlockSpec(memory_space=pl.ANY)
    return pl.pallas_call(
        body, name=name, out_shape=out_shape,
        in_specs=[hbm] * n_in, out_specs=[hbm] * len(out_shape),
        scratch_shapes=[pltpu.SemaphoreType.DMA(s) for s in sems])


def _chip_exchange(xs, name, scatter):
    n = len(xs)

    def body(*refs):
        x_refs, o_refs = refs[:n], refs[n:2 * n]
        ssem, rsem, lsem = refs[2 * n:]
        mx, my, mc = _my_place()
        me = 2 * mx + my
        peers = _chip_peers(mx, my)

        def copy(i, j, src_slot, dst_slot):
            px, py, _ = peers[j]
            return pltpu.make_async_remote_copy(
                src_ref=x_refs[i].at[src_slot] if scatter else x_refs[i], dst_ref=o_refs[i].at[dst_slot],
                send_sem=ssem.at[i, j], recv_sem=rsem.at[i, j], device_id=(px, py, mc), device_id_type=MESH)

        local = [pltpu.make_async_copy(x_refs[i].at[me] if scatter else x_refs[i], o_refs[i].at[me], lsem.at[i])
                 for i in range(n)]
        sends = [copy(i, j, peers[j][2], me) for i in range(n) for j in range(3)]
        for cp in local + sends:
            cp.start()
        for i in range(n):
            for j in range(3):
                copy(i, j, peers[j][2], peers[j][2]).wait_recv()
        for cp in sends:
            cp.wait_send()
        for cp in local:
            cp.wait()

    out_shape = [_sds((4,) + tuple(x.shape[1:] if scatter else x.shape), x.dtype) for x in xs]
    return _hbm_call(body, name, n, out_shape, [(n, 3), (n, 3), (n,)])(*xs)


def _core_gather(xs, name):
    n = len(xs)

    def body(*refs):
        x_refs, o_refs, mine, got = refs[:n], refs[n:2 * n], refs[2 * n:3 * n], refs[3 * n:4 * n]
        lsem, ssem, rsem, osem = refs[4 * n:]
        mx, my, mc = _my_place()
        loads = [pltpu.make_async_copy(x_refs[i], mine[i], lsem.at[i]) for i in range(n)]
        for cp in loads:
            cp.start()
        sends, stores = [], []
        for i in range(n):
            loads[i].wait()
            cp = pltpu.make_async_remote_copy(
                src_ref=mine[i], dst_ref=got[i], send_sem=ssem.at[i], recv_sem=rsem.at[i],
                device_id=(mx, my, 1 - mc), device_id_type=MESH)
            cp.start()
            sends.append(cp)
            for k in range(4):
                st = pltpu.make_async_copy(mine[i].at[k], o_refs[i].at[k, mc], osem.at[i, k])
                st.start()
                stores.append(st)
        for i in range(n):
            sends[i].wait_recv()
            for k in range(4):
                st = pltpu.make_async_copy(got[i].at[k], o_refs[i].at[k, 1 - mc], osem.at[n + i, k])
                st.start()
                stores.append(st)
        for cp in sends:
            cp.wait_send()
        for st in stores:
            st.wait()

    hbm = pl.BlockSpec(memory_space=pl.ANY)
    bufs = [pltpu.VMEM(x.shape, x.dtype) for x in xs]
    return pl.pallas_call(
        body, name=name,
        out_shape=[_sds((4, 2) + tuple(x.shape[1:]), x.dtype) for x in xs],
        in_specs=[hbm] * n, out_specs=[hbm] * n,
        scratch_shapes=bufs + bufs + [pltpu.SemaphoreType.DMA((n,)), pltpu.SemaphoreType.DMA((n,)),
                                      pltpu.SemaphoreType.DMA((n,)), pltpu.SemaphoreType.DMA((2 * n, 4))],
        compiler_params=pltpu.CompilerParams(vmem_limit_bytes=VMEM_LIMIT),
    )(*xs)


def _rows_step(k):
    for r in (256, 128, 64, 32, 16, 8):
        if k % r == 0:
            return r
    return k


def _core_scatter_sum(gs, name):
    n = len(gs)

    def body(*refs):
        g_refs, o_refs = refs[:n], refs[n:2 * n]
        send, got, mine = refs[2 * n:3 * n], refs[3 * n:4 * n], refs[4 * n:5 * n]
        lsem, msem, ssem, rsem, osem = refs[5 * n:]
        mx, my, mc = _my_place()
        pairs = [(i, k) for i in range(n) for k in range(4)]
        out_loads = {(i, k): pltpu.make_async_copy(g_refs[i].at[k, 1 - mc], send[i].at[k], lsem.at[i, k])
                     for i, k in pairs}
        own_loads = {(i, k): pltpu.make_async_copy(g_refs[i].at[k, mc], mine[i].at[k], msem.at[i, k])
                     for i, k in pairs}
        for p in pairs:
            out_loads[p].start()
        for p in pairs:
            own_loads[p].start()
        sends = []
        for i in range(n):
            for k in range(4):
                out_loads[i, k].wait()
            cp = pltpu.make_async_remote_copy(
                src_ref=send[i], dst_ref=got[i], send_sem=ssem.at[i], recv_sem=rsem.at[i],
                device_id=(mx, my, 1 - mc), device_id_type=MESH)
            cp.start()
            sends.append(cp)
        stores = []
        for i in range(n):
            for k in range(4):
                own_loads[i, k].wait()
            sends[i].wait_recv()
            rows = g_refs[i].shape[2]
            step = _rows_step(rows)

            def add(r, _, i=i, step=step):
                sl = pl.ds(pl.multiple_of(r * step, step), step)
                for k in range(4):
                    mine[i][k, sl, :] = (mine[i][k, sl, :].astype(F32) + got[i][k, sl, :].astype(F32)).astype(BF16)
                return 0

            lax.fori_loop(0, rows // step, add, 0)
            st = pltpu.make_async_copy(mine[i], o_refs[i], osem.at[i])
            st.start()
            stores.append(st)
        for cp in sends:
            cp.wait_send()
        for st in stores:
            st.wait()

    hbm = pl.BlockSpec(memory_space=pl.ANY)
    blk = [(4,) + tuple(g.shape[2:]) for g in gs]
    bufs = [pltpu.VMEM(b, BF16) for b in blk]
    return pl.pallas_call(
        body, name=name,
        out_shape=[_sds(b, BF16) for b in blk],
        in_specs=[hbm] * n, out_specs=[hbm] * n,
        scratch_shapes=bufs * 3 + [pltpu.SemaphoreType.DMA((n, 4)), pltpu.SemaphoreType.DMA((n, 4)),
                                   pltpu.SemaphoreType.DMA((n,)), pltpu.SemaphoreType.DMA((n,)),
                                   pltpu.SemaphoreType.DMA((n,))],
        compiler_params=pltpu.CompilerParams(vmem_limit_bytes=VMEM_LIMIT),
    )(*gs)


def _all_gather(xs, name):
    by_chip = _chip_exchange(xs, name + "_chips", scatter=False)
    both = _core_gather(by_chip, name + "_cores")
    return [b.reshape((N_DEV,) + tuple(x.shape)) for b, x in zip(both, xs)]


def _ada_partial(c_all, w_ada_loc, b_loc):
    def body(c_ref, w_ref, b_ref, act_ref, mod_ref):
        c = c_ref[...]
        act = c * (1.0 / (1.0 + jnp.exp(-c)))
        act_ref[...] = act
        mod_ref[...] = _dot(act.astype(BF16), w_ref[...].astype(BF16)) + b_ref[...]

    nb, d = c_all.shape
    return pl.pallas_call(
        body, name="ada_partial",
        out_shape=(_sds((nb, d), F32), _sds((nb, w_ada_loc.shape[1]), F32)),
        compiler_params=pltpu.CompilerParams(vmem_limit_bytes=VMEM_LIMIT),
    )(c_all, w_ada_loc, b_loc)


def _inproj_fwd(x, mod, ln_g, ln_b, w_in_p, w_uq_p, w_kv, gq, gkv, tc, ts1, ts2, dm):
    B, S, D = x.shape
    tm = dm["tm"]
    sbw, qr, kvr, nh = dm["sbw"], dm["qr"], dm["kvr"], dm["nh"]
    o_cq, o_ckv, o_kr = 3 * sbw, 3 * sbw + qr, 3 * sbw + qr + kvr
    qpw = nh * HEAD_PAD

    def body(x_ref, mod_ref, g_ref, b_ref, win_ref, wuq_ref, wkv_ref, gq_ref, gkv_ref, tc_ref, ts1_ref, ts2_ref,
             x0_ref, h_ref, q_ref, k_ref, v_ref, qp_ref, kp_ref, mv_ref, cq_ref, ckv_ref, qn_ref, kvn_ref):
        x0, _, _ = _ln_fwd(x_ref[0], g_ref[...], b_ref[...])
        x0_ref[0] = x0
        mod = mod_ref[0]
        h = (x0 * (1.0 + mod[1:2]) + mod[0:1]).astype(BF16)
        h_ref[0] = h
        proj = _dot(h, win_ref[...])
        q_ref[0] = (proj[:, 0:sbw] * (SB_HD ** -0.5)).astype(BF16)
        k_ref[0] = proj[:, sbw:2 * sbw].astype(BF16)
        v_ref[0] = proj[:, 2 * sbw:3 * sbw].astype(BF16)
        cq = proj[:, o_cq:o_cq + qr]
        ckv = proj[:, o_ckv:o_ckv + kvr]
        cq_ref[0] = cq
        ckv_ref[0] = ckv
        qn = (cq * lax.rsqrt(jnp.mean(cq * cq, axis=-1, keepdims=True) + RMS_EPS) * gq_ref[...]).astype(BF16)
        kvn = (ckv * lax.rsqrt(jnp.mean(ckv * ckv, axis=-1, keepdims=True) + RMS_EPS) * gkv_ref[...]).astype(BF16)
        qn_ref[0] = qn
        kvn_ref[0] = kvn
        c1, s1, s2 = tc_ref[...], ts1_ref[...], ts2_ref[...]
        c8, s18, s28 = jnp.tile(c1, (1, nh)), jnp.tile(s1, (1, nh)), jnp.tile(s2, (1, nh))
        qp_ref[0] = _rope(_dot(qn, wuq_ref[...]), c8, s18, s28).astype(BF16)
        kvo = _dot(kvn, wkv_ref[...])
        kr = pltpu.roll(proj[:, o_kr:o_kr + LANES], 64, 1)
        kr = _rope(kr, c1, s1, s2)
        kp_ref[0] = (kvo[:, 0:qpw] + jnp.tile(kr, (1, nh))).astype(BF16)
        mv_ref[0] = kvo[:, qpw:].astype(BF16)

    tab = pl.BlockSpec((tm, LANES), lambda b, s: (s, 0))
    outs = [(D, F32), (D, BF16), (sbw, BF16), (sbw, BF16), (sbw, BF16), (qpw, BF16), (qpw, BF16),
            (nh * MLA_V, BF16), (qr, F32), (kvr, F32), (qr, BF16), (kvr, BF16)]
    return pl.pallas_call(
        body, name="inproj_fwd", grid=(B, S // tm),
        in_specs=[_tok(tm, D), _perb(N_MOD, D), _full(ln_g), _full(ln_b), _full(w_in_p), _full(w_uq_p),
                  _full(w_kv), _full(gq), _full(gkv), tab, tab, tab],
        out_specs=[_tok(tm, w) for w, _ in outs],
        out_shape=[_sds((B, S, w), t) for w, t in outs],
        compiler_params=_cparams(("parallel", "parallel")),
    )(x, mod, ln_g, ln_b, w_in_p, w_uq_p, w_kv, gq, gkv, tc, ts1, ts2)


def _softplus_parts(z):
    e = jnp.exp(-jnp.abs(z))
    a = -(jnp.maximum(z, 0.0) + jnp.log(1.0 + e))
    return a, e


def _split_dot(a, u):
    hi = a.astype(BF16)
    lo = (a - hi.astype(F32)).astype(BF16)
    return _dot(hi, u) + _dot(lo, u)


def _tri(n, rel):
    row = lax.broadcasted_iota(jnp.int32, (n, n), 0)
    col = lax.broadcasted_iota(jnp.int32, (n, n), 1)
    return rel(row, col).astype(BF16)


def _running_sum(a, tri, reverse, split):
    cs = tri.shape[0]
    n = a.shape[1] // cs
    out = [None] * n
    run = None
    for c in (reversed(range(n)) if reverse else range(n)):
        part = a[:, c * cs:(c + 1) * cs]
        loc = _split_dot(part, tri) if split else _dot(part.astype(BF16), tri)
        out[c] = loc if run is None else loc + run
        tot = jnp.sum(part, axis=1, keepdims=True)
        run = tot if run is None else run + tot
    return (out[0] if n == 1 else jnp.concatenate(out, axis=1)), run


def _sb_fwd(q, k, v, dm):
    B, S, W = q.shape
    tq = dm["tq"]
    nq = S // tq

    def body(q_ref, k_ref, v_ref, y_ref, tot_ref):
        qi = pl.program_id(2)
        q2 = q_ref[0]
        lane = lax.broadcasted_iota(jnp.int32, (tq, LANES), 1)
        qh = [jnp.where(lane < SB_HD, q2, 0).astype(BF16), jnp.where(lane >= SB_HD, q2, 0).astype(BF16)]
        row = lax.broadcasted_iota(jnp.int32, (tq, tq), 0)
        col = lax.broadcasted_iota(jnp.int32, (tq, tq), 1)
        later = _tri(min(tq, CUMSUM_W), lambda j, s: j > s)
        strict = col < row

        def block(j, carry, masked):
            off = pl.multiple_of(j * tq, tq)
            k2 = k_ref[0, pl.ds(off, tq), :]
            v2 = v_ref[0, pl.ds(off, tq), :]
            new = []
            for h in range(2):
                acc, run = carry[2 * h], carry[2 * h + 1]
                z = _dot_nt(qh[h], k2)
                a, _ = _softplus_parts(z)
                if masked:
                    a = jnp.where(strict, a, 0.0)
                a_later, a_tot = _running_sum(a, later, reverse=True, split=True)
                w = jnp.exp(z + a + a_later + run)
                if masked:
                    w = jnp.where(strict, w, 0.0)
                new.append(acc + _dot(w.astype(BF16), v2))
                new.append(run + a_tot)
            return tuple(new)

        zero = jnp.zeros((tq, LANES), F32)
        zrun = jnp.zeros((tq, 1), F32)
        carry = block(qi, (zero, zrun, zero, zrun), True)
        carry = lax.fori_loop(0, qi, lambda jj, c: block(qi - 1 - jj, c, False), carry)
        y_ref[0] = jnp.where(lane < SB_HD, carry[0], carry[2]).astype(BF16)
        tot_ref[0] = jnp.where(lane < SB_HD, carry[1], carry[3])

    qspec = pl.BlockSpec((1, tq, LANES), lambda b, hp, i: (b, i, hp))
    kspec = pl.BlockSpec((1, S, LANES), lambda b, hp, i: (b, 0, hp))
    return pl.pallas_call(
        body, name="sb_fwd", grid=(B, W // LANES, nq),
        in_specs=[qspec, kspec, kspec],
        out_specs=[qspec, qspec],
        out_shape=[_sds((B, S, W), BF16), _sds((B, S, W), F32)],
        compiler_params=_cparams(("parallel", "parallel", "arbitrary")),
    )(q, k, v)


def _sb_bwd(q, k, v, tot, dy, dm):
    B, S, W = q.shape
    tq = dm["tq"]
    nq = S // tq

    def body(q_ref, k_ref, v_ref, tot_ref, dy_ref, dq_ref, dk_ref, dv_ref, dk_acc, dv_acc):
        qi = pl.program_id(2)

        @pl.when(qi == 0)
        def _():
            dk_acc[...] = jnp.zeros_like(dk_acc)
            dv_acc[...] = jnp.zeros_like(dv_acc)

        q2 = q_ref[0]
        dy2 = dy_ref[0]
        tot2 = tot_ref[0]
        lane = lax.broadcasted_iota(jnp.int32, (tq, LANES), 1)
        in_h = [lane < SB_HD, lane >= SB_HD]
        qh = [jnp.where(m, q2, 0).astype(BF16) for m in in_h]
        dyh = [jnp.where(m, dy2, 0).astype(BF16) for m in in_h]
        toth = [tot2[:, 0:1], tot2[:, SB_HD:SB_HD + 1]]
        row = lax.broadcasted_iota(jnp.int32, (tq, tq), 0)
        col = lax.broadcasted_iota(jnp.int32, (tq, tq), 1)
        upto = _tri(min(tq, CUMSUM_W), lambda j, s: j <= s)
        before = _tri(min(tq, CUMSUM_W), lambda s, j: s < j)
        strict = col < row

        def block(j, carry, masked):
            off = pl.multiple_of(j * tq, tq)
            k2 = k_ref[0, pl.ds(off, tq), :]
            v2 = v_ref[0, pl.ds(off, tq), :]
            new = []
            dk_blk = jnp.zeros((tq, LANES), F32)
            dv_blk = jnp.zeros((tq, LANES), F32)
            for h in range(2):
                dq, pa, pg = carry[3 * h], carry[3 * h + 1], carry[3 * h + 2]
                z = _dot_nt(qh[h], k2)
                a, e = _softplus_parts(z)
                if masked:
                    a = jnp.where(strict, a, 0.0)
                a_upto, a_tot = _running_sum(a, upto, reverse=False, split=True)
                w = jnp.exp(z + a + (toth[h] - pa - a_upto))
                if masked:
                    w = jnp.where(strict, w, 0.0)
                g = _dot_nt(dyh[h], v2) * w
                g_before, g_tot = _running_sum(g, before, reverse=False, split=False)
                g_before = g_before + pg
                r = 1.0 / (1.0 + e)
                er = e * r
                pos = z >= 0.0
                sig = jnp.where(pos, r, er)
                dz = g * jnp.where(pos, er, r) - g_before * sig
                if masked:
                    dz = jnp.where(strict, dz, 0.0)
                dzb = dz.astype(BF16)
                dv_blk = dv_blk + _dot_tn(w.astype(BF16), dyh[h])
                dk_blk = dk_blk + _dot_tn(dzb, qh[h])
                new += [dq + _dot(dzb, k2), pa + a_tot, pg + g_tot]
            dk_acc[pl.ds(off, tq), :] += dk_blk
            dv_acc[pl.ds(off, tq), :] += dv_blk
            return tuple(new)

        zero = jnp.zeros((tq, LANES), F32)
        zrun = jnp.zeros((tq, 1), F32)
        carry = lax.fori_loop(0, qi, lambda j, c: block(j, c, False), (zero, zrun, zrun, zero, zrun, zrun))
        carry = block(qi, carry, True)
        dq_ref[0] = (jnp.where(in_h[0], carry[0], carry[3]) * (SB_HD ** -0.5)).astype(BF16)

        @pl.when(qi == nq - 1)
        def _():
            dk_ref[0] = dk_acc[...].astype(BF16)
            dv_ref[0] = dv_acc[...].astype(BF16)

    qspec = pl.BlockSpec((1, tq, LANES), lambda b, hp, i: (b, i, hp))
    kspec = pl.BlockSpec((1, S, LANES), lambda b, hp, i: (b, 0, hp))
    return pl.pallas_call(
        body, name="sb_bwd", grid=(B, W // LANES, nq),
        in_specs=[qspec, kspec, kspec, qspec, qspec],
        out_specs=[qspec, kspec, kspec],
        out_shape=[_sds((B, S, W), BF16)] * 3,
        scratch_shapes=[pltpu.VMEM((S, LANES), F32), pltpu.VMEM((S, LANES), F32)],
        compiler_params=_cparams(("parallel", "parallel", "arbitrary")),
    )(q, k, v, tot, dy)


def _chunk_mask(tq):
    row = lax.broadcasted_iota(jnp.int32, (tq, tq), 0)
    col = lax.broadcasted_iota(jnp.int32, (tq, tq), 1)
    return lax.shift_right_logical(col, 6) <= lax.shift_right_logical(row, 6)


def _mla_fwd(qp, kp, mv, dm):
    B, S, QW = qp.shape
    VW = mv.shape[2]
    tq = dm["tq"]
    nq = S // tq
    scale = (MLA_NOPE + MLA_ROPE) ** -0.5
    assert CHUNK == 64

    def body(q_ref, k_ref, v_ref, y_ref, lse_ref):
        qi = pl.program_id(2)
        q2 = q_ref[0]
        lane = lax.broadcasted_iota(jnp.int32, (tq, LANES), 1)
        allowed = _chunk_mask(tq)

        def block(j, carry, masked):
            off = pl.multiple_of(j * tq, tq)
            v2 = v_ref[0, pl.ds(off, tq), :]
            new = []
            for h in range(2):
                acc, m, l = carry[3 * h], carry[3 * h + 1], carry[3 * h + 2]
                kh = k_ref[0, pl.ds(off, tq), h * HEAD_PAD:(h + 1) * HEAD_PAD]
                s = _dot_nt(q2[:, h * HEAD_PAD:(h + 1) * HEAD_PAD], kh) * scale
                if masked:
                    s = jnp.where(allowed, s, -1e30)
                m_new = jnp.maximum(m, jnp.max(s, axis=1, keepdims=True))
                alpha = jnp.exp(m - m_new)
                p = jnp.exp(s - m_new)
                new += [alpha * acc + _dot(p.astype(BF16), v2), m_new,
                        alpha * l + jnp.sum(p, axis=1, keepdims=True)]
            return tuple(new)

        zero = jnp.zeros((tq, LANES), F32)
        m0 = jnp.full((tq, 1), -1e30, F32)
        l0 = jnp.zeros((tq, 1), F32)
        carry = block(qi, (zero, m0, l0, zero, m0, l0), True)
        carry = lax.fori_loop(0, qi, lambda j, c: block(j, c, False), carry)
        y0 = carry[0] / carry[2]
        y1 = carry[3] / carry[5]
        y_ref[0] = jnp.where(lane < MLA_V, y0, y1).astype(BF16)
        lse_ref[0] = jnp.where(lane < MLA_V, carry[1] + jnp.log(carry[2]), carry[4] + jnp.log(carry[5]))

    qspec = pl.BlockSpec((1, tq, 2 * HEAD_PAD), lambda b, hp, i: (b, i, hp))
    kspec = pl.BlockSpec((1, S, 2 * HEAD_PAD), lambda b, hp, i: (b, 0, hp))
    vspec = pl.BlockSpec((1, S, LANES), lambda b, hp, i: (b, 0, hp))
    yspec = pl.BlockSpec((1, tq, LANES), lambda b, hp, i: (b, i, hp))
    return pl.pallas_call(
        body, name="mla_fwd", grid=(B, VW // LANES, nq),
        in_specs=[qspec, kspec, vspec],
        out_specs=[yspec, yspec],
        out_shape=[_sds((B, S, VW), BF16), _sds((B, S, VW), F32)],
        compiler_params=_cparams(("parallel", "parallel", "arbitrary")),
    )(qp, kp, mv)


def _mla_bwd(qp, kp, mv, y, lse, dy, dm):
    B, S, QW = qp.shape
    VW = mv.shape[2]
    tq = dm["tq"]
    nq = S // tq
    scale = (MLA_NOPE + MLA_ROPE) ** -0.5

    def body(q_ref, k_ref, v_ref, y_ref, lse_ref, dy_ref, dq_ref, dk_ref, dv_ref, dk_acc, dv_acc):
        qi = pl.program_id(2)

        @pl.when(qi == 0)
        def _():
            dk_acc[...] = jnp.zeros_like(dk_acc)
            dv_acc[...] = jnp.zeros_like(dv_acc)

        q2 = q_ref[0]
        dy2 = dy_ref[0]
        lse2 = lse_ref[0]
        lane = lax.broadcasted_iota(jnp.int32, (tq, LANES), 1)
        in_h = [lane < MLA_V, lane >= MLA_V]
        prod = dy2.astype(F32) * y_ref[0].astype(F32)
        delta = [jnp.sum(jnp.where(m, prod, 0.0), axis=1, keepdims=True) for m in in_h]
        dyh = [jnp.where(m, dy2, 0).astype(BF16) for m in in_h]
        lseh = [lse2[:, 0:1], lse2[:, MLA_V:MLA_V + 1]]
        allowed = _chunk_mask(tq)

        def block(j, carry, masked):
            off = pl.multiple_of(j * tq, tq)
            v2 = v_ref[0, pl.ds(off, tq), :]
            new = []
            dv_blk = jnp.zeros((tq, LANES), F32)
            for h in range(2):
                sl = slice(h * HEAD_PAD, (h + 1) * HEAD_PAD)
                qhh = q2[:, sl]
                kh = k_ref[0, pl.ds(off, tq), sl]
                s = _dot_nt(qhh, kh) * scale
                if masked:
                    s = jnp.where(allowed, s, -1e30)
                p = jnp.exp(s - lseh[h])
                ds = (p * (_dot_nt(dyh[h], v2) - delta[h]) * scale).astype(BF16)
                dv_blk = dv_blk + _dot_tn(p.astype(BF16), dyh[h])
                dk_acc[pl.ds(off, tq), sl] += _dot_tn(ds, qhh)
                new.append(carry[h] + _dot(ds, kh))
            dv_acc[pl.ds(off, tq), :] += dv_blk
            return tuple(new)

        zero = jnp.zeros((tq, HEAD_PAD), F32)
        carry = lax.fori_loop(0, qi, lambda j, c: block(j, c, False), (zero, zero))
        carry = block(qi, carry, True)
        dq_ref[0] = jnp.concatenate([carry[0], carry[1]], axis=1).astype(BF16)

        @pl.when(qi == nq - 1)
        def _():
            dk_ref[0] = dk_acc[...].astype(BF16)
            dv_ref[0] = dv_acc[...].astype(BF16)

    qspec = pl.BlockSpec((1, tq, 2 * HEAD_PAD), lambda b, hp, i: (b, i, hp))
    kspec = pl.BlockSpec((1, S, 2 * HEAD_PAD), lambda b, hp, i: (b, 0, hp))
    vspec = pl.BlockSpec((1, S, LANES), lambda b, hp, i: (b, 0, hp))
    yspec = pl.BlockSpec((1, tq, LANES), lambda b, hp, i: (b, i, hp))
    return pl.pallas_call(
        body, name="mla_bwd", grid=(B, VW // LANES, nq),
        in_specs=[qspec, kspec, vspec, yspec, yspec, yspec],
        out_specs=[qspec, kspec, vspec],
        out_shape=[_sds((B, S, QW), BF16), _sds((B, S, QW), BF16), _sds((B, S, VW), BF16)],
        scratch_shapes=[pltpu.VMEM((S, 2 * HEAD_PAD), F32), pltpu.VMEM((S, LANES), F32)],
        compiler_params=_cparams(("parallel", "parallel", "arbitrary")),
    )(qp, kp, mv, y, lse, dy)


def _outproj_fwd(sb_y, mla_y, x0, mod, w_o, ln_g, ln_b, dm):
    B, S, D = x0.shape
    tm = dm["tm"]
    sbw = sb_y.shape[2]

    def body(ya_ref, yb_ref, x0_ref, mod_ref, wo_ref, g_ref, b_ref, mix_ref, x1_ref, h2_ref):
        mod = mod_ref[0]
        mix = _dot(ya_ref[0], wo_ref[0:sbw, :]) + _dot(yb_ref[0], wo_ref[sbw:, :])
        mix_ref[0] = mix
        x1, _, _ = _ln_fwd(ALPHA * x0_ref[0] + (1.0 + mod[2:3]) * mix, g_ref[...], b_ref[...])
        x1_ref[0] = x1
        h2_ref[0] = (x1 * (1.0 + mod[4:5]) + mod[3:4]).astype(BF16)

    return pl.pallas_call(
        body, name="outproj_fwd", grid=(B, S // tm),
        in_specs=[_tok(tm, sbw), _tok(tm, mla_y.shape[2]), _tok(tm, D), _perb(N_MOD, D),
                  _full(w_o), _full(ln_g), _full(ln_b)],
        out_specs=[_tok(tm, D)] * 3,
        out_shape=[_sds((B, S, D), F32), _sds((B, S, D), F32), _sds((B, S, D), BF16)],
        compiler_params=_cparams(("parallel", "parallel")),
    )(sb_y, mla_y, x0, mod, w_o, ln_g, ln_b)


def _stat_specs(B, D):
    specs = [pl.BlockSpec((1, 8, D), lambda b, s: (b, 0, 0)), pl.BlockSpec((8, D), lambda b, s: (0, 0))]
    shapes = [_sds((B, 8, D), F32), _sds((8, D), F32)]
    return specs, shapes


def _stat_init(bst_ref, wst_ref):
    @pl.when(pl.program_id(1) == 0)
    def _():
        bst_ref[...] = jnp.zeros_like(bst_ref)

    @pl.when((pl.program_id(0) == 0) & (pl.program_id(1) == 0))
    def _():
        wst_ref[...] = jnp.zeros_like(wst_ref)


def _mlp_fwd(h2, x1, mod, target, w_up, w_down, ln_g, ln_b, dm):
    B, S, D = x1.shape
    tm = dm["tm"]
    nck, _, ck = w_up.shape
    dff = nck * ck

    def body(h2_ref, x1_ref, mod_ref, t_ref, wu_ref, wd_ref, g_ref, b_ref, u_ref, dr_ref, bst_ref, wst_ref):
        _stat_init(bst_ref, wst_ref)
        mod = mod_ref[0]
        h2 = h2_ref[0]
        ff = jnp.zeros((tm, D), F32)
        for c in range(nck):
            u = _dot(h2, wu_ref[c])
            u_ref[0, :, c * ck:(c + 1) * ck] = u.astype(BF16)
            act = jnp.square(jnp.maximum(u, 0.0)).astype(BF16)
            ff = ff + _dot(act, wd_ref[c])
        g = g_ref[...]
        x2, xhat, rstd = _ln_fwd(ALPHA * x1_ref[0] + (1.0 + mod[5:6]) * ff, g, b_ref[...])
        err = x2 - t_ref[0]
        dy = err * (1.0 / D)
        dr = _ln_bwd(dy, xhat, rstd, g)
        dr_ref[0] = dr
        bst_ref[0, 0:1, :] += _colsum(dr * ff)
        wst_ref[0:1, :] += _colsum(dy * xhat)
        wst_ref[1:2, :] += _colsum(dy)
        wst_ref[2:3, :] += _colsum(err * err) * (0.5 / D)

    sspecs, sshapes = _stat_specs(B, D)
    return pl.pallas_call(
        body, name="mlp_fwd", grid=(B, S // tm),
        in_specs=[_tok(tm, D), _tok(tm, D), _perb(N_MOD, D), _tok(tm, D), _full(w_up), _full(w_down),
                  _full(ln_g), _full(ln_b)],
        out_specs=[_tok(tm, dff), _tok(tm, D)] + sspecs,
        out_shape=[_sds((B, S, dff), BF16), _sds((B, S, D), F32)] + sshapes,
        compiler_params=_cparams(("arbitrary", "arbitrary")),
    )(h2, x1, mod, target, w_up, w_down, ln_g, ln_b)


def _mlp_bwd(dr2, u, x1, x0, mix, mod, w_up, w_down, w_o, ln_g, dm):
    B, S, D = x1.shape
    tm = dm["tm_small"]
    sbw = dm["sbw"]
    nck, _, ck = w_up.shape
    dff = nck * ck

    def body(dr_ref, u_ref, x1_ref, x0_ref, mix_ref, mod_ref, wu_ref, wd_ref, wo_ref, g_ref,
             du_ref, dff_ref, dmix_ref, dx0_ref, dya_ref, dyb_ref, bst_ref, wst_ref):
        _stat_init(bst_ref, wst_ref)
        mod = mod_ref[0]
        dr2 = dr_ref[0]
        dffv = ((1.0 + mod[5:6]) * dr2).astype(BF16)
        dff_ref[0] = dffv
        dh2 = jnp.zeros((tm, D), F32)
        for c in range(nck):
            sl = slice(c * ck, (c + 1) * ck)
            da = _dot_nt(dffv, wd_ref[c])
            du = (da * (2.0 * jnp.maximum(u_ref[0, :, sl].astype(F32), 0.0))).astype(BF16)
            du_ref[0, :, sl] = du
            dh2 = dh2 + _dot_nt(du, wu_ref[c])
        x1 = x1_ref[0]
        dx1 = ALPHA * dr2 + dh2 * (1.0 + mod[4:5])
        bst_ref[0, 0:1, :] += _colsum(dh2 * x1)
        bst_ref[0, 1:2, :] += _colsum(dh2)
        mix = mix_ref[0]
        g = g_ref[...]
        _, xhat, rstd = _ln_fwd(ALPHA * x0_ref[0] + (1.0 + mod[2:3]) * mix, g, 0.0)
        dr1 = _ln_bwd(dx1, xhat, rstd, g)
        wst_ref[0:1, :] += _colsum(dx1 * xhat)
        wst_ref[1:2, :] += _colsum(dx1)
        bst_ref[0, 2:3, :] += _colsum(dr1 * mix)
        dx0_ref[0] = ALPHA * dr1
        dmix = ((1.0 + mod[2:3]) * dr1).astype(BF16)
        dmix_ref[0] = dmix
        dya_ref[0] = _dot_nt(dmix, wo_ref[0:sbw, :]).astype(BF16)
        dyb_ref[0] = _dot_nt(dmix, wo_ref[sbw:, :]).astype(BF16)

    sspecs, sshapes = _stat_specs(B, D)
    wa, wb = sbw, w_o.shape[0] - sbw
    return pl.pallas_call(
        body, name="mlp_bwd", grid=(B, S // tm),
        in_specs=[_tok(tm, D), _tok(tm, dff), _tok(tm, D), _tok(tm, D), _tok(tm, D), _perb(N_MOD, D),
                  _full(w_up), _full(w_down), _full(w_o), _full(ln_g)],
        out_specs=[_tok(tm, dff), _tok(tm, D), _tok(tm, D), _tok(tm, D), _tok(tm, wa), _tok(tm, wb)] + sspecs,
        out_shape=[_sds((B, S, dff), BF16), _sds((B, S, D), BF16), _sds((B, S, D), BF16), _sds((B, S, D), F32),
                   _sds((B, S, wa), BF16), _sds((B, S, wb), BF16)] + sshapes,
        compiler_params=_cparams(("arbitrary", "arbitrary")),
    )(dr2, u, x1, x0, mix, mod, w_up, w_down, w_o, ln_g)


def _inproj_bwd(x, x0, dx0a, mod, ln_g, dq, dk, dv, dqp, dkp, dmv, cq, ckv, w_in_p, w_uq_p, w_kv, gq, gkv,
                tc, ts1, ts2, dm):
    B, S, D = x.shape
    tm = dm["tm"]
    sbw, qr, kvr, nh = dm["sbw"], dm["qr"], dm["kvr"], dm["nh"]
    qpw = nh * HEAD_PAD
    dinp = w_in_p.shape[1]
    kvw = w_kv.shape[1]

    def body(x_ref, x0_ref, dx0a_ref, mod_ref, g_ref, dq_ref, dk_ref, dv_ref, dqp_ref, dkp_ref, dmv_ref,
             cq_ref, ckv_ref, win_ref, wuq_ref, wkv_ref, gq_ref, gkv_ref, tc_ref, ts1_ref, ts2_ref,
             gx_ref, dproj_ref, dqpre_ref, dkvo_ref, bst_ref, wst_ref):
        _stat_init(bst_ref, wst_ref)
        mod = mod_ref[0]
        c1, s1, s2 = tc_ref[...], ts1_ref[...], ts2_ref[...]
        c8, s18, s28 = jnp.tile(c1, (1, nh)), jnp.tile(s1, (1, nh)), jnp.tile(s2, (1, nh))
        dqpre = _rope_t(dqp_ref[0].astype(F32), c8, s18, s28).astype(BF16)
        dqpre_ref[0] = dqpre
        gq = gq_ref[...]
        cq = cq_ref[0]
        rq = lax.rsqrt(jnp.mean(cq * cq, axis=-1, keepdims=True) + RMS_EPS)
        dqn = _dot_nt(dqpre, wuq_ref[...])
        wst_ref[4:5, 0:qr] += _colsum(dqn * cq * rq)
        dqg = dqn * gq
        dcq = rq * dqg - cq * (rq * rq * rq) * jnp.mean(dqg * cq, axis=-1, keepdims=True)

        dkpre = _rope_t(dkp_ref[0].astype(F32), c8, s18, s28)
        dkr = dkpre[:, 0:HEAD_PAD]
        for h in range(1, nh):
            dkr = dkr + dkpre[:, h * HEAD_PAD:(h + 1) * HEAD_PAD]
        lane = lax.broadcasted_iota(jnp.int32, (tm, LANES), 1)
        dkr = jnp.where((lane >= MLA_NOPE) & (lane < MLA_NOPE + MLA_ROPE), dkr, 0.0)
        dkr = pltpu.roll(dkr, LANES - MLA_NOPE, 1)
        dkvo = jnp.concatenate([dkpre.astype(BF16), dmv_ref[0]], axis=1)
        dkvo_ref[0] = dkvo
        gkv = gkv_ref[...]
        ckv = ckv_ref[0]
        rkv = lax.rsqrt(jnp.mean(ckv * ckv, axis=-1, keepdims=True) + RMS_EPS)
        dkvn = _dot_nt(dkvo, wkv_ref[...])
        wst_ref[5:6, 0:kvr] += _colsum(dkvn * ckv * rkv)
        dkg = dkvn * gkv
        dckv = rkv * dkg - ckv * (rkv * rkv * rkv) * jnp.mean(dkg * ckv, axis=-1, keepdims=True)

        dproj = jnp.concatenate([dq_ref[0], dk_ref[0], dv_ref[0], dcq.astype(BF16), dckv.astype(BF16),
                                 dkr.astype(BF16)], axis=1)
        dproj_ref[0] = dproj
        dh = _dot_nt(dproj, win_ref[...])
        x0 = x0_ref[0]
        dx0 = dx0a_ref[0] + dh * (1.0 + mod[1:2])
        bst_ref[0, 0:1, :] += _colsum(dh * x0)
        bst_ref[0, 1:2, :] += _colsum(dh)
        g = g_ref[...]
        _, xhat, rstd = _ln_fwd(x_ref[0], g, 0.0)
        gx_ref[0] = _ln_bwd(dx0, xhat, rstd, g)
        wst_ref[0:1, :] += _colsum(dx0 * xhat)
        wst_ref[1:2, :] += _colsum(dx0)

    tab = pl.BlockSpec((tm, LANES), lambda b, s: (s, 0))
    sspecs, sshapes = _stat_specs(B, D)
    return pl.pallas_call(
        body, name="inproj_bwd", grid=(B, S // tm),
        in_specs=[_tok(tm, D), _tok(tm, D), _tok(tm, D), _perb(N_MOD, D), _full(ln_g),
                  _tok(tm, sbw), _tok(tm, sbw), _tok(tm, sbw), _tok(tm, qpw), _tok(tm, qpw), _tok(tm, nh * MLA_V),
                  _tok(tm, qr), _tok(tm, kvr), _full(w_in_p), _full(w_uq_p), _full(w_kv), _full(gq), _full(gkv),
                  tab, tab, tab],
        out_specs=[_tok(tm, D), _tok(tm, dinp), _tok(tm, qpw), _tok(tm, kvw)] + sspecs,
        out_shape=[_sds((B, S, D), F32), _sds((B, S, dinp), BF16), _sds((B, S, qpw), BF16),
                   _sds((B, S, kvw), BF16)] + sshapes,
        compiler_params=_cparams(("arbitrary", "arbitrary")),
    )(x, x0, dx0a, mod, ln_g, dq, dk, dv, dqp, dkp, dmv, cq, ckv, w_in_p, w_uq_p, w_kv, gq, gkv, tc, ts1, ts2)


def _tile_of(n, cap):
    if n <= cap:
        return n
    best = n
    for t in range(LANES, cap + 1, LANES):
        if n % t == 0:
            best = t
    return best


def _mm_tn(a, g, name, relu_sq=False, out_dtype=F32, col_blocks=None):
    T, K = a.shape
    N = g.shape[1]
    tt = 512 if T % 512 == 0 else T
    tk = _tile_of(K, 1024)
    tn = N // col_blocks if col_blocks else _tile_of(N, 1280)
    nt = T // tt

    def body(a_ref, g_ref, o_ref, acc_ref):
        @pl.when(pl.program_id(2) == 0)
        def _():
            acc_ref[...] = jnp.zeros_like(acc_ref)

        av = a_ref[...]
        if relu_sq:
            av = jnp.square(jnp.maximum(av.astype(F32), 0.0)).astype(BF16)
        acc_ref[...] += _dot_tn(av, g_ref[...])

        @pl.when(pl.program_id(2) == nt - 1)
        def _():
            o_ref[...] = acc_ref[...].astype(out_dtype).reshape(o_ref.shape)

    if col_blocks:
        out_spec = pl.BlockSpec((1, tk, tn), lambda i, j, t: (j, i, 0))
        out_shape = _sds((col_blocks, K, tn), out_dtype)
    else:
        out_spec = pl.BlockSpec((tk, tn), lambda i, j, t: (i, j))
        out_shape = _sds((K, N), out_dtype)
    return pl.pallas_call(
        body, name=name, grid=(K // tk, N // tn, nt),
        in_specs=[pl.BlockSpec((tt, tk), lambda i, j, t: (t, i)), pl.BlockSpec((tt, tn), lambda i, j, t: (t, j))],
        out_specs=out_spec, out_shape=out_shape,
        scratch_shapes=[pltpu.VMEM((tk, tn), F32)],
        compiler_params=_cparams(("parallel", "parallel", "arbitrary")),
    )(a, g)


def _reduce_adamw(parts, w, m, v, name):
    _, K, N = parts.shape
    tr = 256 if K % 256 == 0 else K

    def body(p_ref, w_ref, m_ref, v_ref, g_ref, d_ref, nm_ref, nv_ref):
        g = p_ref[0].astype(F32)
        for k in range(1, 4):
            g = g + p_ref[k].astype(F32)
        g_ref[0] = g
        d_ref[0], nm_ref[0], nv_ref[0] = _adamw(w_ref[0], g, m_ref[0], v_ref[0])

    spec = pl.BlockSpec((1, tr, N), lambda r: (0, r, 0))
    return pl.pallas_call(
        body, name=name, grid=(K // tr,),
        in_specs=[pl.BlockSpec((4, tr, N), lambda r: (0, r, 0)), spec, spec, spec],
        out_specs=[spec] * 4, out_shape=[_sds((1, K, N), F32)] * 4,
        compiler_params=_cparams(("parallel",)),
    )(parts, w, m, v)


def _finish(sm, dmod_all, dmod_my, cact_all, p_small, m_small, v_small, b_ada, m_b, v_b, w_ada, m_w, v_w):
    n0 = p_small.shape[1]
    n1 = sm.shape[1]
    d = cact_all.shape[1]

    def body(sm_ref, dma_ref, dmm_ref, ca_ref, p_ref, pm_ref, pv_ref, b_ref, bm_ref, bv_ref, w_ref, wm_ref, wv_ref,
             gs_ref, ds_ref, ms_ref, vs_ref, gb_ref, db_ref, mb_ref, vb_ref, gw_ref, dw_ref, mw_ref, vw_ref,
             loss_ref):
        gs = sm_ref[0:1, :]
        for k in range(1, N_DEV):
            gs = gs + sm_ref[k:k + 1, :]
        gs_ref[...] = gs
        ds_ref[...], ms_ref[...], vs_ref[...] = _adamw(p_ref[...], gs[:, 0:n0], pm_ref[...], pv_ref[...])
        loss_ref[...] = jnp.zeros((1, LANES), F32) + jnp.sum(gs[:, n1 - d:n1])
        gb = jnp.sum(dma_ref[...], axis=0, keepdims=True)
        gb_ref[...] = gb
        db_ref[...], mb_ref[...], vb_ref[...] = _adamw(b_ref[...], gb, bm_ref[...], bv_ref[...])
        gw = _dot_tn(ca_ref[...].astype(BF16), dmm_ref[...].astype(BF16))
        gw_ref[...] = gw
        dw_ref[...], mw_ref[...], vw_ref[...] = _adamw(w_ref[...], gw, wm_ref[...], wv_ref[...])

    s0 = _sds(p_small.shape, F32)
    sb = _sds(b_ada.shape, F32)
    sw = _sds(w_ada.shape, F32)
    return pl.pallas_call(
        body, name="finish_small",
        out_shape=[_sds((1, n1), F32), s0, s0, s0, sb, sb, sb, sb, sw, sw, sw, sw,
                   _sds((1, LANES), F32)],
        compiler_params=pltpu.CompilerParams(vmem_limit_bytes=VMEM_LIMIT),
    )(sm, dmod_all, dmod_my, cact_all, p_small, m_small, v_small, b_ada, m_b, v_b, w_ada, m_w, v_w)


def _pack(arrs, dtype, width):
    flat = jnp.concatenate([a.astype(dtype).reshape(-1) for a in arrs])
    rows = -(-flat.shape[0] // (256 * width)) * 256
    return jnp.pad(flat, (0, rows * width - flat.shape[0])).reshape(rows, width)


def _unpack(slab, shapes):
    flat = slab.reshape(-1)
    out, o = [], 0
    for s in shapes:
        n = math.prod(s)
        out.append(flat[o:o + n].reshape(s))
        o += n
    return out


def _rope_tables(S):
    inv_freq = 1.0 / (ROPE_BASE ** (jnp.arange(0, MLA_ROPE, 2, dtype=F32) / MLA_ROPE))
    ang = jnp.arange(S, dtype=F32)[:, None] * inv_freq[None, :]
    cos, sin = jnp.cos(ang), jnp.sin(ang)
    one = jnp.ones((S, MLA_NOPE), F32)
    z16 = jnp.zeros((S, 16), F32)
    z32 = jnp.zeros((S, 32), F32)
    z64 = jnp.zeros((S, MLA_NOPE), F32)
    tc = jnp.concatenate([one, cos, cos, jnp.ones((S, 32), F32)], axis=1)
    ts1 = jnp.concatenate([z64, -sin, z16, z32], axis=1)
    ts2 = jnp.concatenate([z64, z16, sin, z32], axis=1)
    return tc, ts1, ts2


def kernel(x, c, ln_in_g, ln_in_b, w_ada, b_ada, w_in, q_norm_g, kv_norm_g, w_uq, w_ukv, w_o, ln1_g, ln1_b, w_up, w_down, ln2_g, ln2_b, loss_target, m_ln_in_g, m_ln_in_b, m_w_ada, m_b_ada, m_w_in, m_q_norm_g, m_kv_norm_g, m_w_uq, m_w_ukv, m_w_o, m_ln1_g, m_ln1_b, m_w_up, m_w_down, m_ln2_g, m_ln2_b, v_ln_in_g, v_ln_in_b, v_w_ada, v_b_ada, v_w_in, v_q_norm_g, v_kv_norm_g, v_w_uq, v_w_ukv, v_w_o, v_ln1_g, v_ln1_b, v_w_up, v_w_down, v_ln2_g, v_ln2_b):
    B, S, D = x.shape
    sbw = D // 2
    mlw = D - sbw
    nh = mlw // MLA_V
    qr = w_uq.shape[1]
    kvr = w_ukv.shape[1]
    qk = MLA_NOPE + MLA_ROPE
    dff = w_up.shape[2] * N_DEV
    din = w_in.shape[2] * N_DEV
    tm = 512 if S % 512 == 0 else S
    tq = min(512, S // 2)
    dm = dict(tm=tm, tm_small=min(tm, 256), tq=tq, sbw=sbw, qr=qr, kvr=kvr, nh=nh)
    width = 1024 if D >= 1024 else LANES
    dev = 4 * lax.axis_index("x") + 2 * lax.axis_index("y") + lax.axis_index("c")

    nada = w_ada.shape[2]
    c_all = _all_gather([c], "gather_c")[0].reshape(N_DEV * B, D)
    b_loc = lax.dynamic_slice(b_ada, (0, dev * nada), (1, nada))
    cact_all, mod_part = _ada_partial(c_all, w_ada[0], b_loc)
    mod_all = _all_gather([mod_part], "gather_mod")[0]
    mod = lax.dynamic_slice(mod_all, (0, dev * B, 0), (N_DEV, B, nada))
    mod = jnp.swapaxes(mod, 0, 1).reshape(B, N_MOD, D)

    big = [w_in, w_uq, w_ukv, w_o, w_up, w_down]
    w_in8, w_uq8, w_ukv8, w_o8, w_up8, w_down8 = _all_gather([a[0].astype(BF16) for a in big], "gather_w")
    cols = lambda a8: jnp.swapaxes(a8, 0, 1).reshape(a8.shape[1], N_DEV * a8.shape[2])
    w_in_p = jnp.pad(cols(w_in8), ((0, 0), (0, LANES - MLA_ROPE)))
    zpad = jnp.zeros((qr, nh, HEAD_PAD - qk), BF16)
    w_uq_p = jnp.concatenate([cols(w_uq8).reshape(qr, nh, qk), zpad], axis=2).reshape(qr, nh * HEAD_PAD)
    w_ukv_f = cols(w_ukv8)
    w_uk = w_ukv_f[:, :nh * MLA_NOPE].reshape(kvr, nh, MLA_NOPE)
    w_uk_p = jnp.concatenate([w_uk, jnp.zeros((kvr, nh, HEAD_PAD - MLA_NOPE), BF16)], axis=2)
    w_kv = jnp.concatenate([w_uk_p.reshape(kvr, nh * HEAD_PAD), w_ukv_f[:, nh * MLA_NOPE:]], axis=1)
    w_o_f = w_o8.reshape(D, D)

    tc, ts1, ts2 = _rope_tables(S)
    g_in, b_in = ln_in_g.reshape(1, D), ln_in_b.reshape(1, D)
    (x0, h, sq, sk, sv, qp, kp, mv, cq, ckv, qn, kvn) = _inproj_fwd(
        x, mod, g_in, b_in, w_in_p, w_uq_p, w_kv, q_norm_g, kv_norm_g, tc, ts1, ts2, dm)
    sb_y, sb_tot = _sb_fwd(sq, sk, sv, dm)
    mla_y, mla_lse = _mla_fwd(qp, kp, mv, dm)
    mix, x1, h2 = _outproj_fwd(sb_y, mla_y, x0, mod, w_o_f, ln1_g, ln1_b, dm)
    u, dr2, bst_c, wst_c = _mlp_fwd(h2, x1, mod, loss_target, w_up8, w_down8, ln2_g, ln2_b, dm)

    du, dffb, dmixb, dx0a, dsb_y, dmla_y, bst_b, wst_b = _mlp_bwd(
        dr2, u, x1, x0, mix, mod, w_up8, w_down8, w_o_f, ln1_g, dm)
    dsq, dsk, dsv = _sb_bwd(sq, sk, sv, sb_tot, dsb_y, dm)
    dqp, dkp, dmv = _mla_bwd(qp, kp, mv, mla_y, mla_lse, dmla_y, dm)
    grad_x, dproj, dqpre, dkvo, bst_a, wst_a = _inproj_bwd(
        x, x0, dx0a, mod, g_in, dsq, dsk, dsv, dqp, dkp, dmv, cq, ckv, w_in_p, w_uq_p, w_kv, q_norm_g, kv_norm_g,
        tc, ts1, ts2, dm)

    T = B * S
    r2 = lambda a: a.reshape(T, a.shape[2])
    g_in_p = _mm_tn(r2(h), r2(dproj), "grad_w_in")
    g_uq_p = _mm_tn(r2(qn), r2(dqpre), "grad_w_uq")
    g_kv = _mm_tn(r2(kvn), r2(dkvo), "grad_w_kv")
    g_o = jnp.concatenate([_mm_tn(r2(sb_y), r2(dmixb), "grad_w_o_sb", out_dtype=BF16),
                           _mm_tn(r2(mla_y), r2(dmixb), "grad_w_o_mla", out_dtype=BF16)], axis=0)
    g_up8 = _mm_tn(r2(h2), r2(du), "grad_w_up", out_dtype=BF16, col_blocks=N_DEV)
    g_down = _mm_tn(r2(u), r2(dffb), "grad_w_down", relu_sq=True, out_dtype=BF16)
    g_uq_f = g_uq_p.reshape(qr, nh, HEAD_PAD)[:, :, :qk].reshape(qr, nh * qk)
    g_uk = g_kv[:, :nh * HEAD_PAD].reshape(kvr, nh, HEAD_PAD)[:, :, :MLA_NOPE].reshape(kvr, nh * MLA_NOPE)
    g_ukv_f = jnp.concatenate([g_uk, g_kv[:, nh * HEAD_PAD:]], axis=1)

    def by_dest_cols(a):
        k, n = a.shape[0], a.shape[1] // N_DEV
        return jnp.swapaxes(a.reshape(k, N_DEV, n), 0, 1).astype(BF16)

    by_dest = [by_dest_cols(g_in_p[:, :din]), by_dest_cols(g_uq_f), by_dest_cols(g_ukv_f),
               g_o.reshape(N_DEV, D // N_DEV, D), g_up8, g_down.reshape(N_DEV, dff // N_DEV, D)]
    chip_sum = _core_scatter_sum([a.reshape((4, 2) + a.shape[1:]) for a in by_dest], "scatter_g_cores")
    names = ["w_in", "w_uq", "w_ukv", "w_o", "w_up", "w_down"]
    quarter = _chip_exchange(chip_sum, "scatter_g_chips", scatter=True)
    moms = [m_w_in, m_w_uq, m_w_ukv, m_w_o, m_w_up, m_w_down]
    vars_ = [v_w_in, v_w_uq, v_w_ukv, v_w_o, v_w_up, v_w_down]
    res = [_reduce_adamw(p, w, m, v, "adamw_" + n) for p, w, m, v, n in zip(quarter, big, moms, vars_, names)]
    gb, db, nmb, nvb = ([r[i] for r in res] for i in range(4))

    dmod = jnp.concatenate([bst_a[:, 1], bst_a[:, 0], bst_b[:, 2], bst_b[:, 1], bst_b[:, 0], bst_c[:, 0]], axis=1)
    small = jnp.concatenate([wst_a[0], wst_a[1], wst_a[4, :qr], wst_a[5, :kvr], wst_b[0], wst_b[1],
                             wst_c[0], wst_c[1], wst_c[2]])
    n1 = small.shape[0]
    both = _all_gather([_pack([dmod, small], F32, LANES)], "gather_small")[0].reshape(N_DEV, -1)
    dmod_all = both[:, :B * N_MOD * D].reshape(N_DEV * B, N_MOD * D)
    sm = both[:, B * N_MOD * D:B * N_MOD * D + n1]
    dmod_my = lax.dynamic_slice(dmod_all, (0, dev * nada), (N_DEV * B, nada))
    row = lambda arrs: jnp.concatenate([a.reshape(1, -1) for a in arrs], axis=1)
    smalls = [ln_in_g, ln_in_b, q_norm_g, kv_norm_g, ln1_g, ln1_b, ln2_g, ln2_b]
    small_shapes = [a.shape for a in smalls]
    (gs, ds, nms, nvs, g_b, d_b, nm_b, nv_b, g_w, d_w, nm_w, nv_w, loss_v) = _finish(
        sm, dmod_all, dmod_my, cact_all, row(smalls),
        row([m_ln_in_g, m_ln_in_b, m_q_norm_g, m_kv_norm_g, m_ln1_g, m_ln1_b, m_ln2_g, m_ln2_b]),
        row([v_ln_in_g, v_ln_in_b, v_q_norm_g, v_kv_norm_g, v_ln1_g, v_ln1_b, v_ln2_g, v_ln2_b]),
        b_ada, m_b_ada, v_b_ada, w_ada[0], m_w_ada[0], v_w_ada[0])
    gsm, dsm, nmsm, nvsm = (_unpack(s, small_shapes) for s in (gs, ds, nms, nvs))

    def ordered(sm_l, w_l, ada_w, ada_b):
        return [sm_l[0], sm_l[1], ada_w[None], ada_b, w_l[0], sm_l[2], sm_l[3], w_l[1], w_l[2], w_l[3],
                sm_l[4], sm_l[5], w_l[4], w_l[5], sm_l[6], sm_l[7]]

    loss = loss_v[0, 0]
    return (loss, grad_x, *ordered(gsm, gb, g_w, g_b), *ordered(dsm, db, d_w, d_b),
            *ordered(nmsm, nmb, nm_w, nm_b), *ordered(nvsm, nvb, nv_w, nv_b))
```

```python
import functools
import math

import jax
import jax.numpy as jnp
from jax import lax
from jax.experimental import pallas as pl
from jax.experimental.pallas import tpu as pltpu

F32 = jnp.float32
BF16 = jnp.bfloat16

SB_HD = 64
MLA_V = 64
MLA_NOPE = 64
MLA_ROPE = 32
HEAD_PAD = 128
CHUNK = 64
ROPE_BASE = 10000.0
LN_EPS = 1e-5
RMS_EPS = 1e-6
DEPTH = 1
ALPHA = (2.0 * DEPTH) ** 0.25
N_MOD = 6
ADAM_LR = 0.001
ADAM_B1 = 0.9
ADAM_B2 = 0.999
ADAM_EPS = 1e-08
ADAM_WD = 0.01
ADAM_STEP = 10
N_DEV = 8
LANES = 128
CUMSUM_W = 256
VMEM_LIMIT = 56 * 1024 * 1024
MESH = pl.DeviceIdType.MESH


def _dot(a, b):
    return jnp.dot(a, b, preferred_element_type=F32)


def _dot_nt(a, b):
    return lax.dot_general(a, b, (((1,), (1,)), ((), ())), preferred_element_type=F32)


def _dot_tn(a, b):
    return lax.dot_general(a, b, (((0,), (0,)), ((), ())), preferred_element_type=F32)


def _cparams(sem):
    return pltpu.CompilerParams(dimension_semantics=sem, vmem_limit_bytes=VMEM_LIMIT)


def _full(a):
    nd = a.ndim
    return pl.BlockSpec(a.shape, lambda *_: (0,) * nd, pipeline_mode=pl.Buffered(1))


def _tok(tm, w):
    return pl.BlockSpec((1, tm, w), lambda b, s: (b, s, 0))


def _perb(rows, w):
    return pl.BlockSpec((1, rows, w), lambda b, s: (b, 0, 0))


def _sds(shape, dtype):
    return jax.ShapeDtypeStruct(shape, dtype)


def _ln_fwd(x, g, b):
    mu = jnp.mean(x, axis=-1, keepdims=True)
    xc = x - mu
    var = jnp.mean(xc * xc, axis=-1, keepdims=True)
    rstd = lax.rsqrt(var + LN_EPS)
    xhat = xc * rstd
    return xhat * g + b, xhat, rstd


def _ln_bwd(dy, xhat, rstd, g):
    dxh = dy * g
    m1 = jnp.mean(dxh, axis=-1, keepdims=True)
    m2 = jnp.mean(dxh * xhat, axis=-1, keepdims=True)
    return rstd * (dxh - m1 - xhat * m2)


def _colsum(a):
    return jnp.sum(a, axis=0, keepdims=True)


def _rope(x, c, s1, s2):
    w = x.shape[-1]
    return x * c + pltpu.roll(x, w - 16, 1) * s1 + pltpu.roll(x, 16, 1) * s2


def _rope_t(x, c, s1, s2):
    w = x.shape[-1]
    return x * c - pltpu.roll(x, w - 16, 1) * s1 - pltpu.roll(x, 16, 1) * s2


def _adamw(w, g, m, v):
    m = ADAM_B1 * m + (1.0 - ADAM_B1) * g
    v = ADAM_B2 * v + (1.0 - ADAM_B2) * (g * g)
    m_hat = m / (1.0 - ADAM_B1 ** ADAM_STEP)
    v_hat = v / (1.0 - ADAM_B2 ** ADAM_STEP)
    delta = -ADAM_LR * (m_hat / (jnp.sqrt(v_hat) + ADAM_EPS) + ADAM_WD * w)
    return delta, m, v


def _my_place():
    return lax.axis_index("x"), lax.axis_index("y"), lax.axis_index("c")


def _chip_peers(mx, my):
    out = []
    for j in (1, 2, 3):
        px = 1 - mx if (j >> 1) else mx
        py = 1 - my if (j & 1) else my
        out.append((px, py, 2 * px + py))
    return out


def _hbm_call(body, name, n_in, out_shape, sems):
    hbm = pl.BlockSpec(memory_space=pl.ANY)
    return pl.pallas_call(
        body, name=name, out_shape=out_shape,
        in_specs=[hbm] * n_in, out_specs=[hbm] * len(out_shape),
        scratch_shapes=[pltpu.SemaphoreType.DMA(s) for s in sems])


def _chip_exchange(xs, name, scatter):
    n = len(xs)

    def body(*refs):
        x_refs, o_refs = refs[:n], refs[n:2 * n]
        ssem, rsem, lsem = refs[2 * n:]
        mx, my, mc = _my_place()
        me = 2 * mx + my
        peers = _chip_peers(mx, my)

        def copy(i, j, src_slot, dst_slot):
            px, py, _ = peers[j]
            return pltpu.make_async_remote_copy(
                src_ref=x_refs[i].at[src_slot] if scatter else x_refs[i], dst_ref=o_refs[i].at[dst_slot],
                send_sem=ssem.at[i, j], recv_sem=rsem.at[i, j], device_id=(px, py, mc), device_id_type=MESH)

        local = [pltpu.make_async_copy(x_refs[i].at[me] if scatter else x_refs[i], o_refs[i].at[me], lsem.at[i])
                 for i in range(n)]
        sends = [copy(i, j, peers[j][2], me) for i in range(n) for j in range(3)]
        for cp in local + sends:
            cp.start()
        for i in range(n):
            for j in range(3):
                copy(i, j, peers[j][2], peers[j][2]).wait_recv()
        for cp in sends:
            cp.wait_send()
        for cp in local:
            cp.wait()

    out_shape = [_sds((4,) + tuple(x.shape[1:] if scatter else x.shape), x.dtype) for x in xs]
    return _hbm_call(body, name, n, out_shape, [(n, 3), (n, 3), (n,)])(*xs)


def _chip_exchange_start(xs, name, scatter):
    n = len(xs)
    blks = [tuple(x.shape[1:] if scatter else x.shape) for x in xs]

    def body(*refs):
        x_refs, land_refs = refs[:n], refs[n:2 * n]
        ssem, rsem = refs[2 * n], refs[2 * n + 1]
        token = refs[-1]
        mx, my, mc = _my_place()
        me = 2 * mx + my
        for i in range(n):
            for j, (px, py, pk) in enumerate(_chip_peers(mx, my)):
                pltpu.make_async_remote_copy(
                    src_ref=x_refs[i].at[pk] if scatter else x_refs[i], dst_ref=land_refs[i].at[me],
                    send_sem=ssem.at[3 * i + j], recv_sem=rsem.at[3 * i + j], device_id=(px, py, mc),
                    device_id_type=MESH).start()
        token[...] = jnp.zeros_like(token)

    hbm = pl.BlockSpec(memory_space=pltpu.HBM)
    sem = pl.BlockSpec(memory_space=pltpu.SEMAPHORE)
    lands = [lax.empty((4,) + b, x.dtype) for b, x in zip(blks, xs)]
    res = pl.pallas_call(
        body, name=name,
        out_shape=[pltpu.SemaphoreType.DMA((3 * n,)), pltpu.SemaphoreType.DMA((3 * n,))]
        + [pltpu.HBM(x.shape, x.dtype) for x in xs] + [pltpu.HBM(l.shape, l.dtype) for l in lands]
        + [_sds((8, LANES), F32)],
        in_specs=[hbm] * (2 * n),
        out_specs=[sem, sem] + [hbm] * (2 * n) + [pl.BlockSpec(memory_space=pltpu.VMEM)],
        input_output_aliases={i: 2 + i for i in range(2 * n)},
        compiler_params=pltpu.CompilerParams(has_side_effects=pltpu.SideEffectType.DATAFLOW_SIDE_EFFECTING),
    )(*[pltpu.with_memory_space_constraint(a, pltpu.HBM) for a in list(xs) + lands])
    return dict(ssem=res[0], rsem=res[1], xs=res[2:2 + n], lands=res[2 + n:2 + 2 * n], n=n, scatter=scatter), res[-1]


def _chip_exchange_wait(handle, after, name):
    n, scatter = handle["n"], handle["scatter"]

    def body(*refs):
        x_refs, land_refs = refs[:n], refs[n:2 * n]
        ssem, rsem = refs[2 * n], refs[2 * n + 1]
        mx, my, mc = _my_place()
        for i in range(n):
            for j, (px, py, pk) in enumerate(_chip_peers(mx, my)):
                cp = pltpu.make_async_remote_copy(
                    src_ref=x_refs[i].at[pk] if scatter else x_refs[i], dst_ref=land_refs[i].at[pk],
                    send_sem=ssem.at[3 * i + j], recv_sem=rsem.at[3 * i + j], device_id=(px, py, mc),
                    device_id_type=MESH)
                cp.wait_send()
                cp.wait_recv()

    hbm = pl.BlockSpec(memory_space=pltpu.HBM)
    sem = pl.BlockSpec(memory_space=pltpu.SEMAPHORE)
    ops = list(handle["xs"]) + list(handle["lands"])
    res = pl.pallas_call(
        body, name=name,
        out_shape=[pltpu.HBM(a.shape, a.dtype) for a in ops],
        in_specs=[hbm] * (2 * n) + [sem, sem, pl.BlockSpec(memory_space=pl.ANY)],
        out_specs=[hbm] * (2 * n),
        input_output_aliases={i: i for i in range(2 * n)},
        compiler_params=pltpu.CompilerParams(has_side_effects=pltpu.SideEffectType.DATAFLOW_SIDE_EFFECTING),
    )(*ops, handle["ssem"], handle["rsem"], after)
    me = 2 * lax.axis_index("x") + lax.axis_index("y")
    out = []
    for x, land in zip(res[:n], res[n:]):
        own = lax.dynamic_index_in_dim(x, me, 0, keepdims=False) if scatter else x
        out.append(lax.dynamic_update_index_in_dim(land, own, me, 0))
    return out


def _core_gather(xs, name):
    n = len(xs)

    def body(*refs):
        x_refs, o_refs, mine, got = refs[:n], refs[n:2 * n], refs[2 * n:3 * n], refs[3 * n:4 * n]
        lsem, ssem, rsem, osem = refs[4 * n:]
        mx, my, mc = _my_place()
        loads = [pltpu.make_async_copy(x_refs[i], mine[i], lsem.at[i]) for i in range(n)]
        for cp in loads:
            cp.start()
        sends, stores = [], []
        for i in range(n):
            loads[i].wait()
            cp = pltpu.make_async_remote_copy(
                src_ref=mine[i], dst_ref=got[i], send_sem=ssem.at[i], recv_sem=rsem.at[i],
                device_id=(mx, my, 1 - mc), device_id_type=MESH)
            cp.start()
            sends.append(cp)
            for k in range(4):
                st = pltpu.make_async_copy(mine[i].at[k], o_refs[i].at[k, mc], osem.at[i, k])
                st.start()
                stores.append(st)
        for i in range(n):
            sends[i].wait_recv()
            for k in range(4):
                st = pltpu.make_async_copy(got[i].at[k], o_refs[i].at[k, 1 - mc], osem.at[n + i, k])
                st.start()
                stores.append(st)
        for cp in sends:
            cp.wait_send()
        for st in stores:
            st.wait()

    hbm = pl.BlockSpec(memory_space=pl.ANY)
    bufs = [pltpu.VMEM(x.shape, x.dtype) for x in xs]
    return pl.pallas_call(
        body, name=name,
        out_shape=[_sds((4, 2) + tuple(x.shape[1:]), x.dtype) for x in xs],
        in_specs=[hbm] * n, out_specs=[hbm] * n,
        scratch_shapes=bufs + bufs + [pltpu.SemaphoreType.DMA((n,)), pltpu.SemaphoreType.DMA((n,)),
                                      pltpu.SemaphoreType.DMA((n,)), pltpu.SemaphoreType.DMA((2 * n, 4))],
        compiler_params=pltpu.CompilerParams(vmem_limit_bytes=VMEM_LIMIT),
    )(*xs)


def _rows_step(k):
    for r in (256, 128, 64, 32, 16, 8):
        if k % r == 0:
            return r
    return k


def _core_scatter_sum(gs, name):
    n = len(gs)

    def body(*refs):
        g_refs, o_refs = refs[:n], refs[n:2 * n]
        send, got, mine = refs[2 * n:3 * n], refs[3 * n:4 * n], refs[4 * n:5 * n]
        lsem, msem, ssem, rsem, osem = refs[5 * n:]
        mx, my, mc = _my_place()
        pairs = [(i, k) for i in range(n) for k in range(4)]
        out_loads = {(i, k): pltpu.make_async_copy(g_refs[i].at[k, 1 - mc], send[i].at[k], lsem.at[i, k])
                     for i, k in pairs}
        own_loads = {(i, k): pltpu.make_async_copy(g_refs[i].at[k, mc], mine[i].at[k], msem.at[i, k])
                     for i, k in pairs}
        for p in pairs:
            out_loads[p].start()
        for p in pairs:
            own_loads[p].start()
        sends = []
        for i in range(n):
            for k in range(4):
                out_loads[i, k].wait()
            cp = pltpu.make_async_remote_copy(
                src_ref=send[i], dst_ref=got[i], send_sem=ssem.at[i], recv_sem=rsem.at[i],
                device_id=(mx, my, 1 - mc), device_id_type=MESH)
            cp.start()
            sends.append(cp)
        stores = []
        for i in range(n):
            for k in range(4):
                own_loads[i, k].wait()
            sends[i].wait_recv()
            rows = g_refs[i].shape[2]
            step = _rows_step(rows)

            def add(r, _, i=i, step=step):
                sl = pl.ds(pl.multiple_of(r * step, step), step)
                for k in range(4):
                    mine[i][k, sl, :] = (mine[i][k, sl, :].astype(F32) + got[i][k, sl, :].astype(F32)).astype(BF16)
                return 0

            lax.fori_loop(0, rows // step, add, 0)
            st = pltpu.make_async_copy(mine[i], o_refs[i], osem.at[i])
            st.start()
            stores.append(st)
        for cp in sends:
            cp.wait_send()
        for st in stores:
            st.wait()

    hbm = pl.BlockSpec(memory_space=pl.ANY)
    blk = [(4,) + tuple(g.shape[2:]) for g in gs]
    bufs = [pltpu.VMEM(b, BF16) for b in blk]
    return pl.pallas_call(
        body, name=name,
        out_shape=[_sds(b, BF16) for b in blk],
        in_specs=[hbm] * n, out_specs=[hbm] * n,
        scratch_shapes=bufs * 3 + [pltpu.SemaphoreType.DMA((n, 4)), pltpu.SemaphoreType.DMA((n, 4)),
                                   pltpu.SemaphoreType.DMA((n,)), pltpu.SemaphoreType.DMA((n,)),
                                   pltpu.SemaphoreType.DMA((n,))],
        compiler_params=pltpu.CompilerParams(vmem_limit_bytes=VMEM_LIMIT),
    )(*gs)


def _all_gather(xs, name):
    by_chip = _chip_exchange(xs, name + "_chips", scatter=False)
    both = _core_gather(by_chip, name + "_cores")
    return [b.reshape((N_DEV,) + tuple(x.shape)) for b, x in zip(both, xs)]


def _ada_partial(c_all, w_ada_loc, b_loc):
    def body(c_ref, w_ref, b_ref, act_ref, mod_ref):
        c = c_ref[...]
        act = c * (1.0 / (1.0 + jnp.exp(-c)))
        act_ref[...] = act
        mod_ref[...] = _dot(act.astype(BF16), w_ref[...].astype(BF16)) + b_ref[...]

    nb, d = c_all.shape
    return pl.pallas_call(
        body, name="ada_partial",
        out_shape=(_sds((nb, d), F32), _sds((nb, w_ada_loc.shape[1]), F32)),
        compiler_params=pltpu.CompilerParams(vmem_limit_bytes=VMEM_LIMIT),
    )(c_all, w_ada_loc, b_loc)


_AFTER = pl.BlockSpec(memory_space=pl.ANY)


def _inproj_fwd(x, mod, ln_g, ln_b, w_in_p, w_uq_p, w_kv, gq, gkv, tc, ts1, ts2, dm, after):
    B, S, D = x.shape
    tm = dm["tm"]
    sbw, qr, kvr, nh = dm["sbw"], dm["qr"], dm["kvr"], dm["nh"]
    o_cq, o_ckv, o_kr = 3 * sbw, 3 * sbw + qr, 3 * sbw + qr + kvr
    qpw = nh * HEAD_PAD

    def body(x_ref, mod_ref, g_ref, b_ref, win_ref, wuq_ref, wkv_ref, gq_ref, gkv_ref, tc_ref, ts1_ref, ts2_ref, _,
             x0_ref, h_ref, q_ref, k_ref, v_ref, qp_ref, kp_ref, mv_ref, cq_ref, ckv_ref, qn_ref, kvn_ref):
        x0, _, _ = _ln_fwd(x_ref[0], g_ref[...], b_ref[...])
        x0_ref[0] = x0
        mod = mod_ref[0]
        h = (x0 * (1.0 + mod[1:2]) + mod[0:1]).astype(BF16)
        h_ref[0] = h
        proj = _dot(h, win_ref[...])
        q_ref[0] = (proj[:, 0:sbw] * (SB_HD ** -0.5)).astype(BF16)
        k_ref[0] = proj[:, sbw:2 * sbw].astype(BF16)
        v_ref[0] = proj[:, 2 * sbw:3 * sbw].astype(BF16)
        cq = proj[:, o_cq:o_cq + qr]
        ckv = proj[:, o_ckv:o_ckv + kvr]
        cq_ref[0] = cq
        ckv_ref[0] = ckv
        qn = (cq * lax.rsqrt(jnp.mean(cq * cq, axis=-1, keepdims=True) + RMS_EPS) * gq_ref[...]).astype(BF16)
        kvn = (ckv * lax.rsqrt(jnp.mean(ckv * ckv, axis=-1, keepdims=True) + RMS_EPS) * gkv_ref[...]).astype(BF16)
        qn_ref[0] = qn
        kvn_ref[0] = kvn
        c1, s1, s2 = tc_ref[...], ts1_ref[...], ts2_ref[...]
        c8, s18, s28 = jnp.tile(c1, (1, nh)), jnp.tile(s1, (1, nh)), jnp.tile(s2, (1, nh))
        qp_ref[0] = _rope(_dot(qn, wuq_ref[...]), c8, s18, s28).astype(BF16)
        kvo = _dot(kvn, wkv_ref[...])
        kr = pltpu.roll(proj[:, o_kr:o_kr + LANES], 64, 1)
        kr = _rope(kr, c1, s1, s2)
        kp_ref[0] = (kvo[:, 0:qpw] + jnp.tile(kr, (1, nh))).astype(BF16)
        mv_ref[0] = kvo[:, qpw:].astype(BF16)

    tab = pl.BlockSpec((tm, LANES), lambda b, s: (s, 0))
    outs = [(D, F32), (D, BF16), (sbw, BF16), (sbw, BF16), (sbw, BF16), (qpw, BF16), (qpw, BF16),
            (nh * MLA_V, BF16), (qr, F32), (kvr, F32), (qr, BF16), (kvr, BF16)]
    return pl.pallas_call(
        body, name="inproj_fwd", grid=(B, S // tm),
        in_specs=[_tok(tm, D), _perb(N_MOD, D), _full(ln_g), _full(ln_b), _full(w_in_p), _full(w_uq_p),
                  _full(w_kv), _full(gq), _full(gkv), tab, tab, tab, _AFTER],
        out_specs=[_tok(tm, w) for w, _ in outs],
        out_shape=[_sds((B, S, w), t) for w, t in outs],
        compiler_params=_cparams(("parallel", "parallel")),
    )(x, mod, ln_g, ln_b, w_in_p, w_uq_p, w_kv, gq, gkv, tc, ts1, ts2, after)


def _softplus_parts(z):
    e = jnp.exp(-jnp.abs(z))
    a = -(jnp.maximum(z, 0.0) + jnp.log(1.0 + e))
    return a, e


def _split_dot(a, u):
    hi = a.astype(BF16)
    lo = (a - hi.astype(F32)).astype(BF16)
    return _dot(hi, u) + _dot(lo, u)


def _tri(n, rel):
    row = lax.broadcasted_iota(jnp.int32, (n, n), 0)
    col = lax.broadcasted_iota(jnp.int32, (n, n), 1)
    return rel(row, col).astype(BF16)


def _running_sum(a, tri, reverse, split):
    cs = tri.shape[0]
    n = a.shape[1] // cs
    out = [None] * n
    run = None
    for c in (reversed(range(n)) if reverse else range(n)):
        part = a[:, c * cs:(c + 1) * cs]
        loc = _split_dot(part, tri) if split else _dot(part.astype(BF16), tri)
        out[c] = loc if run is None else loc + run
        tot = jnp.sum(part, axis=1, keepdims=True)
        run = tot if run is None else run + tot
    return (out[0] if n == 1 else jnp.concatenate(out, axis=1)), run


def _sb_fwd(q, k, v, dm):
    B, S, W = q.shape
    tq = dm["tq"]
    nq = S // tq

    def body(q_ref, k_ref, v_ref, y_ref, tot_ref):
        qi = pl.program_id(2)
        q2 = q_ref[0]
        lane = lax.broadcasted_iota(jnp.int32, (tq, LANES), 1)
        qh = [jnp.where(lane < SB_HD, q2, 0).astype(BF16), jnp.where(lane >= SB_HD, q2, 0).astype(BF16)]
        row = lax.broadcasted_iota(jnp.int32, (tq, tq), 0)
        col = lax.broadcasted_iota(jnp.int32, (tq, tq), 1)
        later = _tri(min(tq, CUMSUM_W), lambda j, s: j > s)
        strict = col < row

        def block(j, carry, masked):
            off = pl.multiple_of(j * tq, tq)
            k2 = k_ref[0, pl.ds(off, tq), :]
            v2 = v_ref[0, pl.ds(off, tq), :]
            new = []
            for h in range(2):
                acc, run = carry[2 * h], carry[2 * h + 1]
                z = _dot_nt(qh[h], k2)
                a, _ = _softplus_parts(z)
                if masked:
                    a = jnp.where(strict, a, 0.0)
                a_later, a_tot = _running_sum(a, later, reverse=True, split=True)
                w = jnp.exp(z + a + a_later + run)
                if masked:
                    w = jnp.where(strict, w, 0.0)
                new.append(acc + _dot(w.astype(BF16), v2))
                new.append(run + a_tot)
            return tuple(new)

        zero = jnp.zeros((tq, LANES), F32)
        zrun = jnp.zeros((tq, 1), F32)
        carry = block(qi, (zero, zrun, zero, zrun), True)
        carry = lax.fori_loop(0, qi, lambda jj, c: block(qi - 1 - jj, c, False), carry)
        y_ref[0] = jnp.where(lane < SB_HD, carry[0], carry[2]).astype(BF16)
        tot_ref[0] = jnp.where(lane < SB_HD, carry[1], carry[3])

    qspec = pl.BlockSpec((1, tq, LANES), lambda b, hp, i: (b, i, hp))
    kspec = pl.BlockSpec((1, S, LANES), lambda b, hp, i: (b, 0, hp))
    return pl.pallas_call(
        body, name="sb_fwd", grid=(B, W // LANES, nq),
        in_specs=[qspec, kspec, kspec],
        out_specs=[qspec, qspec],
        out_shape=[_sds((B, S, W), BF16), _sds((B, S, W), F32)],
        compiler_params=_cparams(("parallel", "parallel", "arbitrary")),
    )(q, k, v)


def _sb_bwd(q, k, v, tot, dy, dm, after):
    B, S, W = q.shape
    tq = dm["tq"]
    nq = S // tq

    def body(q_ref, k_ref, v_ref, tot_ref, dy_ref, _, dq_ref, dk_ref, dv_ref, dk_acc, dv_acc):
        qi = pl.program_id(2)

        @pl.when(qi == 0)
        def _():
            dk_acc[...] = jnp.zeros_like(dk_acc)
            dv_acc[...] = jnp.zeros_like(dv_acc)

        q2 = q_ref[0]
        dy2 = dy_ref[0]
        tot2 = tot_ref[0]
        lane = lax.broadcasted_iota(jnp.int32, (tq, LANES), 1)
        in_h = [lane < SB_HD, lane >= SB_HD]
        qh = [jnp.where(m, q2, 0).astype(BF16) for m in in_h]
        dyh = [jnp.where(m, dy2, 0).astype(BF16) for m in in_h]
        toth = [tot2[:, 0:1], tot2[:, SB_HD:SB_HD + 1]]
        row = lax.broadcasted_iota(jnp.int32, (tq, tq), 0)
        col = lax.broadcasted_iota(jnp.int32, (tq, tq), 1)
        upto = _tri(min(tq, CUMSUM_W), lambda j, s: j <= s)
        before = _tri(min(tq, CUMSUM_W), lambda s, j: s < j)
        strict = col < row

        def block(j, carry, masked):
            off = pl.multiple_of(j * tq, tq)
            k2 = k_ref[0, pl.ds(off, tq), :]
            v2 = v_ref[0, pl.ds(off, tq), :]
            new = []
            dk_blk = jnp.zeros((tq, LANES), F32)
            dv_blk = jnp.zeros((tq, LANES), F32)
            for h in range(2):
                dq, pa, pg = carry[3 * h], carry[3 * h + 1], carry[3 * h + 2]
                z = _dot_nt(qh[h], k2)
                a, e = _softplus_parts(z)
                if masked:
                    a = jnp.where(strict, a, 0.0)
                a_upto, a_tot = _running_sum(a, upto, reverse=False, split=True)
                w = jnp.exp(z + a + (toth[h] - pa - a_upto))
                if masked:
                    w = jnp.where(strict, w, 0.0)
                g = _dot_nt(dyh[h], v2) * w
                g_before, g_tot = _running_sum(g, before, reverse=False, split=False)
                g_before = g_before + pg
                r = 1.0 / (1.0 + e)
                er = e * r
                pos = z >= 0.0
                sig = jnp.where(pos, r, er)
                dz = g * jnp.where(pos, er, r) - g_before * sig
                if masked:
                    dz = jnp.where(strict, dz, 0.0)
                dzb = dz.astype(BF16)
                dv_blk = dv_blk + _dot_tn(w.astype(BF16), dyh[h])
                dk_blk = dk_blk + _dot_tn(dzb, qh[h])
                new += [dq + _dot(dzb, k2), pa + a_tot, pg + g_tot]
            dk_acc[pl.ds(off, tq), :] += dk_blk
            dv_acc[pl.ds(off, tq), :] += dv_blk
            return tuple(new)

        zero = jnp.zeros((tq, LANES), F32)
        zrun = jnp.zeros((tq, 1), F32)
        carry = lax.fori_loop(0, qi, lambda j, c: block(j, c, False), (zero, zrun, zrun, zero, zrun, zrun))
        carry = block(qi, carry, True)
        dq_ref[0] = (jnp.where(in_h[0], carry[0], carry[3]) * (SB_HD ** -0.5)).astype(BF16)

        @pl.when(qi == nq - 1)
        def _():
            dk_ref[0] = dk_acc[...].astype(BF16)
            dv_ref[0] = dv_acc[...].astype(BF16)

    qspec = pl.BlockSpec((1, tq, LANES), lambda b, hp, i: (b, i, hp))
    kspec = pl.BlockSpec((1, S, LANES), lambda b, hp, i: (b, 0, hp))
    return pl.pallas_call(
        body, name="sb_bwd", grid=(B, W // LANES, nq),
        in_specs=[qspec, kspec, kspec, qspec, qspec, _AFTER],
        out_specs=[qspec, kspec, kspec],
        out_shape=[_sds((B, S, W), BF16)] * 3,
        scratch_shapes=[pltpu.VMEM((S, LANES), F32), pltpu.VMEM((S, LANES), F32)],
        compiler_params=_cparams(("parallel", "parallel", "arbitrary")),
    )(q, k, v, tot, dy, after)


def _chunk_mask(tq):
    row = lax.broadcasted_iota(jnp.int32, (tq, tq), 0)
    col = lax.broadcasted_iota(jnp.int32, (tq, tq), 1)
    return lax.shift_right_logical(col, 6) <= lax.shift_right_logical(row, 6)


def _mla_fwd(qp, kp, mv, dm, after):
    B, S, QW = qp.shape
    VW = mv.shape[2]
    tq = dm["tq"]
    nq = S // tq
    scale = (MLA_NOPE + MLA_ROPE) ** -0.5
    assert CHUNK == 64

    def body(q_ref, k_ref, v_ref, _, y_ref, lse_ref):
        qi = pl.program_id(2)
        q2 = q_ref[0]
        lane = lax.broadcasted_iota(jnp.int32, (tq, LANES), 1)
        allowed = _chunk_mask(tq)

        def block(j, carry, masked):
            off = pl.multiple_of(j * tq, tq)
            v2 = v_ref[0, pl.ds(off, tq), :]
            new = []
            for h in range(2):
                acc, m, l = carry[3 * h], carry[3 * h + 1], carry[3 * h + 2]
                kh = k_ref[0, pl.ds(off, tq), h * HEAD_PAD:(h + 1) * HEAD_PAD]
                s = _dot_nt(q2[:, h * HEAD_PAD:(h + 1) * HEAD_PAD], kh) * scale
                if masked:
                    s = jnp.where(allowed, s, -1e30)
                m_new = jnp.maximum(m, jnp.max(s, axis=1, keepdims=True))
                alpha = jnp.exp(m - m_new)
                p = jnp.exp(s - m_new)
                new += [alpha * acc + _dot(p.astype(BF16), v2), m_new,
                        alpha * l + jnp.sum(p, axis=1, keepdims=True)]
            return tuple(new)

        zero = jnp.zeros((tq, LANES), F32)
        m0 = jnp.full((tq, 1), -1e30, F32)
        l0 = jnp.zeros((tq, 1), F32)
        carry = block(qi, (zero, m0, l0, zero, m0, l0), True)
        carry = lax.fori_loop(0, qi, lambda j, c: block(j, c, False), carry)
        y0 = carry[0] / carry[2]
        y1 = carry[3] / carry[5]
        y_ref[0] = jnp.where(lane < MLA_V, y0, y1).astype(BF16)
        lse_ref[0] = jnp.where(lane < MLA_V, carry[1] + jnp.log(carry[2]), carry[4] + jnp.log(carry[5]))

    qspec = pl.BlockSpec((1, tq, 2 * HEAD_PAD), lambda b, hp, i: (b, i, hp))
    kspec = pl.BlockSpec((1, S, 2 * HEAD_PAD), lambda b, hp, i: (b, 0, hp))
    vspec = pl.BlockSpec((1, S, LANES), lambda b, hp, i: (b, 0, hp))
    yspec = pl.BlockSpec((1, tq, LANES), lambda b, hp, i: (b, i, hp))
    return pl.pallas_call(
        body, name="mla_fwd", grid=(B, VW // LANES, nq),
        in_specs=[qspec, kspec, vspec, _AFTER],
        out_specs=[yspec, yspec],
        out_shape=[_sds((B, S, VW), BF16), _sds((B, S, VW), F32)],
        compiler_params=_cparams(("parallel", "parallel", "arbitrary")),
    )(qp, kp, mv, after)


def _mla_bwd(qp, kp, mv, y, lse, dy, dm, after):
    B, S, QW = qp.shape
    VW = mv.shape[2]
    tq = dm["tq"]
    nq = S // tq
    scale = (MLA_NOPE + MLA_ROPE) ** -0.5

    def body(q_ref, k_ref, v_ref, y_ref, lse_ref, dy_ref, _, dq_ref, dk_ref, dv_ref, dk_acc, dv_acc):
        qi = pl.program_id(2)

        @pl.when(qi == 0)
        def _():
            dk_acc[...] = jnp.zeros_like(dk_acc)
            dv_acc[...] = jnp.zeros_like(dv_acc)

        q2 = q_ref[0]
        dy2 = dy_ref[0]
        lse2 = lse_ref[0]
        lane = lax.broadcasted_iota(jnp.int32, (tq, LANES), 1)
        in_h = [lane < MLA_V, lane >= MLA_V]
        prod = dy2.astype(F32) * y_ref[0].astype(F32)
        delta = [jnp.sum(jnp.where(m, prod, 0.0), axis=1, keepdims=True) for m in in_h]
        dyh = [jnp.where(m, dy2, 0).astype(BF16) for m in in_h]
        lseh = [lse2[:, 0:1], lse2[:, MLA_V:MLA_V + 1]]
        allowed = _chunk_mask(tq)

        def block(j, carry, masked):
            off = pl.multiple_of(j * tq, tq)
            v2 = v_ref[0, pl.ds(off, tq), :]
            new = []
            dv_blk = jnp.zeros((tq, LANES), F32)
            for h in range(2):
                sl = slice(h * HEAD_PAD, (h + 1) * HEAD_PAD)
                qhh = q2[:, sl]
                kh = k_ref[0, pl.ds(off, tq), sl]
                s = _dot_nt(qhh, kh) * scale
                if masked:
                    s = jnp.where(allowed, s, -1e30)
                p = jnp.exp(s - lseh[h])
                ds = (p * (_dot_nt(dyh[h], v2) - delta[h]) * scale).astype(BF16)
                dv_blk = dv_blk + _dot_tn(p.astype(BF16), dyh[h])
                dk_acc[pl.ds(off, tq), sl] += _dot_tn(ds, qhh)
                new.append(carry[h] + _dot(ds, kh))
            dv_acc[pl.ds(off, tq), :] += dv_blk
            return tuple(new)

        zero = jnp.zeros((tq, HEAD_PAD), F32)
        carry = lax.fori_loop(0, qi, lambda j, c: block(j, c, False), (zero, zero))
        carry = block(qi, carry, True)
        dq_ref[0] = jnp.concatenate([carry[0], carry[1]], axis=1).astype(BF16)

        @pl.when(qi == nq - 1)
        def _():
            dk_ref[0] = dk_acc[...].astype(BF16)
            dv_ref[0] = dv_acc[...].astype(BF16)

    qspec = pl.BlockSpec((1, tq, 2 * HEAD_PAD), lambda b, hp, i: (b, i, hp))
    kspec = pl.BlockSpec((1, S, 2 * HEAD_PAD), lambda b, hp, i: (b, 0, hp))
    vspec = pl.BlockSpec((1, S, LANES), lambda b, hp, i: (b, 0, hp))
    yspec = pl.BlockSpec((1, tq, LANES), lambda b, hp, i: (b, i, hp))
    return pl.pallas_call(
        body, name="mla_bwd", grid=(B, VW // LANES, nq),
        in_specs=[qspec, kspec, vspec, yspec, yspec, yspec, _AFTER],
        out_specs=[qspec, kspec, vspec],
        out_shape=[_sds((B, S, QW), BF16), _sds((B, S, QW), BF16), _sds((B, S, VW), BF16)],
        scratch_shapes=[pltpu.VMEM((S, 2 * HEAD_PAD), F32), pltpu.VMEM((S, LANES), F32)],
        compiler_params=_cparams(("parallel", "parallel", "arbitrary")),
    )(qp, kp, mv, y, lse, dy, after)


def _outproj_fwd(sb_y, mla_y, x0, mod, w_o, ln_g, ln_b, dm):
    B, S, D = x0.shape
    tm = dm["tm"]
    sbw = sb_y.shape[2]

    def body(ya_ref, yb_ref, x0_ref, mod_ref, wo_ref, g_ref, b_ref, mix_ref, x1_ref, h2_ref):
        mod = mod_ref[0]
        mix = _dot(ya_ref[0], wo_ref[0:sbw, :]) + _dot(yb_ref[0], wo_ref[sbw:, :])
        mix_ref[0] = mix
        x1, _, _ = _ln_fwd(ALPHA * x0_ref[0] + (1.0 + mod[2:3]) * mix, g_ref[...], b_ref[...])
        x1_ref[0] = x1
        h2_ref[0] = (x1 * (1.0 + mod[4:5]) + mod[3:4]).astype(BF16)

    return pl.pallas_call(
        body, name="outproj_fwd", grid=(B, S // tm),
        in_specs=[_tok(tm, sbw), _tok(tm, mla_y.shape[2]), _tok(tm, D), _perb(N_MOD, D),
                  _full(w_o), _full(ln_g), _full(ln_b)],
        out_specs=[_tok(tm, D)] * 3,
        out_shape=[_sds((B, S, D), F32), _sds((B, S, D), F32), _sds((B, S, D), BF16)],
        compiler_params=_cparams(("parallel", "parallel")),
    )(sb_y, mla_y, x0, mod, w_o, ln_g, ln_b)


def _stat_specs(B, D):
    specs = [pl.BlockSpec((1, 8, D), lambda b, s: (b, 0, 0)), pl.BlockSpec((8, D), lambda b, s: (0, 0))]
    shapes = [_sds((B, 8, D), F32), _sds((8, D), F32)]
    return specs, shapes


def _stat_init(bst_ref, wst_ref):
    @pl.when(pl.program_id(1) == 0)
    def _():
        bst_ref[...] = jnp.zeros_like(bst_ref)

    @pl.when((pl.program_id(0) == 0) & (pl.program_id(1) == 0))
    def _():
        wst_ref[...] = jnp.zeros_like(wst_ref)


def _mlp_fwd(h2, x1, mod, target, w_up, w_down, ln_g, ln_b, dm):
    B, S, D = x1.shape
    tm = dm["tm"]
    nck, _, ck = w_up.shape
    dff = nck * ck

    def body(h2_ref, x1_ref, mod_ref, t_ref, wu_ref, wd_ref, g_ref, b_ref, u_ref, dr_ref, bst_ref, wst_ref):
        _stat_init(bst_ref, wst_ref)
        mod = mod_ref[0]
        h2 = h2_ref[0]
        ff = jnp.zeros((tm, D), F32)
        for c in range(nck):
            u = _dot(h2, wu_ref[c])
            u_ref[0, :, c * ck:(c + 1) * ck] = u.astype(BF16)
            act = jnp.square(jnp.maximum(u, 0.0)).astype(BF16)
            ff = ff + _dot(act, wd_ref[c])
        g = g_ref[...]
        x2, xhat, rstd = _ln_fwd(ALPHA * x1_ref[0] + (1.0 + mod[5:6]) * ff, g, b_ref[...])
        err = x2 - t_ref[0]
        dy = err * (1.0 / D)
        dr = _ln_bwd(dy, xhat, rstd, g)
        dr_ref[0] = dr
        bst_ref[0, 0:1, :] += _colsum(dr * ff)
        wst_ref[0:1, :] += _colsum(dy * xhat)
        wst_ref[1:2, :] += _colsum(dy)
        wst_ref[2:3, :] += _colsum(err * err) * (0.5 / D)

    sspecs, sshapes = _stat_specs(B, D)
    return pl.pallas_call(
        body, name="mlp_fwd", grid=(B, S // tm),
        in_specs=[_tok(tm, D), _tok(tm, D), _perb(N_MOD, D), _tok(tm, D), _full(w_up), _full(w_down),
                  _full(ln_g), _full(ln_b)],
        out_specs=[_tok(tm, dff), _tok(tm, D)] + sspecs,
        out_shape=[_sds((B, S, dff), BF16), _sds((B, S, D), F32)] + sshapes,
        compiler_params=_cparams(("arbitrary", "arbitrary")),
    )(h2, x1, mod, target, w_up, w_down, ln_g, ln_b)


def _mlp_bwd(dr2, u, x1, x0, mix, mod, w_up, w_down, w_o, ln_g, dm):
    B, S, D = x1.shape
    tm = dm["tm_small"]
    sbw = dm["sbw"]
    nck, _, ck = w_up.shape
    dff = nck * ck

    def body(dr_ref, u_ref, x1_ref, x0_ref, mix_ref, mod_ref, wu_ref, wd_ref, wo_ref, g_ref,
             du_ref, dff_ref, dmix_ref, dx0_ref, dya_ref, dyb_ref, bst_ref, wst_ref):
        _stat_init(bst_ref, wst_ref)
        mod = mod_ref[0]
        dr2 = dr_ref[0]
        dffv = ((1.0 + mod[5:6]) * dr2).astype(BF16)
        dff_ref[0] = dffv
        dh2 = jnp.zeros((tm, D), F32)
        for c in range(nck):
            sl = slice(c * ck, (c + 1) * ck)
            da = _dot_nt(dffv, wd_ref[c])
            du = (da * (2.0 * jnp.maximum(u_ref[0, :, sl].astype(F32), 0.0))).astype(BF16)
            du_ref[0, :, sl] = du
            dh2 = dh2 + _dot_nt(du, wu_ref[c])
        x1 = x1_ref[0]
        dx1 = ALPHA * dr2 + dh2 * (1.0 + mod[4:5])
        bst_ref[0, 0:1, :] += _colsum(dh2 * x1)
        bst_ref[0, 1:2, :] += _colsum(dh2)
        mix = mix_ref[0]
        g = g_ref[...]
        _, xhat, rstd = _ln_fwd(ALPHA * x0_ref[0] + (1.0 + mod[2:3]) * mix, g, 0.0)
        dr1 = _ln_bwd(dx1, xhat, rstd, g)
        wst_ref[0:1, :] += _colsum(dx1 * xhat)
        wst_ref[1:2, :] += _colsum(dx1)
        bst_ref[0, 2:3, :] += _colsum(dr1 * mix)
        dx0_ref[0] = ALPHA * dr1
        dmix = ((1.0 + mod[2:3]) * dr1).astype(BF16)
        dmix_ref[0] = dmix
        dya_ref[0] = _dot_nt(dmix, wo_ref[0:sbw, :]).astype(BF16)
        dyb_ref[0] = _dot_nt(dmix, wo_ref[sbw:, :]).astype(BF16)

    sspecs, sshapes = _stat_specs(B, D)
    wa, wb = sbw, w_o.shape[0] - sbw
    return pl.pallas_call(
        body, name="mlp_bwd", grid=(B, S // tm),
        in_specs=[_tok(tm, D), _tok(tm, dff), _tok(tm, D), _tok(tm, D), _tok(tm, D), _perb(N_MOD, D),
                  _full(w_up), _full(w_down), _full(w_o), _full(ln_g)],
        out_specs=[_tok(tm, dff), _tok(tm, D), _tok(tm, D), _tok(tm, D), _tok(tm, wa), _tok(tm, wb)] + sspecs,
        out_shape=[_sds((B, S, dff), BF16), _sds((B, S, D), BF16), _sds((B, S, D), BF16), _sds((B, S, D), F32),
                   _sds((B, S, wa), BF16), _sds((B, S, wb), BF16)] + sshapes,
        compiler_params=_cparams(("arbitrary", "arbitrary")),
    )(dr2, u, x1, x0, mix, mod, w_up, w_down, w_o, ln_g)


def _inproj_bwd(x, x0, dx0a, mod, ln_g, dq, dk, dv, dqp, dkp, dmv, cq, ckv, w_in_p, w_uq_p, w_kv, gq, gkv,
                tc, ts1, ts2, dm):
    B, S, D = x.shape
    tm = dm["tm"]
    sbw, qr, kvr, nh = dm["sbw"], dm["qr"], dm["kvr"], dm["nh"]
    qpw = nh * HEAD_PAD
    dinp = w_in_p.shape[1]
    kvw = w_kv.shape[1]

    def body(x_ref, x0_ref, dx0a_ref, mod_ref, g_ref, dq_ref, dk_ref, dv_ref, dqp_ref, dkp_ref, dmv_ref,
             cq_ref, ckv_ref, win_ref, wuq_ref, wkv_ref, gq_ref, gkv_ref, tc_ref, ts1_ref, ts2_ref,
             gx_ref, dproj_ref, dqpre_ref, dkvo_ref, bst_ref, wst_ref):
        _stat_init(bst_ref, wst_ref)
        mod = mod_ref[0]
        c1, s1, s2 = tc_ref[...], ts1_ref[...], ts2_ref[...]
        c8, s18, s28 = jnp.tile(c1, (1, nh)), jnp.tile(s1, (1, nh)), jnp.tile(s2, (1, nh))
        dqpre = _rope_t(dqp_ref[0].astype(F32), c8, s18, s28).astype(BF16)
        dqpre_ref[0] = dqpre
        gq = gq_ref[...]
        cq = cq_ref[0]
        rq = lax.rsqrt(jnp.mean(cq * cq, axis=-1, keepdims=True) + RMS_EPS)
        dqn = _dot_nt(dqpre, wuq_ref[...])
        wst_ref[4:5, 0:qr] += _colsum(dqn * cq * rq)
        dqg = dqn * gq
        dcq = rq * dqg - cq * (rq * rq * rq) * jnp.mean(dqg * cq, axis=-1, keepdims=True)

        dkpre = _rope_t(dkp_ref[0].astype(F32), c8, s18, s28)
        dkr = dkpre[:, 0:HEAD_PAD]
        for h in range(1, nh):
            dkr = dkr + dkpre[:, h * HEAD_PAD:(h + 1) * HEAD_PAD]
        lane = lax.broadcasted_iota(jnp.int32, (tm, LANES), 1)
        dkr = jnp.where((lane >= MLA_NOPE) & (lane < MLA_NOPE + MLA_ROPE), dkr, 0.0)
        dkr = pltpu.roll(dkr, LANES - MLA_NOPE, 1)
        dkvo = jnp.concatenate([dkpre.astype(BF16), dmv_ref[0]], axis=1)
        dkvo_ref[0] = dkvo
        gkv = gkv_ref[...]
        ckv = ckv_ref[0]
        rkv = lax.rsqrt(jnp.mean(ckv * ckv, axis=-1, keepdims=True) + RMS_EPS)
        dkvn = _dot_nt(dkvo, wkv_ref[...])
        wst_ref[5:6, 0:kvr] += _colsum(dkvn * ckv * rkv)
        dkg = dkvn * gkv
        dckv = rkv * dkg - ckv * (rkv * rkv * rkv) * jnp.mean(dkg * ckv, axis=-1, keepdims=True)

        dproj = jnp.concatenate([dq_ref[0], dk_ref[0], dv_ref[0], dcq.astype(BF16), dckv.astype(BF16),
                                 dkr.astype(BF16)], axis=1)
        dproj_ref[0] = dproj
        dh = _dot_nt(dproj, win_ref[...])
        x0 = x0_ref[0]
        dx0 = dx0a_ref[0] + dh * (1.0 + mod[1:2])
        bst_ref[0, 0:1, :] += _colsum(dh * x0)
        bst_ref[0, 1:2, :] += _colsum(dh)
        g = g_ref[...]
        _, xhat, rstd = _ln_fwd(x_ref[0], g, 0.0)
        gx_ref[0] = _ln_bwd(dx0, xhat, rstd, g)
        wst_ref[0:1, :] += _colsum(dx0 * xhat)
        wst_ref[1:2, :] += _colsum(dx0)

    tab = pl.BlockSpec((tm, LANES), lambda b, s: (s, 0))
    sspecs, sshapes = _stat_specs(B, D)
    return pl.pallas_call(
        body, name="inproj_bwd", grid=(B, S // tm),
        in_specs=[_tok(tm, D), _tok(tm, D), _tok(tm, D), _perb(N_MOD, D), _full(ln_g),
                  _tok(tm, sbw), _tok(tm, sbw), _tok(tm, sbw), _tok(tm, qpw), _tok(tm, qpw), _tok(tm, nh * MLA_V),
                  _tok(tm, qr), _tok(tm, kvr), _full(w_in_p), _full(w_uq_p), _full(w_kv), _full(gq), _full(gkv),
                  tab, tab, tab],
        out_specs=[_tok(tm, D), _tok(tm, dinp), _tok(tm, qpw), _tok(tm, kvw)] + sspecs,
        out_shape=[_sds((B, S, D), F32), _sds((B, S, dinp), BF16), _sds((B, S, qpw), BF16),
                   _sds((B, S, kvw), BF16)] + sshapes,
        compiler_params=_cparams(("arbitrary", "arbitrary")),
    )(x, x0, dx0a, mod, ln_g, dq, dk, dv, dqp, dkp, dmv, cq, ckv, w_in_p, w_uq_p, w_kv, gq, gkv, tc, ts1, ts2)


def _tile_of(n, cap):
    if n <= cap:
        return n
    best = n
    for t in range(LANES, cap + 1, LANES):
        if n % t == 0:
            best = t
    return best


def _mm_tn(a, g, name, relu_sq=False, out_dtype=F32, col_blocks=None):
    T, K = a.shape
    N = g.shape[1]
    tt = 512 if T % 512 == 0 else T
    tk = _tile_of(K, 1024)
    tn = N // col_blocks if col_blocks else _tile_of(N, 1280)
    nt = T // tt

    def body(a_ref, g_ref, o_ref, acc_ref):
        @pl.when(pl.program_id(2) == 0)
        def _():
            acc_ref[...] = jnp.zeros_like(acc_ref)

        av = a_ref[...]
        if relu_sq:
            av = jnp.square(jnp.maximum(av.astype(F32), 0.0)).astype(BF16)
        acc_ref[...] += _dot_tn(av, g_ref[...])

        @pl.when(pl.program_id(2) == nt - 1)
        def _():
            o_ref[...] = acc_ref[...].astype(out_dtype).reshape(o_ref.shape)

    if col_blocks:
        out_spec = pl.BlockSpec((1, tk, tn), lambda i, j, t: (j, i, 0))
        out_shape = _sds((col_blocks, K, tn), out_dtype)
    else:
        out_spec = pl.BlockSpec((tk, tn), lambda i, j, t: (i, j))
        out_shape = _sds((K, N), out_dtype)
    return pl.pallas_call(
        body, name=name, grid=(K // tk, N // tn, nt),
        in_specs=[pl.BlockSpec((tt, tk), lambda i, j, t: (t, i)), pl.BlockSpec((tt, tn), lambda i, j, t: (t, j))],
        out_specs=out_spec, out_shape=out_shape,
        scratch_shapes=[pltpu.VMEM((tk, tn), F32)],
        compiler_params=_cparams(("parallel", "parallel", "arbitrary")),
    )(a, g)


def _reduce_adamw(parts, w, m, v, name):
    _, K, N = parts.shape
    tr = 256 if K % 256 == 0 else K

    def body(p_ref, w_ref, m_ref, v_ref, g_ref, d_ref, nm_ref, nv_ref):
        g = p_ref[0].astype(F32)
        for k in range(1, 4):
            g = g + p_ref[k].astype(F32)
        g_ref[0] = g
        d_ref[0], nm_ref[0], nv_ref[0] = _adamw(w_ref[0], g, m_ref[0], v_ref[0])

    spec = pl.BlockSpec((1, tr, N), lambda r: (0, r, 0))
    return pl.pallas_call(
        body, name=name, grid=(K // tr,),
        in_specs=[pl.BlockSpec((4, tr, N), lambda r: (0, r, 0)), spec, spec, spec],
        out_specs=[spec] * 4, out_shape=[_sds((1, K, N), F32)] * 4,
        compiler_params=_cparams(("parallel",)),
    )(parts, w, m, v)


def _finish(sm, dmod_all, dmod_my, cact_all, p_small, m_small, v_small, b_ada, m_b, v_b, w_ada, m_w, v_w):
    n0 = p_small.shape[1]
    n1 = sm.shape[1]
    d = cact_all.shape[1]

    def body(sm_ref, dma_ref, dmm_ref, ca_ref, p_ref, pm_ref, pv_ref, b_ref, bm_ref, bv_ref, w_ref, wm_ref, wv_ref,
             gs_ref, ds_ref, ms_ref, vs_ref, gb_ref, db_ref, mb_ref, vb_ref, gw_ref, dw_ref, mw_ref, vw_ref,
             loss_ref):
        gs = sm_ref[0:1, :]
        for k in range(1, N_DEV):
            gs = gs + sm_ref[k:k + 1, :]
        gs_ref[...] = gs
        ds_ref[...], ms_ref[...], vs_ref[...] = _adamw(p_ref[...], gs[:, 0:n0], pm_ref[...], pv_ref[...])
        loss_ref[...] = jnp.zeros((1, LANES), F32) + jnp.sum(gs[:, n1 - d:n1])
        gb = jnp.sum(dma_ref[...], axis=0, keepdims=True)
        gb_ref[...] = gb
        db_ref[...], mb_ref[...], vb_ref[...] = _adamw(b_ref[...], gb, bm_ref[...], bv_ref[...])
        gw = _dot_tn(ca_ref[...].astype(BF16), dmm_ref[...].astype(BF16))
        gw_ref[...] = gw
        dw_ref[...], mw_ref[...], vw_ref[...] = _adamw(w_ref[...], gw, wm_ref[...], wv_ref[...])

    s0 = _sds(p_small.shape, F32)
    sb = _sds(b_ada.shape, F32)
    sw = _sds(w_ada.shape, F32)
    return pl.pallas_call(
        body, name="finish_small",
        out_shape=[_sds((1, n1), F32), s0, s0, s0, sb, sb, sb, sb, sw, sw, sw, sw,
                   _sds((1, LANES), F32)],
        compiler_params=pltpu.CompilerParams(vmem_limit_bytes=VMEM_LIMIT),
    )(sm, dmod_all, dmod_my, cact_all, p_small, m_small, v_small, b_ada, m_b, v_b, w_ada, m_w, v_w)


def _pack(arrs, dtype, width):
    flat = jnp.concatenate([a.astype(dtype).reshape(-1) for a in arrs])
    rows = -(-flat.shape[0] // (256 * width)) * 256
    return jnp.pad(flat, (0, rows * width - flat.shape[0])).reshape(rows, width)


def _unpack(slab, shapes):
    flat = slab.reshape(-1)
    out, o = [], 0
    for s in shapes:
        n = math.prod(s)
        out.append(flat[o:o + n].reshape(s))
        o += n
    return out


def _rope_tables(S):
    inv_freq = 1.0 / (ROPE_BASE ** (jnp.arange(0, MLA_ROPE, 2, dtype=F32) / MLA_ROPE))
    ang = jnp.arange(S, dtype=F32)[:, None] * inv_freq[None, :]
    cos, sin = jnp.cos(ang), jnp.sin(ang)
    one = jnp.ones((S, MLA_NOPE), F32)
    z16 = jnp.zeros((S, 16), F32)
    z32 = jnp.zeros((S, 32), F32)
    z64 = jnp.zeros((S, MLA_NOPE), F32)
    tc = jnp.concatenate([one, cos, cos, jnp.ones((S, 32), F32)], axis=1)
    ts1 = jnp.concatenate([z64, -sin, z16, z32], axis=1)
    ts2 = jnp.concatenate([z64, z16, sin, z32], axis=1)
    return tc, ts1, ts2


def kernel(x, c, ln_in_g, ln_in_b, w_ada, b_ada, w_in, q_norm_g, kv_norm_g, w_uq, w_ukv, w_o, ln1_g, ln1_b, w_up, w_down, ln2_g, ln2_b, loss_target, m_ln_in_g, m_ln_in_b, m_w_ada, m_b_ada, m_w_in, m_q_norm_g, m_kv_norm_g, m_w_uq, m_w_ukv, m_w_o, m_ln1_g, m_ln1_b, m_w_up, m_w_down, m_ln2_g, m_ln2_b, v_ln_in_g, v_ln_in_b, v_w_ada, v_b_ada, v_w_in, v_q_norm_g, v_kv_norm_g, v_w_uq, v_w_ukv, v_w_o, v_ln1_g, v_ln1_b, v_w_up, v_w_down, v_ln2_g, v_ln2_b):
    B, S, D = x.shape
    sbw = D // 2
    mlw = D - sbw
    nh = mlw // MLA_V
    qr = w_uq.shape[1]
    kvr = w_ukv.shape[1]
    qk = MLA_NOPE + MLA_ROPE
    dff = w_up.shape[2] * N_DEV
    din = w_in.shape[2] * N_DEV
    tm = 512 if S % 512 == 0 else S
    tq = min(512, S // 2)
    dm = dict(tm=tm, tm_small=min(tm, 256), tq=tq, sbw=sbw, qr=qr, kvr=kvr, nh=nh)
    width = 1024 if D >= 1024 else LANES
    dev = 4 * lax.axis_index("x") + 2 * lax.axis_index("y") + lax.axis_index("c")

    nada = w_ada.shape[2]
    c_all = _all_gather([c], "gather_c")[0].reshape(N_DEV * B, D)
    b_loc = lax.dynamic_slice(b_ada, (0, dev * nada), (1, nada))
    cact_all, mod_part = _ada_partial(c_all, w_ada[0], b_loc)
    mod_all = _all_gather([mod_part], "gather_mod")[0]
    mod = lax.dynamic_slice(mod_all, (0, dev * B, 0), (N_DEV, B, nada))
    mod = jnp.swapaxes(mod, 0, 1).reshape(B, N_MOD, D)

    big = [w_in, w_uq, w_ukv, w_o, w_up, w_down]
    late_w, late_token = _chip_exchange_start([a[0].astype(BF16) for a in big[3:]], "gather_w_late_start", scatter=False)
    w_in8, w_uq8, w_ukv8 = _all_gather([a[0].astype(BF16) for a in big[:3]], "gather_w_first")
    cols = lambda a8: jnp.swapaxes(a8, 0, 1).reshape(a8.shape[1], N_DEV * a8.shape[2])
    w_in_p = jnp.pad(cols(w_in8), ((0, 0), (0, LANES - MLA_ROPE)))
    zpad = jnp.zeros((qr, nh, HEAD_PAD - qk), BF16)
    w_uq_p = jnp.concatenate([cols(w_uq8).reshape(qr, nh, qk), zpad], axis=2).reshape(qr, nh * HEAD_PAD)
    w_ukv_f = cols(w_ukv8)
    w_uk = w_ukv_f[:, :nh * MLA_NOPE].reshape(kvr, nh, MLA_NOPE)
    w_uk_p = jnp.concatenate([w_uk, jnp.zeros((kvr, nh, HEAD_PAD - MLA_NOPE), BF16)], axis=2)
    w_kv = jnp.concatenate([w_uk_p.reshape(kvr, nh * HEAD_PAD), w_ukv_f[:, nh * MLA_NOPE:]], axis=1)

    tc, ts1, ts2 = _rope_tables(S)
    g_in, b_in = ln_in_g.reshape(1, D), ln_in_b.reshape(1, D)
    (x0, h, sq, sk, sv, qp, kp, mv, cq, ckv, qn, kvn) = _inproj_fwd(
        x, mod, g_in, b_in, w_in_p, w_uq_p, w_kv, q_norm_g, kv_norm_g, tc, ts1, ts2, dm, late_token)
    sb_y, sb_tot = _sb_fwd(sq, sk, sv, dm)
    mla_y, mla_lse = _mla_fwd(qp, kp, mv, dm, sb_tot)
    late_by_chip = _chip_exchange_wait(late_w, mla_lse, "gather_w_late_wait")
    w_o8, w_up8, w_down8 = [b.reshape((N_DEV,) + b.shape[2:]) for b in _core_gather(late_by_chip, "gather_w_late_cores")]
    w_o_f = w_o8.reshape(D, D)
    mix, x1, h2 = _outproj_fwd(sb_y, mla_y, x0, mod, w_o_f, ln1_g, ln1_b, dm)
    u, dr2, bst_c, wst_c = _mlp_fwd(h2, x1, mod, loss_target, w_up8, w_down8, ln2_g, ln2_b, dm)

    du, dffb, dmixb, dx0a, dsb_y, dmla_y, bst_b, wst_b = _mlp_bwd(
        dr2, u, x1, x0, mix, mod, w_up8, w_down8, w_o_f, ln1_g, dm)
    T = B * S
    r2 = lambda a: a.reshape(T, a.shape[2])
    by_core = lambda a: a.reshape((4, 2) + a.shape[1:])
    g_o = jnp.concatenate([_mm_tn(r2(sb_y), r2(dmixb), "grad_w_o_sb", out_dtype=BF16),
                           _mm_tn(r2(mla_y), r2(dmixb), "grad_w_o_mla", out_dtype=BF16)], axis=0)
    g_up8 = _mm_tn(r2(h2), r2(du), "grad_w_up", out_dtype=BF16, col_blocks=N_DEV)
    g_down = _mm_tn(r2(u), r2(dffb), "grad_w_down", relu_sq=True, out_dtype=BF16)
    early = [g_o.reshape(N_DEV, D // N_DEV, D), g_up8, g_down.reshape(N_DEV, dff // N_DEV, D)]
    early_sum = _core_scatter_sum([by_core(a) for a in early], "scatter_g_early_cores")
    early_g, early_token = _chip_exchange_start(early_sum, "scatter_g_early_start", scatter=True)

    dsq, dsk, dsv = _sb_bwd(sq, sk, sv, sb_tot, dsb_y, dm, early_token)
    dqp, dkp, dmv = _mla_bwd(qp, kp, mv, mla_y, mla_lse, dmla_y, dm, dsq)
    grad_x, dproj, dqpre, dkvo, bst_a, wst_a = _inproj_bwd(
        x, x0, dx0a, mod, g_in, dsq, dsk, dsv, dqp, dkp, dmv, cq, ckv, w_in_p, w_uq_p, w_kv, q_norm_g, kv_norm_g,
        tc, ts1, ts2, dm)
    g_in_p = _mm_tn(r2(h), r2(dproj), "grad_w_in")
    g_uq_p = _mm_tn(r2(qn), r2(dqpre), "grad_w_uq")
    g_kv = _mm_tn(r2(kvn), r2(dkvo), "grad_w_kv")
    g_uq_f = g_uq_p.reshape(qr, nh, HEAD_PAD)[:, :, :qk].reshape(qr, nh * qk)
    g_uk = g_kv[:, :nh * HEAD_PAD].reshape(kvr, nh, HEAD_PAD)[:, :, :MLA_NOPE].reshape(kvr, nh * MLA_NOPE)
    g_ukv_f = jnp.concatenate([g_uk, g_kv[:, nh * HEAD_PAD:]], axis=1)
    early_quarter = _chip_exchange_wait(early_g, g_kv, "scatter_g_early_wait")

    def by_dest_cols(a):
        k, n = a.shape[0], a.shape[1] // N_DEV
        return jnp.swapaxes(a.reshape(k, N_DEV, n), 0, 1).astype(BF16)

    last = [by_dest_cols(g_in_p[:, :din]), by_dest_cols(g_uq_f), by_dest_cols(g_ukv_f)]
    last_sum = _core_scatter_sum([by_core(a) for a in last], "scatter_g_last_cores")
    quarter = list(_chip_exchange(last_sum, "scatter_g_last_chips", scatter=True)) + list(early_quarter)
    names = ["w_in", "w_uq", "w_ukv", "w_o", "w_up", "w_down"]
    moms = [m_w_in, m_w_uq, m_w_ukv, m_w_o, m_w_up, m_w_down]
    vars_ = [v_w_in, v_w_uq, v_w_ukv, v_w_o, v_w_up, v_w_down]
    res = [_reduce_adamw(p, w, m, v, "adamw_" + n) for p, w, m, v, n in zip(quarter, big, moms, vars_, names)]
    gb, db, nmb, nvb = ([r[i] for r in res] for i in range(4))

    dmod = jnp.concatenate([bst_a[:, 1], bst_a[:, 0], bst_b[:, 2], bst_b[:, 1], bst_b[:, 0], bst_c[:, 0]], axis=1)
    small = jnp.concatenate([wst_a[0], wst_a[1], wst_a[4, :qr], wst_a[5, :kvr], wst_b[0], wst_b[1],
                             wst_c[0], wst_c[1], wst_c[2]])
    n1 = small.shape[0]
    both = _all_gather([_pack([dmod, small], F32, LANES)], "gather_small")[0].reshape(N_DEV, -1)
    dmod_all = both[:, :B * N_MOD * D].reshape(N_DEV * B, N_MOD * D)
    sm = both[:, B * N_MOD * D:B * N_MOD * D + n1]
    dmod_my = lax.dynamic_slice(dmod_all, (0, dev * nada), (N_DEV * B, nada))
    row = lambda arrs: jnp.concatenate([a.reshape(1, -1) for a in arrs], axis=1)
    smalls = [ln_in_g, ln_in_b, q_norm_g, kv_norm_g, ln1_g, ln1_b, ln2_g, ln2_b]
    small_shapes = [a.shape for a in smalls]
    (gs, ds, nms, nvs, g_b, d_b, nm_b, nv_b, g_w, d_w, nm_w, nv_w, loss_v) = _finish(
        sm, dmod_all, dmod_my, cact_all, row(smalls),
        row([m_ln_in_g, m_ln_in_b, m_q_norm_g, m_kv_norm_g, m_ln1_g, m_ln1_b, m_ln2_g, m_ln2_b]),
        row([v_ln_in_g, v_ln_in_b, v_q_norm_g, v_kv_norm_g, v_ln1_g, v_ln1_b, v_ln2_g, v_ln2_b]),
        b_ada, m_b_ada, v_b_ada, w_ada[0], m_w_ada[0], v_w_ada[0])
    gsm, dsm, nmsm, nvsm = (_unpack(s, small_shapes) for s in (gs, ds, nms, nvs))

    def ordered(sm_l, w_l, ada_w, ada_b):
        return [sm_l[0], sm_l[1], ada_w[None], ada_b, w_l[0], sm_l[2], sm_l[3], w_l[1], w_l[2], w_l[3],
                sm_l[4], sm_l[5], w_l[4], w_l[5], sm_l[6], sm_l[7]]

    loss = loss_v[0, 0]
    return (loss, grad_x, *ordered(gsm, gb, g_w, g_b), *ordered(dsm, db, d_w, d_b),
            *ordered(nmsm, nmb, nm_w, nm_b), *ordered(nvsm, nvb, nv_w, nv_b))
```

```python
import functools
import math

import jax
import jax.numpy as jnp
from jax import lax
from jax.experimental import pallas as pl
from jax.experimental.pallas import tpu as pltpu

F32 = jnp.float32
BF16 = jnp.bfloat16

SB_HD = 64
MLA_V = 64
MLA_NOPE = 64
MLA_ROPE = 32
HEAD_PAD = 128
CHUNK = 64
ROPE_BASE = 10000.0
LN_EPS = 1e-5
RMS_EPS = 1e-6
DEPTH = 1
ALPHA = (2.0 * DEPTH) ** 0.25
N_MOD = 6
ADAM_LR = 0.001
ADAM_B1 = 0.9
ADAM_B2 = 0.999
ADAM_EPS = 1e-08
ADAM_WD = 0.01
ADAM_STEP = 10
N_DEV = 8
LANES = 128
LOG2E = 1.4426950408889634
CUMSUM_W = 256
VMEM_LIMIT = 56 * 1024 * 1024
MESH = pl.DeviceIdType.MESH


def _dot(a, b):
    return jnp.dot(a, b, preferred_element_type=F32)


def _dot_nt(a, b):
    return lax.dot_general(a, b, (((1,), (1,)), ((), ())), preferred_element_type=F32)


def _dot_tn(a, b):
    return lax.dot_general(a, b, (((0,), (0,)), ((), ())), preferred_element_type=F32)


def _cparams(sem):
    return pltpu.CompilerParams(dimension_semantics=sem, vmem_limit_bytes=VMEM_LIMIT)


def _full(a):
    nd = a.ndim
    return pl.BlockSpec(a.shape, lambda *_: (0,) * nd, pipeline_mode=pl.Buffered(1))


def _tok(tm, w):
    return pl.BlockSpec((1, tm, w), lambda b, s: (b, s, 0))


def _perb(rows, w):
    return pl.BlockSpec((1, rows, w), lambda b, s: (b, 0, 0))


def _sds(shape, dtype):
    return jax.ShapeDtypeStruct(shape, dtype)


def _ln_fwd(x, g, b):
    mu = jnp.mean(x, axis=-1, keepdims=True)
    xc = x - mu
    var = jnp.mean(xc * xc, axis=-1, keepdims=True)
    rstd = lax.rsqrt(var + LN_EPS)
    xhat = xc * rstd
    return xhat * g + b, xhat, rstd


def _ln_bwd(dy, xhat, rstd, g):
    dxh = dy * g
    m1 = jnp.mean(dxh, axis=-1, keepdims=True)
    m2 = jnp.mean(dxh * xhat, axis=-1, keepdims=True)
    return rstd * (dxh - m1 - xhat * m2)


def _colsum(a):
    return jnp.sum(a, axis=0, keepdims=True)


def _rope(x, c, s1, s2):
    w = x.shape[-1]
    return x * c + pltpu.roll(x, w - 16, 1) * s1 + pltpu.roll(x, 16, 1) * s2


def _rope_t(x, c, s1, s2):
    w = x.shape[-1]
    return x * c - pltpu.roll(x, w - 16, 1) * s1 - pltpu.roll(x, 16, 1) * s2


def _adamw(w, g, m, v):
    m = ADAM_B1 * m + (1.0 - ADAM_B1) * g
    v = ADAM_B2 * v + (1.0 - ADAM_B2) * (g * g)
    m_hat = m / (1.0 - ADAM_B1 ** ADAM_STEP)
    v_hat = v / (1.0 - ADAM_B2 ** ADAM_STEP)
    delta = -ADAM_LR * (m_hat / (jnp.sqrt(v_hat) + ADAM_EPS) + ADAM_WD * w)
    return delta, m, v


def _my_place():
    return lax.axis_index("x"), lax.axis_index("y"), lax.axis_index("c")


def _chip_peers(mx, my):
    out = []
    for j in (1, 2, 3):
        px = 1 - mx if (j >> 1) else mx
        py = 1 - my if (j & 1) else my
        out.append((px, py, 2 * px + py))
    return out


def _hbm_call(body, name, n_in, out_shape, sems):
    hbm = pl.BlockSpec(memory_space=pl.ANY)
    return pl.pallas_call(
        body, name=name, out_shape=out_shape,
        in_specs=[hbm] * n_in, out_specs=[hbm] * len(out_shape),
        scratch_shapes=[pltpu.SemaphoreType.DMA(s) for s in sems])


def _chip_exchange(xs, name, scatter):
    n = len(xs)

    def body(*refs):
        x_refs, o_refs = refs[:n], refs[n:2 * n]
        ssem, rsem, lsem = refs[2 * n:]
        mx, my, mc = _my_place()
        me = 2 * mx + my
        peers = _chip_peers(mx, my)

        def copy(i, j, src_slot, dst_slot):
            px, py, _ = peers[j]
            return pltpu.make_async_remote_copy(
                src_ref=x_refs[i].at[src_slot] if scatter else x_refs[i], dst_ref=o_refs[i].at[dst_slot],
                send_sem=ssem.at[i, j], recv_sem=rsem.at[i, j], device_id=(px, py, mc), device_id_type=MESH)

        local = [pltpu.make_async_copy(x_refs[i].at[me] if scatter else x_refs[i], o_refs[i].at[me], lsem.at[i])
                 for i in range(n)]
        sends = [copy(i, j, peers[j][2], me) for i in range(n) for j in range(3)]
        for cp in local + sends:
            cp.start()
        for i in range(n):
            for j in range(3):
                copy(i, j, peers[j][2], peers[j][2]).wait_recv()
        for cp in sends:
            cp.wait_send()
        for cp in local:
            cp.wait()

    out_shape = [_sds((4,) + tuple(x.shape[1:] if scatter else x.shape), x.dtype) for x in xs]
    return _hbm_call(body, name, n, out_shape, [(n, 3), (n, 3), (n,)])(*xs)


def _chip_exchange_start(xs, name, scatter):
    n = len(xs)
    blks = [tuple(x.shape[1:] if scatter else x.shape) for x in xs]

    def body(*refs):
        x_refs, land_refs = refs[:n], refs[n:2 * n]
        ssem, rsem = refs[2 * n], refs[2 * n + 1]
        token = refs[-1]
        mx, my, mc = _my_place()
        me = 2 * mx + my
        for i in range(n):
            for j, (px, py, pk) in enumerate(_chip_peers(mx, my)):
                pltpu.make_async_remote_copy(
                    src_ref=x_refs[i].at[pk] if scatter else x_refs[i], dst_ref=land_refs[i].at[me],
                    send_sem=ssem.at[3 * i + j], recv_sem=rsem.at[3 * i + j], device_id=(px, py, mc),
                    device_id_type=MESH).start()
        token[...] = jnp.zeros_like(token)

    hbm = pl.BlockSpec(memory_space=pltpu.HBM)
    sem = pl.BlockSpec(memory_space=pltpu.SEMAPHORE)
    lands = [lax.empty((4,) + b, x.dtype) for b, x in zip(blks, xs)]
    res = pl.pallas_call(
        body, name=name,
        out_shape=[pltpu.SemaphoreType.DMA((3 * n,)), pltpu.SemaphoreType.DMA((3 * n,))]
        + [pltpu.HBM(x.shape, x.dtype) for x in xs] + [pltpu.HBM(l.shape, l.dtype) for l in lands]
        + [_sds((8, LANES), F32)],
        in_specs=[hbm] * (2 * n),
        out_specs=[sem, sem] + [hbm] * (2 * n) + [pl.BlockSpec(memory_space=pltpu.VMEM)],
        input_output_aliases={i: 2 + i for i in range(2 * n)},
        compiler_params=pltpu.CompilerParams(has_side_effects=pltpu.SideEffectType.DATAFLOW_SIDE_EFFECTING),
    )(*[pltpu.with_memory_space_constraint(a, pltpu.HBM) for a in list(xs) + lands])
    return dict(ssem=res[0], rsem=res[1], xs=res[2:2 + n], lands=res[2 + n:2 + 2 * n], n=n, scatter=scatter), res[-1]


def _chip_exchange_wait(handle, after, name):
    n, scatter = handle["n"], handle["scatter"]

    def body(*refs):
        x_refs, land_refs = refs[:n], refs[n:2 * n]
        ssem, rsem = refs[2 * n], refs[2 * n + 1]
        mx, my, mc = _my_place()
        for i in range(n):
            for j, (px, py, pk) in enumerate(_chip_peers(mx, my)):
                cp = pltpu.make_async_remote_copy(
                    src_ref=x_refs[i].at[pk] if scatter else x_refs[i], dst_ref=land_refs[i].at[pk],
                    send_sem=ssem.at[3 * i + j], recv_sem=rsem.at[3 * i + j], device_id=(px, py, mc),
                    device_id_type=MESH)
                cp.wait_send()
                cp.wait_recv()

    hbm = pl.BlockSpec(memory_space=pltpu.HBM)
    sem = pl.BlockSpec(memory_space=pltpu.SEMAPHORE)
    ops = list(handle["xs"]) + list(handle["lands"])
    res = pl.pallas_call(
        body, name=name,
        out_shape=[pltpu.HBM(a.shape, a.dtype) for a in ops],
        in_specs=[hbm] * (2 * n) + [sem, sem, pl.BlockSpec(memory_space=pl.ANY)],
        out_specs=[hbm] * (2 * n),
        input_output_aliases={i: i for i in range(2 * n)},
        compiler_params=pltpu.CompilerParams(has_side_effects=pltpu.SideEffectType.DATAFLOW_SIDE_EFFECTING),
    )(*ops, handle["ssem"], handle["rsem"], after)
    me = 2 * lax.axis_index("x") + lax.axis_index("y")
    out = []
    for x, land in zip(res[:n], res[n:]):
        own = lax.dynamic_index_in_dim(x, me, 0, keepdims=False) if scatter else x
        out.append(lax.dynamic_update_index_in_dim(land, own, me, 0))
    return out


def _core_gather(xs, name):
    n = len(xs)

    def body(*refs):
        x_refs, o_refs, mine, got = refs[:n], refs[n:2 * n], refs[2 * n:3 * n], refs[3 * n:4 * n]
        lsem, ssem, rsem, osem = refs[4 * n:]
        mx, my, mc = _my_place()
        loads = [pltpu.make_async_copy(x_refs[i], mine[i], lsem.at[i]) for i in range(n)]
        for cp in loads:
            cp.start()
        sends, stores = [], []
        for i in range(n):
            loads[i].wait()
            cp = pltpu.make_async_remote_copy(
                src_ref=mine[i], dst_ref=got[i], send_sem=ssem.at[i], recv_sem=rsem.at[i],
                device_id=(mx, my, 1 - mc), device_id_type=MESH)
            cp.start()
            sends.append(cp)
            for k in range(4):
                st = pltpu.make_async_copy(mine[i].at[k], o_refs[i].at[k, mc], osem.at[i, k])
                st.start()
                stores.append(st)
        for i in range(n):
            sends[i].wait_recv()
            for k in range(4):
                st = pltpu.make_async_copy(got[i].at[k], o_refs[i].at[k, 1 - mc], osem.at[n + i, k])
                st.start()
                stores.append(st)
        for cp in sends:
            cp.wait_send()
        for st in stores:
            st.wait()

    hbm = pl.BlockSpec(memory_space=pl.ANY)
    bufs = [pltpu.VMEM(x.shape, x.dtype) for x in xs]
    return pl.pallas_call(
        body, name=name,
        out_shape=[_sds((4, 2) + tuple(x.shape[1:]), x.dtype) for x in xs],
        in_specs=[hbm] * n, out_specs=[hbm] * n,
        scratch_shapes=bufs + bufs + [pltpu.SemaphoreType.DMA((n,)), pltpu.SemaphoreType.DMA((n,)),
                                      pltpu.SemaphoreType.DMA((n,)), pltpu.SemaphoreType.DMA((2 * n, 4))],
        compiler_params=pltpu.CompilerParams(vmem_limit_bytes=VMEM_LIMIT),
    )(*xs)


def _rows_step(k):
    for r in (256, 128, 64, 32, 16, 8):
        if k % r == 0:
            return r
    return k


def _core_scatter_sum(gs, name):
    n = len(gs)

    def body(*refs):
        g_refs, o_refs = refs[:n], refs[n:2 * n]
        send, got, mine = refs[2 * n:3 * n], refs[3 * n:4 * n], refs[4 * n:5 * n]
        lsem, msem, ssem, rsem, osem = refs[5 * n:]
        mx, my, mc = _my_place()
        pairs = [(i, k) for i in range(n) for k in range(4)]
        out_loads = {(i, k): pltpu.make_async_copy(g_refs[i].at[k, 1 - mc], send[i].at[k], lsem.at[i, k])
                     for i, k in pairs}
        own_loads = {(i, k): pltpu.make_async_copy(g_refs[i].at[k, mc], mine[i].at[k], msem.at[i, k])
                     for i, k in pairs}
        for p in pairs:
            out_loads[p].start()
        for p in pairs:
            own_loads[p].start()
        sends = []
        for i in range(n):
            for k in range(4):
                out_loads[i, k].wait()
            cp = pltpu.make_async_remote_copy(
                src_ref=send[i], dst_ref=got[i], send_sem=ssem.at[i], recv_sem=rsem.at[i],
                device_id=(mx, my, 1 - mc), device_id_type=MESH)
            cp.start()
            sends.append(cp)
        stores = []
        for i in range(n):
            for k in range(4):
                own_loads[i, k].wait()
            sends[i].wait_recv()
            rows = g_refs[i].shape[2]
            step = _rows_step(rows)

            def add(r, _, i=i, step=step):
                sl = pl.ds(pl.multiple_of(r * step, step), step)
                for k in range(4):
                    mine[i][k, sl, :] = (mine[i][k, sl, :].astype(F32) + got[i][k, sl, :].astype(F32)).astype(BF16)
                return 0

            lax.fori_loop(0, rows // step, add, 0)
            st = pltpu.make_async_copy(mine[i], o_refs[i], osem.at[i])
            st.start()
            stores.append(st)
        for cp in sends:
            cp.wait_send()
        for st in stores:
            st.wait()

    hbm = pl.BlockSpec(memory_space=pl.ANY)
    blk = [(4,) + tuple(g.shape[2:]) for g in gs]
    bufs = [pltpu.VMEM(b, BF16) for b in blk]
    return pl.pallas_call(
        body, name=name,
        out_shape=[_sds(b, BF16) for b in blk],
        in_specs=[hbm] * n, out_specs=[hbm] * n,
        scratch_shapes=bufs * 3 + [pltpu.SemaphoreType.DMA((n, 4)), pltpu.SemaphoreType.DMA((n, 4)),
                                   pltpu.SemaphoreType.DMA((n,)), pltpu.SemaphoreType.DMA((n,)),
                                   pltpu.SemaphoreType.DMA((n,))],
        compiler_params=pltpu.CompilerParams(vmem_limit_bytes=VMEM_LIMIT),
    )(*gs)


def _all_gather(xs, name):
    by_chip = _chip_exchange(xs, name + "_chips", scatter=False)
    both = _core_gather(by_chip, name + "_cores")
    return [b.reshape((N_DEV,) + tuple(x.shape)) for b, x in zip(both, xs)]


def _ada_partial(c_all, w_ada_loc, b_loc):
    def body(c_ref, w_ref, b_ref, act_ref, mod_ref):
        c = c_ref[...]
        act = c * (1.0 / (1.0 + jnp.exp(-c)))
        act_ref[...] = act
        mod_ref[...] = _dot(act.astype(BF16), w_ref[...].astype(BF16)) + b_ref[...]

    nb, d = c_all.shape
    return pl.pallas_call(
        body, name="ada_partial",
        out_shape=(_sds((nb, d), F32), _sds((nb, w_ada_loc.shape[1]), F32)),
        compiler_params=pltpu.CompilerParams(vmem_limit_bytes=VMEM_LIMIT),
    )(c_all, w_ada_loc, b_loc)


_AFTER = pl.BlockSpec(memory_space=pl.ANY)


def _inproj_fwd(x, mod, ln_g, ln_b, w_in_p, w_uq_p, w_kv, gq, gkv, tc, ts1, ts2, dm, after):
    B, S, D = x.shape
    tm = dm["tm"]
    sbw, qr, kvr, nh = dm["sbw"], dm["qr"], dm["kvr"], dm["nh"]
    o_cq, o_ckv, o_kr = 3 * sbw, 3 * sbw + qr, 3 * sbw + qr + kvr
    qpw = nh * HEAD_PAD

    def body(x_ref, mod_ref, g_ref, b_ref, win_ref, wuq_ref, wkv_ref, gq_ref, gkv_ref, tc_ref, ts1_ref, ts2_ref, _,
             x0_ref, h_ref, q_ref, k_ref, v_ref, qp_ref, kp_ref, mv_ref, cq_ref, ckv_ref, qn_ref, kvn_ref):
        x0, _, _ = _ln_fwd(x_ref[0], g_ref[...], b_ref[...])
        x0_ref[0] = x0
        mod = mod_ref[0]
        h = (x0 * (1.0 + mod[1:2]) + mod[0:1]).astype(BF16)
        h_ref[0] = h
        proj = _dot(h, win_ref[...])
        q_ref[0] = (proj[:, 0:sbw] * (SB_HD ** -0.5)).astype(BF16)
        k_ref[0] = proj[:, sbw:2 * sbw].astype(BF16)
        v_ref[0] = proj[:, 2 * sbw:3 * sbw].astype(BF16)
        cq = proj[:, o_cq:o_cq + qr]
        ckv = proj[:, o_ckv:o_ckv + kvr]
        cq_ref[0] = cq
        ckv_ref[0] = ckv
        qn = (cq * lax.rsqrt(jnp.mean(cq * cq, axis=-1, keepdims=True) + RMS_EPS) * gq_ref[...]).astype(BF16)
        kvn = (ckv * lax.rsqrt(jnp.mean(ckv * ckv, axis=-1, keepdims=True) + RMS_EPS) * gkv_ref[...]).astype(BF16)
        qn_ref[0] = qn
        kvn_ref[0] = kvn
        c1, s1, s2 = tc_ref[...], ts1_ref[...], ts2_ref[...]
        c8, s18, s28 = jnp.tile(c1, (1, nh)), jnp.tile(s1, (1, nh)), jnp.tile(s2, (1, nh))
        qp_ref[0] = _rope(_dot(qn, wuq_ref[...]), c8, s18, s28).astype(BF16)
        kvo = _dot(kvn, wkv_ref[...])
        kr = pltpu.roll(proj[:, o_kr:o_kr + LANES], 64, 1)
        kr = _rope(kr, c1, s1, s2)
        kp_ref[0] = (kvo[:, 0:qpw] + jnp.tile(kr, (1, nh))).astype(BF16)
        mv_ref[0] = kvo[:, qpw:].astype(BF16)

    tab = pl.BlockSpec((tm, LANES), lambda b, s: (s, 0))
    outs = [(D, F32), (D, BF16), (sbw, BF16), (sbw, BF16), (sbw, BF16), (qpw, BF16), (qpw, BF16),
            (nh * MLA_V, BF16), (qr, F32), (kvr, F32), (qr, BF16), (kvr, BF16)]
    return pl.pallas_call(
        body, name="inproj_fwd", grid=(B, S // tm),
        in_specs=[_tok(tm, D), _perb(N_MOD, D), _full(ln_g), _full(ln_b), _full(w_in_p), _full(w_uq_p),
                  _full(w_kv), _full(gq), _full(gkv), tab, tab, tab, _AFTER],
        out_specs=[_tok(tm, w) for w, _ in outs],
        out_shape=[_sds((B, S, w), t) for w, t in outs],
        compiler_params=_cparams(("parallel", "parallel")),
    )(x, mod, ln_g, ln_b, w_in_p, w_uq_p, w_kv, gq, gkv, tc, ts1, ts2, after)


def _neg_abs(x):
    sign = jnp.uint32(0x80000000)
    return lax.bitcast_convert_type(lax.bitcast_convert_type(x, jnp.uint32) | sign, F32)


def _log2_keep(z):
    zs = z * (-LOG2E)
    return jnp.minimum(zs, 0.0) - jnp.log2(1.0 + jnp.exp2(_neg_abs(zs))), zs


def _split_dot(a, u):
    hi = a.astype(BF16)
    lo = (a - hi.astype(F32)).astype(BF16)
    return _dot(hi, u) + _dot(lo, u)


def _tri(n, rel):
    row = lax.broadcasted_iota(jnp.int32, (n, n), 0)
    col = lax.broadcasted_iota(jnp.int32, (n, n), 1)
    return rel(row, col).astype(BF16)


def _running_sum(a, tri, reverse, split):
    cs = tri.shape[0]
    n = a.shape[1] // cs
    out = [None] * n
    run = None
    for c in (reversed(range(n)) if reverse else range(n)):
        part = a[:, c * cs:(c + 1) * cs]
        loc = _split_dot(part, tri) if split else _dot(part.astype(BF16), tri)
        out[c] = loc if run is None else loc + run
        tot = jnp.sum(part, axis=1, keepdims=True)
        run = tot if run is None else run + tot
    return (out[0] if n == 1 else jnp.concatenate(out, axis=1)), run


def _sb_fwd(q, k, v, dm):
    B, S, W = q.shape
    tq = dm["tq"]
    nq = S // tq

    def body(q_ref, k_ref, v_ref, y_ref, tot_ref):
        qi = pl.program_id(2)
        q2 = q_ref[0]
        lane = lax.broadcasted_iota(jnp.int32, (tq, LANES), 1)
        qh = [jnp.where(lane < SB_HD, q2, 0).astype(BF16), jnp.where(lane >= SB_HD, q2, 0).astype(BF16)]
        row = lax.broadcasted_iota(jnp.int32, (tq, tq), 0)
        col = lax.broadcasted_iota(jnp.int32, (tq, tq), 1)
        later = _tri(min(tq, CUMSUM_W), lambda j, s: j > s)
        strict = col < row

        def block(j, carry, masked):
            off = pl.multiple_of(j * tq, tq)
            k2 = k_ref[0, pl.ds(off, tq), :]
            v2 = v_ref[0, pl.ds(off, tq), :]
            new = []
            for h in range(2):
                acc, run = carry[2 * h], carry[2 * h + 1]
                a, zs = _log2_keep(_dot_nt(qh[h], k2))
                if masked:
                    a = jnp.where(strict, a, 0.0)
                a_later, a_tot = _running_sum(a, later, reverse=True, split=True)
                w = jnp.exp2((a - zs) + a_later + run)
                if masked:
                    w = jnp.where(strict, w, 0.0)
                new.append(acc + _dot(w.astype(BF16), v2))
                new.append(run + a_tot)
            return tuple(new)

        zero = jnp.zeros((tq, LANES), F32)
        zrun = jnp.zeros((tq, 1), F32)
        carry = block(qi, (zero, zrun, zero, zrun), True)
        carry = lax.fori_loop(0, qi, lambda jj, c: block(qi - 1 - jj, c, False), carry)
        y_ref[0] = jnp.where(lane < SB_HD, carry[0], carry[2]).astype(BF16)
        tot_ref[0] = jnp.where(lane < SB_HD, carry[1], carry[3])

    qspec = pl.BlockSpec((1, tq, LANES), lambda b, hp, i: (b, i, hp))
    kspec = pl.BlockSpec((1, S, LANES), lambda b, hp, i: (b, 0, hp))
    return pl.pallas_call(
        body, name="sb_fwd", grid=(B, W // LANES, nq),
        in_specs=[qspec, kspec, kspec],
        out_specs=[qspec, qspec],
        out_shape=[_sds((B, S, W), BF16), _sds((B, S, W), F32)],
        compiler_params=_cparams(("parallel", "parallel", "arbitrary")),
    )(q, k, v)


def _sb_bwd(q, k, v, tot, dy, dm, after):
    B, S, W = q.shape
    tq = dm["tq"]
    nq = S // tq

    def body(q_ref, k_ref, v_ref, tot_ref, dy_ref, _, dq_ref, dk_ref, dv_ref, dk_acc, dv_acc):
        qi = pl.program_id(2)

        @pl.when(qi == 0)
        def _():
            dk_acc[...] = jnp.zeros_like(dk_acc)
            dv_acc[...] = jnp.zeros_like(dv_acc)

        q2 = q_ref[0]
        dy2 = dy_ref[0]
        tot2 = tot_ref[0]
        lane = lax.broadcasted_iota(jnp.int32, (tq, LANES), 1)
        in_h = [lane < SB_HD, lane >= SB_HD]
        qh = [jnp.where(m, q2, 0).astype(BF16) for m in in_h]
        dyh = [jnp.where(m, dy2, 0).astype(BF16) for m in in_h]
        toth = [tot2[:, 0:1], tot2[:, SB_HD:SB_HD + 1]]
        row = lax.broadcasted_iota(jnp.int32, (tq, tq), 0)
        col = lax.broadcasted_iota(jnp.int32, (tq, tq), 1)
        upto = _tri(min(tq, CUMSUM_W), lambda j, s: j <= s)
        before = _tri(min(tq, CUMSUM_W), lambda s, j: s < j)
        strict = col < row

        def block(j, carry, masked):
            off = pl.multiple_of(j * tq, tq)
            k2 = k_ref[0, pl.ds(off, tq), :]
            v2 = v_ref[0, pl.ds(off, tq), :]
            new = []
            dk_blk = jnp.zeros((tq, LANES), F32)
            dv_blk = jnp.zeros((tq, LANES), F32)
            for h in range(2):
                dq, pa, pg = carry[3 * h], carry[3 * h + 1], carry[3 * h + 2]
                a, zs = _log2_keep(_dot_nt(qh[h], k2))
                if masked:
                    a = jnp.where(strict, a, 0.0)
                a_upto, a_tot = _running_sum(a, upto, reverse=False, split=True)
                w = jnp.exp2((a - zs) + ((toth[h] - pa) - a_upto))
                if masked:
                    w = jnp.where(strict, w, 0.0)
                g = _dot_nt(dyh[h], v2) * w
                g_before, g_tot = _running_sum(g, before, reverse=False, split=False)
                g_before = g_before + pg
                dz = (g + g_before) * jnp.exp2(a) - g_before
                if masked:
                    dz = jnp.where(strict, dz, 0.0)
                dzb = dz.astype(BF16)
                dv_blk = dv_blk + _dot_tn(w.astype(BF16), dyh[h])
                dk_blk = dk_blk + _dot_tn(dzb, qh[h])
                new += [dq + _dot(dzb, k2), pa + a_tot, pg + g_tot]
            dk_acc[pl.ds(off, tq), :] += dk_blk
            dv_acc[pl.ds(off, tq), :] += dv_blk
            return tuple(new)

        zero = jnp.zeros((tq, LANES), F32)
        zrun = jnp.zeros((tq, 1), F32)
        carry = lax.fori_loop(0, qi, lambda j, c: block(j, c, False), (zero, zrun, zrun, zero, zrun, zrun))
        carry = block(qi, carry, True)
        dq_ref[0] = (jnp.where(in_h[0], carry[0], carry[3]) * (SB_HD ** -0.5)).astype(BF16)

        @pl.when(qi == nq - 1)
        def _():
            dk_ref[0] = dk_acc[...].astype(BF16)
            dv_ref[0] = dv_acc[...].astype(BF16)

    qspec = pl.BlockSpec((1, tq, LANES), lambda b, hp, i: (b, i, hp))
    kspec = pl.BlockSpec((1, S, LANES), lambda b, hp, i: (b, 0, hp))
    return pl.pallas_call(
        body, name="sb_bwd", grid=(B, W // LANES, nq),
        in_specs=[qspec, kspec, kspec, qspec, qspec, _AFTER],
        out_specs=[qspec, kspec, kspec],
        out_shape=[_sds((B, S, W), BF16)] * 3,
        scratch_shapes=[pltpu.VMEM((S, LANES), F32), pltpu.VMEM((S, LANES), F32)],
        compiler_params=_cparams(("parallel", "parallel", "arbitrary")),
    )(q, k, v, tot, dy, after)


def _chunk_mask(tq):
    row = lax.broadcasted_iota(jnp.int32, (tq, tq), 0)
    col = lax.broadcasted_iota(jnp.int32, (tq, tq), 1)
    return lax.shift_right_logical(col, 6) <= lax.shift_right_logical(row, 6)


def _mla_fwd(qp, kp, mv, dm, after):
    B, S, QW = qp.shape
    VW = mv.shape[2]
    tq = dm["tq"]
    nq = S // tq
    scale = (MLA_NOPE + MLA_ROPE) ** -0.5
    assert CHUNK == 64

    def body(q_ref, k_ref, v_ref, _, y_ref, lse_ref):
        qi = pl.program_id(2)
        q2 = q_ref[0]
        lane = lax.broadcasted_iota(jnp.int32, (tq, LANES), 1)
        allowed = _chunk_mask(tq)

        def block(j, carry, masked):
            off = pl.multiple_of(j * tq, tq)
            v2 = v_ref[0, pl.ds(off, tq), :]
            new = []
            for h in range(2):
                acc, m, l = carry[3 * h], carry[3 * h + 1], carry[3 * h + 2]
                kh = k_ref[0, pl.ds(off, tq), h * HEAD_PAD:(h + 1) * HEAD_PAD]
                s = _dot_nt(q2[:, h * HEAD_PAD:(h + 1) * HEAD_PAD], kh) * (scale * LOG2E)
                if masked:
                    s = jnp.where(allowed, s, -1e30)
                m_new = jnp.maximum(m, jnp.max(s, axis=1, keepdims=True))
                alpha = jnp.exp2(m - m_new)
                p = jnp.exp2(s - m_new)
                new += [alpha * acc + _dot(p.astype(BF16), v2), m_new,
                        alpha * l + jnp.sum(p, axis=1, keepdims=True)]
            return tuple(new)

        zero = jnp.zeros((tq, LANES), F32)
        m0 = jnp.full((tq, 1), -1e30, F32)
        l0 = jnp.zeros((tq, 1), F32)
        carry = block(qi, (zero, m0, l0, zero, m0, l0), True)
        carry = lax.fori_loop(0, qi, lambda j, c: block(j, c, False), carry)
        y0 = carry[0] / carry[2]
        y1 = carry[3] / carry[5]
        y_ref[0] = jnp.where(lane < MLA_V, y0, y1).astype(BF16)
        lse_ref[0] = jnp.where(lane < MLA_V, carry[1] + jnp.log2(carry[2]), carry[4] + jnp.log2(carry[5]))

    qspec = pl.BlockSpec((1, tq, 2 * HEAD_PAD), lambda b, hp, i: (b, i, hp))
    kspec = pl.BlockSpec((1, S, 2 * HEAD_PAD), lambda b, hp, i: (b, 0, hp))
    vspec = pl.BlockSpec((1, S, LANES), lambda b, hp, i: (b, 0, hp))
    yspec = pl.BlockSpec((1, tq, LANES), lambda b, hp, i: (b, i, hp))
    return pl.pallas_call(
        body, name="mla_fwd", grid=(B, VW // LANES, nq),
        in_specs=[qspec, kspec, vspec, _AFTER],
        out_specs=[yspec, yspec],
        out_shape=[_sds((B, S, VW), BF16), _sds((B, S, VW), F32)],
        compiler_params=_cparams(("parallel", "parallel", "arbitrary")),
    )(qp, kp, mv, after)


def _mla_bwd(qp, kp, mv, y, lse, dy, dm, after):
    B, S, QW = qp.shape
    VW = mv.shape[2]
    tq = dm["tq"]
    nq = S // tq
    scale = (MLA_NOPE + MLA_ROPE) ** -0.5

    def body(q_ref, k_ref, v_ref, y_ref, lse_ref, dy_ref, _, dq_ref, dk_ref, dv_ref, dk_acc, dv_acc):
        qi = pl.program_id(2)

        @pl.when(qi == 0)
        def _():
            dk_acc[...] = jnp.zeros_like(dk_acc)
            dv_acc[...] = jnp.zeros_like(dv_acc)

        q2 = q_ref[0]
        dy2 = dy_ref[0]
        lse2 = lse_ref[0]
        lane = lax.broadcasted_iota(jnp.int32, (tq, LANES), 1)
        in_h = [lane < MLA_V, lane >= MLA_V]
        prod = dy2.astype(F32) * y_ref[0].astype(F32)
        delta = [jnp.sum(jnp.where(m, prod, 0.0), axis=1, keepdims=True) for m in in_h]
        dyh = [jnp.where(m, dy2, 0).astype(BF16) for m in in_h]
        lseh = [lse2[:, 0:1], lse2[:, MLA_V:MLA_V + 1]]
        allowed = _chunk_mask(tq)

        def block(j, carry, masked):
            off = pl.multiple_of(j * tq, tq)
            v2 = v_ref[0, pl.ds(off, tq), :]
            new = []
            dv_blk = jnp.zeros((tq, LANES), F32)
            for h in range(2):
                sl = slice(h * HEAD_PAD, (h + 1) * HEAD_PAD)
                qhh = q2[:, sl]
                kh = k_ref[0, pl.ds(off, tq), sl]
                s = _dot_nt(qhh, kh) * (scale * LOG2E)
                if masked:
                    s = jnp.where(allowed, s, -1e30)
                p = jnp.exp2(s - lseh[h])
                ds = (p * (_dot_nt(dyh[h], v2) - delta[h]) * scale).astype(BF16)
                dv_blk = dv_blk + _dot_tn(p.astype(BF16), dyh[h])
                dk_acc[pl.ds(off, tq), sl] += _dot_tn(ds, qhh)
                new.append(carry[h] + _dot(ds, kh))
            dv_acc[pl.ds(off, tq), :] += dv_blk
            return tuple(new)

        zero = jnp.zeros((tq, HEAD_PAD), F32)
        carry = lax.fori_loop(0, qi, lambda j, c: block(j, c, False), (zero, zero))
        carry = block(qi, carry, True)
        dq_ref[0] = jnp.concatenate([carry[0], carry[1]], axis=1).astype(BF16)

        @pl.when(qi == nq - 1)
        def _():
            dk_ref[0] = dk_acc[...].astype(BF16)
            dv_ref[0] = dv_acc[...].astype(BF16)

    qspec = pl.BlockSpec((1, tq, 2 * HEAD_PAD), lambda b, hp, i: (b, i, hp))
    kspec = pl.BlockSpec((1, S, 2 * HEAD_PAD), lambda b, hp, i: (b, 0, hp))
    vspec = pl.BlockSpec((1, S, LANES), lambda b, hp, i: (b, 0, hp))
    yspec = pl.BlockSpec((1, tq, LANES), lambda b, hp, i: (b, i, hp))
    return pl.pallas_call(
        body, name="mla_bwd", grid=(B, VW // LANES, nq),
        in_specs=[qspec, kspec, vspec, yspec, yspec, yspec, _AFTER],
        out_specs=[qspec, kspec, vspec],
        out_shape=[_sds((B, S, QW), BF16), _sds((B, S, QW), BF16), _sds((B, S, VW), BF16)],
        scratch_shapes=[pltpu.VMEM((S, 2 * HEAD_PAD), F32), pltpu.VMEM((S, LANES), F32)],
        compiler_params=_cparams(("parallel", "parallel", "arbitrary")),
    )(qp, kp, mv, y, lse, dy, after)


def _outproj_fwd(sb_y, mla_y, x0, mod, w_o, ln_g, ln_b, dm):
    B, S, D = x0.shape
    tm = dm["tm"]
    sbw = sb_y.shape[2]

    def body(ya_ref, yb_ref, x0_ref, mod_ref, wo_ref, g_ref, b_ref, mix_ref, x1_ref, h2_ref):
        mod = mod_ref[0]
        mix = _dot(ya_ref[0], wo_ref[0:sbw, :]) + _dot(yb_ref[0], wo_ref[sbw:, :])
        mix_ref[0] = mix
        x1, _, _ = _ln_fwd(ALPHA * x0_ref[0] + (1.0 + mod[2:3]) * mix, g_ref[...], b_ref[...])
        x1_ref[0] = x1
        h2_ref[0] = (x1 * (1.0 + mod[4:5]) + mod[3:4]).astype(BF16)

    return pl.pallas_call(
        body, name="outproj_fwd", grid=(B, S // tm),
        in_specs=[_tok(tm, sbw), _tok(tm, mla_y.shape[2]), _tok(tm, D), _perb(N_MOD, D),
                  _full(w_o), _full(ln_g), _full(ln_b)],
        out_specs=[_tok(tm, D)] * 3,
        out_shape=[_sds((B, S, D), F32), _sds((B, S, D), F32), _sds((B, S, D), BF16)],
        compiler_params=_cparams(("parallel", "parallel")),
    )(sb_y, mla_y, x0, mod, w_o, ln_g, ln_b)


def _stat_specs(B, D):
    specs = [pl.BlockSpec((1, 8, D), lambda b, s: (b, 0, 0)), pl.BlockSpec((8, D), lambda b, s: (0, 0))]
    shapes = [_sds((B, 8, D), F32), _sds((8, D), F32)]
    return specs, shapes


def _stat_init(bst_ref, wst_ref):
    @pl.when(pl.program_id(1) == 0)
    def _():
        bst_ref[...] = jnp.zeros_like(bst_ref)

    @pl.when((pl.program_id(0) == 0) & (pl.program_id(1) == 0))
    def _():
        wst_ref[...] = jnp.zeros_like(wst_ref)


def _mlp_fwd(h2, x1, mod, target, w_up, w_down, ln_g, ln_b, dm):
    B, S, D = x1.shape
    tm = dm["tm"]
    nck, _, ck = w_up.shape
    dff = nck * ck

    def body(h2_ref, x1_ref, mod_ref, t_ref, wu_ref, wd_ref, g_ref, b_ref, u_ref, dr_ref, bst_ref, wst_ref):
        _stat_init(bst_ref, wst_ref)
        mod = mod_ref[0]
        h2 = h2_ref[0]
        ff = jnp.zeros((tm, D), F32)
        for c in range(nck):
            u = _dot(h2, wu_ref[c])
            u_ref[0, :, c * ck:(c + 1) * ck] = u.astype(BF16)
            act = jnp.square(jnp.maximum(u, 0.0)).astype(BF16)
            ff = ff + _dot(act, wd_ref[c])
        g = g_ref[...]
        x2, xhat, rstd = _ln_fwd(ALPHA * x1_ref[0] + (1.0 + mod[5:6]) * ff, g, b_ref[...])
        err = x2 - t_ref[0]
        dy = err * (1.0 / D)
        dr = _ln_bwd(dy, xhat, rstd, g)
        dr_ref[0] = dr
        bst_ref[0, 0:1, :] += _colsum(dr * ff)
        wst_ref[0:1, :] += _colsum(dy * xhat)
        wst_ref[1:2, :] += _colsum(dy)
        wst_ref[2:3, :] += _colsum(err * err) * (0.5 / D)

    sspecs, sshapes = _stat_specs(B, D)
    return pl.pallas_call(
        body, name="mlp_fwd", grid=(B, S // tm),
        in_specs=[_tok(tm, D), _tok(tm, D), _perb(N_MOD, D), _tok(tm, D), _full(w_up), _full(w_down),
                  _full(ln_g), _full(ln_b)],
        out_specs=[_tok(tm, dff), _tok(tm, D)] + sspecs,
        out_shape=[_sds((B, S, dff), BF16), _sds((B, S, D), F32)] + sshapes,
        compiler_params=_cparams(("arbitrary", "arbitrary")),
    )(h2, x1, mod, target, w_up, w_down, ln_g, ln_b)


def _mlp_bwd(dr2, u, x1, x0, mix, mod, w_up, w_down, w_o, ln_g, dm):
    B, S, D = x1.shape
    tm = dm["tm_small"]
    sbw = dm["sbw"]
    nck, _, ck = w_up.shape
    dff = nck * ck

    def body(dr_ref, u_ref, x1_ref, x0_ref, mix_ref, mod_ref, wu_ref, wd_ref, wo_ref, g_ref,
             du_ref, dff_ref, dmix_ref, dx0_ref, dya_ref, dyb_ref, bst_ref, wst_ref):
        _stat_init(bst_ref, wst_ref)
        mod = mod_ref[0]
        dr2 = dr_ref[0]
        dffv = ((1.0 + mod[5:6]) * dr2).astype(BF16)
        dff_ref[0] = dffv
        dh2 = jnp.zeros((tm, D), F32)
        for c in range(nck):
            sl = slice(c * ck, (c + 1) * ck)
            da = _dot_nt(dffv, wd_ref[c])
            du = (da * (2.0 * jnp.maximum(u_ref[0, :, sl].astype(F32), 0.0))).astype(BF16)
            du_ref[0, :, sl] = du
            dh2 = dh2 + _dot_nt(du, wu_ref[c])
        x1 = x1_ref[0]
        dx1 = ALPHA * dr2 + dh2 * (1.0 + mod[4:5])
        bst_ref[0, 0:1, :] += _colsum(dh2 * x1)
        bst_ref[0, 1:2, :] += _colsum(dh2)
        mix = mix_ref[0]
        g = g_ref[...]
        _, xhat, rstd = _ln_fwd(ALPHA * x0_ref[0] + (1.0 + mod[2:3]) * mix, g, 0.0)
        dr1 = _ln_bwd(dx1, xhat, rstd, g)
        wst_ref[0:1, :] += _colsum(dx1 * xhat)
        wst_ref[1:2, :] += _colsum(dx1)
        bst_ref[0, 2:3, :] += _colsum(dr1 * mix)
        dx0_ref[0] = ALPHA * dr1
        dmix = ((1.0 + mod[2:3]) * dr1).astype(BF16)
        dmix_ref[0] = dmix
        dya_ref[0] = _dot_nt(dmix, wo_ref[0:sbw, :]).astype(BF16)
        dyb_ref[0] = _dot_nt(dmix, wo_ref[sbw:, :]).astype(BF16)

    sspecs, sshapes = _stat_specs(B, D)
    wa, wb = sbw, w_o.shape[0] - sbw
    return pl.pallas_call(
        body, name="mlp_bwd", grid=(B, S // tm),
        in_specs=[_tok(tm, D), _tok(tm, dff), _tok(tm, D), _tok(tm, D), _tok(tm, D), _perb(N_MOD, D),
                  _full(w_up), _full(w_down), _full(w_o), _full(ln_g)],
        out_specs=[_tok(tm, dff), _tok(tm, D), _tok(tm, D), _tok(tm, D), _tok(tm, wa), _tok(tm, wb)] + sspecs,
        out_shape=[_sds((B, S, dff), BF16), _sds((B, S, D), BF16), _sds((B, S, D), BF16), _sds((B, S, D), F32),
                   _sds((B, S, wa), BF16), _sds((B, S, wb), BF16)] + sshapes,
        compiler_params=_cparams(("arbitrary", "arbitrary")),
    )(dr2, u, x1, x0, mix, mod, w_up, w_down, w_o, ln_g)


def _inproj_bwd(x, x0, dx0a, mod, ln_g, dq, dk, dv, dqp, dkp, dmv, cq, ckv, w_in_p, w_uq_p, w_kv, gq, gkv,
                tc, ts1, ts2, dm):
    B, S, D = x.shape
    tm = dm["tm"]
    sbw, qr, kvr, nh = dm["sbw"], dm["qr"], dm["kvr"], dm["nh"]
    qpw = nh * HEAD_PAD
    dinp = w_in_p.shape[1]
    kvw = w_kv.shape[1]

    def body(x_ref, x0_ref, dx0a_ref, mod_ref, g_ref, dq_ref, dk_ref, dv_ref, dqp_ref, dkp_ref, dmv_ref,
             cq_ref, ckv_ref, win_ref, wuq_ref, wkv_ref, gq_ref, gkv_ref, tc_ref, ts1_ref, ts2_ref,
             gx_ref, dproj_ref, dqpre_ref, dkvo_ref, bst_ref, wst_ref):
        _stat_init(bst_ref, wst_ref)
        mod = mod_ref[0]
        c1, s1, s2 = tc_ref[...], ts1_ref[...], ts2_ref[...]
        c8, s18, s28 = jnp.tile(c1, (1, nh)), jnp.tile(s1, (1, nh)), jnp.tile(s2, (1, nh))
        dqpre = _rope_t(dqp_ref[0].astype(F32), c8, s18, s28).astype(BF16)
        dqpre_ref[0] = dqpre
        gq = gq_ref[...]
        cq = cq_ref[0]
        rq = lax.rsqrt(jnp.mean(cq * cq, axis=-1, keepdims=True) + RMS_EPS)
        dqn = _dot_nt(dqpre, wuq_ref[...])
        wst_ref[4:5, 0:qr] += _colsum(dqn * cq * rq)
        dqg = dqn * gq
        dcq = rq * dqg - cq * (rq * rq * rq) * jnp.mean(dqg * cq, axis=-1, keepdims=True)

        dkpre = _rope_t(dkp_ref[0].astype(F32), c8, s18, s28)
        dkr = dkpre[:, 0:HEAD_PAD]
        for h in range(1, nh):
            dkr = dkr + dkpre[:, h * HEAD_PAD:(h + 1) * HEAD_PAD]
        lane = lax.broadcasted_iota(jnp.int32, (tm, LANES), 1)
        dkr = jnp.where((lane >= MLA_NOPE) & (lane < MLA_NOPE + MLA_ROPE), dkr, 0.0)
        dkr = pltpu.roll(dkr, LANES - MLA_NOPE, 1)
        dkvo = jnp.concatenate([dkpre.astype(BF16), dmv_ref[0]], axis=1)
        dkvo_ref[0] = dkvo
        gkv = gkv_ref[...]
        ckv = ckv_ref[0]
        rkv = lax.rsqrt(jnp.mean(ckv * ckv, axis=-1, keepdims=True) + RMS_EPS)
        dkvn = _dot_nt(dkvo, wkv_ref[...])
        wst_ref[5:6, 0:kvr] += _colsum(dkvn * ckv * rkv)
        dkg = dkvn * gkv
        dckv = rkv * dkg - ckv * (rkv * rkv * rkv) * jnp.mean(dkg * ckv, axis=-1, keepdims=True)

        dproj = jnp.concatenate([dq_ref[0], dk_ref[0], dv_ref[0], dcq.astype(BF16), dckv.astype(BF16),
                                 dkr.astype(BF16)], axis=1)
        dproj_ref[0] = dproj
        dh = _dot_nt(dproj, win_ref[...])
        x0 = x0_ref[0]
        dx0 = dx0a_ref[0] + dh * (1.0 + mod[1:2])
        bst_ref[0, 0:1, :] += _colsum(dh * x0)
        bst_ref[0, 1:2, :] += _colsum(dh)
        g = g_ref[...]
        _, xhat, rstd = _ln_fwd(x_ref[0], g, 0.0)
        gx_ref[0] = _ln_bwd(dx0, xhat, rstd, g)
        wst_ref[0:1, :] += _colsum(dx0 * xhat)
        wst_ref[1:2, :] += _colsum(dx0)

    tab = pl.BlockSpec((tm, LANES), lambda b, s: (s, 0))
    sspecs, sshapes = _stat_specs(B, D)
    return pl.pallas_call(
        body, name="inproj_bwd", grid=(B, S // tm),
        in_specs=[_tok(tm, D), _tok(tm, D), _tok(tm, D), _perb(N_MOD, D), _full(ln_g),
                  _tok(tm, sbw), _tok(tm, sbw), _tok(tm, sbw), _tok(tm, qpw), _tok(tm, qpw), _tok(tm, nh * MLA_V),
                  _tok(tm, qr), _tok(tm, kvr), _full(w_in_p), _full(w_uq_p), _full(w_kv), _full(gq), _full(gkv),
                  tab, tab, tab],
        out_specs=[_tok(tm, D), _tok(tm, dinp), _tok(tm, qpw), _tok(tm, kvw)] + sspecs,
        out_shape=[_sds((B, S, D), F32), _sds((B, S, dinp), BF16), _sds((B, S, qpw), BF16),
                   _sds((B, S, kvw), BF16)] + sshapes,
        compiler_params=_cparams(("arbitrary", "arbitrary")),
    )(x, x0, dx0a, mod, ln_g, dq, dk, dv, dqp, dkp, dmv, cq, ckv, w_in_p, w_uq_p, w_kv, gq, gkv, tc, ts1, ts2)


def _tile_of(n, cap):
    if n <= cap:
        return n
    best = n
    for t in range(LANES, cap + 1, LANES):
        if n % t == 0:
            best = t
    return best


def _mm_tn(a, g, name, relu_sq=False, out_dtype=F32, col_blocks=None):
    T, K = a.shape
    N = g.shape[1]
    tt = 512 if T % 512 == 0 else T
    tk = _tile_of(K, 1024)
    tn = _tile_of(N, 1280)
    nt = T // tt
    bw = N // col_blocks if col_blocks else tn
    assert tn % bw == 0

    def body(a_ref, g_ref, o_ref, acc_ref):
        @pl.when(pl.program_id(2) == 0)
        def _():
            acc_ref[...] = jnp.zeros_like(acc_ref)

        av = a_ref[...]
        if relu_sq:
            av = jnp.square(jnp.maximum(av.astype(F32), 0.0)).astype(BF16)
        acc_ref[...] += _dot_tn(av, g_ref[...])

        @pl.when(pl.program_id(2) == nt - 1)
        def _():
            if col_blocks:
                for c in range(tn // bw):
                    o_ref[c] = acc_ref[:, c * bw:(c + 1) * bw].astype(out_dtype)
            else:
                o_ref[...] = acc_ref[...].astype(out_dtype)

    if col_blocks:
        out_spec = pl.BlockSpec((tn // bw, tk, bw), lambda i, j, t: (j, i, 0))
        out_shape = _sds((col_blocks, K, bw), out_dtype)
    else:
        out_spec = pl.BlockSpec((tk, tn), lambda i, j, t: (i, j))
        out_shape = _sds((K, N), out_dtype)
    return pl.pallas_call(
        body, name=name, grid=(K // tk, N // tn, nt),
        in_specs=[pl.BlockSpec((tt, tk), lambda i, j, t: (t, i)), pl.BlockSpec((tt, tn), lambda i, j, t: (t, j))],
        out_specs=out_spec, out_shape=out_shape,
        scratch_shapes=[pltpu.VMEM((tk, tn), F32)],
        compiler_params=_cparams(("parallel", "parallel", "arbitrary")),
    )(a, g)


def _reduce_adamw(parts, w, m, v, name):
    _, K, N = parts.shape
    tr = 256 if K % 256 == 0 else K

    def body(p_ref, w_ref, m_ref, v_ref, g_ref, d_ref, nm_ref, nv_ref):
        g = p_ref[0].astype(F32)
        for k in range(1, 4):
            g = g + p_ref[k].astype(F32)
        g_ref[0] = g
        d_ref[0], nm_ref[0], nv_ref[0] = _adamw(w_ref[0], g, m_ref[0], v_ref[0])

    spec = pl.BlockSpec((1, tr, N), lambda r: (0, r, 0))
    return pl.pallas_call(
        body, name=name, grid=(K // tr,),
        in_specs=[pl.BlockSpec((4, tr, N), lambda r: (0, r, 0)), spec, spec, spec],
        out_specs=[spec] * 4, out_shape=[_sds((1, K, N), F32)] * 4,
        compiler_params=_cparams(("parallel",)),
    )(parts, w, m, v)


def _finish(sm, dmod_all, dmod_my, cact_all, p_small, m_small, v_small, b_ada, m_b, v_b, w_ada, m_w, v_w):
    n0 = p_small.shape[1]
    n1 = sm.shape[1]
    d = cact_all.shape[1]

    def body(sm_ref, dma_ref, dmm_ref, ca_ref, p_ref, pm_ref, pv_ref, b_ref, bm_ref, bv_ref, w_ref, wm_ref, wv_ref,
             gs_ref, ds_ref, ms_ref, vs_ref, gb_ref, db_ref, mb_ref, vb_ref, gw_ref, dw_ref, mw_ref, vw_ref,
             loss_ref):
        gs = sm_ref[0:1, :]
        for k in range(1, N_DEV):
            gs = gs + sm_ref[k:k + 1, :]
        gs_ref[...] = gs
        ds_ref[...], ms_ref[...], vs_ref[...] = _adamw(p_ref[...], gs[:, 0:n0], pm_ref[...], pv_ref[...])
        loss_ref[...] = jnp.zeros((1, LANES), F32) + jnp.sum(gs[:, n1 - d:n1])
        gb = jnp.sum(dma_ref[...], axis=0, keepdims=True)
        gb_ref[...] = gb
        db_ref[...], mb_ref[...], vb_ref[...] = _adamw(b_ref[...], gb, bm_ref[...], bv_ref[...])
        gw = _dot_tn(ca_ref[...].astype(BF16), dmm_ref[...].astype(BF16))
        gw_ref[...] = gw
        dw_ref[...], mw_ref[...], vw_ref[...] = _adamw(w_ref[...], gw, wm_ref[...], wv_ref[...])

    s0 = _sds(p_small.shape, F32)
    sb = _sds(b_ada.shape, F32)
    sw = _sds(w_ada.shape, F32)
    return pl.pallas_call(
        body, name="finish_small",
        out_shape=[_sds((1, n1), F32), s0, s0, s0, sb, sb, sb, sb, sw, sw, sw, sw,
                   _sds((1, LANES), F32)],
        compiler_params=pltpu.CompilerParams(vmem_limit_bytes=VMEM_LIMIT),
    )(sm, dmod_all, dmod_my, cact_all, p_small, m_small, v_small, b_ada, m_b, v_b, w_ada, m_w, v_w)


def _pack(arrs, dtype, width):
    flat = jnp.concatenate([a.astype(dtype).reshape(-1) for a in arrs])
    rows = -(-flat.shape[0] // (256 * width)) * 256
    return jnp.pad(flat, (0, rows * width - flat.shape[0])).reshape(rows, width)


def _unpack(slab, shapes):
    flat = slab.reshape(-1)
    out, o = [], 0
    for s in shapes:
        n = math.prod(s)
        out.append(flat[o:o + n].reshape(s))
        o += n
    return out


def _rope_tables(S):
    inv_freq = 1.0 / (ROPE_BASE ** (jnp.arange(0, MLA_ROPE, 2, dtype=F32) / MLA_ROPE))
    ang = jnp.arange(S, dtype=F32)[:, None] * inv_freq[None, :]
    cos, sin = jnp.cos(ang), jnp.sin(ang)
    one = jnp.ones((S, MLA_NOPE), F32)
    z16 = jnp.zeros((S, 16), F32)
    z32 = jnp.zeros((S, 32), F32)
    z64 = jnp.zeros((S, MLA_NOPE), F32)
    tc = jnp.concatenate([one, cos, cos, jnp.ones((S, 32), F32)], axis=1)
    ts1 = jnp.concatenate([z64, -sin, z16, z32], axis=1)
    ts2 = jnp.concatenate([z64, z16, sin, z32], axis=1)
    return tc, ts1, ts2


def kernel(x, c, ln_in_g, ln_in_b, w_ada, b_ada, w_in, q_norm_g, kv_norm_g, w_uq, w_ukv, w_o, ln1_g, ln1_b, w_up, w_down, ln2_g, ln2_b, loss_target, m_ln_in_g, m_ln_in_b, m_w_ada, m_b_ada, m_w_in, m_q_norm_g, m_kv_norm_g, m_w_uq, m_w_ukv, m_w_o, m_ln1_g, m_ln1_b, m_w_up, m_w_down, m_ln2_g, m_ln2_b, v_ln_in_g, v_ln_in_b, v_w_ada, v_b_ada, v_w_in, v_q_norm_g, v_kv_norm_g, v_w_uq, v_w_ukv, v_w_o, v_ln1_g, v_ln1_b, v_w_up, v_w_down, v_ln2_g, v_ln2_b):
    B, S, D = x.shape
    sbw = D // 2
    mlw = D - sbw
    nh = mlw // MLA_V
    qr = w_uq.shape[1]
    kvr = w_ukv.shape[1]
    qk = MLA_NOPE + MLA_ROPE
    dff = w_up.shape[2] * N_DEV
    din = w_in.shape[2] * N_DEV
    tm = 512 if S % 512 == 0 else S
    tq = min(512, S // 2)
    dm = dict(tm=tm, tm_small=min(tm, 256), tq=tq, sbw=sbw, qr=qr, kvr=kvr, nh=nh)
    width = 1024 if D >= 1024 else LANES
    dev = 4 * lax.axis_index("x") + 2 * lax.axis_index("y") + lax.axis_index("c")

    nada = w_ada.shape[2]
    c_all = _all_gather([c], "gather_c")[0].reshape(N_DEV * B, D)
    b_loc = lax.dynamic_slice(b_ada, (0, dev * nada), (1, nada))
    cact_all, mod_part = _ada_partial(c_all, w_ada[0], b_loc)
    mod_all = _all_gather([mod_part], "gather_mod")[0]
    mod = lax.dynamic_slice(mod_all, (0, dev * B, 0), (N_DEV, B, nada))
    mod = jnp.swapaxes(mod, 0, 1).reshape(B, N_MOD, D)

    big = [w_in, w_uq, w_ukv, w_o, w_up, w_down]
    late_w, late_token = _chip_exchange_start([a[0].astype(BF16) for a in big[3:]], "gather_w_late_start", scatter=False)
    w_in8, w_uq8, w_ukv8 = _all_gather([a[0].astype(BF16) for a in big[:3]], "gather_w_first")
    cols = lambda a8: jnp.swapaxes(a8, 0, 1).reshape(a8.shape[1], N_DEV * a8.shape[2])
    w_in_p = jnp.pad(cols(w_in8), ((0, 0), (0, LANES - MLA_ROPE)))
    zpad = jnp.zeros((qr, nh, HEAD_PAD - qk), BF16)
    w_uq_p = jnp.concatenate([cols(w_uq8).reshape(qr, nh, qk), zpad], axis=2).reshape(qr, nh * HEAD_PAD)
    w_ukv_f = cols(w_ukv8)
    w_uk = w_ukv_f[:, :nh * MLA_NOPE].reshape(kvr, nh, MLA_NOPE)
    w_uk_p = jnp.concatenate([w_uk, jnp.zeros((kvr, nh, HEAD_PAD - MLA_NOPE), BF16)], axis=2)
    w_kv = jnp.concatenate([w_uk_p.reshape(kvr, nh * HEAD_PAD), w_ukv_f[:, nh * MLA_NOPE:]], axis=1)

    tc, ts1, ts2 = _rope_tables(S)
    g_in, b_in = ln_in_g.reshape(1, D), ln_in_b.reshape(1, D)
    (x0, h, sq, sk, sv, qp, kp, mv, cq, ckv, qn, kvn) = _inproj_fwd(
        x, mod, g_in, b_in, w_in_p, w_uq_p, w_kv, q_norm_g, kv_norm_g, tc, ts1, ts2, dm, late_token)
    sb_y, sb_tot = _sb_fwd(sq, sk, sv, dm)
    mla_y, mla_lse = _mla_fwd(qp, kp, mv, dm, sb_tot)
    late_by_chip = _chip_exchange_wait(late_w, mla_lse, "gather_w_late_wait")
    w_o8, w_up8, w_down8 = [b.reshape((N_DEV,) + b.shape[2:]) for b in _core_gather(late_by_chip, "gather_w_late_cores")]
    w_o_f = w_o8.reshape(D, D)
    mix, x1, h2 = _outproj_fwd(sb_y, mla_y, x0, mod, w_o_f, ln1_g, ln1_b, dm)
    u, dr2, bst_c, wst_c = _mlp_fwd(h2, x1, mod, loss_target, w_up8, w_down8, ln2_g, ln2_b, dm)

    du, dffb, dmixb, dx0a, dsb_y, dmla_y, bst_b, wst_b = _mlp_bwd(
        dr2, u, x1, x0, mix, mod, w_up8, w_down8, w_o_f, ln1_g, dm)
    T = B * S
    r2 = lambda a: a.reshape(T, a.shape[2])
    by_core = lambda a: a.reshape((4, 2) + a.shape[1:])
    g_o = jnp.concatenate([_mm_tn(r2(sb_y), r2(dmixb), "grad_w_o_sb", out_dtype=BF16),
                           _mm_tn(r2(mla_y), r2(dmixb), "grad_w_o_mla", out_dtype=BF16)], axis=0)
    g_up8 = _mm_tn(r2(h2), r2(du), "grad_w_up", out_dtype=BF16, col_blocks=N_DEV)
    g_down = _mm_tn(r2(u), r2(dffb), "grad_w_down", relu_sq=True, out_dtype=BF16)
    early = [g_o.reshape(N_DEV, D // N_DEV, D), g_up8, g_down.reshape(N_DEV, dff // N_DEV, D)]
    early_sum = _core_scatter_sum([by_core(a) for a in early], "scatter_g_early_cores")
    early_g, early_token = _chip_exchange_start(early_sum, "scatter_g_early_start", scatter=True)

    dsq, dsk, dsv = _sb_bwd(sq, sk, sv, sb_tot, dsb_y, dm, early_token)
    dqp, dkp, dmv = _mla_bwd(qp, kp, mv, mla_y, mla_lse, dmla_y, dm, dsq)
    grad_x, dproj, dqpre, dkvo, bst_a, wst_a = _inproj_bwd(
        x, x0, dx0a, mod, g_in, dsq, dsk, dsv, dqp, dkp, dmv, cq, ckv, w_in_p, w_uq_p, w_kv, q_norm_g, kv_norm_g,
        tc, ts1, ts2, dm)
    g_in_p = _mm_tn(r2(h), r2(dproj), "grad_w_in")
    g_uq_p = _mm_tn(r2(qn), r2(dqpre), "grad_w_uq")
    g_kv = _mm_tn(r2(kvn), r2(dkvo), "grad_w_kv")
    g_uq_f = g_uq_p.reshape(qr, nh, HEAD_PAD)[:, :, :qk].reshape(qr, nh * qk)
    g_uk = g_kv[:, :nh * HEAD_PAD].reshape(kvr, nh, HEAD_PAD)[:, :, :MLA_NOPE].reshape(kvr, nh * MLA_NOPE)
    g_ukv_f = jnp.concatenate([g_uk, g_kv[:, nh * HEAD_PAD:]], axis=1)
    early_quarter = _chip_exchange_wait(early_g, g_kv, "scatter_g_early_wait")

    def by_dest_cols(a):
        k, n = a.shape[0], a.shape[1] // N_DEV
        return jnp.swapaxes(a.reshape(k, N_DEV, n), 0, 1).astype(BF16)

    last = [by_dest_cols(g_in_p[:, :din]), by_dest_cols(g_uq_f), by_dest_cols(g_ukv_f)]
    last_sum = _core_scatter_sum([by_core(a) for a in last], "scatter_g_last_cores")
    quarter = list(_chip_exchange(last_sum, "scatter_g_last_chips", scatter=True)) + list(early_quarter)
    names = ["w_in", "w_uq", "w_ukv", "w_o", "w_up", "w_down"]
    moms = [m_w_in, m_w_uq, m_w_ukv, m_w_o, m_w_up, m_w_down]
    vars_ = [v_w_in, v_w_uq, v_w_ukv, v_w_o, v_w_up, v_w_down]
    res = [_reduce_adamw(p, w, m, v, "adamw_" + n) for p, w, m, v, n in zip(quarter, big, moms, vars_, names)]
    gb, db, nmb, nvb = ([r[i] for r in res] for i in range(4))

    dmod = jnp.concatenate([bst_a[:, 1], bst_a[:, 0], bst_b[:, 2], bst_b[:, 1], bst_b[:, 0], bst_c[:, 0]], axis=1)
    small = jnp.concatenate([wst_a[0], wst_a[1], wst_a[4, :qr], wst_a[5, :kvr], wst_b[0], wst_b[1],
                             wst_c[0], wst_c[1], wst_c[2]])
    n1 = small.shape[0]
    both = _all_gather([_pack([dmod, small], F32, LANES)], "gather_small")[0].reshape(N_DEV, -1)
    dmod_all = both[:, :B * N_MOD * D].reshape(N_DEV * B, N_MOD * D)
    sm = both[:, B * N_MOD * D:B * N_MOD * D + n1]
    dmod_my = lax.dynamic_slice(dmod_all, (0, dev * nada), (N_DEV * B, nada))
    row = lambda arrs: jnp.concatenate([a.reshape(1, -1) for a in arrs], axis=1)
    smalls = [ln_in_g, ln_in_b, q_norm_g, kv_norm_g, ln1_g, ln1_b, ln2_g, ln2_b]
    small_shapes = [a.shape for a in smalls]
    (gs, ds, nms, nvs, g_b, d_b, nm_b, nv_b, g_w, d_w, nm_w, nv_w, loss_v) = _finish(
        sm, dmod_all, dmod_my, cact_all, row(smalls),
        row([m_ln_in_g, m_ln_in_b, m_q_norm_g, m_kv_norm_g, m_ln1_g, m_ln1_b, m_ln2_g, m_ln2_b]),
        row([v_ln_in_g, v_ln_in_b, v_q_norm_g, v_kv_norm_g, v_ln1_g, v_ln1_b, v_ln2_g, v_ln2_b]),
        b_ada, m_b_ada, v_b_ada, w_ada[0], m_w_ada[0], v_w_ada[0])
    gsm, dsm, nmsm, nvsm = (_unpack(s, small_shapes) for s in (gs, ds, nms, nvs))

    def ordered(sm_l, w_l, ada_w, ada_b):
        return [sm_l[0], sm_l[1], ada_w[None], ada_b, w_l[0], sm_l[2], sm_l[3], w_l[1], w_l[2], w_l[3],
                sm_l[4], sm_l[5], w_l[4], w_l[5], sm_l[6], sm_l[7]]

    loss = loss_v[0, 0]
    return (loss, grad_x, *ordered(gsm, gb, g_w, g_b), *ordered(dsm, db, d_w, d_b),
            *ordered(nmsm, nmb, nm_w, nm_b), *ordered(nvsm, nvb, nv_w, nv_b))
```

```python
import functools
import math

import jax
import jax.numpy as jnp
from jax import lax
from jax.experimental import pallas as pl
from jax.experimental.pallas import tpu as pltpu

F32 = jnp.float32
BF16 = jnp.bfloat16

SB_HD = 64
MLA_V = 64
MLA_NOPE = 64
MLA_ROPE = 32
HEAD_PAD = 128
CHUNK = 64
ROPE_BASE = 10000.0
LN_EPS = 1e-5
RMS_EPS = 1e-6
DEPTH = 1
ALPHA = (2.0 * DEPTH) ** 0.25
N_MOD = 6
ADAM_LR = 0.001
ADAM_B1 = 0.9
ADAM_B2 = 0.999
ADAM_EPS = 1e-08
ADAM_WD = 0.01
ADAM_STEP = 10
N_DEV = 8
LANES = 128
LOG2E = 1.4426950408889634
CUMSUM_W = 256
VMEM_LIMIT = 56 * 1024 * 1024
MESH = pl.DeviceIdType.MESH


def _dot(a, b):
    return jnp.dot(a, b, preferred_element_type=F32)


def _dot_nt(a, b):
    return lax.dot_general(a, b, (((1,), (1,)), ((), ())), preferred_element_type=F32)


def _dot_tn(a, b):
    return lax.dot_general(a, b, (((0,), (0,)), ((), ())), preferred_element_type=F32)


def _cparams(sem):
    return pltpu.CompilerParams(dimension_semantics=sem, vmem_limit_bytes=VMEM_LIMIT)


def _full(a):
    nd = a.ndim
    return pl.BlockSpec(a.shape, lambda *_: (0,) * nd, pipeline_mode=pl.Buffered(1))


def _tok(tm, w):
    return pl.BlockSpec((1, tm, w), lambda b, s: (b, s, 0))


def _perb(rows, w):
    return pl.BlockSpec((1, rows, w), lambda b, s: (b, 0, 0))


def _sds(shape, dtype):
    return jax.ShapeDtypeStruct(shape, dtype)


def _ln_fwd(x, g, b):
    mu = jnp.mean(x, axis=-1, keepdims=True)
    xc = x - mu
    var = jnp.mean(xc * xc, axis=-1, keepdims=True)
    rstd = lax.rsqrt(var + LN_EPS)
    xhat = xc * rstd
    return xhat * g + b, xhat, rstd


def _ln_bwd(dy, xhat, rstd, g):
    dxh = dy * g
    m1 = jnp.mean(dxh, axis=-1, keepdims=True)
    m2 = jnp.mean(dxh * xhat, axis=-1, keepdims=True)
    return rstd * (dxh - m1 - xhat * m2)


def _colsum(a):
    return jnp.sum(a, axis=0, keepdims=True)


def _rope(x, c, s1, s2):
    w = x.shape[-1]
    return x * c + pltpu.roll(x, w - 16, 1) * s1 + pltpu.roll(x, 16, 1) * s2


def _rope_t(x, c, s1, s2):
    w = x.shape[-1]
    return x * c - pltpu.roll(x, w - 16, 1) * s1 - pltpu.roll(x, 16, 1) * s2


def _adamw(w, g, m, v):
    m = ADAM_B1 * m + (1.0 - ADAM_B1) * g
    v = ADAM_B2 * v + (1.0 - ADAM_B2) * (g * g)
    m_hat = m / (1.0 - ADAM_B1 ** ADAM_STEP)
    v_hat = v / (1.0 - ADAM_B2 ** ADAM_STEP)
    delta = -ADAM_LR * (m_hat / (jnp.sqrt(v_hat) + ADAM_EPS) + ADAM_WD * w)
    return delta, m, v


def _my_place():
    return lax.axis_index("x"), lax.axis_index("y"), lax.axis_index("c")


def _chip_peers(mx, my):
    out = []
    for j in (1, 2, 3):
        px = 1 - mx if (j >> 1) else mx
        py = 1 - my if (j & 1) else my
        out.append((px, py, 2 * px + py))
    return out


def _hbm_call(body, name, n_in, out_shape, sems):
    hbm = pl.BlockSpec(memory_space=pl.ANY)
    return pl.pallas_call(
        body, name=name, out_shape=out_shape,
        in_specs=[hbm] * n_in, out_specs=[hbm] * len(out_shape),
        scratch_shapes=[pltpu.SemaphoreType.DMA(s) for s in sems])


def _chip_exchange(xs, name, scatter):
    n = len(xs)

    def body(*refs):
        x_refs, o_refs = refs[:n], refs[n:2 * n]
        ssem, rsem, lsem = refs[2 * n:]
        mx, my, mc = _my_place()
        me = 2 * mx + my
        peers = _chip_peers(mx, my)

        def copy(i, j, src_slot, dst_slot):
            px, py, _ = peers[j]
            return pltpu.make_async_remote_copy(
                src_ref=x_refs[i].at[src_slot] if scatter else x_refs[i], dst_ref=o_refs[i].at[dst_slot],
                send_sem=ssem.at[i, j], recv_sem=rsem.at[i, j], device_id=(px, py, mc), device_id_type=MESH)

        local = [pltpu.make_async_copy(x_refs[i].at[me] if scatter else x_refs[i], o_refs[i].at[me], lsem.at[i])
                 for i in range(n)]
        sends = [copy(i, j, peers[j][2], me) for i in range(n) for j in range(3)]
        for cp in local + sends:
            cp.start()
        for i in range(n):
            for j in range(3):
                copy(i, j, peers[j][2], peers[j][2]).wait_recv()
        for cp in sends:
            cp.wait_send()
        for cp in local:
            cp.wait()

    out_shape = [_sds((4,) + tuple(x.shape[1:] if scatter else x.shape), x.dtype) for x in xs]
    return _hbm_call(body, name, n, out_shape, [(n, 3), (n, 3), (n,)])(*xs)


def _chip_exchange_start(xs, name, scatter):
    n = len(xs)
    blks = [tuple(x.shape[1:] if scatter else x.shape) for x in xs]

    def body(*refs):
        x_refs, land_refs = refs[:n], refs[n:2 * n]
        ssem, rsem = refs[2 * n], refs[2 * n + 1]
        token = refs[-1]
        mx, my, mc = _my_place()
        me = 2 * mx + my
        for i in range(n):
            for j, (px, py, pk) in enumerate(_chip_peers(mx, my)):
                pltpu.make_async_remote_copy(
                    src_ref=x_refs[i].at[pk] if scatter else x_refs[i], dst_ref=land_refs[i].at[me],
                    send_sem=ssem.at[3 * i + j], recv_sem=rsem.at[3 * i + j], device_id=(px, py, mc),
                    device_id_type=MESH).start()
        token[...] = jnp.zeros_like(token)

    hbm = pl.BlockSpec(memory_space=pltpu.HBM)
    sem = pl.BlockSpec(memory_space=pltpu.SEMAPHORE)
    lands = [lax.empty((4,) + b, x.dtype) for b, x in zip(blks, xs)]
    res = pl.pallas_call(
        body, name=name,
        out_shape=[pltpu.SemaphoreType.DMA((3 * n,)), pltpu.SemaphoreType.DMA((3 * n,))]
        + [pltpu.HBM(x.shape, x.dtype) for x in xs] + [pltpu.HBM(l.shape, l.dtype) for l in lands]
        + [_sds((8, LANES), F32)],
        in_specs=[hbm] * (2 * n),
        out_specs=[sem, sem] + [hbm] * (2 * n) + [pl.BlockSpec(memory_space=pltpu.VMEM)],
        input_output_aliases={i: 2 + i for i in range(2 * n)},
        compiler_params=pltpu.CompilerParams(has_side_effects=pltpu.SideEffectType.DATAFLOW_SIDE_EFFECTING),
    )(*[pltpu.with_memory_space_constraint(a, pltpu.HBM) for a in list(xs) + lands])
    return dict(ssem=res[0], rsem=res[1], xs=res[2:2 + n], lands=res[2 + n:2 + 2 * n], n=n, scatter=scatter), res[-1]


def _chip_exchange_wait(handle, after, name):
    n, scatter = handle["n"], handle["scatter"]

    def body(*refs):
        x_refs, land_refs = refs[:n], refs[n:2 * n]
        ssem, rsem = refs[2 * n], refs[2 * n + 1]
        mx, my, mc = _my_place()
        for i in range(n):
            for j, (px, py, pk) in enumerate(_chip_peers(mx, my)):
                cp = pltpu.make_async_remote_copy(
                    src_ref=x_refs[i].at[pk] if scatter else x_refs[i], dst_ref=land_refs[i].at[pk],
                    send_sem=ssem.at[3 * i + j], recv_sem=rsem.at[3 * i + j], device_id=(px, py, mc),
                    device_id_type=MESH)
                cp.wait_send()
                cp.wait_recv()

    hbm = pl.BlockSpec(memory_space=pltpu.HBM)
    sem = pl.BlockSpec(memory_space=pltpu.SEMAPHORE)
    ops = list(handle["xs"]) + list(handle["lands"])
    res = pl.pallas_call(
        body, name=name,
        out_shape=[pltpu.HBM(a.shape, a.dtype) for a in ops],
        in_specs=[hbm] * (2 * n) + [sem, sem, pl.BlockSpec(memory_space=pl.ANY)],
        out_specs=[hbm] * (2 * n),
        input_output_aliases={i: i for i in range(2 * n)},
        compiler_params=pltpu.CompilerParams(has_side_effects=pltpu.SideEffectType.DATAFLOW_SIDE_EFFECTING),
    )(*ops, handle["ssem"], handle["rsem"], after)
    me = 2 * lax.axis_index("x") + lax.axis_index("y")
    out = []
    for x, land in zip(res[:n], res[n:]):
        own = lax.dynamic_index_in_dim(x, me, 0, keepdims=False) if scatter else x
        out.append(lax.dynamic_update_index_in_dim(land, own, me, 0))
    return out


def _core_gather(xs, name):
    n = len(xs)

    def body(*refs):
        x_refs, o_refs, mine, got = refs[:n], refs[n:2 * n], refs[2 * n:3 * n], refs[3 * n:4 * n]
        lsem, ssem, rsem, osem = refs[4 * n:]
        mx, my, mc = _my_place()
        loads = [pltpu.make_async_copy(x_refs[i], mine[i], lsem.at[i]) for i in range(n)]
        for cp in loads:
            cp.start()
        sends, stores = [], []
        for i in range(n):
            loads[i].wait()
            cp = pltpu.make_async_remote_copy(
                src_ref=mine[i], dst_ref=got[i], send_sem=ssem.at[i], recv_sem=rsem.at[i],
                device_id=(mx, my, 1 - mc), device_id_type=MESH)
            cp.start()
            sends.append(cp)
            for k in range(4):
                st = pltpu.make_async_copy(mine[i].at[k], o_refs[i].at[k, mc], osem.at[i, k])
                st.start()
                stores.append(st)
        for i in range(n):
            sends[i].wait_recv()
            for k in range(4):
                st = pltpu.make_async_copy(got[i].at[k], o_refs[i].at[k, 1 - mc], osem.at[n + i, k])
                st.start()
                stores.append(st)
        for cp in sends:
            cp.wait_send()
        for st in stores:
            st.wait()

    hbm = pl.BlockSpec(memory_space=pl.ANY)
    bufs = [pltpu.VMEM(x.shape, x.dtype) for x in xs]
    return pl.pallas_call(
        body, name=name,
        out_shape=[_sds((4, 2) + tuple(x.shape[1:]), x.dtype) for x in xs],
        in_specs=[hbm] * n, out_specs=[hbm] * n,
        scratch_shapes=bufs + bufs + [pltpu.SemaphoreType.DMA((n,)), pltpu.SemaphoreType.DMA((n,)),
                                      pltpu.SemaphoreType.DMA((n,)), pltpu.SemaphoreType.DMA((2 * n, 4))],
        compiler_params=pltpu.CompilerParams(vmem_limit_bytes=VMEM_LIMIT),
    )(*xs)


def _rows_step(k):
    for r in (256, 128, 64, 32, 16, 8):
        if k % r == 0:
            return r
    return k


def _core_scatter_sum(gs, name):
    n = len(gs)

    def body(*refs):
        g_refs, o_refs = refs[:n], refs[n:2 * n]
        send, got, mine = refs[2 * n:3 * n], refs[3 * n:4 * n], refs[4 * n:5 * n]
        lsem, msem, ssem, rsem, osem = refs[5 * n:]
        mx, my, mc = _my_place()
        pairs = [(i, k) for i in range(n) for k in range(4)]
        out_loads = {(i, k): pltpu.make_async_copy(g_refs[i].at[k, 1 - mc], send[i].at[k], lsem.at[i, k])
                     for i, k in pairs}
        own_loads = {(i, k): pltpu.make_async_copy(g_refs[i].at[k, mc], mine[i].at[k], msem.at[i, k])
                     for i, k in pairs}
        for p in pairs:
            out_loads[p].start()
        for p in pairs:
            own_loads[p].start()
        sends = []
        for i in range(n):
            for k in range(4):
                out_loads[i, k].wait()
            cp = pltpu.make_async_remote_copy(
                src_ref=send[i], dst_ref=got[i], send_sem=ssem.at[i], recv_sem=rsem.at[i],
                device_id=(mx, my, 1 - mc), device_id_type=MESH)
            cp.start()
            sends.append(cp)
        stores = []
        for i in range(n):
            for k in range(4):
                own_loads[i, k].wait()
            sends[i].wait_recv()
            rows = g_refs[i].shape[2]
            step = _rows_step(rows)

            def add(r, _, i=i, step=step):
                sl = pl.ds(pl.multiple_of(r * step, step), step)
                for k in range(4):
                    mine[i][k, sl, :] = (mine[i][k, sl, :].astype(F32) + got[i][k, sl, :].astype(F32)).astype(BF16)
                return 0

            lax.fori_loop(0, rows // step, add, 0)
            st = pltpu.make_async_copy(mine[i], o_refs[i], osem.at[i])
            st.start()
            stores.append(st)
        for cp in sends:
            cp.wait_send()
        for st in stores:
            st.wait()

    hbm = pl.BlockSpec(memory_space=pl.ANY)
    blk = [(4,) + tuple(g.shape[2:]) for g in gs]
    bufs = [pltpu.VMEM(b, BF16) for b in blk]
    return pl.pallas_call(
        body, name=name,
        out_shape=[_sds(b, BF16) for b in blk],
        in_specs=[hbm] * n, out_specs=[hbm] * n,
        scratch_shapes=bufs * 3 + [pltpu.SemaphoreType.DMA((n, 4)), pltpu.SemaphoreType.DMA((n, 4)),
                                   pltpu.SemaphoreType.DMA((n,)), pltpu.SemaphoreType.DMA((n,)),
                                   pltpu.SemaphoreType.DMA((n,))],
        compiler_params=pltpu.CompilerParams(vmem_limit_bytes=VMEM_LIMIT),
    )(*gs)


def _all_gather(xs, name):
    by_chip = _chip_exchange(xs, name + "_chips", scatter=False)
    both = _core_gather(by_chip, name + "_cores")
    return [b.reshape((N_DEV,) + tuple(x.shape)) for b, x in zip(both, xs)]


def _ada_partial(c_all, w_ada_loc, b_loc):
    def body(c_ref, w_ref, b_ref, act_ref, mod_ref):
        c = c_ref[...]
        act = c * (1.0 / (1.0 + jnp.exp(-c)))
        act_ref[...] = act
        mod_ref[...] = _dot(act.astype(BF16), w_ref[...].astype(BF16)) + b_ref[...]

    nb, d = c_all.shape
    return pl.pallas_call(
        body, name="ada_partial",
        out_shape=(_sds((nb, d), F32), _sds((nb, w_ada_loc.shape[1]), F32)),
        compiler_params=pltpu.CompilerParams(vmem_limit_bytes=VMEM_LIMIT),
    )(c_all, w_ada_loc, b_loc)


_AFTER = pl.BlockSpec(memory_space=pl.ANY)


def _inproj_fwd(x, mod, ln_g, ln_b, w_in_p, w_uq_p, w_kv, gq, gkv, tc, ts1, ts2, dm, after):
    B, S, D = x.shape
    tm = dm["tm"]
    sbw, qr, kvr, nh = dm["sbw"], dm["qr"], dm["kvr"], dm["nh"]
    o_cq, o_ckv, o_kr = 3 * sbw, 3 * sbw + qr, 3 * sbw + qr + kvr
    qpw = nh * HEAD_PAD

    def body(x_ref, mod_ref, g_ref, b_ref, win_ref, wuq_ref, wkv_ref, gq_ref, gkv_ref, tc_ref, ts1_ref, ts2_ref, _,
             x0_ref, h_ref, q_ref, k_ref, v_ref, qp_ref, kp_ref, mv_ref, cq_ref, ckv_ref, qn_ref, kvn_ref):
        x0, _, _ = _ln_fwd(x_ref[0], g_ref[...], b_ref[...])
        x0_ref[0] = x0
        mod = mod_ref[0]
        h = (x0 * (1.0 + mod[1:2]) + mod[0:1]).astype(BF16)
        h_ref[0] = h
        proj = _dot(h, win_ref[...])
        q_ref[0] = (proj[:, 0:sbw] * (SB_HD ** -0.5)).astype(BF16)
        k_ref[0] = proj[:, sbw:2 * sbw].astype(BF16)
        v_ref[0] = proj[:, 2 * sbw:3 * sbw].astype(BF16)
        cq = proj[:, o_cq:o_cq + qr]
        ckv = proj[:, o_ckv:o_ckv + kvr]
        cq_ref[0] = cq
        ckv_ref[0] = ckv
        qn = (cq * lax.rsqrt(jnp.mean(cq * cq, axis=-1, keepdims=True) + RMS_EPS) * gq_ref[...]).astype(BF16)
        kvn = (ckv * lax.rsqrt(jnp.mean(ckv * ckv, axis=-1, keepdims=True) + RMS_EPS) * gkv_ref[...]).astype(BF16)
        qn_ref[0] = qn
        kvn_ref[0] = kvn
        c1, s1, s2 = tc_ref[...], ts1_ref[...], ts2_ref[...]
        c8, s18, s28 = jnp.tile(c1, (1, nh)), jnp.tile(s1, (1, nh)), jnp.tile(s2, (1, nh))
        qp_ref[0] = _rope(_dot(qn, wuq_ref[...]), c8, s18, s28).astype(BF16)
        kvo = _dot(kvn, wkv_ref[...])
        kr = pltpu.roll(proj[:, o_kr:o_kr + LANES], 64, 1)
        kr = _rope(kr, c1, s1, s2)
        kp_ref[0] = (kvo[:, 0:qpw] + jnp.tile(kr, (1, nh))).astype(BF16)
        mv_ref[0] = kvo[:, qpw:].astype(BF16)

    tab = pl.BlockSpec((tm, LANES), lambda b, s: (s, 0))
    outs = [(D, F32), (D, BF16), (sbw, BF16), (sbw, BF16), (sbw, BF16), (qpw, BF16), (qpw, BF16),
            (nh * MLA_V, BF16), (qr, F32), (kvr, F32), (qr, BF16), (kvr, BF16)]
    return pl.pallas_call(
        body, name="inproj_fwd", grid=(B, S // tm),
        in_specs=[_tok(tm, D), _perb(N_MOD, D), _full(ln_g), _full(ln_b), _full(w_in_p), _full(w_uq_p),
                  _full(w_kv), _full(gq), _full(gkv), tab, tab, tab, _AFTER],
        out_specs=[_tok(tm, w) for w, _ in outs],
        out_shape=[_sds((B, S, w), t) for w, t in outs],
        compiler_params=_cparams(("parallel", "parallel")),
    )(x, mod, ln_g, ln_b, w_in_p, w_uq_p, w_kv, gq, gkv, tc, ts1, ts2, after)


def _neg_abs(x):
    sign = jnp.uint32(0x80000000)
    return lax.bitcast_convert_type(lax.bitcast_convert_type(x, jnp.uint32) | sign, F32)


def _log2_keep(z):
    zs = z * (-LOG2E)
    return jnp.minimum(zs, 0.0) - jnp.log2(1.0 + jnp.exp2(_neg_abs(zs))), zs


def _split_dot(a, u):
    hi = a.astype(BF16)
    lo = (a - hi.astype(F32)).astype(BF16)
    return _dot(hi, u) + _dot(lo, u)


def _tri(n, rel):
    row = lax.broadcasted_iota(jnp.int32, (n, n), 0)
    col = lax.broadcasted_iota(jnp.int32, (n, n), 1)
    return rel(row, col).astype(BF16)


def _running_sum(a, tri, reverse, split):
    cs = tri.shape[0]
    n = a.shape[1] // cs
    out = [None] * n
    run = None
    for c in (reversed(range(n)) if reverse else range(n)):
        part = a[:, c * cs:(c + 1) * cs]
        loc = _split_dot(part, tri) if split else _dot(part.astype(BF16), tri)
        out[c] = loc if run is None else loc + run
        tot = jnp.sum(part, axis=1, keepdims=True)
        run = tot if run is None else run + tot
    return (out[0] if n == 1 else jnp.concatenate(out, axis=1)), run


def _sb_fwd(q, k, v, dm):
    B, S, W = q.shape
    tq = dm["tq"]
    nq = S // tq

    def body(q_ref, k_ref, v_ref, y_ref, tot_ref):
        qi = pl.program_id(2)
        q2 = q_ref[0]
        lane = lax.broadcasted_iota(jnp.int32, (tq, LANES), 1)
        qh = [jnp.where(lane < SB_HD, q2, 0).astype(BF16), jnp.where(lane >= SB_HD, q2, 0).astype(BF16)]
        row = lax.broadcasted_iota(jnp.int32, (tq, tq), 0)
        col = lax.broadcasted_iota(jnp.int32, (tq, tq), 1)
        later = _tri(min(tq, CUMSUM_W), lambda j, s: j > s)
        strict = col < row

        def block(j, carry, masked):
            off = pl.multiple_of(j * tq, tq)
            k2 = k_ref[0, pl.ds(off, tq), :]
            v2 = v_ref[0, pl.ds(off, tq), :]
            heads = range(2)
            z = [_dot_nt(qh[h], k2) for h in heads]
            az = [_log2_keep(z[h]) for h in heads]
            a = [jnp.where(strict, az[h][0], 0.0) if masked else az[h][0] for h in heads]
            rs = [_running_sum(a[h], later, reverse=True, split=True) for h in heads]
            w = [jnp.exp2((a[h] - az[h][1]) + rs[h][0] + carry[2 * h + 1]) for h in heads]
            if masked:
                w = [jnp.where(strict, w[h], 0.0) for h in heads]
            acc = [carry[2 * h] + _dot(w[h].astype(BF16), v2) for h in heads]
            return acc[0], carry[1] + rs[0][1], acc[1], carry[3] + rs[1][1]

        zero = jnp.zeros((tq, LANES), F32)
        zrun = jnp.zeros((tq, 1), F32)
        carry = block(qi, (zero, zrun, zero, zrun), True)
        carry = lax.fori_loop(0, qi, lambda jj, c: block(qi - 1 - jj, c, False), carry)
        y_ref[0] = jnp.where(lane < SB_HD, carry[0], carry[2]).astype(BF16)
        tot_ref[0] = jnp.where(lane < SB_HD, carry[1], carry[3])

    qspec = pl.BlockSpec((1, tq, LANES), lambda b, hp, i: (b, i, hp))
    kspec = pl.BlockSpec((1, S, LANES), lambda b, hp, i: (b, 0, hp))
    return pl.pallas_call(
        body, name="sb_fwd", grid=(B, W // LANES, nq),
        in_specs=[qspec, kspec, kspec],
        out_specs=[qspec, qspec],
        out_shape=[_sds((B, S, W), BF16), _sds((B, S, W), F32)],
        compiler_params=_cparams(("parallel", "parallel", "arbitrary")),
    )(q, k, v)


def _sb_bwd(q, k, v, tot, dy, dm, after):
    B, S, W = q.shape
    tq = dm["tq"]
    nq = S // tq

    def body(q_ref, k_ref, v_ref, tot_ref, dy_ref, _, dq_ref, dk_ref, dv_ref, dk_acc, dv_acc):
        qi = pl.program_id(2)

        @pl.when(qi == 0)
        def _():
            dk_acc[...] = jnp.zeros_like(dk_acc)
            dv_acc[...] = jnp.zeros_like(dv_acc)

        q2 = q_ref[0]
        dy2 = dy_ref[0]
        tot2 = tot_ref[0]
        lane = lax.broadcasted_iota(jnp.int32, (tq, LANES), 1)
        in_h = [lane < SB_HD, lane >= SB_HD]
        qh = [jnp.where(m, q2, 0).astype(BF16) for m in in_h]
        dyh = [jnp.where(m, dy2, 0).astype(BF16) for m in in_h]
        toth = [tot2[:, 0:1], tot2[:, SB_HD:SB_HD + 1]]
        row = lax.broadcasted_iota(jnp.int32, (tq, tq), 0)
        col = lax.broadcasted_iota(jnp.int32, (tq, tq), 1)
        upto = _tri(min(tq, CUMSUM_W), lambda j, s: j <= s)
        before = _tri(min(tq, CUMSUM_W), lambda s, j: s < j)
        strict = col < row

        def block(j, carry, masked):
            off = pl.multiple_of(j * tq, tq)
            k2 = k_ref[0, pl.ds(off, tq), :]
            v2 = v_ref[0, pl.ds(off, tq), :]
            new = []
            dk_blk = jnp.zeros((tq, LANES), F32)
            dv_blk = jnp.zeros((tq, LANES), F32)
            for h in range(2):
                dq, pa, pg = carry[3 * h], carry[3 * h + 1], carry[3 * h + 2]
                a, zs = _log2_keep(_dot_nt(qh[h], k2))
                if masked:
                    a = jnp.where(strict, a, 0.0)
                a_upto, a_tot = _running_sum(a, upto, reverse=False, split=True)
                w = jnp.exp2((a - zs) + ((toth[h] - pa) - a_upto))
                if masked:
                    w = jnp.where(strict, w, 0.0)
                g = _dot_nt(dyh[h], v2) * w
                g_before, g_tot = _running_sum(g, before, reverse=False, split=False)
                g_before = g_before + pg
                dz = (g + g_before) * jnp.exp2(a) - g_before
                if masked:
                    dz = jnp.where(strict, dz, 0.0)
                dzb = dz.astype(BF16)
                dv_blk = dv_blk + _dot_tn(w.astype(BF16), dyh[h])
                dk_blk = dk_blk + _dot_tn(dzb, qh[h])
                new += [dq + _dot(dzb, k2), pa + a_tot, pg + g_tot]
            dk_acc[pl.ds(off, tq), :] += dk_blk
            dv_acc[pl.ds(off, tq), :] += dv_blk
            return tuple(new)

        zero = jnp.zeros((tq, LANES), F32)
        zrun = jnp.zeros((tq, 1), F32)
        carry = lax.fori_loop(0, qi, lambda j, c: block(j, c, False), (zero, zrun, zrun, zero, zrun, zrun))
        carry = block(qi, carry, True)
        dq_ref[0] = (jnp.where(in_h[0], carry[0], carry[3]) * (SB_HD ** -0.5)).astype(BF16)

        @pl.when(qi == nq - 1)
        def _():
            dk_ref[0] = dk_acc[...].astype(BF16)
            dv_ref[0] = dv_acc[...].astype(BF16)

    qspec = pl.BlockSpec((1, tq, LANES), lambda b, hp, i: (b, i, hp))
    kspec = pl.BlockSpec((1, S, LANES), lambda b, hp, i: (b, 0, hp))
    return pl.pallas_call(
        body, name="sb_bwd", grid=(B, W // LANES, nq),
        in_specs=[qspec, kspec, kspec, qspec, qspec, _AFTER],
        out_specs=[qspec, kspec, kspec],
        out_shape=[_sds((B, S, W), BF16)] * 3,
        scratch_shapes=[pltpu.VMEM((S, LANES), F32), pltpu.VMEM((S, LANES), F32)],
        compiler_params=_cparams(("parallel", "parallel", "arbitrary")),
    )(q, k, v, tot, dy, after)


def _chunk_mask(tq):
    row = lax.broadcasted_iota(jnp.int32, (tq, tq), 0)
    col = lax.broadcasted_iota(jnp.int32, (tq, tq), 1)
    return lax.shift_right_logical(col, 6) <= lax.shift_right_logical(row, 6)


def _mla_fwd(qp, kp, mv, dm, after):
    B, S, QW = qp.shape
    VW = mv.shape[2]
    tq = dm["tq"]
    nq = S // tq
    scale = (MLA_NOPE + MLA_ROPE) ** -0.5
    assert CHUNK == 64

    def body(q_ref, k_ref, v_ref, _, y_ref, lse_ref):
        qi = pl.program_id(2)
        q2 = q_ref[0]
        lane = lax.broadcasted_iota(jnp.int32, (tq, LANES), 1)
        allowed = _chunk_mask(tq)

        def block(j, carry, masked):
            off = pl.multiple_of(j * tq, tq)
            v2 = v_ref[0, pl.ds(off, tq), :]
            heads = range(2)
            sl = [slice(h * HEAD_PAD, (h + 1) * HEAD_PAD) for h in heads]
            s = [_dot_nt(q2[:, sl[h]], k_ref[0, pl.ds(off, tq), sl[h]]) * (scale * LOG2E) for h in heads]
            if masked:
                s = [jnp.where(allowed, s[h], -1e30) for h in heads]
            m_new = [jnp.maximum(carry[3 * h + 1], jnp.max(s[h], axis=1, keepdims=True)) for h in heads]
            alpha = [jnp.exp2(carry[3 * h + 1] - m_new[h]) for h in heads]
            p = [jnp.exp2(s[h] - m_new[h]) for h in heads]
            acc = [alpha[h] * carry[3 * h] + _dot(p[h].astype(BF16), v2) for h in heads]
            l = [alpha[h] * carry[3 * h + 2] + jnp.sum(p[h], axis=1, keepdims=True) for h in heads]
            return acc[0], m_new[0], l[0], acc[1], m_new[1], l[1]

        zero = jnp.zeros((tq, LANES), F32)
        m0 = jnp.full((tq, 1), -1e30, F32)
        l0 = jnp.zeros((tq, 1), F32)
        carry = block(qi, (zero, m0, l0, zero, m0, l0), True)
        carry = lax.fori_loop(0, qi, lambda j, c: block(j, c, False), carry)
        y0 = carry[0] / carry[2]
        y1 = carry[3] / carry[5]
        y_ref[0] = jnp.where(lane < MLA_V, y0, y1).astype(BF16)
        lse_ref[0] = jnp.where(lane < MLA_V, carry[1] + jnp.log2(carry[2]), carry[4] + jnp.log2(carry[5]))

    qspec = pl.BlockSpec((1, tq, 2 * HEAD_PAD), lambda b, hp, i: (b, i, hp))
    kspec = pl.BlockSpec((1, S, 2 * HEAD_PAD), lambda b, hp, i: (b, 0, hp))
    vspec = pl.BlockSpec((1, S, LANES), lambda b, hp, i: (b, 0, hp))
    yspec = pl.BlockSpec((1, tq, LANES), lambda b, hp, i: (b, i, hp))
    return pl.pallas_call(
        body, name="mla_fwd", grid=(B, VW // LANES, nq),
        in_specs=[qspec, kspec, vspec, _AFTER],
        out_specs=[yspec, yspec],
        out_shape=[_sds((B, S, VW), BF16), _sds((B, S, VW), F32)],
        compiler_params=_cparams(("parallel", "parallel", "arbitrary")),
    )(qp, kp, mv, after)


def _mla_bwd(qp, kp, mv, y, lse, dy, dm, after):
    B, S, QW = qp.shape
    VW = mv.shape[2]
    tq = dm["tq"]
    nq = S // tq
    scale = (MLA_NOPE + MLA_ROPE) ** -0.5

    def body(q_ref, k_ref, v_ref, y_ref, lse_ref, dy_ref, _, dq_ref, dk_ref, dv_ref, dk_acc, dv_acc):
        qi = pl.program_id(2)

        @pl.when(qi == 0)
        def _():
            dk_acc[...] = jnp.zeros_like(dk_acc)
            dv_acc[...] = jnp.zeros_like(dv_acc)

        q2 = q_ref[0]
        dy2 = dy_ref[0]
        lse2 = lse_ref[0]
        lane = lax.broadcasted_iota(jnp.int32, (tq, LANES), 1)
        in_h = [lane < MLA_V, lane >= MLA_V]
        prod = dy2.astype(F32) * y_ref[0].astype(F32)
        delta = [jnp.sum(jnp.where(m, prod, 0.0), axis=1, keepdims=True) for m in in_h]
        dyh = [jnp.where(m, dy2, 0).astype(BF16) for m in in_h]
        lseh = [lse2[:, 0:1], lse2[:, MLA_V:MLA_V + 1]]
        allowed = _chunk_mask(tq)

        def block(j, carry, masked):
            off = pl.multiple_of(j * tq, tq)
            v2 = v_ref[0, pl.ds(off, tq), :]
            heads = range(2)
            sl = [slice(h * HEAD_PAD, (h + 1) * HEAD_PAD) for h in heads]
            qhh = [q2[:, sl[h]] for h in heads]
            kh = [k_ref[0, pl.ds(off, tq), sl[h]] for h in heads]
            s = [_dot_nt(qhh[h], kh[h]) * (scale * LOG2E) for h in heads]
            dp = [_dot_nt(dyh[h], v2) for h in heads]
            if masked:
                s = [jnp.where(allowed, s[h], -1e30) for h in heads]
            p = [jnp.exp2(s[h] - lseh[h]) for h in heads]
            dv_acc[pl.ds(off, tq), :] += _dot_tn(p[0].astype(BF16), dyh[0]) + _dot_tn(p[1].astype(BF16), dyh[1])
            ds = [(p[h] * (dp[h] - delta[h]) * scale).astype(BF16) for h in heads]
            for h in heads:
                dk_acc[pl.ds(off, tq), sl[h]] += _dot_tn(ds[h], qhh[h])
            return tuple(carry[h] + _dot(ds[h], kh[h]) for h in heads)

        zero = jnp.zeros((tq, HEAD_PAD), F32)
        carry = lax.fori_loop(0, qi, lambda j, c: block(j, c, False), (zero, zero))
        carry = block(qi, carry, True)
        dq_ref[0] = jnp.concatenate([carry[0], carry[1]], axis=1).astype(BF16)

        @pl.when(qi == nq - 1)
        def _():
            dk_ref[0] = dk_acc[...].astype(BF16)
            dv_ref[0] = dv_acc[...].astype(BF16)

    qspec = pl.BlockSpec((1, tq, 2 * HEAD_PAD), lambda b, hp, i: (b, i, hp))
    kspec = pl.BlockSpec((1, S, 2 * HEAD_PAD), lambda b, hp, i: (b, 0, hp))
    vspec = pl.BlockSpec((1, S, LANES), lambda b, hp, i: (b, 0, hp))
    yspec = pl.BlockSpec((1, tq, LANES), lambda b, hp, i: (b, i, hp))
    return pl.pallas_call(
        body, name="mla_bwd", grid=(B, VW // LANES, nq),
        in_specs=[qspec, kspec, vspec, yspec, yspec, yspec, _AFTER],
        out_specs=[qspec, kspec, vspec],
        out_shape=[_sds((B, S, QW), BF16), _sds((B, S, QW), BF16), _sds((B, S, VW), BF16)],
        scratch_shapes=[pltpu.VMEM((S, 2 * HEAD_PAD), F32), pltpu.VMEM((S, LANES), F32)],
        compiler_params=_cparams(("parallel", "parallel", "arbitrary")),
    )(qp, kp, mv, y, lse, dy, after)


def _outproj_fwd(sb_y, mla_y, x0, mod, w_o, ln_g, ln_b, dm):
    B, S, D = x0.shape
    tm = dm["tm"]
    sbw = sb_y.shape[2]

    def body(ya_ref, yb_ref, x0_ref, mod_ref, wo_ref, g_ref, b_ref, mix_ref, x1_ref, h2_ref):
        mod = mod_ref[0]
        mix = _dot(ya_ref[0], wo_ref[0:sbw, :]) + _dot(yb_ref[0], wo_ref[sbw:, :])
        mix_ref[0] = mix
        x1, _, _ = _ln_fwd(ALPHA * x0_ref[0] + (1.0 + mod[2:3]) * mix, g_ref[...], b_ref[...])
        x1_ref[0] = x1
        h2_ref[0] = (x1 * (1.0 + mod[4:5]) + mod[3:4]).astype(BF16)

    return pl.pallas_call(
        body, name="outproj_fwd", grid=(B, S // tm),
        in_specs=[_tok(tm, sbw), _tok(tm, mla_y.shape[2]), _tok(tm, D), _perb(N_MOD, D),
                  _full(w_o), _full(ln_g), _full(ln_b)],
        out_specs=[_tok(tm, D)] * 3,
        out_shape=[_sds((B, S, D), F32), _sds((B, S, D), F32), _sds((B, S, D), BF16)],
        compiler_params=_cparams(("parallel", "parallel")),
    )(sb_y, mla_y, x0, mod, w_o, ln_g, ln_b)


def _stat_specs(B, D):
    specs = [pl.BlockSpec((1, 8, D), lambda b, s: (b, 0, 0)), pl.BlockSpec((8, D), lambda b, s: (0, 0))]
    shapes = [_sds((B, 8, D), F32), _sds((8, D), F32)]
    return specs, shapes


def _stat_init(bst_ref, wst_ref):
    @pl.when(pl.program_id(1) == 0)
    def _():
        bst_ref[...] = jnp.zeros_like(bst_ref)

    @pl.when((pl.program_id(0) == 0) & (pl.program_id(1) == 0))
    def _():
        wst_ref[...] = jnp.zeros_like(wst_ref)


def _mlp_fwd(h2, x1, mod, target, w_up, w_down, ln_g, ln_b, dm):
    B, S, D = x1.shape
    tm = dm["tm"]
    nck, _, ck = w_up.shape
    dff = nck * ck

    def body(h2_ref, x1_ref, mod_ref, t_ref, wu_ref, wd_ref, g_ref, b_ref, u_ref, dr_ref, bst_ref, wst_ref):
        _stat_init(bst_ref, wst_ref)
        mod = mod_ref[0]
        h2 = h2_ref[0]
        ff = jnp.zeros((tm, D), F32)
        for c in range(nck):
            u = _dot(h2, wu_ref[c])
            u_ref[0, :, c * ck:(c + 1) * ck] = u.astype(BF16)
            act = jnp.square(jnp.maximum(u, 0.0)).astype(BF16)
            ff = ff + _dot(act, wd_ref[c])
        g = g_ref[...]
        x2, xhat, rstd = _ln_fwd(ALPHA * x1_ref[0] + (1.0 + mod[5:6]) * ff, g, b_ref[...])
        err = x2 - t_ref[0]
        dy = err * (1.0 / D)
        dr = _ln_bwd(dy, xhat, rstd, g)
        dr_ref[0] = dr
        bst_ref[0, 0:1, :] += _colsum(dr * ff)
        wst_ref[0:1, :] += _colsum(dy * xhat)
        wst_ref[1:2, :] += _colsum(dy)
        wst_ref[2:3, :] += _colsum(err * err) * (0.5 / D)

    sspecs, sshapes = _stat_specs(B, D)
    return pl.pallas_call(
        body, name="mlp_fwd", grid=(B, S // tm),
        in_specs=[_tok(tm, D), _tok(tm, D), _perb(N_MOD, D), _tok(tm, D), _full(w_up), _full(w_down),
                  _full(ln_g), _full(ln_b)],
        out_specs=[_tok(tm, dff), _tok(tm, D)] + sspecs,
        out_shape=[_sds((B, S, dff), BF16), _sds((B, S, D), F32)] + sshapes,
        compiler_params=_cparams(("arbitrary", "arbitrary")),
    )(h2, x1, mod, target, w_up, w_down, ln_g, ln_b)


def _mlp_bwd(dr2, u, x1, x0, mix, mod, w_up, w_down, w_o, ln_g, dm):
    B, S, D = x1.shape
    tm = dm["tm_small"]
    sbw = dm["sbw"]
    nck, _, ck = w_up.shape
    dff = nck * ck

    def body(dr_ref, u_ref, x1_ref, x0_ref, mix_ref, mod_ref, wu_ref, wd_ref, wo_ref, g_ref,
             du_ref, dff_ref, dmix_ref, dx0_ref, dya_ref, dyb_ref, bst_ref, wst_ref):
        _stat_init(bst_ref, wst_ref)
        mod = mod_ref[0]
        dr2 = dr_ref[0]
        dffv = ((1.0 + mod[5:6]) * dr2).astype(BF16)
        dff_ref[0] = dffv
        dh2 = jnp.zeros((tm, D), F32)
        for c in range(nck):
            sl = slice(c * ck, (c + 1) * ck)
            da = _dot_nt(dffv, wd_ref[c])
            du = (da * (2.0 * jnp.maximum(u_ref[0, :, sl].astype(F32), 0.0))).astype(BF16)
            du_ref[0, :, sl] = du
            dh2 = dh2 + _dot_nt(du, wu_ref[c])
        x1 = x1_ref[0]
        dx1 = ALPHA * dr2 + dh2 * (1.0 + mod[4:5])
        bst_ref[0, 0:1, :] += _colsum(dh2 * x1)
        bst_ref[0, 1:2, :] += _colsum(dh2)
        mix = mix_ref[0]
        g = g_ref[...]
        _, xhat, rstd = _ln_fwd(ALPHA * x0_ref[0] + (1.0 + mod[2:3]) * mix, g, 0.0)
        dr1 = _ln_bwd(dx1, xhat, rstd, g)
        wst_ref[0:1, :] += _colsum(dx1 * xhat)
        wst_ref[1:2, :] += _colsum(dx1)
        bst_ref[0, 2:3, :] += _colsum(dr1 * mix)
        dx0_ref[0] = ALPHA * dr1
        dmix = ((1.0 + mod[2:3]) * dr1).astype(BF16)
        dmix_ref[0] = dmix
        dya_ref[0] = _dot_nt(dmix, wo_ref[0:sbw, :]).astype(BF16)
        dyb_ref[0] = _dot_nt(dmix, wo_ref[sbw:, :]).astype(BF16)

    sspecs, sshapes = _stat_specs(B, D)
    wa, wb = sbw, w_o.shape[0] - sbw
    return pl.pallas_call(
        body, name="mlp_bwd", grid=(B, S // tm),
        in_specs=[_tok(tm, D), _tok(tm, dff), _tok(tm, D), _tok(tm, D), _tok(tm, D), _perb(N_MOD, D),
                  _full(w_up), _full(w_down), _full(w_o), _full(ln_g)],
        out_specs=[_tok(tm, dff), _tok(tm, D), _tok(tm, D), _tok(tm, D), _tok(tm, wa), _tok(tm, wb)] + sspecs,
        out_shape=[_sds((B, S, dff), BF16), _sds((B, S, D), BF16), _sds((B, S, D), BF16), _sds((B, S, D), F32),
                   _sds((B, S, wa), BF16), _sds((B, S, wb), BF16)] + sshapes,
        compiler_params=_cparams(("arbitrary", "arbitrary")),
    )(dr2, u, x1, x0, mix, mod, w_up, w_down, w_o, ln_g)


def _inproj_bwd(x, x0, dx0a, mod, ln_g, dq, dk, dv, dqp, dkp, dmv, cq, ckv, w_in_p, w_uq_p, w_kv, gq, gkv,
                tc, ts1, ts2, dm):
    B, S, D = x.shape
    tm = dm["tm"]
    sbw, qr, kvr, nh = dm["sbw"], dm["qr"], dm["kvr"], dm["nh"]
    qpw = nh * HEAD_PAD
    dinp = w_in_p.shape[1]
    kvw = w_kv.shape[1]

    def body(x_ref, x0_ref, dx0a_ref, mod_ref, g_ref, dq_ref, dk_ref, dv_ref, dqp_ref, dkp_ref, dmv_ref,
             cq_ref, ckv_ref, win_ref, wuq_ref, wkv_ref, gq_ref, gkv_ref, tc_ref, ts1_ref, ts2_ref,
             gx_ref, dproj_ref, dqpre_ref, dkvo_ref, bst_ref, wst_ref):
        _stat_init(bst_ref, wst_ref)
        mod = mod_ref[0]
        c1, s1, s2 = tc_ref[...], ts1_ref[...], ts2_ref[...]
        c8, s18, s28 = jnp.tile(c1, (1, nh)), jnp.tile(s1, (1, nh)), jnp.tile(s2, (1, nh))
        dqpre = _rope_t(dqp_ref[0].astype(F32), c8, s18, s28).astype(BF16)
        dqpre_ref[0] = dqpre
        gq = gq_ref[...]
        cq = cq_ref[0]
        rq = lax.rsqrt(jnp.mean(cq * cq, axis=-1, keepdims=True) + RMS_EPS)
        dqn = _dot_nt(dqpre, wuq_ref[...])
        wst_ref[4:5, 0:qr] += _colsum(dqn * cq * rq)
        dqg = dqn * gq
        dcq = rq * dqg - cq * (rq * rq * rq) * jnp.mean(dqg * cq, axis=-1, keepdims=True)

        dkpre = _rope_t(dkp_ref[0].astype(F32), c8, s18, s28)
        dkr = dkpre[:, 0:HEAD_PAD]
        for h in range(1, nh):
            dkr = dkr + dkpre[:, h * HEAD_PAD:(h + 1) * HEAD_PAD]
        lane = lax.broadcasted_iota(jnp.int32, (tm, LANES), 1)
        dkr = jnp.where((lane >= MLA_NOPE) & (lane < MLA_NOPE + MLA_ROPE), dkr, 0.0)
        dkr = pltpu.roll(dkr, LANES - MLA_NOPE, 1)
        dkvo = jnp.concatenate([dkpre.astype(BF16), dmv_ref[0]], axis=1)
        dkvo_ref[0] = dkvo
        gkv = gkv_ref[...]
        ckv = ckv_ref[0]
        rkv = lax.rsqrt(jnp.mean(ckv * ckv, axis=-1, keepdims=True) + RMS_EPS)
        dkvn = _dot_nt(dkvo, wkv_ref[...])
        wst_ref[5:6, 0:kvr] += _colsum(dkvn * ckv * rkv)
        dkg = dkvn * gkv
        dckv = rkv * dkg - ckv * (rkv * rkv * rkv) * jnp.mean(dkg * ckv, axis=-1, keepdims=True)

        dproj = jnp.concatenate([dq_ref[0], dk_ref[0], dv_ref[0], dcq.astype(BF16), dckv.astype(BF16),
                                 dkr.astype(BF16)], axis=1)
        dproj_ref[0] = dproj
        dh = _dot_nt(dproj, win_ref[...])
        x0 = x0_ref[0]
        dx0 = dx0a_ref[0] + dh * (1.0 + mod[1:2])
        bst_ref[0, 0:1, :] += _colsum(dh * x0)
        bst_ref[0, 1:2, :] += _colsum(dh)
        g = g_ref[...]
        _, xhat, rstd = _ln_fwd(x_ref[0], g, 0.0)
        gx_ref[0] = _ln_bwd(dx0, xhat, rstd, g)
        wst_ref[0:1, :] += _colsum(dx0 * xhat)
        wst_ref[1:2, :] += _colsum(dx0)

    tab = pl.BlockSpec((tm, LANES), lambda b, s: (s, 0))
    sspecs, sshapes = _stat_specs(B, D)
    return pl.pallas_call(
        body, name="inproj_bwd", grid=(B, S // tm),
        in_specs=[_tok(tm, D), _tok(tm, D), _tok(tm, D), _perb(N_MOD, D), _full(ln_g),
                  _tok(tm, sbw), _tok(tm, sbw), _tok(tm, sbw), _tok(tm, qpw), _tok(tm, qpw), _tok(tm, nh * MLA_V),
                  _tok(tm, qr), _tok(tm, kvr), _full(w_in_p), _full(w_uq_p), _full(w_kv), _full(gq), _full(gkv),
                  tab, tab, tab],
        out_specs=[_tok(tm, D), _tok(tm, dinp), _tok(tm, qpw), _tok(tm, kvw)] + sspecs,
        out_shape=[_sds((B, S, D), F32), _sds((B, S, dinp), BF16), _sds((B, S, qpw), BF16),
                   _sds((B, S, kvw), BF16)] + sshapes,
        compiler_params=_cparams(("arbitrary", "arbitrary")),
    )(x, x0, dx0a, mod, ln_g, dq, dk, dv, dqp, dkp, dmv, cq, ckv, w_in_p, w_uq_p, w_kv, gq, gkv, tc, ts1, ts2)


def _tile_of(n, cap):
    if n <= cap:
        return n
    best = n
    for t in range(LANES, cap + 1, LANES):
        if n % t == 0:
            best = t
    return best


def _mm_tn(a, g, name, relu_sq=False, out_dtype=F32, col_blocks=None):
    T, K = a.shape
    N = g.shape[1]
    tt = 1024 if T % 1024 == 0 else (512 if T % 512 == 0 else T)
    tk = _tile_of(K, 1024)
    tn = _tile_of(N, 1280)
    nt = T // tt
    bw = N // col_blocks if col_blocks else tn
    assert tn % bw == 0

    def body(a_ref, g_ref, o_ref, acc_ref):
        @pl.when(pl.program_id(2) == 0)
        def _():
            acc_ref[...] = jnp.zeros_like(acc_ref)

        av = a_ref[...]
        if relu_sq:
            av = jnp.square(jnp.maximum(av.astype(F32), 0.0)).astype(BF16)
        acc_ref[...] += _dot_tn(av, g_ref[...])

        @pl.when(pl.program_id(2) == nt - 1)
        def _():
            if col_blocks:
                for c in range(tn // bw):
                    o_ref[c] = acc_ref[:, c * bw:(c + 1) * bw].astype(out_dtype)
            else:
                o_ref[...] = acc_ref[...].astype(out_dtype)

    if col_blocks:
        out_spec = pl.BlockSpec((tn // bw, tk, bw), lambda i, j, t: (j, i, 0))
        out_shape = _sds((col_blocks, K, bw), out_dtype)
    else:
        out_spec = pl.BlockSpec((tk, tn), lambda i, j, t: (i, j))
        out_shape = _sds((K, N), out_dtype)
    return pl.pallas_call(
        body, name=name, grid=(K // tk, N // tn, nt),
        in_specs=[pl.BlockSpec((tt, tk), lambda i, j, t: (t, i)), pl.BlockSpec((tt, tn), lambda i, j, t: (t, j))],
        out_specs=out_spec, out_shape=out_shape,
        scratch_shapes=[pltpu.VMEM((tk, tn), F32)],
        compiler_params=_cparams(("parallel", "parallel", "arbitrary")),
    )(a, g)


def _reduce_adamw(parts, w, m, v, name):
    _, K, N = parts.shape
    tr = 256 if K % 256 == 0 else K

    def body(p_ref, w_ref, m_ref, v_ref, g_ref, d_ref, nm_ref, nv_ref):
        g = p_ref[0].astype(F32)
        for k in range(1, 4):
            g = g + p_ref[k].astype(F32)
        g_ref[0] = g
        d_ref[0], nm_ref[0], nv_ref[0] = _adamw(w_ref[0], g, m_ref[0], v_ref[0])

    spec = pl.BlockSpec((1, tr, N), lambda r: (0, r, 0))
    return pl.pallas_call(
        body, name=name, grid=(K // tr,),
        in_specs=[pl.BlockSpec((4, tr, N), lambda r: (0, r, 0)), spec, spec, spec],
        out_specs=[spec] * 4, out_shape=[_sds((1, K, N), F32)] * 4,
        compiler_params=_cparams(("parallel",)),
    )(parts, w, m, v)


def _finish(sm, dmod_all, dmod_my, cact_all, p_small, m_small, v_small, b_ada, m_b, v_b, w_ada, m_w, v_w):
    n0 = p_small.shape[1]
    n1 = sm.shape[1]
    d = cact_all.shape[1]

    def body(sm_ref, dma_ref, dmm_ref, ca_ref, p_ref, pm_ref, pv_ref, b_ref, bm_ref, bv_ref, w_ref, wm_ref, wv_ref,
             gs_ref, ds_ref, ms_ref, vs_ref, gb_ref, db_ref, mb_ref, vb_ref, gw_ref, dw_ref, mw_ref, vw_ref,
             loss_ref):
        gs = sm_ref[0:1, :]
        for k in range(1, N_DEV):
            gs = gs + sm_ref[k:k + 1, :]
        gs_ref[...] = gs
        ds_ref[...], ms_ref[...], vs_ref[...] = _adamw(p_ref[...], gs[:, 0:n0], pm_ref[...], pv_ref[...])
        loss_ref[...] = jnp.zeros((1, LANES), F32) + jnp.sum(gs[:, n1 - d:n1])
        gb = jnp.sum(dma_ref[...], axis=0, keepdims=True)
        gb_ref[...] = gb
        db_ref[...], mb_ref[...], vb_ref[...] = _adamw(b_ref[...], gb, bm_ref[...], bv_ref[...])
        gw = _dot_tn(ca_ref[...].astype(BF16), dmm_ref[...].astype(BF16))
        gw_ref[...] = gw
        dw_ref[...], mw_ref[...], vw_ref[...] = _adamw(w_ref[...], gw, wm_ref[...], wv_ref[...])

    s0 = _sds(p_small.shape, F32)
    sb = _sds(b_ada.shape, F32)
    sw = _sds(w_ada.shape, F32)
    return pl.pallas_call(
        body, name="finish_small",
        out_shape=[_sds((1, n1), F32), s0, s0, s0, sb, sb, sb, sb, sw, sw, sw, sw,
                   _sds((1, LANES), F32)],
        compiler_params=pltpu.CompilerParams(vmem_limit_bytes=VMEM_LIMIT),
    )(sm, dmod_all, dmod_my, cact_all, p_small, m_small, v_small, b_ada, m_b, v_b, w_ada, m_w, v_w)


def _pack(arrs, dtype, width):
    flat = jnp.concatenate([a.astype(dtype).reshape(-1) for a in arrs])
    rows = -(-flat.shape[0] // (256 * width)) * 256
    return jnp.pad(flat, (0, rows * width - flat.shape[0])).reshape(rows, width)


def _unpack(slab, shapes):
    flat = slab.reshape(-1)
    out, o = [], 0
    for s in shapes:
        n = math.prod(s)
        out.append(flat[o:o + n].reshape(s))
        o += n
    return out


def _rope_tables(S):
    inv_freq = 1.0 / (ROPE_BASE ** (jnp.arange(0, MLA_ROPE, 2, dtype=F32) / MLA_ROPE))
    ang = jnp.arange(S, dtype=F32)[:, None] * inv_freq[None, :]
    cos, sin = jnp.cos(ang), jnp.sin(ang)
    one = jnp.ones((S, MLA_NOPE), F32)
    z16 = jnp.zeros((S, 16), F32)
    z32 = jnp.zeros((S, 32), F32)
    z64 = jnp.zeros((S, MLA_NOPE), F32)
    tc = jnp.concatenate([one, cos, cos, jnp.ones((S, 32), F32)], axis=1)
    ts1 = jnp.concatenate([z64, -sin, z16, z32], axis=1)
    ts2 = jnp.concatenate([z64, z16, sin, z32], axis=1)
    return tc, ts1, ts2


def kernel(x, c, ln_in_g, ln_in_b, w_ada, b_ada, w_in, q_norm_g, kv_norm_g, w_uq, w_ukv, w_o, ln1_g, ln1_b, w_up, w_down, ln2_g, ln2_b, loss_target, m_ln_in_g, m_ln_in_b, m_w_ada, m_b_ada, m_w_in, m_q_norm_g, m_kv_norm_g, m_w_uq, m_w_ukv, m_w_o, m_ln1_g, m_ln1_b, m_w_up, m_w_down, m_ln2_g, m_ln2_b, v_ln_in_g, v_ln_in_b, v_w_ada, v_b_ada, v_w_in, v_q_norm_g, v_kv_norm_g, v_w_uq, v_w_ukv, v_w_o, v_ln1_g, v_ln1_b, v_w_up, v_w_down, v_ln2_g, v_ln2_b):
    B, S, D = x.shape
    sbw = D // 2
    mlw = D - sbw
    nh = mlw // MLA_V
    qr = w_uq.shape[1]
    kvr = w_ukv.shape[1]
    qk = MLA_NOPE + MLA_ROPE
    dff = w_up.shape[2] * N_DEV
    din = w_in.shape[2] * N_DEV
    tm = 512 if S % 512 == 0 else S
    tq = min(512, S // 2)
    dm = dict(tm=tm, tm_small=min(tm, 256), tq=tq, sbw=sbw, qr=qr, kvr=kvr, nh=nh)
    width = 1024 if D >= 1024 else LANES
    dev = 4 * lax.axis_index("x") + 2 * lax.axis_index("y") + lax.axis_index("c")

    nada = w_ada.shape[2]
    c_all = _all_gather([c], "gather_c")[0].reshape(N_DEV * B, D)
    b_loc = lax.dynamic_slice(b_ada, (0, dev * nada), (1, nada))
    cact_all, mod_part = _ada_partial(c_all, w_ada[0], b_loc)
    mod_all = _all_gather([mod_part], "gather_mod")[0]
    mod = lax.dynamic_slice(mod_all, (0, dev * B, 0), (N_DEV, B, nada))
    mod = jnp.swapaxes(mod, 0, 1).reshape(B, N_MOD, D)

    big = [w_in, w_uq, w_ukv, w_o, w_up, w_down]
    late_w, late_token = _chip_exchange_start([a[0].astype(BF16) for a in big[3:]], "gather_w_late_start", scatter=False)
    w_in8, w_uq8, w_ukv8 = _all_gather([a[0].astype(BF16) for a in big[:3]], "gather_w_first")
    cols = lambda a8: jnp.swapaxes(a8, 0, 1).reshape(a8.shape[1], N_DEV * a8.shape[2])
    w_in_p = jnp.pad(cols(w_in8), ((0, 0), (0, LANES - MLA_ROPE)))
    zpad = jnp.zeros((qr, nh, HEAD_PAD - qk), BF16)
    w_uq_p = jnp.concatenate([cols(w_uq8).reshape(qr, nh, qk), zpad], axis=2).reshape(qr, nh * HEAD_PAD)
    w_ukv_f = cols(w_ukv8)
    w_uk = w_ukv_f[:, :nh * MLA_NOPE].reshape(kvr, nh, MLA_NOPE)
    w_uk_p = jnp.concatenate([w_uk, jnp.zeros((kvr, nh, HEAD_PAD - MLA_NOPE), BF16)], axis=2)
    w_kv = jnp.concatenate([w_uk_p.reshape(kvr, nh * HEAD_PAD), w_ukv_f[:, nh * MLA_NOPE:]], axis=1)

    tc, ts1, ts2 = _rope_tables(S)
    g_in, b_in = ln_in_g.reshape(1, D), ln_in_b.reshape(1, D)
    (x0, h, sq, sk, sv, qp, kp, mv, cq, ckv, qn, kvn) = _inproj_fwd(
        x, mod, g_in, b_in, w_in_p, w_uq_p, w_kv, q_norm_g, kv_norm_g, tc, ts1, ts2, dm, late_token)
    sb_y, sb_tot = _sb_fwd(sq, sk, sv, dm)
    mla_y, mla_lse = _mla_fwd(qp, kp, mv, dm, sb_tot)
    late_by_chip = _chip_exchange_wait(late_w, mla_lse, "gather_w_late_wait")
    w_o8, w_up8, w_down8 = [b.reshape((N_DEV,) + b.shape[2:]) for b in _core_gather(late_by_chip, "gather_w_late_cores")]
    w_o_f = w_o8.reshape(D, D)
    mix, x1, h2 = _outproj_fwd(sb_y, mla_y, x0, mod, w_o_f, ln1_g, ln1_b, dm)
    u, dr2, bst_c, wst_c = _mlp_fwd(h2, x1, mod, loss_target, w_up8, w_down8, ln2_g, ln2_b, dm)

    du, dffb, dmixb, dx0a, dsb_y, dmla_y, bst_b, wst_b = _mlp_bwd(
        dr2, u, x1, x0, mix, mod, w_up8, w_down8, w_o_f, ln1_g, dm)
    T = B * S
    r2 = lambda a: a.reshape(T, a.shape[2])
    by_core = lambda a: a.reshape((4, 2) + a.shape[1:])
    g_o = jnp.concatenate([_mm_tn(r2(sb_y), r2(dmixb), "grad_w_o_sb", out_dtype=BF16),
                           _mm_tn(r2(mla_y), r2(dmixb), "grad_w_o_mla", out_dtype=BF16)], axis=0)
    g_up8 = _mm_tn(r2(h2), r2(du), "grad_w_up", out_dtype=BF16, col_blocks=N_DEV)
    g_down = _mm_tn(r2(u), r2(dffb), "grad_w_down", relu_sq=True, out_dtype=BF16)
    early = [g_o.reshape(N_DEV, D // N_DEV, D), g_up8, g_down.reshape(N_DEV, dff // N_DEV, D)]
    early_sum = _core_scatter_sum([by_core(a) for a in early], "scatter_g_early_cores")
    early_g, early_token = _chip_exchange_start(early_sum, "scatter_g_early_start", scatter=True)

    dsq, dsk, dsv = _sb_bwd(sq, sk, sv, sb_tot, dsb_y, dm, early_token)
    dqp, dkp, dmv = _mla_bwd(qp, kp, mv, mla_y, mla_lse, dmla_y, dm, dsq)
    grad_x, dproj, dqpre, dkvo, bst_a, wst_a = _inproj_bwd(
        x, x0, dx0a, mod, g_in, dsq, dsk, dsv, dqp, dkp, dmv, cq, ckv, w_in_p, w_uq_p, w_kv, q_norm_g, kv_norm_g,
        tc, ts1, ts2, dm)
    g_in_p = _mm_tn(r2(h), r2(dproj), "grad_w_in")
    g_uq_p = _mm_tn(r2(qn), r2(dqpre), "grad_w_uq")
    g_kv = _mm_tn(r2(kvn), r2(dkvo), "grad_w_kv")
    g_uq_f = g_uq_p.reshape(qr, nh, HEAD_PAD)[:, :, :qk].reshape(qr, nh * qk)
    g_uk = g_kv[:, :nh * HEAD_PAD].reshape(kvr, nh, HEAD_PAD)[:, :, :MLA_NOPE].reshape(kvr, nh * MLA_NOPE)
    g_ukv_f = jnp.concatenate([g_uk, g_kv[:, nh * HEAD_PAD:]], axis=1)
    early_quarter = _chip_exchange_wait(early_g, g_kv, "scatter_g_early_wait")

    def by_dest_cols(a):
        k, n = a.shape[0], a.shape[1] // N_DEV
        return jnp.swapaxes(a.reshape(k, N_DEV, n), 0, 1).astype(BF16)

    last = [by_dest_cols(g_in_p[:, :din]), by_dest_cols(g_uq_f), by_dest_cols(g_ukv_f)]
    last_sum = _core_scatter_sum([by_core(a) for a in last], "scatter_g_last_cores")
    quarter = list(_chip_exchange(last_sum, "scatter_g_last_chips", scatter=True)) + list(early_quarter)
    names = ["w_in", "w_uq", "w_ukv", "w_o", "w_up", "w_down"]
    moms = [m_w_in, m_w_uq, m_w_ukv, m_w_o, m_w_up, m_w_down]
    vars_ = [v_w_in, v_w_uq, v_w_ukv, v_w_o, v_w_up, v_w_down]
    res = [_reduce_adamw(p, w, m, v, "adamw_" + n) for p, w, m, v, n in zip(quarter, big, moms, vars_, names)]
    gb, db, nmb, nvb = ([r[i] for r in res] for i in range(4))

    dmod = jnp.concatenate([bst_a[:, 1], bst_a[:, 0], bst_b[:, 2], bst_b[:, 1], bst_b[:, 0], bst_c[:, 0]], axis=1)
    small = jnp.concatenate([wst_a[0], wst_a[1], wst_a[4, :qr], wst_a[5, :kvr], wst_b[0], wst_b[1],
                             wst_c[0], wst_c[1], wst_c[2]])
    n1 = small.shape[0]
    both = _all_gather([_pack([dmod, small], F32, LANES)], "gather_small")[0].reshape(N_DEV, -1)
    dmod_all = both[:, :B * N_MOD * D].reshape(N_DEV * B, N_MOD * D)
    sm = both[:, B * N_MOD * D:B * N_MOD * D + n1]
    dmod_my = lax.dynamic_slice(dmod_all, (0, dev * nada), (N_DEV * B, nada))
    row = lambda arrs: jnp.concatenate([a.reshape(1, -1) for a in arrs], axis=1)
    smalls = [ln_in_g, ln_in_b, q_norm_g, kv_norm_g, ln1_g, ln1_b, ln2_g, ln2_b]
    small_shapes = [a.shape for a in smalls]
    (gs, ds, nms, nvs, g_b, d_b, nm_b, nv_b, g_w, d_w, nm_w, nv_w, loss_v) = _finish(
        sm, dmod_all, dmod_my, cact_all, row(smalls),
        row([m_ln_in_g, m_ln_in_b, m_q_norm_g, m_kv_norm_g, m_ln1_g, m_ln1_b, m_ln2_g, m_ln2_b]),
        row([v_ln_in_g, v_ln_in_b, v_q_norm_g, v_kv_norm_g, v_ln1_g, v_ln1_b, v_ln2_g, v_ln2_b]),
        b_ada, m_b_ada, v_b_ada, w_ada[0], m_w_ada[0], v_w_ada[0])
    gsm, dsm, nmsm, nvsm = (_unpack(s, small_shapes) for s in (gs, ds, nms, nvs))

    def ordered(sm_l, w_l, ada_w, ada_b):
        return [sm_l[0], sm_l[1], ada_w[None], ada_b, w_l[0], sm_l[2], sm_l[3], w_l[1], w_l[2], w_l[3],
                sm_l[4], sm_l[5], w_l[4], w_l[5], sm_l[6], sm_l[7]]

    loss = loss_v[0, 0]
    return (loss, grad_x, *ordered(gsm, gb, g_w, g_b), *ordered(dsm, db, d_w, d_b),
            *ordered(nmsm, nmb, nm_w, nm_b), *ordered(nvsm, nvb, nv_w, nv_b))
```

```python
import functools
import math

import jax
import jax.numpy as jnp
from jax import lax
from jax.experimental import pallas as pl
from jax.experimental.pallas import tpu as pltpu

F32 = jnp.float32
BF16 = jnp.bfloat16

SB_HD = 64
MLA_V = 64
MLA_NOPE = 64
MLA_ROPE = 32
HEAD_PAD = 128
CHUNK = 64
ROPE_BASE = 10000.0
LN_EPS = 1e-5
RMS_EPS = 1e-6
DEPTH = 1
ALPHA = (2.0 * DEPTH) ** 0.25
N_MOD = 6
ADAM_LR = 0.001
ADAM_B1 = 0.9
ADAM_B2 = 0.999
ADAM_EPS = 1e-08
ADAM_WD = 0.01
ADAM_STEP = 10
N_DEV = 8
LANES = 128
LOG2E = 1.4426950408889634
CUMSUM_W = 256
VMEM_LIMIT = 56 * 1024 * 1024
MESH = pl.DeviceIdType.MESH


def _dot(a, b):
    return jnp.dot(a, b, preferred_element_type=F32)


def _dot_nt(a, b):
    return lax.dot_general(a, b, (((1,), (1,)), ((), ())), preferred_element_type=F32)


def _dot_tn(a, b):
    return lax.dot_general(a, b, (((0,), (0,)), ((), ())), preferred_element_type=F32)


def _cparams(sem):
    return pltpu.CompilerParams(dimension_semantics=sem, vmem_limit_bytes=VMEM_LIMIT)


def _full(a):
    nd = a.ndim
    return pl.BlockSpec(a.shape, lambda *_: (0,) * nd, pipeline_mode=pl.Buffered(1))


def _tok(tm, w):
    return pl.BlockSpec((1, tm, w), lambda b, s: (b, s, 0))


def _perb(rows, w):
    return pl.BlockSpec((1, rows, w), lambda b, s: (b, 0, 0))


def _sds(shape, dtype):
    return jax.ShapeDtypeStruct(shape, dtype)


def _ln_fwd(x, g, b):
    mu = jnp.mean(x, axis=-1, keepdims=True)
    xc = x - mu
    var = jnp.mean(xc * xc, axis=-1, keepdims=True)
    rstd = lax.rsqrt(var + LN_EPS)
    xhat = xc * rstd
    return xhat * g + b, xhat, rstd


def _ln_bwd(dy, xhat, rstd, g):
    dxh = dy * g
    m1 = jnp.mean(dxh, axis=-1, keepdims=True)
    m2 = jnp.mean(dxh * xhat, axis=-1, keepdims=True)
    return rstd * (dxh - m1 - xhat * m2)


def _colsum(a):
    return jnp.sum(a, axis=0, keepdims=True)


def _rope(x, c, s1, s2):
    w = x.shape[-1]
    return x * c + pltpu.roll(x, w - 16, 1) * s1 + pltpu.roll(x, 16, 1) * s2


def _rope_t(x, c, s1, s2):
    w = x.shape[-1]
    return x * c - pltpu.roll(x, w - 16, 1) * s1 - pltpu.roll(x, 16, 1) * s2


def _adamw(w, g, m, v):
    m = ADAM_B1 * m + (1.0 - ADAM_B1) * g
    v = ADAM_B2 * v + (1.0 - ADAM_B2) * (g * g)
    m_hat = m / (1.0 - ADAM_B1 ** ADAM_STEP)
    v_hat = v / (1.0 - ADAM_B2 ** ADAM_STEP)
    delta = -ADAM_LR * (m_hat / (jnp.sqrt(v_hat) + ADAM_EPS) + ADAM_WD * w)
    return delta, m, v


def _my_place():
    return lax.axis_index("x"), lax.axis_index("y"), lax.axis_index("c")


def _chip_peers(mx, my):
    out = []
    for j in (1, 2, 3):
        px = 1 - mx if (j >> 1) else mx
        py = 1 - my if (j & 1) else my
        out.append((px, py, 2 * px + py))
    return out


def _hbm_call(body, name, n_in, out_shape, sems):
    hbm = pl.BlockSpec(memory_space=pl.ANY)
    return pl.pallas_call(
        body, name=name, out_shape=out_shape,
        in_specs=[hbm] * n_in, out_specs=[hbm] * len(out_shape),
        scratch_shapes=[pltpu.SemaphoreType.DMA(s) for s in sems])


def _chip_exchange(xs, name, scatter):
    n = len(xs)

    def body(*refs):
        x_refs, o_refs = refs[:n], refs[n:2 * n]
        ssem, rsem, lsem = refs[2 * n:]
        mx, my, mc = _my_place()
        me = 2 * mx + my
        peers = _chip_peers(mx, my)

        def copy(i, j, src_slot, dst_slot):
            px, py, _ = peers[j]
            return pltpu.make_async_remote_copy(
                src_ref=x_refs[i].at[src_slot] if scatter else x_refs[i], dst_ref=o_refs[i].at[dst_slot],
                send_sem=ssem.at[i, j], recv_sem=rsem.at[i, j], device_id=(px, py, mc), device_id_type=MESH)

        local = [pltpu.make_async_copy(x_refs[i].at[me] if scatter else x_refs[i], o_refs[i].at[me], lsem.at[i])
                 for i in range(n)]
        sends = [copy(i, j, peers[j][2], me) for i in range(n) for j in range(3)]
        for cp in local + sends:
            cp.start()
        for i in range(n):
            for j in range(3):
                copy(i, j, peers[j][2], peers[j][2]).wait_recv()
        for cp in sends:
            cp.wait_send()
        for cp in local:
            cp.wait()

    out_shape = [_sds((4,) + tuple(x.shape[1:] if scatter else x.shape), x.dtype) for x in xs]
    return _hbm_call(body, name, n, out_shape, [(n, 3), (n, 3), (n,)])(*xs)


def _chip_exchange_start(xs, name, scatter, after):
    n = len(xs)
    blks = [tuple(x.shape[1:] if scatter else x.shape) for x in xs]

    def body(*refs):
        x_refs, land_refs = refs[:n], refs[n:2 * n]
        ssem, rsem = refs[2 * n + 1], refs[2 * n + 2]
        token = refs[-1]
        mx, my, mc = _my_place()
        me = 2 * mx + my
        for i in range(n):
            for j, (px, py, pk) in enumerate(_chip_peers(mx, my)):
                pltpu.make_async_remote_copy(
                    src_ref=x_refs[i].at[pk] if scatter else x_refs[i], dst_ref=land_refs[i].at[me],
                    send_sem=ssem.at[3 * i + j], recv_sem=rsem.at[3 * i + j], device_id=(px, py, mc),
                    device_id_type=MESH).start()
        token[...] = jnp.zeros_like(token)

    hbm = pl.BlockSpec(memory_space=pltpu.HBM)
    sem = pl.BlockSpec(memory_space=pltpu.SEMAPHORE)
    lands = [lax.empty((4,) + b, x.dtype) for b, x in zip(blks, xs)]
    res = pl.pallas_call(
        body, name=name,
        out_shape=[pltpu.SemaphoreType.DMA((3 * n,)), pltpu.SemaphoreType.DMA((3 * n,))]
        + [pltpu.HBM(x.shape, x.dtype) for x in xs] + [pltpu.HBM(l.shape, l.dtype) for l in lands]
        + [_sds((8, LANES), F32)],
        in_specs=[hbm] * (2 * n) + [_AFTER],
        out_specs=[sem, sem] + [hbm] * (2 * n) + [pl.BlockSpec(memory_space=pltpu.VMEM)],
        input_output_aliases={i: 2 + i for i in range(2 * n)},
        compiler_params=pltpu.CompilerParams(has_side_effects=pltpu.SideEffectType.DATAFLOW_SIDE_EFFECTING),
    )(*[pltpu.with_memory_space_constraint(a, pltpu.HBM) for a in list(xs) + lands], after)
    return dict(ssem=res[0], rsem=res[1], xs=res[2:2 + n], lands=res[2 + n:2 + 2 * n], n=n, scatter=scatter), res[-1]


def _chip_exchange_wait(handle, after, name):
    n, scatter = handle["n"], handle["scatter"]

    def body(*refs):
        x_refs, land_refs = refs[:n], refs[n:2 * n]
        ssem, rsem = refs[2 * n], refs[2 * n + 1]
        mx, my, mc = _my_place()
        for i in range(n):
            for j, (px, py, pk) in enumerate(_chip_peers(mx, my)):
                cp = pltpu.make_async_remote_copy(
                    src_ref=x_refs[i].at[pk] if scatter else x_refs[i], dst_ref=land_refs[i].at[pk],
                    send_sem=ssem.at[3 * i + j], recv_sem=rsem.at[3 * i + j], device_id=(px, py, mc),
                    device_id_type=MESH)
                cp.wait_send()
                cp.wait_recv()

    hbm = pl.BlockSpec(memory_space=pltpu.HBM)
    sem = pl.BlockSpec(memory_space=pltpu.SEMAPHORE)
    ops = list(handle["xs"]) + list(handle["lands"])
    res = pl.pallas_call(
        body, name=name,
        out_shape=[pltpu.HBM(a.shape, a.dtype) for a in ops],
        in_specs=[hbm] * (2 * n) + [sem, sem, pl.BlockSpec(memory_space=pl.ANY)],
        out_specs=[hbm] * (2 * n),
        input_output_aliases={i: i for i in range(2 * n)},
        compiler_params=pltpu.CompilerParams(has_side_effects=pltpu.SideEffectType.DATAFLOW_SIDE_EFFECTING),
    )(*ops, handle["ssem"], handle["rsem"], after)
    me = 2 * lax.axis_index("x") + lax.axis_index("y")
    out = []
    for x, land in zip(res[:n], res[n:]):
        own = lax.dynamic_index_in_dim(x, me, 0, keepdims=False) if scatter else x
        out.append(lax.dynamic_update_index_in_dim(land, own, me, 0))
    return out


def _core_gather(xs, name):
    n = len(xs)

    def body(*refs):
        x_refs, o_refs, mine, got = refs[:n], refs[n:2 * n], refs[2 * n:3 * n], refs[3 * n:4 * n]
        lsem, ssem, rsem, osem = refs[4 * n:]
        mx, my, mc = _my_place()
        loads = [pltpu.make_async_copy(x_refs[i], mine[i], lsem.at[i]) for i in range(n)]
        for cp in loads:
            cp.start()
        sends, stores = [], []
        for i in range(n):
            loads[i].wait()
            cp = pltpu.make_async_remote_copy(
                src_ref=mine[i], dst_ref=got[i], send_sem=ssem.at[i], recv_sem=rsem.at[i],
                device_id=(mx, my, 1 - mc), device_id_type=MESH)
            cp.start()
            sends.append(cp)
            for k in range(4):
                st = pltpu.make_async_copy(mine[i].at[k], o_refs[i].at[k, mc], osem.at[i, k])
                st.start()
                stores.append(st)
        for i in range(n):
            sends[i].wait_recv()
            for k in range(4):
                st = pltpu.make_async_copy(got[i].at[k], o_refs[i].at[k, 1 - mc], osem.at[n + i, k])
                st.start()
                stores.append(st)
        for cp in sends:
            cp.wait_send()
        for st in stores:
            st.wait()

    hbm = pl.BlockSpec(memory_space=pl.ANY)
    bufs = [pltpu.VMEM(x.shape, x.dtype) for x in xs]
    return pl.pallas_call(
        body, name=name,
        out_shape=[_sds((4, 2) + tuple(x.shape[1:]), x.dtype) for x in xs],
        in_specs=[hbm] * n, out_specs=[hbm] * n,
        scratch_shapes=bufs + bufs + [pltpu.SemaphoreType.DMA((n,)), pltpu.SemaphoreType.DMA((n,)),
                                      pltpu.SemaphoreType.DMA((n,)), pltpu.SemaphoreType.DMA((2 * n, 4))],
        compiler_params=pltpu.CompilerParams(vmem_limit_bytes=VMEM_LIMIT),
    )(*xs)


def _rows_step(k):
    for r in (256, 128, 64, 32, 16, 8):
        if k % r == 0:
            return r
    return k


def _core_scatter_sum(gs, name):
    n = len(gs)

    def body(*refs):
        g_refs, o_refs = refs[:n], refs[n:2 * n]
        send, got, mine = refs[2 * n:3 * n], refs[3 * n:4 * n], refs[4 * n:5 * n]
        lsem, msem, ssem, rsem, osem = refs[5 * n:]
        mx, my, mc = _my_place()
        pairs = [(i, k) for i in range(n) for k in range(4)]
        out_loads = {(i, k): pltpu.make_async_copy(g_refs[i].at[k, 1 - mc], send[i].at[k], lsem.at[i, k])
                     for i, k in pairs}
        own_loads = {(i, k): pltpu.make_async_copy(g_refs[i].at[k, mc], mine[i].at[k], msem.at[i, k])
                     for i, k in pairs}
        for p in pairs:
            out_loads[p].start()
        for p in pairs:
            own_loads[p].start()
        sends = []
        for i in range(n):
            for k in range(4):
                out_loads[i, k].wait()
            cp = pltpu.make_async_remote_copy(
                src_ref=send[i], dst_ref=got[i], send_sem=ssem.at[i], recv_sem=rsem.at[i],
                device_id=(mx, my, 1 - mc), device_id_type=MESH)
            cp.start()
            sends.append(cp)
        stores = []
        for i in range(n):
            for k in range(4):
                own_loads[i, k].wait()
            sends[i].wait_recv()
            rows = g_refs[i].shape[2]
            step = _rows_step(rows)

            def add(r, _, i=i, step=step):
                sl = pl.ds(pl.multiple_of(r * step, step), step)
                for k in range(4):
                    mine[i][k, sl, :] = (mine[i][k, sl, :].astype(F32) + got[i][k, sl, :].astype(F32)).astype(BF16)
                return 0

            lax.fori_loop(0, rows // step, add, 0)
            st = pltpu.make_async_copy(mine[i], o_refs[i], osem.at[i])
            st.start()
            stores.append(st)
        for cp in sends:
            cp.wait_send()
        for st in stores:
            st.wait()

    hbm = pl.BlockSpec(memory_space=pl.ANY)
    blk = [(4,) + tuple(g.shape[2:]) for g in gs]
    bufs = [pltpu.VMEM(b, BF16) for b in blk]
    return pl.pallas_call(
        body, name=name,
        out_shape=[_sds(b, BF16) for b in blk],
        in_specs=[hbm] * n, out_specs=[hbm] * n,
        scratch_shapes=bufs * 3 + [pltpu.SemaphoreType.DMA((n, 4)), pltpu.SemaphoreType.DMA((n, 4)),
                                   pltpu.SemaphoreType.DMA((n,)), pltpu.SemaphoreType.DMA((n,)),
                                   pltpu.SemaphoreType.DMA((n,))],
        compiler_params=pltpu.CompilerParams(vmem_limit_bytes=VMEM_LIMIT),
    )(*gs)


def _all_gather(xs, name):
    by_chip = _chip_exchange(xs, name + "_chips", scatter=False)
    both = _core_gather(by_chip, name + "_cores")
    return [b.reshape((N_DEV,) + tuple(x.shape)) for b, x in zip(both, xs)]


def _ada_partial(c_all, w_ada_loc, b_loc):
    def body(c_ref, w_ref, b_ref, act_ref, mod_ref):
        c = c_ref[...]
        act = c * (1.0 / (1.0 + jnp.exp(-c)))
        act_ref[...] = act
        mod_ref[...] = _dot(act.astype(BF16), w_ref[...].astype(BF16)) + b_ref[...]

    nb, d = c_all.shape
    return pl.pallas_call(
        body, name="ada_partial",
        out_shape=(_sds((nb, d), F32), _sds((nb, w_ada_loc.shape[1]), F32)),
        compiler_params=pltpu.CompilerParams(vmem_limit_bytes=VMEM_LIMIT),
    )(c_all, w_ada_loc, b_loc)


_AFTER = pl.BlockSpec(memory_space=pl.ANY)


def _inproj_fwd(x, mod, ln_g, ln_b, w_in_p, w_uq_p, w_kv, gq, gkv, tc, ts1, ts2, dm, after):
    B, S, D = x.shape
    tm = dm["tm"]
    sbw, qr, kvr, nh = dm["sbw"], dm["qr"], dm["kvr"], dm["nh"]
    o_cq, o_ckv, o_kr = 3 * sbw, 3 * sbw + qr, 3 * sbw + qr + kvr
    qpw = nh * HEAD_PAD

    def body(x_ref, mod_ref, g_ref, b_ref, win_ref, wuq_ref, wkv_ref, gq_ref, gkv_ref, tc_ref, ts1_ref, ts2_ref, _,
             x0_ref, h_ref, q_ref, k_ref, v_ref, qp_ref, kp_ref, mv_ref, cq_ref, ckv_ref, qn_ref, kvn_ref):
        x0, _, _ = _ln_fwd(x_ref[0], g_ref[...], b_ref[...])
        x0_ref[0] = x0
        mod = mod_ref[0]
        h = (x0 * (1.0 + mod[1:2]) + mod[0:1]).astype(BF16)
        h_ref[0] = h
        proj = _dot(h, win_ref[...])
        q_ref[0] = (proj[:, 0:sbw] * (SB_HD ** -0.5)).astype(BF16)
        k_ref[0] = proj[:, sbw:2 * sbw].astype(BF16)
        v_ref[0] = proj[:, 2 * sbw:3 * sbw].astype(BF16)
        cq = proj[:, o_cq:o_cq + qr]
        ckv = proj[:, o_ckv:o_ckv + kvr]
        cq_ref[0] = cq
        ckv_ref[0] = ckv
        qn = (cq * lax.rsqrt(jnp.mean(cq * cq, axis=-1, keepdims=True) + RMS_EPS) * gq_ref[...]).astype(BF16)
        kvn = (ckv * lax.rsqrt(jnp.mean(ckv * ckv, axis=-1, keepdims=True) + RMS_EPS) * gkv_ref[...]).astype(BF16)
        qn_ref[0] = qn
        kvn_ref[0] = kvn
        c1, s1, s2 = tc_ref[...], ts1_ref[...], ts2_ref[...]
        c8, s18, s28 = jnp.tile(c1, (1, nh)), jnp.tile(s1, (1, nh)), jnp.tile(s2, (1, nh))
        qp_ref[0] = _rope(_dot(qn, wuq_ref[...]), c8, s18, s28).astype(BF16)
        kvo = _dot(kvn, wkv_ref[...])
        kr = pltpu.roll(proj[:, o_kr:o_kr + LANES], 64, 1)
        kr = _rope(kr, c1, s1, s2)
        kp_ref[0] = (kvo[:, 0:qpw] + jnp.tile(kr, (1, nh))).astype(BF16)
        mv_ref[0] = kvo[:, qpw:].astype(BF16)

    tab = pl.BlockSpec((tm, LANES), lambda b, s: (s, 0))
    outs = [(D, F32), (D, BF16), (sbw, BF16), (sbw, BF16), (sbw, BF16), (qpw, BF16), (qpw, BF16),
            (nh * MLA_V, BF16), (qr, F32), (kvr, F32), (qr, BF16), (kvr, BF16)]
    return pl.pallas_call(
        body, name="inproj_fwd", grid=(B, S // tm),
        in_specs=[_tok(tm, D), _perb(N_MOD, D), _full(ln_g), _full(ln_b), _full(w_in_p), _full(w_uq_p),
                  _full(w_kv), _full(gq), _full(gkv), tab, tab, tab, _AFTER],
        out_specs=[_tok(tm, w) for w, _ in outs],
        out_shape=[_sds((B, S, w), t) for w, t in outs],
        compiler_params=_cparams(("parallel", "parallel")),
    )(x, mod, ln_g, ln_b, w_in_p, w_uq_p, w_kv, gq, gkv, tc, ts1, ts2, after)


def _neg_abs(x):
    sign = jnp.uint32(0x80000000)
    return lax.bitcast_convert_type(lax.bitcast_convert_type(x, jnp.uint32) | sign, F32)


def _log2_keep(z):
    zs = z * (-LOG2E)
    return jnp.minimum(zs, 0.0) - jnp.log2(1.0 + jnp.exp2(_neg_abs(zs))), zs


def _split_dot(a, u):
    hi = a.astype(BF16)
    lo = (a - hi.astype(F32)).astype(BF16)
    return _dot(hi, u) + _dot(lo, u)


def _tri(n, rel):
    row = lax.broadcasted_iota(jnp.int32, (n, n), 0)
    col = lax.broadcasted_iota(jnp.int32, (n, n), 1)
    return rel(row, col).astype(BF16)


def _running_sum(a, tri, reverse, split):
    cs = tri.shape[0]
    n = a.shape[1] // cs
    out = [None] * n
    run = None
    for c in (reversed(range(n)) if reverse else range(n)):
        part = a[:, c * cs:(c + 1) * cs]
        loc = _split_dot(part, tri) if split else _dot(part.astype(BF16), tri)
        out[c] = loc if run is None else loc + run
        tot = jnp.sum(part, axis=1, keepdims=True)
        run = tot if run is None else run + tot
    return (out[0] if n == 1 else jnp.concatenate(out, axis=1)), run


def _tile_mask(nr, nk, r0, c0, rel):
    row = lax.broadcasted_iota(jnp.int32, (nr, nk), 0) + r0
    col = lax.broadcasted_iota(jnp.int32, (nr, nk), 1) + c0
    return rel(row, col)


def _put_rows(whole, part, r0):
    return part if r0 == 0 else jnp.concatenate([whole[:r0], part], axis=0)


def _diag_tiles(tq, split):
    half = tq // 2
    return [(0, tq, 0, half), (half, half, half, half)] if split else [(0, tq, 0, tq)]


def _sb_fwd(q, k, v, dm):
    B, S, W = q.shape
    tq = dm["tq"]
    nq = S // tq

    def body(q_ref, k_ref, v_ref, y_ref, tot_ref):
        qi = pl.program_id(2)
        q2 = q_ref[0]
        lane = lax.broadcasted_iota(jnp.int32, (tq, LANES), 1)
        qh = [jnp.where(lane < SB_HD, q2, 0).astype(BF16), jnp.where(lane >= SB_HD, q2, 0).astype(BF16)]

        def tile(j, carry, r0, nr, c0, nk, masked):
            off = pl.multiple_of(j * tq + c0, math.gcd(tq, c0))
            k2 = k_ref[0, pl.ds(off, nk), :]
            v2 = v_ref[0, pl.ds(off, nk), :]
            later = _tri(min(nk, CUMSUM_W), lambda a, b: a > b)
            strict = _tile_mask(nr, nk, r0, c0, lambda t, s: s < t) if masked else None
            heads = range(2)
            run = [carry[2 * h + 1][r0:r0 + nr] for h in heads]
            z = [_dot_nt(qh[h][r0:r0 + nr], k2) for h in heads]
            az = [_log2_keep(z[h]) for h in heads]
            a = [jnp.where(strict, az[h][0], 0.0) if masked else az[h][0] for h in heads]
            rs = [_running_sum(a[h], later, reverse=True, split=True) for h in heads]
            w = [jnp.exp2((a[h] - az[h][1]) + rs[h][0] + run[h]) for h in heads]
            if masked:
                w = [jnp.where(strict, w[h], 0.0) for h in heads]
            acc = [carry[2 * h][r0:r0 + nr] + _dot(w[h].astype(BF16), v2) for h in heads]
            out = []
            for h in heads:
                out += [_put_rows(carry[2 * h], acc[h], r0), _put_rows(carry[2 * h + 1], run[h] + rs[h][1], r0)]
            return tuple(out)

        zero = jnp.zeros((tq, LANES), F32)
        zrun = jnp.zeros((tq, 1), F32)
        carry = (zero, zrun, zero, zrun)
        for r0, nr, c0, nk in reversed(_diag_tiles(tq, False)):
            carry = tile(qi, carry, r0, nr, c0, nk, True)
        carry = lax.fori_loop(0, qi, lambda jj, c: tile(qi - 1 - jj, c, 0, tq, 0, tq, False), carry)
        y_ref[0] = jnp.where(lane < SB_HD, carry[0], carry[2]).astype(BF16)
        tot_ref[0] = jnp.where(lane < SB_HD, carry[1], carry[3])

    qspec = pl.BlockSpec((1, tq, LANES), lambda b, hp, i: (b, i, hp))
    kspec = pl.BlockSpec((1, S, LANES), lambda b, hp, i: (b, 0, hp))
    return pl.pallas_call(
        body, name="sb_fwd", grid=(B, W // LANES, nq),
        in_specs=[qspec, kspec, kspec],
        out_specs=[qspec, qspec],
        out_shape=[_sds((B, S, W), BF16), _sds((B, S, W), F32)],
        compiler_params=_cparams(("parallel", "parallel", "arbitrary")),
    )(q, k, v)


def _sb_bwd(q, k, v, tot, dy, dm, after):
    B, S, W = q.shape
    tq = dm["tq"]
    nq = S // tq

    def body(q_ref, k_ref, v_ref, tot_ref, dy_ref, _, dq_ref, dk_ref, dv_ref, dk_acc, dv_acc):
        qi = pl.program_id(2)

        @pl.when(qi == 0)
        def _():
            dk_acc[...] = jnp.zeros_like(dk_acc)
            dv_acc[...] = jnp.zeros_like(dv_acc)

        q2 = q_ref[0]
        dy2 = dy_ref[0]
        tot2 = tot_ref[0]
        lane = lax.broadcasted_iota(jnp.int32, (tq, LANES), 1)
        in_h = [lane < SB_HD, lane >= SB_HD]
        qh = [jnp.where(m, q2, 0).astype(BF16) for m in in_h]
        dyh = [jnp.where(m, dy2, 0).astype(BF16) for m in in_h]
        toth = [tot2[:, 0:1], tot2[:, SB_HD:SB_HD + 1]]

        def tile(j, carry, r0, nr, c0, nk, masked):
            off = pl.multiple_of(j * tq + c0, math.gcd(tq, c0))
            k2 = k_ref[0, pl.ds(off, nk), :]
            v2 = v_ref[0, pl.ds(off, nk), :]
            upto = _tri(min(nk, CUMSUM_W), lambda a, b: a <= b)
            before = _tri(min(nk, CUMSUM_W), lambda a, b: a < b)
            strict = _tile_mask(nr, nk, r0, c0, lambda t, s: s < t) if masked else None
            rows = slice(r0, r0 + nr)
            new = []
            dk_blk = jnp.zeros((nk, LANES), F32)
            dv_blk = jnp.zeros((nk, LANES), F32)
            for h in range(2):
                dq, pa, pg = carry[3 * h][rows], carry[3 * h + 1][rows], carry[3 * h + 2][rows]
                a, zs = _log2_keep(_dot_nt(qh[h][rows], k2))
                if masked:
                    a = jnp.where(strict, a, 0.0)
                a_upto, a_tot = _running_sum(a, upto, reverse=False, split=True)
                w = jnp.exp2((a - zs) + ((toth[h][rows] - pa) - a_upto))
                if masked:
                    w = jnp.where(strict, w, 0.0)
                g = _dot_nt(dyh[h][rows], v2) * w
                g_before, g_tot = _running_sum(g, before, reverse=False, split=False)
                g_before = g_before + pg
                dz = (g + g_before) * jnp.exp2(a) - g_before
                if masked:
                    dz = jnp.where(strict, dz, 0.0)
                dzb = dz.astype(BF16)
                dv_blk = dv_blk + _dot_tn(w.astype(BF16), dyh[h][rows])
                dk_blk = dk_blk + _dot_tn(dzb, qh[h][rows])
                new += [_put_rows(carry[3 * h], dq + _dot(dzb, k2), r0), _put_rows(carry[3 * h + 1], pa + a_tot, r0),
                        _put_rows(carry[3 * h + 2], pg + g_tot, r0)]
            dk_acc[pl.ds(off, nk), :] += dk_blk
            dv_acc[pl.ds(off, nk), :] += dv_blk
            return tuple(new)

        zero = jnp.zeros((tq, LANES), F32)
        zrun = jnp.zeros((tq, 1), F32)
        carry = lax.fori_loop(0, qi, lambda j, c: tile(j, c, 0, tq, 0, tq, False),
                              (zero, zrun, zrun, zero, zrun, zrun))
        for r0, nr, c0, nk in _diag_tiles(tq, False):
            carry = tile(qi, carry, r0, nr, c0, nk, True)
        dq_ref[0] = (jnp.where(in_h[0], carry[0], carry[3]) * (SB_HD ** -0.5)).astype(BF16)

        @pl.when(qi == nq - 1)
        def _():
            dk_ref[0] = dk_acc[...].astype(BF16)
            dv_ref[0] = dv_acc[...].astype(BF16)

    qspec = pl.BlockSpec((1, tq, LANES), lambda b, hp, i: (b, i, hp))
    kspec = pl.BlockSpec((1, S, LANES), lambda b, hp, i: (b, 0, hp))
    return pl.pallas_call(
        body, name="sb_bwd", grid=(B, W // LANES, nq),
        in_specs=[qspec, kspec, kspec, qspec, qspec, _AFTER],
        out_specs=[qspec, kspec, kspec],
        out_shape=[_sds((B, S, W), BF16)] * 3,
        scratch_shapes=[pltpu.VMEM((S, LANES), F32), pltpu.VMEM((S, LANES), F32)],
        compiler_params=_cparams(("parallel", "parallel", "arbitrary")),
    )(q, k, v, tot, dy, after)


def _same_or_earlier_chunk(row, col):
    return lax.shift_right_logical(col, 6) <= lax.shift_right_logical(row, 6)


def _mla_fwd(qp, kp, mv, dm, after):
    B, S, QW = qp.shape
    VW = mv.shape[2]
    tq = dm["tq"]
    nq = S // tq
    scale = (MLA_NOPE + MLA_ROPE) ** -0.5
    assert CHUNK == 64

    def body(q_ref, k_ref, v_ref, _, y_ref, lse_ref):
        qi = pl.program_id(2)
        q2 = q_ref[0]
        lane = lax.broadcasted_iota(jnp.int32, (tq, LANES), 1)

        def tile(j, carry, r0, nr, c0, nk, masked):
            off = pl.multiple_of(j * tq + c0, math.gcd(tq, c0))
            v2 = v_ref[0, pl.ds(off, nk), :]
            allowed = _tile_mask(nr, nk, r0, c0, _same_or_earlier_chunk) if masked else None
            rows = slice(r0, r0 + nr)
            heads = range(2)
            sl = [slice(h * HEAD_PAD, (h + 1) * HEAD_PAD) for h in heads]
            m_old = [carry[3 * h + 1][rows] for h in heads]
            s = [_dot_nt(q2[rows, sl[h]], k_ref[0, pl.ds(off, nk), sl[h]]) * (scale * LOG2E) for h in heads]
            if masked:
                s = [jnp.where(allowed, s[h], -1e30) for h in heads]
            m_new = [jnp.maximum(m_old[h], jnp.max(s[h], axis=1, keepdims=True)) for h in heads]
            alpha = [jnp.exp2(m_old[h] - m_new[h]) for h in heads]
            p = [jnp.exp2(s[h] - m_new[h]) for h in heads]
            acc = [alpha[h] * carry[3 * h][rows] + _dot(p[h].astype(BF16), v2) for h in heads]
            l = [alpha[h] * carry[3 * h + 2][rows] + jnp.sum(p[h], axis=1, keepdims=True) for h in heads]
            out = []
            for h in heads:
                out += [_put_rows(carry[3 * h], acc[h], r0), _put_rows(carry[3 * h + 1], m_new[h], r0),
                        _put_rows(carry[3 * h + 2], l[h], r0)]
            return tuple(out)

        zero = jnp.zeros((tq, LANES), F32)
        m0 = jnp.full((tq, 1), -1e30, F32)
        l0 = jnp.zeros((tq, 1), F32)
        carry = (zero, m0, l0, zero, m0, l0)
        for r0, nr, c0, nk in _diag_tiles(tq, False):
            carry = tile(qi, carry, r0, nr, c0, nk, True)
        carry = lax.fori_loop(0, qi, lambda j, c: tile(j, c, 0, tq, 0, tq, False), carry)
        y0 = carry[0] / carry[2]
        y1 = carry[3] / carry[5]
        y_ref[0] = jnp.where(lane < MLA_V, y0, y1).astype(BF16)
        lse_ref[0] = jnp.where(lane < MLA_V, carry[1] + jnp.log2(carry[2]), carry[4] + jnp.log2(carry[5]))

    qspec = pl.BlockSpec((1, tq, 2 * HEAD_PAD), lambda b, hp, i: (b, i, hp))
    kspec = pl.BlockSpec((1, S, 2 * HEAD_PAD), lambda b, hp, i: (b, 0, hp))
    vspec = pl.BlockSpec((1, S, LANES), lambda b, hp, i: (b, 0, hp))
    yspec = pl.BlockSpec((1, tq, LANES), lambda b, hp, i: (b, i, hp))
    return pl.pallas_call(
        body, name="mla_fwd", grid=(B, VW // LANES, nq),
        in_specs=[qspec, kspec, vspec, _AFTER],
        out_specs=[yspec, yspec],
        out_shape=[_sds((B, S, VW), BF16), _sds((B, S, VW), F32)],
        compiler_params=_cparams(("parallel", "parallel", "arbitrary")),
    )(qp, kp, mv, after)


def _mla_bwd(qp, kp, mv, y, lse, dy, dm, after):
    B, S, QW = qp.shape
    VW = mv.shape[2]
    tq = dm["tq"]
    nq = S // tq
    scale = (MLA_NOPE + MLA_ROPE) ** -0.5

    def body(q_ref, k_ref, v_ref, y_ref, lse_ref, dy_ref, _, dq_ref, dk_ref, dv_ref, dk_acc, dv_acc):
        qi = pl.program_id(2)

        @pl.when(qi == 0)
        def _():
            dk_acc[...] = jnp.zeros_like(dk_acc)
            dv_acc[...] = jnp.zeros_like(dv_acc)

        q2 = q_ref[0]
        dy2 = dy_ref[0]
        lse2 = lse_ref[0]
        lane = lax.broadcasted_iota(jnp.int32, (tq, LANES), 1)
        in_h = [lane < MLA_V, lane >= MLA_V]
        prod = dy2.astype(F32) * y_ref[0].astype(F32)
        delta = [jnp.sum(jnp.where(m, prod, 0.0), axis=1, keepdims=True) for m in in_h]
        dyh = [jnp.where(m, dy2, 0).astype(BF16) for m in in_h]
        lseh = [lse2[:, 0:1], lse2[:, MLA_V:MLA_V + 1]]

        def tile(j, carry, r0, nr, c0, nk, masked):
            off = pl.multiple_of(j * tq + c0, math.gcd(tq, c0))
            v2 = v_ref[0, pl.ds(off, nk), :]
            allowed = _tile_mask(nr, nk, r0, c0, _same_or_earlier_chunk) if masked else None
            rows = slice(r0, r0 + nr)
            heads = range(2)
            sl = [slice(h * HEAD_PAD, (h + 1) * HEAD_PAD) for h in heads]
            qhh = [q2[rows, sl[h]] for h in heads]
            dyr = [dyh[h][rows] for h in heads]
            kh = [k_ref[0, pl.ds(off, nk), sl[h]] for h in heads]
            s = [_dot_nt(qhh[h], kh[h]) * (scale * LOG2E) for h in heads]
            dp = [_dot_nt(dyr[h], v2) for h in heads]
            if masked:
                s = [jnp.where(allowed, s[h], -1e30) for h in heads]
            p = [jnp.exp2(s[h] - lseh[h][rows]) for h in heads]
            dv_acc[pl.ds(off, nk), :] += _dot_tn(p[0].astype(BF16), dyr[0]) + _dot_tn(p[1].astype(BF16), dyr[1])
            ds = [(p[h] * (dp[h] - delta[h][rows]) * scale).astype(BF16) for h in heads]
            for h in heads:
                dk_acc[pl.ds(off, nk), sl[h]] += _dot_tn(ds[h], qhh[h])
            return tuple(_put_rows(carry[h], carry[h][rows] + _dot(ds[h], kh[h]), r0) for h in heads)

        zero = jnp.zeros((tq, HEAD_PAD), F32)
        carry = lax.fori_loop(0, qi, lambda j, c: tile(j, c, 0, tq, 0, tq, False), (zero, zero))
        for r0, nr, c0, nk in _diag_tiles(tq, True):
            carry = tile(qi, carry, r0, nr, c0, nk, True)
        dq_ref[0] = jnp.concatenate([carry[0], carry[1]], axis=1).astype(BF16)

        @pl.when(qi == nq - 1)
        def _():
            dk_ref[0] = dk_acc[...].astype(BF16)
            dv_ref[0] = dv_acc[...].astype(BF16)

    qspec = pl.BlockSpec((1, tq, 2 * HEAD_PAD), lambda b, hp, i: (b, i, hp))
    kspec = pl.BlockSpec((1, S, 2 * HEAD_PAD), lambda b, hp, i: (b, 0, hp))
    vspec = pl.BlockSpec((1, S, LANES), lambda b, hp, i: (b, 0, hp))
    yspec = pl.BlockSpec((1, tq, LANES), lambda b, hp, i: (b, i, hp))
    return pl.pallas_call(
        body, name="mla_bwd", grid=(B, VW // LANES, nq),
        in_specs=[qspec, kspec, vspec, yspec, yspec, yspec, _AFTER],
        out_specs=[qspec, kspec, vspec],
        out_shape=[_sds((B, S, QW), BF16), _sds((B, S, QW), BF16), _sds((B, S, VW), BF16)],
        scratch_shapes=[pltpu.VMEM((S, 2 * HEAD_PAD), F32), pltpu.VMEM((S, LANES), F32)],
        compiler_params=_cparams(("parallel", "parallel", "arbitrary")),
    )(qp, kp, mv, y, lse, dy, after)


def _outproj_fwd(sb_y, mla_y, x0, mod, w_o, ln_g, ln_b, dm):
    B, S, D = x0.shape
    tm = dm["tm"]
    sbw = sb_y.shape[2]

    def body(ya_ref, yb_ref, x0_ref, mod_ref, wo_ref, g_ref, b_ref, mix_ref, x1_ref, h2_ref):
        mod = mod_ref[0]
        mix = _dot(ya_ref[0], wo_ref[0:sbw, :]) + _dot(yb_ref[0], wo_ref[sbw:, :])
        mix_ref[0] = mix
        x1, _, _ = _ln_fwd(ALPHA * x0_ref[0] + (1.0 + mod[2:3]) * mix, g_ref[...], b_ref[...])
        x1_ref[0] = x1
        h2_ref[0] = (x1 * (1.0 + mod[4:5]) + mod[3:4]).astype(BF16)

    return pl.pallas_call(
        body, name="outproj_fwd", grid=(B, S // tm),
        in_specs=[_tok(tm, sbw), _tok(tm, mla_y.shape[2]), _tok(tm, D), _perb(N_MOD, D),
                  _full(w_o), _full(ln_g), _full(ln_b)],
        out_specs=[_tok(tm, D)] * 3,
        out_shape=[_sds((B, S, D), F32), _sds((B, S, D), F32), _sds((B, S, D), BF16)],
        compiler_params=_cparams(("parallel", "parallel")),
    )(sb_y, mla_y, x0, mod, w_o, ln_g, ln_b)


def _stat_specs(B, D):
    specs = [pl.BlockSpec((1, 8, D), lambda b, s: (b, 0, 0)), pl.BlockSpec((8, D), lambda b, s: (0, 0))]
    shapes = [_sds((B, 8, D), F32), _sds((8, D), F32)]
    return specs, shapes


def _stat_init(bst_ref, wst_ref):
    @pl.when(pl.program_id(1) == 0)
    def _():
        bst_ref[...] = jnp.zeros_like(bst_ref)

    @pl.when((pl.program_id(0) == 0) & (pl.program_id(1) == 0))
    def _():
        wst_ref[...] = jnp.zeros_like(wst_ref)


def _mlp_fwd(h2, x1, mod, target, w_up, w_down, ln_g, ln_b, dm):
    B, S, D = x1.shape
    tm = dm["tm"]
    nck, _, ck = w_up.shape
    dff = nck * ck

    def body(h2_ref, x1_ref, mod_ref, t_ref, wu_ref, wd_ref, g_ref, b_ref, u_ref, dr_ref, bst_ref, wst_ref):
        _stat_init(bst_ref, wst_ref)
        mod = mod_ref[0]
        h2 = h2_ref[0]
        ff = jnp.zeros((tm, D), F32)
        for c in range(nck):
            u = _dot(h2, wu_ref[c])
            u_ref[0, :, c * ck:(c + 1) * ck] = u.astype(BF16)
            act = jnp.square(jnp.maximum(u, 0.0)).astype(BF16)
            ff = ff + _dot(act, wd_ref[c])
        g = g_ref[...]
        x2, xhat, rstd = _ln_fwd(ALPHA * x1_ref[0] + (1.0 + mod[5:6]) * ff, g, b_ref[...])
        err = x2 - t_ref[0]
        dy = err * (1.0 / D)
        dr = _ln_bwd(dy, xhat, rstd, g)
        dr_ref[0] = dr
        bst_ref[0, 0:1, :] += _colsum(dr * ff)
        wst_ref[0:1, :] += _colsum(dy * xhat)
        wst_ref[1:2, :] += _colsum(dy)
        wst_ref[2:3, :] += _colsum(err * err) * (0.5 / D)

    sspecs, sshapes = _stat_specs(B, D)
    return pl.pallas_call(
        body, name="mlp_fwd", grid=(B, S // tm),
        in_specs=[_tok(tm, D), _tok(tm, D), _perb(N_MOD, D), _tok(tm, D), _full(w_up), _full(w_down),
                  _full(ln_g), _full(ln_b)],
        out_specs=[_tok(tm, dff), _tok(tm, D)] + sspecs,
        out_shape=[_sds((B, S, dff), BF16), _sds((B, S, D), F32)] + sshapes,
        compiler_params=_cparams(("arbitrary", "arbitrary")),
    )(h2, x1, mod, target, w_up, w_down, ln_g, ln_b)


def _mlp_bwd(dr2, u, x1, x0, mix, mod, w_up, w_down, w_o, ln_g, dm):
    B, S, D = x1.shape
    tm = dm["tm_small"]
    sbw = dm["sbw"]
    nck, _, ck = w_up.shape
    dff = nck * ck

    def body(dr_ref, u_ref, x1_ref, x0_ref, mix_ref, mod_ref, wu_ref, wd_ref, wo_ref, g_ref,
             du_ref, dff_ref, dmix_ref, dx0_ref, dya_ref, dyb_ref, bst_ref, wst_ref):
        _stat_init(bst_ref, wst_ref)
        mod = mod_ref[0]
        dr2 = dr_ref[0]
        dffv = ((1.0 + mod[5:6]) * dr2).astype(BF16)
        dff_ref[0] = dffv
        dh2 = jnp.zeros((tm, D), F32)
        for c in range(nck):
            sl = slice(c * ck, (c + 1) * ck)
            da = _dot_nt(dffv, wd_ref[c])
            du = (da * (2.0 * jnp.maximum(u_ref[0, :, sl].astype(F32), 0.0))).astype(BF16)
            du_ref[0, :, sl] = du
            dh2 = dh2 + _dot_nt(du, wu_ref[c])
        x1 = x1_ref[0]
        dx1 = ALPHA * dr2 + dh2 * (1.0 + mod[4:5])
        bst_ref[0, 0:1, :] += _colsum(dh2 * x1)
        bst_ref[0, 1:2, :] += _colsum(dh2)
        mix = mix_ref[0]
        g = g_ref[...]
        _, xhat, rstd = _ln_fwd(ALPHA * x0_ref[0] + (1.0 + mod[2:3]) * mix, g, 0.0)
        dr1 = _ln_bwd(dx1, xhat, rstd, g)
        wst_ref[0:1, :] += _colsum(dx1 * xhat)
        wst_ref[1:2, :] += _colsum(dx1)
        bst_ref[0, 2:3, :] += _colsum(dr1 * mix)
        dx0_ref[0] = ALPHA * dr1
        dmix = ((1.0 + mod[2:3]) * dr1).astype(BF16)
        dmix_ref[0] = dmix
        dya_ref[0] = _dot_nt(dmix, wo_ref[0:sbw, :]).astype(BF16)
        dyb_ref[0] = _dot_nt(dmix, wo_ref[sbw:, :]).astype(BF16)

    sspecs, sshapes = _stat_specs(B, D)
    wa, wb = sbw, w_o.shape[0] - sbw
    return pl.pallas_call(
        body, name="mlp_bwd", grid=(B, S // tm),
        in_specs=[_tok(tm, D), _tok(tm, dff), _tok(tm, D), _tok(tm, D), _tok(tm, D), _perb(N_MOD, D),
                  _full(w_up), _full(w_down), _full(w_o), _full(ln_g)],
        out_specs=[_tok(tm, dff), _tok(tm, D), _tok(tm, D), _tok(tm, D), _tok(tm, wa), _tok(tm, wb)] + sspecs,
        out_shape=[_sds((B, S, dff), BF16), _sds((B, S, D), BF16), _sds((B, S, D), BF16), _sds((B, S, D), F32),
                   _sds((B, S, wa), BF16), _sds((B, S, wb), BF16)] + sshapes,
        compiler_params=_cparams(("arbitrary", "arbitrary")),
    )(dr2, u, x1, x0, mix, mod, w_up, w_down, w_o, ln_g)


def _inproj_bwd(x, x0, dx0a, mod, ln_g, dq, dk, dv, dqp, dkp, dmv, cq, ckv, w_in_p, w_uq_p, w_kv, gq, gkv,
                tc, ts1, ts2, dm):
    B, S, D = x.shape
    tm = dm["tm"]
    sbw, qr, kvr, nh = dm["sbw"], dm["qr"], dm["kvr"], dm["nh"]
    qpw = nh * HEAD_PAD
    dinp = w_in_p.shape[1]
    kvw = w_kv.shape[1]

    def body(x_ref, x0_ref, dx0a_ref, mod_ref, g_ref, dq_ref, dk_ref, dv_ref, dqp_ref, dkp_ref, dmv_ref,
             cq_ref, ckv_ref, win_ref, wuq_ref, wkv_ref, gq_ref, gkv_ref, tc_ref, ts1_ref, ts2_ref,
             gx_ref, dproj_ref, dqpre_ref, dkvo_ref, bst_ref, wst_ref):
        _stat_init(bst_ref, wst_ref)
        mod = mod_ref[0]
        c1, s1, s2 = tc_ref[...], ts1_ref[...], ts2_ref[...]
        c8, s18, s28 = jnp.tile(c1, (1, nh)), jnp.tile(s1, (1, nh)), jnp.tile(s2, (1, nh))
        dqpre = _rope_t(dqp_ref[0].astype(F32), c8, s18, s28).astype(BF16)
        dqpre_ref[0] = dqpre
        gq = gq_ref[...]
        cq = cq_ref[0]
        rq = lax.rsqrt(jnp.mean(cq * cq, axis=-1, keepdims=True) + RMS_EPS)
        dqn = _dot_nt(dqpre, wuq_ref[...])
        wst_ref[4:5, 0:qr] += _colsum(dqn * cq * rq)
        dqg = dqn * gq
        dcq = rq * dqg - cq * (rq * rq * rq) * jnp.mean(dqg * cq, axis=-1, keepdims=True)

        dkpre = _rope_t(dkp_ref[0].astype(F32), c8, s18, s28)
        dkr = dkpre[:, 0:HEAD_PAD]
        for h in range(1, nh):
            dkr = dkr + dkpre[:, h * HEAD_PAD:(h + 1) * HEAD_PAD]
        lane = lax.broadcasted_iota(jnp.int32, (tm, LANES), 1)
        dkr = jnp.where((lane >= MLA_NOPE) & (lane < MLA_NOPE + MLA_ROPE), dkr, 0.0)
        dkr = pltpu.roll(dkr, LANES - MLA_NOPE, 1)
        dkvo = jnp.concatenate([dkpre.astype(BF16), dmv_ref[0]], axis=1)
        dkvo_ref[0] = dkvo
        gkv = gkv_ref[...]
        ckv = ckv_ref[0]
        rkv = lax.rsqrt(jnp.mean(ckv * ckv, axis=-1, keepdims=True) + RMS_EPS)
        dkvn = _dot_nt(dkvo, wkv_ref[...])
        wst_ref[5:6, 0:kvr] += _colsum(dkvn * ckv * rkv)
        dkg = dkvn * gkv
        dckv = rkv * dkg - ckv * (rkv * rkv * rkv) * jnp.mean(dkg * ckv, axis=-1, keepdims=True)

        dproj = jnp.concatenate([dq_ref[0], dk_ref[0], dv_ref[0], dcq.astype(BF16), dckv.astype(BF16),
                                 dkr.astype(BF16)], axis=1)
        dproj_ref[0] = dproj
        dh = _dot_nt(dproj, win_ref[...])
        x0 = x0_ref[0]
        dx0 = dx0a_ref[0] + dh * (1.0 + mod[1:2])
        bst_ref[0, 0:1, :] += _colsum(dh * x0)
        bst_ref[0, 1:2, :] += _colsum(dh)
        g = g_ref[...]
        _, xhat, rstd = _ln_fwd(x_ref[0], g, 0.0)
        gx_ref[0] = _ln_bwd(dx0, xhat, rstd, g)
        wst_ref[0:1, :] += _colsum(dx0 * xhat)
        wst_ref[1:2, :] += _colsum(dx0)

    tab = pl.BlockSpec((tm, LANES), lambda b, s: (s, 0))
    sspecs, sshapes = _stat_specs(B, D)
    return pl.pallas_call(
        body, name="inproj_bwd", grid=(B, S // tm),
        in_specs=[_tok(tm, D), _tok(tm, D), _tok(tm, D), _perb(N_MOD, D), _full(ln_g),
                  _tok(tm, sbw), _tok(tm, sbw), _tok(tm, sbw), _tok(tm, qpw), _tok(tm, qpw), _tok(tm, nh * MLA_V),
                  _tok(tm, qr), _tok(tm, kvr), _full(w_in_p), _full(w_uq_p), _full(w_kv), _full(gq), _full(gkv),
                  tab, tab, tab],
        out_specs=[_tok(tm, D), _tok(tm, dinp), _tok(tm, qpw), _tok(tm, kvw)] + sspecs,
        out_shape=[_sds((B, S, D), F32), _sds((B, S, dinp), BF16), _sds((B, S, qpw), BF16),
                   _sds((B, S, kvw), BF16)] + sshapes,
        compiler_params=_cparams(("arbitrary", "arbitrary")),
    )(x, x0, dx0a, mod, ln_g, dq, dk, dv, dqp, dkp, dmv, cq, ckv, w_in_p, w_uq_p, w_kv, gq, gkv, tc, ts1, ts2)


def _tile_of(n, cap):
    if n <= cap:
        return n
    best = n
    for t in range(LANES, cap + 1, LANES):
        if n % t == 0:
            best = t
    return best


def _mm_tn(a, g, name, relu_sq=False, out_dtype=F32, col_blocks=None):
    T, K = a.shape
    N = g.shape[1]
    tt = 1024 if T % 1024 == 0 else (512 if T % 512 == 0 else T)
    tk = _tile_of(K, 1024)
    tn = _tile_of(N, 1280)
    nt = T // tt
    bw = N // col_blocks if col_blocks else tn
    assert tn % bw == 0

    def body(a_ref, g_ref, o_ref, acc_ref):
        @pl.when(pl.program_id(2) == 0)
        def _():
            acc_ref[...] = jnp.zeros_like(acc_ref)

        av = a_ref[...]
        if relu_sq:
            av = jnp.square(jnp.maximum(av.astype(F32), 0.0)).astype(BF16)
        acc_ref[...] += _dot_tn(av, g_ref[...])

        @pl.when(pl.program_id(2) == nt - 1)
        def _():
            if col_blocks:
                for c in range(tn // bw):
                    o_ref[c] = acc_ref[:, c * bw:(c + 1) * bw].astype(out_dtype)
            else:
                o_ref[...] = acc_ref[...].astype(out_dtype)

    if col_blocks:
        out_spec = pl.BlockSpec((tn // bw, tk, bw), lambda i, j, t: (j, i, 0))
        out_shape = _sds((col_blocks, K, bw), out_dtype)
    else:
        out_spec = pl.BlockSpec((tk, tn), lambda i, j, t: (i, j))
        out_shape = _sds((K, N), out_dtype)
    return pl.pallas_call(
        body, name=name, grid=(K // tk, N // tn, nt),
        in_specs=[pl.BlockSpec((tt, tk), lambda i, j, t: (t, i)), pl.BlockSpec((tt, tn), lambda i, j, t: (t, j))],
        out_specs=out_spec, out_shape=out_shape,
        scratch_shapes=[pltpu.VMEM((tk, tn), F32)],
        compiler_params=_cparams(("parallel", "parallel", "arbitrary")),
    )(a, g)


def _reduce_adamw(parts, w, m, v, name):
    _, K, N = parts.shape
    tr = 256 if K % 256 == 0 else K

    def body(p_ref, w_ref, m_ref, v_ref, g_ref, d_ref, nm_ref, nv_ref):
        g = p_ref[0].astype(F32)
        for k in range(1, 4):
            g = g + p_ref[k].astype(F32)
        g_ref[0] = g
        d_ref[0], nm_ref[0], nv_ref[0] = _adamw(w_ref[0], g, m_ref[0], v_ref[0])

    spec = pl.BlockSpec((1, tr, N), lambda r: (0, r, 0))
    return pl.pallas_call(
        body, name=name, grid=(K // tr,),
        in_specs=[pl.BlockSpec((4, tr, N), lambda r: (0, r, 0)), spec, spec, spec],
        out_specs=[spec] * 4, out_shape=[_sds((1, K, N), F32)] * 4,
        compiler_params=_cparams(("parallel",)),
    )(parts, w, m, v)


def _finish(sm, dmod_all, dmod_my, cact_all, p_small, m_small, v_small, b_ada, m_b, v_b, w_ada, m_w, v_w):
    n0 = p_small.shape[1]
    n1 = sm.shape[1]
    d = cact_all.shape[1]

    def body(sm_ref, dma_ref, dmm_ref, ca_ref, p_ref, pm_ref, pv_ref, b_ref, bm_ref, bv_ref, w_ref, wm_ref, wv_ref,
             gs_ref, ds_ref, ms_ref, vs_ref, gb_ref, db_ref, mb_ref, vb_ref, gw_ref, dw_ref, mw_ref, vw_ref,
             loss_ref):
        gs = sm_ref[0:1, :]
        for k in range(1, N_DEV):
            gs = gs + sm_ref[k:k + 1, :]
        gs_ref[...] = gs
        ds_ref[...], ms_ref[...], vs_ref[...] = _adamw(p_ref[...], gs[:, 0:n0], pm_ref[...], pv_ref[...])
        loss_ref[...] = jnp.zeros((1, LANES), F32) + jnp.sum(gs[:, n1 - d:n1])
        gb = jnp.sum(dma_ref[...], axis=0, keepdims=True)
        gb_ref[...] = gb
        db_ref[...], mb_ref[...], vb_ref[...] = _adamw(b_ref[...], gb, bm_ref[...], bv_ref[...])
        gw = _dot_tn(ca_ref[...].astype(BF16), dmm_ref[...].astype(BF16))
        gw_ref[...] = gw
        dw_ref[...], mw_ref[...], vw_ref[...] = _adamw(w_ref[...], gw, wm_ref[...], wv_ref[...])

    s0 = _sds(p_small.shape, F32)
    sb = _sds(b_ada.shape, F32)
    sw = _sds(w_ada.shape, F32)
    return pl.pallas_call(
        body, name="finish_small",
        out_shape=[_sds((1, n1), F32), s0, s0, s0, sb, sb, sb, sb, sw, sw, sw, sw,
                   _sds((1, LANES), F32)],
        compiler_params=pltpu.CompilerParams(vmem_limit_bytes=VMEM_LIMIT),
    )(sm, dmod_all, dmod_my, cact_all, p_small, m_small, v_small, b_ada, m_b, v_b, w_ada, m_w, v_w)


def _pack(arrs, dtype, width):
    flat = jnp.concatenate([a.astype(dtype).reshape(-1) for a in arrs])
    rows = -(-flat.shape[0] // (256 * width)) * 256
    return jnp.pad(flat, (0, rows * width - flat.shape[0])).reshape(rows, width)


def _unpack(slab, shapes):
    flat = slab.reshape(-1)
    out, o = [], 0
    for s in shapes:
        n = math.prod(s)
        out.append(flat[o:o + n].reshape(s))
        o += n
    return out


def _rope_tables(S):
    inv_freq = 1.0 / (ROPE_BASE ** (jnp.arange(0, MLA_ROPE, 2, dtype=F32) / MLA_ROPE))
    ang = jnp.arange(S, dtype=F32)[:, None] * inv_freq[None, :]
    cos, sin = jnp.cos(ang), jnp.sin(ang)
    one = jnp.ones((S, MLA_NOPE), F32)
    z16 = jnp.zeros((S, 16), F32)
    z32 = jnp.zeros((S, 32), F32)
    z64 = jnp.zeros((S, MLA_NOPE), F32)
    tc = jnp.concatenate([one, cos, cos, jnp.ones((S, 32), F32)], axis=1)
    ts1 = jnp.concatenate([z64, -sin, z16, z32], axis=1)
    ts2 = jnp.concatenate([z64, z16, sin, z32], axis=1)
    return tc, ts1, ts2


def kernel(x, c, ln_in_g, ln_in_b, w_ada, b_ada, w_in, q_norm_g, kv_norm_g, w_uq, w_ukv, w_o, ln1_g, ln1_b, w_up, w_down, ln2_g, ln2_b, loss_target, m_ln_in_g, m_ln_in_b, m_w_ada, m_b_ada, m_w_in, m_q_norm_g, m_kv_norm_g, m_w_uq, m_w_ukv, m_w_o, m_ln1_g, m_ln1_b, m_w_up, m_w_down, m_ln2_g, m_ln2_b, v_ln_in_g, v_ln_in_b, v_w_ada, v_b_ada, v_w_in, v_q_norm_g, v_kv_norm_g, v_w_uq, v_w_ukv, v_w_o, v_ln1_g, v_ln1_b, v_w_up, v_w_down, v_ln2_g, v_ln2_b):
    B, S, D = x.shape
    sbw = D // 2
    mlw = D - sbw
    nh = mlw // MLA_V
    qr = w_uq.shape[1]
    kvr = w_ukv.shape[1]
    qk = MLA_NOPE + MLA_ROPE
    dff = w_up.shape[2] * N_DEV
    din = w_in.shape[2] * N_DEV
    tm = 512 if S % 512 == 0 else S
    tq = min(512, S // 2)
    dm = dict(tm=tm, tm_small=min(tm, 256), tq=tq, sbw=sbw, qr=qr, kvr=kvr, nh=nh)
    width = 1024 if D >= 1024 else LANES
    dev = 4 * lax.axis_index("x") + 2 * lax.axis_index("y") + lax.axis_index("c")

    big = [w_in, w_uq, w_ukv, w_o, w_up, w_down]
    first_w, first_token = _chip_exchange_start([a[0].astype(BF16) for a in big[:3]], "gather_w_first_start",
                                                scatter=False, after=c)

    nada = w_ada.shape[2]
    c_all = _all_gather([c + first_token[0, 0]], "gather_c")[0].reshape(N_DEV * B, D)
    b_loc = lax.dynamic_slice(b_ada, (0, dev * nada), (1, nada))
    cact_all, mod_part = _ada_partial(c_all, w_ada[0], b_loc)
    mod_all = _all_gather([mod_part], "gather_mod")[0]
    mod = lax.dynamic_slice(mod_all, (0, dev * B, 0), (N_DEV, B, nada))
    mod = jnp.swapaxes(mod, 0, 1).reshape(B, N_MOD, D)

    first_by_chip = _chip_exchange_wait(first_w, mod_all, "gather_w_first_wait")
    w_in8, w_uq8, w_ukv8 = [b.reshape((N_DEV,) + b.shape[2:]) for b in _core_gather(first_by_chip, "gather_w_first_cores")]
    late_w, late_token = _chip_exchange_start([a[0].astype(BF16) for a in big[3:]], "gather_w_late_start",
                                              scatter=False, after=w_in8)
    cols = lambda a8: jnp.swapaxes(a8, 0, 1).reshape(a8.shape[1], N_DEV * a8.shape[2])
    w_in_p = jnp.pad(cols(w_in8), ((0, 0), (0, LANES - MLA_ROPE)))
    zpad = jnp.zeros((qr, nh, HEAD_PAD - qk), BF16)
    w_uq_p = jnp.concatenate([cols(w_uq8).reshape(qr, nh, qk), zpad], axis=2).reshape(qr, nh * HEAD_PAD)
    w_ukv_f = cols(w_ukv8)
    w_uk = w_ukv_f[:, :nh * MLA_NOPE].reshape(kvr, nh, MLA_NOPE)
    w_uk_p = jnp.concatenate([w_uk, jnp.zeros((kvr, nh, HEAD_PAD - MLA_NOPE), BF16)], axis=2)
    w_kv = jnp.concatenate([w_uk_p.reshape(kvr, nh * HEAD_PAD), w_ukv_f[:, nh * MLA_NOPE:]], axis=1)

    tc, ts1, ts2 = _rope_tables(S)
    g_in, b_in = ln_in_g.reshape(1, D), ln_in_b.reshape(1, D)
    (x0, h, sq, sk, sv, qp, kp, mv, cq, ckv, qn, kvn) = _inproj_fwd(
        x, mod, g_in, b_in, w_in_p, w_uq_p, w_kv, q_norm_g, kv_norm_g, tc, ts1, ts2, dm, late_token)
    sb_y, sb_tot = _sb_fwd(sq, sk, sv, dm)
    mla_y, mla_lse = _mla_fwd(qp, kp, mv, dm, sb_tot)
    late_by_chip = _chip_exchange_wait(late_w, mla_lse, "gather_w_late_wait")
    w_o8, w_up8, w_down8 = [b.reshape((N_DEV,) + b.shape[2:]) for b in _core_gather(late_by_chip, "gather_w_late_cores")]
    w_o_f = w_o8.reshape(D, D)
    mix, x1, h2 = _outproj_fwd(sb_y, mla_y, x0, mod, w_o_f, ln1_g, ln1_b, dm)
    u, dr2, bst_c, wst_c = _mlp_fwd(h2, x1, mod, loss_target, w_up8, w_down8, ln2_g, ln2_b, dm)

    du, dffb, dmixb, dx0a, dsb_y, dmla_y, bst_b, wst_b = _mlp_bwd(
        dr2, u, x1, x0, mix, mod, w_up8, w_down8, w_o_f, ln1_g, dm)
    T = B * S
    r2 = lambda a: a.reshape(T, a.shape[2])
    by_core = lambda a: a.reshape((4, 2) + a.shape[1:])
    g_o = jnp.concatenate([_mm_tn(r2(sb_y), r2(dmixb), "grad_w_o_sb", out_dtype=BF16),
                           _mm_tn(r2(mla_y), r2(dmixb), "grad_w_o_mla", out_dtype=BF16)], axis=0)
    g_up8 = _mm_tn(r2(h2), r2(du), "grad_w_up", out_dtype=BF16, col_blocks=N_DEV)
    g_down = _mm_tn(r2(u), r2(dffb), "grad_w_down", relu_sq=True, out_dtype=BF16)
    early = [g_o.reshape(N_DEV, D // N_DEV, D), g_up8, g_down.reshape(N_DEV, dff // N_DEV, D)]
    early_sum = _core_scatter_sum([by_core(a) for a in early], "scatter_g_early_cores")
    early_g, early_token = _chip_exchange_start(early_sum, "scatter_g_early_start", scatter=True, after=dr2)

    dsq, dsk, dsv = _sb_bwd(sq, sk, sv, sb_tot, dsb_y, dm, early_token)
    dqp, dkp, dmv = _mla_bwd(qp, kp, mv, mla_y, mla_lse, dmla_y, dm, dsq)
    grad_x, dproj, dqpre, dkvo, bst_a, wst_a = _inproj_bwd(
        x, x0, dx0a, mod, g_in, dsq, dsk, dsv, dqp, dkp, dmv, cq, ckv, w_in_p, w_uq_p, w_kv, q_norm_g, kv_norm_g,
        tc, ts1, ts2, dm)
    g_in_p = _mm_tn(r2(h), r2(dproj), "grad_w_in")
    g_uq_p = _mm_tn(r2(qn), r2(dqpre), "grad_w_uq")
    g_kv = _mm_tn(r2(kvn), r2(dkvo), "grad_w_kv")
    g_uq_f = g_uq_p.reshape(qr, nh, HEAD_PAD)[:, :, :qk].reshape(qr, nh * qk)
    g_uk = g_kv[:, :nh * HEAD_PAD].reshape(kvr, nh, HEAD_PAD)[:, :, :MLA_NOPE].reshape(kvr, nh * MLA_NOPE)
    g_ukv_f = jnp.concatenate([g_uk, g_kv[:, nh * HEAD_PAD:]], axis=1)
    early_quarter = _chip_exchange_wait(early_g, g_kv, "scatter_g_early_wait")

    def by_dest_cols(a):
        k, n = a.shape[0], a.shape[1] // N_DEV
        return jnp.swapaxes(a.reshape(k, N_DEV, n), 0, 1).astype(BF16)

    last = [by_dest_cols(g_in_p[:, :din]), by_dest_cols(g_uq_f), by_dest_cols(g_ukv_f)]
    last_sum = _core_scatter_sum([by_core(a) for a in last], "scatter_g_last_cores")
    last_g, last_token = _chip_exchange_start(last_sum, "scatter_g_last_start", scatter=True, after=grad_x)
    names = ["w_in", "w_uq", "w_ukv", "w_o", "w_up", "w_down"]
    moms = [m_w_in, m_w_uq, m_w_ukv, m_w_o, m_w_up, m_w_down]
    vars_ = [v_w_in, v_w_uq, v_w_ukv, v_w_o, v_w_up, v_w_down]
    res_early = [_reduce_adamw(p, w, m, v, "adamw_" + n)
                 for p, w, m, v, n in zip(early_quarter, big[3:], moms[3:], vars_[3:], names[3:])]

    dmod = jnp.concatenate([bst_a[:, 1], bst_a[:, 0], bst_b[:, 2], bst_b[:, 1], bst_b[:, 0], bst_c[:, 0]], axis=1)
    small = jnp.concatenate([wst_a[0], wst_a[1], wst_a[4, :qr], wst_a[5, :kvr], wst_b[0], wst_b[1],
                             wst_c[0], wst_c[1], wst_c[2]]) + last_token[0, 0]
    n1 = small.shape[0]
    both = _all_gather([_pack([dmod, small], F32, LANES)], "gather_small")[0].reshape(N_DEV, -1)
    dmod_all = both[:, :B * N_MOD * D].reshape(N_DEV * B, N_MOD * D)
    sm = both[:, B * N_MOD * D:B * N_MOD * D + n1]
    dmod_my = lax.dynamic_slice(dmod_all, (0, dev * nada), (N_DEV * B, nada))
    row = lambda arrs: jnp.concatenate([a.reshape(1, -1) for a in arrs], axis=1)
    smalls = [ln_in_g, ln_in_b, q_norm_g, kv_norm_g, ln1_g, ln1_b, ln2_g, ln2_b]
    small_shapes = [a.shape for a in smalls]
    (gs, ds, nms, nvs, g_b, d_b, nm_b, nv_b, g_w, d_w, nm_w, nv_w, loss_v) = _finish(
        sm, dmod_all, dmod_my, cact_all, row(smalls),
        row([m_ln_in_g, m_ln_in_b, m_q_norm_g, m_kv_norm_g, m_ln1_g, m_ln1_b, m_ln2_g, m_ln2_b]),
        row([v_ln_in_g, v_ln_in_b, v_q_norm_g, v_kv_norm_g, v_ln1_g, v_ln1_b, v_ln2_g, v_ln2_b]),
        b_ada, m_b_ada, v_b_ada, w_ada[0], m_w_ada[0], v_w_ada[0])
    gsm, dsm, nmsm, nvsm = (_unpack(s, small_shapes) for s in (gs, ds, nms, nvs))
    last_quarter = _chip_exchange_wait(last_g, loss_v, "scatter_g_last_wait")
    res_last = [_reduce_adamw(p, w, m, v, "adamw_" + n)
                for p, w, m, v, n in zip(last_quarter, big[:3], moms[:3], vars_[:3], names[:3])]
    gb, db, nmb, nvb = ([r[i] for r in res_last + res_early] for i in range(4))

    def ordered(sm_l, w_l, ada_w, ada_b):
        return [sm_l[0], sm_l[1], ada_w[None], ada_b, w_l[0], sm_l[2], sm_l[3], w_l[1], w_l[2], w_l[3],
                sm_l[4], sm_l[5], w_l[4], w_l[5], sm_l[6], sm_l[7]]

    loss = loss_v[0, 0]
    return (loss, grad_x, *ordered(gsm, gb, g_w, g_b), *ordered(dsm, db, d_w, d_b),
            *ordered(nmsm, nmb, nm_w, nm_b), *ordered(nvsm, nvb, nv_w, nv_b))
```

```python
import functools
import math

import jax
import jax.numpy as jnp
from jax import lax
from jax.experimental import pallas as pl
from jax.experimental.pallas import tpu as pltpu

F32 = jnp.float32
BF16 = jnp.bfloat16

SB_HD = 64
MLA_V = 64
MLA_NOPE = 64
MLA_ROPE = 32
HEAD_PAD = 128
CHUNK = 64
ROPE_BASE = 10000.0
LN_EPS = 1e-5
RMS_EPS = 1e-6
DEPTH = 1
ALPHA = (2.0 * DEPTH) ** 0.25
N_MOD = 6
ADAM_LR = 0.001
ADAM_B1 = 0.9
ADAM_B2 = 0.999
ADAM_EPS = 1e-08
ADAM_WD = 0.01
ADAM_STEP = 10
N_DEV = 8
LANES = 128
LOG2E = 1.4426950408889634
CUMSUM_W = 256
VMEM_LIMIT = 56 * 1024 * 1024
MESH = pl.DeviceIdType.MESH


def _dot(a, b):
    return jnp.dot(a, b, preferred_element_type=F32)


def _dot_nt(a, b):
    return lax.dot_general(a, b, (((1,), (1,)), ((), ())), preferred_element_type=F32)


def _dot_tn(a, b):
    return lax.dot_general(a, b, (((0,), (0,)), ((), ())), preferred_element_type=F32)


def _cparams(sem):
    return pltpu.CompilerParams(dimension_semantics=sem, vmem_limit_bytes=VMEM_LIMIT)


def _full(a):
    nd = a.ndim
    return pl.BlockSpec(a.shape, lambda *_: (0,) * nd, pipeline_mode=pl.Buffered(1))


def _tok(tm, w):
    return pl.BlockSpec((1, tm, w), lambda b, s: (b, s, 0))


def _perb(rows, w):
    return pl.BlockSpec((1, rows, w), lambda b, s: (b, 0, 0))


def _sds(shape, dtype):
    return jax.ShapeDtypeStruct(shape, dtype)


def _ln_fwd(x, g, b):
    mu = jnp.mean(x, axis=-1, keepdims=True)
    xc = x - mu
    var = jnp.mean(xc * xc, axis=-1, keepdims=True)
    rstd = lax.rsqrt(var + LN_EPS)
    xhat = xc * rstd
    return xhat * g + b, xhat, rstd


def _ln_bwd(dy, xhat, rstd, g):
    dxh = dy * g
    m1 = jnp.mean(dxh, axis=-1, keepdims=True)
    m2 = jnp.mean(dxh * xhat, axis=-1, keepdims=True)
    return rstd * (dxh - m1 - xhat * m2)


def _colsum(a):
    return jnp.sum(a, axis=0, keepdims=True)


def _rope(x, c, s1, s2):
    w = x.shape[-1]
    return x * c + pltpu.roll(x, w - 16, 1) * s1 + pltpu.roll(x, 16, 1) * s2


def _rope_t(x, c, s1, s2):
    w = x.shape[-1]
    return x * c - pltpu.roll(x, w - 16, 1) * s1 - pltpu.roll(x, 16, 1) * s2


def _adamw(w, g, m, v):
    m = ADAM_B1 * m + (1.0 - ADAM_B1) * g
    v = ADAM_B2 * v + (1.0 - ADAM_B2) * (g * g)
    m_hat = m / (1.0 - ADAM_B1 ** ADAM_STEP)
    v_hat = v / (1.0 - ADAM_B2 ** ADAM_STEP)
    delta = -ADAM_LR * (m_hat / (jnp.sqrt(v_hat) + ADAM_EPS) + ADAM_WD * w)
    return delta, m, v


def _my_place():
    return lax.axis_index("x"), lax.axis_index("y"), lax.axis_index("c")


def _chip_peers(mx, my):
    out = []
    for j in (1, 2, 3):
        px = 1 - mx if (j >> 1) else mx
        py = 1 - my if (j & 1) else my
        out.append((px, py, 2 * px + py))
    return out


def _hbm_call(body, name, n_in, out_shape, sems):
    hbm = pl.BlockSpec(memory_space=pl.ANY)
    return pl.pallas_call(
        body, name=name, out_shape=out_shape,
        in_specs=[hbm] * n_in, out_specs=[hbm] * len(out_shape),
        scratch_shapes=[pltpu.SemaphoreType.DMA(s) for s in sems])


def _chip_exchange(xs, name, scatter):
    n = len(xs)

    def body(*refs):
        x_refs, o_refs = refs[:n], refs[n:2 * n]
        ssem, rsem, lsem = refs[2 * n:]
        mx, my, mc = _my_place()
        me = 2 * mx + my
        peers = _chip_peers(mx, my)

        def copy(i, j, src_slot, dst_slot):
            px, py, _ = peers[j]
            return pltpu.make_async_remote_copy(
                src_ref=x_refs[i].at[src_slot] if scatter else x_refs[i], dst_ref=o_refs[i].at[dst_slot],
                send_sem=ssem.at[i, j], recv_sem=rsem.at[i, j], device_id=(px, py, mc), device_id_type=MESH)

        local = [pltpu.make_async_copy(x_refs[i].at[me] if scatter else x_refs[i], o_refs[i].at[me], lsem.at[i])
                 for i in range(n)]
        sends = [copy(i, j, peers[j][2], me) for i in range(n) for j in range(3)]
        for cp in local + sends:
            cp.start()
        for i in range(n):
            for j in range(3):
                copy(i, j, peers[j][2], peers[j][2]).wait_recv()
        for cp in sends:
            cp.wait_send()
        for cp in local:
            cp.wait()

    out_shape = [_sds((4,) + tuple(x.shape[1:] if scatter else x.shape), x.dtype) for x in xs]
    return _hbm_call(body, name, n, out_shape, [(n, 3), (n, 3), (n,)])(*xs)


def _chip_exchange_start(xs, name, scatter, after):
    n = len(xs)
    blks = [tuple(x.shape[1:] if scatter else x.shape) for x in xs]

    def body(*refs):
        x_refs, land_refs = refs[:n], refs[n:2 * n]
        ssem, rsem = refs[2 * n + 1], refs[2 * n + 2]
        token = refs[-1]
        mx, my, mc = _my_place()
        me = 2 * mx + my
        for i in range(n):
            for j, (px, py, pk) in enumerate(_chip_peers(mx, my)):
                pltpu.make_async_remote_copy(
                    src_ref=x_refs[i].at[pk] if scatter else x_refs[i], dst_ref=land_refs[i].at[me],
                    send_sem=ssem.at[3 * i + j], recv_sem=rsem.at[3 * i + j], device_id=(px, py, mc),
                    device_id_type=MESH).start()
        token[...] = jnp.zeros_like(token)

    hbm = pl.BlockSpec(memory_space=pltpu.HBM)
    sem = pl.BlockSpec(memory_space=pltpu.SEMAPHORE)
    lands = [lax.empty((4,) + b, x.dtype) for b, x in zip(blks, xs)]
    res = pl.pallas_call(
        body, name=name,
        out_shape=[pltpu.SemaphoreType.DMA((3 * n,)), pltpu.SemaphoreType.DMA((3 * n,))]
        + [pltpu.HBM(x.shape, x.dtype) for x in xs] + [pltpu.HBM(l.shape, l.dtype) for l in lands]
        + [_sds((8, LANES), F32)],
        in_specs=[hbm] * (2 * n) + [_AFTER],
        out_specs=[sem, sem] + [hbm] * (2 * n) + [pl.BlockSpec(memory_space=pltpu.VMEM)],
        input_output_aliases={i: 2 + i for i in range(2 * n)},
        compiler_params=pltpu.CompilerParams(has_side_effects=pltpu.SideEffectType.DATAFLOW_SIDE_EFFECTING),
    )(*[pltpu.with_memory_space_constraint(a, pltpu.HBM) for a in list(xs) + lands], after)
    return dict(ssem=res[0], rsem=res[1], xs=res[2:2 + n], lands=res[2 + n:2 + 2 * n], n=n, scatter=scatter), res[-1]


def _chip_exchange_wait(handle, after, name):
    n, scatter = handle["n"], handle["scatter"]

    def body(*refs):
        x_refs, land_refs = refs[:n], refs[n:2 * n]
        ssem, rsem = refs[2 * n], refs[2 * n + 1]
        mx, my, mc = _my_place()
        for i in range(n):
            for j, (px, py, pk) in enumerate(_chip_peers(mx, my)):
                cp = pltpu.make_async_remote_copy(
                    src_ref=x_refs[i].at[pk] if scatter else x_refs[i], dst_ref=land_refs[i].at[pk],
                    send_sem=ssem.at[3 * i + j], recv_sem=rsem.at[3 * i + j], device_id=(px, py, mc),
                    device_id_type=MESH)
                cp.wait_send()
                cp.wait_recv()

    hbm = pl.BlockSpec(memory_space=pltpu.HBM)
    sem = pl.BlockSpec(memory_space=pltpu.SEMAPHORE)
    ops = list(handle["xs"]) + list(handle["lands"])
    res = pl.pallas_call(
        body, name=name,
        out_shape=[pltpu.HBM(a.shape, a.dtype) for a in ops],
        in_specs=[hbm] * (2 * n) + [sem, sem, pl.BlockSpec(memory_space=pl.ANY)],
        out_specs=[hbm] * (2 * n),
        input_output_aliases={i: i for i in range(2 * n)},
        compiler_params=pltpu.CompilerParams(has_side_effects=pltpu.SideEffectType.DATAFLOW_SIDE_EFFECTING),
    )(*ops, handle["ssem"], handle["rsem"], after)
    me = 2 * lax.axis_index("x") + lax.axis_index("y")
    out = []
    for x, land in zip(res[:n], res[n:]):
        own = lax.dynamic_index_in_dim(x, me, 0, keepdims=False) if scatter else x
        out.append(lax.dynamic_update_index_in_dim(land, own, me, 0))
    return out


def _core_gather(xs, name):
    n = len(xs)

    def body(*refs):
        x_refs, o_refs, mine, got = refs[:n], refs[n:2 * n], refs[2 * n:3 * n], refs[3 * n:4 * n]
        lsem, ssem, rsem, osem = refs[4 * n:]
        mx, my, mc = _my_place()
        loads = [pltpu.make_async_copy(x_refs[i], mine[i], lsem.at[i]) for i in range(n)]
        for cp in loads:
            cp.start()
        sends, stores = [], []
        for i in range(n):
            loads[i].wait()
            cp = pltpu.make_async_remote_copy(
                src_ref=mine[i], dst_ref=got[i], send_sem=ssem.at[i], recv_sem=rsem.at[i],
                device_id=(mx, my, 1 - mc), device_id_type=MESH)
            cp.start()
            sends.append(cp)
            for k in range(4):
                st = pltpu.make_async_copy(mine[i].at[k], o_refs[i].at[k, mc], osem.at[i, k])
                st.start()
                stores.append(st)
        for i in range(n):
            sends[i].wait_recv()
            for k in range(4):
                st = pltpu.make_async_copy(got[i].at[k], o_refs[i].at[k, 1 - mc], osem.at[n + i, k])
                st.start()
                stores.append(st)
        for cp in sends:
            cp.wait_send()
        for st in stores:
            st.wait()

    hbm = pl.BlockSpec(memory_space=pl.ANY)
    bufs = [pltpu.VMEM(x.shape, x.dtype) for x in xs]
    return pl.pallas_call(
        body, name=name,
        out_shape=[_sds((4, 2) + tuple(x.shape[1:]), x.dtype) for x in xs],
        in_specs=[hbm] * n, out_specs=[hbm] * n,
        scratch_shapes=bufs + bufs + [pltpu.SemaphoreType.DMA((n,)), pltpu.SemaphoreType.DMA((n,)),
                                      pltpu.SemaphoreType.DMA((n,)), pltpu.SemaphoreType.DMA((2 * n, 4))],
        compiler_params=pltpu.CompilerParams(vmem_limit_bytes=VMEM_LIMIT),
    )(*xs)


def _rows_step(k):
    for r in (256, 128, 64, 32, 16, 8):
        if k % r == 0:
            return r
    return k


def _core_scatter_sum(gs, name):
    n = len(gs)

    def body(*refs):
        g_refs, o_refs = refs[:n], refs[n:2 * n]
        send, got, mine = refs[2 * n:3 * n], refs[3 * n:4 * n], refs[4 * n:5 * n]
        lsem, msem, ssem, rsem, osem = refs[5 * n:]
        mx, my, mc = _my_place()
        pairs = [(i, k) for i in range(n) for k in range(4)]
        out_loads = {(i, k): pltpu.make_async_copy(g_refs[i].at[k, 1 - mc], send[i].at[k], lsem.at[i, k])
                     for i, k in pairs}
        own_loads = {(i, k): pltpu.make_async_copy(g_refs[i].at[k, mc], mine[i].at[k], msem.at[i, k])
                     for i, k in pairs}
        for p in pairs:
            out_loads[p].start()
        for p in pairs:
            own_loads[p].start()
        sends = []
        for i in range(n):
            for k in range(4):
                out_loads[i, k].wait()
            cp = pltpu.make_async_remote_copy(
                src_ref=send[i], dst_ref=got[i], send_sem=ssem.at[i], recv_sem=rsem.at[i],
                device_id=(mx, my, 1 - mc), device_id_type=MESH)
            cp.start()
            sends.append(cp)
        stores = []
        for i in range(n):
            for k in range(4):
                own_loads[i, k].wait()
            sends[i].wait_recv()
            rows = g_refs[i].shape[2]
            step = _rows_step(rows)

            def add(r, _, i=i, step=step):
                sl = pl.ds(pl.multiple_of(r * step, step), step)
                for k in range(4):
                    mine[i][k, sl, :] = (mine[i][k, sl, :].astype(F32) + got[i][k, sl, :].astype(F32)).astype(BF16)
                return 0

            lax.fori_loop(0, rows // step, add, 0)
            st = pltpu.make_async_copy(mine[i], o_refs[i], osem.at[i])
            st.start()
            stores.append(st)
        for cp in sends:
            cp.wait_send()
        for st in stores:
            st.wait()

    hbm = pl.BlockSpec(memory_space=pl.ANY)
    blk = [(4,) + tuple(g.shape[2:]) for g in gs]
    bufs = [pltpu.VMEM(b, BF16) for b in blk]
    return pl.pallas_call(
        body, name=name,
        out_shape=[_sds(b, BF16) for b in blk],
        in_specs=[hbm] * n, out_specs=[hbm] * n,
        scratch_shapes=bufs * 3 + [pltpu.SemaphoreType.DMA((n, 4)), pltpu.SemaphoreType.DMA((n, 4)),
                                   pltpu.SemaphoreType.DMA((n,)), pltpu.SemaphoreType.DMA((n,)),
                                   pltpu.SemaphoreType.DMA((n,))],
        compiler_params=pltpu.CompilerParams(vmem_limit_bytes=VMEM_LIMIT),
    )(*gs)


def _all_gather(xs, name):
    by_chip = _chip_exchange(xs, name + "_chips", scatter=False)
    both = _core_gather(by_chip, name + "_cores")
    return [b.reshape((N_DEV,) + tuple(x.shape)) for b, x in zip(both, xs)]


def _ada_partial(c_all, w_ada_loc, b_loc):
    def body(c_ref, w_ref, b_ref, act_ref, mod_ref):
        c = c_ref[...]
        act = c * (1.0 / (1.0 + jnp.exp(-c)))
        act_ref[...] = act
        mod_ref[...] = _dot(act.astype(BF16), w_ref[...].astype(BF16)) + b_ref[...]

    nb, d = c_all.shape
    return pl.pallas_call(
        body, name="ada_partial",
        out_shape=(_sds((nb, d), F32), _sds((nb, w_ada_loc.shape[1]), F32)),
        compiler_params=pltpu.CompilerParams(vmem_limit_bytes=VMEM_LIMIT),
    )(c_all, w_ada_loc, b_loc)


_AFTER = pl.BlockSpec(memory_space=pl.ANY)


def _inproj_fwd(x, mod, ln_g, ln_b, w_in_p, w_uq_p, w_kv, gq, gkv, tc, ts1, ts2, dm, after):
    B, S, D = x.shape
    tm = dm["tm"]
    sbw, qr, kvr, nh = dm["sbw"], dm["qr"], dm["kvr"], dm["nh"]
    o_cq, o_ckv, o_kr = 3 * sbw, 3 * sbw + qr, 3 * sbw + qr + kvr
    qpw = nh * HEAD_PAD

    def body(x_ref, mod_ref, g_ref, b_ref, win_ref, wuq_ref, wkv_ref, gq_ref, gkv_ref, tc_ref, ts1_ref, ts2_ref, _,
             x0_ref, h_ref, q_ref, k_ref, v_ref, qp_ref, kp_ref, mv_ref, cq_ref, ckv_ref, qn_ref, kvn_ref):
        x0, _, _ = _ln_fwd(x_ref[0], g_ref[...], b_ref[...])
        x0_ref[0] = x0
        mod = mod_ref[0]
        h = (x0 * (1.0 + mod[1:2]) + mod[0:1]).astype(BF16)
        h_ref[0] = h
        proj = _dot(h, win_ref[...])
        q_ref[0] = (proj[:, 0:sbw] * (SB_HD ** -0.5)).astype(BF16)
        k_ref[0] = proj[:, sbw:2 * sbw].astype(BF16)
        v_ref[0] = proj[:, 2 * sbw:3 * sbw].astype(BF16)
        cq = proj[:, o_cq:o_cq + qr]
        ckv = proj[:, o_ckv:o_ckv + kvr]
        cq_ref[0] = cq
        ckv_ref[0] = ckv
        qn = (cq * lax.rsqrt(jnp.mean(cq * cq, axis=-1, keepdims=True) + RMS_EPS) * gq_ref[...]).astype(BF16)
        kvn = (ckv * lax.rsqrt(jnp.mean(ckv * ckv, axis=-1, keepdims=True) + RMS_EPS) * gkv_ref[...]).astype(BF16)
        qn_ref[0] = qn
        kvn_ref[0] = kvn
        c1, s1, s2 = tc_ref[...], ts1_ref[...], ts2_ref[...]
        c8, s18, s28 = jnp.tile(c1, (1, nh)), jnp.tile(s1, (1, nh)), jnp.tile(s2, (1, nh))
        qp_ref[0] = _rope(_dot(qn, wuq_ref[...]), c8, s18, s28).astype(BF16)
        kvo = _dot(kvn, wkv_ref[...])
        kr = pltpu.roll(proj[:, o_kr:o_kr + LANES], 64, 1)
        kr = _rope(kr, c1, s1, s2)
        kp_ref[0] = (kvo[:, 0:qpw] + jnp.tile(kr, (1, nh))).astype(BF16)
        mv_ref[0] = kvo[:, qpw:].astype(BF16)

    tab = pl.BlockSpec((tm, LANES), lambda b, s: (s, 0))
    outs = [(D, F32), (D, BF16), (sbw, BF16), (sbw, BF16), (sbw, BF16), (qpw, BF16), (qpw, BF16),
            (nh * MLA_V, BF16), (qr, F32), (kvr, F32), (qr, BF16), (kvr, BF16)]
    return pl.pallas_call(
        body, name="inproj_fwd", grid=(B, S // tm),
        in_specs=[_tok(tm, D), _perb(N_MOD, D), _full(ln_g), _full(ln_b), _full(w_in_p), _full(w_uq_p),
                  _full(w_kv), _full(gq), _full(gkv), tab, tab, tab, _AFTER],
        out_specs=[_tok(tm, w) for w, _ in outs],
        out_shape=[_sds((B, S, w), t) for w, t in outs],
        compiler_params=_cparams(("parallel", "parallel")),
    )(x, mod, ln_g, ln_b, w_in_p, w_uq_p, w_kv, gq, gkv, tc, ts1, ts2, after)


def _neg_abs(x):
    sign = jnp.uint32(0x80000000)
    return lax.bitcast_convert_type(lax.bitcast_convert_type(x, jnp.uint32) | sign, F32)


def _log2_keep(z):
    zs = z * (-LOG2E)
    return jnp.minimum(zs, 0.0) - jnp.log2(1.0 + jnp.exp2(_neg_abs(zs))), zs


def _split_dot(a, u):
    hi = a.astype(BF16)
    lo = (a - hi.astype(F32)).astype(BF16)
    return _dot(hi, u) + _dot(lo, u)


def _tri(n, rel):
    row = lax.broadcasted_iota(jnp.int32, (n, n), 0)
    col = lax.broadcasted_iota(jnp.int32, (n, n), 1)
    return rel(row, col).astype(BF16)


def _running_sum(a, tri, reverse, split):
    cs = tri.shape[0]
    n = a.shape[1] // cs
    out = [None] * n
    run = None
    for c in (reversed(range(n)) if reverse else range(n)):
        part = a[:, c * cs:(c + 1) * cs]
        loc = _split_dot(part, tri) if split else _dot(part.astype(BF16), tri)
        out[c] = loc if run is None else loc + run
        tot = jnp.sum(part, axis=1, keepdims=True)
        run = tot if run is None else run + tot
    return (out[0] if n == 1 else jnp.concatenate(out, axis=1)), run


def _tile_mask(nr, nk, r0, c0, rel):
    row = lax.broadcasted_iota(jnp.int32, (nr, nk), 0) + r0
    col = lax.broadcasted_iota(jnp.int32, (nr, nk), 1) + c0
    return rel(row, col)


def _put_rows(whole, part, r0):
    return part if r0 == 0 else jnp.concatenate([whole[:r0], part], axis=0)


def _diag_tiles(tq, split):
    half = tq // 2
    return [(0, tq, 0, half), (half, half, half, half)] if split else [(0, tq, 0, tq)]


def _sb_fwd(q, k, v, dm):
    B, S, W = q.shape
    tq = dm["tq"]
    nq = S // tq

    def body(q_ref, k_ref, v_ref, y_ref, tot_ref):
        qi = pl.program_id(2)
        q2 = q_ref[0]
        lane = lax.broadcasted_iota(jnp.int32, (tq, LANES), 1)
        qs = jnp.concatenate([jnp.where(lane < SB_HD, q2, 0), jnp.where(lane >= SB_HD, q2, 0)], axis=0).astype(BF16)
        later = _tri(min(tq, CUMSUM_W), lambda a, b: a > b)
        assert tq & (tq - 1) == 0
        strict = _tile_mask(2 * tq, tq, 0, 0, lambda t, s: s < (t & (tq - 1)))

        def block(j, carry, masked):
            acc, run = carry
            off = pl.multiple_of(j * tq, tq)
            a, zs = _log2_keep(_dot_nt(qs, k_ref[0, pl.ds(off, tq), :]))
            if masked:
                a = jnp.where(strict, a, 0.0)
            a_later, a_tot = _running_sum(a, later, reverse=True, split=True)
            w = jnp.exp2((a - zs) + a_later + run)
            if masked:
                w = jnp.where(strict, w, 0.0)
            return acc + _dot(w.astype(BF16), v_ref[0, pl.ds(off, tq), :]), run + a_tot

        carry = block(qi, (jnp.zeros((2 * tq, LANES), F32), jnp.zeros((2 * tq, 1), F32)), True)
        acc, run = lax.fori_loop(0, qi, lambda jj, c: block(qi - 1 - jj, c, False), carry)
        y_ref[0] = jnp.where(lane < SB_HD, acc[:tq], acc[tq:]).astype(BF16)
        tot_ref[0] = jnp.where(lane < SB_HD, run[:tq], run[tq:])

    qspec = pl.BlockSpec((1, tq, LANES), lambda b, hp, i: (b, i, hp))
    kspec = pl.BlockSpec((1, S, LANES), lambda b, hp, i: (b, 0, hp))
    return pl.pallas_call(
        body, name="sb_fwd", grid=(B, W // LANES, nq),
        in_specs=[qspec, kspec, kspec],
        out_specs=[qspec, qspec],
        out_shape=[_sds((B, S, W), BF16), _sds((B, S, W), F32)],
        compiler_params=_cparams(("parallel", "parallel", "arbitrary")),
    )(q, k, v)


def _sb_bwd(q, k, v, tot, dy, dm, after):
    B, S, W = q.shape
    tq = dm["tq"]
    nq = S // tq

    def body(q_ref, k_ref, v_ref, tot_ref, dy_ref, _, dq_ref, dk_ref, dv_ref, dk_acc, dv_acc):
        qi = pl.program_id(2)

        @pl.when(qi == 0)
        def _():
            dk_acc[...] = jnp.zeros_like(dk_acc)
            dv_acc[...] = jnp.zeros_like(dv_acc)

        q2 = q_ref[0]
        dy2 = dy_ref[0]
        tot2 = tot_ref[0]
        lane = lax.broadcasted_iota(jnp.int32, (tq, LANES), 1)
        in_h = [lane < SB_HD, lane >= SB_HD]
        qh = [jnp.where(m, q2, 0).astype(BF16) for m in in_h]
        dyh = [jnp.where(m, dy2, 0).astype(BF16) for m in in_h]
        toth = [tot2[:, 0:1], tot2[:, SB_HD:SB_HD + 1]]

        def tile(j, carry, r0, nr, c0, nk, masked):
            off = pl.multiple_of(j * tq + c0, math.gcd(tq, c0))
            k2 = k_ref[0, pl.ds(off, nk), :]
            v2 = v_ref[0, pl.ds(off, nk), :]
            upto = _tri(min(nk, CUMSUM_W), lambda a, b: a <= b)
            before = _tri(min(nk, CUMSUM_W), lambda a, b: a < b)
            strict = _tile_mask(nr, nk, r0, c0, lambda t, s: s < t) if masked else None
            rows = slice(r0, r0 + nr)
            new = []
            dk_blk = jnp.zeros((nk, LANES), F32)
            dv_blk = jnp.zeros((nk, LANES), F32)
            for h in range(2):
                dq, pa, pg = carry[3 * h][rows], carry[3 * h + 1][rows], carry[3 * h + 2][rows]
                a, zs = _log2_keep(_dot_nt(qh[h][rows], k2))
                if masked:
                    a = jnp.where(strict, a, 0.0)
                a_upto, a_tot = _running_sum(a, upto, reverse=False, split=True)
                w = jnp.exp2((a - zs) + ((toth[h][rows] - pa) - a_upto))
                if masked:
                    w = jnp.where(strict, w, 0.0)
                g = _dot_nt(dyh[h][rows], v2) * w
                g_before, g_tot = _running_sum(g, before, reverse=False, split=False)
                g_before = g_before + pg
                dz = (g + g_before) * jnp.exp2(a) - g_before
                if masked:
                    dz = jnp.where(strict, dz, 0.0)
                dzb = dz.astype(BF16)
                dv_blk = dv_blk + _dot_tn(w.astype(BF16), dyh[h][rows])
                dk_blk = dk_blk + _dot_tn(dzb, qh[h][rows])
                new += [_put_rows(carry[3 * h], dq + _dot(dzb, k2), r0), _put_rows(carry[3 * h + 1], pa + a_tot, r0),
                        _put_rows(carry[3 * h + 2], pg + g_tot, r0)]
            dk_acc[pl.ds(off, nk), :] += dk_blk
            dv_acc[pl.ds(off, nk), :] += dv_blk
            return tuple(new)

        zero = jnp.zeros((tq, LANES), F32)
        zrun = jnp.zeros((tq, 1), F32)
        carry = lax.fori_loop(0, qi, lambda j, c: tile(j, c, 0, tq, 0, tq, False),
                              (zero, zrun, zrun, zero, zrun, zrun))
        for r0, nr, c0, nk in _diag_tiles(tq, False):
            carry = tile(qi, carry, r0, nr, c0, nk, True)
        dq_ref[0] = (jnp.where(in_h[0], carry[0], carry[3]) * (SB_HD ** -0.5)).astype(BF16)

        @pl.when(qi == nq - 1)
        def _():
            dk_ref[0] = dk_acc[...].astype(BF16)
            dv_ref[0] = dv_acc[...].astype(BF16)

    qspec = pl.BlockSpec((1, tq, LANES), lambda b, hp, i: (b, i, hp))
    kspec = pl.BlockSpec((1, S, LANES), lambda b, hp, i: (b, 0, hp))
    return pl.pallas_call(
        body, name="sb_bwd", grid=(B, W // LANES, nq),
        in_specs=[qspec, kspec, kspec, qspec, qspec, _AFTER],
        out_specs=[qspec, kspec, kspec],
        out_shape=[_sds((B, S, W), BF16)] * 3,
        scratch_shapes=[pltpu.VMEM((S, LANES), F32), pltpu.VMEM((S, LANES), F32)],
        compiler_params=_cparams(("parallel", "parallel", "arbitrary")),
    )(q, k, v, tot, dy, after)


def _same_or_earlier_chunk(row, col):
    return lax.shift_right_logical(col, 6) <= lax.shift_right_logical(row, 6)


def _mla_fwd(qp, kp, mv, dm, after):
    B, S, QW = qp.shape
    VW = mv.shape[2]
    tq = dm["tq"]
    nq = S // tq
    scale = (MLA_NOPE + MLA_ROPE) ** -0.5
    assert CHUNK == 64

    def body(q_ref, k_ref, v_ref, _, y_ref, lse_ref):
        qi = pl.program_id(2)
        q2 = q_ref[0]
        lane = lax.broadcasted_iota(jnp.int32, (tq, LANES), 1)

        def tile(j, carry, r0, nr, c0, nk, masked):
            off = pl.multiple_of(j * tq + c0, math.gcd(tq, c0))
            v2 = v_ref[0, pl.ds(off, nk), :]
            allowed = _tile_mask(nr, nk, r0, c0, _same_or_earlier_chunk) if masked else None
            rows = slice(r0, r0 + nr)
            heads = range(2)
            sl = [slice(h * HEAD_PAD, (h + 1) * HEAD_PAD) for h in heads]
            m_old = [carry[3 * h + 1][rows] for h in heads]
            s = [_dot_nt(q2[rows, sl[h]], k_ref[0, pl.ds(off, nk), sl[h]]) * (scale * LOG2E) for h in heads]
            if masked:
                s = [jnp.where(allowed, s[h], -1e30) for h in heads]
            m_new = [jnp.maximum(m_old[h], jnp.max(s[h], axis=1, keepdims=True)) for h in heads]
            alpha = [jnp.exp2(m_old[h] - m_new[h]) for h in heads]
            p = [jnp.exp2(s[h] - m_new[h]) for h in heads]
            acc = [alpha[h] * carry[3 * h][rows] + _dot(p[h].astype(BF16), v2) for h in heads]
            l = [alpha[h] * carry[3 * h + 2][rows] + jnp.sum(p[h], axis=1, keepdims=True) for h in heads]
            out = []
            for h in heads:
                out += [_put_rows(carry[3 * h], acc[h], r0), _put_rows(carry[3 * h + 1], m_new[h], r0),
                        _put_rows(carry[3 * h + 2], l[h], r0)]
            return tuple(out)

        zero = jnp.zeros((tq, LANES), F32)
        m0 = jnp.full((tq, 1), -1e30, F32)
        l0 = jnp.zeros((tq, 1), F32)
        carry = (zero, m0, l0, zero, m0, l0)
        for r0, nr, c0, nk in _diag_tiles(tq, False):
            carry = tile(qi, carry, r0, nr, c0, nk, True)
        carry = lax.fori_loop(0, qi, lambda j, c: tile(j, c, 0, tq, 0, tq, False), carry)
        y0 = carry[0] / carry[2]
        y1 = carry[3] / carry[5]
        y_ref[0] = jnp.where(lane < MLA_V, y0, y1).astype(BF16)
        lse_ref[0] = jnp.where(lane < MLA_V, carry[1] + jnp.log2(carry[2]), carry[4] + jnp.log2(carry[5]))

    qspec = pl.BlockSpec((1, tq, 2 * HEAD_PAD), lambda b, hp, i: (b, i, hp))
    kspec = pl.BlockSpec((1, S, 2 * HEAD_PAD), lambda b, hp, i: (b, 0, hp))
    vspec = pl.BlockSpec((1, S, LANES), lambda b, hp, i: (b, 0, hp))
    yspec = pl.BlockSpec((1, tq, LANES), lambda b, hp, i: (b, i, hp))
    return pl.pallas_call(
        body, name="mla_fwd", grid=(B, VW // LANES, nq),
        in_specs=[qspec, kspec, vspec, _AFTER],
        out_specs=[yspec, yspec],
        out_shape=[_sds((B, S, VW), BF16), _sds((B, S, VW), F32)],
        compiler_params=_cparams(("parallel", "parallel", "arbitrary")),
    )(qp, kp, mv, after)


def _mla_bwd(qp, kp, mv, y, lse, dy, dm, after):
    B, S, QW = qp.shape
    VW = mv.shape[2]
    tq = dm["tq"]
    nq = S // tq
    scale = (MLA_NOPE + MLA_ROPE) ** -0.5

    def body(q_ref, k_ref, v_ref, y_ref, lse_ref, dy_ref, _, dq_ref, dk_ref, dv_ref, dk_acc, dv_acc):
        qi = pl.program_id(2)

        @pl.when(qi == 0)
        def _():
            dk_acc[...] = jnp.zeros_like(dk_acc)
            dv_acc[...] = jnp.zeros_like(dv_acc)

        q2 = q_ref[0]
        dy2 = dy_ref[0]
        lse2 = lse_ref[0]
        lane = lax.broadcasted_iota(jnp.int32, (tq, LANES), 1)
        in_h = [lane < MLA_V, lane >= MLA_V]
        prod = dy2.astype(F32) * y_ref[0].astype(F32)
        delta = [jnp.sum(jnp.where(m, prod, 0.0), axis=1, keepdims=True) for m in in_h]
        dyh = [jnp.where(m, dy2, 0).astype(BF16) for m in in_h]
        lseh = [lse2[:, 0:1], lse2[:, MLA_V:MLA_V + 1]]

        def tile(j, carry, r0, nr, c0, nk, masked):
            off = pl.multiple_of(j * tq + c0, math.gcd(tq, c0))
            v2 = v_ref[0, pl.ds(off, nk), :]
            allowed = _tile_mask(nr, nk, r0, c0, _same_or_earlier_chunk) if masked else None
            rows = slice(r0, r0 + nr)
            heads = range(2)
            sl = [slice(h * HEAD_PAD, (h + 1) * HEAD_PAD) for h in heads]
            qhh = [q2[rows, sl[h]] for h in heads]
            dyr = [dyh[h][rows] for h in heads]
            kh = [k_ref[0, pl.ds(off, nk), sl[h]] for h in heads]
            s = [_dot_nt(qhh[h], kh[h]) * (scale * LOG2E) for h in heads]
            dp = [_dot_nt(dyr[h], v2) for h in heads]
            if masked:
                s = [jnp.where(allowed, s[h], -1e30) for h in heads]
            p = [jnp.exp2(s[h] - lseh[h][rows]) for h in heads]
            dv_acc[pl.ds(off, nk), :] += _dot_tn(p[0].astype(BF16), dyr[0]) + _dot_tn(p[1].astype(BF16), dyr[1])
            ds = [(p[h] * (dp[h] - delta[h][rows]) * scale).astype(BF16) for h in heads]
            for h in heads:
                dk_acc[pl.ds(off, nk), sl[h]] += _dot_tn(ds[h], qhh[h])
            return tuple(_put_rows(carry[h], carry[h][rows] + _dot(ds[h], kh[h]), r0) for h in heads)

        zero = jnp.zeros((tq, HEAD_PAD), F32)
        carry = lax.fori_loop(0, qi, lambda j, c: tile(j, c, 0, tq, 0, tq, False), (zero, zero))
        for r0, nr, c0, nk in _diag_tiles(tq, True):
            carry = tile(qi, carry, r0, nr, c0, nk, True)
        dq_ref[0] = jnp.concatenate([carry[0], carry[1]], axis=1).astype(BF16)

        @pl.when(qi == nq - 1)
        def _():
            dk_ref[0] = dk_acc[...].astype(BF16)
            dv_ref[0] = dv_acc[...].astype(BF16)

    qspec = pl.BlockSpec((1, tq, 2 * HEAD_PAD), lambda b, hp, i: (b, i, hp))
    kspec = pl.BlockSpec((1, S, 2 * HEAD_PAD), lambda b, hp, i: (b, 0, hp))
    vspec = pl.BlockSpec((1, S, LANES), lambda b, hp, i: (b, 0, hp))
    yspec = pl.BlockSpec((1, tq, LANES), lambda b, hp, i: (b, i, hp))
    return pl.pallas_call(
        body, name="mla_bwd", grid=(B, VW // LANES, nq),
        in_specs=[qspec, kspec, vspec, yspec, yspec, yspec, _AFTER],
        out_specs=[qspec, kspec, vspec],
        out_shape=[_sds((B, S, QW), BF16), _sds((B, S, QW), BF16), _sds((B, S, VW), BF16)],
        scratch_shapes=[pltpu.VMEM((S, 2 * HEAD_PAD), F32), pltpu.VMEM((S, LANES), F32)],
        compiler_params=_cparams(("parallel", "parallel", "arbitrary")),
    )(qp, kp, mv, y, lse, dy, after)


def _outproj_fwd(sb_y, mla_y, x0, mod, w_o, ln_g, ln_b, dm):
    B, S, D = x0.shape
    tm = dm["tm"]
    sbw = sb_y.shape[2]

    def body(ya_ref, yb_ref, x0_ref, mod_ref, wo_ref, g_ref, b_ref, mix_ref, x1_ref, h2_ref):
        mod = mod_ref[0]
        mix = _dot(ya_ref[0], wo_ref[0:sbw, :]) + _dot(yb_ref[0], wo_ref[sbw:, :])
        mix_ref[0] = mix
        x1, _, _ = _ln_fwd(ALPHA * x0_ref[0] + (1.0 + mod[2:3]) * mix, g_ref[...], b_ref[...])
        x1_ref[0] = x1
        h2_ref[0] = (x1 * (1.0 + mod[4:5]) + mod[3:4]).astype(BF16)

    return pl.pallas_call(
        body, name="outproj_fwd", grid=(B, S // tm),
        in_specs=[_tok(tm, sbw), _tok(tm, mla_y.shape[2]), _tok(tm, D), _perb(N_MOD, D),
                  _full(w_o), _full(ln_g), _full(ln_b)],
        out_specs=[_tok(tm, D)] * 3,
        out_shape=[_sds((B, S, D), F32), _sds((B, S, D), F32), _sds((B, S, D), BF16)],
        compiler_params=_cparams(("parallel", "parallel")),
    )(sb_y, mla_y, x0, mod, w_o, ln_g, ln_b)


def _stat_specs(B, D):
    specs = [pl.BlockSpec((1, 8, D), lambda b, s: (b, 0, 0)), pl.BlockSpec((8, D), lambda b, s: (0, 0))]
    shapes = [_sds((B, 8, D), F32), _sds((8, D), F32)]
    return specs, shapes


def _stat_init(bst_ref, wst_ref):
    @pl.when(pl.program_id(1) == 0)
    def _():
        bst_ref[...] = jnp.zeros_like(bst_ref)

    @pl.when((pl.program_id(0) == 0) & (pl.program_id(1) == 0))
    def _():
        wst_ref[...] = jnp.zeros_like(wst_ref)


def _mlp_fwd(h2, x1, mod, target, w_up, w_down, ln_g, ln_b, dm):
    B, S, D = x1.shape
    tm = dm["tm"]
    nck, _, ck = w_up.shape
    dff = nck * ck

    def body(h2_ref, x1_ref, mod_ref, t_ref, wu_ref, wd_ref, g_ref, b_ref, u_ref, dr_ref, bst_ref, wst_ref):
        _stat_init(bst_ref, wst_ref)
        mod = mod_ref[0]
        h2 = h2_ref[0]
        ff = jnp.zeros((tm, D), F32)
        for c in range(nck):
            u = _dot(h2, wu_ref[c])
            u_ref[0, :, c * ck:(c + 1) * ck] = u.astype(BF16)
            act = jnp.square(jnp.maximum(u, 0.0)).astype(BF16)
            ff = ff + _dot(act, wd_ref[c])
        g = g_ref[...]
        x2, xhat, rstd = _ln_fwd(ALPHA * x1_ref[0] + (1.0 + mod[5:6]) * ff, g, b_ref[...])
        err = x2 - t_ref[0]
        dy = err * (1.0 / D)
        dr = _ln_bwd(dy, xhat, rstd, g)
        dr_ref[0] = dr
        bst_ref[0, 0:1, :] += _colsum(dr * ff)
        wst_ref[0:1, :] += _colsum(dy * xhat)
        wst_ref[1:2, :] += _colsum(dy)
        wst_ref[2:3, :] += _colsum(err * err) * (0.5 / D)

    sspecs, sshapes = _stat_specs(B, D)
    return pl.pallas_call(
        body, name="mlp_fwd", grid=(B, S // tm),
        in_specs=[_tok(tm, D), _tok(tm, D), _perb(N_MOD, D), _tok(tm, D), _full(w_up), _full(w_down),
                  _full(ln_g), _full(ln_b)],
        out_specs=[_tok(tm, dff), _tok(tm, D)] + sspecs,
        out_shape=[_sds((B, S, dff), BF16), _sds((B, S, D), F32)] + sshapes,
        compiler_params=_cparams(("arbitrary", "arbitrary")),
    )(h2, x1, mod, target, w_up, w_down, ln_g, ln_b)


def _mlp_bwd(dr2, u, x1, x0, mix, mod, w_up, w_down, w_o, ln_g, dm):
    B, S, D = x1.shape
    tm = dm["tm_small"]
    sbw = dm["sbw"]
    nck, _, ck = w_up.shape
    dff = nck * ck

    def body(dr_ref, u_ref, x1_ref, x0_ref, mix_ref, mod_ref, wu_ref, wd_ref, wo_ref, g_ref,
             du_ref, dff_ref, dmix_ref, dx0_ref, dya_ref, dyb_ref, bst_ref, wst_ref):
        _stat_init(bst_ref, wst_ref)
        mod = mod_ref[0]
        dr2 = dr_ref[0]
        dffv = ((1.0 + mod[5:6]) * dr2).astype(BF16)
        dff_ref[0] = dffv
        dh2 = jnp.zeros((tm, D), F32)
        for c in range(nck):
            sl = slice(c * ck, (c + 1) * ck)
            da = _dot_nt(dffv, wd_ref[c])
            du = (da * (2.0 * jnp.maximum(u_ref[0, :, sl].astype(F32), 0.0))).astype(BF16)
            du_ref[0, :, sl] = du
            dh2 = dh2 + _dot_nt(du, wu_ref[c])
        x1 = x1_ref[0]
        dx1 = ALPHA * dr2 + dh2 * (1.0 + mod[4:5])
        bst_ref[0, 0:1, :] += _colsum(dh2 * x1)
        bst_ref[0, 1:2, :] += _colsum(dh2)
        mix = mix_ref[0]
        g = g_ref[...]
        _, xhat, rstd = _ln_fwd(ALPHA * x0_ref[0] + (1.0 + mod[2:3]) * mix, g, 0.0)
        dr1 = _ln_bwd(dx1, xhat, rstd, g)
        wst_ref[0:1, :] += _colsum(dx1 * xhat)
        wst_ref[1:2, :] += _colsum(dx1)
        bst_ref[0, 2:3, :] += _colsum(dr1 * mix)
        dx0_ref[0] = ALPHA * dr1
        dmix = ((1.0 + mod[2:3]) * dr1).astype(BF16)
        dmix_ref[0] = dmix
        dya_ref[0] = _dot_nt(dmix, wo_ref[0:sbw, :]).astype(BF16)
        dyb_ref[0] = _dot_nt(dmix, wo_ref[sbw:, :]).astype(BF16)

    sspecs, sshapes = _stat_specs(B, D)
    wa, wb = sbw, w_o.shape[0] - sbw
    return pl.pallas_call(
        body, name="mlp_bwd", grid=(B, S // tm),
        in_specs=[_tok(tm, D), _tok(tm, dff), _tok(tm, D), _tok(tm, D), _tok(tm, D), _perb(N_MOD, D),
                  _full(w_up), _full(w_down), _full(w_o), _full(ln_g)],
        out_specs=[_tok(tm, dff), _tok(tm, D), _tok(tm, D), _tok(tm, D), _tok(tm, wa), _tok(tm, wb)] + sspecs,
        out_shape=[_sds((B, S, dff), BF16), _sds((B, S, D), BF16), _sds((B, S, D), BF16), _sds((B, S, D), F32),
                   _sds((B, S, wa), BF16), _sds((B, S, wb), BF16)] + sshapes,
        compiler_params=_cparams(("arbitrary", "arbitrary")),
    )(dr2, u, x1, x0, mix, mod, w_up, w_down, w_o, ln_g)


def _inproj_bwd(x, x0, dx0a, mod, ln_g, dq, dk, dv, dqp, dkp, dmv, cq, ckv, w_in_p, w_uq_p, w_kv, gq, gkv,
                tc, ts1, ts2, dm):
    B, S, D = x.shape
    tm = dm["tm"]
    sbw, qr, kvr, nh = dm["sbw"], dm["qr"], dm["kvr"], dm["nh"]
    qpw = nh * HEAD_PAD
    dinp = w_in_p.shape[1]
    kvw = w_kv.shape[1]

    def body(x_ref, x0_ref, dx0a_ref, mod_ref, g_ref, dq_ref, dk_ref, dv_ref, dqp_ref, dkp_ref, dmv_ref,
             cq_ref, ckv_ref, win_ref, wuq_ref, wkv_ref, gq_ref, gkv_ref, tc_ref, ts1_ref, ts2_ref,
             gx_ref, dproj_ref, dqpre_ref, dkvo_ref, bst_ref, wst_ref):
        _stat_init(bst_ref, wst_ref)
        mod = mod_ref[0]
        c1, s1, s2 = tc_ref[...], ts1_ref[...], ts2_ref[...]
        c8, s18, s28 = jnp.tile(c1, (1, nh)), jnp.tile(s1, (1, nh)), jnp.tile(s2, (1, nh))
        dqpre = _rope_t(dqp_ref[0].astype(F32), c8, s18, s28).astype(BF16)
        dqpre_ref[0] = dqpre
        gq = gq_ref[...]
        cq = cq_ref[0]
        rq = lax.rsqrt(jnp.mean(cq * cq, axis=-1, keepdims=True) + RMS_EPS)
        dqn = _dot_nt(dqpre, wuq_ref[...])
        wst_ref[4:5, 0:qr] += _colsum(dqn * cq * rq)
        dqg = dqn * gq
        dcq = rq * dqg - cq * (rq * rq * rq) * jnp.mean(dqg * cq, axis=-1, keepdims=True)

        dkpre = _rope_t(dkp_ref[0].astype(F32), c8, s18, s28)
        dkr = dkpre[:, 0:HEAD_PAD]
        for h in range(1, nh):
            dkr = dkr + dkpre[:, h * HEAD_PAD:(h + 1) * HEAD_PAD]
        lane = lax.broadcasted_iota(jnp.int32, (tm, LANES), 1)
        dkr = jnp.where((lane >= MLA_NOPE) & (lane < MLA_NOPE + MLA_ROPE), dkr, 0.0)
        dkr = pltpu.roll(dkr, LANES - MLA_NOPE, 1)
        dkvo = jnp.concatenate([dkpre.astype(BF16), dmv_ref[0]], axis=1)
        dkvo_ref[0] = dkvo
        gkv = gkv_ref[...]
        ckv = ckv_ref[0]
        rkv = lax.rsqrt(jnp.mean(ckv * ckv, axis=-1, keepdims=True) + RMS_EPS)
        dkvn = _dot_nt(dkvo, wkv_ref[...])
        wst_ref[5:6, 0:kvr] += _colsum(dkvn * ckv * rkv)
        dkg = dkvn * gkv
        dckv = rkv * dkg - ckv * (rkv * rkv * rkv) * jnp.mean(dkg * ckv, axis=-1, keepdims=True)

        dproj = jnp.concatenate([dq_ref[0], dk_ref[0], dv_ref[0], dcq.astype(BF16), dckv.astype(BF16),
                                 dkr.astype(BF16)], axis=1)
        dproj_ref[0] = dproj
        dh = _dot_nt(dproj, win_ref[...])
        x0 = x0_ref[0]
        dx0 = dx0a_ref[0] + dh * (1.0 + mod[1:2])
        bst_ref[0, 0:1, :] += _colsum(dh * x0)
        bst_ref[0, 1:2, :] += _colsum(dh)
        g = g_ref[...]
        _, xhat, rstd = _ln_fwd(x_ref[0], g, 0.0)
        gx_ref[0] = _ln_bwd(dx0, xhat, rstd, g)
        wst_ref[0:1, :] += _colsum(dx0 * xhat)
        wst_ref[1:2, :] += _colsum(dx0)

    tab = pl.BlockSpec((tm, LANES), lambda b, s: (s, 0))
    sspecs, sshapes = _stat_specs(B, D)
    return pl.pallas_call(
        body, name="inproj_bwd", grid=(B, S // tm),
        in_specs=[_tok(tm, D), _tok(tm, D), _tok(tm, D), _perb(N_MOD, D), _full(ln_g),
                  _tok(tm, sbw), _tok(tm, sbw), _tok(tm, sbw), _tok(tm, qpw), _tok(tm, qpw), _tok(tm, nh * MLA_V),
                  _tok(tm, qr), _tok(tm, kvr), _full(w_in_p), _full(w_uq_p), _full(w_kv), _full(gq), _full(gkv),
                  tab, tab, tab],
        out_specs=[_tok(tm, D), _tok(tm, dinp), _tok(tm, qpw), _tok(tm, kvw)] + sspecs,
        out_shape=[_sds((B, S, D), F32), _sds((B, S, dinp), BF16), _sds((B, S, qpw), BF16),
                   _sds((B, S, kvw), BF16)] + sshapes,
        compiler_params=_cparams(("arbitrary", "arbitrary")),
    )(x, x0, dx0a, mod, ln_g, dq, dk, dv, dqp, dkp, dmv, cq, ckv, w_in_p, w_uq_p, w_kv, gq, gkv, tc, ts1, ts2)


def _tile_of(n, cap):
    if n <= cap:
        return n
    best = n
    for t in range(LANES, cap + 1, LANES):
        if n % t == 0:
            best = t
    return best


def _mm_tn(a, g, name, after, relu_sq=False, out_dtype=F32, col_blocks=None):
    T, K = a.shape
    N = g.shape[1]
    tt = 1024 if T % 1024 == 0 else (512 if T % 512 == 0 else T)
    tk = _tile_of(K, 1024)
    tn = _tile_of(N, 1280)
    nt = T // tt
    bw = N // col_blocks if col_blocks else tn
    assert tn % bw == 0

    def body(a_ref, g_ref, _, o_ref, acc_ref):
        @pl.when(pl.program_id(2) == 0)
        def _():
            acc_ref[...] = jnp.zeros_like(acc_ref)

        av = a_ref[...]
        if relu_sq:
            av = jnp.square(jnp.maximum(av.astype(F32), 0.0)).astype(BF16)
        acc_ref[...] += _dot_tn(av, g_ref[...])

        @pl.when(pl.program_id(2) == nt - 1)
        def _():
            if col_blocks:
                for c in range(tn // bw):
                    o_ref[c] = acc_ref[:, c * bw:(c + 1) * bw].astype(out_dtype)
            else:
                o_ref[...] = acc_ref[...].astype(out_dtype)

    if col_blocks:
        out_spec = pl.BlockSpec((tn // bw, tk, bw), lambda i, j, t: (j, i, 0))
        out_shape = _sds((col_blocks, K, bw), out_dtype)
    else:
        out_spec = pl.BlockSpec((tk, tn), lambda i, j, t: (i, j))
        out_shape = _sds((K, N), out_dtype)
    return pl.pallas_call(
        body, name=name, grid=(K // tk, N // tn, nt),
        in_specs=[pl.BlockSpec((tt, tk), lambda i, j, t: (t, i)), pl.BlockSpec((tt, tn), lambda i, j, t: (t, j)),
                  _AFTER],
        out_specs=out_spec, out_shape=out_shape,
        scratch_shapes=[pltpu.VMEM((tk, tn), F32)],
        compiler_params=_cparams(("parallel", "parallel", "arbitrary")),
    )(a, g, after)


def _reduce_adamw(parts, w, m, v, name):
    _, K, N = parts.shape
    tr = 256 if K % 256 == 0 else K

    def body(p_ref, w_ref, m_ref, v_ref, g_ref, d_ref, nm_ref, nv_ref):
        g = p_ref[0].astype(F32)
        for k in range(1, 4):
            g = g + p_ref[k].astype(F32)
        g_ref[0] = g
        d_ref[0], nm_ref[0], nv_ref[0] = _adamw(w_ref[0], g, m_ref[0], v_ref[0])

    spec = pl.BlockSpec((1, tr, N), lambda r: (0, r, 0))
    return pl.pallas_call(
        body, name=name, grid=(K // tr,),
        in_specs=[pl.BlockSpec((4, tr, N), lambda r: (0, r, 0)), spec, spec, spec],
        out_specs=[spec] * 4, out_shape=[_sds((1, K, N), F32)] * 4,
        compiler_params=_cparams(("parallel",)),
    )(parts, w, m, v)


def _finish(sm, dmod_all, dmod_my, cact_all, p_small, m_small, v_small, b_ada, m_b, v_b, w_ada, m_w, v_w):
    n0 = p_small.shape[1]
    n1 = sm.shape[1]
    d = cact_all.shape[1]

    def body(sm_ref, dma_ref, dmm_ref, ca_ref, p_ref, pm_ref, pv_ref, b_ref, bm_ref, bv_ref, w_ref, wm_ref, wv_ref,
             gs_ref, ds_ref, ms_ref, vs_ref, gb_ref, db_ref, mb_ref, vb_ref, gw_ref, dw_ref, mw_ref, vw_ref,
             loss_ref):
        gs = sm_ref[0:1, :]
        for k in range(1, N_DEV):
            gs = gs + sm_ref[k:k + 1, :]
        gs_ref[...] = gs
        ds_ref[...], ms_ref[...], vs_ref[...] = _adamw(p_ref[...], gs[:, 0:n0], pm_ref[...], pv_ref[...])
        loss_ref[...] = jnp.zeros((1, LANES), F32) + jnp.sum(gs[:, n1 - d:n1])
        gb = jnp.sum(dma_ref[...], axis=0, keepdims=True)
        gb_ref[...] = gb
        db_ref[...], mb_ref[...], vb_ref[...] = _adamw(b_ref[...], gb, bm_ref[...], bv_ref[...])
        gw = _dot_tn(ca_ref[...].astype(BF16), dmm_ref[...].astype(BF16))
        gw_ref[...] = gw
        dw_ref[...], mw_ref[...], vw_ref[...] = _adamw(w_ref[...], gw, wm_ref[...], wv_ref[...])

    s0 = _sds(p_small.shape, F32)
    sb = _sds(b_ada.shape, F32)
    sw = _sds(w_ada.shape, F32)
    return pl.pallas_call(
        body, name="finish_small",
        out_shape=[_sds((1, n1), F32), s0, s0, s0, sb, sb, sb, sb, sw, sw, sw, sw,
                   _sds((1, LANES), F32)],
        compiler_params=pltpu.CompilerParams(vmem_limit_bytes=VMEM_LIMIT),
    )(sm, dmod_all, dmod_my, cact_all, p_small, m_small, v_small, b_ada, m_b, v_b, w_ada, m_w, v_w)


def _pack(arrs, dtype, width):
    flat = jnp.concatenate([a.astype(dtype).reshape(-1) for a in arrs])
    rows = -(-flat.shape[0] // (256 * width)) * 256
    return jnp.pad(flat, (0, rows * width - flat.shape[0])).reshape(rows, width)


def _unpack(slab, shapes):
    flat = slab.reshape(-1)
    out, o = [], 0
    for s in shapes:
        n = math.prod(s)
        out.append(flat[o:o + n].reshape(s))
        o += n
    return out


def _rope_tables(S):
    inv_freq = 1.0 / (ROPE_BASE ** (jnp.arange(0, MLA_ROPE, 2, dtype=F32) / MLA_ROPE))
    ang = jnp.arange(S, dtype=F32)[:, None] * inv_freq[None, :]
    cos, sin = jnp.cos(ang), jnp.sin(ang)
    one = jnp.ones((S, MLA_NOPE), F32)
    z16 = jnp.zeros((S, 16), F32)
    z32 = jnp.zeros((S, 32), F32)
    z64 = jnp.zeros((S, MLA_NOPE), F32)
    tc = jnp.concatenate([one, cos, cos, jnp.ones((S, 32), F32)], axis=1)
    ts1 = jnp.concatenate([z64, -sin, z16, z32], axis=1)
    ts2 = jnp.concatenate([z64, z16, sin, z32], axis=1)
    return tc, ts1, ts2


def kernel(x, c, ln_in_g, ln_in_b, w_ada, b_ada, w_in, q_norm_g, kv_norm_g, w_uq, w_ukv, w_o, ln1_g, ln1_b, w_up, w_down, ln2_g, ln2_b, loss_target, m_ln_in_g, m_ln_in_b, m_w_ada, m_b_ada, m_w_in, m_q_norm_g, m_kv_norm_g, m_w_uq, m_w_ukv, m_w_o, m_ln1_g, m_ln1_b, m_w_up, m_w_down, m_ln2_g, m_ln2_b, v_ln_in_g, v_ln_in_b, v_w_ada, v_b_ada, v_w_in, v_q_norm_g, v_kv_norm_g, v_w_uq, v_w_ukv, v_w_o, v_ln1_g, v_ln1_b, v_w_up, v_w_down, v_ln2_g, v_ln2_b):
    B, S, D = x.shape
    sbw = D // 2
    mlw = D - sbw
    nh = mlw // MLA_V
    qr = w_uq.shape[1]
    kvr = w_ukv.shape[1]
    qk = MLA_NOPE + MLA_ROPE
    dff = w_up.shape[2] * N_DEV
    din = w_in.shape[2] * N_DEV
    tm = 512 if S % 512 == 0 else S
    tq = min(512, S // 2)
    dm = dict(tm=tm, tm_small=min(tm, 256), tq=tq, sbw=sbw, qr=qr, kvr=kvr, nh=nh)
    width = 1024 if D >= 1024 else LANES
    dev = 4 * lax.axis_index("x") + 2 * lax.axis_index("y") + lax.axis_index("c")

    big = [w_in, w_uq, w_ukv, w_o, w_up, w_down]
    first_w, first_token = _chip_exchange_start([a[0].astype(BF16) for a in big[:3]], "gather_w_first_start",
                                                scatter=False, after=c)

    nada = w_ada.shape[2]
    c_all = _all_gather([c + first_token[0, 0]], "gather_c")[0].reshape(N_DEV * B, D)
    b_loc = lax.dynamic_slice(b_ada, (0, dev * nada), (1, nada))
    cact_all, mod_part = _ada_partial(c_all, w_ada[0], b_loc)
    mod_all = _all_gather([mod_part], "gather_mod")[0]
    mod = lax.dynamic_slice(mod_all, (0, dev * B, 0), (N_DEV, B, nada))
    mod = jnp.swapaxes(mod, 0, 1).reshape(B, N_MOD, D)

    first_by_chip = _chip_exchange_wait(first_w, mod_all, "gather_w_first_wait")
    w_in8, w_uq8, w_ukv8 = [b.reshape((N_DEV,) + b.shape[2:]) for b in _core_gather(first_by_chip, "gather_w_first_cores")]
    late_w, late_token = _chip_exchange_start([a[0].astype(BF16) for a in big[3:]], "gather_w_late_start",
                                              scatter=False, after=w_in8)
    cols = lambda a8: jnp.swapaxes(a8, 0, 1).reshape(a8.shape[1], N_DEV * a8.shape[2])
    w_in_p = jnp.pad(cols(w_in8), ((0, 0), (0, LANES - MLA_ROPE)))
    zpad = jnp.zeros((qr, nh, HEAD_PAD - qk), BF16)
    w_uq_p = jnp.concatenate([cols(w_uq8).reshape(qr, nh, qk), zpad], axis=2).reshape(qr, nh * HEAD_PAD)
    w_ukv_f = cols(w_ukv8)
    w_uk = w_ukv_f[:, :nh * MLA_NOPE].reshape(kvr, nh, MLA_NOPE)
    w_uk_p = jnp.concatenate([w_uk, jnp.zeros((kvr, nh, HEAD_PAD - MLA_NOPE), BF16)], axis=2)
    w_kv = jnp.concatenate([w_uk_p.reshape(kvr, nh * HEAD_PAD), w_ukv_f[:, nh * MLA_NOPE:]], axis=1)

    tc, ts1, ts2 = _rope_tables(S)
    g_in, b_in = ln_in_g.reshape(1, D), ln_in_b.reshape(1, D)
    (x0, h, sq, sk, sv, qp, kp, mv, cq, ckv, qn, kvn) = _inproj_fwd(
        x, mod, g_in, b_in, w_in_p, w_uq_p, w_kv, q_norm_g, kv_norm_g, tc, ts1, ts2, dm, late_token)
    sb_y, sb_tot = _sb_fwd(sq, sk, sv, dm)
    mla_y, mla_lse = _mla_fwd(qp, kp, mv, dm, sb_tot)
    late_by_chip = _chip_exchange_wait(late_w, mla_lse, "gather_w_late_wait")
    w_o8, w_up8, w_down8 = [b.reshape((N_DEV,) + b.shape[2:]) for b in _core_gather(late_by_chip, "gather_w_late_cores")]
    w_o_f = w_o8.reshape(D, D)
    mix, x1, h2 = _outproj_fwd(sb_y, mla_y, x0, mod, w_o_f, ln1_g, ln1_b, dm)
    u, dr2, bst_c, wst_c = _mlp_fwd(h2, x1, mod, loss_target, w_up8, w_down8, ln2_g, ln2_b, dm)

    du, dffb, dmixb, dx0a, dsb_y, dmla_y, bst_b, wst_b = _mlp_bwd(
        dr2, u, x1, x0, mix, mod, w_up8, w_down8, w_o_f, ln1_g, dm)
    T = B * S
    r2 = lambda a: a.reshape(T, a.shape[2])
    by_core = lambda a: a.reshape((4, 2) + a.shape[1:])
    g_o = jnp.concatenate([_mm_tn(r2(sb_y), r2(dmixb), "grad_w_o_sb", dr2, out_dtype=BF16),
                           _mm_tn(r2(mla_y), r2(dmixb), "grad_w_o_mla", dr2, out_dtype=BF16)], axis=0)
    g_up8 = _mm_tn(r2(h2), r2(du), "grad_w_up", dr2, out_dtype=BF16, col_blocks=N_DEV)
    g_down = _mm_tn(r2(u), r2(dffb), "grad_w_down", dr2, relu_sq=True, out_dtype=BF16)
    early = [g_o.reshape(N_DEV, D // N_DEV, D), g_up8, g_down.reshape(N_DEV, dff // N_DEV, D)]
    early_sum = _core_scatter_sum([by_core(a) for a in early], "scatter_g_early_cores")
    early_g, early_token = _chip_exchange_start(early_sum, "scatter_g_early_start", scatter=True, after=dr2)

    dsq, dsk, dsv = _sb_bwd(sq, sk, sv, sb_tot, dsb_y, dm, early_token)
    dqp, dkp, dmv = _mla_bwd(qp, kp, mv, mla_y, mla_lse, dmla_y, dm, dsq)
    grad_x, dproj, dqpre, dkvo, bst_a, wst_a = _inproj_bwd(
        x, x0, dx0a, mod, g_in, dsq, dsk, dsv, dqp, dkp, dmv, cq, ckv, w_in_p, w_uq_p, w_kv, q_norm_g, kv_norm_g,
        tc, ts1, ts2, dm)

    dmod = jnp.concatenate([bst_a[:, 1], bst_a[:, 0], bst_b[:, 2], bst_b[:, 1], bst_b[:, 0], bst_c[:, 0]], axis=1)
    small = jnp.concatenate([wst_a[0], wst_a[1], wst_a[4, :qr], wst_a[5, :kvr], wst_b[0], wst_b[1],
                             wst_c[0], wst_c[1], wst_c[2]])
    n1 = small.shape[0]
    small_g, small_token = _chip_exchange_start([_pack([dmod, small], F32, LANES)], "gather_small_start",
                                                scatter=False, after=grad_x)
    g_in_p = _mm_tn(r2(h), r2(dproj), "grad_w_in", small_token)
    g_uq_p = _mm_tn(r2(qn), r2(dqpre), "grad_w_uq", small_token)
    g_kv = _mm_tn(r2(kvn), r2(dkvo), "grad_w_kv", small_token)
    small_by_chip = _chip_exchange_wait(small_g, g_kv, "gather_small_wait")
    both = _core_gather(small_by_chip, "gather_small_cores")[0].reshape(N_DEV, -1)
    g_uq_f = g_uq_p.reshape(qr, nh, HEAD_PAD)[:, :, :qk].reshape(qr, nh * qk)
    g_uk = g_kv[:, :nh * HEAD_PAD].reshape(kvr, nh, HEAD_PAD)[:, :, :MLA_NOPE].reshape(kvr, nh * MLA_NOPE)
    g_ukv_f = jnp.concatenate([g_uk, g_kv[:, nh * HEAD_PAD:]], axis=1)
    early_quarter = _chip_exchange_wait(early_g, g_kv, "scatter_g_early_wait")

    def by_dest_cols(a):
        k, n = a.shape[0], a.shape[1] // N_DEV
        return jnp.swapaxes(a.reshape(k, N_DEV, n), 0, 1).astype(BF16)

    last = [by_dest_cols(g_in_p[:, :din]), by_dest_cols(g_uq_f), by_dest_cols(g_ukv_f)]
    last_sum = _core_scatter_sum([by_core(a) for a in last], "scatter_g_last_cores")
    last_g, last_token = _chip_exchange_start(last_sum, "scatter_g_last_start", scatter=True, after=grad_x)
    names = ["w_in", "w_uq", "w_ukv", "w_o", "w_up", "w_down"]
    moms = [m_w_in, m_w_uq, m_w_ukv, m_w_o, m_w_up, m_w_down]
    vars_ = [v_w_in, v_w_uq, v_w_ukv, v_w_o, v_w_up, v_w_down]
    res_early = [_reduce_adamw(p, w, m, v, "adamw_" + n)
                 for p, w, m, v, n in zip(early_quarter, big[3:], moms[3:], vars_[3:], names[3:])]

    dmod_all = both[:, :B * N_MOD * D].reshape(N_DEV * B, N_MOD * D)
    sm = both[:, B * N_MOD * D:B * N_MOD * D + n1] + last_token[0, 0]
    dmod_my = lax.dynamic_slice(dmod_all, (0, dev * nada), (N_DEV * B, nada))
    row = lambda arrs: jnp.concatenate([a.reshape(1, -1) for a in arrs], axis=1)
    smalls = [ln_in_g, ln_in_b, q_norm_g, kv_norm_g, ln1_g, ln1_b, ln2_g, ln2_b]
    small_shapes = [a.shape for a in smalls]
    (gs, ds, nms, nvs, g_b, d_b, nm_b, nv_b, g_w, d_w, nm_w, nv_w, loss_v) = _finish(
        sm, dmod_all, dmod_my, cact_all, row(smalls),
        row([m_ln_in_g, m_ln_in_b, m_q_norm_g, m_kv_norm_g, m_ln1_g, m_ln1_b, m_ln2_g, m_ln2_b]),
        row([v_ln_in_g, v_ln_in_b, v_q_norm_g, v_kv_norm_g, v_ln1_g, v_ln1_b, v_ln2_g, v_ln2_b]),
        b_ada, m_b_ada, v_b_ada, w_ada[0], m_w_ada[0], v_w_ada[0])
    gsm, dsm, nmsm, nvsm = (_unpack(s, small_shapes) for s in (gs, ds, nms, nvs))
    last_quarter = _chip_exchange_wait(last_g, loss_v, "scatter_g_last_wait")
    res_last = [_reduce_adamw(p, w, m, v, "adamw_" + n)
                for p, w, m, v, n in zip(last_quarter, big[:3], moms[:3], vars_[:3], names[:3])]
    gb, db, nmb, nvb = ([r[i] for r in res_last + res_early] for i in range(4))

    def ordered(sm_l, w_l, ada_w, ada_b):
        return [sm_l[0], sm_l[1], ada_w[None], ada_b, w_l[0], sm_l[2], sm_l[3], w_l[1], w_l[2], w_l[3],
                sm_l[4], sm_l[5], w_l[4], w_l[5], sm_l[6], sm_l[7]]

    loss = loss_v[0, 0]
    return (loss, grad_x, *ordered(gsm, gb, g_w, g_b), *ordered(dsm, db, d_w, d_b),
            *ordered(nmsm, nmb, nm_w, nm_b), *ordered(nvsm, nvb, nv_w, nv_b))
```

```python
import functools
import math

import jax
import jax.numpy as jnp
from jax import lax
from jax.experimental import pallas as pl
from jax.experimental.pallas import tpu as pltpu

F32 = jnp.float32
BF16 = jnp.bfloat16

SB_HD = 64
MLA_V = 64
MLA_NOPE = 64
MLA_ROPE = 32
HEAD_PAD = 128
CHUNK = 64
ROPE_BASE = 10000.0
LN_EPS = 1e-5
RMS_EPS = 1e-6
DEPTH = 1
ALPHA = (2.0 * DEPTH) ** 0.25
N_MOD = 6
ADAM_LR = 0.001
ADAM_B1 = 0.9
ADAM_B2 = 0.999
ADAM_EPS = 1e-08
ADAM_WD = 0.01
ADAM_STEP = 10
N_DEV = 8
LANES = 128
LOG2E = 1.4426950408889634
CUMSUM_W = 256
VMEM_LIMIT = 56 * 1024 * 1024
MESH = pl.DeviceIdType.MESH


def _dot(a, b):
    return jnp.dot(a, b, preferred_element_type=F32)


def _dot_nt(a, b):
    return lax.dot_general(a, b, (((1,), (1,)), ((), ())), preferred_element_type=F32)


def _dot_tn(a, b):
    return lax.dot_general(a, b, (((0,), (0,)), ((), ())), preferred_element_type=F32)


def _cparams(sem):
    return pltpu.CompilerParams(dimension_semantics=sem, vmem_limit_bytes=VMEM_LIMIT)


def _full(a):
    nd = a.ndim
    return pl.BlockSpec(a.shape, lambda *_: (0,) * nd, pipeline_mode=pl.Buffered(1))


def _tok(tm, w):
    return pl.BlockSpec((1, tm, w), lambda b, s: (b, s, 0))


def _perb(rows, w):
    return pl.BlockSpec((1, rows, w), lambda b, s: (b, 0, 0))


def _sds(shape, dtype):
    return jax.ShapeDtypeStruct(shape, dtype)


def _ln_fwd(x, g, b):
    mu = jnp.mean(x, axis=-1, keepdims=True)
    xc = x - mu
    var = jnp.mean(xc * xc, axis=-1, keepdims=True)
    rstd = lax.rsqrt(var + LN_EPS)
    xhat = xc * rstd
    return xhat * g + b, xhat, rstd


def _ln_bwd(dy, xhat, rstd, g):
    dxh = dy * g
    m1 = jnp.mean(dxh, axis=-1, keepdims=True)
    m2 = jnp.mean(dxh * xhat, axis=-1, keepdims=True)
    return rstd * (dxh - m1 - xhat * m2)


def _colsum(a):
    return jnp.sum(a, axis=0, keepdims=True)


def _rope(x, c, s1, s2):
    w = x.shape[-1]
    return x * c + pltpu.roll(x, w - 16, 1) * s1 + pltpu.roll(x, 16, 1) * s2


def _rope_t(x, c, s1, s2):
    w = x.shape[-1]
    return x * c - pltpu.roll(x, w - 16, 1) * s1 - pltpu.roll(x, 16, 1) * s2


def _adamw(w, g, m, v):
    m = ADAM_B1 * m + (1.0 - ADAM_B1) * g
    v = ADAM_B2 * v + (1.0 - ADAM_B2) * (g * g)
    m_hat = m / (1.0 - ADAM_B1 ** ADAM_STEP)
    v_hat = v / (1.0 - ADAM_B2 ** ADAM_STEP)
    delta = -ADAM_LR * (m_hat / (jnp.sqrt(v_hat) + ADAM_EPS) + ADAM_WD * w)
    return delta, m, v


def _my_place():
    return lax.axis_index("x"), lax.axis_index("y"), lax.axis_index("c")


def _chip_peers(mx, my):
    out = []
    for j in (1, 2, 3):
        px = 1 - mx if (j >> 1) else mx
        py = 1 - my if (j & 1) else my
        out.append((px, py, 2 * px + py))
    return out


def _hbm_call(body, name, n_in, out_shape, sems):
    hbm = pl.BlockSpec(memory_space=pl.ANY)
    return pl.pallas_call(
        body, name=name, out_shape=out_shape,
        in_specs=[hbm] * n_in, out_specs=[hbm] * len(out_shape),
        scratch_shapes=[pltpu.SemaphoreType.DMA(s) for s in sems])


def _chip_exchange(xs, name, scatter):
    n = len(xs)

    def body(*refs):
        x_refs, o_refs = refs[:n], refs[n:2 * n]
        ssem, rsem, lsem = refs[2 * n:]
        mx, my, mc = _my_place()
        me = 2 * mx + my
        peers = _chip_peers(mx, my)

        def copy(i, j, src_slot, dst_slot):
            px, py, _ = peers[j]
            return pltpu.make_async_remote_copy(
                src_ref=x_refs[i].at[src_slot] if scatter else x_refs[i], dst_ref=o_refs[i].at[dst_slot],
                send_sem=ssem.at[i, j], recv_sem=rsem.at[i, j], device_id=(px, py, mc), device_id_type=MESH)

        local = [pltpu.make_async_copy(x_refs[i].at[me] if scatter else x_refs[i], o_refs[i].at[me], lsem.at[i])
                 for i in range(n)]
        sends = [copy(i, j, peers[j][2], me) for i in range(n) for j in range(3)]
        for cp in local + sends:
            cp.start()
        for i in range(n):
            for j in range(3):
                copy(i, j, peers[j][2], peers[j][2]).wait_recv()
        for cp in sends:
            cp.wait_send()
        for cp in local:
            cp.wait()

    out_shape = [_sds((4,) + tuple(x.shape[1:] if scatter else x.shape), x.dtype) for x in xs]
    return _hbm_call(body, name, n, out_shape, [(n, 3), (n, 3), (n,)])(*xs)


def _chip_exchange_start(xs, name, scatter, after):
    n = len(xs)
    blks = [tuple(x.shape[1:] if scatter else x.shape) for x in xs]

    def body(*refs):
        x_refs, land_refs = refs[:n], refs[n:2 * n]
        ssem, rsem = refs[2 * n + 1], refs[2 * n + 2]
        token = refs[-1]
        mx, my, mc = _my_place()
        me = 2 * mx + my
        for i in range(n):
            for j, (px, py, pk) in enumerate(_chip_peers(mx, my)):
                pltpu.make_async_remote_copy(
                    src_ref=x_refs[i].at[pk] if scatter else x_refs[i], dst_ref=land_refs[i].at[me],
                    send_sem=ssem.at[3 * i + j], recv_sem=rsem.at[3 * i + j], device_id=(px, py, mc),
                    device_id_type=MESH).start()
        token[...] = jnp.zeros_like(token)

    hbm = pl.BlockSpec(memory_space=pltpu.HBM)
    sem = pl.BlockSpec(memory_space=pltpu.SEMAPHORE)
    lands = [lax.empty((4,) + b, x.dtype) for b, x in zip(blks, xs)]
    res = pl.pallas_call(
        body, name=name,
        out_shape=[pltpu.SemaphoreType.DMA((3 * n,)), pltpu.SemaphoreType.DMA((3 * n,))]
        + [pltpu.HBM(x.shape, x.dtype) for x in xs] + [pltpu.HBM(l.shape, l.dtype) for l in lands]
        + [_sds((8, LANES), F32)],
        in_specs=[hbm] * (2 * n) + [_AFTER],
        out_specs=[sem, sem] + [hbm] * (2 * n) + [pl.BlockSpec(memory_space=pltpu.VMEM)],
        input_output_aliases={i: 2 + i for i in range(2 * n)},
        compiler_params=pltpu.CompilerParams(has_side_effects=pltpu.SideEffectType.DATAFLOW_SIDE_EFFECTING),
    )(*[pltpu.with_memory_space_constraint(a, pltpu.HBM) for a in list(xs) + lands], after)
    return dict(ssem=res[0], rsem=res[1], xs=res[2:2 + n], lands=res[2 + n:2 + 2 * n], n=n, scatter=scatter), res[-1]


def _chip_exchange_wait(handle, after, name):
    n, scatter = handle["n"], handle["scatter"]

    def body(*refs):
        x_refs, land_refs = refs[:n], refs[n:2 * n]
        ssem, rsem = refs[2 * n], refs[2 * n + 1]
        mx, my, mc = _my_place()
        for i in range(n):
            for j, (px, py, pk) in enumerate(_chip_peers(mx, my)):
                cp = pltpu.make_async_remote_copy(
                    src_ref=x_refs[i].at[pk] if scatter else x_refs[i], dst_ref=land_refs[i].at[pk],
                    send_sem=ssem.at[3 * i + j], recv_sem=rsem.at[3 * i + j], device_id=(px, py, mc),
                    device_id_type=MESH)
                cp.wait_send()
                cp.wait_recv()

    hbm = pl.BlockSpec(memory_space=pltpu.HBM)
    sem = pl.BlockSpec(memory_space=pltpu.SEMAPHORE)
    ops = list(handle["xs"]) + list(handle["lands"])
    res = pl.pallas_call(
        body, name=name,
        out_shape=[pltpu.HBM(a.shape, a.dtype) for a in ops],
        in_specs=[hbm] * (2 * n) + [sem, sem, pl.BlockSpec(memory_space=pl.ANY)],
        out_specs=[hbm] * (2 * n),
        input_output_aliases={i: i for i in range(2 * n)},
        compiler_params=pltpu.CompilerParams(has_side_effects=pltpu.SideEffectType.DATAFLOW_SIDE_EFFECTING),
    )(*ops, handle["ssem"], handle["rsem"], after)
    me = 2 * lax.axis_index("x") + lax.axis_index("y")
    out = []
    for x, land in zip(res[:n], res[n:]):
        own = lax.dynamic_index_in_dim(x, me, 0, keepdims=False) if scatter else x
        out.append(lax.dynamic_update_index_in_dim(land, own, me, 0))
    return out


def _core_gather(xs, name):
    n = len(xs)

    def body(*refs):
        x_refs, o_refs, mine, got = refs[:n], refs[n:2 * n], refs[2 * n:3 * n], refs[3 * n:4 * n]
        lsem, ssem, rsem, osem = refs[4 * n:]
        mx, my, mc = _my_place()
        loads = [pltpu.make_async_copy(x_refs[i], mine[i], lsem.at[i]) for i in range(n)]
        for cp in loads:
            cp.start()
        sends, stores = [], []
        for i in range(n):
            loads[i].wait()
            cp = pltpu.make_async_remote_copy(
                src_ref=mine[i], dst_ref=got[i], send_sem=ssem.at[i], recv_sem=rsem.at[i],
                device_id=(mx, my, 1 - mc), device_id_type=MESH)
            cp.start()
            sends.append(cp)
            for k in range(4):
                st = pltpu.make_async_copy(mine[i].at[k], o_refs[i].at[k, mc], osem.at[i, k])
                st.start()
                stores.append(st)
        for i in range(n):
            sends[i].wait_recv()
            for k in range(4):
                st = pltpu.make_async_copy(got[i].at[k], o_refs[i].at[k, 1 - mc], osem.at[n + i, k])
                st.start()
                stores.append(st)
        for cp in sends:
            cp.wait_send()
        for st in stores:
            st.wait()

    hbm = pl.BlockSpec(memory_space=pl.ANY)
    bufs = [pltpu.VMEM(x.shape, x.dtype) for x in xs]
    return pl.pallas_call(
        body, name=name,
        out_shape=[_sds((4, 2) + tuple(x.shape[1:]), x.dtype) for x in xs],
        in_specs=[hbm] * n, out_specs=[hbm] * n,
        scratch_shapes=bufs + bufs + [pltpu.SemaphoreType.DMA((n,)), pltpu.SemaphoreType.DMA((n,)),
                                      pltpu.SemaphoreType.DMA((n,)), pltpu.SemaphoreType.DMA((2 * n, 4))],
        compiler_params=pltpu.CompilerParams(vmem_limit_bytes=VMEM_LIMIT),
    )(*xs)


def _rows_step(k):
    for r in (256, 128, 64, 32, 16, 8):
        if k % r == 0:
            return r
    return k


def _core_scatter_sum(gs, name):
    n = len(gs)

    def body(*refs):
        g_refs, o_refs = refs[:n], refs[n:2 * n]
        send, got, mine = refs[2 * n:3 * n], refs[3 * n:4 * n], refs[4 * n:5 * n]
        lsem, msem, ssem, rsem, osem = refs[5 * n:]
        mx, my, mc = _my_place()
        pairs = [(i, k) for i in range(n) for k in range(4)]
        out_loads = {(i, k): pltpu.make_async_copy(g_refs[i].at[k, 1 - mc], send[i].at[k], lsem.at[i, k])
                     for i, k in pairs}
        own_loads = {(i, k): pltpu.make_async_copy(g_refs[i].at[k, mc], mine[i].at[k], msem.at[i, k])
                     for i, k in pairs}
        for p in pairs:
            out_loads[p].start()
        for p in pairs:
            own_loads[p].start()
        sends = []
        for i in range(n):
            for k in range(4):
                out_loads[i, k].wait()
            cp = pltpu.make_async_remote_copy(
                src_ref=send[i], dst_ref=got[i], send_sem=ssem.at[i], recv_sem=rsem.at[i],
                device_id=(mx, my, 1 - mc), device_id_type=MESH)
            cp.start()
            sends.append(cp)
        stores = []
        for i in range(n):
            for k in range(4):
                own_loads[i, k].wait()
            sends[i].wait_recv()
            rows = g_refs[i].shape[2]
            step = _rows_step(rows)

            def add(r, _, i=i, step=step):
                sl = pl.ds(pl.multiple_of(r * step, step), step)
                for k in range(4):
                    mine[i][k, sl, :] = (mine[i][k, sl, :].astype(F32) + got[i][k, sl, :].astype(F32)).astype(BF16)
                return 0

            lax.fori_loop(0, rows // step, add, 0)
            st = pltpu.make_async_copy(mine[i], o_refs[i], osem.at[i])
            st.start()
            stores.append(st)
        for cp in sends:
            cp.wait_send()
        for st in stores:
            st.wait()

    hbm = pl.BlockSpec(memory_space=pl.ANY)
    blk = [(4,) + tuple(g.shape[2:]) for g in gs]
    bufs = [pltpu.VMEM(b, BF16) for b in blk]
    return pl.pallas_call(
        body, name=name,
        out_shape=[_sds(b, BF16) for b in blk],
        in_specs=[hbm] * n, out_specs=[hbm] * n,
        scratch_shapes=bufs * 3 + [pltpu.SemaphoreType.DMA((n, 4)), pltpu.SemaphoreType.DMA((n, 4)),
                                   pltpu.SemaphoreType.DMA((n,)), pltpu.SemaphoreType.DMA((n,)),
                                   pltpu.SemaphoreType.DMA((n,))],
        compiler_params=pltpu.CompilerParams(vmem_limit_bytes=VMEM_LIMIT),
    )(*gs)


def _all_gather(xs, name):
    by_chip = _chip_exchange(xs, name + "_chips", scatter=False)
    both = _core_gather(by_chip, name + "_cores")
    return [b.reshape((N_DEV,) + tuple(x.shape)) for b, x in zip(both, xs)]


def _ada_partial(c_all, w_ada_loc, b_loc):
    def body(c_ref, w_ref, b_ref, act_ref, mod_ref):
        c = c_ref[...]
        act = c * (1.0 / (1.0 + jnp.exp(-c)))
        act_ref[...] = act
        mod_ref[...] = _dot(act.astype(BF16), w_ref[...].astype(BF16)) + b_ref[...]

    nb, d = c_all.shape
    return pl.pallas_call(
        body, name="ada_partial",
        out_shape=(_sds((nb, d), F32), _sds((nb, w_ada_loc.shape[1]), F32)),
        compiler_params=pltpu.CompilerParams(vmem_limit_bytes=VMEM_LIMIT),
    )(c_all, w_ada_loc, b_loc)


_AFTER = pl.BlockSpec(memory_space=pl.ANY)


def _inproj_fwd(x, mod, ln_g, ln_b, w_in_p, w_uq_p, w_kv, gq, gkv, tc, ts1, ts2, dm, after):
    B, S, D = x.shape
    tm = dm["tm"]
    sbw, qr, kvr, nh = dm["sbw"], dm["qr"], dm["kvr"], dm["nh"]
    o_cq, o_ckv, o_kr = 3 * sbw, 3 * sbw + qr, 3 * sbw + qr + kvr
    qpw = nh * HEAD_PAD

    def body(x_ref, mod_ref, g_ref, b_ref, win_ref, wuq_ref, wkv_ref, gq_ref, gkv_ref, tc_ref, ts1_ref, ts2_ref, _,
             x0_ref, h_ref, q_ref, k_ref, v_ref, qp_ref, kp_ref, mv_ref, cq_ref, ckv_ref, qn_ref, kvn_ref):
        x0, _, _ = _ln_fwd(x_ref[0], g_ref[...], b_ref[...])
        x0_ref[0] = x0
        mod = mod_ref[0]
        h = (x0 * (1.0 + mod[1:2]) + mod[0:1]).astype(BF16)
        h_ref[0] = h
        proj = _dot(h, win_ref[...])
        q_ref[0] = (proj[:, 0:sbw] * SB_Q_SCALE).astype(BF16)
        k_ref[0] = proj[:, sbw:2 * sbw].astype(BF16)
        v_ref[0] = proj[:, 2 * sbw:3 * sbw].astype(BF16)
        cq = proj[:, o_cq:o_cq + qr]
        ckv = proj[:, o_ckv:o_ckv + kvr]
        cq_ref[0] = cq
        ckv_ref[0] = ckv
        qn = (cq * lax.rsqrt(jnp.mean(cq * cq, axis=-1, keepdims=True) + RMS_EPS) * gq_ref[...]).astype(BF16)
        kvn = (ckv * lax.rsqrt(jnp.mean(ckv * ckv, axis=-1, keepdims=True) + RMS_EPS) * gkv_ref[...]).astype(BF16)
        qn_ref[0] = qn
        kvn_ref[0] = kvn
        c1, s1, s2 = tc_ref[...], ts1_ref[...], ts2_ref[...]
        c8, s18, s28 = jnp.tile(c1, (1, nh)), jnp.tile(s1, (1, nh)), jnp.tile(s2, (1, nh))
        qp_ref[0] = (_rope(_dot(qn, wuq_ref[...]), c8, s18, s28) * MLA_Q_SCALE).astype(BF16)
        kvo = _dot(kvn, wkv_ref[...])
        kr = pltpu.roll(proj[:, o_kr:o_kr + LANES], 64, 1)
        kr = _rope(kr, c1, s1, s2)
        kp_ref[0] = (kvo[:, 0:qpw] + jnp.tile(kr, (1, nh))).astype(BF16)
        mv_ref[0] = kvo[:, qpw:].astype(BF16)

    tab = pl.BlockSpec((tm, LANES), lambda b, s: (s, 0))
    outs = [(D, F32), (D, BF16), (sbw, BF16), (sbw, BF16), (sbw, BF16), (qpw, BF16), (qpw, BF16),
            (nh * MLA_V, BF16), (qr, F32), (kvr, F32), (qr, BF16), (kvr, BF16)]
    return pl.pallas_call(
        body, name="inproj_fwd", grid=(B, S // tm),
        in_specs=[_tok(tm, D), _perb(N_MOD, D), _full(ln_g), _full(ln_b), _full(w_in_p), _full(w_uq_p),
                  _full(w_kv), _full(gq), _full(gkv), tab, tab, tab, _AFTER],
        out_specs=[_tok(tm, w) for w, _ in outs],
        out_shape=[_sds((B, S, w), t) for w, t in outs],
        compiler_params=_cparams(("parallel", "parallel")),
    )(x, mod, ln_g, ln_b, w_in_p, w_uq_p, w_kv, gq, gkv, tc, ts1, ts2, after)


def _neg_abs(x):
    sign = jnp.uint32(0x80000000)
    return lax.bitcast_convert_type(lax.bitcast_convert_type(x, jnp.uint32) | sign, F32)


SB_Q_SCALE = -(SB_HD ** -0.5) * LOG2E
MLA_Q_SCALE = (MLA_NOPE + MLA_ROPE) ** -0.5 * LOG2E


def _log2_keep(zs):
    return jnp.minimum(zs, 0.0) - jnp.log2(1.0 + jnp.exp2(_neg_abs(zs)))


def _split_dot(a, u):
    hi = a.astype(BF16)
    lo = (a - hi.astype(F32)).astype(BF16)
    return _dot(hi, u) + _dot(lo, u)


def _tri(n, rel):
    row = lax.broadcasted_iota(jnp.int32, (n, n), 0)
    col = lax.broadcasted_iota(jnp.int32, (n, n), 1)
    return rel(row, col).astype(BF16)


def _running_sum(a, tri, reverse, split):
    cs = tri.shape[0]
    n = a.shape[1] // cs
    out = [None] * n
    run = None
    for c in (reversed(range(n)) if reverse else range(n)):
        part = a[:, c * cs:(c + 1) * cs]
        loc = _split_dot(part, tri) if split else _dot(part.astype(BF16), tri)
        out[c] = loc if run is None else loc + run
        tot = jnp.sum(part, axis=1, keepdims=True)
        run = tot if run is None else run + tot
    return (out[0] if n == 1 else jnp.concatenate(out, axis=1)), run


def _tile_mask(nr, nk, r0, c0, rel):
    row = lax.broadcasted_iota(jnp.int32, (nr, nk), 0) + r0
    col = lax.broadcasted_iota(jnp.int32, (nr, nk), 1) + c0
    return rel(row, col)


def _put_rows(whole, part, r0):
    return part if r0 == 0 else jnp.concatenate([whole[:r0], part], axis=0)


def _diag_tiles(tq, split):
    half = tq // 2
    return [(0, tq, 0, half), (half, half, half, half)] if split else [(0, tq, 0, tq)]


def _sb_fwd(q, k, v, dm):
    B, S, W = q.shape
    tq = dm["tq"]
    nq = S // tq

    def body(q_ref, k_ref, v_ref, y_ref, tot_ref):
        qi = pl.program_id(2)
        q2 = q_ref[0]
        lane = lax.broadcasted_iota(jnp.int32, (tq, LANES), 1)
        qs = jnp.concatenate([jnp.where(lane < SB_HD, q2, 0), jnp.where(lane >= SB_HD, q2, 0)], axis=0).astype(BF16)
        later = _tri(min(tq, CUMSUM_W), lambda a, b: a > b)
        assert tq & (tq - 1) == 0
        strict = _tile_mask(2 * tq, tq, 0, 0, lambda t, s: s < (t & (tq - 1)))

        def block(j, carry, masked):
            acc, run = carry
            off = pl.multiple_of(j * tq, tq)
            zs = _dot_nt(qs, k_ref[0, pl.ds(off, tq), :])
            a = _log2_keep(zs)
            if masked:
                a = jnp.where(strict, a, 0.0)
            a_later, a_tot = _running_sum(a, later, reverse=True, split=True)
            w = jnp.exp2((a - zs) + a_later + run)
            if masked:
                w = jnp.where(strict, w, 0.0)
            return acc + _dot(w.astype(BF16), v_ref[0, pl.ds(off, tq), :]), run + a_tot

        carry = block(qi, (jnp.zeros((2 * tq, LANES), F32), jnp.zeros((2 * tq, 1), F32)), True)
        acc, run = lax.fori_loop(0, qi, lambda jj, c: block(qi - 1 - jj, c, False), carry)
        y_ref[0] = jnp.where(lane < SB_HD, acc[:tq], acc[tq:]).astype(BF16)
        tot_ref[0] = jnp.where(lane < SB_HD, run[:tq], run[tq:])

    qspec = pl.BlockSpec((1, tq, LANES), lambda b, hp, i: (b, i, hp))
    kspec = pl.BlockSpec((1, S, LANES), lambda b, hp, i: (b, 0, hp))
    return pl.pallas_call(
        body, name="sb_fwd", grid=(B, W // LANES, nq),
        in_specs=[qspec, kspec, kspec],
        out_specs=[qspec, qspec],
        out_shape=[_sds((B, S, W), BF16), _sds((B, S, W), F32)],
        compiler_params=_cparams(("parallel", "parallel", "arbitrary")),
    )(q, k, v)


def _sb_bwd(q, k, v, tot, dy, dm, after):
    B, S, W = q.shape
    tq = dm["tq"]
    nq = S // tq

    def body(q_ref, k_ref, v_ref, tot_ref, dy_ref, _, dq_ref, dk_ref, dv_ref, dk_acc, dv_acc):
        qi = pl.program_id(2)

        @pl.when(qi == 0)
        def _():
            dk_acc[...] = jnp.zeros_like(dk_acc)
            dv_acc[...] = jnp.zeros_like(dv_acc)

        q2 = q_ref[0]
        dy2 = dy_ref[0]
        tot2 = tot_ref[0]
        lane = lax.broadcasted_iota(jnp.int32, (tq, LANES), 1)
        in_h = [lane < SB_HD, lane >= SB_HD]
        qh = [jnp.where(m, q2, 0).astype(BF16) for m in in_h]
        dyh = [jnp.where(m, dy2, 0).astype(BF16) for m in in_h]
        toth = [tot2[:, 0:1], tot2[:, SB_HD:SB_HD + 1]]

        def tile(j, carry, r0, nr, c0, nk, masked):
            off = pl.multiple_of(j * tq + c0, math.gcd(tq, c0))
            k2 = k_ref[0, pl.ds(off, nk), :]
            v2 = v_ref[0, pl.ds(off, nk), :]
            upto = _tri(min(nk, CUMSUM_W), lambda a, b: a <= b)
            before = _tri(min(nk, CUMSUM_W), lambda a, b: a < b)
            strict = _tile_mask(nr, nk, r0, c0, lambda t, s: s < t) if masked else None
            rows = slice(r0, r0 + nr)
            new = []
            dk_blk = jnp.zeros((nk, LANES), F32)
            dv_blk = jnp.zeros((nk, LANES), F32)
            for h in range(2):
                dq, pa, pg = carry[3 * h][rows], carry[3 * h + 1][rows], carry[3 * h + 2][rows]
                zs = _dot_nt(qh[h][rows], k2)
                a = _log2_keep(zs)
                if masked:
                    a = jnp.where(strict, a, 0.0)
                a_upto, a_tot = _running_sum(a, upto, reverse=False, split=True)
                w = jnp.exp2((a - zs) + ((toth[h][rows] - pa) - a_upto))
                if masked:
                    w = jnp.where(strict, w, 0.0)
                g = _dot_nt(dyh[h][rows], v2) * w
                g_before, g_tot = _running_sum(g, before, reverse=False, split=False)
                g_before = g_before + pg
                dz = (g + g_before) * jnp.exp2(a) - g_before
                if masked:
                    dz = jnp.where(strict, dz, 0.0)
                dzb = dz.astype(BF16)
                dv_blk = dv_blk + _dot_tn(w.astype(BF16), dyh[h][rows])
                dk_blk = dk_blk + _dot_tn(dzb, qh[h][rows])
                new += [_put_rows(carry[3 * h], dq + _dot(dzb, k2), r0), _put_rows(carry[3 * h + 1], pa + a_tot, r0),
                        _put_rows(carry[3 * h + 2], pg + g_tot, r0)]
            dk_acc[pl.ds(off, nk), :] += dk_blk
            dv_acc[pl.ds(off, nk), :] += dv_blk
            return tuple(new)

        zero = jnp.zeros((tq, LANES), F32)
        zrun = jnp.zeros((tq, 1), F32)
        carry = lax.fori_loop(0, qi, lambda j, c: tile(j, c, 0, tq, 0, tq, False),
                              (zero, zrun, zrun, zero, zrun, zrun))
        for r0, nr, c0, nk in _diag_tiles(tq, False):
            carry = tile(qi, carry, r0, nr, c0, nk, True)
        dq_ref[0] = (jnp.where(in_h[0], carry[0], carry[3]) * (SB_HD ** -0.5)).astype(BF16)

        @pl.when(qi == nq - 1)
        def _():
            dk_ref[0] = (dk_acc[...] * (-1.0 / LOG2E)).astype(BF16)
            dv_ref[0] = dv_acc[...].astype(BF16)

    qspec = pl.BlockSpec((1, tq, LANES), lambda b, hp, i: (b, i, hp))
    kspec = pl.BlockSpec((1, S, LANES), lambda b, hp, i: (b, 0, hp))
    return pl.pallas_call(
        body, name="sb_bwd", grid=(B, W // LANES, nq),
        in_specs=[qspec, kspec, kspec, qspec, qspec, _AFTER],
        out_specs=[qspec, kspec, kspec],
        out_shape=[_sds((B, S, W), BF16)] * 3,
        scratch_shapes=[pltpu.VMEM((S, LANES), F32), pltpu.VMEM((S, LANES), F32)],
        compiler_params=_cparams(("parallel", "parallel", "arbitrary")),
    )(q, k, v, tot, dy, after)


def _same_or_earlier_chunk(row, col):
    return lax.shift_right_logical(col, 6) <= lax.shift_right_logical(row, 6)


def _mla_fwd(qp, kp, mv, dm, after):
    B, S, QW = qp.shape
    VW = mv.shape[2]
    tq = dm["tq"]
    nq = S // tq
    scale = (MLA_NOPE + MLA_ROPE) ** -0.5
    assert CHUNK == 64

    def body(q_ref, k_ref, v_ref, _, y_ref, lse_ref):
        qi = pl.program_id(2)
        q2 = q_ref[0]
        lane = lax.broadcasted_iota(jnp.int32, (tq, LANES), 1)

        def tile(j, carry, r0, nr, c0, nk, masked):
            off = pl.multiple_of(j * tq + c0, math.gcd(tq, c0))
            v2 = v_ref[0, pl.ds(off, nk), :]
            allowed = _tile_mask(nr, nk, r0, c0, _same_or_earlier_chunk) if masked else None
            rows = slice(r0, r0 + nr)
            heads = range(2)
            sl = [slice(h * HEAD_PAD, (h + 1) * HEAD_PAD) for h in heads]
            m_old = [carry[3 * h + 1][rows] for h in heads]
            s = [_dot_nt(q2[rows, sl[h]], k_ref[0, pl.ds(off, nk), sl[h]]) for h in heads]
            if masked:
                s = [jnp.where(allowed, s[h], -1e30) for h in heads]
            m_new = [jnp.maximum(m_old[h], jnp.max(s[h], axis=1, keepdims=True)) for h in heads]
            alpha = [jnp.exp2(m_old[h] - m_new[h]) for h in heads]
            p = [jnp.exp2(s[h] - m_new[h]) for h in heads]
            acc = [alpha[h] * carry[3 * h][rows] + _dot(p[h].astype(BF16), v2) for h in heads]
            l = [alpha[h] * carry[3 * h + 2][rows] + jnp.sum(p[h], axis=1, keepdims=True) for h in heads]
            out = []
            for h in heads:
                out += [_put_rows(carry[3 * h], acc[h], r0), _put_rows(carry[3 * h + 1], m_new[h], r0),
                        _put_rows(carry[3 * h + 2], l[h], r0)]
            return tuple(out)

        zero = jnp.zeros((tq, LANES), F32)
        m0 = jnp.full((tq, 1), -1e30, F32)
        l0 = jnp.zeros((tq, 1), F32)
        carry = (zero, m0, l0, zero, m0, l0)
        for r0, nr, c0, nk in _diag_tiles(tq, False):
            carry = tile(qi, carry, r0, nr, c0, nk, True)
        carry = lax.fori_loop(0, qi, lambda j, c: tile(j, c, 0, tq, 0, tq, False), carry)
        y0 = carry[0] / carry[2]
        y1 = carry[3] / carry[5]
        y_ref[0] = jnp.where(lane < MLA_V, y0, y1).astype(BF16)
        lse_ref[0] = jnp.where(lane < MLA_V, carry[1] + jnp.log2(carry[2]), carry[4] + jnp.log2(carry[5]))

    qspec = pl.BlockSpec((1, tq, 2 * HEAD_PAD), lambda b, hp, i: (b, i, hp))
    kspec = pl.BlockSpec((1, S, 2 * HEAD_PAD), lambda b, hp, i: (b, 0, hp))
    vspec = pl.BlockSpec((1, S, LANES), lambda b, hp, i: (b, 0, hp))
    yspec = pl.BlockSpec((1, tq, LANES), lambda b, hp, i: (b, i, hp))
    return pl.pallas_call(
        body, name="mla_fwd", grid=(B, VW // LANES, nq),
        in_specs=[qspec, kspec, vspec, _AFTER],
        out_specs=[yspec, yspec],
        out_shape=[_sds((B, S, VW), BF16), _sds((B, S, VW), F32)],
        compiler_params=_cparams(("parallel", "parallel", "arbitrary")),
    )(qp, kp, mv, after)


def _mla_bwd(qp, kp, mv, y, lse, dy, dm, after):
    B, S, QW = qp.shape
    VW = mv.shape[2]
    tq = dm["tq"]
    nq = S // tq
    scale = (MLA_NOPE + MLA_ROPE) ** -0.5

    def body(q_ref, k_ref, v_ref, y_ref, lse_ref, dy_ref, _, dq_ref, dk_ref, dv_ref, dk_acc, dv_acc):
        qi = pl.program_id(2)

        @pl.when(qi == 0)
        def _():
            dk_acc[...] = jnp.zeros_like(dk_acc)
            dv_acc[...] = jnp.zeros_like(dv_acc)

        q2 = q_ref[0]
        dy2 = dy_ref[0]
        lse2 = lse_ref[0]
        lane = lax.broadcasted_iota(jnp.int32, (tq, LANES), 1)
        in_h = [lane < MLA_V, lane >= MLA_V]
        prod = dy2.astype(F32) * y_ref[0].astype(F32)
        delta = [jnp.sum(jnp.where(m, prod, 0.0), axis=1, keepdims=True) for m in in_h]
        dyh = [jnp.where(m, dy2, 0).astype(BF16) for m in in_h]
        lseh = [lse2[:, 0:1], lse2[:, MLA_V:MLA_V + 1]]

        def tile(j, carry, r0, nr, c0, nk, masked):
            off = pl.multiple_of(j * tq + c0, math.gcd(tq, c0))
            v2 = v_ref[0, pl.ds(off, nk), :]
            allowed = _tile_mask(nr, nk, r0, c0, _same_or_earlier_chunk) if masked else None
            rows = slice(r0, r0 + nr)
            heads = range(2)
            sl = [slice(h * HEAD_PAD, (h + 1) * HEAD_PAD) for h in heads]
            qhh = [q2[rows, sl[h]] for h in heads]
            dyr = [dyh[h][rows] for h in heads]
            kh = [k_ref[0, pl.ds(off, nk), sl[h]] for h in heads]
            s = [_dot_nt(qhh[h], kh[h]) for h in heads]
            dp = [_dot_nt(dyr[h], v2) for h in heads]
            if masked:
                s = [jnp.where(allowed, s[h], -1e30) for h in heads]
            p = [jnp.exp2(s[h] - lseh[h][rows]) for h in heads]
            dv_acc[pl.ds(off, nk), :] += _dot_tn(p[0].astype(BF16), dyr[0]) + _dot_tn(p[1].astype(BF16), dyr[1])
            ds = [(p[h] * (dp[h] - delta[h][rows])).astype(BF16) for h in heads]
            for h in heads:
                dk_acc[pl.ds(off, nk), sl[h]] += _dot_tn(ds[h], qhh[h])
            return tuple(_put_rows(carry[h], carry[h][rows] + _dot(ds[h], kh[h]), r0) for h in heads)

        zero = jnp.zeros((tq, HEAD_PAD), F32)
        carry = lax.fori_loop(0, qi, lambda j, c: tile(j, c, 0, tq, 0, tq, False), (zero, zero))
        for r0, nr, c0, nk in _diag_tiles(tq, True):
            carry = tile(qi, carry, r0, nr, c0, nk, True)
        dq_ref[0] = (jnp.concatenate([carry[0], carry[1]], axis=1) * scale).astype(BF16)

        @pl.when(qi == nq - 1)
        def _():
            dk_ref[0] = (dk_acc[...] * (1.0 / LOG2E)).astype(BF16)
            dv_ref[0] = dv_acc[...].astype(BF16)

    qspec = pl.BlockSpec((1, tq, 2 * HEAD_PAD), lambda b, hp, i: (b, i, hp))
    kspec = pl.BlockSpec((1, S, 2 * HEAD_PAD), lambda b, hp, i: (b, 0, hp))
    vspec = pl.BlockSpec((1, S, LANES), lambda b, hp, i: (b, 0, hp))
    yspec = pl.BlockSpec((1, tq, LANES), lambda b, hp, i: (b, i, hp))
    return pl.pallas_call(
        body, name="mla_bwd", grid=(B, VW // LANES, nq),
        in_specs=[qspec, kspec, vspec, yspec, yspec, yspec, _AFTER],
        out_specs=[qspec, kspec, vspec],
        out_shape=[_sds((B, S, QW), BF16), _sds((B, S, QW), BF16), _sds((B, S, VW), BF16)],
        scratch_shapes=[pltpu.VMEM((S, 2 * HEAD_PAD), F32), pltpu.VMEM((S, LANES), F32)],
        compiler_params=_cparams(("parallel", "parallel", "arbitrary")),
    )(qp, kp, mv, y, lse, dy, after)


def _outproj_fwd(sb_y, mla_y, x0, mod, w_o, ln_g, ln_b, dm):
    B, S, D = x0.shape
    tm = dm["tm"]
    sbw = sb_y.shape[2]

    def body(ya_ref, yb_ref, x0_ref, mod_ref, wo_ref, g_ref, b_ref, mix_ref, x1_ref, h2_ref):
        mod = mod_ref[0]
        mix = _dot(ya_ref[0], wo_ref[0:sbw, :]) + _dot(yb_ref[0], wo_ref[sbw:, :])
        mix_ref[0] = mix
        x1, _, _ = _ln_fwd(ALPHA * x0_ref[0] + (1.0 + mod[2:3]) * mix, g_ref[...], b_ref[...])
        x1_ref[0] = x1
        h2_ref[0] = (x1 * (1.0 + mod[4:5]) + mod[3:4]).astype(BF16)

    return pl.pallas_call(
        body, name="outproj_fwd", grid=(B, S // tm),
        in_specs=[_tok(tm, sbw), _tok(tm, mla_y.shape[2]), _tok(tm, D), _perb(N_MOD, D),
                  _full(w_o), _full(ln_g), _full(ln_b)],
        out_specs=[_tok(tm, D)] * 3,
        out_shape=[_sds((B, S, D), F32), _sds((B, S, D), F32), _sds((B, S, D), BF16)],
        compiler_params=_cparams(("parallel", "parallel")),
    )(sb_y, mla_y, x0, mod, w_o, ln_g, ln_b)


def _stat_specs(B, D):
    specs = [pl.BlockSpec((1, 8, D), lambda b, s: (b, 0, 0)), pl.BlockSpec((8, D), lambda b, s: (0, 0))]
    shapes = [_sds((B, 8, D), F32), _sds((8, D), F32)]
    return specs, shapes


def _stat_init(bst_ref, wst_ref):
    @pl.when(pl.program_id(1) == 0)
    def _():
        bst_ref[...] = jnp.zeros_like(bst_ref)

    @pl.when((pl.program_id(0) == 0) & (pl.program_id(1) == 0))
    def _():
        wst_ref[...] = jnp.zeros_like(wst_ref)


def _mlp_fwd(h2, x1, mod, target, w_up, w_down, ln_g, ln_b, dm):
    B, S, D = x1.shape
    tm = dm["tm"]
    nck, _, ck = w_up.shape
    dff = nck * ck

    def body(h2_ref, x1_ref, mod_ref, t_ref, wu_ref, wd_ref, g_ref, b_ref, u_ref, dr_ref, bst_ref, wst_ref):
        _stat_init(bst_ref, wst_ref)
        mod = mod_ref[0]
        g = g_ref[...]
        h2 = h2_ref[0]
        ff = jnp.zeros((tm, D), F32)
        for c in range(nck):
            u = _dot(h2, wu_ref[c])
            u_ref[0, :, c * ck:(c + 1) * ck] = u.astype(BF16)
            act = jnp.square(jnp.maximum(u, 0.0)).astype(BF16)
            ff = ff + _dot(act, wd_ref[c])
        x2, xhat, rstd = _ln_fwd(ALPHA * x1_ref[0] + (1.0 + mod[5:6]) * ff, g, b_ref[...])
        err = x2 - t_ref[0]
        dy = err * (1.0 / D)
        dr = _ln_bwd(dy, xhat, rstd, g)
        dr_ref[0] = dr
        bst_ref[0, 0:1, :] += _colsum(dr * ff)
        wst_ref[0:1, :] += _colsum(dy * xhat)
        wst_ref[1:2, :] += _colsum(dy)
        wst_ref[2:3, :] += _colsum(err * err) * (0.5 / D)

    sspecs, sshapes = _stat_specs(B, D)
    return pl.pallas_call(
        body, name="mlp_fwd", grid=(B, S // tm),
        in_specs=[_tok(tm, D), _tok(tm, D), _perb(N_MOD, D), _tok(tm, D), _full(w_up), _full(w_down),
                  _full(ln_g), _full(ln_b)],
        out_specs=[_tok(tm, dff), _tok(tm, D)] + sspecs,
        out_shape=[_sds((B, S, dff), BF16), _sds((B, S, D), F32)] + sshapes,
        compiler_params=_cparams(("arbitrary", "arbitrary")),
    )(h2, x1, mod, target, w_up, w_down, ln_g, ln_b)


def _mlp_bwd(dr2, u, x1, x0, mix, mod, w_up, w_down, w_o, ln_g, dm):
    B, S, D = x1.shape
    tm = dm["tm_small"]
    sbw = dm["sbw"]
    nck, _, ck = w_up.shape
    dff = nck * ck

    def body(dr_ref, u_ref, x1_ref, x0_ref, mix_ref, mod_ref, wu_ref, wd_ref, wo_ref, g_ref,
             du_ref, dff_ref, dmix_ref, dx0_ref, dya_ref, dyb_ref, bst_ref, wst_ref):
        _stat_init(bst_ref, wst_ref)
        mod = mod_ref[0]
        dr2 = dr_ref[0]
        dffv = ((1.0 + mod[5:6]) * dr2).astype(BF16)
        dff_ref[0] = dffv
        dh2 = jnp.zeros((tm, D), F32)
        for c in range(nck):
            sl = slice(c * ck, (c + 1) * ck)
            da = _dot_nt(dffv, wd_ref[c])
            du = (da * (2.0 * jnp.maximum(u_ref[0, :, sl].astype(F32), 0.0))).astype(BF16)
            du_ref[0, :, sl] = du
            dh2 = dh2 + _dot_nt(du, wu_ref[c])
        x1 = x1_ref[0]
        dx1 = ALPHA * dr2 + dh2 * (1.0 + mod[4:5])
        bst_ref[0, 0:1, :] += _colsum(dh2 * x1)
        bst_ref[0, 1:2, :] += _colsum(dh2)
        mix = mix_ref[0]
        g = g_ref[...]
        _, xhat, rstd = _ln_fwd(ALPHA * x0_ref[0] + (1.0 + mod[2:3]) * mix, g, 0.0)
        dr1 = _ln_bwd(dx1, xhat, rstd, g)
        wst_ref[0:1, :] += _colsum(dx1 * xhat)
        wst_ref[1:2, :] += _colsum(dx1)
        bst_ref[0, 2:3, :] += _colsum(dr1 * mix)
        dx0_ref[0] = ALPHA * dr1
        dmix = ((1.0 + mod[2:3]) * dr1).astype(BF16)
        dmix_ref[0] = dmix
        dya_ref[0] = _dot_nt(dmix, wo_ref[0:sbw, :]).astype(BF16)
        dyb_ref[0] = _dot_nt(dmix, wo_ref[sbw:, :]).astype(BF16)

    sspecs, sshapes = _stat_specs(B, D)
    wa, wb = sbw, w_o.shape[0] - sbw
    return pl.pallas_call(
        body, name="mlp_bwd", grid=(B, S // tm),
        in_specs=[_tok(tm, D), _tok(tm, dff), _tok(tm, D), _tok(tm, D), _tok(tm, D), _perb(N_MOD, D),
                  _full(w_up), _full(w_down), _full(w_o), _full(ln_g)],
        out_specs=[_tok(tm, dff), _tok(tm, D), _tok(tm, D), _tok(tm, D), _tok(tm, wa), _tok(tm, wb)] + sspecs,
        out_shape=[_sds((B, S, dff), BF16), _sds((B, S, D), BF16), _sds((B, S, D), BF16), _sds((B, S, D), F32),
                   _sds((B, S, wa), BF16), _sds((B, S, wb), BF16)] + sshapes,
        compiler_params=_cparams(("arbitrary", "arbitrary")),
    )(dr2, u, x1, x0, mix, mod, w_up, w_down, w_o, ln_g)


def _inproj_bwd(x, x0, dx0a, mod, ln_g, dq, dk, dv, dqp, dkp, dmv, cq, ckv, w_in_p, w_uq_p, w_kv, gq, gkv,
                tc, ts1, ts2, dm):
    B, S, D = x.shape
    tm = dm["tm"]
    sbw, qr, kvr, nh = dm["sbw"], dm["qr"], dm["kvr"], dm["nh"]
    qpw = nh * HEAD_PAD
    dinp = w_in_p.shape[1]
    kvw = w_kv.shape[1]

    def body(x_ref, x0_ref, dx0a_ref, mod_ref, g_ref, dq_ref, dk_ref, dv_ref, dqp_ref, dkp_ref, dmv_ref,
             cq_ref, ckv_ref, win_ref, wuq_ref, wkv_ref, gq_ref, gkv_ref, tc_ref, ts1_ref, ts2_ref,
             gx_ref, dproj_ref, dqpre_ref, dkvo_ref, bst_ref, wst_ref):
        _stat_init(bst_ref, wst_ref)
        mod = mod_ref[0]
        c1, s1, s2 = tc_ref[...], ts1_ref[...], ts2_ref[...]
        c8, s18, s28 = jnp.tile(c1, (1, nh)), jnp.tile(s1, (1, nh)), jnp.tile(s2, (1, nh))
        dqpre = _rope_t(dqp_ref[0].astype(F32), c8, s18, s28).astype(BF16)
        dqpre_ref[0] = dqpre
        gq = gq_ref[...]
        cq = cq_ref[0]
        rq = lax.rsqrt(jnp.mean(cq * cq, axis=-1, keepdims=True) + RMS_EPS)
        dqn = _dot_nt(dqpre, wuq_ref[...])
        wst_ref[4:5, 0:qr] += _colsum(dqn * cq * rq)
        dqg = dqn * gq
        dcq = rq * dqg - cq * (rq * rq * rq) * jnp.mean(dqg * cq, axis=-1, keepdims=True)

        dkpre = _rope_t(dkp_ref[0].astype(F32), c8, s18, s28)
        dkr = dkpre[:, 0:HEAD_PAD]
        for h in range(1, nh):
            dkr = dkr + dkpre[:, h * HEAD_PAD:(h + 1) * HEAD_PAD]
        lane = lax.broadcasted_iota(jnp.int32, (tm, LANES), 1)
        dkr = jnp.where((lane >= MLA_NOPE) & (lane < MLA_NOPE + MLA_ROPE), dkr, 0.0)
        dkr = pltpu.roll(dkr, LANES - MLA_NOPE, 1)
        dkvo = jnp.concatenate([dkpre.astype(BF16), dmv_ref[0]], axis=1)
        dkvo_ref[0] = dkvo
        gkv = gkv_ref[...]
        ckv = ckv_ref[0]
        rkv = lax.rsqrt(jnp.mean(ckv * ckv, axis=-1, keepdims=True) + RMS_EPS)
        dkvn = _dot_nt(dkvo, wkv_ref[...])
        wst_ref[5:6, 0:kvr] += _colsum(dkvn * ckv * rkv)
        dkg = dkvn * gkv
        dckv = rkv * dkg - ckv * (rkv * rkv * rkv) * jnp.mean(dkg * ckv, axis=-1, keepdims=True)

        dproj = jnp.concatenate([dq_ref[0], dk_ref[0], dv_ref[0], dcq.astype(BF16), dckv.astype(BF16),
                                 dkr.astype(BF16)], axis=1)
        dproj_ref[0] = dproj
        dh = _dot_nt(dproj, win_ref[...])
        x0 = x0_ref[0]
        dx0 = dx0a_ref[0] + dh * (1.0 + mod[1:2])
        bst_ref[0, 0:1, :] += _colsum(dh * x0)
        bst_ref[0, 1:2, :] += _colsum(dh)
        g = g_ref[...]
        _, xhat, rstd = _ln_fwd(x_ref[0], g, 0.0)
        gx_ref[0] = _ln_bwd(dx0, xhat, rstd, g)
        wst_ref[0:1, :] += _colsum(dx0 * xhat)
        wst_ref[1:2, :] += _colsum(dx0)

    tab = pl.BlockSpec((tm, LANES), lambda b, s: (s, 0))
    sspecs, sshapes = _stat_specs(B, D)
    return pl.pallas_call(
        body, name="inproj_bwd", grid=(B, S // tm),
        in_specs=[_tok(tm, D), _tok(tm, D), _tok(tm, D), _perb(N_MOD, D), _full(ln_g),
                  _tok(tm, sbw), _tok(tm, sbw), _tok(tm, sbw), _tok(tm, qpw), _tok(tm, qpw), _tok(tm, nh * MLA_V),
                  _tok(tm, qr), _tok(tm, kvr), _full(w_in_p), _full(w_uq_p), _full(w_kv), _full(gq), _full(gkv),
                  tab, tab, tab],
        out_specs=[_tok(tm, D), _tok(tm, dinp), _tok(tm, qpw), _tok(tm, kvw)] + sspecs,
        out_shape=[_sds((B, S, D), F32), _sds((B, S, dinp), BF16), _sds((B, S, qpw), BF16),
                   _sds((B, S, kvw), BF16)] + sshapes,
        compiler_params=_cparams(("arbitrary", "arbitrary")),
    )(x, x0, dx0a, mod, ln_g, dq, dk, dv, dqp, dkp, dmv, cq, ckv, w_in_p, w_uq_p, w_kv, gq, gkv, tc, ts1, ts2)


def _tile_of(n, cap):
    if n <= cap:
        return n
    best = n
    for t in range(LANES, cap + 1, LANES):
        if n % t == 0:
            best = t
    return best


def _mm_tn(a, g, name, after, relu_sq=False, out_dtype=F32, col_blocks=None):
    T, K = a.shape
    N = g.shape[1]
    tt = 1024 if T % 1024 == 0 else (512 if T % 512 == 0 else T)
    tk = _tile_of(K, 1024)
    tn = _tile_of(N, 1280)
    nt = T // tt
    bw = N // col_blocks if col_blocks else tn
    assert tn % bw == 0

    def body(a_ref, g_ref, _, o_ref, acc_ref):
        @pl.when(pl.program_id(2) == 0)
        def _():
            acc_ref[...] = jnp.zeros_like(acc_ref)

        av = a_ref[...]
        if relu_sq:
            av = jnp.square(jnp.maximum(av.astype(F32), 0.0)).astype(BF16)
        acc_ref[...] += _dot_tn(av, g_ref[...])

        @pl.when(pl.program_id(2) == nt - 1)
        def _():
            if col_blocks:
                for c in range(tn // bw):
                    o_ref[c] = acc_ref[:, c * bw:(c + 1) * bw].astype(out_dtype)
            else:
                o_ref[...] = acc_ref[...].astype(out_dtype)

    if col_blocks:
        out_spec = pl.BlockSpec((tn // bw, tk, bw), lambda i, j, t: (j, i, 0))
        out_shape = _sds((col_blocks, K, bw), out_dtype)
    else:
        out_spec = pl.BlockSpec((tk, tn), lambda i, j, t: (i, j))
        out_shape = _sds((K, N), out_dtype)
    return pl.pallas_call(
        body, name=name, grid=(K // tk, N // tn, nt),
        in_specs=[pl.BlockSpec((tt, tk), lambda i, j, t: (t, i)), pl.BlockSpec((tt, tn), lambda i, j, t: (t, j)),
                  _AFTER],
        out_specs=out_spec, out_shape=out_shape,
        scratch_shapes=[pltpu.VMEM((tk, tn), F32)],
        compiler_params=_cparams(("parallel", "parallel", "arbitrary")),
    )(a, g, after)


def _reduce_adamw(parts, w, m, v, name):
    _, K, N = parts.shape
    tr = 256 if K % 256 == 0 else K

    def body(p_ref, w_ref, m_ref, v_ref, g_ref, d_ref, nm_ref, nv_ref):
        g = p_ref[0].astype(F32)
        for k in range(1, 4):
            g = g + p_ref[k].astype(F32)
        g_ref[0] = g
        d_ref[0], nm_ref[0], nv_ref[0] = _adamw(w_ref[0], g, m_ref[0], v_ref[0])

    spec = pl.BlockSpec((1, tr, N), lambda r: (0, r, 0))
    return pl.pallas_call(
        body, name=name, grid=(K // tr,),
        in_specs=[pl.BlockSpec((4, tr, N), lambda r: (0, r, 0)), spec, spec, spec],
        out_specs=[spec] * 4, out_shape=[_sds((1, K, N), F32)] * 4,
        compiler_params=_cparams(("parallel",)),
    )(parts, w, m, v)


def _finish(sm, dmod_all, dmod_my, cact_all, p_small, m_small, v_small, b_ada, m_b, v_b, w_ada, m_w, v_w):
    n0 = p_small.shape[1]
    n1 = sm.shape[1]
    d = cact_all.shape[1]

    def body(sm_ref, dma_ref, dmm_ref, ca_ref, p_ref, pm_ref, pv_ref, b_ref, bm_ref, bv_ref, w_ref, wm_ref, wv_ref,
             gs_ref, ds_ref, ms_ref, vs_ref, gb_ref, db_ref, mb_ref, vb_ref, gw_ref, dw_ref, mw_ref, vw_ref,
             loss_ref):
        gs = sm_ref[0:1, :]
        for k in range(1, N_DEV):
            gs = gs + sm_ref[k:k + 1, :]
        gs_ref[...] = gs
        ds_ref[...], ms_ref[...], vs_ref[...] = _adamw(p_ref[...], gs[:, 0:n0], pm_ref[...], pv_ref[...])
        loss_ref[...] = jnp.zeros((1, LANES), F32) + jnp.sum(gs[:, n1 - d:n1])
        gb = jnp.sum(dma_ref[...], axis=0, keepdims=True)
        gb_ref[...] = gb
        db_ref[...], mb_ref[...], vb_ref[...] = _adamw(b_ref[...], gb, bm_ref[...], bv_ref[...])
        gw = _dot_tn(ca_ref[...].astype(BF16), dmm_ref[...].astype(BF16))
        gw_ref[...] = gw
        dw_ref[...], mw_ref[...], vw_ref[...] = _adamw(w_ref[...], gw, wm_ref[...], wv_ref[...])

    s0 = _sds(p_small.shape, F32)
    sb = _sds(b_ada.shape, F32)
    sw = _sds(w_ada.shape, F32)
    return pl.pallas_call(
        body, name="finish_small",
        out_shape=[_sds((1, n1), F32), s0, s0, s0, sb, sb, sb, sb, sw, sw, sw, sw,
                   _sds((1, LANES), F32)],
        compiler_params=pltpu.CompilerParams(vmem_limit_bytes=VMEM_LIMIT),
    )(sm, dmod_all, dmod_my, cact_all, p_small, m_small, v_small, b_ada, m_b, v_b, w_ada, m_w, v_w)


def _pack(arrs, dtype, width):
    flat = jnp.concatenate([a.astype(dtype).reshape(-1) for a in arrs])
    rows = -(-flat.shape[0] // (256 * width)) * 256
    return jnp.pad(flat, (0, rows * width - flat.shape[0])).reshape(rows, width)


def _unpack(slab, shapes):
    flat = slab.reshape(-1)
    out, o = [], 0
    for s in shapes:
        n = math.prod(s)
        out.append(flat[o:o + n].reshape(s))
        o += n
    return out


def _rope_tables(S):
    inv_freq = 1.0 / (ROPE_BASE ** (jnp.arange(0, MLA_ROPE, 2, dtype=F32) / MLA_ROPE))
    ang = jnp.arange(S, dtype=F32)[:, None] * inv_freq[None, :]
    cos, sin = jnp.cos(ang), jnp.sin(ang)
    one = jnp.ones((S, MLA_NOPE), F32)
    z16 = jnp.zeros((S, 16), F32)
    z32 = jnp.zeros((S, 32), F32)
    z64 = jnp.zeros((S, MLA_NOPE), F32)
    tc = jnp.concatenate([one, cos, cos, jnp.ones((S, 32), F32)], axis=1)
    ts1 = jnp.concatenate([z64, -sin, z16, z32], axis=1)
    ts2 = jnp.concatenate([z64, z16, sin, z32], axis=1)
    return tc, ts1, ts2


def kernel(x, c, ln_in_g, ln_in_b, w_ada, b_ada, w_in, q_norm_g, kv_norm_g, w_uq, w_ukv, w_o, ln1_g, ln1_b, w_up, w_down, ln2_g, ln2_b, loss_target, m_ln_in_g, m_ln_in_b, m_w_ada, m_b_ada, m_w_in, m_q_norm_g, m_kv_norm_g, m_w_uq, m_w_ukv, m_w_o, m_ln1_g, m_ln1_b, m_w_up, m_w_down, m_ln2_g, m_ln2_b, v_ln_in_g, v_ln_in_b, v_w_ada, v_b_ada, v_w_in, v_q_norm_g, v_kv_norm_g, v_w_uq, v_w_ukv, v_w_o, v_ln1_g, v_ln1_b, v_w_up, v_w_down, v_ln2_g, v_ln2_b):
    B, S, D = x.shape
    sbw = D // 2
    mlw = D - sbw
    nh = mlw // MLA_V
    qr = w_uq.shape[1]
    kvr = w_ukv.shape[1]
    qk = MLA_NOPE + MLA_ROPE
    dff = w_up.shape[2] * N_DEV
    din = w_in.shape[2] * N_DEV
    tm = 512 if S % 512 == 0 else S
    tq = min(512, S // 2)
    dm = dict(tm=tm, tm_small=min(tm, 256), tq=tq, sbw=sbw, qr=qr, kvr=kvr, nh=nh)
    width = 1024 if D >= 1024 else LANES
    dev = 4 * lax.axis_index("x") + 2 * lax.axis_index("y") + lax.axis_index("c")

    big = [w_in, w_uq, w_ukv, w_o, w_up, w_down]
    first_w, first_token = _chip_exchange_start([a[0].astype(BF16) for a in big[:3]], "gather_w_first_start",
                                                scatter=False, after=c)

    nada = w_ada.shape[2]
    c_all = _all_gather([c + first_token[0, 0]], "gather_c")[0].reshape(N_DEV * B, D)
    b_loc = lax.dynamic_slice(b_ada, (0, dev * nada), (1, nada))
    cact_all, mod_part = _ada_partial(c_all, w_ada[0], b_loc)
    mod_all = _all_gather([mod_part], "gather_mod")[0]
    mod = lax.dynamic_slice(mod_all, (0, dev * B, 0), (N_DEV, B, nada))
    mod = jnp.swapaxes(mod, 0, 1).reshape(B, N_MOD, D)

    first_by_chip = _chip_exchange_wait(first_w, mod_all, "gather_w_first_wait")
    w_in8, w_uq8, w_ukv8 = [b.reshape((N_DEV,) + b.shape[2:]) for b in _core_gather(first_by_chip, "gather_w_first_cores")]
    late_w, late_token = _chip_exchange_start([a[0].astype(BF16) for a in big[3:]], "gather_w_late_start",
                                              scatter=False, after=w_in8)
    cols = lambda a8: jnp.swapaxes(a8, 0, 1).reshape(a8.shape[1], N_DEV * a8.shape[2])
    w_in_p = jnp.pad(cols(w_in8), ((0, 0), (0, LANES - MLA_ROPE)))
    zpad = jnp.zeros((qr, nh, HEAD_PAD - qk), BF16)
    w_uq_p = jnp.concatenate([cols(w_uq8).reshape(qr, nh, qk), zpad], axis=2).reshape(qr, nh * HEAD_PAD)
    w_ukv_f = cols(w_ukv8)
    w_uk = w_ukv_f[:, :nh * MLA_NOPE].reshape(kvr, nh, MLA_NOPE)
    w_uk_p = jnp.concatenate([w_uk, jnp.zeros((kvr, nh, HEAD_PAD - MLA_NOPE), BF16)], axis=2)
    w_kv = jnp.concatenate([w_uk_p.reshape(kvr, nh * HEAD_PAD), w_ukv_f[:, nh * MLA_NOPE:]], axis=1)

    tc, ts1, ts2 = _rope_tables(S)
    g_in, b_in = ln_in_g.reshape(1, D), ln_in_b.reshape(1, D)
    (x0, h, sq, sk, sv, qp, kp, mv, cq, ckv, qn, kvn) = _inproj_fwd(
        x, mod, g_in, b_in, w_in_p, w_uq_p, w_kv, q_norm_g, kv_norm_g, tc, ts1, ts2, dm, late_token)
    sb_y, sb_tot = _sb_fwd(sq, sk, sv, dm)
    mla_y, mla_lse = _mla_fwd(qp, kp, mv, dm, sb_tot)
    late_by_chip = _chip_exchange_wait(late_w, mla_lse, "gather_w_late_wait")
    w_o8, w_up8, w_down8 = [b.reshape((N_DEV,) + b.shape[2:]) for b in _core_gather(late_by_chip, "gather_w_late_cores")]
    w_o_f = w_o8.reshape(D, D)
    mix, x1, h2 = _outproj_fwd(sb_y, mla_y, x0, mod, w_o_f, ln1_g, ln1_b, dm)
    u, dr2, bst_c, wst_c = _mlp_fwd(h2, x1, mod, loss_target, w_up8, w_down8, ln2_g, ln2_b, dm)

    du, dffb, dmixb, dx0a, dsb_y, dmla_y, bst_b, wst_b = _mlp_bwd(
        dr2, u, x1, x0, mix, mod, w_up8, w_down8, w_o_f, ln1_g, dm)
    T = B * S
    r2 = lambda a: a.reshape(T, a.shape[2])
    by_core = lambda a: a.reshape((4, 2) + a.shape[1:])
    g_o = jnp.concatenate([_mm_tn(r2(sb_y), r2(dmixb), "grad_w_o_sb", dr2, out_dtype=BF16),
                           _mm_tn(r2(mla_y), r2(dmixb), "grad_w_o_mla", dr2, out_dtype=BF16)], axis=0)
    g_up8 = _mm_tn(r2(h2), r2(du), "grad_w_up", dr2, out_dtype=BF16, col_blocks=N_DEV)
    g_down = _mm_tn(r2(u), r2(dffb), "grad_w_down", dr2, relu_sq=True, out_dtype=BF16)
    early = [g_o.reshape(N_DEV, D // N_DEV, D), g_up8, g_down.reshape(N_DEV, dff // N_DEV, D)]
    early_sum = _core_scatter_sum([by_core(a) for a in early], "scatter_g_early_cores")
    early_g, early_token = _chip_exchange_start(early_sum, "scatter_g_early_start", scatter=True, after=dr2)

    dsq, dsk, dsv = _sb_bwd(sq, sk, sv, sb_tot, dsb_y, dm, early_token)
    dqp, dkp, dmv = _mla_bwd(qp, kp, mv, mla_y, mla_lse, dmla_y, dm, dsq)
    grad_x, dproj, dqpre, dkvo, bst_a, wst_a = _inproj_bwd(
        x, x0, dx0a, mod, g_in, dsq, dsk, dsv, dqp, dkp, dmv, cq, ckv, w_in_p, w_uq_p, w_kv, q_norm_g, kv_norm_g,
        tc, ts1, ts2, dm)

    dmod = jnp.concatenate([bst_a[:, 1], bst_a[:, 0], bst_b[:, 2], bst_b[:, 1], bst_b[:, 0], bst_c[:, 0]], axis=1)
    small = jnp.concatenate([wst_a[0], wst_a[1], wst_a[4, :qr], wst_a[5, :kvr], wst_b[0], wst_b[1],
                             wst_c[0], wst_c[1], wst_c[2]])
    n1 = small.shape[0]
    small_g, small_token = _chip_exchange_start([_pack([dmod, small], F32, LANES)], "gather_small_start",
                                                scatter=False, after=grad_x)
    g_in_p = _mm_tn(r2(h), r2(dproj), "grad_w_in", small_token)
    g_uq_p = _mm_tn(r2(qn), r2(dqpre), "grad_w_uq", small_token)
    g_kv = _mm_tn(r2(kvn), r2(dkvo), "grad_w_kv", small_token)
    small_by_chip = _chip_exchange_wait(small_g, g_kv, "gather_small_wait")
    both = _core_gather(small_by_chip, "gather_small_cores")[0].reshape(N_DEV, -1)
    g_uq_f = g_uq_p.reshape(qr, nh, HEAD_PAD)[:, :, :qk].reshape(qr, nh * qk)
    g_uk = g_kv[:, :nh * HEAD_PAD].reshape(kvr, nh, HEAD_PAD)[:, :, :MLA_NOPE].reshape(kvr, nh * MLA_NOPE)
    g_ukv_f = jnp.concatenate([g_uk, g_kv[:, nh * HEAD_PAD:]], axis=1)
    early_quarter = _chip_exchange_wait(early_g, g_kv, "scatter_g_early_wait")

    def by_dest_cols(a):
        k, n = a.shape[0], a.shape[1] // N_DEV
        return jnp.swapaxes(a.reshape(k, N_DEV, n), 0, 1).astype(BF16)

    last = [by_dest_cols(g_in_p[:, :din]), by_dest_cols(g_uq_f), by_dest_cols(g_ukv_f)]
    last_sum = _core_scatter_sum([by_core(a) for a in last], "scatter_g_last_cores")
    last_g, last_token = _chip_exchange_start(last_sum, "scatter_g_last_start", scatter=True, after=grad_x)
    names = ["w_in", "w_uq", "w_ukv", "w_o", "w_up", "w_down"]
    moms = [m_w_in, m_w_uq, m_w_ukv, m_w_o, m_w_up, m_w_down]
    vars_ = [v_w_in, v_w_uq, v_w_ukv, v_w_o, v_w_up, v_w_down]
    res_early = [_reduce_adamw(p, w, m, v, "adamw_" + n)
                 for p, w, m, v, n in zip(early_quarter, big[3:], moms[3:], vars_[3:], names[3:])]

    dmod_all = both[:, :B * N_MOD * D].reshape(N_DEV * B, N_MOD * D)
    sm = both[:, B * N_MOD * D:B * N_MOD * D + n1] + last_token[0, 0]
    dmod_my = lax.dynamic_slice(dmod_all, (0, dev * nada), (N_DEV * B, nada))
    row = lambda arrs: jnp.concatenate([a.reshape(1, -1) for a in arrs], axis=1)
    smalls = [ln_in_g, ln_in_b, q_norm_g, kv_norm_g, ln1_g, ln1_b, ln2_g, ln2_b]
    small_shapes = [a.shape for a in smalls]
    (gs, ds, nms, nvs, g_b, d_b, nm_b, nv_b, g_w, d_w, nm_w, nv_w, loss_v) = _finish(
        sm, dmod_all, dmod_my, cact_all, row(smalls),
        row([m_ln_in_g, m_ln_in_b, m_q_norm_g, m_kv_norm_g, m_ln1_g, m_ln1_b, m_ln2_g, m_ln2_b]),
        row([v_ln_in_g, v_ln_in_b, v_q_norm_g, v_kv_norm_g, v_ln1_g, v_ln1_b, v_ln2_g, v_ln2_b]),
        b_ada, m_b_ada, v_b_ada, w_ada[0], m_w_ada[0], v_w_ada[0])
    gsm, dsm, nmsm, nvsm = (_unpack(s, small_shapes) for s in (gs, ds, nms, nvs))
    last_quarter = _chip_exchange_wait(last_g, loss_v, "scatter_g_last_wait")
    res_last = [_reduce_adamw(p, w, m, v, "adamw_" + n)
                for p, w, m, v, n in zip(last_quarter, big[:3], moms[:3], vars_[:3], names[:3])]
    gb, db, nmb, nvb = ([r[i] for r in res_last + res_early] for i in range(4))

    def ordered(sm_l, w_l, ada_w, ada_b):
        return [sm_l[0], sm_l[1], ada_w[None], ada_b, w_l[0], sm_l[2], sm_l[3], w_l[1], w_l[2], w_l[3],
                sm_l[4], sm_l[5], w_l[4], w_l[5], sm_l[6], sm_l[7]]

    loss = loss_v[0, 0]
    return (loss, grad_x, *ordered(gsm, gb, g_w, g_b), *ordered(dsm, db, d_w, d_b),
            *ordered(nmsm, nmb, nm_w, nm_b), *ordered(nvsm, nvb, nv_w, nv_b))
```

```python
import functools
import math

import jax
import jax.numpy as jnp
from jax import lax
from jax.experimental import pallas as pl
from jax.experimental.pallas import tpu as pltpu

F32 = jnp.float32
BF16 = jnp.bfloat16

SB_HD = 64
MLA_V = 64
MLA_NOPE = 64
MLA_ROPE = 32
HEAD_PAD = 128
CHUNK = 64
ROPE_BASE = 10000.0
LN_EPS = 1e-5
RMS_EPS = 1e-6
DEPTH = 1
ALPHA = (2.0 * DEPTH) ** 0.25
N_MOD = 6
ADAM_LR = 0.001
ADAM_B1 = 0.9
ADAM_B2 = 0.999
ADAM_EPS = 1e-08
ADAM_WD = 0.01
ADAM_STEP = 10
N_DEV = 8
LANES = 128
LOG2E = 1.4426950408889634
CUMSUM_W = 256
VMEM_LIMIT = 56 * 1024 * 1024
MESH = pl.DeviceIdType.MESH


def _dot(a, b):
    return jnp.dot(a, b, preferred_element_type=F32)


def _dot_nt(a, b):
    return lax.dot_general(a, b, (((1,), (1,)), ((), ())), preferred_element_type=F32)


def _dot_tn(a, b):
    return lax.dot_general(a, b, (((0,), (0,)), ((), ())), preferred_element_type=F32)


def _cparams(sem):
    return pltpu.CompilerParams(dimension_semantics=sem, vmem_limit_bytes=VMEM_LIMIT)


def _full(a):
    nd = a.ndim
    return pl.BlockSpec(a.shape, lambda *_: (0,) * nd, pipeline_mode=pl.Buffered(1))


def _tok(tm, w):
    return pl.BlockSpec((1, tm, w), lambda b, s: (b, s, 0))


def _perb(rows, w):
    return pl.BlockSpec((1, rows, w), lambda b, s: (b, 0, 0))


def _sds(shape, dtype):
    return jax.ShapeDtypeStruct(shape, dtype)


def _ln_fwd(x, g, b):
    mu = jnp.mean(x, axis=-1, keepdims=True)
    xc = x - mu
    var = jnp.mean(xc * xc, axis=-1, keepdims=True)
    rstd = lax.rsqrt(var + LN_EPS)
    xhat = xc * rstd
    return xhat * g + b, xhat, rstd


def _ln_bwd(dy, xhat, rstd, g):
    dxh = dy * g
    m1 = jnp.mean(dxh, axis=-1, keepdims=True)
    m2 = jnp.mean(dxh * xhat, axis=-1, keepdims=True)
    return rstd * (dxh - m1 - xhat * m2)


def _colsum(a):
    return jnp.sum(a, axis=0, keepdims=True)


def _rope(x, c, s1, s2):
    w = x.shape[-1]
    return x * c + pltpu.roll(x, w - 16, 1) * s1 + pltpu.roll(x, 16, 1) * s2


def _rope_t(x, c, s1, s2):
    w = x.shape[-1]
    return x * c - pltpu.roll(x, w - 16, 1) * s1 - pltpu.roll(x, 16, 1) * s2


def _adamw(w, g, m, v):
    m = ADAM_B1 * m + (1.0 - ADAM_B1) * g
    v = ADAM_B2 * v + (1.0 - ADAM_B2) * (g * g)
    m_hat = m / (1.0 - ADAM_B1 ** ADAM_STEP)
    v_hat = v / (1.0 - ADAM_B2 ** ADAM_STEP)
    delta = -ADAM_LR * (m_hat / (jnp.sqrt(v_hat) + ADAM_EPS) + ADAM_WD * w)
    return delta, m, v


def _my_place():
    return lax.axis_index("x"), lax.axis_index("y"), lax.axis_index("c")


def _chip_peers(mx, my):
    out = []
    for j in (1, 2, 3):
        px = 1 - mx if (j >> 1) else mx
        py = 1 - my if (j & 1) else my
        out.append((px, py, 2 * px + py))
    return out


def _split_peers(everyone):
    mx, my, mc = _my_place()
    if not everyone:
        return [(px, py, mc, pk) for px, py, pk in _chip_peers(mx, my)], 2 * mx + my
    peers = []
    for j in range(1, N_DEV):
        px = 1 - mx if (j >> 2) & 1 else mx
        py = 1 - my if (j >> 1) & 1 else my
        pc = 1 - mc if j & 1 else mc
        peers.append((px, py, pc, 4 * px + 2 * py + pc))
    return peers, 4 * mx + 2 * my + mc


def _hbm_call(body, name, n_in, out_shape, sems):
    hbm = pl.BlockSpec(memory_space=pl.ANY)
    return pl.pallas_call(
        body, name=name, out_shape=out_shape,
        in_specs=[hbm] * n_in, out_specs=[hbm] * len(out_shape),
        scratch_shapes=[pltpu.SemaphoreType.DMA(s) for s in sems])


def _chip_exchange(xs, name, scatter):
    n = len(xs)

    def body(*refs):
        x_refs, o_refs = refs[:n], refs[n:2 * n]
        ssem, rsem, lsem = refs[2 * n:]
        mx, my, mc = _my_place()
        me = 2 * mx + my
        peers = _chip_peers(mx, my)

        def copy(i, j, src_slot, dst_slot):
            px, py, _ = peers[j]
            return pltpu.make_async_remote_copy(
                src_ref=x_refs[i].at[src_slot] if scatter else x_refs[i], dst_ref=o_refs[i].at[dst_slot],
                send_sem=ssem.at[i, j], recv_sem=rsem.at[i, j], device_id=(px, py, mc), device_id_type=MESH)

        local = [pltpu.make_async_copy(x_refs[i].at[me] if scatter else x_refs[i], o_refs[i].at[me], lsem.at[i])
                 for i in range(n)]
        sends = [copy(i, j, peers[j][2], me) for i in range(n) for j in range(3)]
        for cp in local + sends:
            cp.start()
        for i in range(n):
            for j in range(3):
                copy(i, j, peers[j][2], peers[j][2]).wait_recv()
        for cp in sends:
            cp.wait_send()
        for cp in local:
            cp.wait()

    out_shape = [_sds((4,) + tuple(x.shape[1:] if scatter else x.shape), x.dtype) for x in xs]
    return _hbm_call(body, name, n, out_shape, [(n, 3), (n, 3), (n,)])(*xs)


def _chip_exchange_start(xs, name, scatter, after, everyone=False):
    n = len(xs)
    npeer = N_DEV - 1 if everyone else 3
    blks = [tuple(x.shape[1:] if scatter else x.shape) for x in xs]

    def body(*refs):
        x_refs, land_refs = refs[:n], refs[n:2 * n]
        ssem, rsem = refs[2 * n + 1], refs[2 * n + 2]
        token = refs[-1]
        peers, me = _split_peers(everyone)
        for i in range(n):
            for j, (px, py, pc, slot) in enumerate(peers):
                pltpu.make_async_remote_copy(
                    src_ref=x_refs[i].at[slot] if scatter else x_refs[i], dst_ref=land_refs[i].at[me],
                    send_sem=ssem.at[npeer * i + j], recv_sem=rsem.at[npeer * i + j], device_id=(px, py, pc),
                    device_id_type=MESH).start()
        token[...] = jnp.zeros_like(token)

    hbm = pl.BlockSpec(memory_space=pltpu.HBM)
    sem = pl.BlockSpec(memory_space=pltpu.SEMAPHORE)
    lands = [lax.empty((npeer + 1,) + b, x.dtype) for b, x in zip(blks, xs)]
    res = pl.pallas_call(
        body, name=name,
        out_shape=[pltpu.SemaphoreType.DMA((npeer * n,)), pltpu.SemaphoreType.DMA((npeer * n,))]
        + [pltpu.HBM(x.shape, x.dtype) for x in xs] + [pltpu.HBM(l.shape, l.dtype) for l in lands]
        + [_sds((8, LANES), F32)],
        in_specs=[hbm] * (2 * n) + [_AFTER],
        out_specs=[sem, sem] + [hbm] * (2 * n) + [pl.BlockSpec(memory_space=pltpu.VMEM)],
        input_output_aliases={i: 2 + i for i in range(2 * n)},
        compiler_params=pltpu.CompilerParams(has_side_effects=pltpu.SideEffectType.DATAFLOW_SIDE_EFFECTING),
    )(*[pltpu.with_memory_space_constraint(a, pltpu.HBM) for a in list(xs) + lands], after)
    return dict(ssem=res[0], rsem=res[1], xs=res[2:2 + n], lands=res[2 + n:2 + 2 * n], n=n, scatter=scatter,
                everyone=everyone), res[-1]


def _chip_exchange_wait(handle, after, name):
    n, scatter, everyone = handle["n"], handle["scatter"], handle["everyone"]
    npeer = N_DEV - 1 if everyone else 3

    def body(*refs):
        x_refs, land_refs = refs[:n], refs[n:2 * n]
        ssem, rsem = refs[2 * n], refs[2 * n + 1]
        peers, _ = _split_peers(everyone)
        for i in range(n):
            for j, (px, py, pc, slot) in enumerate(peers):
                cp = pltpu.make_async_remote_copy(
                    src_ref=x_refs[i].at[slot] if scatter else x_refs[i], dst_ref=land_refs[i].at[slot],
                    send_sem=ssem.at[npeer * i + j], recv_sem=rsem.at[npeer * i + j], device_id=(px, py, pc),
                    device_id_type=MESH)
                cp.wait_send()
                cp.wait_recv()

    hbm = pl.BlockSpec(memory_space=pltpu.HBM)
    sem = pl.BlockSpec(memory_space=pltpu.SEMAPHORE)
    ops = list(handle["xs"]) + list(handle["lands"])
    res = pl.pallas_call(
        body, name=name,
        out_shape=[pltpu.HBM(a.shape, a.dtype) for a in ops],
        in_specs=[hbm] * (2 * n) + [sem, sem, pl.BlockSpec(memory_space=pl.ANY)],
        out_specs=[hbm] * (2 * n),
        input_output_aliases={i: i for i in range(2 * n)},
        compiler_params=pltpu.CompilerParams(has_side_effects=pltpu.SideEffectType.DATAFLOW_SIDE_EFFECTING),
    )(*ops, handle["ssem"], handle["rsem"], after)
    me = 2 * lax.axis_index("x") + lax.axis_index("y")
    if everyone:
        me = 2 * me + lax.axis_index("c")
    out = []
    for x, land in zip(res[:n], res[n:]):
        own = lax.dynamic_index_in_dim(x, me, 0, keepdims=False) if scatter else x
        out.append(lax.dynamic_update_index_in_dim(land, own, me, 0))
    return out


def _core_gather(xs, name):
    n = len(xs)

    def body(*refs):
        x_refs, o_refs, mine, got = refs[:n], refs[n:2 * n], refs[2 * n:3 * n], refs[3 * n:4 * n]
        lsem, ssem, rsem, osem = refs[4 * n:]
        mx, my, mc = _my_place()
        loads = [pltpu.make_async_copy(x_refs[i], mine[i], lsem.at[i]) for i in range(n)]
        for cp in loads:
            cp.start()
        sends, stores = [], []
        for i in range(n):
            loads[i].wait()
            cp = pltpu.make_async_remote_copy(
                src_ref=mine[i], dst_ref=got[i], send_sem=ssem.at[i], recv_sem=rsem.at[i],
                device_id=(mx, my, 1 - mc), device_id_type=MESH)
            cp.start()
            sends.append(cp)
            for k in range(4):
                st = pltpu.make_async_copy(mine[i].at[k], o_refs[i].at[k, mc], osem.at[i, k])
                st.start()
                stores.append(st)
        for i in range(n):
            sends[i].wait_recv()
            for k in range(4):
                st = pltpu.make_async_copy(got[i].at[k], o_refs[i].at[k, 1 - mc], osem.at[n + i, k])
                st.start()
                stores.append(st)
        for cp in sends:
            cp.wait_send()
        for st in stores:
            st.wait()

    hbm = pl.BlockSpec(memory_space=pl.ANY)
    bufs = [pltpu.VMEM(x.shape, x.dtype) for x in xs]
    return pl.pallas_call(
        body, name=name,
        out_shape=[_sds((4, 2) + tuple(x.shape[1:]), x.dtype) for x in xs],
        in_specs=[hbm] * n, out_specs=[hbm] * n,
        scratch_shapes=bufs + bufs + [pltpu.SemaphoreType.DMA((n,)), pltpu.SemaphoreType.DMA((n,)),
                                      pltpu.SemaphoreType.DMA((n,)), pltpu.SemaphoreType.DMA((2 * n, 4))],
        compiler_params=pltpu.CompilerParams(vmem_limit_bytes=VMEM_LIMIT),
    )(*xs)


def _rows_step(k):
    for r in (256, 128, 64, 32, 16, 8):
        if k % r == 0:
            return r
    return k


def _core_scatter_sum(gs, name):
    n = len(gs)

    def body(*refs):
        g_refs, o_refs = refs[:n], refs[n:2 * n]
        send, got, mine = refs[2 * n:3 * n], refs[3 * n:4 * n], refs[4 * n:5 * n]
        lsem, msem, ssem, rsem, osem = refs[5 * n:]
        mx, my, mc = _my_place()
        pairs = [(i, k) for i in range(n) for k in range(4)]
        out_loads = {(i, k): pltpu.make_async_copy(g_refs[i].at[k, 1 - mc], send[i].at[k], lsem.at[i, k])
                     for i, k in pairs}
        own_loads = {(i, k): pltpu.make_async_copy(g_refs[i].at[k, mc], mine[i].at[k], msem.at[i, k])
                     for i, k in pairs}
        for p in pairs:
            out_loads[p].start()
        for p in pairs:
            own_loads[p].start()
        sends = []
        for i in range(n):
            for k in range(4):
                out_loads[i, k].wait()
            cp = pltpu.make_async_remote_copy(
                src_ref=send[i], dst_ref=got[i], send_sem=ssem.at[i], recv_sem=rsem.at[i],
                device_id=(mx, my, 1 - mc), device_id_type=MESH)
            cp.start()
            sends.append(cp)
        stores = []
        for i in range(n):
            for k in range(4):
                own_loads[i, k].wait()
            sends[i].wait_recv()
            rows = g_refs[i].shape[2]
            step = _rows_step(rows)

            def add(r, _, i=i, step=step):
                sl = pl.ds(pl.multiple_of(r * step, step), step)
                for k in range(4):
                    mine[i][k, sl, :] = (mine[i][k, sl, :].astype(F32) + got[i][k, sl, :].astype(F32)).astype(BF16)
                return 0

            lax.fori_loop(0, rows // step, add, 0)
            st = pltpu.make_async_copy(mine[i], o_refs[i], osem.at[i])
            st.start()
            stores.append(st)
        for cp in sends:
            cp.wait_send()
        for st in stores:
            st.wait()

    hbm = pl.BlockSpec(memory_space=pl.ANY)
    blk = [(4,) + tuple(g.shape[2:]) for g in gs]
    bufs = [pltpu.VMEM(b, BF16) for b in blk]
    return pl.pallas_call(
        body, name=name,
        out_shape=[_sds(b, BF16) for b in blk],
        in_specs=[hbm] * n, out_specs=[hbm] * n,
        scratch_shapes=bufs * 3 + [pltpu.SemaphoreType.DMA((n, 4)), pltpu.SemaphoreType.DMA((n, 4)),
                                   pltpu.SemaphoreType.DMA((n,)), pltpu.SemaphoreType.DMA((n,)),
                                   pltpu.SemaphoreType.DMA((n,))],
        compiler_params=pltpu.CompilerParams(vmem_limit_bytes=VMEM_LIMIT),
    )(*gs)


def _all_gather(xs, name):
    by_chip = _chip_exchange(xs, name + "_chips", scatter=False)
    both = _core_gather(by_chip, name + "_cores")
    return [b.reshape((N_DEV,) + tuple(x.shape)) for b, x in zip(both, xs)]


def _ada_partial(c_all, w_ada_loc, b_loc):
    def body(c_ref, w_ref, b_ref, act_ref, mod_ref):
        c = c_ref[...]
        act = c * (1.0 / (1.0 + jnp.exp(-c)))
        act_ref[...] = act
        mod_ref[...] = _dot(act.astype(BF16), w_ref[...].astype(BF16)) + b_ref[...]

    nb, d = c_all.shape
    return pl.pallas_call(
        body, name="ada_partial",
        out_shape=(_sds((nb, d), F32), _sds((nb, w_ada_loc.shape[1]), F32)),
        compiler_params=pltpu.CompilerParams(vmem_limit_bytes=VMEM_LIMIT),
    )(c_all, w_ada_loc, b_loc)


_AFTER = pl.BlockSpec(memory_space=pl.ANY)


def _inproj_fwd(x, mod, ln_g, ln_b, w_in_p, w_uq_p, w_kv, gq, gkv, tc, ts1, ts2, dm, after):
    B, S, D = x.shape
    tm = dm["tm"]
    sbw, qr, kvr, nh = dm["sbw"], dm["qr"], dm["kvr"], dm["nh"]
    o_cq, o_ckv, o_kr = 3 * sbw, 3 * sbw + qr, 3 * sbw + qr + kvr
    qpw = nh * HEAD_PAD

    def body(x_ref, mod_ref, g_ref, b_ref, win_ref, wuq_ref, wkv_ref, gq_ref, gkv_ref, tc_ref, ts1_ref, ts2_ref, _,
             x0_ref, h_ref, q_ref, k_ref, v_ref, qp_ref, kp_ref, mv_ref, cq_ref, ckv_ref, qn_ref, kvn_ref):
        x0, _, _ = _ln_fwd(x_ref[0], g_ref[...], b_ref[...])
        x0_ref[0] = x0
        mod = mod_ref[0]
        h = (x0 * (1.0 + mod[1:2]) + mod[0:1]).astype(BF16)
        h_ref[0] = h
        proj = _dot(h, win_ref[...])
        q_ref[0] = (proj[:, 0:sbw] * SB_Q_SCALE).astype(BF16)
        k_ref[0] = proj[:, sbw:2 * sbw].astype(BF16)
        v_ref[0] = proj[:, 2 * sbw:3 * sbw].astype(BF16)
        cq = proj[:, o_cq:o_cq + qr]
        ckv = proj[:, o_ckv:o_ckv + kvr]
        cq_ref[0] = cq
        ckv_ref[0] = ckv
        qn = (cq * lax.rsqrt(jnp.mean(cq * cq, axis=-1, keepdims=True) + RMS_EPS) * gq_ref[...]).astype(BF16)
        kvn = (ckv * lax.rsqrt(jnp.mean(ckv * ckv, axis=-1, keepdims=True) + RMS_EPS) * gkv_ref[...]).astype(BF16)
        qn_ref[0] = qn
        kvn_ref[0] = kvn
        c1, s1, s2 = tc_ref[...], ts1_ref[...], ts2_ref[...]
        c8, s18, s28 = jnp.tile(c1, (1, nh)), jnp.tile(s1, (1, nh)), jnp.tile(s2, (1, nh))
        qp_ref[0] = (_rope(_dot(qn, wuq_ref[...]), c8, s18, s28) * MLA_Q_SCALE).astype(BF16)
        kvo = _dot(kvn, wkv_ref[...])
        kr = pltpu.roll(proj[:, o_kr:o_kr + LANES], 64, 1)
        kr = _rope(kr, c1, s1, s2)
        kp_ref[0] = (kvo[:, 0:qpw] + jnp.tile(kr, (1, nh))).astype(BF16)
        mv_ref[0] = kvo[:, qpw:].astype(BF16)

    tab = pl.BlockSpec((tm, LANES), lambda b, s: (s, 0))
    outs = [(D, F32), (D, BF16), (sbw, BF16), (sbw, BF16), (sbw, BF16), (qpw, BF16), (qpw, BF16),
            (nh * MLA_V, BF16), (qr, F32), (kvr, F32), (qr, BF16), (kvr, BF16)]
    return pl.pallas_call(
        body, name="inproj_fwd", grid=(B, S // tm),
        in_specs=[_tok(tm, D), _perb(N_MOD, D), _full(ln_g), _full(ln_b), _full(w_in_p), _full(w_uq_p),
                  _full(w_kv), _full(gq), _full(gkv), tab, tab, tab, _AFTER],
        out_specs=[_tok(tm, w) for w, _ in outs],
        out_shape=[_sds((B, S, w), t) for w, t in outs],
        compiler_params=_cparams(("parallel", "parallel")),
    )(x, mod, ln_g, ln_b, w_in_p, w_uq_p, w_kv, gq, gkv, tc, ts1, ts2, after)


def _neg_abs(x):
    sign = jnp.uint32(0x80000000)
    return lax.bitcast_convert_type(lax.bitcast_convert_type(x, jnp.uint32) | sign, F32)


SB_Q_SCALE = -(SB_HD ** -0.5) * LOG2E
MLA_Q_SCALE = (MLA_NOPE + MLA_ROPE) ** -0.5 * LOG2E


def _log2_keep(zs):
    return jnp.minimum(zs, 0.0) - jnp.log2(1.0 + jnp.exp2(_neg_abs(zs)))


def _split_dot(a, u):
    hi = a.astype(BF16)
    lo = (a - hi.astype(F32)).astype(BF16)
    return _dot(hi, u) + _dot(lo, u)


def _tri(n, rel):
    row = lax.broadcasted_iota(jnp.int32, (n, n), 0)
    col = lax.broadcasted_iota(jnp.int32, (n, n), 1)
    return rel(row, col).astype(BF16)


def _running_sum(a, tri, reverse, split):
    cs = tri.shape[0]
    n = a.shape[1] // cs
    out = [None] * n
    run = None
    for c in (reversed(range(n)) if reverse else range(n)):
        part = a[:, c * cs:(c + 1) * cs]
        loc = _split_dot(part, tri) if split else _dot(part.astype(BF16), tri)
        out[c] = loc if run is None else loc + run
        tot = jnp.sum(part, axis=1, keepdims=True)
        run = tot if run is None else run + tot
    return (out[0] if n == 1 else jnp.concatenate(out, axis=1)), run


def _tile_mask(nr, nk, r0, c0, rel):
    row = lax.broadcasted_iota(jnp.int32, (nr, nk), 0) + r0
    col = lax.broadcasted_iota(jnp.int32, (nr, nk), 1) + c0
    return rel(row, col)


def _put_rows(whole, part, r0):
    return part if r0 == 0 else jnp.concatenate([whole[:r0], part], axis=0)


def _diag_tiles(tq, split):
    half = tq // 2
    return [(0, tq, 0, half), (half, half, half, half)] if split else [(0, tq, 0, tq)]


def _sb_fwd(q, k, v, dm):
    B, S, W = q.shape
    tq = dm["tq"]
    nq = S // tq

    def body(q_ref, k_ref, v_ref, y_ref, tot_ref):
        qi = pl.program_id(2)
        q2 = q_ref[0]
        lane = lax.broadcasted_iota(jnp.int32, (tq, LANES), 1)
        qs = jnp.concatenate([jnp.where(lane < SB_HD, q2, 0), jnp.where(lane >= SB_HD, q2, 0)], axis=0).astype(BF16)
        later = _tri(min(tq, CUMSUM_W), lambda a, b: a > b)
        assert tq & (tq - 1) == 0
        strict = _tile_mask(2 * tq, tq, 0, 0, lambda t, s: s < (t & (tq - 1)))

        def block(j, carry, masked):
            acc, run = carry
            off = pl.multiple_of(j * tq, tq)
            zs = _dot_nt(qs, k_ref[0, pl.ds(off, tq), :])
            a = _log2_keep(zs)
            if masked:
                a = jnp.where(strict, a, 0.0)
            a_later, a_tot = _running_sum(a, later, reverse=True, split=True)
            w = jnp.exp2((a - zs) + a_later + run)
            if masked:
                w = jnp.where(strict, w, 0.0)
            return acc + _dot(w.astype(BF16), v_ref[0, pl.ds(off, tq), :]), run + a_tot

        carry = block(qi, (jnp.zeros((2 * tq, LANES), F32), jnp.zeros((2 * tq, 1), F32)), True)
        acc, run = lax.fori_loop(0, qi, lambda jj, c: block(qi - 1 - jj, c, False), carry)
        y_ref[0] = jnp.where(lane < SB_HD, acc[:tq], acc[tq:]).astype(BF16)
        tot_ref[0] = jnp.where(lane < SB_HD, run[:tq], run[tq:])

    qspec = pl.BlockSpec((1, tq, LANES), lambda b, hp, i: (b, i, hp))
    kspec = pl.BlockSpec((1, S, LANES), lambda b, hp, i: (b, 0, hp))
    return pl.pallas_call(
        body, name="sb_fwd", grid=(B, W // LANES, nq),
        in_specs=[qspec, kspec, kspec],
        out_specs=[qspec, qspec],
        out_shape=[_sds((B, S, W), BF16), _sds((B, S, W), F32)],
        compiler_params=_cparams(("parallel", "parallel", "arbitrary")),
    )(q, k, v)


def _sb_bwd(q, k, v, tot, dy, dm, after):
    B, S, W = q.shape
    tq = dm["tq"]
    nq = S // tq

    def body(q_ref, k_ref, v_ref, tot_ref, dy_ref, _, dq_ref, dk_ref, dv_ref, dk_acc, dv_acc):
        qi = pl.program_id(2)

        @pl.when(qi == 0)
        def _():
            dk_acc[...] = jnp.zeros_like(dk_acc)
            dv_acc[...] = jnp.zeros_like(dv_acc)

        q2 = q_ref[0]
        dy2 = dy_ref[0]
        tot2 = tot_ref[0]
        lane = lax.broadcasted_iota(jnp.int32, (tq, LANES), 1)
        in_h = [lane < SB_HD, lane >= SB_HD]
        qh = [jnp.where(m, q2, 0).astype(BF16) for m in in_h]
        dyh = [jnp.where(m, dy2, 0).astype(BF16) for m in in_h]
        toth = [tot2[:, 0:1], tot2[:, SB_HD:SB_HD + 1]]

        def tile(j, carry, r0, nr, c0, nk, masked):
            off = pl.multiple_of(j * tq + c0, math.gcd(tq, c0))
            k2 = k_ref[0, pl.ds(off, nk), :]
            v2 = v_ref[0, pl.ds(off, nk), :]
            upto = _tri(min(nk, CUMSUM_W), lambda a, b: a <= b)
            before = _tri(min(nk, CUMSUM_W), lambda a, b: a < b)
            strict = _tile_mask(nr, nk, r0, c0, lambda t, s: s < t) if masked else None
            rows = slice(r0, r0 + nr)
            new = []
            dk_blk = jnp.zeros((nk, LANES), F32)
            dv_blk = jnp.zeros((nk, LANES), F32)
            for h in range(2):
                dq, pa, pg = carry[3 * h][rows], carry[3 * h + 1][rows], carry[3 * h + 2][rows]
                zs = _dot_nt(qh[h][rows], k2)
                a = _log2_keep(zs)
                if masked:
                    a = jnp.where(strict, a, 0.0)
                a_upto, a_tot = _running_sum(a, upto, reverse=False, split=True)
                w = jnp.exp2((a - zs) + ((toth[h][rows] - pa) - a_upto))
                if masked:
                    w = jnp.where(strict, w, 0.0)
                g = _dot_nt(dyh[h][rows], v2) * w
                g_before, g_tot = _running_sum(g, before, reverse=False, split=False)
                g_before = g_before + pg
                dz = (g + g_before) * jnp.exp2(a) - g_before
                if masked:
                    dz = jnp.where(strict, dz, 0.0)
                dzb = dz.astype(BF16)
                dv_blk = dv_blk + _dot_tn(w.astype(BF16), dyh[h][rows])
                dk_blk = dk_blk + _dot_tn(dzb, qh[h][rows])
                new += [_put_rows(carry[3 * h], dq + _dot(dzb, k2), r0), _put_rows(carry[3 * h + 1], pa + a_tot, r0),
                        _put_rows(carry[3 * h + 2], pg + g_tot, r0)]
            dk_acc[pl.ds(off, nk), :] += dk_blk
            dv_acc[pl.ds(off, nk), :] += dv_blk
            return tuple(new)

        zero = jnp.zeros((tq, LANES), F32)
        zrun = jnp.zeros((tq, 1), F32)
        carry = lax.fori_loop(0, qi, lambda j, c: tile(j, c, 0, tq, 0, tq, False),
                              (zero, zrun, zrun, zero, zrun, zrun))
        for r0, nr, c0, nk in _diag_tiles(tq, False):
            carry = tile(qi, carry, r0, nr, c0, nk, True)
        dq_ref[0] = (jnp.where(in_h[0], carry[0], carry[3]) * (SB_HD ** -0.5)).astype(BF16)

        @pl.when(qi == nq - 1)
        def _():
            dk_ref[0] = (dk_acc[...] * (-1.0 / LOG2E)).astype(BF16)
            dv_ref[0] = dv_acc[...].astype(BF16)

    qspec = pl.BlockSpec((1, tq, LANES), lambda b, hp, i: (b, i, hp))
    kspec = pl.BlockSpec((1, S, LANES), lambda b, hp, i: (b, 0, hp))
    return pl.pallas_call(
        body, name="sb_bwd", grid=(B, W // LANES, nq),
        in_specs=[qspec, kspec, kspec, qspec, qspec, _AFTER],
        out_specs=[qspec, kspec, kspec],
        out_shape=[_sds((B, S, W), BF16)] * 3,
        scratch_shapes=[pltpu.VMEM((S, LANES), F32), pltpu.VMEM((S, LANES), F32)],
        compiler_params=_cparams(("parallel", "parallel", "arbitrary")),
    )(q, k, v, tot, dy, after)


def _same_or_earlier_chunk(row, col):
    return lax.shift_right_logical(col, 6) <= lax.shift_right_logical(row, 6)


def _mla_fwd(qp, kp, mv, dm, after):
    B, S, QW = qp.shape
    VW = mv.shape[2]
    tq = dm["tq"]
    nq = S // tq
    scale = (MLA_NOPE + MLA_ROPE) ** -0.5
    assert CHUNK == 64

    def body(q_ref, k_ref, v_ref, _, y_ref, lse_ref):
        qi = pl.program_id(2)
        q2 = q_ref[0]
        lane = lax.broadcasted_iota(jnp.int32, (tq, LANES), 1)

        def tile(j, carry, r0, nr, c0, nk, masked):
            off = pl.multiple_of(j * tq + c0, math.gcd(tq, c0))
            v2 = v_ref[0, pl.ds(off, nk), :]
            allowed = _tile_mask(nr, nk, r0, c0, _same_or_earlier_chunk) if masked else None
            rows = slice(r0, r0 + nr)
            heads = range(2)
            sl = [slice(h * HEAD_PAD, (h + 1) * HEAD_PAD) for h in heads]
            m_old = [carry[3 * h + 1][rows] for h in heads]
            s = [_dot_nt(q2[rows, sl[h]], k_ref[0, pl.ds(off, nk), sl[h]]) for h in heads]
            if masked:
                s = [jnp.where(allowed, s[h], -1e30) for h in heads]
            m_new = [jnp.maximum(m_old[h], jnp.max(s[h], axis=1, keepdims=True)) for h in heads]
            alpha = [jnp.exp2(m_old[h] - m_new[h]) for h in heads]
            p = [jnp.exp2(s[h] - m_new[h]) for h in heads]
            acc = [alpha[h] * carry[3 * h][rows] + _dot(p[h].astype(BF16), v2) for h in heads]
            l = [alpha[h] * carry[3 * h + 2][rows] + jnp.sum(p[h], axis=1, keepdims=True) for h in heads]
            out = []
            for h in heads:
                out += [_put_rows(carry[3 * h], acc[h], r0), _put_rows(carry[3 * h + 1], m_new[h], r0),
                        _put_rows(carry[3 * h + 2], l[h], r0)]
            return tuple(out)

        zero = jnp.zeros((tq, LANES), F32)
        m0 = jnp.full((tq, 1), -1e30, F32)
        l0 = jnp.zeros((tq, 1), F32)
        carry = (zero, m0, l0, zero, m0, l0)
        for r0, nr, c0, nk in _diag_tiles(tq, False):
            carry = tile(qi, carry, r0, nr, c0, nk, True)
        carry = lax.fori_loop(0, qi, lambda j, c: tile(j, c, 0, tq, 0, tq, False), carry)
        y0 = carry[0] / carry[2]
        y1 = carry[3] / carry[5]
        y_ref[0] = jnp.where(lane < MLA_V, y0, y1).astype(BF16)
        lse_ref[0] = jnp.where(lane < MLA_V, carry[1] + jnp.log2(carry[2]), carry[4] + jnp.log2(carry[5]))

    qspec = pl.BlockSpec((1, tq, 2 * HEAD_PAD), lambda b, hp, i: (b, i, hp))
    kspec = pl.BlockSpec((1, S, 2 * HEAD_PAD), lambda b, hp, i: (b, 0, hp))
    vspec = pl.BlockSpec((1, S, LANES), lambda b, hp, i: (b, 0, hp))
    yspec = pl.BlockSpec((1, tq, LANES), lambda b, hp, i: (b, i, hp))
    return pl.pallas_call(
        body, name="mla_fwd", grid=(B, VW // LANES, nq),
        in_specs=[qspec, kspec, vspec, _AFTER],
        out_specs=[yspec, yspec],
        out_shape=[_sds((B, S, VW), BF16), _sds((B, S, VW), F32)],
        compiler_params=_cparams(("parallel", "parallel", "arbitrary")),
    )(qp, kp, mv, after)


def _mla_bwd(qp, kp, mv, y, lse, dy, dm, after):
    B, S, QW = qp.shape
    VW = mv.shape[2]
    tq = dm["tq"]
    nq = S // tq
    scale = (MLA_NOPE + MLA_ROPE) ** -0.5

    def body(q_ref, k_ref, v_ref, y_ref, lse_ref, dy_ref, _, dq_ref, dk_ref, dv_ref, dk_acc, dv_acc):
        qi = pl.program_id(2)

        @pl.when(qi == 0)
        def _():
            dk_acc[...] = jnp.zeros_like(dk_acc)
            dv_acc[...] = jnp.zeros_like(dv_acc)

        q2 = q_ref[0]
        dy2 = dy_ref[0]
        lse2 = lse_ref[0]
        lane = lax.broadcasted_iota(jnp.int32, (tq, LANES), 1)
        in_h = [lane < MLA_V, lane >= MLA_V]
        prod = dy2.astype(F32) * y_ref[0].astype(F32)
        delta = [jnp.sum(jnp.where(m, prod, 0.0), axis=1, keepdims=True) for m in in_h]
        dyh = [jnp.where(m, dy2, 0).astype(BF16) for m in in_h]
        lseh = [lse2[:, 0:1], lse2[:, MLA_V:MLA_V + 1]]

        def tile(j, carry, r0, nr, c0, nk, masked):
            off = pl.multiple_of(j * tq + c0, math.gcd(tq, c0))
            v2 = v_ref[0, pl.ds(off, nk), :]
            allowed = _tile_mask(nr, nk, r0, c0, _same_or_earlier_chunk) if masked else None
            rows = slice(r0, r0 + nr)
            heads = range(2)
            sl = [slice(h * HEAD_PAD, (h + 1) * HEAD_PAD) for h in heads]
            qhh = [q2[rows, sl[h]] for h in heads]
            dyr = [dyh[h][rows] for h in heads]
            kh = [k_ref[0, pl.ds(off, nk), sl[h]] for h in heads]
            s = [_dot_nt(qhh[h], kh[h]) for h in heads]
            dp = [_dot_nt(dyr[h], v2) for h in heads]
            if masked:
                s = [jnp.where(allowed, s[h], -1e30) for h in heads]
            p = [jnp.exp2(s[h] - lseh[h][rows]) for h in heads]
            dv_acc[pl.ds(off, nk), :] += _dot_tn(p[0].astype(BF16), dyr[0]) + _dot_tn(p[1].astype(BF16), dyr[1])
            ds = [(p[h] * (dp[h] - delta[h][rows])).astype(BF16) for h in heads]
            for h in heads:
                dk_acc[pl.ds(off, nk), sl[h]] += _dot_tn(ds[h], qhh[h])
            return tuple(_put_rows(carry[h], carry[h][rows] + _dot(ds[h], kh[h]), r0) for h in heads)

        zero = jnp.zeros((tq, HEAD_PAD), F32)
        carry = lax.fori_loop(0, qi, lambda j, c: tile(j, c, 0, tq, 0, tq, False), (zero, zero))
        for r0, nr, c0, nk in _diag_tiles(tq, True):
            carry = tile(qi, carry, r0, nr, c0, nk, True)
        dq_ref[0] = (jnp.concatenate([carry[0], carry[1]], axis=1) * scale).astype(BF16)

        @pl.when(qi == nq - 1)
        def _():
            dk_ref[0] = (dk_acc[...] * (1.0 / LOG2E)).astype(BF16)
            dv_ref[0] = dv_acc[...].astype(BF16)

    qspec = pl.BlockSpec((1, tq, 2 * HEAD_PAD), lambda b, hp, i: (b, i, hp))
    kspec = pl.BlockSpec((1, S, 2 * HEAD_PAD), lambda b, hp, i: (b, 0, hp))
    vspec = pl.BlockSpec((1, S, LANES), lambda b, hp, i: (b, 0, hp))
    yspec = pl.BlockSpec((1, tq, LANES), lambda b, hp, i: (b, i, hp))
    return pl.pallas_call(
        body, name="mla_bwd", grid=(B, VW // LANES, nq),
        in_specs=[qspec, kspec, vspec, yspec, yspec, yspec, _AFTER],
        out_specs=[qspec, kspec, vspec],
        out_shape=[_sds((B, S, QW), BF16), _sds((B, S, QW), BF16), _sds((B, S, VW), BF16)],
        scratch_shapes=[pltpu.VMEM((S, 2 * HEAD_PAD), F32), pltpu.VMEM((S, LANES), F32)],
        compiler_params=_cparams(("parallel", "parallel", "arbitrary")),
    )(qp, kp, mv, y, lse, dy, after)


def _outproj_fwd(sb_y, mla_y, x0, mod, w_o, ln_g, ln_b, dm):
    B, S, D = x0.shape
    tm = dm["tm"]
    sbw = sb_y.shape[2]

    def body(ya_ref, yb_ref, x0_ref, mod_ref, wo_ref, g_ref, b_ref, mix_ref, x1_ref, h2_ref):
        mod = mod_ref[0]
        mix = _dot(ya_ref[0], wo_ref[0:sbw, :]) + _dot(yb_ref[0], wo_ref[sbw:, :])
        mix_ref[0] = mix
        x1, _, _ = _ln_fwd(ALPHA * x0_ref[0] + (1.0 + mod[2:3]) * mix, g_ref[...], b_ref[...])
        x1_ref[0] = x1
        h2_ref[0] = (x1 * (1.0 + mod[4:5]) + mod[3:4]).astype(BF16)

    return pl.pallas_call(
        body, name="outproj_fwd", grid=(B, S // tm),
        in_specs=[_tok(tm, sbw), _tok(tm, mla_y.shape[2]), _tok(tm, D), _perb(N_MOD, D),
                  _full(w_o), _full(ln_g), _full(ln_b)],
        out_specs=[_tok(tm, D)] * 3,
        out_shape=[_sds((B, S, D), F32), _sds((B, S, D), F32), _sds((B, S, D), BF16)],
        compiler_params=_cparams(("parallel", "parallel")),
    )(sb_y, mla_y, x0, mod, w_o, ln_g, ln_b)


def _stat_specs(B, D):
    specs = [pl.BlockSpec((1, 8, D), lambda b, s: (b, 0, 0)), pl.BlockSpec((8, D), lambda b, s: (0, 0))]
    shapes = [_sds((B, 8, D), F32), _sds((8, D), F32)]
    return specs, shapes


def _stat_init(bst_ref, wst_ref):
    @pl.when(pl.program_id(1) == 0)
    def _():
        bst_ref[...] = jnp.zeros_like(bst_ref)

    @pl.when((pl.program_id(0) == 0) & (pl.program_id(1) == 0))
    def _():
        wst_ref[...] = jnp.zeros_like(wst_ref)


def _mlp_fwd(h2, x1, mod, target, w_up, w_down, ln_g, ln_b, dm):
    B, S, D = x1.shape
    tm = dm["tm"]
    nck, _, ck = w_up.shape
    dff = nck * ck

    def body(h2_ref, x1_ref, mod_ref, t_ref, wu_ref, wd_ref, g_ref, b_ref, u_ref, dr_ref, bst_ref, wst_ref):
        _stat_init(bst_ref, wst_ref)
        mod = mod_ref[0]
        g = g_ref[...]
        h2 = h2_ref[0]
        ff = jnp.zeros((tm, D), F32)
        for c in range(nck):
            u = _dot(h2, wu_ref[c])
            u_ref[0, :, c * ck:(c + 1) * ck] = u.astype(BF16)
            act = jnp.square(jnp.maximum(u, 0.0)).astype(BF16)
            ff = ff + _dot(act, wd_ref[c])
        x2, xhat, rstd = _ln_fwd(ALPHA * x1_ref[0] + (1.0 + mod[5:6]) * ff, g, b_ref[...])
        err = x2 - t_ref[0]
        dy = err * (1.0 / D)
        dr = _ln_bwd(dy, xhat, rstd, g)
        dr_ref[0] = dr
        bst_ref[0, 0:1, :] += _colsum(dr * ff)
        wst_ref[0:1, :] += _colsum(dy * xhat)
        wst_ref[1:2, :] += _colsum(dy)
        wst_ref[2:3, :] += _colsum(err * err) * (0.5 / D)

    sspecs, sshapes = _stat_specs(B, D)
    return pl.pallas_call(
        body, name="mlp_fwd", grid=(B, S // tm),
        in_specs=[_tok(tm, D), _tok(tm, D), _perb(N_MOD, D), _tok(tm, D), _full(w_up), _full(w_down),
                  _full(ln_g), _full(ln_b)],
        out_specs=[_tok(tm, dff), _tok(tm, D)] + sspecs,
        out_shape=[_sds((B, S, dff), BF16), _sds((B, S, D), F32)] + sshapes,
        compiler_params=_cparams(("arbitrary", "arbitrary")),
    )(h2, x1, mod, target, w_up, w_down, ln_g, ln_b)


def _mlp_bwd(dr2, u, x1, x0, mix, mod, w_up, w_down, w_o, ln_g, dm):
    B, S, D = x1.shape
    tm = dm["tm_small"]
    sbw = dm["sbw"]
    nck, _, ck = w_up.shape
    dff = nck * ck

    def body(dr_ref, u_ref, x1_ref, x0_ref, mix_ref, mod_ref, wu_ref, wd_ref, wo_ref, g_ref,
             du_ref, dff_ref, dmix_ref, dx0_ref, dya_ref, dyb_ref, bst_ref, wst_ref):
        _stat_init(bst_ref, wst_ref)
        mod = mod_ref[0]
        dr2 = dr_ref[0]
        dffv = ((1.0 + mod[5:6]) * dr2).astype(BF16)
        dff_ref[0] = dffv
        dh2 = jnp.zeros((tm, D), F32)
        for c in range(nck):
            sl = slice(c * ck, (c + 1) * ck)
            da = _dot_nt(dffv, wd_ref[c])
            du = (da * (2.0 * jnp.maximum(u_ref[0, :, sl].astype(F32), 0.0))).astype(BF16)
            du_ref[0, :, sl] = du
            dh2 = dh2 + _dot_nt(du, wu_ref[c])
        x1 = x1_ref[0]
        dx1 = ALPHA * dr2 + dh2 * (1.0 + mod[4:5])
        bst_ref[0, 0:1, :] += _colsum(dh2 * x1)
        bst_ref[0, 1:2, :] += _colsum(dh2)
        mix = mix_ref[0]
        g = g_ref[...]
        _, xhat, rstd = _ln_fwd(ALPHA * x0_ref[0] + (1.0 + mod[2:3]) * mix, g, 0.0)
        dr1 = _ln_bwd(dx1, xhat, rstd, g)
        wst_ref[0:1, :] += _colsum(dx1 * xhat)
        wst_ref[1:2, :] += _colsum(dx1)
        bst_ref[0, 2:3, :] += _colsum(dr1 * mix)
        dx0_ref[0] = ALPHA * dr1
        dmix = ((1.0 + mod[2:3]) * dr1).astype(BF16)
        dmix_ref[0] = dmix
        dya_ref[0] = _dot_nt(dmix, wo_ref[0:sbw, :]).astype(BF16)
        dyb_ref[0] = _dot_nt(dmix, wo_ref[sbw:, :]).astype(BF16)

    sspecs, sshapes = _stat_specs(B, D)
    wa, wb = sbw, w_o.shape[0] - sbw
    return pl.pallas_call(
        body, name="mlp_bwd", grid=(B, S // tm),
        in_specs=[_tok(tm, D), _tok(tm, dff), _tok(tm, D), _tok(tm, D), _tok(tm, D), _perb(N_MOD, D),
                  _full(w_up), _full(w_down), _full(w_o), _full(ln_g)],
        out_specs=[_tok(tm, dff), _tok(tm, D), _tok(tm, D), _tok(tm, D), _tok(tm, wa), _tok(tm, wb)] + sspecs,
        out_shape=[_sds((B, S, dff), BF16), _sds((B, S, D), BF16), _sds((B, S, D), BF16), _sds((B, S, D), F32),
                   _sds((B, S, wa), BF16), _sds((B, S, wb), BF16)] + sshapes,
        compiler_params=_cparams(("arbitrary", "arbitrary")),
    )(dr2, u, x1, x0, mix, mod, w_up, w_down, w_o, ln_g)


def _inproj_bwd(x, x0, dx0a, mod, ln_g, dq, dk, dv, dqp, dkp, dmv, cq, ckv, w_in_p, w_uq_p, w_kv, gq, gkv,
                tc, ts1, ts2, dm):
    B, S, D = x.shape
    tm = dm["tm"]
    sbw, qr, kvr, nh = dm["sbw"], dm["qr"], dm["kvr"], dm["nh"]
    qpw = nh * HEAD_PAD
    dinp = w_in_p.shape[1]
    kvw = w_kv.shape[1]

    def body(x_ref, x0_ref, dx0a_ref, mod_ref, g_ref, dq_ref, dk_ref, dv_ref, dqp_ref, dkp_ref, dmv_ref,
             cq_ref, ckv_ref, win_ref, wuq_ref, wkv_ref, gq_ref, gkv_ref, tc_ref, ts1_ref, ts2_ref,
             gx_ref, dproj_ref, dqpre_ref, dkvo_ref, bst_ref, wst_ref):
        _stat_init(bst_ref, wst_ref)
        mod = mod_ref[0]
        c1, s1, s2 = tc_ref[...], ts1_ref[...], ts2_ref[...]
        c8, s18, s28 = jnp.tile(c1, (1, nh)), jnp.tile(s1, (1, nh)), jnp.tile(s2, (1, nh))
        dqpre = _rope_t(dqp_ref[0].astype(F32), c8, s18, s28).astype(BF16)
        dqpre_ref[0] = dqpre
        gq = gq_ref[...]
        cq = cq_ref[0]
        rq = lax.rsqrt(jnp.mean(cq * cq, axis=-1, keepdims=True) + RMS_EPS)
        dqn = _dot_nt(dqpre, wuq_ref[...])
        wst_ref[4:5, 0:qr] += _colsum(dqn * cq * rq)
        dqg = dqn * gq
        dcq = rq * dqg - cq * (rq * rq * rq) * jnp.mean(dqg * cq, axis=-1, keepdims=True)

        dkpre = _rope_t(dkp_ref[0].astype(F32), c8, s18, s28)
        dkr = dkpre[:, 0:HEAD_PAD]
        for h in range(1, nh):
            dkr = dkr + dkpre[:, h * HEAD_PAD:(h + 1) * HEAD_PAD]
        lane = lax.broadcasted_iota(jnp.int32, (tm, LANES), 1)
        dkr = jnp.where((lane >= MLA_NOPE) & (lane < MLA_NOPE + MLA_ROPE), dkr, 0.0)
        dkr = pltpu.roll(dkr, LANES - MLA_NOPE, 1)
        dkvo = jnp.concatenate([dkpre.astype(BF16), dmv_ref[0]], axis=1)
        dkvo_ref[0] = dkvo
        gkv = gkv_ref[...]
        ckv = ckv_ref[0]
        rkv = lax.rsqrt(jnp.mean(ckv * ckv, axis=-1, keepdims=True) + RMS_EPS)
        dkvn = _dot_nt(dkvo, wkv_ref[...])
        wst_ref[5:6, 0:kvr] += _colsum(dkvn * ckv * rkv)
        dkg = dkvn * gkv
        dckv = rkv * dkg - ckv * (rkv * rkv * rkv) * jnp.mean(dkg * ckv, axis=-1, keepdims=True)

        dproj = jnp.concatenate([dq_ref[0], dk_ref[0], dv_ref[0], dcq.astype(BF16), dckv.astype(BF16),
                                 dkr.astype(BF16)], axis=1)
        dproj_ref[0] = dproj
        dh = _dot_nt(dproj, win_ref[...])
        x0 = x0_ref[0]
        dx0 = dx0a_ref[0] + dh * (1.0 + mod[1:2])
        bst_ref[0, 0:1, :] += _colsum(dh * x0)
        bst_ref[0, 1:2, :] += _colsum(dh)
        g = g_ref[...]
        _, xhat, rstd = _ln_fwd(x_ref[0], g, 0.0)
        gx_ref[0] = _ln_bwd(dx0, xhat, rstd, g)
        wst_ref[0:1, :] += _colsum(dx0 * xhat)
        wst_ref[1:2, :] += _colsum(dx0)

    tab = pl.BlockSpec((tm, LANES), lambda b, s: (s, 0))
    sspecs, sshapes = _stat_specs(B, D)
    return pl.pallas_call(
        body, name="inproj_bwd", grid=(B, S // tm),
        in_specs=[_tok(tm, D), _tok(tm, D), _tok(tm, D), _perb(N_MOD, D), _full(ln_g),
                  _tok(tm, sbw), _tok(tm, sbw), _tok(tm, sbw), _tok(tm, qpw), _tok(tm, qpw), _tok(tm, nh * MLA_V),
                  _tok(tm, qr), _tok(tm, kvr), _full(w_in_p), _full(w_uq_p), _full(w_kv), _full(gq), _full(gkv),
                  tab, tab, tab],
        out_specs=[_tok(tm, D), _tok(tm, dinp), _tok(tm, qpw), _tok(tm, kvw)] + sspecs,
        out_shape=[_sds((B, S, D), F32), _sds((B, S, dinp), BF16), _sds((B, S, qpw), BF16),
                   _sds((B, S, kvw), BF16)] + sshapes,
        compiler_params=_cparams(("arbitrary", "arbitrary")),
    )(x, x0, dx0a, mod, ln_g, dq, dk, dv, dqp, dkp, dmv, cq, ckv, w_in_p, w_uq_p, w_kv, gq, gkv, tc, ts1, ts2)


def _tile_of(n, cap):
    if n <= cap:
        return n
    best = n
    for t in range(LANES, cap + 1, LANES):
        if n % t == 0:
            best = t
    return best


def _mm_tn(a, g, name, after, relu_sq=False, out_dtype=F32, col_blocks=None):
    T, K = a.shape
    N = g.shape[1]
    tt = 1024 if T % 1024 == 0 else (512 if T % 512 == 0 else T)
    tk = _tile_of(K, 1024)
    tn = _tile_of(N, 1280)
    nt = T // tt
    bw = N // col_blocks if col_blocks else tn
    assert tn % bw == 0

    def body(a_ref, g_ref, _, o_ref, acc_ref):
        @pl.when(pl.program_id(2) == 0)
        def _():
            acc_ref[...] = jnp.zeros_like(acc_ref)

        av = a_ref[...]
        if relu_sq:
            av = jnp.square(jnp.maximum(av.astype(F32), 0.0)).astype(BF16)
        acc_ref[...] += _dot_tn(av, g_ref[...])

        @pl.when(pl.program_id(2) == nt - 1)
        def _():
            if col_blocks:
                for c in range(tn // bw):
                    o_ref[c] = acc_ref[:, c * bw:(c + 1) * bw].astype(out_dtype)
            else:
                o_ref[...] = acc_ref[...].astype(out_dtype)

    if col_blocks:
        out_spec = pl.BlockSpec((tn // bw, tk, bw), lambda i, j, t: (j, i, 0))
        out_shape = _sds((col_blocks, K, bw), out_dtype)
    else:
        out_spec = pl.BlockSpec((tk, tn), lambda i, j, t: (i, j))
        out_shape = _sds((K, N), out_dtype)
    return pl.pallas_call(
        body, name=name, grid=(K // tk, N // tn, nt),
        in_specs=[pl.BlockSpec((tt, tk), lambda i, j, t: (t, i)), pl.BlockSpec((tt, tn), lambda i, j, t: (t, j)),
                  _AFTER],
        out_specs=out_spec, out_shape=out_shape,
        scratch_shapes=[pltpu.VMEM((tk, tn), F32)],
        compiler_params=_cparams(("parallel", "parallel", "arbitrary")),
    )(a, g, after)


def _reduce_adamw(parts, w, m, v, name):
    P, K, N = parts.shape
    tr = 256 if K % 256 == 0 else K

    def body(p_ref, w_ref, m_ref, v_ref, g_ref, d_ref, nm_ref, nv_ref):
        g = p_ref[0].astype(F32)
        for k in range(1, P):
            g = g + p_ref[k].astype(F32)
        g_ref[0] = g
        d_ref[0], nm_ref[0], nv_ref[0] = _adamw(w_ref[0], g, m_ref[0], v_ref[0])

    spec = pl.BlockSpec((1, tr, N), lambda r: (0, r, 0))
    return pl.pallas_call(
        body, name=name, grid=(K // tr,),
        in_specs=[pl.BlockSpec((P, tr, N), lambda r: (0, r, 0)), spec, spec, spec],
        out_specs=[spec] * 4, out_shape=[_sds((1, K, N), F32)] * 4,
        compiler_params=_cparams(("parallel",)),
    )(parts, w, m, v)


def _finish(sm, dmod_all, dmod_my, cact_all, p_small, m_small, v_small, b_ada, m_b, v_b, w_ada, m_w, v_w):
    n0 = p_small.shape[1]
    n1 = sm.shape[1]
    d = cact_all.shape[1]

    def body(sm_ref, dma_ref, dmm_ref, ca_ref, p_ref, pm_ref, pv_ref, b_ref, bm_ref, bv_ref, w_ref, wm_ref, wv_ref,
             gs_ref, ds_ref, ms_ref, vs_ref, gb_ref, db_ref, mb_ref, vb_ref, gw_ref, dw_ref, mw_ref, vw_ref,
             loss_ref):
        gs = sm_ref[0:1, :]
        for k in range(1, N_DEV):
            gs = gs + sm_ref[k:k + 1, :]
        gs_ref[...] = gs
        ds_ref[...], ms_ref[...], vs_ref[...] = _adamw(p_ref[...], gs[:, 0:n0], pm_ref[...], pv_ref[...])
        loss_ref[...] = jnp.zeros((1, LANES), F32) + jnp.sum(gs[:, n1 - d:n1])
        gb = jnp.sum(dma_ref[...], axis=0, keepdims=True)
        gb_ref[...] = gb
        db_ref[...], mb_ref[...], vb_ref[...] = _adamw(b_ref[...], gb, bm_ref[...], bv_ref[...])
        gw = _dot_tn(ca_ref[...].astype(BF16), dmm_ref[...].astype(BF16))
        gw_ref[...] = gw
        dw_ref[...], mw_ref[...], vw_ref[...] = _adamw(w_ref[...], gw, wm_ref[...], wv_ref[...])

    s0 = _sds(p_small.shape, F32)
    sb = _sds(b_ada.shape, F32)
    sw = _sds(w_ada.shape, F32)
    return pl.pallas_call(
        body, name="finish_small",
        out_shape=[_sds((1, n1), F32), s0, s0, s0, sb, sb, sb, sb, sw, sw, sw, sw,
                   _sds((1, LANES), F32)],
        compiler_params=pltpu.CompilerParams(vmem_limit_bytes=VMEM_LIMIT),
    )(sm, dmod_all, dmod_my, cact_all, p_small, m_small, v_small, b_ada, m_b, v_b, w_ada, m_w, v_w)


def _pack(arrs, dtype, width):
    flat = jnp.concatenate([a.astype(dtype).reshape(-1) for a in arrs])
    rows = -(-flat.shape[0] // (256 * width)) * 256
    return jnp.pad(flat, (0, rows * width - flat.shape[0])).reshape(rows, width)


def _unpack(slab, shapes):
    flat = slab.reshape(-1)
    out, o = [], 0
    for s in shapes:
        n = math.prod(s)
        out.append(flat[o:o + n].reshape(s))
        o += n
    return out


def _rope_tables(S):
    inv_freq = 1.0 / (ROPE_BASE ** (jnp.arange(0, MLA_ROPE, 2, dtype=F32) / MLA_ROPE))
    ang = jnp.arange(S, dtype=F32)[:, None] * inv_freq[None, :]
    cos, sin = jnp.cos(ang), jnp.sin(ang)
    one = jnp.ones((S, MLA_NOPE), F32)
    z16 = jnp.zeros((S, 16), F32)
    z32 = jnp.zeros((S, 32), F32)
    z64 = jnp.zeros((S, MLA_NOPE), F32)
    tc = jnp.concatenate([one, cos, cos, jnp.ones((S, 32), F32)], axis=1)
    ts1 = jnp.concatenate([z64, -sin, z16, z32], axis=1)
    ts2 = jnp.concatenate([z64, z16, sin, z32], axis=1)
    return tc, ts1, ts2


def kernel(x, c, ln_in_g, ln_in_b, w_ada, b_ada, w_in, q_norm_g, kv_norm_g, w_uq, w_ukv, w_o, ln1_g, ln1_b, w_up, w_down, ln2_g, ln2_b, loss_target, m_ln_in_g, m_ln_in_b, m_w_ada, m_b_ada, m_w_in, m_q_norm_g, m_kv_norm_g, m_w_uq, m_w_ukv, m_w_o, m_ln1_g, m_ln1_b, m_w_up, m_w_down, m_ln2_g, m_ln2_b, v_ln_in_g, v_ln_in_b, v_w_ada, v_b_ada, v_w_in, v_q_norm_g, v_kv_norm_g, v_w_uq, v_w_ukv, v_w_o, v_ln1_g, v_ln1_b, v_w_up, v_w_down, v_ln2_g, v_ln2_b):
    B, S, D = x.shape
    sbw = D // 2
    mlw = D - sbw
    nh = mlw // MLA_V
    qr = w_uq.shape[1]
    kvr = w_ukv.shape[1]
    qk = MLA_NOPE + MLA_ROPE
    dff = w_up.shape[2] * N_DEV
    din = w_in.shape[2] * N_DEV
    tm = 512 if S % 512 == 0 else S
    tq = min(512, S // 2)
    dm = dict(tm=tm, tm_small=min(tm, 256), tq=tq, sbw=sbw, qr=qr, kvr=kvr, nh=nh)
    width = 1024 if D >= 1024 else LANES
    dev = 4 * lax.axis_index("x") + 2 * lax.axis_index("y") + lax.axis_index("c")

    big = [w_in, w_uq, w_ukv, w_o, w_up, w_down]
    first_w, first_token = _chip_exchange_start([a[0].astype(BF16) for a in big[:3]], "gather_w_first_start",
                                                scatter=False, after=c)

    nada = w_ada.shape[2]
    c_all = _all_gather([c + first_token[0, 0]], "gather_c")[0].reshape(N_DEV * B, D)
    b_loc = lax.dynamic_slice(b_ada, (0, dev * nada), (1, nada))
    cact_all, mod_part = _ada_partial(c_all, w_ada[0], b_loc)
    mod_all = _all_gather([mod_part], "gather_mod")[0]
    mod = lax.dynamic_slice(mod_all, (0, dev * B, 0), (N_DEV, B, nada))
    mod = jnp.swapaxes(mod, 0, 1).reshape(B, N_MOD, D)

    first_by_chip = _chip_exchange_wait(first_w, mod_all, "gather_w_first_wait")
    w_in8, w_uq8, w_ukv8 = [b.reshape((N_DEV,) + b.shape[2:]) for b in _core_gather(first_by_chip, "gather_w_first_cores")]
    late_w, late_token = _chip_exchange_start([a[0].astype(BF16) for a in big[3:]], "gather_w_late_start",
                                              scatter=False, after=w_in8, everyone=True)
    cols = lambda a8: jnp.swapaxes(a8, 0, 1).reshape(a8.shape[1], N_DEV * a8.shape[2])
    w_in_p = jnp.pad(cols(w_in8), ((0, 0), (0, LANES - MLA_ROPE)))
    zpad = jnp.zeros((qr, nh, HEAD_PAD - qk), BF16)
    w_uq_p = jnp.concatenate([cols(w_uq8).reshape(qr, nh, qk), zpad], axis=2).reshape(qr, nh * HEAD_PAD)
    w_ukv_f = cols(w_ukv8)
    w_uk = w_ukv_f[:, :nh * MLA_NOPE].reshape(kvr, nh, MLA_NOPE)
    w_uk_p = jnp.concatenate([w_uk, jnp.zeros((kvr, nh, HEAD_PAD - MLA_NOPE), BF16)], axis=2)
    w_kv = jnp.concatenate([w_uk_p.reshape(kvr, nh * HEAD_PAD), w_ukv_f[:, nh * MLA_NOPE:]], axis=1)

    tc, ts1, ts2 = _rope_tables(S)
    g_in, b_in = ln_in_g.reshape(1, D), ln_in_b.reshape(1, D)
    (x0, h, sq, sk, sv, qp, kp, mv, cq, ckv, qn, kvn) = _inproj_fwd(
        x, mod, g_in, b_in, w_in_p, w_uq_p, w_kv, q_norm_g, kv_norm_g, tc, ts1, ts2, dm, late_token)
    sb_y, sb_tot = _sb_fwd(sq, sk, sv, dm)
    mla_y, mla_lse = _mla_fwd(qp, kp, mv, dm, sb_tot)
    w_o8, w_up8, w_down8 = _chip_exchange_wait(late_w, mla_lse, "gather_w_late_wait")
    w_o_f = w_o8.reshape(D, D)
    mix, x1, h2 = _outproj_fwd(sb_y, mla_y, x0, mod, w_o_f, ln1_g, ln1_b, dm)
    u, dr2, bst_c, wst_c = _mlp_fwd(h2, x1, mod, loss_target, w_up8, w_down8, ln2_g, ln2_b, dm)

    du, dffb, dmixb, dx0a, dsb_y, dmla_y, bst_b, wst_b = _mlp_bwd(
        dr2, u, x1, x0, mix, mod, w_up8, w_down8, w_o_f, ln1_g, dm)
    T = B * S
    r2 = lambda a: a.reshape(T, a.shape[2])
    by_core = lambda a: a.reshape((4, 2) + a.shape[1:])
    g_o = jnp.concatenate([_mm_tn(r2(sb_y), r2(dmixb), "grad_w_o_sb", dr2, out_dtype=BF16),
                           _mm_tn(r2(mla_y), r2(dmixb), "grad_w_o_mla", dr2, out_dtype=BF16)], axis=0)
    g_up8 = _mm_tn(r2(h2), r2(du), "grad_w_up", dr2, out_dtype=BF16, col_blocks=N_DEV)
    g_down = _mm_tn(r2(u), r2(dffb), "grad_w_down", dr2, relu_sq=True, out_dtype=BF16)
    early = [g_o.reshape(N_DEV, D // N_DEV, D), g_up8, g_down.reshape(N_DEV, dff // N_DEV, D)]
    early_g, early_token = _chip_exchange_start(early, "scatter_g_early_start", scatter=True, after=dr2,
                                                everyone=True)

    dsq, dsk, dsv = _sb_bwd(sq, sk, sv, sb_tot, dsb_y, dm, early_token)
    dqp, dkp, dmv = _mla_bwd(qp, kp, mv, mla_y, mla_lse, dmla_y, dm, dsq)
    grad_x, dproj, dqpre, dkvo, bst_a, wst_a = _inproj_bwd(
        x, x0, dx0a, mod, g_in, dsq, dsk, dsv, dqp, dkp, dmv, cq, ckv, w_in_p, w_uq_p, w_kv, q_norm_g, kv_norm_g,
        tc, ts1, ts2, dm)

    dmod = jnp.concatenate([bst_a[:, 1], bst_a[:, 0], bst_b[:, 2], bst_b[:, 1], bst_b[:, 0], bst_c[:, 0]], axis=1)
    small = jnp.concatenate([wst_a[0], wst_a[1], wst_a[4, :qr], wst_a[5, :kvr], wst_b[0], wst_b[1],
                             wst_c[0], wst_c[1], wst_c[2]])
    n1 = small.shape[0]
    small_g, small_token = _chip_exchange_start([_pack([dmod, small], F32, LANES)], "gather_small_start",
                                                scatter=False, after=grad_x)
    g_in_p = _mm_tn(r2(h), r2(dproj), "grad_w_in", small_token)
    g_uq_p = _mm_tn(r2(qn), r2(dqpre), "grad_w_uq", small_token)
    g_kv = _mm_tn(r2(kvn), r2(dkvo), "grad_w_kv", small_token)
    small_by_chip = _chip_exchange_wait(small_g, g_kv, "gather_small_wait")
    both = _core_gather(small_by_chip, "gather_small_cores")[0].reshape(N_DEV, -1)
    g_uq_f = g_uq_p.reshape(qr, nh, HEAD_PAD)[:, :, :qk].reshape(qr, nh * qk)
    g_uk = g_kv[:, :nh * HEAD_PAD].reshape(kvr, nh, HEAD_PAD)[:, :, :MLA_NOPE].reshape(kvr, nh * MLA_NOPE)
    g_ukv_f = jnp.concatenate([g_uk, g_kv[:, nh * HEAD_PAD:]], axis=1)
    early_quarter = _chip_exchange_wait(early_g, g_kv, "scatter_g_early_wait")

    def by_dest_cols(a):
        k, n = a.shape[0], a.shape[1] // N_DEV
        return jnp.swapaxes(a.reshape(k, N_DEV, n), 0, 1).astype(BF16)

    last = [by_dest_cols(g_in_p[:, :din]), by_dest_cols(g_uq_f), by_dest_cols(g_ukv_f)]
    last_sum = _core_scatter_sum([by_core(a) for a in last], "scatter_g_last_cores")
    last_g, last_token = _chip_exchange_start(last_sum, "scatter_g_last_start", scatter=True, after=grad_x)
    names = ["w_in", "w_uq", "w_ukv", "w_o", "w_up", "w_down"]
    moms = [m_w_in, m_w_uq, m_w_ukv, m_w_o, m_w_up, m_w_down]
    vars_ = [v_w_in, v_w_uq, v_w_ukv, v_w_o, v_w_up, v_w_down]
    res_early = [_reduce_adamw(p, w, m, v, "adamw_" + n)
                 for p, w, m, v, n in zip(early_quarter, big[3:], moms[3:], vars_[3:], names[3:])]

    dmod_all = both[:, :B * N_MOD * D].reshape(N_DEV * B, N_MOD * D)
    sm = both[:, B * N_MOD * D:B * N_MOD * D + n1] + last_token[0, 0]
    dmod_my = lax.dynamic_slice(dmod_all, (0, dev * nada), (N_DEV * B, nada))
    row = lambda arrs: jnp.concatenate([a.reshape(1, -1) for a in arrs], axis=1)
    smalls = [ln_in_g, ln_in_b, q_norm_g, kv_norm_g, ln1_g, ln1_b, ln2_g, ln2_b]
    small_shapes = [a.shape for a in smalls]
    (gs, ds, nms, nvs, g_b, d_b, nm_b, nv_b, g_w, d_w, nm_w, nv_w, loss_v) = _finish(
        sm, dmod_all, dmod_my, cact_all, row(smalls),
        row([m_ln_in_g, m_ln_in_b, m_q_norm_g, m_kv_norm_g, m_ln1_g, m_ln1_b, m_ln2_g, m_ln2_b]),
        row([v_ln_in_g, v_ln_in_b, v_q_norm_g, v_kv_norm_g, v_ln1_g, v_ln1_b, v_ln2_g, v_ln2_b]),
        b_ada, m_b_ada, v_b_ada, w_ada[0], m_w_ada[0], v_w_ada[0])
    gsm, dsm, nmsm, nvsm = (_unpack(s, small_shapes) for s in (gs, ds, nms, nvs))
    last_quarter = _chip_exchange_wait(last_g, loss_v, "scatter_g_last_wait")
    res_last = [_reduce_adamw(p, w, m, v, "adamw_" + n)
                for p, w, m, v, n in zip(last_quarter, big[:3], moms[:3], vars_[:3], names[:3])]
    gb, db, nmb, nvb = ([r[i] for r in res_last + res_early] for i in range(4))

    def ordered(sm_l, w_l, ada_w, ada_b):
        return [sm_l[0], sm_l[1], ada_w[None], ada_b, w_l[0], sm_l[2], sm_l[3], w_l[1], w_l[2], w_l[3],
                sm_l[4], sm_l[5], w_l[4], w_l[5], sm_l[6], sm_l[7]]

    loss = loss_v[0, 0]
    return (loss, grad_x, *ordered(gsm, gb, g_w, g_b), *ordered(dsm, db, d_w, d_b),
            *ordered(nmsm, nmb, nm_w, nm_b), *ordered(nvsm, nvb, nv_w, nv_b))
```

```python
import math

import jax
import jax.numpy as jnp
from jax import lax
from jax.experimental import pallas as pl
from jax.experimental.pallas import tpu as pltpu

F32 = jnp.float32
BF16 = jnp.bfloat16

SB_HD = 64
MLA_V = 64
MLA_NOPE = 64
MLA_ROPE = 32
HEAD_PAD = 128
CHUNK = 64
ROPE_BASE = 10000.0
LN_EPS = 1e-5
RMS_EPS = 1e-6
DEPTH = 1
ALPHA = (2.0 * DEPTH) ** 0.25
N_MOD = 6
ADAM_LR = 0.001
ADAM_B1 = 0.9
ADAM_B2 = 0.999
ADAM_EPS = 1e-08
ADAM_WD = 0.01
ADAM_STEP = 10
N_DEV = 8
LANES = 128
LOG2E = 1.4426950408889634
CUMSUM_W = 256
VMEM_LIMIT = 56 * 1024 * 1024
MESH = pl.DeviceIdType.MESH


def _dot(a, b):
    return jnp.dot(a, b, preferred_element_type=F32)


def _dot_nt(a, b):
    return lax.dot_general(a, b, (((1,), (1,)), ((), ())), preferred_element_type=F32)


def _dot_tn(a, b):
    return lax.dot_general(a, b, (((0,), (0,)), ((), ())), preferred_element_type=F32)


def _cparams(sem):
    return pltpu.CompilerParams(dimension_semantics=sem, vmem_limit_bytes=VMEM_LIMIT)


def _full(a):
    nd = a.ndim
    return pl.BlockSpec(a.shape, lambda *_: (0,) * nd, pipeline_mode=pl.Buffered(1))


def _tok(tm, w):
    return pl.BlockSpec((1, tm, w), lambda b, s: (b, s, 0))


def _perb(rows, w):
    return pl.BlockSpec((1, rows, w), lambda b, s: (b, 0, 0))


def _sds(shape, dtype):
    return jax.ShapeDtypeStruct(shape, dtype)


def _ln_fwd(x, g, b):
    mu = jnp.mean(x, axis=-1, keepdims=True)
    xc = x - mu
    var = jnp.mean(xc * xc, axis=-1, keepdims=True)
    rstd = lax.rsqrt(var + LN_EPS)
    xhat = xc * rstd
    return xhat * g + b, xhat, rstd


def _ln_bwd(dy, xhat, rstd, g):
    dxh = dy * g
    m1 = jnp.mean(dxh, axis=-1, keepdims=True)
    m2 = jnp.mean(dxh * xhat, axis=-1, keepdims=True)
    return rstd * (dxh - m1 - xhat * m2)


def _colsum(a):
    return jnp.sum(a, axis=0, keepdims=True)


def _rope(x, c, s1, s2):
    w = x.shape[-1]
    return x * c + pltpu.roll(x, w - 16, 1) * s1 + pltpu.roll(x, 16, 1) * s2


def _rope_t(x, c, s1, s2):
    w = x.shape[-1]
    return x * c - pltpu.roll(x, w - 16, 1) * s1 - pltpu.roll(x, 16, 1) * s2


def _adamw(w, g, m, v):
    m = ADAM_B1 * m + (1.0 - ADAM_B1) * g
    v = ADAM_B2 * v + (1.0 - ADAM_B2) * (g * g)
    m_hat = m / (1.0 - ADAM_B1 ** ADAM_STEP)
    v_hat = v / (1.0 - ADAM_B2 ** ADAM_STEP)
    delta = -ADAM_LR * (m_hat / (jnp.sqrt(v_hat) + ADAM_EPS) + ADAM_WD * w)
    return delta, m, v


def _my_place():
    return lax.axis_index("x"), lax.axis_index("y"), lax.axis_index("c")


def _chip_peers(mx, my):
    out = []
    for j in (1, 2, 3):
        px = 1 - mx if (j >> 1) else mx
        py = 1 - my if (j & 1) else my
        out.append((px, py, 2 * px + py))
    return out


def _split_peers(everyone):
    mx, my, mc = _my_place()
    if not everyone:
        return [(px, py, mc, pk) for px, py, pk in _chip_peers(mx, my)], 2 * mx + my
    peers = []
    for j in range(1, N_DEV):
        px = 1 - mx if (j >> 2) & 1 else mx
        py = 1 - my if (j >> 1) & 1 else my
        pc = 1 - mc if j & 1 else mc
        peers.append((px, py, pc, 4 * px + 2 * py + pc))
    return peers, 4 * mx + 2 * my + mc


def _hbm_call(body, name, n_in, out_shape, sems):
    hbm = pl.BlockSpec(memory_space=pl.ANY)
    return pl.pallas_call(
        body, name=name, out_shape=out_shape,
        in_specs=[hbm] * n_in, out_specs=[hbm] * len(out_shape),
        scratch_shapes=[pltpu.SemaphoreType.DMA(s) for s in sems])


def _chip_exchange(xs, name, scatter):
    n = len(xs)

    def body(*refs):
        x_refs, o_refs = refs[:n], refs[n:2 * n]
        ssem, rsem, lsem = refs[2 * n:]
        mx, my, mc = _my_place()
        me = 2 * mx + my
        peers = _chip_peers(mx, my)

        def copy(i, j, src_slot, dst_slot):
            px, py, _ = peers[j]
            return pltpu.make_async_remote_copy(
                src_ref=x_refs[i].at[src_slot] if scatter else x_refs[i], dst_ref=o_refs[i].at[dst_slot],
                send_sem=ssem.at[i, j], recv_sem=rsem.at[i, j], device_id=(px, py, mc), device_id_type=MESH)

        local = [pltpu.make_async_copy(x_refs[i].at[me] if scatter else x_refs[i], o_refs[i].at[me], lsem.at[i])
                 for i in range(n)]
        sends = [copy(i, j, peers[j][2], me) for i in range(n) for j in range(3)]
        for cp in local + sends:
            cp.start()
        for i in range(n):
            for j in range(3):
                copy(i, j, peers[j][2], peers[j][2]).wait_recv()
        for cp in sends:
            cp.wait_send()
        for cp in local:
            cp.wait()

    out_shape = [_sds((4,) + tuple(x.shape[1:] if scatter else x.shape), x.dtype) for x in xs]
    return _hbm_call(body, name, n, out_shape, [(n, 3), (n, 3), (n,)])(*xs)


def _chip_exchange_start(xs, name, scatter, after, everyone=False):
    n = len(xs)
    npeer = N_DEV - 1 if everyone else 3
    blks = [tuple(x.shape[1:] if scatter else x.shape) for x in xs]

    def body(*refs):
        x_refs, land_refs = refs[:n], refs[n:2 * n]
        ssem, rsem = refs[2 * n + 1], refs[2 * n + 2]
        token = refs[-1]
        peers, me = _split_peers(everyone)
        for i in range(n):
            for j, (px, py, pc, slot) in enumerate(peers):
                pltpu.make_async_remote_copy(
                    src_ref=x_refs[i].at[slot] if scatter else x_refs[i], dst_ref=land_refs[i].at[me],
                    send_sem=ssem.at[npeer * i + j], recv_sem=rsem.at[npeer * i + j], device_id=(px, py, pc),
                    device_id_type=MESH).start()
        token[...] = jnp.zeros_like(token)

    hbm = pl.BlockSpec(memory_space=pltpu.HBM)
    sem = pl.BlockSpec(memory_space=pltpu.SEMAPHORE)
    lands = [lax.empty((npeer + 1,) + b, x.dtype) for b, x in zip(blks, xs)]
    res = pl.pallas_call(
        body, name=name,
        out_shape=[pltpu.SemaphoreType.DMA((npeer * n,)), pltpu.SemaphoreType.DMA((npeer * n,))]
        + [pltpu.HBM(x.shape, x.dtype) for x in xs] + [pltpu.HBM(l.shape, l.dtype) for l in lands]
        + [_sds((8, LANES), F32)],
        in_specs=[hbm] * (2 * n) + [_AFTER],
        out_specs=[sem, sem] + [hbm] * (2 * n) + [pl.BlockSpec(memory_space=pltpu.VMEM)],
        input_output_aliases={i: 2 + i for i in range(2 * n)},
        compiler_params=pltpu.CompilerParams(has_side_effects=pltpu.SideEffectType.DATAFLOW_SIDE_EFFECTING),
    )(*[pltpu.with_memory_space_constraint(a, pltpu.HBM) for a in list(xs) + lands], after)
    return dict(ssem=res[0], rsem=res[1], xs=res[2:2 + n], lands=res[2 + n:2 + 2 * n], n=n, scatter=scatter,
                everyone=everyone), res[-1]


def _chip_exchange_wait(handle, after, name):
    n, scatter, everyone = handle["n"], handle["scatter"], handle["everyone"]
    npeer = N_DEV - 1 if everyone else 3

    def body(*refs):
        x_refs, land_refs = refs[:n], refs[n:2 * n]
        ssem, rsem = refs[2 * n], refs[2 * n + 1]
        peers, _ = _split_peers(everyone)
        for i in range(n):
            for j, (px, py, pc, slot) in enumerate(peers):
                cp = pltpu.make_async_remote_copy(
                    src_ref=x_refs[i].at[slot] if scatter else x_refs[i], dst_ref=land_refs[i].at[slot],
                    send_sem=ssem.at[npeer * i + j], recv_sem=rsem.at[npeer * i + j], device_id=(px, py, pc),
                    device_id_type=MESH)
                cp.wait_send()
                cp.wait_recv()

    hbm = pl.BlockSpec(memory_space=pltpu.HBM)
    sem = pl.BlockSpec(memory_space=pltpu.SEMAPHORE)
    ops = list(handle["xs"]) + list(handle["lands"])
    res = pl.pallas_call(
        body, name=name,
        out_shape=[pltpu.HBM(a.shape, a.dtype) for a in ops],
        in_specs=[hbm] * (2 * n) + [sem, sem, pl.BlockSpec(memory_space=pl.ANY)],
        out_specs=[hbm] * (2 * n),
        input_output_aliases={i: i for i in range(2 * n)},
        compiler_params=pltpu.CompilerParams(has_side_effects=pltpu.SideEffectType.DATAFLOW_SIDE_EFFECTING),
    )(*ops, handle["ssem"], handle["rsem"], after)
    me = 2 * lax.axis_index("x") + lax.axis_index("y")
    if everyone:
        me = 2 * me + lax.axis_index("c")
    out = []
    for x, land in zip(res[:n], res[n:]):
        own = lax.dynamic_index_in_dim(x, me, 0, keepdims=False) if scatter else x
        out.append(lax.dynamic_update_index_in_dim(land, own, me, 0))
    return out


def _core_gather(xs, name):
    n = len(xs)

    def body(*refs):
        x_refs, o_refs, mine, got = refs[:n], refs[n:2 * n], refs[2 * n:3 * n], refs[3 * n:4 * n]
        lsem, ssem, rsem, osem = refs[4 * n:]
        mx, my, mc = _my_place()
        loads = [pltpu.make_async_copy(x_refs[i], mine[i], lsem.at[i]) for i in range(n)]
        for cp in loads:
            cp.start()
        sends, stores = [], []
        for i in range(n):
            loads[i].wait()
            cp = pltpu.make_async_remote_copy(
                src_ref=mine[i], dst_ref=got[i], send_sem=ssem.at[i], recv_sem=rsem.at[i],
                device_id=(mx, my, 1 - mc), device_id_type=MESH)
            cp.start()
            sends.append(cp)
            for k in range(4):
                st = pltpu.make_async_copy(mine[i].at[k], o_refs[i].at[k, mc], osem.at[i, k])
                st.start()
                stores.append(st)
        for i in range(n):
            sends[i].wait_recv()
            for k in range(4):
                st = pltpu.make_async_copy(got[i].at[k], o_refs[i].at[k, 1 - mc], osem.at[n + i, k])
                st.start()
                stores.append(st)
        for cp in sends:
            cp.wait_send()
        for st in stores:
            st.wait()

    hbm = pl.BlockSpec(memory_space=pl.ANY)
    bufs = [pltpu.VMEM(x.shape, x.dtype) for x in xs]
    return pl.pallas_call(
        body, name=name,
        out_shape=[_sds((4, 2) + tuple(x.shape[1:]), x.dtype) for x in xs],
        in_specs=[hbm] * n, out_specs=[hbm] * n,
        scratch_shapes=bufs + bufs + [pltpu.SemaphoreType.DMA((n,)), pltpu.SemaphoreType.DMA((n,)),
                                      pltpu.SemaphoreType.DMA((n,)), pltpu.SemaphoreType.DMA((2 * n, 4))],
        compiler_params=pltpu.CompilerParams(vmem_limit_bytes=VMEM_LIMIT),
    )(*xs)


def _rows_step(k):
    for r in (256, 128, 64, 32, 16, 8):
        if k % r == 0:
            return r
    return k


def _core_scatter_sum(gs, name):
    n = len(gs)

    def body(*refs):
        g_refs, o_refs = refs[:n], refs[n:2 * n]
        send, got, mine = refs[2 * n:3 * n], refs[3 * n:4 * n], refs[4 * n:5 * n]
        lsem, msem, ssem, rsem, osem = refs[5 * n:]
        mx, my, mc = _my_place()
        pairs = [(i, k) for i in range(n) for k in range(4)]
        out_loads = {(i, k): pltpu.make_async_copy(g_refs[i].at[k, 1 - mc], send[i].at[k], lsem.at[i, k])
                     for i, k in pairs}
        own_loads = {(i, k): pltpu.make_async_copy(g_refs[i].at[k, mc], mine[i].at[k], msem.at[i, k])
                     for i, k in pairs}
        for p in pairs:
            out_loads[p].start()
        for p in pairs:
            own_loads[p].start()
        sends = []
        for i in range(n):
            for k in range(4):
                out_loads[i, k].wait()
            cp = pltpu.make_async_remote_copy(
                src_ref=send[i], dst_ref=got[i], send_sem=ssem.at[i], recv_sem=rsem.at[i],
                device_id=(mx, my, 1 - mc), device_id_type=MESH)
            cp.start()
            sends.append(cp)
        stores = []
        for i in range(n):
            for k in range(4):
                own_loads[i, k].wait()
            sends[i].wait_recv()
            rows = g_refs[i].shape[2]
            step = _rows_step(rows)

            def add(r, _, i=i, step=step):
                sl = pl.ds(pl.multiple_of(r * step, step), step)
                for k in range(4):
                    mine[i][k, sl, :] = (mine[i][k, sl, :].astype(F32) + got[i][k, sl, :].astype(F32)).astype(BF16)
                return 0

            lax.fori_loop(0, rows // step, add, 0)
            st = pltpu.make_async_copy(mine[i], o_refs[i], osem.at[i])
            st.start()
            stores.append(st)
        for cp in sends:
            cp.wait_send()
        for st in stores:
            st.wait()

    hbm = pl.BlockSpec(memory_space=pl.ANY)
    blk = [(4,) + tuple(g.shape[2:]) for g in gs]
    bufs = [pltpu.VMEM(b, BF16) for b in blk]
    return pl.pallas_call(
        body, name=name,
        out_shape=[_sds(b, BF16) for b in blk],
        in_specs=[hbm] * n, out_specs=[hbm] * n,
        scratch_shapes=bufs * 3 + [pltpu.SemaphoreType.DMA((n, 4)), pltpu.SemaphoreType.DMA((n, 4)),
                                   pltpu.SemaphoreType.DMA((n,)), pltpu.SemaphoreType.DMA((n,)),
                                   pltpu.SemaphoreType.DMA((n,))],
        compiler_params=pltpu.CompilerParams(vmem_limit_bytes=VMEM_LIMIT),
    )(*gs)


def _all_gather(xs, name):
    by_chip = _chip_exchange(xs, name + "_chips", scatter=False)
    both = _core_gather(by_chip, name + "_cores")
    return [b.reshape((N_DEV,) + tuple(x.shape)) for b, x in zip(both, xs)]


def _ada_partial(c_all, w_ada_loc, b_loc):
    def body(c_ref, w_ref, b_ref, act_ref, mod_ref):
        c = c_ref[...]
        act = c * (1.0 / (1.0 + jnp.exp(-c)))
        act_ref[...] = act
        mod_ref[...] = _dot(act.astype(BF16), w_ref[...].astype(BF16)) + b_ref[...]

    nb, d = c_all.shape
    return pl.pallas_call(
        body, name="ada_partial",
        out_shape=(_sds((nb, d), F32), _sds((nb, w_ada_loc.shape[1]), F32)),
        compiler_params=pltpu.CompilerParams(vmem_limit_bytes=VMEM_LIMIT),
    )(c_all, w_ada_loc, b_loc)


_AFTER = pl.BlockSpec(memory_space=pl.ANY)


def _inproj_fwd(x, mod, ln_g, ln_b, w_in_p, w_uq_p, w_kv, gq, gkv, tc, ts1, ts2, dm, after):
    B, S, D = x.shape
    tm = dm["tm"]
    sbw, qr, kvr, nh = dm["sbw"], dm["qr"], dm["kvr"], dm["nh"]
    o_cq, o_ckv, o_kr = 3 * sbw, 3 * sbw + qr, 3 * sbw + qr + kvr
    qpw = nh * HEAD_PAD

    def body(x_ref, mod_ref, g_ref, b_ref, win_ref, wuq_ref, wkv_ref, gq_ref, gkv_ref, tc_ref, ts1_ref, ts2_ref, _,
             x0_ref, h_ref, q_ref, k_ref, v_ref, qp_ref, kp_ref, mv_ref, cq_ref, ckv_ref, qn_ref, kvn_ref):
        x0, _, _ = _ln_fwd(x_ref[0], g_ref[...], b_ref[...])
        x0_ref[0] = x0
        mod = mod_ref[0]
        h = (x0 * (1.0 + mod[1:2]) + mod[0:1]).astype(BF16)
        h_ref[0] = h
        proj = _dot(h, win_ref[...])
        q_ref[0] = (proj[:, 0:sbw] * SB_Q_SCALE).astype(BF16)
        k_ref[0] = proj[:, sbw:2 * sbw].astype(BF16)
        v_ref[0] = proj[:, 2 * sbw:3 * sbw].astype(BF16)
        cq = proj[:, o_cq:o_cq + qr]
        ckv = proj[:, o_ckv:o_ckv + kvr]
        cq_ref[0] = cq
        ckv_ref[0] = ckv
        qn = (cq * lax.rsqrt(jnp.mean(cq * cq, axis=-1, keepdims=True) + RMS_EPS) * gq_ref[...]).astype(BF16)
        kvn = (ckv * lax.rsqrt(jnp.mean(ckv * ckv, axis=-1, keepdims=True) + RMS_EPS) * gkv_ref[...]).astype(BF16)
        qn_ref[0] = qn
        kvn_ref[0] = kvn
        c1, s1, s2 = tc_ref[...], ts1_ref[...], ts2_ref[...]
        c8, s18, s28 = jnp.tile(c1, (1, nh)), jnp.tile(s1, (1, nh)), jnp.tile(s2, (1, nh))
        qp_ref[0] = (_rope(_dot(qn, wuq_ref[...]), c8, s18, s28) * MLA_Q_SCALE).astype(BF16)
        kvo = _dot(kvn, wkv_ref[...])
        kr = pltpu.roll(proj[:, o_kr:o_kr + LANES], 64, 1)
        kr = _rope(kr, c1, s1, s2)
        kp_ref[0] = (kvo[:, 0:qpw] + jnp.tile(kr, (1, nh))).astype(BF16)
        mv_ref[0] = kvo[:, qpw:].astype(BF16)

    tab = pl.BlockSpec((tm, LANES), lambda b, s: (s, 0))
    outs = [(D, F32), (D, BF16), (sbw, BF16), (sbw, BF16), (sbw, BF16), (qpw, BF16), (qpw, BF16),
            (nh * MLA_V, BF16), (qr, F32), (kvr, F32), (qr, BF16), (kvr, BF16)]
    return pl.pallas_call(
        body, name="inproj_fwd", grid=(B, S // tm),
        in_specs=[_tok(tm, D), _perb(N_MOD, D), _full(ln_g), _full(ln_b), _full(w_in_p), _full(w_uq_p),
                  _full(w_kv), _full(gq), _full(gkv), tab, tab, tab, _AFTER],
        out_specs=[_tok(tm, w) for w, _ in outs],
        out_shape=[_sds((B, S, w), t) for w, t in outs],
        compiler_params=_cparams(("parallel", "parallel")),
    )(x, mod, ln_g, ln_b, w_in_p, w_uq_p, w_kv, gq, gkv, tc, ts1, ts2, after)


def _neg_abs(x):
    sign = jnp.uint32(0x80000000)
    return lax.bitcast_convert_type(lax.bitcast_convert_type(x, jnp.uint32) | sign, F32)


SB_Q_SCALE = -(SB_HD ** -0.5) * LOG2E
MLA_Q_SCALE = (MLA_NOPE + MLA_ROPE) ** -0.5 * LOG2E


def _log2_keep(zs):
    return jnp.minimum(zs, 0.0) - jnp.log2(1.0 + jnp.exp2(_neg_abs(zs)))


def _split_dot(a, u):
    hi = a.astype(BF16)
    lo = (a - hi.astype(F32)).astype(BF16)
    return _dot(jnp.concatenate([hi, lo], axis=1), jnp.concatenate([u, u], axis=0))


def _tri(n, rel):
    row = lax.broadcasted_iota(jnp.int32, (n, n), 0)
    col = lax.broadcasted_iota(jnp.int32, (n, n), 1)
    return rel(row, col).astype(BF16)


def _running_sum(a, tri, reverse, split):
    cs = tri.shape[0]
    n = a.shape[1] // cs
    out = [None] * n
    run = None
    for c in (reversed(range(n)) if reverse else range(n)):
        part = a[:, c * cs:(c + 1) * cs]
        loc = _split_dot(part, tri) if split else _dot(part.astype(BF16), tri)
        out[c] = loc if run is None else loc + run
        tot = jnp.sum(part, axis=1, keepdims=True)
        run = tot if run is None else run + tot
    return (out[0] if n == 1 else jnp.concatenate(out, axis=1)), run


def _tile_mask(nr, nk, r0, c0, rel):
    row = lax.broadcasted_iota(jnp.int32, (nr, nk), 0) + r0
    col = lax.broadcasted_iota(jnp.int32, (nr, nk), 1) + c0
    return rel(row, col)


def _put_rows(whole, part, r0):
    return part if r0 == 0 else jnp.concatenate([whole[:r0], part], axis=0)


def _diag_tiles(tq, split):
    half = tq // 2
    return [(0, tq, 0, half), (half, half, half, half)] if split else [(0, tq, 0, tq)]


def _sb_fwd(q, k, v, dm):
    B, S, W = q.shape
    tq = dm["tq"]
    nq = S // tq

    def body(q_ref, k_ref, v_ref, y_ref, tot_ref):
        qi = pl.program_id(2)
        q2 = q_ref[0]
        lane = lax.broadcasted_iota(jnp.int32, (tq, LANES), 1)
        qs = jnp.concatenate([jnp.where(lane < SB_HD, q2, 0), jnp.where(lane >= SB_HD, q2, 0)], axis=0).astype(BF16)
        later = _tri(min(tq, CUMSUM_W), lambda a, b: a > b)
        assert tq & (tq - 1) == 0
        strict = _tile_mask(2 * tq, tq, 0, 0, lambda t, s: s < (t & (tq - 1)))

        def block(j, carry, masked):
            acc, run = carry
            off = pl.multiple_of(j * tq, tq)
            zs = _dot_nt(qs, k_ref[0, pl.ds(off, tq), :])
            a = _log2_keep(zs)
            if masked:
                a = jnp.where(strict, a, 0.0)
            a_later, a_tot = _running_sum(a, later, reverse=True, split=True)
            w = jnp.exp2((a - zs) + a_later + run)
            if masked:
                w = jnp.where(strict, w, 0.0)
            return acc + _dot(w.astype(BF16), v_ref[0, pl.ds(off, tq), :]), run + a_tot

        carry = block(qi, (jnp.zeros((2 * tq, LANES), F32), jnp.zeros((2 * tq, 1), F32)), True)
        acc, run = lax.fori_loop(0, qi, lambda jj, c: block(qi - 1 - jj, c, False), carry)
        y_ref[0] = jnp.where(lane < SB_HD, acc[:tq], acc[tq:]).astype(BF16)
        tot_ref[0] = jnp.where(lane < SB_HD, run[:tq], run[tq:])

    qspec = pl.BlockSpec((1, tq, LANES), lambda b, hp, i: (b, i, hp))
    kspec = pl.BlockSpec((1, S, LANES), lambda b, hp, i: (b, 0, hp))
    return pl.pallas_call(
        body, name="sb_fwd", grid=(B, W // LANES, nq),
        in_specs=[qspec, kspec, kspec],
        out_specs=[qspec, qspec],
        out_shape=[_sds((B, S, W), BF16), _sds((B, S, W), F32)],
        compiler_params=_cparams(("parallel", "parallel", "arbitrary")),
    )(q, k, v)


def _sb_bwd(q, k, v, tot, dy, dm, after):
    B, S, W = q.shape
    tq = dm["tq"]
    nq = S // tq

    def body(q_ref, k_ref, v_ref, tot_ref, dy_ref, _, dq_ref, dk_ref, dv_ref, dk_acc, dv_acc):
        qi = pl.program_id(2)

        @pl.when(qi == 0)
        def _():
            dk_acc[...] = jnp.zeros_like(dk_acc)
            dv_acc[...] = jnp.zeros_like(dv_acc)

        q2 = q_ref[0]
        dy2 = dy_ref[0]
        tot2 = tot_ref[0]
        lane = lax.broadcasted_iota(jnp.int32, (tq, LANES), 1)
        in_h = [lane < SB_HD, lane >= SB_HD]
        qh = [jnp.where(m, q2, 0).astype(BF16) for m in in_h]
        dyh = [jnp.where(m, dy2, 0).astype(BF16) for m in in_h]
        toth = [tot2[:, 0:1], tot2[:, SB_HD:SB_HD + 1]]

        def tile(j, carry, r0, nr, c0, nk, masked):
            off = pl.multiple_of(j * tq + c0, math.gcd(tq, c0))
            k2 = k_ref[0, pl.ds(off, nk), :]
            v2 = v_ref[0, pl.ds(off, nk), :]
            upto = _tri(min(nk, CUMSUM_W), lambda a, b: a <= b)
            before = _tri(min(nk, CUMSUM_W), lambda a, b: a < b)
            strict = _tile_mask(nr, nk, r0, c0, lambda t, s: s < t) if masked else None
            rows = slice(r0, r0 + nr)
            new = []
            dk_blk = jnp.zeros((nk, LANES), F32)
            dv_blk = jnp.zeros((nk, LANES), F32)
            for h in range(2):
                dq, pa, pg = carry[3 * h][rows], carry[3 * h + 1][rows], carry[3 * h + 2][rows]
                zs = _dot_nt(qh[h][rows], k2)
                a = _log2_keep(zs)
                if masked:
                    a = jnp.where(strict, a, 0.0)
                a_upto, a_tot = _running_sum(a, upto, reverse=False, split=True)
                w = jnp.exp2((a - zs) + ((toth[h][rows] - pa) - a_upto))
                if masked:
                    w = jnp.where(strict, w, 0.0)
                g = _dot_nt(dyh[h][rows], v2) * w
                g_before, g_tot = _running_sum(g, before, reverse=False, split=False)
                g_before = g_before + pg
                dz = (g + g_before) * jnp.exp2(a) - g_before
                if masked:
                    dz = jnp.where(strict, dz, 0.0)
                dzb = dz.astype(BF16)
                dv_blk = dv_blk + _dot_tn(w.astype(BF16), dyh[h][rows])
                dk_blk = dk_blk + _dot_tn(dzb, qh[h][rows])
                new += [_put_rows(carry[3 * h], dq + _dot(dzb, k2), r0), _put_rows(carry[3 * h + 1], pa + a_tot, r0),
                        _put_rows(carry[3 * h + 2], pg + g_tot, r0)]
            dk_acc[pl.ds(off, nk), :] += dk_blk
            dv_acc[pl.ds(off, nk), :] += dv_blk
            return tuple(new)

        zero = jnp.zeros((tq, LANES), F32)
        zrun = jnp.zeros((tq, 1), F32)
        carry = lax.fori_loop(0, qi, lambda j, c: tile(j, c, 0, tq, 0, tq, False),
                              (zero, zrun, zrun, zero, zrun, zrun))
        for r0, nr, c0, nk in _diag_tiles(tq, False):
            carry = tile(qi, carry, r0, nr, c0, nk, True)
        dq_ref[0] = (jnp.where(in_h[0], carry[0], carry[3]) * (SB_HD ** -0.5)).astype(BF16)

        @pl.when(qi == nq - 1)
        def _():
            dk_ref[0] = (dk_acc[...] * (-1.0 / LOG2E)).astype(BF16)
            dv_ref[0] = dv_acc[...].astype(BF16)

    qspec = pl.BlockSpec((1, tq, LANES), lambda b, hp, i: (b, i, hp))
    kspec = pl.BlockSpec((1, S, LANES), lambda b, hp, i: (b, 0, hp))
    return pl.pallas_call(
        body, name="sb_bwd", grid=(B, W // LANES, nq),
        in_specs=[qspec, kspec, kspec, qspec, qspec, _AFTER],
        out_specs=[qspec, kspec, kspec],
        out_shape=[_sds((B, S, W), BF16)] * 3,
        scratch_shapes=[pltpu.VMEM((S, LANES), F32), pltpu.VMEM((S, LANES), F32)],
        compiler_params=_cparams(("parallel", "parallel", "arbitrary")),
    )(q, k, v, tot, dy, after)


def _same_or_earlier_chunk(row, col):
    return lax.shift_right_logical(col, 6) <= lax.shift_right_logical(row, 6)


def _mla_fwd(qp, kp, mv, dm, after):
    B, S, QW = qp.shape
    VW = mv.shape[2]
    tq = dm["tq"]
    nq = S // tq
    assert CHUNK == 64

    def body(q_ref, k_ref, v_ref, _, y_ref, lse_ref):
        qi = pl.program_id(2)
        q2 = q_ref[0]
        lane = lax.broadcasted_iota(jnp.int32, (tq, LANES), 1)

        def tile(j, carry, r0, nr, c0, nk, masked):
            off = pl.multiple_of(j * tq + c0, math.gcd(tq, c0))
            v2 = v_ref[0, pl.ds(off, nk), :]
            allowed = _tile_mask(nr, nk, r0, c0, _same_or_earlier_chunk) if masked else None
            rows = slice(r0, r0 + nr)
            heads = range(2)
            sl = [slice(h * HEAD_PAD, (h + 1) * HEAD_PAD) for h in heads]
            m_old = [carry[3 * h + 1][rows] for h in heads]
            s = [_dot_nt(q2[rows, sl[h]], k_ref[0, pl.ds(off, nk), sl[h]]) for h in heads]
            if masked:
                s = [jnp.where(allowed, s[h], -1e30) for h in heads]
            m_new = [jnp.maximum(m_old[h], jnp.max(s[h], axis=1, keepdims=True)) for h in heads]
            alpha = [jnp.exp2(m_old[h] - m_new[h]) for h in heads]
            p = [jnp.exp2(s[h] - m_new[h]) for h in heads]
            acc = [alpha[h] * carry[3 * h][rows] + _dot(p[h].astype(BF16), v2) for h in heads]
            l = [alpha[h] * carry[3 * h + 2][rows] + jnp.sum(p[h], axis=1, keepdims=True) for h in heads]
            out = []
            for h in heads:
                out += [_put_rows(carry[3 * h], acc[h], r0), _put_rows(carry[3 * h + 1], m_new[h], r0),
                        _put_rows(carry[3 * h + 2], l[h], r0)]
            return tuple(out)

        zero = jnp.zeros((tq, LANES), F32)
        m0 = jnp.full((tq, 1), -1e30, F32)
        l0 = jnp.zeros((tq, 1), F32)
        carry = (zero, m0, l0, zero, m0, l0)
        for r0, nr, c0, nk in _diag_tiles(tq, False):
            carry = tile(qi, carry, r0, nr, c0, nk, True)
        carry = lax.fori_loop(0, qi, lambda j, c: tile(j, c, 0, tq, 0, tq, False), carry)
        y0 = carry[0] / carry[2]
        y1 = carry[3] / carry[5]
        y_ref[0] = jnp.where(lane < MLA_V, y0, y1).astype(BF16)
        lse_ref[0] = jnp.where(lane < MLA_V, carry[1] + jnp.log2(carry[2]), carry[4] + jnp.log2(carry[5]))

    qspec = pl.BlockSpec((1, tq, 2 * HEAD_PAD), lambda b, hp, i: (b, i, hp))
    kspec = pl.BlockSpec((1, S, 2 * HEAD_PAD), lambda b, hp, i: (b, 0, hp))
    vspec = pl.BlockSpec((1, S, LANES), lambda b, hp, i: (b, 0, hp))
    yspec = pl.BlockSpec((1, tq, LANES), lambda b, hp, i: (b, i, hp))
    return pl.pallas_call(
        body, name="mla_fwd", grid=(B, VW // LANES, nq),
        in_specs=[qspec, kspec, vspec, _AFTER],
        out_specs=[yspec, yspec],
        out_shape=[_sds((B, S, VW), BF16), _sds((B, S, VW), F32)],
        compiler_params=_cparams(("parallel", "parallel", "arbitrary")),
    )(qp, kp, mv, after)


def _mla_bwd(qp, kp, mv, y, lse, dy, dm, after):
    B, S, QW = qp.shape
    VW = mv.shape[2]
    tq = dm["tq"]
    nq = S // tq
    scale = (MLA_NOPE + MLA_ROPE) ** -0.5

    def body(q_ref, k_ref, v_ref, y_ref, lse_ref, dy_ref, _, dq_ref, dk_ref, dv_ref, dk_acc, dv_acc):
        qi = pl.program_id(2)

        @pl.when(qi == 0)
        def _():
            dk_acc[...] = jnp.zeros_like(dk_acc)
            dv_acc[...] = jnp.zeros_like(dv_acc)

        q2 = q_ref[0]
        dy2 = dy_ref[0]
        lse2 = lse_ref[0]
        lane = lax.broadcasted_iota(jnp.int32, (tq, LANES), 1)
        in_h = [lane < MLA_V, lane >= MLA_V]
        prod = dy2.astype(F32) * y_ref[0].astype(F32)
        delta = [jnp.sum(jnp.where(m, prod, 0.0), axis=1, keepdims=True) for m in in_h]
        dyh = [jnp.where(m, dy2, 0).astype(BF16) for m in in_h]
        lseh = [lse2[:, 0:1], lse2[:, MLA_V:MLA_V + 1]]

        def tile(j, carry, r0, nr, c0, nk, masked):
            off = pl.multiple_of(j * tq + c0, math.gcd(tq, c0))
            v2 = v_ref[0, pl.ds(off, nk), :]
            allowed = _tile_mask(nr, nk, r0, c0, _same_or_earlier_chunk) if masked else None
            rows = slice(r0, r0 + nr)
            heads = range(2)
            sl = [slice(h * HEAD_PAD, (h + 1) * HEAD_PAD) for h in heads]
            qhh = [q2[rows, sl[h]] for h in heads]
            dyr = [dyh[h][rows] for h in heads]
            kh = [k_ref[0, pl.ds(off, nk), sl[h]] for h in heads]
            s = [_dot_nt(qhh[h], kh[h]) for h in heads]
            dp = [_dot_nt(dyr[h], v2) for h in heads]
            if masked:
                s = [jnp.where(allowed, s[h], -1e30) for h in heads]
            p = [jnp.exp2(s[h] - lseh[h][rows]) for h in heads]
            dv_acc[pl.ds(off, nk), :] += _dot_tn(p[0].astype(BF16), dyr[0]) + _dot_tn(p[1].astype(BF16), dyr[1])
            ds = [(p[h] * (dp[h] - delta[h][rows])).astype(BF16) for h in heads]
            for h in heads:
                dk_acc[pl.ds(off, nk), sl[h]] += _dot_tn(ds[h], qhh[h])
            return tuple(_put_rows(carry[h], carry[h][rows] + _dot(ds[h], kh[h]), r0) for h in heads)

        zero = jnp.zeros((tq, HEAD_PAD), F32)
        carry = lax.fori_loop(0, qi, lambda j, c: tile(j, c, 0, tq, 0, tq, False), (zero, zero))
        for r0, nr, c0, nk in _diag_tiles(tq, True):
            carry = tile(qi, carry, r0, nr, c0, nk, True)
        dq_ref[0] = (jnp.concatenate([carry[0], carry[1]], axis=1) * scale).astype(BF16)

        @pl.when(qi == nq - 1)
        def _():
            dk_ref[0] = (dk_acc[...] * (1.0 / LOG2E)).astype(BF16)
            dv_ref[0] = dv_acc[...].astype(BF16)

    qspec = pl.BlockSpec((1, tq, 2 * HEAD_PAD), lambda b, hp, i: (b, i, hp))
    kspec = pl.BlockSpec((1, S, 2 * HEAD_PAD), lambda b, hp, i: (b, 0, hp))
    vspec = pl.BlockSpec((1, S, LANES), lambda b, hp, i: (b, 0, hp))
    yspec = pl.BlockSpec((1, tq, LANES), lambda b, hp, i: (b, i, hp))
    return pl.pallas_call(
        body, name="mla_bwd", grid=(B, VW // LANES, nq),
        in_specs=[qspec, kspec, vspec, yspec, yspec, yspec, _AFTER],
        out_specs=[qspec, kspec, vspec],
        out_shape=[_sds((B, S, QW), BF16), _sds((B, S, QW), BF16), _sds((B, S, VW), BF16)],
        scratch_shapes=[pltpu.VMEM((S, 2 * HEAD_PAD), F32), pltpu.VMEM((S, LANES), F32)],
        compiler_params=_cparams(("parallel", "parallel", "arbitrary")),
    )(qp, kp, mv, y, lse, dy, after)


def _outproj_fwd(sb_y, mla_y, x0, mod, w_o, ln_g, ln_b, dm):
    B, S, D = x0.shape
    tm = dm["tm"]
    sbw = sb_y.shape[2]

    def body(ya_ref, yb_ref, x0_ref, mod_ref, wo_ref, g_ref, b_ref, mix_ref, x1_ref, h2_ref):
        mod = mod_ref[0]
        mix = _dot(ya_ref[0], wo_ref[0:sbw, :]) + _dot(yb_ref[0], wo_ref[sbw:, :])
        mix_ref[0] = mix
        x1, _, _ = _ln_fwd(ALPHA * x0_ref[0] + (1.0 + mod[2:3]) * mix, g_ref[...], b_ref[...])
        x1_ref[0] = x1
        h2_ref[0] = (x1 * (1.0 + mod[4:5]) + mod[3:4]).astype(BF16)

    return pl.pallas_call(
        body, name="outproj_fwd", grid=(B, S // tm),
        in_specs=[_tok(tm, sbw), _tok(tm, mla_y.shape[2]), _tok(tm, D), _perb(N_MOD, D),
                  _full(w_o), _full(ln_g), _full(ln_b)],
        out_specs=[_tok(tm, D)] * 3,
        out_shape=[_sds((B, S, D), F32), _sds((B, S, D), F32), _sds((B, S, D), BF16)],
        compiler_params=_cparams(("parallel", "parallel")),
    )(sb_y, mla_y, x0, mod, w_o, ln_g, ln_b)


def _stat_specs(B, D):
    specs = [pl.BlockSpec((1, 8, D), lambda b, s: (b, 0, 0)), pl.BlockSpec((8, D), lambda b, s: (0, 0))]
    shapes = [_sds((B, 8, D), F32), _sds((8, D), F32)]
    return specs, shapes


def _stat_init(bst_ref, wst_ref):
    @pl.when(pl.program_id(1) == 0)
    def _():
        bst_ref[...] = jnp.zeros_like(bst_ref)

    @pl.when((pl.program_id(0) == 0) & (pl.program_id(1) == 0))
    def _():
        wst_ref[...] = jnp.zeros_like(wst_ref)


def _mlp_fwd(h2, x1, mod, target, w_up, w_down, ln_g, ln_b, dm):
    B, S, D = x1.shape
    tm = dm["tm"]
    nck, _, ck = w_up.shape
    dff = nck * ck

    def body(h2_ref, x1_ref, mod_ref, t_ref, wu_ref, wd_ref, g_ref, b_ref, u_ref, dr_ref, bst_ref, wst_ref):
        _stat_init(bst_ref, wst_ref)
        mod = mod_ref[0]
        g = g_ref[...]
        h2 = h2_ref[0]
        ff = jnp.zeros((tm, D), F32)
        for c in range(nck):
            u = _dot(h2, wu_ref[c])
            u_ref[0, :, c * ck:(c + 1) * ck] = u.astype(BF16)
            act = jnp.square(jnp.maximum(u, 0.0)).astype(BF16)
            ff = ff + _dot(act, wd_ref[c])
        x2, xhat, rstd = _ln_fwd(ALPHA * x1_ref[0] + (1.0 + mod[5:6]) * ff, g, b_ref[...])
        err = x2 - t_ref[0]
        dy = err * (1.0 / D)
        dr = _ln_bwd(dy, xhat, rstd, g)
        dr_ref[0] = dr
        bst_ref[0, 0:1, :] += _colsum(dr * ff)
        wst_ref[0:1, :] += _colsum(dy * xhat)
        wst_ref[1:2, :] += _colsum(dy)
        wst_ref[2:3, :] += _colsum(err * err) * (0.5 / D)

    sspecs, sshapes = _stat_specs(B, D)
    return pl.pallas_call(
        body, name="mlp_fwd", grid=(B, S // tm),
        in_specs=[_tok(tm, D), _tok(tm, D), _perb(N_MOD, D), _tok(tm, D), _full(w_up), _full(w_down),
                  _full(ln_g), _full(ln_b)],
        out_specs=[_tok(tm, dff), _tok(tm, D)] + sspecs,
        out_shape=[_sds((B, S, dff), BF16), _sds((B, S, D), F32)] + sshapes,
        compiler_params=_cparams(("arbitrary", "arbitrary")),
    )(h2, x1, mod, target, w_up, w_down, ln_g, ln_b)


def _mlp_bwd(dr2, u, x1, x0, mix, mod, w_up, w_down, w_o, ln_g, dm):
    B, S, D = x1.shape
    tm = dm["tm_small"]
    sbw = dm["sbw"]
    nck, _, ck = w_up.shape
    dff = nck * ck

    def body(dr_ref, u_ref, x1_ref, x0_ref, mix_ref, mod_ref, wu_ref, wd_ref, wo_ref, g_ref,
             du_ref, dff_ref, dmix_ref, dx0_ref, dya_ref, dyb_ref, bst_ref, wst_ref):
        _stat_init(bst_ref, wst_ref)
        mod = mod_ref[0]
        dr2 = dr_ref[0]
        dffv = ((1.0 + mod[5:6]) * dr2).astype(BF16)
        dff_ref[0] = dffv
        dh2 = jnp.zeros((tm, D), F32)
        for c in range(nck):
            sl = slice(c * ck, (c + 1) * ck)
            da = _dot_nt(dffv, wd_ref[c])
            du = (da * (2.0 * jnp.maximum(u_ref[0, :, sl].astype(F32), 0.0))).astype(BF16)
            du_ref[0, :, sl] = du
            dh2 = dh2 + _dot_nt(du, wu_ref[c])
        x1 = x1_ref[0]
        dx1 = ALPHA * dr2 + dh2 * (1.0 + mod[4:5])
        bst_ref[0, 0:1, :] += _colsum(dh2 * x1)
        bst_ref[0, 1:2, :] += _colsum(dh2)
        mix = mix_ref[0]
        g = g_ref[...]
        _, xhat, rstd = _ln_fwd(ALPHA * x0_ref[0] + (1.0 + mod[2:3]) * mix, g, 0.0)
        dr1 = _ln_bwd(dx1, xhat, rstd, g)
        wst_ref[0:1, :] += _colsum(dx1 * xhat)
        wst_ref[1:2, :] += _colsum(dx1)
        bst_ref[0, 2:3, :] += _colsum(dr1 * mix)
        dx0_ref[0] = ALPHA * dr1
        dmix = ((1.0 + mod[2:3]) * dr1).astype(BF16)
        dmix_ref[0] = dmix
        dya_ref[0] = _dot_nt(dmix, wo_ref[0:sbw, :]).astype(BF16)
        dyb_ref[0] = _dot_nt(dmix, wo_ref[sbw:, :]).astype(BF16)

    sspecs, sshapes = _stat_specs(B, D)
    wa, wb = sbw, w_o.shape[0] - sbw
    return pl.pallas_call(
        body, name="mlp_bwd", grid=(B, S // tm),
        in_specs=[_tok(tm, D), _tok(tm, dff), _tok(tm, D), _tok(tm, D), _tok(tm, D), _perb(N_MOD, D),
                  _full(w_up), _full(w_down), _full(w_o), _full(ln_g)],
        out_specs=[_tok(tm, dff), _tok(tm, D), _tok(tm, D), _tok(tm, D), _tok(tm, wa), _tok(tm, wb)] + sspecs,
        out_shape=[_sds((B, S, dff), BF16), _sds((B, S, D), BF16), _sds((B, S, D), BF16), _sds((B, S, D), F32),
                   _sds((B, S, wa), BF16), _sds((B, S, wb), BF16)] + sshapes,
        compiler_params=_cparams(("arbitrary", "arbitrary")),
    )(dr2, u, x1, x0, mix, mod, w_up, w_down, w_o, ln_g)


def _inproj_bwd(x, x0, dx0a, mod, ln_g, dq, dk, dv, dqp, dkp, dmv, cq, ckv, w_in_p, w_uq_p, w_kv, gq, gkv,
                tc, ts1, ts2, dm):
    B, S, D = x.shape
    tm = dm["tm"]
    sbw, qr, kvr, nh = dm["sbw"], dm["qr"], dm["kvr"], dm["nh"]
    qpw = nh * HEAD_PAD
    dinp = w_in_p.shape[1]
    kvw = w_kv.shape[1]

    def body(x_ref, x0_ref, dx0a_ref, mod_ref, g_ref, dq_ref, dk_ref, dv_ref, dqp_ref, dkp_ref, dmv_ref,
             cq_ref, ckv_ref, win_ref, wuq_ref, wkv_ref, gq_ref, gkv_ref, tc_ref, ts1_ref, ts2_ref,
             gx_ref, dproj_ref, dqpre_ref, dkvo_ref, bst_ref, wst_ref):
        _stat_init(bst_ref, wst_ref)
        mod = mod_ref[0]
        c1, s1, s2 = tc_ref[...], ts1_ref[...], ts2_ref[...]
        c8, s18, s28 = jnp.tile(c1, (1, nh)), jnp.tile(s1, (1, nh)), jnp.tile(s2, (1, nh))
        dqpre = _rope_t(dqp_ref[0].astype(F32), c8, s18, s28).astype(BF16)
        dqpre_ref[0] = dqpre
        gq = gq_ref[...]
        cq = cq_ref[0]
        rq = lax.rsqrt(jnp.mean(cq * cq, axis=-1, keepdims=True) + RMS_EPS)
        dqn = _dot_nt(dqpre, wuq_ref[...])
        wst_ref[4:5, 0:qr] += _colsum(dqn * cq * rq)
        dqg = dqn * gq
        dcq = rq * dqg - cq * (rq * rq * rq) * jnp.mean(dqg * cq, axis=-1, keepdims=True)

        dkpre = _rope_t(dkp_ref[0].astype(F32), c8, s18, s28)
        dkr = dkpre[:, 0:HEAD_PAD]
        for h in range(1, nh):
            dkr = dkr + dkpre[:, h * HEAD_PAD:(h + 1) * HEAD_PAD]
        lane = lax.broadcasted_iota(jnp.int32, (tm, LANES), 1)
        dkr = jnp.where((lane >= MLA_NOPE) & (lane < MLA_NOPE + MLA_ROPE), dkr, 0.0)
        dkr = pltpu.roll(dkr, LANES - MLA_NOPE, 1)
        dkvo = jnp.concatenate([dkpre.astype(BF16), dmv_ref[0]], axis=1)
        dkvo_ref[0] = dkvo
        gkv = gkv_ref[...]
        ckv = ckv_ref[0]
        rkv = lax.rsqrt(jnp.mean(ckv * ckv, axis=-1, keepdims=True) + RMS_EPS)
        dkvn = _dot_nt(dkvo, wkv_ref[...])
        wst_ref[5:6, 0:kvr] += _colsum(dkvn * ckv * rkv)
        dkg = dkvn * gkv
        dckv = rkv * dkg - ckv * (rkv * rkv * rkv) * jnp.mean(dkg * ckv, axis=-1, keepdims=True)

        dproj = jnp.concatenate([dq_ref[0], dk_ref[0], dv_ref[0], dcq.astype(BF16), dckv.astype(BF16),
                                 dkr.astype(BF16)], axis=1)
        dproj_ref[0] = dproj
        dh = _dot_nt(dproj, win_ref[...])
        x0 = x0_ref[0]
        dx0 = dx0a_ref[0] + dh * (1.0 + mod[1:2])
        bst_ref[0, 0:1, :] += _colsum(dh * x0)
        bst_ref[0, 1:2, :] += _colsum(dh)
        g = g_ref[...]
        _, xhat, rstd = _ln_fwd(x_ref[0], g, 0.0)
        gx_ref[0] = _ln_bwd(dx0, xhat, rstd, g)
        wst_ref[0:1, :] += _colsum(dx0 * xhat)
        wst_ref[1:2, :] += _colsum(dx0)

    tab = pl.BlockSpec((tm, LANES), lambda b, s: (s, 0))
    sspecs, sshapes = _stat_specs(B, D)
    return pl.pallas_call(
        body, name="inproj_bwd", grid=(B, S // tm),
        in_specs=[_tok(tm, D), _tok(tm, D), _tok(tm, D), _perb(N_MOD, D), _full(ln_g),
                  _tok(tm, sbw), _tok(tm, sbw), _tok(tm, sbw), _tok(tm, qpw), _tok(tm, qpw), _tok(tm, nh * MLA_V),
                  _tok(tm, qr), _tok(tm, kvr), _full(w_in_p), _full(w_uq_p), _full(w_kv), _full(gq), _full(gkv),
                  tab, tab, tab],
        out_specs=[_tok(tm, D), _tok(tm, dinp), _tok(tm, qpw), _tok(tm, kvw)] + sspecs,
        out_shape=[_sds((B, S, D), F32), _sds((B, S, dinp), BF16), _sds((B, S, qpw), BF16),
                   _sds((B, S, kvw), BF16)] + sshapes,
        compiler_params=_cparams(("arbitrary", "arbitrary")),
    )(x, x0, dx0a, mod, ln_g, dq, dk, dv, dqp, dkp, dmv, cq, ckv, w_in_p, w_uq_p, w_kv, gq, gkv, tc, ts1, ts2)


def _tile_of(n, cap):
    if n <= cap:
        return n
    best = n
    for t in range(LANES, cap + 1, LANES):
        if n % t == 0:
            best = t
    return best


def _mm_tn(a, g, name, after, relu_sq=False, out_dtype=F32, col_blocks=None):
    T, K = a.shape
    N = g.shape[1]
    tt = 1024 if T % 1024 == 0 else (512 if T % 512 == 0 else T)
    tk = _tile_of(K, 1024)
    tn = _tile_of(N, 1280)
    nt = T // tt
    bw = N // col_blocks if col_blocks else tn
    assert tn % bw == 0

    def body(a_ref, g_ref, _, o_ref, acc_ref):
        @pl.when(pl.program_id(2) == 0)
        def _():
            acc_ref[...] = jnp.zeros_like(acc_ref)

        av = a_ref[...]
        if relu_sq:
            av = jnp.square(jnp.maximum(av.astype(F32), 0.0)).astype(BF16)
        acc_ref[...] += _dot_tn(av, g_ref[...])

        @pl.when(pl.program_id(2) == nt - 1)
        def _():
            if col_blocks:
                for c in range(tn // bw):
                    o_ref[c] = acc_ref[:, c * bw:(c + 1) * bw].astype(out_dtype)
            else:
                o_ref[...] = acc_ref[...].astype(out_dtype)

    if col_blocks:
        out_spec = pl.BlockSpec((tn // bw, tk, bw), lambda i, j, t: (j, i, 0))
        out_shape = _sds((col_blocks, K, bw), out_dtype)
    else:
        out_spec = pl.BlockSpec((tk, tn), lambda i, j, t: (i, j))
        out_shape = _sds((K, N), out_dtype)
    return pl.pallas_call(
        body, name=name, grid=(K // tk, N // tn, nt),
        in_specs=[pl.BlockSpec((tt, tk), lambda i, j, t: (t, i)), pl.BlockSpec((tt, tn), lambda i, j, t: (t, j)),
                  _AFTER],
        out_specs=out_spec, out_shape=out_shape,
        scratch_shapes=[pltpu.VMEM((tk, tn), F32)],
        compiler_params=_cparams(("parallel", "parallel", "arbitrary")),
    )(a, g, after)


def _reduce_adamw(parts, w, m, v, name):
    P, K, N = parts.shape
    tr = 256 if K % 256 == 0 else K

    def body(p_ref, w_ref, m_ref, v_ref, g_ref, d_ref, nm_ref, nv_ref):
        g = p_ref[0].astype(F32)
        for k in range(1, P):
            g = g + p_ref[k].astype(F32)
        g_ref[0] = g
        d_ref[0], nm_ref[0], nv_ref[0] = _adamw(w_ref[0], g, m_ref[0], v_ref[0])

    spec = pl.BlockSpec((1, tr, N), lambda r: (0, r, 0))
    return pl.pallas_call(
        body, name=name, grid=(K // tr,),
        in_specs=[pl.BlockSpec((P, tr, N), lambda r: (0, r, 0)), spec, spec, spec],
        out_specs=[spec] * 4, out_shape=[_sds((1, K, N), F32)] * 4,
        compiler_params=_cparams(("parallel",)),
    )(parts, w, m, v)


def _finish(sm, dmod_all, dmod_my, cact_all, p_small, m_small, v_small, b_ada, m_b, v_b, w_ada, m_w, v_w):
    n0 = p_small.shape[1]
    n1 = sm.shape[1]
    d = cact_all.shape[1]

    def body(sm_ref, dma_ref, dmm_ref, ca_ref, p_ref, pm_ref, pv_ref, b_ref, bm_ref, bv_ref, w_ref, wm_ref, wv_ref,
             gs_ref, ds_ref, ms_ref, vs_ref, gb_ref, db_ref, mb_ref, vb_ref, gw_ref, dw_ref, mw_ref, vw_ref,
             loss_ref):
        gs = sm_ref[0:1, :]
        for k in range(1, N_DEV):
            gs = gs + sm_ref[k:k + 1, :]
        gs_ref[...] = gs
        ds_ref[...], ms_ref[...], vs_ref[...] = _adamw(p_ref[...], gs[:, 0:n0], pm_ref[...], pv_ref[...])
        loss_ref[...] = jnp.zeros((1, LANES), F32) + jnp.sum(gs[:, n1 - d:n1])
        gb = jnp.sum(dma_ref[...], axis=0, keepdims=True)
        gb_ref[...] = gb
        db_ref[...], mb_ref[...], vb_ref[...] = _adamw(b_ref[...], gb, bm_ref[...], bv_ref[...])
        gw = _dot_tn(ca_ref[...].astype(BF16), dmm_ref[...].astype(BF16))
        gw_ref[...] = gw
        dw_ref[...], mw_ref[...], vw_ref[...] = _adamw(w_ref[...], gw, wm_ref[...], wv_ref[...])

    s0 = _sds(p_small.shape, F32)
    sb = _sds(b_ada.shape, F32)
    sw = _sds(w_ada.shape, F32)
    return pl.pallas_call(
        body, name="finish_small",
        out_shape=[_sds((1, n1), F32), s0, s0, s0, sb, sb, sb, sb, sw, sw, sw, sw,
                   _sds((1, LANES), F32)],
        compiler_params=pltpu.CompilerParams(vmem_limit_bytes=VMEM_LIMIT),
    )(sm, dmod_all, dmod_my, cact_all, p_small, m_small, v_small, b_ada, m_b, v_b, w_ada, m_w, v_w)


def _pack(arrs, dtype, width):
    flat = jnp.concatenate([a.astype(dtype).reshape(-1) for a in arrs])
    rows = -(-flat.shape[0] // (256 * width)) * 256
    return jnp.pad(flat, (0, rows * width - flat.shape[0])).reshape(rows, width)


def _unpack(slab, shapes):
    flat = slab.reshape(-1)
    out, o = [], 0
    for s in shapes:
        n = math.prod(s)
        out.append(flat[o:o + n].reshape(s))
        o += n
    return out


def _rope_tables(S):
    inv_freq = 1.0 / (ROPE_BASE ** (jnp.arange(0, MLA_ROPE, 2, dtype=F32) / MLA_ROPE))
    ang = jnp.arange(S, dtype=F32)[:, None] * inv_freq[None, :]
    cos, sin = jnp.cos(ang), jnp.sin(ang)
    one = jnp.ones((S, MLA_NOPE), F32)
    z16 = jnp.zeros((S, 16), F32)
    z32 = jnp.zeros((S, 32), F32)
    z64 = jnp.zeros((S, MLA_NOPE), F32)
    tc = jnp.concatenate([one, cos, cos, jnp.ones((S, 32), F32)], axis=1)
    ts1 = jnp.concatenate([z64, -sin, z16, z32], axis=1)
    ts2 = jnp.concatenate([z64, z16, sin, z32], axis=1)
    return tc, ts1, ts2


def kernel(x, c, ln_in_g, ln_in_b, w_ada, b_ada, w_in, q_norm_g, kv_norm_g, w_uq, w_ukv, w_o, ln1_g, ln1_b, w_up, w_down, ln2_g, ln2_b, loss_target, m_ln_in_g, m_ln_in_b, m_w_ada, m_b_ada, m_w_in, m_q_norm_g, m_kv_norm_g, m_w_uq, m_w_ukv, m_w_o, m_ln1_g, m_ln1_b, m_w_up, m_w_down, m_ln2_g, m_ln2_b, v_ln_in_g, v_ln_in_b, v_w_ada, v_b_ada, v_w_in, v_q_norm_g, v_kv_norm_g, v_w_uq, v_w_ukv, v_w_o, v_ln1_g, v_ln1_b, v_w_up, v_w_down, v_ln2_g, v_ln2_b):
    B, S, D = x.shape
    sbw = D // 2
    mlw = D - sbw
    nh = mlw // MLA_V
    qr = w_uq.shape[1]
    kvr = w_ukv.shape[1]
    qk = MLA_NOPE + MLA_ROPE
    dff = w_up.shape[2] * N_DEV
    din = w_in.shape[2] * N_DEV
    tm = 512 if S % 512 == 0 else S
    tq = min(512, S // 2)
    dm = dict(tm=tm, tm_small=min(tm, 256), tq=tq, sbw=sbw, qr=qr, kvr=kvr, nh=nh)
    dev =4 * lax.axis_index("x") + 2 * lax.axis_index("y") + lax.axis_index("c")

    big = [w_in, w_uq, w_ukv, w_o, w_up, w_down]
    first_w, first_token = _chip_exchange_start([a[0].astype(BF16) for a in big[:3]], "gather_w_first_start",
                                                scatter=False, after=c)

    nada = w_ada.shape[2]
    c_all = _all_gather([c + first_token[0, 0]], "gather_c")[0].reshape(N_DEV * B, D)
    b_loc = lax.dynamic_slice(b_ada, (0, dev * nada), (1, nada))
    cact_all, mod_part = _ada_partial(c_all, w_ada[0], b_loc)
    mod_all = _all_gather([mod_part], "gather_mod")[0]
    mod = lax.dynamic_slice(mod_all, (0, dev * B, 0), (N_DEV, B, nada))
    mod = jnp.swapaxes(mod, 0, 1).reshape(B, N_MOD, D)

    first_by_chip = _chip_exchange_wait(first_w, mod_all, "gather_w_first_wait")
    w_in8, w_uq8, w_ukv8 = [b.reshape((N_DEV,) + b.shape[2:]) for b in _core_gather(first_by_chip, "gather_w_first_cores")]
    late_w, late_token = _chip_exchange_start([a[0].astype(BF16) for a in big[3:]], "gather_w_late_start",
                                              scatter=False, after=w_in8, everyone=True)
    cols = lambda a8: jnp.swapaxes(a8, 0, 1).reshape(a8.shape[1], N_DEV * a8.shape[2])
    w_in_p = jnp.pad(cols(w_in8), ((0, 0), (0, LANES - MLA_ROPE)))
    zpad = jnp.zeros((qr, nh, HEAD_PAD - qk), BF16)
    w_uq_p = jnp.concatenate([cols(w_uq8).reshape(qr, nh, qk), zpad], axis=2).reshape(qr, nh * HEAD_PAD)
    w_ukv_f = cols(w_ukv8)
    w_uk = w_ukv_f[:, :nh * MLA_NOPE].reshape(kvr, nh, MLA_NOPE)
    w_uk_p = jnp.concatenate([w_uk, jnp.zeros((kvr, nh, HEAD_PAD - MLA_NOPE), BF16)], axis=2)
    w_kv = jnp.concatenate([w_uk_p.reshape(kvr, nh * HEAD_PAD), w_ukv_f[:, nh * MLA_NOPE:]], axis=1)

    tc, ts1, ts2 = _rope_tables(S)
    g_in, b_in = ln_in_g.reshape(1, D), ln_in_b.reshape(1, D)
    (x0, h, sq, sk, sv, qp, kp, mv, cq, ckv, qn, kvn) = _inproj_fwd(
        x, mod, g_in, b_in, w_in_p, w_uq_p, w_kv, q_norm_g, kv_norm_g, tc, ts1, ts2, dm, late_token)
    sb_y, sb_tot = _sb_fwd(sq, sk, sv, dm)
    mla_y, mla_lse = _mla_fwd(qp, kp, mv, dm, sb_tot)
    w_o8, w_up8, w_down8 = _chip_exchange_wait(late_w, mla_lse, "gather_w_late_wait")
    w_o_f = w_o8.reshape(D, D)
    mix, x1, h2 = _outproj_fwd(sb_y, mla_y, x0, mod, w_o_f, ln1_g, ln1_b, dm)
    u, dr2, bst_c, wst_c = _mlp_fwd(h2, x1, mod, loss_target, w_up8, w_down8, ln2_g, ln2_b, dm)

    du, dffb, dmixb, dx0a, dsb_y, dmla_y, bst_b, wst_b = _mlp_bwd(
        dr2, u, x1, x0, mix, mod, w_up8, w_down8, w_o_f, ln1_g, dm)
    T = B * S
    r2 = lambda a: a.reshape(T, a.shape[2])
    by_core = lambda a: a.reshape((4, 2) + a.shape[1:])
    g_o = jnp.concatenate([_mm_tn(r2(sb_y), r2(dmixb), "grad_w_o_sb", dr2, out_dtype=BF16),
                           _mm_tn(r2(mla_y), r2(dmixb), "grad_w_o_mla", dr2, out_dtype=BF16)], axis=0)
    g_up8 = _mm_tn(r2(h2), r2(du), "grad_w_up", dr2, out_dtype=BF16, col_blocks=N_DEV)
    g_down = _mm_tn(r2(u), r2(dffb), "grad_w_down", dr2, relu_sq=True, out_dtype=BF16)
    early = [g_o.reshape(N_DEV, D // N_DEV, D), g_up8, g_down.reshape(N_DEV, dff // N_DEV, D)]
    early_g, early_token = _chip_exchange_start(early, "scatter_g_early_start", scatter=True, after=dr2,
                                                everyone=True)

    dsq, dsk, dsv = _sb_bwd(sq, sk, sv, sb_tot, dsb_y, dm, early_token)
    dqp, dkp, dmv = _mla_bwd(qp, kp, mv, mla_y, mla_lse, dmla_y, dm, dsq)
    grad_x, dproj, dqpre, dkvo, bst_a, wst_a = _inproj_bwd(
        x, x0, dx0a, mod, g_in, dsq, dsk, dsv, dqp, dkp, dmv, cq, ckv, w_in_p, w_uq_p, w_kv, q_norm_g, kv_norm_g,
        tc, ts1, ts2, dm)

    dmod = jnp.concatenate([bst_a[:, 1], bst_a[:, 0], bst_b[:, 2], bst_b[:, 1], bst_b[:, 0], bst_c[:, 0]], axis=1)
    small = jnp.concatenate([wst_a[0], wst_a[1], wst_a[4, :qr], wst_a[5, :kvr], wst_b[0], wst_b[1],
                             wst_c[0], wst_c[1], wst_c[2]])
    n1 = small.shape[0]
    small_g, small_token = _chip_exchange_start([_pack([dmod, small], F32, LANES)], "gather_small_start",
                                                scatter=False, after=grad_x)
    g_in_p = _mm_tn(r2(h), r2(dproj), "grad_w_in", small_token)
    g_uq_p = _mm_tn(r2(qn), r2(dqpre), "grad_w_uq", small_token)
    g_kv = _mm_tn(r2(kvn), r2(dkvo), "grad_w_kv", small_token)
    small_by_chip = _chip_exchange_wait(small_g, g_kv, "gather_small_wait")
    both = _core_gather(small_by_chip, "gather_small_cores")[0].reshape(N_DEV, -1)
    g_uq_f = g_uq_p.reshape(qr, nh, HEAD_PAD)[:, :, :qk].reshape(qr, nh * qk)
    g_uk = g_kv[:, :nh * HEAD_PAD].reshape(kvr, nh, HEAD_PAD)[:, :, :MLA_NOPE].reshape(kvr, nh * MLA_NOPE)
    g_ukv_f = jnp.concatenate([g_uk, g_kv[:, nh * HEAD_PAD:]], axis=1)
    early_quarter = _chip_exchange_wait(early_g, g_kv, "scatter_g_early_wait")

    def by_dest_cols(a):
        k, n = a.shape[0], a.shape[1] // N_DEV
        return jnp.swapaxes(a.reshape(k, N_DEV, n), 0, 1).astype(BF16)

    last = [by_dest_cols(g_in_p[:, :din]), by_dest_cols(g_uq_f), by_dest_cols(g_ukv_f)]
    last_sum = _core_scatter_sum([by_core(a) for a in last], "scatter_g_last_cores")
    last_g, last_token = _chip_exchange_start(last_sum, "scatter_g_last_start", scatter=True, after=grad_x)
    names = ["w_in", "w_uq", "w_ukv", "w_o", "w_up", "w_down"]
    moms = [m_w_in, m_w_uq, m_w_ukv, m_w_o, m_w_up, m_w_down]
    vars_ = [v_w_in, v_w_uq, v_w_ukv, v_w_o, v_w_up, v_w_down]
    res_early = [_reduce_adamw(p, w, m, v, "adamw_" + n)
                 for p, w, m, v, n in zip(early_quarter, big[3:], moms[3:], vars_[3:], names[3:])]

    dmod_all = both[:, :B * N_MOD * D].reshape(N_DEV * B, N_MOD * D)
    sm = both[:, B * N_MOD * D:B * N_MOD * D + n1] + last_token[0, 0]
    dmod_my = lax.dynamic_slice(dmod_all, (0, dev * nada), (N_DEV * B, nada))
    row = lambda arrs: jnp.concatenate([a.reshape(1, -1) for a in arrs], axis=1)
    smalls = [ln_in_g, ln_in_b, q_norm_g, kv_norm_g, ln1_g, ln1_b, ln2_g, ln2_b]
    small_shapes = [a.shape for a in smalls]
    (gs, ds, nms, nvs, g_b, d_b, nm_b, nv_b, g_w, d_w, nm_w, nv_w, loss_v) = _finish(
        sm, dmod_all, dmod_my, cact_all, row(smalls),
        row([m_ln_in_g, m_ln_in_b, m_q_norm_g, m_kv_norm_g, m_ln1_g, m_ln1_b, m_ln2_g, m_ln2_b]),
        row([v_ln_in_g, v_ln_in_b, v_q_norm_g, v_kv_norm_g, v_ln1_g, v_ln1_b, v_ln2_g, v_ln2_b]),
        b_ada, m_b_ada, v_b_ada, w_ada[0], m_w_ada[0], v_w_ada[0])
    gsm, dsm, nmsm, nvsm = (_unpack(s, small_shapes) for s in (gs, ds, nms, nvs))
    last_quarter = _chip_exchange_wait(last_g, loss_v, "scatter_g_last_wait")
    res_last = [_reduce_adamw(p, w, m, v, "adamw_" + n)
                for p, w, m, v, n in zip(last_quarter, big[:3], moms[:3], vars_[:3], names[:3])]
    gb, db, nmb, nvb = ([r[i] for r in res_last + res_early] for i in range(4))

    def ordered(sm_l, w_l, ada_w, ada_b):
        return [sm_l[0], sm_l[1], ada_w[None], ada_b, w_l[0], sm_l[2], sm_l[3], w_l[1], w_l[2], w_l[3],
                sm_l[4], sm_l[5], w_l[4], w_l[5], sm_l[6], sm_l[7]]

    loss = loss_v[0, 0]
    return (loss, grad_x, *ordered(gsm, gb, g_w, g_b), *ordered(dsm, db, d_w, d_b),
            *ordered(nmsm, nmb, nm_w, nm_b), *ordered(nvsm, nvb, nv_w, nv_b))
```

```python
import math

import jax
import jax.numpy as jnp
from jax import lax
from jax.experimental import pallas as pl
from jax.experimental.pallas import tpu as pltpu

F32 = jnp.float32
BF16 = jnp.bfloat16

SB_HD = 64
MLA_V = 64
MLA_NOPE = 64
MLA_ROPE = 32
HEAD_PAD = 128
CHUNK = 64
ROPE_BASE = 10000.0
LN_EPS = 1e-5
RMS_EPS = 1e-6
DEPTH = 1
ALPHA = (2.0 * DEPTH) ** 0.25
N_MOD = 6
ADAM_LR = 0.001
ADAM_B1 = 0.9
ADAM_B2 = 0.999
ADAM_EPS = 1e-08
ADAM_WD = 0.01
ADAM_STEP = 10
N_DEV = 8
LANES = 128
LOG2E = 1.4426950408889634
CUMSUM_W = 256
VMEM_LIMIT = 56 * 1024 * 1024
MESH = pl.DeviceIdType.MESH


def _dot(a, b):
    return jnp.dot(a, b, preferred_element_type=F32)


def _dot_nt(a, b):
    return lax.dot_general(a, b, (((1,), (1,)), ((), ())), preferred_element_type=F32)


def _dot_tn(a, b):
    return lax.dot_general(a, b, (((0,), (0,)), ((), ())), preferred_element_type=F32)


def _cparams(sem):
    return pltpu.CompilerParams(dimension_semantics=sem, vmem_limit_bytes=VMEM_LIMIT)


def _full(a):
    nd = a.ndim
    return pl.BlockSpec(a.shape, lambda *_: (0,) * nd, pipeline_mode=pl.Buffered(1))


def _tok(tm, w):
    return pl.BlockSpec((1, tm, w), lambda b, s: (b, s, 0))


def _perb(rows, w):
    return pl.BlockSpec((1, rows, w), lambda b, s: (b, 0, 0))


def _sds(shape, dtype):
    return jax.ShapeDtypeStruct(shape, dtype)


def _ln_fwd(x, g, b):
    mu = jnp.mean(x, axis=-1, keepdims=True)
    xc = x - mu
    var = jnp.mean(xc * xc, axis=-1, keepdims=True)
    rstd = lax.rsqrt(var + LN_EPS)
    xhat = xc * rstd
    return xhat * g + b, xhat, rstd


def _ln_bwd(dy, xhat, rstd, g):
    dxh = dy * g
    m1 = jnp.mean(dxh, axis=-1, keepdims=True)
    m2 = jnp.mean(dxh * xhat, axis=-1, keepdims=True)
    return rstd * (dxh - m1 - xhat * m2)


def _colsum(a):
    return jnp.sum(a, axis=0, keepdims=True)


def _rope(x, c, s1, s2):
    w = x.shape[-1]
    return x * c + pltpu.roll(x, w - 16, 1) * s1 + pltpu.roll(x, 16, 1) * s2


def _rope_t(x, c, s1, s2):
    w = x.shape[-1]
    return x * c - pltpu.roll(x, w - 16, 1) * s1 - pltpu.roll(x, 16, 1) * s2


def _adamw(w, g, m, v):
    m = ADAM_B1 * m + (1.0 - ADAM_B1) * g
    v = ADAM_B2 * v + (1.0 - ADAM_B2) * (g * g)
    m_hat = m / (1.0 - ADAM_B1 ** ADAM_STEP)
    v_hat = v / (1.0 - ADAM_B2 ** ADAM_STEP)
    delta = -ADAM_LR * (m_hat / (jnp.sqrt(v_hat) + ADAM_EPS) + ADAM_WD * w)
    return delta, m, v


def _my_place():
    return lax.axis_index("x"), lax.axis_index("y"), lax.axis_index("c")


def _chip_peers(mx, my):
    out = []
    for j in (1, 2, 3):
        px = 1 - mx if (j >> 1) else mx
        py = 1 - my if (j & 1) else my
        out.append((px, py, 2 * px + py))
    return out


def _split_peers(everyone):
    mx, my, mc = _my_place()
    if not everyone:
        return [(px, py, mc, pk) for px, py, pk in _chip_peers(mx, my)], 2 * mx + my
    peers = []
    for j in range(1, N_DEV):
        px = 1 - mx if (j >> 2) & 1 else mx
        py = 1 - my if (j >> 1) & 1 else my
        pc = 1 - mc if j & 1 else mc
        peers.append((px, py, pc, 4 * px + 2 * py + pc))
    return peers, 4 * mx + 2 * my + mc


def _hbm_call(body, name, n_in, out_shape, sems):
    hbm = pl.BlockSpec(memory_space=pl.ANY)
    return pl.pallas_call(
        body, name=name, out_shape=out_shape,
        in_specs=[hbm] * n_in, out_specs=[hbm] * len(out_shape),
        scratch_shapes=[pltpu.SemaphoreType.DMA(s) for s in sems])


def _chip_exchange(xs, name, scatter):
    n = len(xs)

    def body(*refs):
        x_refs, o_refs = refs[:n], refs[n:2 * n]
        ssem, rsem, lsem = refs[2 * n:]
        mx, my, mc = _my_place()
        me = 2 * mx + my
        peers = _chip_peers(mx, my)

        def copy(i, j, src_slot, dst_slot):
            px, py, _ = peers[j]
            return pltpu.make_async_remote_copy(
                src_ref=x_refs[i].at[src_slot] if scatter else x_refs[i], dst_ref=o_refs[i].at[dst_slot],
                send_sem=ssem.at[i, j], recv_sem=rsem.at[i, j], device_id=(px, py, mc), device_id_type=MESH)

        local = [pltpu.make_async_copy(x_refs[i].at[me] if scatter else x_refs[i], o_refs[i].at[me], lsem.at[i])
                 for i in range(n)]
        sends = [copy(i, j, peers[j][2], me) for i in range(n) for j in range(3)]
        for cp in local + sends:
            cp.start()
        for i in range(n):
            for j in range(3):
                copy(i, j, peers[j][2], peers[j][2]).wait_recv()
        for cp in sends:
            cp.wait_send()
        for cp in local:
            cp.wait()

    out_shape = [_sds((4,) + tuple(x.shape[1:] if scatter else x.shape), x.dtype) for x in xs]
    return _hbm_call(body, name, n, out_shape, [(n, 3), (n, 3), (n,)])(*xs)


def _chip_exchange_start(xs, name, scatter, after, everyone=False):
    n = len(xs)
    npeer = N_DEV - 1 if everyone else 3
    blks = [tuple(x.shape[1:] if scatter else x.shape) for x in xs]

    def body(*refs):
        x_refs, land_refs = refs[:n], refs[n:2 * n]
        ssem, rsem = refs[2 * n + 1], refs[2 * n + 2]
        token = refs[-1]
        peers, me = _split_peers(everyone)
        for i in range(n):
            for j, (px, py, pc, slot) in enumerate(peers):
                pltpu.make_async_remote_copy(
                    src_ref=x_refs[i].at[slot] if scatter else x_refs[i], dst_ref=land_refs[i].at[me],
                    send_sem=ssem.at[npeer * i + j], recv_sem=rsem.at[npeer * i + j], device_id=(px, py, pc),
                    device_id_type=MESH).start()
        token[...] = jnp.zeros_like(token)

    hbm = pl.BlockSpec(memory_space=pltpu.HBM)
    sem = pl.BlockSpec(memory_space=pltpu.SEMAPHORE)
    lands = [lax.empty((npeer + 1,) + b, x.dtype) for b, x in zip(blks, xs)]
    res = pl.pallas_call(
        body, name=name,
        out_shape=[pltpu.SemaphoreType.DMA((npeer * n,)), pltpu.SemaphoreType.DMA((npeer * n,))]
        + [pltpu.HBM(x.shape, x.dtype) for x in xs] + [pltpu.HBM(l.shape, l.dtype) for l in lands]
        + [_sds((8, LANES), F32)],
        in_specs=[hbm] * (2 * n) + [_AFTER],
        out_specs=[sem, sem] + [hbm] * (2 * n) + [pl.BlockSpec(memory_space=pltpu.VMEM)],
        input_output_aliases={i: 2 + i for i in range(2 * n)},
        compiler_params=pltpu.CompilerParams(has_side_effects=pltpu.SideEffectType.DATAFLOW_SIDE_EFFECTING),
    )(*[pltpu.with_memory_space_constraint(a, pltpu.HBM) for a in list(xs) + lands], after)
    return dict(ssem=res[0], rsem=res[1], xs=res[2:2 + n], lands=res[2 + n:2 + 2 * n], n=n, scatter=scatter,
                everyone=everyone), res[-1]


def _chip_exchange_wait(handle, after, name):
    n, scatter, everyone = handle["n"], handle["scatter"], handle["everyone"]
    npeer = N_DEV - 1 if everyone else 3

    def body(*refs):
        x_refs, land_refs = refs[:n], refs[n:2 * n]
        ssem, rsem = refs[2 * n], refs[2 * n + 1]
        peers, _ = _split_peers(everyone)
        for i in range(n):
            for j, (px, py, pc, slot) in enumerate(peers):
                cp = pltpu.make_async_remote_copy(
                    src_ref=x_refs[i].at[slot] if scatter else x_refs[i], dst_ref=land_refs[i].at[slot],
                    send_sem=ssem.at[npeer * i + j], recv_sem=rsem.at[npeer * i + j], device_id=(px, py, pc),
                    device_id_type=MESH)
                cp.wait_send()
                cp.wait_recv()

    hbm = pl.BlockSpec(memory_space=pltpu.HBM)
    sem = pl.BlockSpec(memory_space=pltpu.SEMAPHORE)
    ops = list(handle["xs"]) + list(handle["lands"])
    res = pl.pallas_call(
        body, name=name,
        out_shape=[pltpu.HBM(a.shape, a.dtype) for a in ops],
        in_specs=[hbm] * (2 * n) + [sem, sem, pl.BlockSpec(memory_space=pl.ANY)],
        out_specs=[hbm] * (2 * n),
        input_output_aliases={i: i for i in range(2 * n)},
        compiler_params=pltpu.CompilerParams(has_side_effects=pltpu.SideEffectType.DATAFLOW_SIDE_EFFECTING),
    )(*ops, handle["ssem"], handle["rsem"], after)
    me = 2 * lax.axis_index("x") + lax.axis_index("y")
    if everyone:
        me = 2 * me + lax.axis_index("c")
    out = []
    for x, land in zip(res[:n], res[n:]):
        own = lax.dynamic_index_in_dim(x, me, 0, keepdims=False) if scatter else x
        out.append(lax.dynamic_update_index_in_dim(land, own, me, 0))
    return out


def _core_gather(xs, name):
    n = len(xs)

    def body(*refs):
        x_refs, o_refs, mine, got = refs[:n], refs[n:2 * n], refs[2 * n:3 * n], refs[3 * n:4 * n]
        lsem, ssem, rsem, osem = refs[4 * n:]
        mx, my, mc = _my_place()
        loads = [pltpu.make_async_copy(x_refs[i], mine[i], lsem.at[i]) for i in range(n)]
        for cp in loads:
            cp.start()
        sends, stores = [], []
        for i in range(n):
            loads[i].wait()
            cp = pltpu.make_async_remote_copy(
                src_ref=mine[i], dst_ref=got[i], send_sem=ssem.at[i], recv_sem=rsem.at[i],
                device_id=(mx, my, 1 - mc), device_id_type=MESH)
            cp.start()
            sends.append(cp)
            for k in range(4):
                st = pltpu.make_async_copy(mine[i].at[k], o_refs[i].at[k, mc], osem.at[i, k])
                st.start()
                stores.append(st)
        for i in range(n):
            sends[i].wait_recv()
            for k in range(4):
                st = pltpu.make_async_copy(got[i].at[k], o_refs[i].at[k, 1 - mc], osem.at[n + i, k])
                st.start()
                stores.append(st)
        for cp in sends:
            cp.wait_send()
        for st in stores:
            st.wait()

    hbm = pl.BlockSpec(memory_space=pl.ANY)
    bufs = [pltpu.VMEM(x.shape, x.dtype) for x in xs]
    return pl.pallas_call(
        body, name=name,
        out_shape=[_sds((4, 2) + tuple(x.shape[1:]), x.dtype) for x in xs],
        in_specs=[hbm] * n, out_specs=[hbm] * n,
        scratch_shapes=bufs + bufs + [pltpu.SemaphoreType.DMA((n,)), pltpu.SemaphoreType.DMA((n,)),
                                      pltpu.SemaphoreType.DMA((n,)), pltpu.SemaphoreType.DMA((2 * n, 4))],
        compiler_params=pltpu.CompilerParams(vmem_limit_bytes=VMEM_LIMIT),
    )(*xs)


def _rows_step(k):
    for r in (256, 128, 64, 32, 16, 8):
        if k % r == 0:
            return r
    return k


def _core_scatter_sum(gs, name):
    n = len(gs)

    def body(*refs):
        g_refs, o_refs = refs[:n], refs[n:2 * n]
        send, got, mine = refs[2 * n:3 * n], refs[3 * n:4 * n], refs[4 * n:5 * n]
        lsem, msem, ssem, rsem, osem = refs[5 * n:]
        mx, my, mc = _my_place()
        pairs = [(i, k) for i in range(n) for k in range(4)]
        out_loads = {(i, k): pltpu.make_async_copy(g_refs[i].at[k, 1 - mc], send[i].at[k], lsem.at[i, k])
                     for i, k in pairs}
        own_loads = {(i, k): pltpu.make_async_copy(g_refs[i].at[k, mc], mine[i].at[k], msem.at[i, k])
                     for i, k in pairs}
        for p in pairs:
            out_loads[p].start()
        for p in pairs:
            own_loads[p].start()
        sends = []
        for i in range(n):
            for k in range(4):
                out_loads[i, k].wait()
            cp = pltpu.make_async_remote_copy(
                src_ref=send[i], dst_ref=got[i], send_sem=ssem.at[i], recv_sem=rsem.at[i],
                device_id=(mx, my, 1 - mc), device_id_type=MESH)
            cp.start()
            sends.append(cp)
        stores = []
        for i in range(n):
            for k in range(4):
                own_loads[i, k].wait()
            sends[i].wait_recv()
            rows = g_refs[i].shape[2]
            step = _rows_step(rows)

            def add(r, _, i=i, step=step):
                sl = pl.ds(pl.multiple_of(r * step, step), step)
                for k in range(4):
                    mine[i][k, sl, :] = (mine[i][k, sl, :].astype(F32) + got[i][k, sl, :].astype(F32)).astype(BF16)
                return 0

            lax.fori_loop(0, rows // step, add, 0)
            st = pltpu.make_async_copy(mine[i], o_refs[i], osem.at[i])
            st.start()
            stores.append(st)
        for cp in sends:
            cp.wait_send()
        for st in stores:
            st.wait()

    hbm = pl.BlockSpec(memory_space=pl.ANY)
    blk = [(4,) + tuple(g.shape[2:]) for g in gs]
    bufs = [pltpu.VMEM(b, BF16) for b in blk]
    return pl.pallas_call(
        body, name=name,
        out_shape=[_sds(b, BF16) for b in blk],
        in_specs=[hbm] * n, out_specs=[hbm] * n,
        scratch_shapes=bufs * 3 + [pltpu.SemaphoreType.DMA((n, 4)), pltpu.SemaphoreType.DMA((n, 4)),
                                   pltpu.SemaphoreType.DMA((n,)), pltpu.SemaphoreType.DMA((n,)),
                                   pltpu.SemaphoreType.DMA((n,))],
        compiler_params=pltpu.CompilerParams(vmem_limit_bytes=VMEM_LIMIT),
    )(*gs)


def _all_gather(xs, name):
    by_chip = _chip_exchange(xs, name + "_chips", scatter=False)
    both = _core_gather(by_chip, name + "_cores")
    return [b.reshape((N_DEV,) + tuple(x.shape)) for b, x in zip(both, xs)]


def _ada_partial(c_all, w_ada_loc, b_loc):
    def body(c_ref, w_ref, b_ref, act_ref, mod_ref):
        c = c_ref[...]
        act = c * (1.0 / (1.0 + jnp.exp(-c)))
        act_ref[...] = act
        mod_ref[...] = _dot(act.astype(BF16), w_ref[...].astype(BF16)) + b_ref[...]

    nb, d = c_all.shape
    return pl.pallas_call(
        body, name="ada_partial",
        out_shape=(_sds((nb, d), F32), _sds((nb, w_ada_loc.shape[1]), F32)),
        compiler_params=pltpu.CompilerParams(vmem_limit_bytes=VMEM_LIMIT),
    )(c_all, w_ada_loc, b_loc)


_AFTER = pl.BlockSpec(memory_space=pl.ANY)


def _inproj_fwd(x, mod, ln_g, ln_b, w_in_p, w_uq_p, w_kv, gq, gkv, tc, ts1, ts2, dm, after):
    B, S, D = x.shape
    tm = dm["tm"]
    sbw, qr, kvr, nh = dm["sbw"], dm["qr"], dm["kvr"], dm["nh"]
    o_cq, o_ckv, o_kr = 3 * sbw, 3 * sbw + qr, 3 * sbw + qr + kvr
    qpw = nh * HEAD_PAD

    def body(x_ref, mod_ref, g_ref, b_ref, win_ref, wuq_ref, wkv_ref, gq_ref, gkv_ref, tc_ref, ts1_ref, ts2_ref, _,
             x0_ref, h_ref, q_ref, k_ref, v_ref, qp_ref, kp_ref, mv_ref, cq_ref, ckv_ref, qn_ref, kvn_ref):
        x0, _, _ = _ln_fwd(x_ref[0], g_ref[...], b_ref[...])
        x0_ref[0] = x0
        mod = mod_ref[0]
        h = (x0 * (1.0 + mod[1:2]) + mod[0:1]).astype(BF16)
        h_ref[0] = h
        proj = _dot(h, win_ref[...])
        q_ref[0] = (proj[:, 0:sbw] * SB_Q_SCALE).astype(BF16)
        k_ref[0] = proj[:, sbw:2 * sbw].astype(BF16)
        v_ref[0] = proj[:, 2 * sbw:3 * sbw].astype(BF16)
        cq = proj[:, o_cq:o_cq + qr]
        ckv = proj[:, o_ckv:o_ckv + kvr]
        cq_ref[0] = cq
        ckv_ref[0] = ckv
        qn = (cq * lax.rsqrt(jnp.mean(cq * cq, axis=-1, keepdims=True) + RMS_EPS) * gq_ref[...]).astype(BF16)
        kvn = (ckv * lax.rsqrt(jnp.mean(ckv * ckv, axis=-1, keepdims=True) + RMS_EPS) * gkv_ref[...]).astype(BF16)
        qn_ref[0] = qn
        kvn_ref[0] = kvn
        c1, s1, s2 = tc_ref[...], ts1_ref[...], ts2_ref[...]
        c8, s18, s28 = jnp.tile(c1, (1, nh)), jnp.tile(s1, (1, nh)), jnp.tile(s2, (1, nh))
        qp_ref[0] = (_rope(_dot(qn, wuq_ref[...]), c8, s18, s28) * MLA_Q_SCALE).astype(BF16)
        kvo = _dot(kvn, wkv_ref[...])
        kr = pltpu.roll(proj[:, o_kr:o_kr + LANES], 64, 1)
        kr = _rope(kr, c1, s1, s2)
        kp_ref[0] = (kvo[:, 0:qpw] + jnp.tile(kr, (1, nh))).astype(BF16)
        mv_ref[0] = kvo[:, qpw:].astype(BF16)

    tab = pl.BlockSpec((tm, LANES), lambda b, s: (s, 0))
    outs = [(D, F32), (D, BF16), (sbw, BF16), (sbw, BF16), (sbw, BF16), (qpw, BF16), (qpw, BF16),
            (nh * MLA_V, BF16), (qr, F32), (kvr, F32), (qr, BF16), (kvr, BF16)]
    return pl.pallas_call(
        body, name="inproj_fwd", grid=(B, S // tm),
        in_specs=[_tok(tm, D), _perb(N_MOD, D), _full(ln_g), _full(ln_b), _full(w_in_p), _full(w_uq_p),
                  _full(w_kv), _full(gq), _full(gkv), tab, tab, tab, _AFTER],
        out_specs=[_tok(tm, w) for w, _ in outs],
        out_shape=[_sds((B, S, w), t) for w, t in outs],
        compiler_params=_cparams(("parallel", "parallel")),
    )(x, mod, ln_g, ln_b, w_in_p, w_uq_p, w_kv, gq, gkv, tc, ts1, ts2, after)


def _neg_abs(x):
    sign = jnp.uint32(0x80000000)
    return lax.bitcast_convert_type(lax.bitcast_convert_type(x, jnp.uint32) | sign, F32)


SB_Q_SCALE = -(SB_HD ** -0.5) * LOG2E
MLA_Q_SCALE = (MLA_NOPE + MLA_ROPE) ** -0.5 * LOG2E


def _log2_keep(zs):
    return jnp.minimum(zs, 0.0) - jnp.log2(1.0 + jnp.exp2(_neg_abs(zs)))


def _split_dot(a, u):
    hi = a.astype(BF16)
    lo = (a - hi.astype(F32)).astype(BF16)
    return _dot(jnp.concatenate([hi, lo], axis=1), jnp.concatenate([u, u], axis=0))


def _tri(n, rel):
    row = lax.broadcasted_iota(jnp.int32, (n, n), 0)
    col = lax.broadcasted_iota(jnp.int32, (n, n), 1)
    return rel(row, col).astype(BF16)


def _running_sum(a, tri, reverse, split):
    cs = tri.shape[0]
    n = a.shape[1] // cs
    out = [None] * n
    run = None
    for c in (reversed(range(n)) if reverse else range(n)):
        part = a[:, c * cs:(c + 1) * cs]
        loc = _split_dot(part, tri) if split else _dot(part.astype(BF16), tri)
        out[c] = loc if run is None else loc + run
        tot = jnp.sum(part, axis=1, keepdims=True)
        run = tot if run is None else run + tot
    return (out[0] if n == 1 else jnp.concatenate(out, axis=1)), run


def _transpose_bf16(a):
    return a.astype(F32).T.astype(BF16)


def _tile_mask(nr, nk, r0, c0, rel):
    row = lax.broadcasted_iota(jnp.int32, (nr, nk), 0) + r0
    col = lax.broadcasted_iota(jnp.int32, (nr, nk), 1) + c0
    return rel(row, col)


def _put_rows(whole, part, r0):
    return part if r0 == 0 else jnp.concatenate([whole[:r0], part], axis=0)


def _diag_tiles(tq, split):
    half = tq // 2
    return [(0, tq, 0, half), (half, half, half, half)] if split else [(0, tq, 0, tq)]


def _sb_fwd(q, k, v, dm):
    B, S, W = q.shape
    tq = dm["tq"]
    nq = S // tq

    def body(q_ref, k_ref, v_ref, y_ref, tot_ref):
        qi = pl.program_id(2)
        q2 = q_ref[0]
        lane = lax.broadcasted_iota(jnp.int32, (tq, LANES), 1)
        qs = jnp.concatenate([jnp.where(lane < SB_HD, q2, 0), jnp.where(lane >= SB_HD, q2, 0)], axis=0).astype(BF16)
        later = _tri(min(tq, CUMSUM_W), lambda a, b: a > b)
        assert tq & (tq - 1) == 0
        strict = _tile_mask(2 * tq, tq, 0, 0, lambda t, s: s < (t & (tq - 1)))

        def block(j, carry, masked):
            acc, run = carry
            off = pl.multiple_of(j * tq, tq)
            zs = _dot_nt(qs, k_ref[0, pl.ds(off, tq), :])
            a = _log2_keep(zs)
            if masked:
                a = jnp.where(strict, a, 0.0)
            a_later, a_tot = _running_sum(a, later, reverse=True, split=True)
            w = jnp.exp2((a - zs) + a_later + run)
            if masked:
                w = jnp.where(strict, w, 0.0)
            return acc + _dot(w.astype(BF16), v_ref[0, pl.ds(off, tq), :]), run + a_tot

        carry = block(qi, (jnp.zeros((2 * tq, LANES), F32), jnp.zeros((2 * tq, 1), F32)), True)
        acc, run = lax.fori_loop(0, qi, lambda jj, c: block(qi - 1 - jj, c, False), carry)
        y_ref[0] = jnp.where(lane < SB_HD, acc[:tq], acc[tq:]).astype(BF16)
        tot_ref[0] = jnp.where(lane < SB_HD, run[:tq], run[tq:])

    qspec = pl.BlockSpec((1, tq, LANES), lambda b, hp, i: (b, i, hp))
    kspec = pl.BlockSpec((1, S, LANES), lambda b, hp, i: (b, 0, hp))
    return pl.pallas_call(
        body, name="sb_fwd", grid=(B, W // LANES, nq),
        in_specs=[qspec, kspec, kspec],
        out_specs=[qspec, qspec],
        out_shape=[_sds((B, S, W), BF16), _sds((B, S, W), F32)],
        compiler_params=_cparams(("parallel", "parallel", "arbitrary")),
    )(q, k, v)


def _sb_bwd(q, k, v, tot, dy, dm, after):
    B, S, W = q.shape
    tq = dm["tq"]
    nq = S // tq

    def body(q_ref, k_ref, v_ref, tot_ref, dy_ref, _, dq_ref, dk_ref, dv_ref, dk_acc, dv_acc):
        qi = pl.program_id(2)

        @pl.when(qi == 0)
        def _():
            dk_acc[...] = jnp.zeros_like(dk_acc)
            dv_acc[...] = jnp.zeros_like(dv_acc)

        q2 = q_ref[0]
        dy2 = dy_ref[0]
        tot2 = tot_ref[0]
        lane = lax.broadcasted_iota(jnp.int32, (tq, LANES), 1)
        in_h = [lane < SB_HD, lane >= SB_HD]
        qh = [jnp.where(m, q2, 0).astype(BF16) for m in in_h]
        dyh = [jnp.where(m, dy2, 0).astype(BF16) for m in in_h]
        q_t = [_transpose_bf16(a) for a in qh]
        dy_t = [_transpose_bf16(a) for a in dyh]
        toth = [tot2[:, 0:1], tot2[:, SB_HD:SB_HD + 1]]

        def tile(j, carry, r0, nr, c0, nk, masked):
            off = pl.multiple_of(j * tq + c0, math.gcd(tq, c0))
            k2 = k_ref[0, pl.ds(off, nk), :]
            v2 = v_ref[0, pl.ds(off, nk), :]
            upto = _tri(min(nk, CUMSUM_W), lambda a, b: a <= b)
            before = _tri(min(nk, CUMSUM_W), lambda a, b: a < b)
            strict = _tile_mask(nr, nk, r0, c0, lambda t, s: s < t) if masked else None
            rows = slice(r0, r0 + nr)
            new = []
            dk_blk = jnp.zeros((LANES, nk), F32)
            dv_blk = jnp.zeros((LANES, nk), F32)
            for h in range(2):
                dq, pa, pg = carry[3 * h][rows], carry[3 * h + 1][rows], carry[3 * h + 2][rows]
                zs = _dot_nt(qh[h][rows], k2)
                a = _log2_keep(zs)
                if masked:
                    a = jnp.where(strict, a, 0.0)
                a_upto, a_tot = _running_sum(a, upto, reverse=False, split=True)
                w = jnp.exp2((a - zs) + ((toth[h][rows] - pa) - a_upto))
                if masked:
                    w = jnp.where(strict, w, 0.0)
                g = _dot_nt(dyh[h][rows], v2) * w
                g_before, g_tot = _running_sum(g, before, reverse=False, split=False)
                g_before = g_before + pg
                dz = (g + g_before) * jnp.exp2(a) - g_before
                if masked:
                    dz = jnp.where(strict, dz, 0.0)
                dzb = dz.astype(BF16)
                dv_blk = dv_blk + _dot(dy_t[h][:, rows], w.astype(BF16))
                dk_blk = dk_blk + _dot(q_t[h][:, rows], dzb)
                new += [_put_rows(carry[3 * h], dq + _dot(dzb, k2), r0), _put_rows(carry[3 * h + 1], pa + a_tot, r0),
                        _put_rows(carry[3 * h + 2], pg + g_tot, r0)]
            dk_acc[j, :, c0:c0 + nk] += dk_blk
            dv_acc[j, :, c0:c0 + nk] += dv_blk
            return tuple(new)

        zero = jnp.zeros((tq, LANES), F32)
        zrun = jnp.zeros((tq, 1), F32)
        carry = lax.fori_loop(0, qi, lambda j, c: tile(j, c, 0, tq, 0, tq, False),
                              (zero, zrun, zrun, zero, zrun, zrun))
        for r0, nr, c0, nk in _diag_tiles(tq, False):
            carry = tile(qi, carry, r0, nr, c0, nk, True)
        dq_ref[0] = (jnp.where(in_h[0], carry[0], carry[3]) * (SB_HD ** -0.5)).astype(BF16)

        @pl.when(qi == nq - 1)
        def _():
            for jb in range(nq):
                dk_ref[0, jb * tq:(jb + 1) * tq, :] = (dk_acc[jb].T * (-1.0 / LOG2E)).astype(BF16)
                dv_ref[0, jb * tq:(jb + 1) * tq, :] = dv_acc[jb].T.astype(BF16)

    qspec = pl.BlockSpec((1, tq, LANES), lambda b, hp, i: (b, i, hp))
    kspec = pl.BlockSpec((1, S, LANES), lambda b, hp, i: (b, 0, hp))
    return pl.pallas_call(
        body, name="sb_bwd", grid=(B, W // LANES, nq),
        in_specs=[qspec, kspec, kspec, qspec, qspec, _AFTER],
        out_specs=[qspec, kspec, kspec],
        out_shape=[_sds((B, S, W), BF16)] * 3,
        scratch_shapes=[pltpu.VMEM((nq, LANES, tq), F32), pltpu.VMEM((nq, LANES, tq), F32)],
        compiler_params=_cparams(("parallel", "parallel", "arbitrary")),
    )(q, k, v, tot, dy, after)


def _same_or_earlier_chunk(row, col):
    return lax.shift_right_logical(col, 6) <= lax.shift_right_logical(row, 6)


def _mla_fwd(qp, kp, mv, dm, after):
    B, S, QW = qp.shape
    VW = mv.shape[2]
    tq = dm["tq"]
    nq = S // tq
    assert CHUNK == 64

    def body(q_ref, k_ref, v_ref, _, y_ref, lse_ref):
        qi = pl.program_id(2)
        q2 = q_ref[0]
        lane = lax.broadcasted_iota(jnp.int32, (tq, LANES), 1)

        def tile(j, carry, r0, nr, c0, nk, masked):
            off = pl.multiple_of(j * tq + c0, math.gcd(tq, c0))
            v2 = v_ref[0, pl.ds(off, nk), :]
            allowed = _tile_mask(nr, nk, r0, c0, _same_or_earlier_chunk) if masked else None
            rows = slice(r0, r0 + nr)
            heads = range(2)
            sl = [slice(h * HEAD_PAD, (h + 1) * HEAD_PAD) for h in heads]
            m_old = [carry[3 * h + 1][rows] for h in heads]
            s = [_dot_nt(q2[rows, sl[h]], k_ref[0, pl.ds(off, nk), sl[h]]) for h in heads]
            if masked:
                s = [jnp.where(allowed, s[h], -1e30) for h in heads]
            m_new = [jnp.maximum(m_old[h], jnp.max(s[h], axis=1, keepdims=True)) for h in heads]
            alpha = [jnp.exp2(m_old[h] - m_new[h]) for h in heads]
            p = [jnp.exp2(s[h] - m_new[h]) for h in heads]
            acc = [alpha[h] * carry[3 * h][rows] + _dot(p[h].astype(BF16), v2) for h in heads]
            l = [alpha[h] * carry[3 * h + 2][rows] + jnp.sum(p[h], axis=1, keepdims=True) for h in heads]
            out = []
            for h in heads:
                out += [_put_rows(carry[3 * h], acc[h], r0), _put_rows(carry[3 * h + 1], m_new[h], r0),
                        _put_rows(carry[3 * h + 2], l[h], r0)]
            return tuple(out)

        zero = jnp.zeros((tq, LANES), F32)
        m0 = jnp.full((tq, 1), -1e30, F32)
        l0 = jnp.zeros((tq, 1), F32)
        carry = (zero, m0, l0, zero, m0, l0)
        for r0, nr, c0, nk in _diag_tiles(tq, False):
            carry = tile(qi, carry, r0, nr, c0, nk, True)
        carry = lax.fori_loop(0, qi, lambda j, c: tile(j, c, 0, tq, 0, tq, False), carry)
        y0 = carry[0] / carry[2]
        y1 = carry[3] / carry[5]
        y_ref[0] = jnp.where(lane < MLA_V, y0, y1).astype(BF16)
        lse_ref[0] = jnp.where(lane < MLA_V, carry[1] + jnp.log2(carry[2]), carry[4] + jnp.log2(carry[5]))

    qspec = pl.BlockSpec((1, tq, 2 * HEAD_PAD), lambda b, hp, i: (b, i, hp))
    kspec = pl.BlockSpec((1, S, 2 * HEAD_PAD), lambda b, hp, i: (b, 0, hp))
    vspec = pl.BlockSpec((1, S, LANES), lambda b, hp, i: (b, 0, hp))
    yspec = pl.BlockSpec((1, tq, LANES), lambda b, hp, i: (b, i, hp))
    return pl.pallas_call(
        body, name="mla_fwd", grid=(B, VW // LANES, nq),
        in_specs=[qspec, kspec, vspec, _AFTER],
        out_specs=[yspec, yspec],
        out_shape=[_sds((B, S, VW), BF16), _sds((B, S, VW), F32)],
        compiler_params=_cparams(("parallel", "parallel", "arbitrary")),
    )(qp, kp, mv, after)


def _mla_bwd(qp, kp, mv, y, lse, dy, dm, after):
    B, S, QW = qp.shape
    VW = mv.shape[2]
    tq = dm["tq"]
    nq = S // tq
    scale = (MLA_NOPE + MLA_ROPE) ** -0.5

    def body(q_ref, k_ref, v_ref, y_ref, lse_ref, dy_ref, _, dq_ref, dk_ref, dv_ref, dk_acc, dv_acc):
        qi = pl.program_id(2)

        @pl.when(qi == 0)
        def _():
            dk_acc[...] = jnp.zeros_like(dk_acc)
            dv_acc[...] = jnp.zeros_like(dv_acc)

        q2 = q_ref[0]
        dy2 = dy_ref[0]
        lse2 = lse_ref[0]
        lane = lax.broadcasted_iota(jnp.int32, (tq, LANES), 1)
        in_h = [lane < MLA_V, lane >= MLA_V]
        prod = dy2.astype(F32) * y_ref[0].astype(F32)
        delta = [jnp.sum(jnp.where(m, prod, 0.0), axis=1, keepdims=True) for m in in_h]
        dyh = [jnp.where(m, dy2, 0).astype(BF16) for m in in_h]
        lseh = [lse2[:, 0:1], lse2[:, MLA_V:MLA_V + 1]]
        q_t = _transpose_bf16(q2)
        dy_t = [_transpose_bf16(a) for a in dyh]

        def tile(j, carry, r0, nr, c0, nk, masked):
            off = pl.multiple_of(j * tq + c0, math.gcd(tq, c0))
            v2 = v_ref[0, pl.ds(off, nk), :]
            allowed = _tile_mask(nr, nk, r0, c0, _same_or_earlier_chunk) if masked else None
            rows = slice(r0, r0 + nr)
            keys = slice(c0, c0 + nk)
            heads = range(2)
            sl = [slice(h * HEAD_PAD, (h + 1) * HEAD_PAD) for h in heads]
            qhh = [q2[rows, sl[h]] for h in heads]
            dyr = [dyh[h][rows] for h in heads]
            kh = [k_ref[0, pl.ds(off, nk), sl[h]] for h in heads]
            s = [_dot_nt(qhh[h], kh[h]) for h in heads]
            dp = [_dot_nt(dyr[h], v2) for h in heads]
            if masked:
                s = [jnp.where(allowed, s[h], -1e30) for h in heads]
            p = [jnp.exp2(s[h] - lseh[h][rows]) for h in heads]
            dv_acc[j, :, keys] += (_dot(dy_t[0][:, rows], p[0].astype(BF16))
                                   + _dot(dy_t[1][:, rows], p[1].astype(BF16)))
            ds = [(p[h] * (dp[h] - delta[h][rows])).astype(BF16) for h in heads]
            for h in heads:
                dk_acc[j, sl[h], keys] += _dot(q_t[sl[h], rows], ds[h])
            return tuple(_put_rows(carry[h], carry[h][rows] + _dot(ds[h], kh[h]), r0) for h in heads)

        zero = jnp.zeros((tq, HEAD_PAD), F32)
        carry = lax.fori_loop(0, qi, lambda j, c: tile(j, c, 0, tq, 0, tq, False), (zero, zero))
        for r0, nr, c0, nk in _diag_tiles(tq, True):
            carry = tile(qi, carry, r0, nr, c0, nk, True)
        dq_ref[0] = (jnp.concatenate([carry[0], carry[1]], axis=1) * scale).astype(BF16)

        @pl.when(qi == nq - 1)
        def _():
            for jb in range(nq):
                dk_ref[0, jb * tq:(jb + 1) * tq, :] = (dk_acc[jb].T * (1.0 / LOG2E)).astype(BF16)
                dv_ref[0, jb * tq:(jb + 1) * tq, :] = dv_acc[jb].T.astype(BF16)

    qspec = pl.BlockSpec((1, tq, 2 * HEAD_PAD), lambda b, hp, i: (b, i, hp))
    kspec = pl.BlockSpec((1, S, 2 * HEAD_PAD), lambda b, hp, i: (b, 0, hp))
    vspec = pl.BlockSpec((1, S, LANES), lambda b, hp, i: (b, 0, hp))
    yspec = pl.BlockSpec((1, tq, LANES), lambda b, hp, i: (b, i, hp))
    return pl.pallas_call(
        body, name="mla_bwd", grid=(B, VW // LANES, nq),
        in_specs=[qspec, kspec, vspec, yspec, yspec, yspec, _AFTER],
        out_specs=[qspec, kspec, vspec],
        out_shape=[_sds((B, S, QW), BF16), _sds((B, S, QW), BF16), _sds((B, S, VW), BF16)],
        scratch_shapes=[pltpu.VMEM((nq, 2 * HEAD_PAD, tq), F32), pltpu.VMEM((nq, LANES, tq), F32)],
        compiler_params=_cparams(("parallel", "parallel", "arbitrary")),
    )(qp, kp, mv, y, lse, dy, after)


def _outproj_fwd(sb_y, mla_y, x0, mod, w_o, ln_g, ln_b, dm):
    B, S, D = x0.shape
    tm = dm["tm"]
    sbw = sb_y.shape[2]

    def body(ya_ref, yb_ref, x0_ref, mod_ref, wo_ref, g_ref, b_ref, mix_ref, x1_ref, h2_ref):
        mod = mod_ref[0]
        mix = _dot(ya_ref[0], wo_ref[0:sbw, :]) + _dot(yb_ref[0], wo_ref[sbw:, :])
        mix_ref[0] = mix
        x1, _, _ = _ln_fwd(ALPHA * x0_ref[0] + (1.0 + mod[2:3]) * mix, g_ref[...], b_ref[...])
        x1_ref[0] = x1
        h2_ref[0] = (x1 * (1.0 + mod[4:5]) + mod[3:4]).astype(BF16)

    return pl.pallas_call(
        body, name="outproj_fwd", grid=(B, S // tm),
        in_specs=[_tok(tm, sbw), _tok(tm, mla_y.shape[2]), _tok(tm, D), _perb(N_MOD, D),
                  _full(w_o), _full(ln_g), _full(ln_b)],
        out_specs=[_tok(tm, D)] * 3,
        out_shape=[_sds((B, S, D), F32), _sds((B, S, D), F32), _sds((B, S, D), BF16)],
        compiler_params=_cparams(("parallel", "parallel")),
    )(sb_y, mla_y, x0, mod, w_o, ln_g, ln_b)


def _stat_specs(B, D):
    specs = [pl.BlockSpec((1, 8, D), lambda b, s: (b, 0, 0)), pl.BlockSpec((8, D), lambda b, s: (0, 0))]
    shapes = [_sds((B, 8, D), F32), _sds((8, D), F32)]
    return specs, shapes


def _stat_init(bst_ref, wst_ref):
    @pl.when(pl.program_id(1) == 0)
    def _():
        bst_ref[...] = jnp.zeros_like(bst_ref)

    @pl.when((pl.program_id(0) == 0) & (pl.program_id(1) == 0))
    def _():
        wst_ref[...] = jnp.zeros_like(wst_ref)


def _mlp_fwd(h2, x1, mod, target, w_up, w_down, ln_g, ln_b, dm):
    B, S, D = x1.shape
    tm = dm["tm"]
    nck, _, ck = w_up.shape
    dff = nck * ck

    def body(h2_ref, x1_ref, mod_ref, t_ref, wu_ref, wd_ref, g_ref, b_ref, u_ref, dr_ref, bst_ref, wst_ref):
        _stat_init(bst_ref, wst_ref)
        mod = mod_ref[0]
        g = g_ref[...]
        h2 = h2_ref[0]
        ff = jnp.zeros((tm, D), F32)
        for c in range(nck):
            u = _dot(h2, wu_ref[c])
            u_ref[0, :, c * ck:(c + 1) * ck] = u.astype(BF16)
            act = jnp.square(jnp.maximum(u, 0.0)).astype(BF16)
            ff = ff + _dot(act, wd_ref[c])
        x2, xhat, rstd = _ln_fwd(ALPHA * x1_ref[0] + (1.0 + mod[5:6]) * ff, g, b_ref[...])
        err = x2 - t_ref[0]
        dy = err * (1.0 / D)
        dr = _ln_bwd(dy, xhat, rstd, g)
        dr_ref[0] = dr
        bst_ref[0, 0:1, :] += _colsum(dr * ff)
        wst_ref[0:1, :] += _colsum(dy * xhat)
        wst_ref[1:2, :] += _colsum(dy)
        wst_ref[2:3, :] += _colsum(err * err) * (0.5 / D)

    sspecs, sshapes = _stat_specs(B, D)
    return pl.pallas_call(
        body, name="mlp_fwd", grid=(B, S // tm),
        in_specs=[_tok(tm, D), _tok(tm, D), _perb(N_MOD, D), _tok(tm, D), _full(w_up), _full(w_down),
                  _full(ln_g), _full(ln_b)],
        out_specs=[_tok(tm, dff), _tok(tm, D)] + sspecs,
        out_shape=[_sds((B, S, dff), BF16), _sds((B, S, D), F32)] + sshapes,
        compiler_params=_cparams(("arbitrary", "arbitrary")),
    )(h2, x1, mod, target, w_up, w_down, ln_g, ln_b)


def _mlp_bwd(dr2, u, x1, x0, mix, mod, w_up, w_down, w_o, ln_g, dm):
    B, S, D = x1.shape
    tm = dm["tm_small"]
    sbw = dm["sbw"]
    nck, _, ck = w_up.shape
    dff = nck * ck

    def body(dr_ref, u_ref, x1_ref, x0_ref, mix_ref, mod_ref, wu_ref, wd_ref, wo_ref, g_ref,
             du_ref, dff_ref, dmix_ref, dx0_ref, dya_ref, dyb_ref, bst_ref, wst_ref):
        _stat_init(bst_ref, wst_ref)
        mod = mod_ref[0]
        dr2 = dr_ref[0]
        dffv = ((1.0 + mod[5:6]) * dr2).astype(BF16)
        dff_ref[0] = dffv
        dh2 = jnp.zeros((tm, D), F32)
        for c in range(nck):
            sl = slice(c * ck, (c + 1) * ck)
            da = _dot_nt(dffv, wd_ref[c])
            du = (da * (2.0 * jnp.maximum(u_ref[0, :, sl].astype(F32), 0.0))).astype(BF16)
            du_ref[0, :, sl] = du
            dh2 = dh2 + _dot_nt(du, wu_ref[c])
        x1 = x1_ref[0]
        dx1 = ALPHA * dr2 + dh2 * (1.0 + mod[4:5])
        bst_ref[0, 0:1, :] += _colsum(dh2 * x1)
        bst_ref[0, 1:2, :] += _colsum(dh2)
        mix = mix_ref[0]
        g = g_ref[...]
        _, xhat, rstd = _ln_fwd(ALPHA * x0_ref[0] + (1.0 + mod[2:3]) * mix, g, 0.0)
        dr1 = _ln_bwd(dx1, xhat, rstd, g)
        wst_ref[0:1, :] += _colsum(dx1 * xhat)
        wst_ref[1:2, :] += _colsum(dx1)
        bst_ref[0, 2:3, :] += _colsum(dr1 * mix)
        dx0_ref[0] = ALPHA * dr1
        dmix = ((1.0 + mod[2:3]) * dr1).astype(BF16)
        dmix_ref[0] = dmix
        dya_ref[0] = _dot_nt(dmix, wo_ref[0:sbw, :]).astype(BF16)
        dyb_ref[0] = _dot_nt(dmix, wo_ref[sbw:, :]).astype(BF16)

    sspecs, sshapes = _stat_specs(B, D)
    wa, wb = sbw, w_o.shape[0] - sbw
    return pl.pallas_call(
        body, name="mlp_bwd", grid=(B, S // tm),
        in_specs=[_tok(tm, D), _tok(tm, dff), _tok(tm, D), _tok(tm, D), _tok(tm, D), _perb(N_MOD, D),
                  _full(w_up), _full(w_down), _full(w_o), _full(ln_g)],
        out_specs=[_tok(tm, dff), _tok(tm, D), _tok(tm, D), _tok(tm, D), _tok(tm, wa), _tok(tm, wb)] + sspecs,
        out_shape=[_sds((B, S, dff), BF16), _sds((B, S, D), BF16), _sds((B, S, D), BF16), _sds((B, S, D), F32),
                   _sds((B, S, wa), BF16), _sds((B, S, wb), BF16)] + sshapes,
        compiler_params=_cparams(("arbitrary", "arbitrary")),
    )(dr2, u, x1, x0, mix, mod, w_up, w_down, w_o, ln_g)


def _inproj_bwd(x, x0, dx0a, mod, ln_g, dq, dk, dv, dqp, dkp, dmv, cq, ckv, w_in_p, w_uq_p, w_kv, gq, gkv,
                tc, ts1, ts2, dm):
    B, S, D = x.shape
    tm = dm["tm"]
    sbw, qr, kvr, nh = dm["sbw"], dm["qr"], dm["kvr"], dm["nh"]
    qpw = nh * HEAD_PAD
    dinp = w_in_p.shape[1]
    kvw = w_kv.shape[1]

    def body(x_ref, x0_ref, dx0a_ref, mod_ref, g_ref, dq_ref, dk_ref, dv_ref, dqp_ref, dkp_ref, dmv_ref,
             cq_ref, ckv_ref, win_ref, wuq_ref, wkv_ref, gq_ref, gkv_ref, tc_ref, ts1_ref, ts2_ref,
             gx_ref, dproj_ref, dqpre_ref, dkvo_ref, bst_ref, wst_ref):
        _stat_init(bst_ref, wst_ref)
        mod = mod_ref[0]
        c1, s1, s2 = tc_ref[...], ts1_ref[...], ts2_ref[...]
        c8, s18, s28 = jnp.tile(c1, (1, nh)), jnp.tile(s1, (1, nh)), jnp.tile(s2, (1, nh))
        dqpre = _rope_t(dqp_ref[0].astype(F32), c8, s18, s28).astype(BF16)
        dqpre_ref[0] = dqpre
        gq = gq_ref[...]
        cq = cq_ref[0]
        rq = lax.rsqrt(jnp.mean(cq * cq, axis=-1, keepdims=True) + RMS_EPS)
        dqn = _dot_nt(dqpre, wuq_ref[...])
        wst_ref[4:5, 0:qr] += _colsum(dqn * cq * rq)
        dqg = dqn * gq
        dcq = rq * dqg - cq * (rq * rq * rq) * jnp.mean(dqg * cq, axis=-1, keepdims=True)

        dkpre = _rope_t(dkp_ref[0].astype(F32), c8, s18, s28)
        dkr = dkpre[:, 0:HEAD_PAD]
        for h in range(1, nh):
            dkr = dkr + dkpre[:, h * HEAD_PAD:(h + 1) * HEAD_PAD]
        lane = lax.broadcasted_iota(jnp.int32, (tm, LANES), 1)
        dkr = jnp.where((lane >= MLA_NOPE) & (lane < MLA_NOPE + MLA_ROPE), dkr, 0.0)
        dkr = pltpu.roll(dkr, LANES - MLA_NOPE, 1)
        dkvo = jnp.concatenate([dkpre.astype(BF16), dmv_ref[0]], axis=1)
        dkvo_ref[0] = dkvo
        gkv = gkv_ref[...]
        ckv = ckv_ref[0]
        rkv = lax.rsqrt(jnp.mean(ckv * ckv, axis=-1, keepdims=True) + RMS_EPS)
        dkvn = _dot_nt(dkvo, wkv_ref[...])
        wst_ref[5:6, 0:kvr] += _colsum(dkvn * ckv * rkv)
        dkg = dkvn * gkv
        dckv = rkv * dkg - ckv * (rkv * rkv * rkv) * jnp.mean(dkg * ckv, axis=-1, keepdims=True)

        dproj = jnp.concatenate([dq_ref[0], dk_ref[0], dv_ref[0], dcq.astype(BF16), dckv.astype(BF16),
                                 dkr.astype(BF16)], axis=1)
        dproj_ref[0] = dproj
        dh = _dot_nt(dproj, win_ref[...])
        x0 = x0_ref[0]
        dx0 = dx0a_ref[0] + dh * (1.0 + mod[1:2])
        bst_ref[0, 0:1, :] += _colsum(dh * x0)
        bst_ref[0, 1:2, :] += _colsum(dh)
        g = g_ref[...]
        _, xhat, rstd = _ln_fwd(x_ref[0], g, 0.0)
        gx_ref[0] = _ln_bwd(dx0, xhat, rstd, g)
        wst_ref[0:1, :] += _colsum(dx0 * xhat)
        wst_ref[1:2, :] += _colsum(dx0)

    tab = pl.BlockSpec((tm, LANES), lambda b, s: (s, 0))
    sspecs, sshapes = _stat_specs(B, D)
    return pl.pallas_call(
        body, name="inproj_bwd", grid=(B, S // tm),
        in_specs=[_tok(tm, D), _tok(tm, D), _tok(tm, D), _perb(N_MOD, D), _full(ln_g),
                  _tok(tm, sbw), _tok(tm, sbw), _tok(tm, sbw), _tok(tm, qpw), _tok(tm, qpw), _tok(tm, nh * MLA_V),
                  _tok(tm, qr), _tok(tm, kvr), _full(w_in_p), _full(w_uq_p), _full(w_kv), _full(gq), _full(gkv),
                  tab, tab, tab],
        out_specs=[_tok(tm, D), _tok(tm, dinp), _tok(tm, qpw), _tok(tm, kvw)] + sspecs,
        out_shape=[_sds((B, S, D), F32), _sds((B, S, dinp), BF16), _sds((B, S, qpw), BF16),
                   _sds((B, S, kvw), BF16)] + sshapes,
        compiler_params=_cparams(("arbitrary", "arbitrary")),
    )(x, x0, dx0a, mod, ln_g, dq, dk, dv, dqp, dkp, dmv, cq, ckv, w_in_p, w_uq_p, w_kv, gq, gkv, tc, ts1, ts2)


def _tile_of(n, cap):
    if n <= cap:
        return n
    best = n
    for t in range(LANES, cap + 1, LANES):
        if n % t == 0:
            best = t
    return best


def _mm_tn(a, g, name, after, relu_sq=False, out_dtype=F32, col_blocks=None):
    T, K = a.shape
    N = g.shape[1]
    tt = 1024 if T % 1024 == 0 else (512 if T % 512 == 0 else T)
    tk = _tile_of(K, 1024)
    tn = _tile_of(N, 1280)
    nt = T // tt
    bw = N // col_blocks if col_blocks else tn
    assert tn % bw == 0

    def body(a_ref, g_ref, _, o_ref, acc_ref):
        @pl.when(pl.program_id(2) == 0)
        def _():
            acc_ref[...] = jnp.zeros_like(acc_ref)

        av = a_ref[...]
        if relu_sq:
            av = jnp.square(jnp.maximum(av.astype(F32), 0.0)).astype(BF16)
        acc_ref[...] += _dot_tn(av, g_ref[...])

        @pl.when(pl.program_id(2) == nt - 1)
        def _():
            if col_blocks:
                for c in range(tn // bw):
                    o_ref[c] = acc_ref[:, c * bw:(c + 1) * bw].astype(out_dtype)
            else:
                o_ref[...] = acc_ref[...].astype(out_dtype)

    if col_blocks:
        out_spec = pl.BlockSpec((tn // bw, tk, bw), lambda i, j, t: (j, i, 0))
        out_shape = _sds((col_blocks, K, bw), out_dtype)
    else:
        out_spec = pl.BlockSpec((tk, tn), lambda i, j, t: (i, j))
        out_shape = _sds((K, N), out_dtype)
    return pl.pallas_call(
        body, name=name, grid=(K // tk, N // tn, nt),
        in_specs=[pl.BlockSpec((tt, tk), lambda i, j, t: (t, i)), pl.BlockSpec((tt, tn), lambda i, j, t: (t, j)),
                  _AFTER],
        out_specs=out_spec, out_shape=out_shape,
        scratch_shapes=[pltpu.VMEM((tk, tn), F32)],
        compiler_params=_cparams(("parallel", "parallel", "arbitrary")),
    )(a, g, after)


def _reduce_adamw(parts, w, m, v, name):
    P, K, N = parts.shape
    tr = 256 if K % 256 == 0 else K

    def body(p_ref, w_ref, m_ref, v_ref, g_ref, d_ref, nm_ref, nv_ref):
        g = p_ref[0].astype(F32)
        for k in range(1, P):
            g = g + p_ref[k].astype(F32)
        g_ref[0] = g
        d_ref[0], nm_ref[0], nv_ref[0] = _adamw(w_ref[0], g, m_ref[0], v_ref[0])

    spec = pl.BlockSpec((1, tr, N), lambda r: (0, r, 0))
    return pl.pallas_call(
        body, name=name, grid=(K // tr,),
        in_specs=[pl.BlockSpec((P, tr, N), lambda r: (0, r, 0)), spec, spec, spec],
        out_specs=[spec] * 4, out_shape=[_sds((1, K, N), F32)] * 4,
        compiler_params=_cparams(("parallel",)),
    )(parts, w, m, v)


def _finish(sm, dmod_all, dmod_my, cact_all, p_small, m_small, v_small, b_ada, m_b, v_b, w_ada, m_w, v_w):
    n0 = p_small.shape[1]
    n1 = sm.shape[1]
    d = cact_all.shape[1]

    def body(sm_ref, dma_ref, dmm_ref, ca_ref, p_ref, pm_ref, pv_ref, b_ref, bm_ref, bv_ref, w_ref, wm_ref, wv_ref,
             gs_ref, ds_ref, ms_ref, vs_ref, gb_ref, db_ref, mb_ref, vb_ref, gw_ref, dw_ref, mw_ref, vw_ref,
             loss_ref):
        gs = sm_ref[0:1, :]
        for k in range(1, N_DEV):
            gs = gs + sm_ref[k:k + 1, :]
        gs_ref[...] = gs
        ds_ref[...], ms_ref[...], vs_ref[...] = _adamw(p_ref[...], gs[:, 0:n0], pm_ref[...], pv_ref[...])
        loss_ref[...] = jnp.zeros((1, LANES), F32) + jnp.sum(gs[:, n1 - d:n1])
        gb = jnp.sum(dma_ref[...], axis=0, keepdims=True)
        gb_ref[...] = gb
        db_ref[...], mb_ref[...], vb_ref[...] = _adamw(b_ref[...], gb, bm_ref[...], bv_ref[...])
        gw = _dot_tn(ca_ref[...].astype(BF16), dmm_ref[...].astype(BF16))
        gw_ref[...] = gw
        dw_ref[...], mw_ref[...], vw_ref[...] = _adamw(w_ref[...], gw, wm_ref[...], wv_ref[...])

    s0 = _sds(p_small.shape, F32)
    sb = _sds(b_ada.shape, F32)
    sw = _sds(w_ada.shape, F32)
    return pl.pallas_call(
        body, name="finish_small",
        out_shape=[_sds((1, n1), F32), s0, s0, s0, sb, sb, sb, sb, sw, sw, sw, sw,
                   _sds((1, LANES), F32)],
        compiler_params=pltpu.CompilerParams(vmem_limit_bytes=VMEM_LIMIT),
    )(sm, dmod_all, dmod_my, cact_all, p_small, m_small, v_small, b_ada, m_b, v_b, w_ada, m_w, v_w)


def _pack(arrs, dtype, width):
    flat = jnp.concatenate([a.astype(dtype).reshape(-1) for a in arrs])
    rows = -(-flat.shape[0] // (256 * width)) * 256
    return jnp.pad(flat, (0, rows * width - flat.shape[0])).reshape(rows, width)


def _unpack(slab, shapes):
    flat = slab.reshape(-1)
    out, o = [], 0
    for s in shapes:
        n = math.prod(s)
        out.append(flat[o:o + n].reshape(s))
        o += n
    return out


def _rope_tables(S):
    inv_freq = 1.0 / (ROPE_BASE ** (jnp.arange(0, MLA_ROPE, 2, dtype=F32) / MLA_ROPE))
    ang = jnp.arange(S, dtype=F32)[:, None] * inv_freq[None, :]
    cos, sin = jnp.cos(ang), jnp.sin(ang)
    one = jnp.ones((S, MLA_NOPE), F32)
    z16 = jnp.zeros((S, 16), F32)
    z32 = jnp.zeros((S, 32), F32)
    z64 = jnp.zeros((S, MLA_NOPE), F32)
    tc = jnp.concatenate([one, cos, cos, jnp.ones((S, 32), F32)], axis=1)
    ts1 = jnp.concatenate([z64, -sin, z16, z32], axis=1)
    ts2 = jnp.concatenate([z64, z16, sin, z32], axis=1)
    return tc, ts1, ts2


def kernel(x, c, ln_in_g, ln_in_b, w_ada, b_ada, w_in, q_norm_g, kv_norm_g, w_uq, w_ukv, w_o, ln1_g, ln1_b, w_up, w_down, ln2_g, ln2_b, loss_target, m_ln_in_g, m_ln_in_b, m_w_ada, m_b_ada, m_w_in, m_q_norm_g, m_kv_norm_g, m_w_uq, m_w_ukv, m_w_o, m_ln1_g, m_ln1_b, m_w_up, m_w_down, m_ln2_g, m_ln2_b, v_ln_in_g, v_ln_in_b, v_w_ada, v_b_ada, v_w_in, v_q_norm_g, v_kv_norm_g, v_w_uq, v_w_ukv, v_w_o, v_ln1_g, v_ln1_b, v_w_up, v_w_down, v_ln2_g, v_ln2_b):
    B, S, D = x.shape
    sbw = D // 2
    mlw = D - sbw
    nh = mlw // MLA_V
    qr = w_uq.shape[1]
    kvr = w_ukv.shape[1]
    qk = MLA_NOPE + MLA_ROPE
    dff = w_up.shape[2] * N_DEV
    din = w_in.shape[2] * N_DEV
    tm = 512 if S % 512 == 0 else S
    tq = min(512, S // 2)
    dm = dict(tm=tm, tm_small=min(tm, 256), tq=tq, sbw=sbw, qr=qr, kvr=kvr, nh=nh)
    dev =4 * lax.axis_index("x") + 2 * lax.axis_index("y") + lax.axis_index("c")

    big = [w_in, w_uq, w_ukv, w_o, w_up, w_down]
    first_w, first_token = _chip_exchange_start([a[0].astype(BF16) for a in big[:3]], "gather_w_first_start",
                                                scatter=False, after=c)

    nada = w_ada.shape[2]
    c_all = _all_gather([c + first_token[0, 0]], "gather_c")[0].reshape(N_DEV * B, D)
    b_loc = lax.dynamic_slice(b_ada, (0, dev * nada), (1, nada))
    cact_all, mod_part = _ada_partial(c_all, w_ada[0], b_loc)
    mod_all = _all_gather([mod_part], "gather_mod")[0]
    mod = lax.dynamic_slice(mod_all, (0, dev * B, 0), (N_DEV, B, nada))
    mod = jnp.swapaxes(mod, 0, 1).reshape(B, N_MOD, D)

    first_by_chip = _chip_exchange_wait(first_w, mod_all, "gather_w_first_wait")
    w_in8, w_uq8, w_ukv8 = [b.reshape((N_DEV,) + b.shape[2:]) for b in _core_gather(first_by_chip, "gather_w_first_cores")]
    late_w, late_token = _chip_exchange_start([a[0].astype(BF16) for a in big[3:]], "gather_w_late_start",
                                              scatter=False, after=w_in8, everyone=True)
    cols = lambda a8: jnp.swapaxes(a8, 0, 1).reshape(a8.shape[1], N_DEV * a8.shape[2])
    w_in_p = jnp.pad(cols(w_in8), ((0, 0), (0, LANES - MLA_ROPE)))
    zpad = jnp.zeros((qr, nh, HEAD_PAD - qk), BF16)
    w_uq_p = jnp.concatenate([cols(w_uq8).reshape(qr, nh, qk), zpad], axis=2).reshape(qr, nh * HEAD_PAD)
    w_ukv_f = cols(w_ukv8)
    w_uk = w_ukv_f[:, :nh * MLA_NOPE].reshape(kvr, nh, MLA_NOPE)
    w_uk_p = jnp.concatenate([w_uk, jnp.zeros((kvr, nh, HEAD_PAD - MLA_NOPE), BF16)], axis=2)
    w_kv = jnp.concatenate([w_uk_p.reshape(kvr, nh * HEAD_PAD), w_ukv_f[:, nh * MLA_NOPE:]], axis=1)

    tc, ts1, ts2 = _rope_tables(S)
    g_in, b_in = ln_in_g.reshape(1, D), ln_in_b.reshape(1, D)
    (x0, h, sq, sk, sv, qp, kp, mv, cq, ckv, qn, kvn) = _inproj_fwd(
        x, mod, g_in, b_in, w_in_p, w_uq_p, w_kv, q_norm_g, kv_norm_g, tc, ts1, ts2, dm, late_token)
    sb_y, sb_tot = _sb_fwd(sq, sk, sv, dm)
    mla_y, mla_lse = _mla_fwd(qp, kp, mv, dm, sb_tot)
    w_o8, w_up8, w_down8 = _chip_exchange_wait(late_w, mla_lse, "gather_w_late_wait")
    w_o_f = w_o8.reshape(D, D)
    mix, x1, h2 = _outproj_fwd(sb_y, mla_y, x0, mod, w_o_f, ln1_g, ln1_b, dm)
    u, dr2, bst_c, wst_c = _mlp_fwd(h2, x1, mod, loss_target, w_up8, w_down8, ln2_g, ln2_b, dm)

    du, dffb, dmixb, dx0a, dsb_y, dmla_y, bst_b, wst_b = _mlp_bwd(
        dr2, u, x1, x0, mix, mod, w_up8, w_down8, w_o_f, ln1_g, dm)
    T = B * S
    r2 = lambda a: a.reshape(T, a.shape[2])
    by_core = lambda a: a.reshape((4, 2) + a.shape[1:])
    g_o = jnp.concatenate([_mm_tn(r2(sb_y), r2(dmixb), "grad_w_o_sb", dr2, out_dtype=BF16),
                           _mm_tn(r2(mla_y), r2(dmixb), "grad_w_o_mla", dr2, out_dtype=BF16)], axis=0)
    g_up8 = _mm_tn(r2(h2), r2(du), "grad_w_up", dr2, out_dtype=BF16, col_blocks=N_DEV)
    g_down = _mm_tn(r2(u), r2(dffb), "grad_w_down", dr2, relu_sq=True, out_dtype=BF16)
    early = [g_o.reshape(N_DEV, D // N_DEV, D), g_up8, g_down.reshape(N_DEV, dff // N_DEV, D)]
    early_g, early_token = _chip_exchange_start(early, "scatter_g_early_start", scatter=True, after=dr2,
                                                everyone=True)

    dsq, dsk, dsv = _sb_bwd(sq, sk, sv, sb_tot, dsb_y, dm, early_token)
    dqp, dkp, dmv = _mla_bwd(qp, kp, mv, mla_y, mla_lse, dmla_y, dm, dsq)
    grad_x, dproj, dqpre, dkvo, bst_a, wst_a = _inproj_bwd(
        x, x0, dx0a, mod, g_in, dsq, dsk, dsv, dqp, dkp, dmv, cq, ckv, w_in_p, w_uq_p, w_kv, q_norm_g, kv_norm_g,
        tc, ts1, ts2, dm)

    dmod = jnp.concatenate([bst_a[:, 1], bst_a[:, 0], bst_b[:, 2], bst_b[:, 1], bst_b[:, 0], bst_c[:, 0]], axis=1)
    small = jnp.concatenate([wst_a[0], wst_a[1], wst_a[4, :qr], wst_a[5, :kvr], wst_b[0], wst_b[1],
                             wst_c[0], wst_c[1], wst_c[2]])
    n1 = small.shape[0]
    small_g, small_token = _chip_exchange_start([_pack([dmod, small], F32, LANES)], "gather_small_start",
                                                scatter=False, after=grad_x)
    g_in_p = _mm_tn(r2(h), r2(dproj), "grad_w_in", small_token)
    g_uq_p = _mm_tn(r2(qn), r2(dqpre), "grad_w_uq", small_token)
    g_kv = _mm_tn(r2(kvn), r2(dkvo), "grad_w_kv", small_token)
    small_by_chip = _chip_exchange_wait(small_g, g_kv, "gather_small_wait")
    both = _core_gather(small_by_chip, "gather_small_cores")[0].reshape(N_DEV, -1)
    g_uq_f = g_uq_p.reshape(qr, nh, HEAD_PAD)[:, :, :qk].reshape(qr, nh * qk)
    g_uk = g_kv[:, :nh * HEAD_PAD].reshape(kvr, nh, HEAD_PAD)[:, :, :MLA_NOPE].reshape(kvr, nh * MLA_NOPE)
    g_ukv_f = jnp.concatenate([g_uk, g_kv[:, nh * HEAD_PAD:]], axis=1)
    early_quarter = _chip_exchange_wait(early_g, g_kv, "scatter_g_early_wait")

    def by_dest_cols(a):
        k, n = a.shape[0], a.shape[1] // N_DEV
        return jnp.swapaxes(a.reshape(k, N_DEV, n), 0, 1).astype(BF16)

    last = [by_dest_cols(g_in_p[:, :din]), by_dest_cols(g_uq_f), by_dest_cols(g_ukv_f)]
    last_sum = _core_scatter_sum([by_core(a) for a in last], "scatter_g_last_cores")
    last_g, last_token = _chip_exchange_start(last_sum, "scatter_g_last_start", scatter=True, after=grad_x)
    names = ["w_in", "w_uq", "w_ukv", "w_o", "w_up", "w_down"]
    moms = [m_w_in, m_w_uq, m_w_ukv, m_w_o, m_w_up, m_w_down]
    vars_ = [v_w_in, v_w_uq, v_w_ukv, v_w_o, v_w_up, v_w_down]
    res_early = [_reduce_adamw(p, w, m, v, "adamw_" + n)
                 for p, w, m, v, n in zip(early_quarter, big[3:], moms[3:], vars_[3:], names[3:])]

    dmod_all = both[:, :B * N_MOD * D].reshape(N_DEV * B, N_MOD * D)
    sm = both[:, B * N_MOD * D:B * N_MOD * D + n1] + last_token[0, 0]
    dmod_my = lax.dynamic_slice(dmod_all, (0, dev * nada), (N_DEV * B, nada))
    row = lambda arrs: jnp.concatenate([a.reshape(1, -1) for a in arrs], axis=1)
    smalls = [ln_in_g, ln_in_b, q_norm_g, kv_norm_g, ln1_g, ln1_b, ln2_g, ln2_b]
    small_shapes = [a.shape for a in smalls]
    (gs, ds, nms, nvs, g_b, d_b, nm_b, nv_b, g_w, d_w, nm_w, nv_w, loss_v) = _finish(
        sm, dmod_all, dmod_my, cact_all, row(smalls),
        row([m_ln_in_g, m_ln_in_b, m_q_norm_g, m_kv_norm_g, m_ln1_g, m_ln1_b, m_ln2_g, m_ln2_b]),
        row([v_ln_in_g, v_ln_in_b, v_q_norm_g, v_kv_norm_g, v_ln1_g, v_ln1_b, v_ln2_g, v_ln2_b]),
        b_ada, m_b_ada, v_b_ada, w_ada[0], m_w_ada[0], v_w_ada[0])
    gsm, dsm, nmsm, nvsm = (_unpack(s, small_shapes) for s in (gs, ds, nms, nvs))
    last_quarter = _chip_exchange_wait(last_g, loss_v, "scatter_g_last_wait")
    res_last = [_reduce_adamw(p, w, m, v, "adamw_" + n)
                for p, w, m, v, n in zip(last_quarter, big[:3], moms[:3], vars_[:3], names[:3])]
    gb, db, nmb, nvb = ([r[i] for r in res_last + res_early] for i in range(4))

    def ordered(sm_l, w_l, ada_w, ada_b):
        return [sm_l[0], sm_l[1], ada_w[None], ada_b, w_l[0], sm_l[2], sm_l[3], w_l[1], w_l[2], w_l[3],
                sm_l[4], sm_l[5], w_l[4], w_l[5], sm_l[6], sm_l[7]]

    loss = loss_v[0, 0]
    return (loss, grad_x, *ordered(gsm, gb, g_w, g_b), *ordered(dsm, db, d_w, d_b),
            *ordered(nmsm, nmb, nm_w, nm_b), *ordered(nvsm, nvb, nv_w, nv_b))
```

```python
import math

import jax
import jax.numpy as jnp
from jax import lax
from jax.experimental import pallas as pl
from jax.experimental.pallas import tpu as pltpu

F32 = jnp.float32
BF16 = jnp.bfloat16

SB_HD = 64
MLA_V = 64
MLA_NOPE = 64
MLA_ROPE = 32
HEAD_PAD = 128
CHUNK = 64
ROPE_BASE = 10000.0
LN_EPS = 1e-5
RMS_EPS = 1e-6
DEPTH = 1
ALPHA = (2.0 * DEPTH) ** 0.25
N_MOD = 6
ADAM_LR = 0.001
ADAM_B1 = 0.9
ADAM_B2 = 0.999
ADAM_EPS = 1e-08
ADAM_WD = 0.01
ADAM_STEP = 10
N_DEV = 8
LANES = 128
LOG2E = 1.4426950408889634
CUMSUM_W = 256
VMEM_LIMIT = 56 * 1024 * 1024
MESH = pl.DeviceIdType.MESH


def _dot(a, b):
    return jnp.dot(a, b, preferred_element_type=F32)


def _dot_nt(a, b):
    return lax.dot_general(a, b, (((1,), (1,)), ((), ())), preferred_element_type=F32)


def _dot_tn(a, b):
    return lax.dot_general(a, b, (((0,), (0,)), ((), ())), preferred_element_type=F32)


def _cparams(sem):
    return pltpu.CompilerParams(dimension_semantics=sem, vmem_limit_bytes=VMEM_LIMIT)


def _full(a):
    nd = a.ndim
    return pl.BlockSpec(a.shape, lambda *_: (0,) * nd, pipeline_mode=pl.Buffered(1))


def _tok(tm, w):
    return pl.BlockSpec((1, tm, w), lambda b, s: (b, s, 0))


def _perb(rows, w):
    return pl.BlockSpec((1, rows, w), lambda b, s: (b, 0, 0))


def _sds(shape, dtype):
    return jax.ShapeDtypeStruct(shape, dtype)


def _ln_fwd(x, g, b):
    mu = jnp.mean(x, axis=-1, keepdims=True)
    xc = x - mu
    var = jnp.mean(xc * xc, axis=-1, keepdims=True)
    rstd = lax.rsqrt(var + LN_EPS)
    xhat = xc * rstd
    return xhat * g + b, xhat, rstd


def _ln_bwd(dy, xhat, rstd, g):
    dxh = dy * g
    m1 = jnp.mean(dxh, axis=-1, keepdims=True)
    m2 = jnp.mean(dxh * xhat, axis=-1, keepdims=True)
    return rstd * (dxh - m1 - xhat * m2)


def _colsum(a):
    return jnp.sum(a, axis=0, keepdims=True)


def _rope(x, c, s1, s2):
    w = x.shape[-1]
    return x * c + pltpu.roll(x, w - 16, 1) * s1 + pltpu.roll(x, 16, 1) * s2


def _rope_t(x, c, s1, s2):
    w = x.shape[-1]
    return x * c - pltpu.roll(x, w - 16, 1) * s1 - pltpu.roll(x, 16, 1) * s2


def _adamw(w, g, m, v):
    m = ADAM_B1 * m + (1.0 - ADAM_B1) * g
    v = ADAM_B2 * v + (1.0 - ADAM_B2) * (g * g)
    m_hat = m / (1.0 - ADAM_B1 ** ADAM_STEP)
    v_hat = v / (1.0 - ADAM_B2 ** ADAM_STEP)
    delta = -ADAM_LR * (m_hat / (jnp.sqrt(v_hat) + ADAM_EPS) + ADAM_WD * w)
    return delta, m, v


def _my_place():
    return lax.axis_index("x"), lax.axis_index("y"), lax.axis_index("c")


def _chip_peers(mx, my):
    out = []
    for j in (1, 2, 3):
        px = 1 - mx if (j >> 1) else mx
        py = 1 - my if (j & 1) else my
        out.append((px, py, 2 * px + py))
    return out


def _split_peers(everyone):
    mx, my, mc = _my_place()
    if not everyone:
        return [(px, py, mc, pk) for px, py, pk in _chip_peers(mx, my)], 2 * mx + my
    peers = []
    for j in range(1, N_DEV):
        px = 1 - mx if (j >> 2) & 1 else mx
        py = 1 - my if (j >> 1) & 1 else my
        pc = 1 - mc if j & 1 else mc
        peers.append((px, py, pc, 4 * px + 2 * py + pc))
    return peers, 4 * mx + 2 * my + mc


def _hbm_call(body, name, n_in, out_shape, sems):
    hbm = pl.BlockSpec(memory_space=pl.ANY)
    return pl.pallas_call(
        body, name=name, out_shape=out_shape,
        in_specs=[hbm] * n_in, out_specs=[hbm] * len(out_shape),
        scratch_shapes=[pltpu.SemaphoreType.DMA(s) for s in sems])


def _chip_exchange(xs, name, scatter):
    n = len(xs)

    def body(*refs):
        x_refs, o_refs = refs[:n], refs[n:2 * n]
        ssem, rsem, lsem = refs[2 * n:]
        mx, my, mc = _my_place()
        me = 2 * mx + my
        peers = _chip_peers(mx, my)

        def copy(i, j, src_slot, dst_slot):
            px, py, _ = peers[j]
            return pltpu.make_async_remote_copy(
                src_ref=x_refs[i].at[src_slot] if scatter else x_refs[i], dst_ref=o_refs[i].at[dst_slot],
                send_sem=ssem.at[i, j], recv_sem=rsem.at[i, j], device_id=(px, py, mc), device_id_type=MESH)

        local = [pltpu.make_async_copy(x_refs[i].at[me] if scatter else x_refs[i], o_refs[i].at[me], lsem.at[i])
                 for i in range(n)]
        sends = [copy(i, j, peers[j][2], me) for i in range(n) for j in range(3)]
        for cp in local + sends:
            cp.start()
        for i in range(n):
            for j in range(3):
                copy(i, j, peers[j][2], peers[j][2]).wait_recv()
        for cp in sends:
            cp.wait_send()
        for cp in local:
            cp.wait()

    out_shape = [_sds((4,) + tuple(x.shape[1:] if scatter else x.shape), x.dtype) for x in xs]
    return _hbm_call(body, name, n, out_shape, [(n, 3), (n, 3), (n,)])(*xs)


def _chip_exchange_start(xs, name, scatter, after, everyone=False):
    n = len(xs)
    npeer = N_DEV - 1 if everyone else 3
    blks = [tuple(x.shape[1:] if scatter else x.shape) for x in xs]

    def body(*refs):
        x_refs, land_refs = refs[:n], refs[n:2 * n]
        ssem, rsem = refs[2 * n + 1], refs[2 * n + 2]
        token = refs[-1]
        peers, me = _split_peers(everyone)
        for i in range(n):
            for j, (px, py, pc, slot) in enumerate(peers):
                pltpu.make_async_remote_copy(
                    src_ref=x_refs[i].at[slot] if scatter else x_refs[i], dst_ref=land_refs[i].at[me],
                    send_sem=ssem.at[npeer * i + j], recv_sem=rsem.at[npeer * i + j], device_id=(px, py, pc),
                    device_id_type=MESH).start()
        token[...] = jnp.zeros_like(token)

    hbm = pl.BlockSpec(memory_space=pltpu.HBM)
    sem = pl.BlockSpec(memory_space=pltpu.SEMAPHORE)
    lands = [lax.empty((npeer + 1,) + b, x.dtype) for b, x in zip(blks, xs)]
    res = pl.pallas_call(
        body, name=name,
        out_shape=[pltpu.SemaphoreType.DMA((npeer * n,)), pltpu.SemaphoreType.DMA((npeer * n,))]
        + [pltpu.HBM(x.shape, x.dtype) for x in xs] + [pltpu.HBM(l.shape, l.dtype) for l in lands]
        + [_sds((8, LANES), F32)],
        in_specs=[hbm] * (2 * n) + [_AFTER],
        out_specs=[sem, sem] + [hbm] * (2 * n) + [pl.BlockSpec(memory_space=pltpu.VMEM)],
        input_output_aliases={i: 2 + i for i in range(2 * n)},
        compiler_params=pltpu.CompilerParams(has_side_effects=pltpu.SideEffectType.DATAFLOW_SIDE_EFFECTING),
    )(*[pltpu.with_memory_space_constraint(a, pltpu.HBM) for a in list(xs) + lands], after)
    return dict(ssem=res[0], rsem=res[1], xs=res[2:2 + n], lands=res[2 + n:2 + 2 * n], n=n, scatter=scatter,
                everyone=everyone), res[-1]


def _chip_exchange_wait(handle, after, name):
    n, scatter, everyone = handle["n"], handle["scatter"], handle["everyone"]
    npeer = N_DEV - 1 if everyone else 3

    def body(*refs):
        x_refs, land_refs = refs[:n], refs[n:2 * n]
        ssem, rsem = refs[2 * n], refs[2 * n + 1]
        peers, _ = _split_peers(everyone)
        for i in range(n):
            for j, (px, py, pc, slot) in enumerate(peers):
                cp = pltpu.make_async_remote_copy(
                    src_ref=x_refs[i].at[slot] if scatter else x_refs[i], dst_ref=land_refs[i].at[slot],
                    send_sem=ssem.at[npeer * i + j], recv_sem=rsem.at[npeer * i + j], device_id=(px, py, pc),
                    device_id_type=MESH)
                cp.wait_send()
                cp.wait_recv()

    hbm = pl.BlockSpec(memory_space=pltpu.HBM)
    sem = pl.BlockSpec(memory_space=pltpu.SEMAPHORE)
    ops = list(handle["xs"]) + list(handle["lands"])
    res = pl.pallas_call(
        body, name=name,
        out_shape=[pltpu.HBM(a.shape, a.dtype) for a in ops],
        in_specs=[hbm] * (2 * n) + [sem, sem, pl.BlockSpec(memory_space=pl.ANY)],
        out_specs=[hbm] * (2 * n),
        input_output_aliases={i: i for i in range(2 * n)},
        compiler_params=pltpu.CompilerParams(has_side_effects=pltpu.SideEffectType.DATAFLOW_SIDE_EFFECTING),
    )(*ops, handle["ssem"], handle["rsem"], after)
    me = 2 * lax.axis_index("x") + lax.axis_index("y")
    if everyone:
        me = 2 * me + lax.axis_index("c")
    out = []
    for x, land in zip(res[:n], res[n:]):
        own = lax.dynamic_index_in_dim(x, me, 0, keepdims=False) if scatter else x
        out.append(lax.dynamic_update_index_in_dim(land, own, me, 0))
    return out


def _core_gather(xs, name):
    n = len(xs)

    def body(*refs):
        x_refs, o_refs, mine, got = refs[:n], refs[n:2 * n], refs[2 * n:3 * n], refs[3 * n:4 * n]
        lsem, ssem, rsem, osem = refs[4 * n:]
        mx, my, mc = _my_place()
        loads = [pltpu.make_async_copy(x_refs[i], mine[i], lsem.at[i]) for i in range(n)]
        for cp in loads:
            cp.start()
        sends, stores = [], []
        for i in range(n):
            loads[i].wait()
            cp = pltpu.make_async_remote_copy(
                src_ref=mine[i], dst_ref=got[i], send_sem=ssem.at[i], recv_sem=rsem.at[i],
                device_id=(mx, my, 1 - mc), device_id_type=MESH)
            cp.start()
            sends.append(cp)
            for k in range(4):
                st = pltpu.make_async_copy(mine[i].at[k], o_refs[i].at[k, mc], osem.at[i, k])
                st.start()
                stores.append(st)
        for i in range(n):
            sends[i].wait_recv()
            for k in range(4):
                st = pltpu.make_async_copy(got[i].at[k], o_refs[i].at[k, 1 - mc], osem.at[n + i, k])
                st.start()
                stores.append(st)
        for cp in sends:
            cp.wait_send()
        for st in stores:
            st.wait()

    hbm = pl.BlockSpec(memory_space=pl.ANY)
    bufs = [pltpu.VMEM(x.shape, x.dtype) for x in xs]
    return pl.pallas_call(
        body, name=name,
        out_shape=[_sds((4, 2) + tuple(x.shape[1:]), x.dtype) for x in xs],
        in_specs=[hbm] * n, out_specs=[hbm] * n,
        scratch_shapes=bufs + bufs + [pltpu.SemaphoreType.DMA((n,)), pltpu.SemaphoreType.DMA((n,)),
                                      pltpu.SemaphoreType.DMA((n,)), pltpu.SemaphoreType.DMA((2 * n, 4))],
        compiler_params=pltpu.CompilerParams(vmem_limit_bytes=VMEM_LIMIT),
    )(*xs)


def _rows_step(k):
    for r in (256, 128, 64, 32, 16, 8):
        if k % r == 0:
            return r
    return k


def _core_scatter_sum(gs, name):
    n = len(gs)

    def body(*refs):
        g_refs, o_refs = refs[:n], refs[n:2 * n]
        send, got, mine = refs[2 * n:3 * n], refs[3 * n:4 * n], refs[4 * n:5 * n]
        lsem, msem, ssem, rsem, osem = refs[5 * n:]
        mx, my, mc = _my_place()
        pairs = [(i, k) for i in range(n) for k in range(4)]
        out_loads = {(i, k): pltpu.make_async_copy(g_refs[i].at[k, 1 - mc], send[i].at[k], lsem.at[i, k])
                     for i, k in pairs}
        own_loads = {(i, k): pltpu.make_async_copy(g_refs[i].at[k, mc], mine[i].at[k], msem.at[i, k])
                     for i, k in pairs}
        for p in pairs:
            out_loads[p].start()
        for p in pairs:
            own_loads[p].start()
        sends = []
        for i in range(n):
            for k in range(4):
                out_loads[i, k].wait()
            cp = pltpu.make_async_remote_copy(
                src_ref=send[i], dst_ref=got[i], send_sem=ssem.at[i], recv_sem=rsem.at[i],
                device_id=(mx, my, 1 - mc), device_id_type=MESH)
            cp.start()
            sends.append(cp)
        stores = []
        for i in range(n):
            for k in range(4):
                own_loads[i, k].wait()
            sends[i].wait_recv()
            rows = g_refs[i].shape[2]
            step = _rows_step(rows)

            def add(r, _, i=i, step=step):
                sl = pl.ds(pl.multiple_of(r * step, step), step)
                for k in range(4):
                    mine[i][k, sl, :] = (mine[i][k, sl, :].astype(F32) + got[i][k, sl, :].astype(F32)).astype(BF16)
                return 0

            lax.fori_loop(0, rows // step, add, 0)
            st = pltpu.make_async_copy(mine[i], o_refs[i], osem.at[i])
            st.start()
            stores.append(st)
        for cp in sends:
            cp.wait_send()
        for st in stores:
            st.wait()

    hbm = pl.BlockSpec(memory_space=pl.ANY)
    blk = [(4,) + tuple(g.shape[2:]) for g in gs]
    bufs = [pltpu.VMEM(b, BF16) for b in blk]
    return pl.pallas_call(
        body, name=name,
        out_shape=[_sds(b, BF16) for b in blk],
        in_specs=[hbm] * n, out_specs=[hbm] * n,
        scratch_shapes=bufs * 3 + [pltpu.SemaphoreType.DMA((n, 4)), pltpu.SemaphoreType.DMA((n, 4)),
                                   pltpu.SemaphoreType.DMA((n,)), pltpu.SemaphoreType.DMA((n,)),
                                   pltpu.SemaphoreType.DMA((n,))],
        compiler_params=pltpu.CompilerParams(vmem_limit_bytes=VMEM_LIMIT),
    )(*gs)


def _all_gather(xs, name):
    by_chip = _chip_exchange(xs, name + "_chips", scatter=False)
    both = _core_gather(by_chip, name + "_cores")
    return [b.reshape((N_DEV,) + tuple(x.shape)) for b, x in zip(both, xs)]


def _ada_partial(c_all, w_ada_loc, b_loc):
    def body(c_ref, w_ref, b_ref, act_ref, mod_ref):
        c = c_ref[...]
        act = c * (1.0 / (1.0 + jnp.exp(-c)))
        act_ref[...] = act
        mod_ref[...] = _dot(act.astype(BF16), w_ref[...].astype(BF16)) + b_ref[...]

    nb, d = c_all.shape
    return pl.pallas_call(
        body, name="ada_partial",
        out_shape=(_sds((nb, d), F32), _sds((nb, w_ada_loc.shape[1]), F32)),
        compiler_params=pltpu.CompilerParams(vmem_limit_bytes=VMEM_LIMIT),
    )(c_all, w_ada_loc, b_loc)


_AFTER = pl.BlockSpec(memory_space=pl.ANY)


def _inproj_fwd(x, mod, ln_g, ln_b, w_in_p, w_uq_p, w_kv, gq, gkv, tc, ts1, ts2, dm, after):
    B, S, D = x.shape
    tm = dm["tm"]
    sbw, qr, kvr, nh = dm["sbw"], dm["qr"], dm["kvr"], dm["nh"]
    o_cq, o_ckv, o_kr = 3 * sbw, 3 * sbw + qr, 3 * sbw + qr + kvr
    qpw = nh * HEAD_PAD

    def body(x_ref, mod_ref, g_ref, b_ref, win_ref, wuq_ref, wkv_ref, gq_ref, gkv_ref, tc_ref, ts1_ref, ts2_ref, _,
             x0_ref, h_ref, q_ref, k_ref, v_ref, qp_ref, kp_ref, mv_ref, cq_ref, ckv_ref, qn_ref, kvn_ref):
        x0, _, _ = _ln_fwd(x_ref[0], g_ref[...], b_ref[...])
        x0_ref[0] = x0
        mod = mod_ref[0]
        h = (x0 * (1.0 + mod[1:2]) + mod[0:1]).astype(BF16)
        h_ref[0] = h
        proj = _dot(h, win_ref[...])
        q_ref[0] = (proj[:, 0:sbw] * SB_Q_SCALE).astype(BF16)
        k_ref[0] = proj[:, sbw:2 * sbw].astype(BF16)
        v_ref[0] = proj[:, 2 * sbw:3 * sbw].astype(BF16)
        cq = proj[:, o_cq:o_cq + qr]
        ckv = proj[:, o_ckv:o_ckv + kvr]
        cq_ref[0] = cq
        ckv_ref[0] = ckv
        qn = (cq * lax.rsqrt(jnp.mean(cq * cq, axis=-1, keepdims=True) + RMS_EPS) * gq_ref[...]).astype(BF16)
        kvn = (ckv * lax.rsqrt(jnp.mean(ckv * ckv, axis=-1, keepdims=True) + RMS_EPS) * gkv_ref[...]).astype(BF16)
        qn_ref[0] = qn
        kvn_ref[0] = kvn
        c1, s1, s2 = tc_ref[...], ts1_ref[...], ts2_ref[...]
        c8, s18, s28 = jnp.tile(c1, (1, nh)), jnp.tile(s1, (1, nh)), jnp.tile(s2, (1, nh))
        qp_ref[0] = (_rope(_dot(qn, wuq_ref[...]), c8, s18, s28) * MLA_Q_SCALE).astype(BF16)
        kvo = _dot(kvn, wkv_ref[...])
        kr = pltpu.roll(proj[:, o_kr:o_kr + LANES], 64, 1)
        kr = _rope(kr, c1, s1, s2)
        kp_ref[0] = (kvo[:, 0:qpw] + jnp.tile(kr, (1, nh))).astype(BF16)
        mv_ref[0] = kvo[:, qpw:].astype(BF16)

    tab = pl.BlockSpec((tm, LANES), lambda b, s: (s, 0))
    outs = [(D, F32), (D, BF16), (sbw, BF16), (sbw, BF16), (sbw, BF16), (qpw, BF16), (qpw, BF16),
            (nh * MLA_V, BF16), (qr, F32), (kvr, F32), (qr, BF16), (kvr, BF16)]
    return pl.pallas_call(
        body, name="inproj_fwd", grid=(B, S // tm),
        in_specs=[_tok(tm, D), _perb(N_MOD, D), _full(ln_g), _full(ln_b), _full(w_in_p), _full(w_uq_p),
                  _full(w_kv), _full(gq), _full(gkv), tab, tab, tab, _AFTER],
        out_specs=[_tok(tm, w) for w, _ in outs],
        out_shape=[_sds((B, S, w), t) for w, t in outs],
        compiler_params=_cparams(("parallel", "parallel")),
    )(x, mod, ln_g, ln_b, w_in_p, w_uq_p, w_kv, gq, gkv, tc, ts1, ts2, after)


def _neg_abs(x):
    sign = jnp.uint32(0x80000000)
    return lax.bitcast_convert_type(lax.bitcast_convert_type(x, jnp.uint32) | sign, F32)


SB_Q_SCALE = -(SB_HD ** -0.5) * LOG2E
MLA_Q_SCALE = (MLA_NOPE + MLA_ROPE) ** -0.5 * LOG2E


def _log2_keep(zs):
    return jnp.minimum(zs, 0.0) - jnp.log2(1.0 + jnp.exp2(_neg_abs(zs)))


def _split_dot(a, u):
    hi = a.astype(BF16)
    lo = (a - hi.astype(F32)).astype(BF16)
    return _dot(jnp.concatenate([hi, lo], axis=1), jnp.concatenate([u, u], axis=0))


def _tri(n, rel):
    row = lax.broadcasted_iota(jnp.int32, (n, n), 0)
    col = lax.broadcasted_iota(jnp.int32, (n, n), 1)
    return rel(row, col).astype(BF16)


def _running_sum(a, tri, reverse, split, start):
    cs = tri.shape[0]
    n = a.shape[1] // cs
    out = [None] * n
    run = start
    for c in (reversed(range(n)) if reverse else range(n)):
        part = a[:, c * cs:(c + 1) * cs]
        out[c] = (_split_dot(part, tri) if split else _dot(part.astype(BF16), tri)) + run
        run = run + jnp.sum(part, axis=1, keepdims=True)
    return (out[0] if n == 1 else jnp.concatenate(out, axis=1)), run


def _transpose_bf16(a):
    return a.astype(F32).T.astype(BF16)


def _tile_mask(nr, nk, r0, c0, rel):
    row = lax.broadcasted_iota(jnp.int32, (nr, nk), 0) + r0
    col = lax.broadcasted_iota(jnp.int32, (nr, nk), 1) + c0
    return rel(row, col)


def _put_rows(whole, part, r0):
    return part if r0 == 0 else jnp.concatenate([whole[:r0], part], axis=0)


def _diag_tiles(tq, split):
    half = tq // 2
    return [(0, tq, 0, half), (half, half, half, half)] if split else [(0, tq, 0, tq)]


def _sb_fwd(q, k, v, dm):
    B, S, W = q.shape
    tq = dm["tq"]
    nq = S // tq

    def body(q_ref, k_ref, v_ref, y_ref, tot_ref):
        qi = pl.program_id(2)
        q2 = q_ref[0]
        lane = lax.broadcasted_iota(jnp.int32, (tq, LANES), 1)
        qs = jnp.concatenate([jnp.where(lane < SB_HD, q2, 0), jnp.where(lane >= SB_HD, q2, 0)], axis=0).astype(BF16)
        later = _tri(min(tq, CUMSUM_W), lambda a, b: a > b)
        assert tq & (tq - 1) == 0
        strict = _tile_mask(2 * tq, tq, 0, 0, lambda t, s: s < (t & (tq - 1)))

        def block(j, carry, masked):
            acc, run = carry
            off = pl.multiple_of(j * tq, tq)
            zs = _dot_nt(qs, k_ref[0, pl.ds(off, tq), :])
            a = _log2_keep(zs)
            if masked:
                a = jnp.where(strict, a, 0.0)
            a_later, run = _running_sum(a, later, reverse=True, split=True, start=run)
            w = jnp.exp2((a - zs) + a_later)
            if masked:
                w = jnp.where(strict, w, 0.0)
            return acc + _dot(w.astype(BF16), v_ref[0, pl.ds(off, tq), :]), run

        carry = block(qi, (jnp.zeros((2 * tq, LANES), F32), jnp.zeros((2 * tq, 1), F32)), True)
        acc, run = lax.fori_loop(0, qi, lambda jj, c: block(qi - 1 - jj, c, False), carry)
        y_ref[0] = jnp.where(lane < SB_HD, acc[:tq], acc[tq:]).astype(BF16)
        tot_ref[0] = jnp.where(lane < SB_HD, run[:tq], run[tq:])

    qspec = pl.BlockSpec((1, tq, LANES), lambda b, hp, i: (b, i, hp))
    kspec = pl.BlockSpec((1, S, LANES), lambda b, hp, i: (b, 0, hp))
    return pl.pallas_call(
        body, name="sb_fwd", grid=(B, W // LANES, nq),
        in_specs=[qspec, kspec, kspec],
        out_specs=[qspec, qspec],
        out_shape=[_sds((B, S, W), BF16), _sds((B, S, W), F32)],
        compiler_params=_cparams(("parallel", "parallel", "arbitrary")),
    )(q, k, v)


def _sb_bwd(q, k, v, tot, dy, dm, after):
    B, S, W = q.shape
    tq = dm["tq"]
    nq = S // tq

    def body(q_ref, k_ref, v_ref, tot_ref, dy_ref, _, dq_ref, dk_ref, dv_ref, dk_acc, dv_acc):
        qi = pl.program_id(2)

        @pl.when(qi == 0)
        def _():
            dk_acc[...] = jnp.zeros_like(dk_acc)
            dv_acc[...] = jnp.zeros_like(dv_acc)

        q2 = q_ref[0]
        dy2 = dy_ref[0]
        tot2 = tot_ref[0]
        lane = lax.broadcasted_iota(jnp.int32, (tq, LANES), 1)
        in_h = [lane < SB_HD, lane >= SB_HD]
        qh = [jnp.where(m, q2, 0).astype(BF16) for m in in_h]
        dyh = [jnp.where(m, dy2, 0).astype(BF16) for m in in_h]
        q_t = [_transpose_bf16(a) for a in qh]
        dy_t = [_transpose_bf16(a) for a in dyh]
        toth = [tot2[:, 0:1], tot2[:, SB_HD:SB_HD + 1]]

        def tile(j, carry, r0, nr, c0, nk, masked):
            off = pl.multiple_of(j * tq + c0, math.gcd(tq, c0))
            k2 = k_ref[0, pl.ds(off, nk), :]
            v2 = v_ref[0, pl.ds(off, nk), :]
            upto = _tri(min(nk, CUMSUM_W), lambda a, b: a <= b)
            before = _tri(min(nk, CUMSUM_W), lambda a, b: a < b)
            strict = _tile_mask(nr, nk, r0, c0, lambda t, s: s < t) if masked else None
            rows = slice(r0, r0 + nr)
            new = []
            dk_blk = jnp.zeros((LANES, nk), F32)
            dv_blk = jnp.zeros((LANES, nk), F32)
            for h in range(2):
                dq, pa, pg = carry[3 * h][rows], carry[3 * h + 1][rows], carry[3 * h + 2][rows]
                zs = _dot_nt(qh[h][rows], k2)
                a = _log2_keep(zs)
                if masked:
                    a = jnp.where(strict, a, 0.0)
                a_upto, pa = _running_sum(a, upto, reverse=False, split=True, start=pa)
                w = jnp.exp2((a - zs) - a_upto)
                if masked:
                    w = jnp.where(strict, w, 0.0)
                g = _dot_nt(dyh[h][rows], v2) * w
                g_before, pg = _running_sum(g, before, reverse=False, split=False, start=pg)
                dz = (g + g_before) * jnp.exp2(a) - g_before
                if masked:
                    dz = jnp.where(strict, dz, 0.0)
                dzb = dz.astype(BF16)
                dv_blk = dv_blk + _dot(dy_t[h][:, rows], w.astype(BF16))
                dk_blk = dk_blk + _dot(q_t[h][:, rows], dzb)
                new += [_put_rows(carry[3 * h], dq + _dot(dzb, k2), r0), _put_rows(carry[3 * h + 1], pa, r0),
                        _put_rows(carry[3 * h + 2], pg, r0)]
            dk_acc[j, :, c0:c0 + nk] += dk_blk
            dv_acc[j, :, c0:c0 + nk] += dv_blk
            return tuple(new)

        zero = jnp.zeros((tq, LANES), F32)
        zrun = jnp.zeros((tq, 1), F32)
        carry = lax.fori_loop(0, qi, lambda j, c: tile(j, c, 0, tq, 0, tq, False),
                              (zero, -toth[0], zrun, zero, -toth[1], zrun))
        for r0, nr, c0, nk in _diag_tiles(tq, False):
            carry = tile(qi, carry, r0, nr, c0, nk, True)
        dq_ref[0] = (jnp.where(in_h[0], carry[0], carry[3]) * (SB_HD ** -0.5)).astype(BF16)

        @pl.when(qi == nq - 1)
        def _():
            for jb in range(nq):
                dk_ref[0, jb * tq:(jb + 1) * tq, :] = (dk_acc[jb].T * (-1.0 / LOG2E)).astype(BF16)
                dv_ref[0, jb * tq:(jb + 1) * tq, :] = dv_acc[jb].T.astype(BF16)

    qspec = pl.BlockSpec((1, tq, LANES), lambda b, hp, i: (b, i, hp))
    kspec = pl.BlockSpec((1, S, LANES), lambda b, hp, i: (b, 0, hp))
    return pl.pallas_call(
        body, name="sb_bwd", grid=(B, W // LANES, nq),
        in_specs=[qspec, kspec, kspec, qspec, qspec, _AFTER],
        out_specs=[qspec, kspec, kspec],
        out_shape=[_sds((B, S, W), BF16)] * 3,
        scratch_shapes=[pltpu.VMEM((nq, LANES, tq), F32), pltpu.VMEM((nq, LANES, tq), F32)],
        compiler_params=_cparams(("parallel", "parallel", "arbitrary")),
    )(q, k, v, tot, dy, after)


def _same_or_earlier_chunk(row, col):
    return lax.shift_right_logical(col, 6) <= lax.shift_right_logical(row, 6)


def _mla_fwd(qp, kp, mv, dm, after):
    B, S, QW = qp.shape
    VW = mv.shape[2]
    tq = dm["tq"]
    nq = S // tq
    assert CHUNK == 64

    def body(q_ref, k_ref, v_ref, _, y_ref, lse_ref):
        qi = pl.program_id(2)
        q2 = q_ref[0]
        lane = lax.broadcasted_iota(jnp.int32, (tq, LANES), 1)

        def tile(j, carry, r0, nr, c0, nk, masked):
            off = pl.multiple_of(j * tq + c0, math.gcd(tq, c0))
            v2 = v_ref[0, pl.ds(off, nk), :]
            allowed = _tile_mask(nr, nk, r0, c0, _same_or_earlier_chunk) if masked else None
            rows = slice(r0, r0 + nr)
            heads = range(2)
            sl = [slice(h * HEAD_PAD, (h + 1) * HEAD_PAD) for h in heads]
            m_old = [carry[3 * h + 1][rows] for h in heads]
            s = [_dot_nt(q2[rows, sl[h]], k_ref[0, pl.ds(off, nk), sl[h]]) for h in heads]
            if masked:
                s = [jnp.where(allowed, s[h], -1e30) for h in heads]
            m_new = [jnp.maximum(m_old[h], jnp.max(s[h], axis=1, keepdims=True)) for h in heads]
            alpha = [jnp.exp2(m_old[h] - m_new[h]) for h in heads]
            p = [jnp.exp2(s[h] - m_new[h]) for h in heads]
            acc = [alpha[h] * carry[3 * h][rows] + _dot(p[h].astype(BF16), v2) for h in heads]
            l = [alpha[h] * carry[3 * h + 2][rows] + jnp.sum(p[h], axis=1, keepdims=True) for h in heads]
            out = []
            for h in heads:
                out += [_put_rows(carry[3 * h], acc[h], r0), _put_rows(carry[3 * h + 1], m_new[h], r0),
                        _put_rows(carry[3 * h + 2], l[h], r0)]
            return tuple(out)

        zero = jnp.zeros((tq, LANES), F32)
        m0 = jnp.full((tq, 1), -1e30, F32)
        l0 = jnp.zeros((tq, 1), F32)
        carry = (zero, m0, l0, zero, m0, l0)
        for r0, nr, c0, nk in _diag_tiles(tq, False):
            carry = tile(qi, carry, r0, nr, c0, nk, True)
        carry = lax.fori_loop(0, qi, lambda j, c: tile(j, c, 0, tq, 0, tq, False), carry)
        y0 = carry[0] / carry[2]
        y1 = carry[3] / carry[5]
        y_ref[0] = jnp.where(lane < MLA_V, y0, y1).astype(BF16)
        lse_ref[0] = jnp.where(lane < MLA_V, carry[1] + jnp.log2(carry[2]), carry[4] + jnp.log2(carry[5]))

    qspec = pl.BlockSpec((1, tq, 2 * HEAD_PAD), lambda b, hp, i: (b, i, hp))
    kspec = pl.BlockSpec((1, S, 2 * HEAD_PAD), lambda b, hp, i: (b, 0, hp))
    vspec = pl.BlockSpec((1, S, LANES), lambda b, hp, i: (b, 0, hp))
    yspec = pl.BlockSpec((1, tq, LANES), lambda b, hp, i: (b, i, hp))
    return pl.pallas_call(
        body, name="mla_fwd", grid=(B, VW // LANES, nq),
        in_specs=[qspec, kspec, vspec, _AFTER],
        out_specs=[yspec, yspec],
        out_shape=[_sds((B, S, VW), BF16), _sds((B, S, VW), F32)],
        compiler_params=_cparams(("parallel", "parallel", "arbitrary")),
    )(qp, kp, mv, after)


def _mla_bwd(qp, kp, mv, y, lse, dy, dm, after):
    B, S, QW = qp.shape
    VW = mv.shape[2]
    tq = dm["tq"]
    nq = S // tq
    scale = (MLA_NOPE + MLA_ROPE) ** -0.5

    def body(q_ref, k_ref, v_ref, y_ref, lse_ref, dy_ref, _, dq_ref, dk_ref, dv_ref, dk_acc, dv_acc):
        qi = pl.program_id(2)

        @pl.when(qi == 0)
        def _():
            dk_acc[...] = jnp.zeros_like(dk_acc)
            dv_acc[...] = jnp.zeros_like(dv_acc)

        q2 = q_ref[0]
        dy2 = dy_ref[0]
        lse2 = lse_ref[0]
        lane = lax.broadcasted_iota(jnp.int32, (tq, LANES), 1)
        in_h = [lane < MLA_V, lane >= MLA_V]
        prod = dy2.astype(F32) * y_ref[0].astype(F32)
        delta = [jnp.sum(jnp.where(m, prod, 0.0), axis=1, keepdims=True) for m in in_h]
        dyh = [jnp.where(m, dy2, 0).astype(BF16) for m in in_h]
        lseh = [lse2[:, 0:1], lse2[:, MLA_V:MLA_V + 1]]
        q_t = _transpose_bf16(q2)
        dy_t = [_transpose_bf16(a) for a in dyh]

        def tile(j, carry, r0, nr, c0, nk, masked):
            off = pl.multiple_of(j * tq + c0, math.gcd(tq, c0))
            v2 = v_ref[0, pl.ds(off, nk), :]
            allowed = _tile_mask(nr, nk, r0, c0, _same_or_earlier_chunk) if masked else None
            rows = slice(r0, r0 + nr)
            keys = slice(c0, c0 + nk)
            heads = range(2)
            sl = [slice(h * HEAD_PAD, (h + 1) * HEAD_PAD) for h in heads]
            qhh = [q2[rows, sl[h]] for h in heads]
            dyr = [dyh[h][rows] for h in heads]
            kh = [k_ref[0, pl.ds(off, nk), sl[h]] for h in heads]
            s = [_dot_nt(qhh[h], kh[h]) for h in heads]
            dp = [_dot_nt(dyr[h], v2) for h in heads]
            if masked:
                s = [jnp.where(allowed, s[h], -1e30) for h in heads]
            p = [jnp.exp2(s[h] - lseh[h][rows]) for h in heads]
            dv_acc[j, :, keys] += (_dot(dy_t[0][:, rows], p[0].astype(BF16))
                                   + _dot(dy_t[1][:, rows], p[1].astype(BF16)))
            ds = [(p[h] * (dp[h] - delta[h][rows])).astype(BF16) for h in heads]
            for h in heads:
                dk_acc[j, sl[h], keys] += _dot(q_t[sl[h], rows], ds[h])
            return tuple(_put_rows(carry[h], carry[h][rows] + _dot(ds[h], kh[h]), r0) for h in heads)

        zero = jnp.zeros((tq, HEAD_PAD), F32)
        carry = lax.fori_loop(0, qi, lambda j, c: tile(j, c, 0, tq, 0, tq, False), (zero, zero))
        for r0, nr, c0, nk in _diag_tiles(tq, True):
            carry = tile(qi, carry, r0, nr, c0, nk, True)
        dq_ref[0] = (jnp.concatenate([carry[0], carry[1]], axis=1) * scale).astype(BF16)

        @pl.when(qi == nq - 1)
        def _():
            for jb in range(nq):
                dk_ref[0, jb * tq:(jb + 1) * tq, :] = (dk_acc[jb].T * (1.0 / LOG2E)).astype(BF16)
                dv_ref[0, jb * tq:(jb + 1) * tq, :] = dv_acc[jb].T.astype(BF16)

    qspec = pl.BlockSpec((1, tq, 2 * HEAD_PAD), lambda b, hp, i: (b, i, hp))
    kspec = pl.BlockSpec((1, S, 2 * HEAD_PAD), lambda b, hp, i: (b, 0, hp))
    vspec = pl.BlockSpec((1, S, LANES), lambda b, hp, i: (b, 0, hp))
    yspec = pl.BlockSpec((1, tq, LANES), lambda b, hp, i: (b, i, hp))
    return pl.pallas_call(
        body, name="mla_bwd", grid=(B, VW // LANES, nq),
        in_specs=[qspec, kspec, vspec, yspec, yspec, yspec, _AFTER],
        out_specs=[qspec, kspec, vspec],
        out_shape=[_sds((B, S, QW), BF16), _sds((B, S, QW), BF16), _sds((B, S, VW), BF16)],
        scratch_shapes=[pltpu.VMEM((nq, 2 * HEAD_PAD, tq), F32), pltpu.VMEM((nq, LANES, tq), F32)],
        compiler_params=_cparams(("parallel", "parallel", "arbitrary")),
    )(qp, kp, mv, y, lse, dy, after)


def _outproj_fwd(sb_y, mla_y, x0, mod, w_o, ln_g, ln_b, dm):
    B, S, D = x0.shape
    tm = dm["tm"]
    sbw = sb_y.shape[2]

    def body(ya_ref, yb_ref, x0_ref, mod_ref, wo_ref, g_ref, b_ref, mix_ref, x1_ref, h2_ref):
        mod = mod_ref[0]
        mix = _dot(ya_ref[0], wo_ref[0:sbw, :]) + _dot(yb_ref[0], wo_ref[sbw:, :])
        mix_ref[0] = mix
        x1, _, _ = _ln_fwd(ALPHA * x0_ref[0] + (1.0 + mod[2:3]) * mix, g_ref[...], b_ref[...])
        x1_ref[0] = x1
        h2_ref[0] = (x1 * (1.0 + mod[4:5]) + mod[3:4]).astype(BF16)

    return pl.pallas_call(
        body, name="outproj_fwd", grid=(B, S // tm),
        in_specs=[_tok(tm, sbw), _tok(tm, mla_y.shape[2]), _tok(tm, D), _perb(N_MOD, D),
                  _full(w_o), _full(ln_g), _full(ln_b)],
        out_specs=[_tok(tm, D)] * 3,
        out_shape=[_sds((B, S, D), F32), _sds((B, S, D), F32), _sds((B, S, D), BF16)],
        compiler_params=_cparams(("parallel", "parallel")),
    )(sb_y, mla_y, x0, mod, w_o, ln_g, ln_b)


def _stat_specs(B, D):
    specs = [pl.BlockSpec((1, 8, D), lambda b, s: (b, 0, 0)), pl.BlockSpec((8, D), lambda b, s: (0, 0))]
    shapes = [_sds((B, 8, D), F32), _sds((8, D), F32)]
    return specs, shapes


def _stat_init(bst_ref, wst_ref):
    @pl.when(pl.program_id(1) == 0)
    def _():
        bst_ref[...] = jnp.zeros_like(bst_ref)

    @pl.when((pl.program_id(0) == 0) & (pl.program_id(1) == 0))
    def _():
        wst_ref[...] = jnp.zeros_like(wst_ref)


def _mlp_fwd(h2, x1, mod, target, w_up, w_down, ln_g, ln_b, dm):
    B, S, D = x1.shape
    tm = dm["tm"]
    nck, _, ck = w_up.shape
    dff = nck * ck

    def body(h2_ref, x1_ref, mod_ref, t_ref, wu_ref, wd_ref, g_ref, b_ref, u_ref, dr_ref, bst_ref, wst_ref):
        _stat_init(bst_ref, wst_ref)
        mod = mod_ref[0]
        g = g_ref[...]
        h2 = h2_ref[0]
        ff = jnp.zeros((tm, D), F32)
        for c in range(nck):
            u = _dot(h2, wu_ref[c])
            u_ref[0, :, c * ck:(c + 1) * ck] = u.astype(BF16)
            act = jnp.square(jnp.maximum(u, 0.0)).astype(BF16)
            ff = ff + _dot(act, wd_ref[c])
        x2, xhat, rstd = _ln_fwd(ALPHA * x1_ref[0] + (1.0 + mod[5:6]) * ff, g, b_ref[...])
        err = x2 - t_ref[0]
        dy = err * (1.0 / D)
        dr = _ln_bwd(dy, xhat, rstd, g)
        dr_ref[0] = dr
        bst_ref[0, 0:1, :] += _colsum(dr * ff)
        wst_ref[0:1, :] += _colsum(dy * xhat)
        wst_ref[1:2, :] += _colsum(dy)
        wst_ref[2:3, :] += _colsum(err * err) * (0.5 / D)

    sspecs, sshapes = _stat_specs(B, D)
    return pl.pallas_call(
        body, name="mlp_fwd", grid=(B, S // tm),
        in_specs=[_tok(tm, D), _tok(tm, D), _perb(N_MOD, D), _tok(tm, D), _full(w_up), _full(w_down),
                  _full(ln_g), _full(ln_b)],
        out_specs=[_tok(tm, dff), _tok(tm, D)] + sspecs,
        out_shape=[_sds((B, S, dff), BF16), _sds((B, S, D), F32)] + sshapes,
        compiler_params=_cparams(("arbitrary", "arbitrary")),
    )(h2, x1, mod, target, w_up, w_down, ln_g, ln_b)


def _mlp_bwd(dr2, u, x1, x0, mix, mod, w_up, w_down, w_o, ln_g, dm):
    B, S, D = x1.shape
    tm = dm["tm_small"]
    sbw = dm["sbw"]
    nck, _, ck = w_up.shape
    dff = nck * ck

    def body(dr_ref, u_ref, x1_ref, x0_ref, mix_ref, mod_ref, wu_ref, wd_ref, wo_ref, g_ref,
             du_ref, dff_ref, dmix_ref, dx0_ref, dya_ref, dyb_ref, bst_ref, wst_ref):
        _stat_init(bst_ref, wst_ref)
        mod = mod_ref[0]
        dr2 = dr_ref[0]
        dffv = ((1.0 + mod[5:6]) * dr2).astype(BF16)
        dff_ref[0] = dffv
        dh2 = jnp.zeros((tm, D), F32)
        for c in range(nck):
            sl = slice(c * ck, (c + 1) * ck)
            da = _dot_nt(dffv, wd_ref[c])
            du = (da * (2.0 * jnp.maximum(u_ref[0, :, sl].astype(F32), 0.0))).astype(BF16)
            du_ref[0, :, sl] = du
            dh2 = dh2 + _dot_nt(du, wu_ref[c])
        x1 = x1_ref[0]
        dx1 = ALPHA * dr2 + dh2 * (1.0 + mod[4:5])
        bst_ref[0, 0:1, :] += _colsum(dh2 * x1)
        bst_ref[0, 1:2, :] += _colsum(dh2)
        mix = mix_ref[0]
        g = g_ref[...]
        _, xhat, rstd = _ln_fwd(ALPHA * x0_ref[0] + (1.0 + mod[2:3]) * mix, g, 0.0)
        dr1 = _ln_bwd(dx1, xhat, rstd, g)
        wst_ref[0:1, :] += _colsum(dx1 * xhat)
        wst_ref[1:2, :] += _colsum(dx1)
        bst_ref[0, 2:3, :] += _colsum(dr1 * mix)
        dx0_ref[0] = ALPHA * dr1
        dmix = ((1.0 + mod[2:3]) * dr1).astype(BF16)
        dmix_ref[0] = dmix
        dya_ref[0] = _dot_nt(dmix, wo_ref[0:sbw, :]).astype(BF16)
        dyb_ref[0] = _dot_nt(dmix, wo_ref[sbw:, :]).astype(BF16)

    sspecs, sshapes = _stat_specs(B, D)
    wa, wb = sbw, w_o.shape[0] - sbw
    return pl.pallas_call(
        body, name="mlp_bwd", grid=(B, S // tm),
        in_specs=[_tok(tm, D), _tok(tm, dff), _tok(tm, D), _tok(tm, D), _tok(tm, D), _perb(N_MOD, D),
                  _full(w_up), _full(w_down), _full(w_o), _full(ln_g)],
        out_specs=[_tok(tm, dff), _tok(tm, D), _tok(tm, D), _tok(tm, D), _tok(tm, wa), _tok(tm, wb)] + sspecs,
        out_shape=[_sds((B, S, dff), BF16), _sds((B, S, D), BF16), _sds((B, S, D), BF16), _sds((B, S, D), F32),
                   _sds((B, S, wa), BF16), _sds((B, S, wb), BF16)] + sshapes,
        compiler_params=_cparams(("arbitrary", "arbitrary")),
    )(dr2, u, x1, x0, mix, mod, w_up, w_down, w_o, ln_g)


def _inproj_bwd(x, x0, dx0a, mod, ln_g, dq, dk, dv, dqp, dkp, dmv, cq, ckv, w_in_p, w_uq_p, w_kv, gq, gkv,
                tc, ts1, ts2, dm):
    B, S, D = x.shape
    tm = dm["tm"]
    sbw, qr, kvr, nh = dm["sbw"], dm["qr"], dm["kvr"], dm["nh"]
    qpw = nh * HEAD_PAD
    dinp = w_in_p.shape[1]
    kvw = w_kv.shape[1]

    def body(x_ref, x0_ref, dx0a_ref, mod_ref, g_ref, dq_ref, dk_ref, dv_ref, dqp_ref, dkp_ref, dmv_ref,
             cq_ref, ckv_ref, win_ref, wuq_ref, wkv_ref, gq_ref, gkv_ref, tc_ref, ts1_ref, ts2_ref,
             gx_ref, dproj_ref, dqpre_ref, dkvo_ref, bst_ref, wst_ref):
        _stat_init(bst_ref, wst_ref)
        mod = mod_ref[0]
        c1, s1, s2 = tc_ref[...], ts1_ref[...], ts2_ref[...]
        c8, s18, s28 = jnp.tile(c1, (1, nh)), jnp.tile(s1, (1, nh)), jnp.tile(s2, (1, nh))
        dqpre = _rope_t(dqp_ref[0].astype(F32), c8, s18, s28).astype(BF16)
        dqpre_ref[0] = dqpre
        gq = gq_ref[...]
        cq = cq_ref[0]
        rq = lax.rsqrt(jnp.mean(cq * cq, axis=-1, keepdims=True) + RMS_EPS)
        dqn = _dot_nt(dqpre, wuq_ref[...])
        wst_ref[4:5, 0:qr] += _colsum(dqn * cq * rq)
        dqg = dqn * gq
        dcq = rq * dqg - cq * (rq * rq * rq) * jnp.mean(dqg * cq, axis=-1, keepdims=True)

        dkpre = _rope_t(dkp_ref[0].astype(F32), c8, s18, s28)
        dkr = dkpre[:, 0:HEAD_PAD]
        for h in range(1, nh):
            dkr = dkr + dkpre[:, h * HEAD_PAD:(h + 1) * HEAD_PAD]
        lane = lax.broadcasted_iota(jnp.int32, (tm, LANES), 1)
        dkr = jnp.where((lane >= MLA_NOPE) & (lane < MLA_NOPE + MLA_ROPE), dkr, 0.0)
        dkr = pltpu.roll(dkr, LANES - MLA_NOPE, 1)
        dkvo = jnp.concatenate([dkpre.astype(BF16), dmv_ref[0]], axis=1)
        dkvo_ref[0] = dkvo
        gkv = gkv_ref[...]
        ckv = ckv_ref[0]
        rkv = lax.rsqrt(jnp.mean(ckv * ckv, axis=-1, keepdims=True) + RMS_EPS)
        dkvn = _dot_nt(dkvo, wkv_ref[...])
        wst_ref[5:6, 0:kvr] += _colsum(dkvn * ckv * rkv)
        dkg = dkvn * gkv
        dckv = rkv * dkg - ckv * (rkv * rkv * rkv) * jnp.mean(dkg * ckv, axis=-1, keepdims=True)

        dproj = jnp.concatenate([dq_ref[0], dk_ref[0], dv_ref[0], dcq.astype(BF16), dckv.astype(BF16),
                                 dkr.astype(BF16)], axis=1)
        dproj_ref[0] = dproj
        dh = _dot_nt(dproj, win_ref[...])
        x0 = x0_ref[0]
        dx0 = dx0a_ref[0] + dh * (1.0 + mod[1:2])
        bst_ref[0, 0:1, :] += _colsum(dh * x0)
        bst_ref[0, 1:2, :] += _colsum(dh)
        g = g_ref[...]
        _, xhat, rstd = _ln_fwd(x_ref[0], g, 0.0)
        gx_ref[0] = _ln_bwd(dx0, xhat, rstd, g)
        wst_ref[0:1, :] += _colsum(dx0 * xhat)
        wst_ref[1:2, :] += _colsum(dx0)

    tab = pl.BlockSpec((tm, LANES), lambda b, s: (s, 0))
    sspecs, sshapes = _stat_specs(B, D)
    return pl.pallas_call(
        body, name="inproj_bwd", grid=(B, S // tm),
        in_specs=[_tok(tm, D), _tok(tm, D), _tok(tm, D), _perb(N_MOD, D), _full(ln_g),
                  _tok(tm, sbw), _tok(tm, sbw), _tok(tm, sbw), _tok(tm, qpw), _tok(tm, qpw), _tok(tm, nh * MLA_V),
                  _tok(tm, qr), _tok(tm, kvr), _full(w_in_p), _full(w_uq_p), _full(w_kv), _full(gq), _full(gkv),
                  tab, tab, tab],
        out_specs=[_tok(tm, D), _tok(tm, dinp), _tok(tm, qpw), _tok(tm, kvw)] + sspecs,
        out_shape=[_sds((B, S, D), F32), _sds((B, S, dinp), BF16), _sds((B, S, qpw), BF16),
                   _sds((B, S, kvw), BF16)] + sshapes,
        compiler_params=_cparams(("arbitrary", "arbitrary")),
    )(x, x0, dx0a, mod, ln_g, dq, dk, dv, dqp, dkp, dmv, cq, ckv, w_in_p, w_uq_p, w_kv, gq, gkv, tc, ts1, ts2)


def _tile_of(n, cap):
    if n <= cap:
        return n
    best = n
    for t in range(LANES, cap + 1, LANES):
        if n % t == 0:
            best = t
    return best


def _mm_tn(a, g, name, after, relu_sq=False, out_dtype=F32, col_blocks=None):
    T, K = a.shape
    N = g.shape[1]
    tt = 1024 if T % 1024 == 0 else (512 if T % 512 == 0 else T)
    tk = _tile_of(K, 1024)
    tn = _tile_of(N, 1280)
    nt = T // tt
    bw = N // col_blocks if col_blocks else tn
    assert tn % bw == 0

    def body(a_ref, g_ref, _, o_ref, acc_ref):
        @pl.when(pl.program_id(2) == 0)
        def _():
            acc_ref[...] = jnp.zeros_like(acc_ref)

        av = a_ref[...]
        if relu_sq:
            av = jnp.square(jnp.maximum(av.astype(F32), 0.0)).astype(BF16)
        acc_ref[...] += _dot_tn(av, g_ref[...])

        @pl.when(pl.program_id(2) == nt - 1)
        def _():
            if col_blocks:
                for c in range(tn // bw):
                    o_ref[c] = acc_ref[:, c * bw:(c + 1) * bw].astype(out_dtype)
            else:
                o_ref[...] = acc_ref[...].astype(out_dtype)

    if col_blocks:
        out_spec = pl.BlockSpec((tn // bw, tk, bw), lambda i, j, t: (j, i, 0))
        out_shape = _sds((col_blocks, K, bw), out_dtype)
    else:
        out_spec = pl.BlockSpec((tk, tn), lambda i, j, t: (i, j))
        out_shape = _sds((K, N), out_dtype)
    return pl.pallas_call(
        body, name=name, grid=(K // tk, N // tn, nt),
        in_specs=[pl.BlockSpec((tt, tk), lambda i, j, t: (t, i)), pl.BlockSpec((tt, tn), lambda i, j, t: (t, j)),
                  _AFTER],
        out_specs=out_spec, out_shape=out_shape,
        scratch_shapes=[pltpu.VMEM((tk, tn), F32)],
        compiler_params=_cparams(("parallel", "parallel", "arbitrary")),
    )(a, g, after)


def _reduce_adamw(parts, w, m, v, name):
    P, K, N = parts.shape
    tr = 256 if K % 256 == 0 else K

    def body(p_ref, w_ref, m_ref, v_ref, g_ref, d_ref, nm_ref, nv_ref):
        g = p_ref[0].astype(F32)
        for k in range(1, P):
            g = g + p_ref[k].astype(F32)
        g_ref[0] = g
        d_ref[0], nm_ref[0], nv_ref[0] = _adamw(w_ref[0], g, m_ref[0], v_ref[0])

    spec = pl.BlockSpec((1, tr, N), lambda r: (0, r, 0))
    return pl.pallas_call(
        body, name=name, grid=(K // tr,),
        in_specs=[pl.BlockSpec((P, tr, N), lambda r: (0, r, 0)), spec, spec, spec],
        out_specs=[spec] * 4, out_shape=[_sds((1, K, N), F32)] * 4,
        compiler_params=_cparams(("parallel",)),
    )(parts, w, m, v)


def _finish(sm, dmod_all, dmod_my, cact_all, p_small, m_small, v_small, b_ada, m_b, v_b, w_ada, m_w, v_w):
    n0 = p_small.shape[1]
    n1 = sm.shape[1]
    d = cact_all.shape[1]

    def body(sm_ref, dma_ref, dmm_ref, ca_ref, p_ref, pm_ref, pv_ref, b_ref, bm_ref, bv_ref, w_ref, wm_ref, wv_ref,
             gs_ref, ds_ref, ms_ref, vs_ref, gb_ref, db_ref, mb_ref, vb_ref, gw_ref, dw_ref, mw_ref, vw_ref,
             loss_ref):
        gs = sm_ref[0:1, :]
        for k in range(1, N_DEV):
            gs = gs + sm_ref[k:k + 1, :]
        gs_ref[...] = gs
        ds_ref[...], ms_ref[...], vs_ref[...] = _adamw(p_ref[...], gs[:, 0:n0], pm_ref[...], pv_ref[...])
        loss_ref[...] = jnp.zeros((1, LANES), F32) + jnp.sum(gs[:, n1 - d:n1])
        gb = jnp.sum(dma_ref[...], axis=0, keepdims=True)
        gb_ref[...] = gb
        db_ref[...], mb_ref[...], vb_ref[...] = _adamw(b_ref[...], gb, bm_ref[...], bv_ref[...])
        gw = _dot_tn(ca_ref[...].astype(BF16), dmm_ref[...].astype(BF16))
        gw_ref[...] = gw
        dw_ref[...], mw_ref[...], vw_ref[...] = _adamw(w_ref[...], gw, wm_ref[...], wv_ref[...])

    s0 = _sds(p_small.shape, F32)
    sb = _sds(b_ada.shape, F32)
    sw = _sds(w_ada.shape, F32)
    return pl.pallas_call(
        body, name="finish_small",
        out_shape=[_sds((1, n1), F32), s0, s0, s0, sb, sb, sb, sb, sw, sw, sw, sw,
                   _sds((1, LANES), F32)],
        compiler_params=pltpu.CompilerParams(vmem_limit_bytes=VMEM_LIMIT),
    )(sm, dmod_all, dmod_my, cact_all, p_small, m_small, v_small, b_ada, m_b, v_b, w_ada, m_w, v_w)


def _pack(arrs, dtype, width):
    flat = jnp.concatenate([a.astype(dtype).reshape(-1) for a in arrs])
    rows = -(-flat.shape[0] // (256 * width)) * 256
    return jnp.pad(flat, (0, rows * width - flat.shape[0])).reshape(rows, width)


def _unpack(slab, shapes):
    flat = slab.reshape(-1)
    out, o = [], 0
    for s in shapes:
        n = math.prod(s)
        out.append(flat[o:o + n].reshape(s))
        o += n
    return out


def _rope_tables(S):
    inv_freq = 1.0 / (ROPE_BASE ** (jnp.arange(0, MLA_ROPE, 2, dtype=F32) / MLA_ROPE))
    ang = jnp.arange(S, dtype=F32)[:, None] * inv_freq[None, :]
    cos, sin = jnp.cos(ang), jnp.sin(ang)
    one = jnp.ones((S, MLA_NOPE), F32)
    z16 = jnp.zeros((S, 16), F32)
    z32 = jnp.zeros((S, 32), F32)
    z64 = jnp.zeros((S, MLA_NOPE), F32)
    tc = jnp.concatenate([one, cos, cos, jnp.ones((S, 32), F32)], axis=1)
    ts1 = jnp.concatenate([z64, -sin, z16, z32], axis=1)
    ts2 = jnp.concatenate([z64, z16, sin, z32], axis=1)
    return tc, ts1, ts2


def kernel(x, c, ln_in_g, ln_in_b, w_ada, b_ada, w_in, q_norm_g, kv_norm_g, w_uq, w_ukv, w_o, ln1_g, ln1_b, w_up, w_down, ln2_g, ln2_b, loss_target, m_ln_in_g, m_ln_in_b, m_w_ada, m_b_ada, m_w_in, m_q_norm_g, m_kv_norm_g, m_w_uq, m_w_ukv, m_w_o, m_ln1_g, m_ln1_b, m_w_up, m_w_down, m_ln2_g, m_ln2_b, v_ln_in_g, v_ln_in_b, v_w_ada, v_b_ada, v_w_in, v_q_norm_g, v_kv_norm_g, v_w_uq, v_w_ukv, v_w_o, v_ln1_g, v_ln1_b, v_w_up, v_w_down, v_ln2_g, v_ln2_b):
    B, S, D = x.shape
    sbw = D // 2
    mlw = D - sbw
    nh = mlw // MLA_V
    qr = w_uq.shape[1]
    kvr = w_ukv.shape[1]
    qk = MLA_NOPE + MLA_ROPE
    dff = w_up.shape[2] * N_DEV
    din = w_in.shape[2] * N_DEV
    tm = 512 if S % 512 == 0 else S
    tq = min(512, S // 2)
    dm = dict(tm=tm, tm_small=min(tm, 256), tq=tq, sbw=sbw, qr=qr, kvr=kvr, nh=nh)
    dev =4 * lax.axis_index("x") + 2 * lax.axis_index("y") + lax.axis_index("c")

    big = [w_in, w_uq, w_ukv, w_o, w_up, w_down]
    first_w, first_token = _chip_exchange_start([a[0].astype(BF16) for a in big[:3]], "gather_w_first_start",
                                                scatter=False, after=c)

    nada = w_ada.shape[2]
    c_all = _all_gather([c + first_token[0, 0]], "gather_c")[0].reshape(N_DEV * B, D)
    b_loc = lax.dynamic_slice(b_ada, (0, dev * nada), (1, nada))
    cact_all, mod_part = _ada_partial(c_all, w_ada[0], b_loc)
    mod_all = _all_gather([mod_part], "gather_mod")[0]
    mod = lax.dynamic_slice(mod_all, (0, dev * B, 0), (N_DEV, B, nada))
    mod = jnp.swapaxes(mod, 0, 1).reshape(B, N_MOD, D)

    first_by_chip = _chip_exchange_wait(first_w, mod_all, "gather_w_first_wait")
    w_in8, w_uq8, w_ukv8 = [b.reshape((N_DEV,) + b.shape[2:]) for b in _core_gather(first_by_chip, "gather_w_first_cores")]
    late_w, late_token = _chip_exchange_start([a[0].astype(BF16) for a in big[3:]], "gather_w_late_start",
                                              scatter=False, after=w_in8, everyone=True)
    cols = lambda a8: jnp.swapaxes(a8, 0, 1).reshape(a8.shape[1], N_DEV * a8.shape[2])
    w_in_p = jnp.pad(cols(w_in8), ((0, 0), (0, LANES - MLA_ROPE)))
    zpad = jnp.zeros((qr, nh, HEAD_PAD - qk), BF16)
    w_uq_p = jnp.concatenate([cols(w_uq8).reshape(qr, nh, qk), zpad], axis=2).reshape(qr, nh * HEAD_PAD)
    w_ukv_f = cols(w_ukv8)
    w_uk = w_ukv_f[:, :nh * MLA_NOPE].reshape(kvr, nh, MLA_NOPE)
    w_uk_p = jnp.concatenate([w_uk, jnp.zeros((kvr, nh, HEAD_PAD - MLA_NOPE), BF16)], axis=2)
    w_kv = jnp.concatenate([w_uk_p.reshape(kvr, nh * HEAD_PAD), w_ukv_f[:, nh * MLA_NOPE:]], axis=1)

    tc, ts1, ts2 = _rope_tables(S)
    g_in, b_in = ln_in_g.reshape(1, D), ln_in_b.reshape(1, D)
    (x0, h, sq, sk, sv, qp, kp, mv, cq, ckv, qn, kvn) = _inproj_fwd(
        x, mod, g_in, b_in, w_in_p, w_uq_p, w_kv, q_norm_g, kv_norm_g, tc, ts1, ts2, dm, late_token)
    sb_y, sb_tot = _sb_fwd(sq, sk, sv, dm)
    mla_y, mla_lse = _mla_fwd(qp, kp, mv, dm, sb_tot)
    w_o8, w_up8, w_down8 = _chip_exchange_wait(late_w, mla_lse, "gather_w_late_wait")
    w_o_f = w_o8.reshape(D, D)
    mix, x1, h2 = _outproj_fwd(sb_y, mla_y, x0, mod, w_o_f, ln1_g, ln1_b, dm)
    u, dr2, bst_c, wst_c = _mlp_fwd(h2, x1, mod, loss_target, w_up8, w_down8, ln2_g, ln2_b, dm)

    du, dffb, dmixb, dx0a, dsb_y, dmla_y, bst_b, wst_b = _mlp_bwd(
        dr2, u, x1, x0, mix, mod, w_up8, w_down8, w_o_f, ln1_g, dm)
    T = B * S
    r2 = lambda a: a.reshape(T, a.shape[2])
    by_core = lambda a: a.reshape((4, 2) + a.shape[1:])
    g_o = jnp.concatenate([_mm_tn(r2(sb_y), r2(dmixb), "grad_w_o_sb", dr2, out_dtype=BF16),
                           _mm_tn(r2(mla_y), r2(dmixb), "grad_w_o_mla", dr2, out_dtype=BF16)], axis=0)
    g_up8 = _mm_tn(r2(h2), r2(du), "grad_w_up", dr2, out_dtype=BF16, col_blocks=N_DEV)
    g_down = _mm_tn(r2(u), r2(dffb), "grad_w_down", dr2, relu_sq=True, out_dtype=BF16)
    early = [g_o.reshape(N_DEV, D // N_DEV, D), g_up8, g_down.reshape(N_DEV, dff // N_DEV, D)]
    early_g, early_token = _chip_exchange_start(early, "scatter_g_early_start", scatter=True, after=dr2,
                                                everyone=True)

    dsq, dsk, dsv = _sb_bwd(sq, sk, sv, sb_tot, dsb_y, dm, early_token)
    dqp, dkp, dmv = _mla_bwd(qp, kp, mv, mla_y, mla_lse, dmla_y, dm, dsq)
    grad_x, dproj, dqpre, dkvo, bst_a, wst_a = _inproj_bwd(
        x, x0, dx0a, mod, g_in, dsq, dsk, dsv, dqp, dkp, dmv, cq, ckv, w_in_p, w_uq_p, w_kv, q_norm_g, kv_norm_g,
        tc, ts1, ts2, dm)

    dmod = jnp.concatenate([bst_a[:, 1], bst_a[:, 0], bst_b[:, 2], bst_b[:, 1], bst_b[:, 0], bst_c[:, 0]], axis=1)
    small = jnp.concatenate([wst_a[0], wst_a[1], wst_a[4, :qr], wst_a[5, :kvr], wst_b[0], wst_b[1],
                             wst_c[0], wst_c[1], wst_c[2]])
    n1 = small.shape[0]
    small_g, small_token = _chip_exchange_start([_pack([dmod, small], F32, LANES)], "gather_small_start",
                                                scatter=False, after=grad_x)
    g_in_p = _mm_tn(r2(h), r2(dproj), "grad_w_in", small_token)
    g_uq_p = _mm_tn(r2(qn), r2(dqpre), "grad_w_uq", small_token)
    g_kv = _mm_tn(r2(kvn), r2(dkvo), "grad_w_kv", small_token)
    small_by_chip = _chip_exchange_wait(small_g, g_kv, "gather_small_wait")
    both = _core_gather(small_by_chip, "gather_small_cores")[0].reshape(N_DEV, -1)
    g_uq_f = g_uq_p.reshape(qr, nh, HEAD_PAD)[:, :, :qk].reshape(qr, nh * qk)
    g_uk = g_kv[:, :nh * HEAD_PAD].reshape(kvr, nh, HEAD_PAD)[:, :, :MLA_NOPE].reshape(kvr, nh * MLA_NOPE)
    g_ukv_f = jnp.concatenate([g_uk, g_kv[:, nh * HEAD_PAD:]], axis=1)
    early_quarter = _chip_exchange_wait(early_g, g_kv, "scatter_g_early_wait")

    def by_dest_cols(a):
        k, n = a.shape[0], a.shape[1] // N_DEV
        return jnp.swapaxes(a.reshape(k, N_DEV, n), 0, 1).astype(BF16)

    last = [by_dest_cols(g_in_p[:, :din]), by_dest_cols(g_uq_f), by_dest_cols(g_ukv_f)]
    last_sum = _core_scatter_sum([by_core(a) for a in last], "scatter_g_last_cores")
    last_g, last_token = _chip_exchange_start(last_sum, "scatter_g_last_start", scatter=True, after=grad_x)
    names = ["w_in", "w_uq", "w_ukv", "w_o", "w_up", "w_down"]
    moms = [m_w_in, m_w_uq, m_w_ukv, m_w_o, m_w_up, m_w_down]
    vars_ = [v_w_in, v_w_uq, v_w_ukv, v_w_o, v_w_up, v_w_down]
    res_early = [_reduce_adamw(p, w, m, v, "adamw_" + n)
                 for p, w, m, v, n in zip(early_quarter, big[3:], moms[3:], vars_[3:], names[3:])]

    dmod_all = both[:, :B * N_MOD * D].reshape(N_DEV * B, N_MOD * D)
    sm = both[:, B * N_MOD * D:B * N_MOD * D + n1] + last_token[0, 0]
    dmod_my = lax.dynamic_slice(dmod_all, (0, dev * nada), (N_DEV * B, nada))
    row = lambda arrs: jnp.concatenate([a.reshape(1, -1) for a in arrs], axis=1)
    smalls = [ln_in_g, ln_in_b, q_norm_g, kv_norm_g, ln1_g, ln1_b, ln2_g, ln2_b]
    small_shapes = [a.shape for a in smalls]
    (gs, ds, nms, nvs, g_b, d_b, nm_b, nv_b, g_w, d_w, nm_w, nv_w, loss_v) = _finish(
        sm, dmod_all, dmod_my, cact_all, row(smalls),
        row([m_ln_in_g, m_ln_in_b, m_q_norm_g, m_kv_norm_g, m_ln1_g, m_ln1_b, m_ln2_g, m_ln2_b]),
        row([v_ln_in_g, v_ln_in_b, v_q_norm_g, v_kv_norm_g, v_ln1_g, v_ln1_b, v_ln2_g, v_ln2_b]),
        b_ada, m_b_ada, v_b_ada, w_ada[0], m_w_ada[0], v_w_ada[0])
    gsm, dsm, nmsm, nvsm = (_unpack(s, small_shapes) for s in (gs, ds, nms, nvs))
    last_quarter = _chip_exchange_wait(last_g, loss_v, "scatter_g_last_wait")
    res_last = [_reduce_adamw(p, w, m, v, "adamw_" + n)
                for p, w, m, v, n in zip(last_quarter, big[:3], moms[:3], vars_[:3], names[:3])]
    gb, db, nmb, nvb = ([r[i] for r in res_last + res_early] for i in range(4))

    def ordered(sm_l, w_l, ada_w, ada_b):
        return [sm_l[0], sm_l[1], ada_w[None], ada_b, w_l[0], sm_l[2], sm_l[3], w_l[1], w_l[2], w_l[3],
                sm_l[4], sm_l[5], w_l[4], w_l[5], sm_l[6], sm_l[7]]

    loss = loss_v[0, 0]
    return (loss, grad_x, *ordered(gsm, gb, g_w, g_b), *ordered(dsm, db, d_w, d_b),
            *ordered(nmsm, nmb, nm_w, nm_b), *ordered(nvsm, nvb, nv_w, nv_b))
```

```python
import math

import jax
import jax.numpy as jnp
from jax import lax
from jax.experimental import pallas as pl
from jax.experimental.pallas import tpu as pltpu

F32 = jnp.float32
BF16 = jnp.bfloat16

SB_HD = 64
MLA_V = 64
MLA_NOPE = 64
MLA_ROPE = 32
HEAD_PAD = 128
CHUNK = 64
ROPE_BASE = 10000.0
LN_EPS = 1e-5
RMS_EPS = 1e-6
DEPTH = 1
ALPHA = (2.0 * DEPTH) ** 0.25
N_MOD = 6
ADAM_LR = 0.001
ADAM_B1 = 0.9
ADAM_B2 = 0.999
ADAM_EPS = 1e-08
ADAM_WD = 0.01
ADAM_STEP = 10
N_DEV = 8
LANES = 128
LOG2E = 1.4426950408889634
CUMSUM_W = 256
VMEM_LIMIT = 56 * 1024 * 1024
MESH = pl.DeviceIdType.MESH


def _dot(a, b):
    return jnp.dot(a, b, preferred_element_type=F32)


def _dot_nt(a, b):
    return lax.dot_general(a, b, (((1,), (1,)), ((), ())), preferred_element_type=F32)


def _dot_tn(a, b):
    return lax.dot_general(a, b, (((0,), (0,)), ((), ())), preferred_element_type=F32)


def _cparams(sem):
    return pltpu.CompilerParams(dimension_semantics=sem, vmem_limit_bytes=VMEM_LIMIT)


def _full(a):
    nd = a.ndim
    return pl.BlockSpec(a.shape, lambda *_: (0,) * nd, pipeline_mode=pl.Buffered(1))


def _tok(tm, w):
    return pl.BlockSpec((1, tm, w), lambda b, s: (b, s, 0))


def _perb(rows, w):
    return pl.BlockSpec((1, rows, w), lambda b, s: (b, 0, 0))


def _sds(shape, dtype):
    return jax.ShapeDtypeStruct(shape, dtype)


def _ln_fwd(x, g, b):
    mu = jnp.mean(x, axis=-1, keepdims=True)
    xc = x - mu
    var = jnp.mean(xc * xc, axis=-1, keepdims=True)
    rstd = lax.rsqrt(var + LN_EPS)
    xhat = xc * rstd
    return xhat * g + b, xhat, rstd


def _ln_bwd(dy, xhat, rstd, g):
    dxh = dy * g
    m1 = jnp.mean(dxh, axis=-1, keepdims=True)
    m2 = jnp.mean(dxh * xhat, axis=-1, keepdims=True)
    return rstd * (dxh - m1 - xhat * m2)


def _colsum(a):
    return jnp.sum(a, axis=0, keepdims=True)


def _rope(x, c, s1, s2):
    w = x.shape[-1]
    return x * c + pltpu.roll(x, w - 16, 1) * s1 + pltpu.roll(x, 16, 1) * s2


def _rope_t(x, c, s1, s2):
    w = x.shape[-1]
    return x * c - pltpu.roll(x, w - 16, 1) * s1 - pltpu.roll(x, 16, 1) * s2


def _adamw(w, g, m, v):
    m = ADAM_B1 * m + (1.0 - ADAM_B1) * g
    v = ADAM_B2 * v + (1.0 - ADAM_B2) * (g * g)
    m_hat = m / (1.0 - ADAM_B1 ** ADAM_STEP)
    v_hat = v / (1.0 - ADAM_B2 ** ADAM_STEP)
    delta = -ADAM_LR * (m_hat / (jnp.sqrt(v_hat) + ADAM_EPS) + ADAM_WD * w)
    return delta, m, v


def _my_place():
    return lax.axis_index("x"), lax.axis_index("y"), lax.axis_index("c")


def _chip_peers(mx, my):
    out = []
    for j in (1, 2, 3):
        px = 1 - mx if (j >> 1) else mx
        py = 1 - my if (j & 1) else my
        out.append((px, py, 2 * px + py))
    return out


def _split_peers(everyone):
    mx, my, mc = _my_place()
    if not everyone:
        return [(px, py, mc, pk) for px, py, pk in _chip_peers(mx, my)], 2 * mx + my
    peers = []
    for j in range(1, N_DEV):
        px = 1 - mx if (j >> 2) & 1 else mx
        py = 1 - my if (j >> 1) & 1 else my
        pc = 1 - mc if j & 1 else mc
        peers.append((px, py, pc, 4 * px + 2 * py + pc))
    return peers, 4 * mx + 2 * my + mc


def _hbm_call(body, name, n_in, out_shape, sems):
    hbm = pl.BlockSpec(memory_space=pl.ANY)
    return pl.pallas_call(
        body, name=name, out_shape=out_shape,
        in_specs=[hbm] * n_in, out_specs=[hbm] * len(out_shape),
        scratch_shapes=[pltpu.SemaphoreType.DMA(s) for s in sems])


def _chip_exchange(xs, name, scatter):
    n = len(xs)

    def body(*refs):
        x_refs, o_refs = refs[:n], refs[n:2 * n]
        ssem, rsem, lsem = refs[2 * n:]
        mx, my, mc = _my_place()
        me = 2 * mx + my
        peers = _chip_peers(mx, my)

        def copy(i, j, src_slot, dst_slot):
            px, py, _ = peers[j]
            return pltpu.make_async_remote_copy(
                src_ref=x_refs[i].at[src_slot] if scatter else x_refs[i], dst_ref=o_refs[i].at[dst_slot],
                send_sem=ssem.at[i, j], recv_sem=rsem.at[i, j], device_id=(px, py, mc), device_id_type=MESH)

        local = [pltpu.make_async_copy(x_refs[i].at[me] if scatter else x_refs[i], o_refs[i].at[me], lsem.at[i])
                 for i in range(n)]
        sends = [copy(i, j, peers[j][2], me) for i in range(n) for j in range(3)]
        for cp in local + sends:
            cp.start()
        for i in range(n):
            for j in range(3):
                copy(i, j, peers[j][2], peers[j][2]).wait_recv()
        for cp in sends:
            cp.wait_send()
        for cp in local:
            cp.wait()

    out_shape = [_sds((4,) + tuple(x.shape[1:] if scatter else x.shape), x.dtype) for x in xs]
    return _hbm_call(body, name, n, out_shape, [(n, 3), (n, 3), (n,)])(*xs)


def _chip_exchange_start(xs, name, scatter, after, everyone=False):
    n = len(xs)
    npeer = N_DEV - 1 if everyone else 3
    blks = [tuple(x.shape[1:] if scatter else x.shape) for x in xs]

    def body(*refs):
        x_refs, land_refs = refs[:n], refs[n:2 * n]
        ssem, rsem = refs[2 * n + 1], refs[2 * n + 2]
        token = refs[-1]
        peers, me = _split_peers(everyone)
        for i in range(n):
            for j, (px, py, pc, slot) in enumerate(peers):
                pltpu.make_async_remote_copy(
                    src_ref=x_refs[i].at[slot] if scatter else x_refs[i], dst_ref=land_refs[i].at[me],
                    send_sem=ssem.at[npeer * i + j], recv_sem=rsem.at[npeer * i + j], device_id=(px, py, pc),
                    device_id_type=MESH).start()
        token[...] = jnp.zeros_like(token)

    hbm = pl.BlockSpec(memory_space=pltpu.HBM)
    sem = pl.BlockSpec(memory_space=pltpu.SEMAPHORE)
    lands = [lax.empty((npeer + 1,) + b, x.dtype) for b, x in zip(blks, xs)]
    res = pl.pallas_call(
        body, name=name,
        out_shape=[pltpu.SemaphoreType.DMA((npeer * n,)), pltpu.SemaphoreType.DMA((npeer * n,))]
        + [pltpu.HBM(x.shape, x.dtype) for x in xs] + [pltpu.HBM(l.shape, l.dtype) for l in lands]
        + [_sds((8, LANES), F32)],
        in_specs=[hbm] * (2 * n) + [_AFTER],
        out_specs=[sem, sem] + [hbm] * (2 * n) + [pl.BlockSpec(memory_space=pltpu.VMEM)],
        input_output_aliases={i: 2 + i for i in range(2 * n)},
        compiler_params=pltpu.CompilerParams(has_side_effects=pltpu.SideEffectType.DATAFLOW_SIDE_EFFECTING),
    )(*[pltpu.with_memory_space_constraint(a, pltpu.HBM) for a in list(xs) + lands], after)
    return dict(ssem=res[0], rsem=res[1], xs=res[2:2 + n], lands=res[2 + n:2 + 2 * n], n=n, scatter=scatter,
                everyone=everyone), res[-1]


def _chip_exchange_wait(handle, after, name):
    n, scatter, everyone = handle["n"], handle["scatter"], handle["everyone"]
    npeer = N_DEV - 1 if everyone else 3

    def body(*refs):
        x_refs, land_refs = refs[:n], refs[n:2 * n]
        ssem, rsem = refs[2 * n], refs[2 * n + 1]
        peers, _ = _split_peers(everyone)
        for i in range(n):
            for j, (px, py, pc, slot) in enumerate(peers):
                cp = pltpu.make_async_remote_copy(
                    src_ref=x_refs[i].at[slot] if scatter else x_refs[i], dst_ref=land_refs[i].at[slot],
                    send_sem=ssem.at[npeer * i + j], recv_sem=rsem.at[npeer * i + j], device_id=(px, py, pc),
                    device_id_type=MESH)
                cp.wait_send()
                cp.wait_recv()

    hbm = pl.BlockSpec(memory_space=pltpu.HBM)
    sem = pl.BlockSpec(memory_space=pltpu.SEMAPHORE)
    ops = list(handle["xs"]) + list(handle["lands"])
    res = pl.pallas_call(
        body, name=name,
        out_shape=[pltpu.HBM(a.shape, a.dtype) for a in ops],
        in_specs=[hbm] * (2 * n) + [sem, sem, pl.BlockSpec(memory_space=pl.ANY)],
        out_specs=[hbm] * (2 * n),
        input_output_aliases={i: i for i in range(2 * n)},
        compiler_params=pltpu.CompilerParams(has_side_effects=pltpu.SideEffectType.DATAFLOW_SIDE_EFFECTING),
    )(*ops, handle["ssem"], handle["rsem"], after)
    me = 2 * lax.axis_index("x") + lax.axis_index("y")
    if everyone:
        me = 2 * me + lax.axis_index("c")
    out = []
    for x, land in zip(res[:n], res[n:]):
        own = lax.dynamic_index_in_dim(x, me, 0, keepdims=False) if scatter else x
        out.append(lax.dynamic_update_index_in_dim(land, own, me, 0))
    return out


def _core_gather(xs, name):
    n = len(xs)

    def body(*refs):
        x_refs, o_refs, mine, got = refs[:n], refs[n:2 * n], refs[2 * n:3 * n], refs[3 * n:4 * n]
        lsem, ssem, rsem, osem = refs[4 * n:]
        mx, my, mc = _my_place()
        loads = [pltpu.make_async_copy(x_refs[i], mine[i], lsem.at[i]) for i in range(n)]
        for cp in loads:
            cp.start()
        sends, stores = [], []
        for i in range(n):
            loads[i].wait()
            cp = pltpu.make_async_remote_copy(
                src_ref=mine[i], dst_ref=got[i], send_sem=ssem.at[i], recv_sem=rsem.at[i],
                device_id=(mx, my, 1 - mc), device_id_type=MESH)
            cp.start()
            sends.append(cp)
            for k in range(4):
                st = pltpu.make_async_copy(mine[i].at[k], o_refs[i].at[k, mc], osem.at[i, k])
                st.start()
                stores.append(st)
        for i in range(n):
            sends[i].wait_recv()
            for k in range(4):
                st = pltpu.make_async_copy(got[i].at[k], o_refs[i].at[k, 1 - mc], osem.at[n + i, k])
                st.start()
                stores.append(st)
        for cp in sends:
            cp.wait_send()
        for st in stores:
            st.wait()

    hbm = pl.BlockSpec(memory_space=pl.ANY)
    bufs = [pltpu.VMEM(x.shape, x.dtype) for x in xs]
    return pl.pallas_call(
        body, name=name,
        out_shape=[_sds((4, 2) + tuple(x.shape[1:]), x.dtype) for x in xs],
        in_specs=[hbm] * n, out_specs=[hbm] * n,
        scratch_shapes=bufs + bufs + [pltpu.SemaphoreType.DMA((n,)), pltpu.SemaphoreType.DMA((n,)),
                                      pltpu.SemaphoreType.DMA((n,)), pltpu.SemaphoreType.DMA((2 * n, 4))],
        compiler_params=pltpu.CompilerParams(vmem_limit_bytes=VMEM_LIMIT),
    )(*xs)


def _rows_step(k):
    for r in (256, 128, 64, 32, 16, 8):
        if k % r == 0:
            return r
    return k


def _core_scatter_sum(gs, name):
    n = len(gs)

    def body(*refs):
        g_refs, o_refs = refs[:n], refs[n:2 * n]
        send, got, mine = refs[2 * n:3 * n], refs[3 * n:4 * n], refs[4 * n:5 * n]
        lsem, msem, ssem, rsem, osem = refs[5 * n:]
        mx, my, mc = _my_place()
        pairs = [(i, k) for i in range(n) for k in range(4)]
        out_loads = {(i, k): pltpu.make_async_copy(g_refs[i].at[k, 1 - mc], send[i].at[k], lsem.at[i, k])
                     for i, k in pairs}
        own_loads = {(i, k): pltpu.make_async_copy(g_refs[i].at[k, mc], mine[i].at[k], msem.at[i, k])
                     for i, k in pairs}
        for p in pairs:
            out_loads[p].start()
        for p in pairs:
            own_loads[p].start()
        sends = []
        for i in range(n):
            for k in range(4):
                out_loads[i, k].wait()
            cp = pltpu.make_async_remote_copy(
                src_ref=send[i], dst_ref=got[i], send_sem=ssem.at[i], recv_sem=rsem.at[i],
                device_id=(mx, my, 1 - mc), device_id_type=MESH)
            cp.start()
            sends.append(cp)
        stores = []
        for i in range(n):
            for k in range(4):
                own_loads[i, k].wait()
            sends[i].wait_recv()
            rows = g_refs[i].shape[2]
            step = _rows_step(rows)

            def add(r, _, i=i, step=step):
                sl = pl.ds(pl.multiple_of(r * step, step), step)
                for k in range(4):
                    mine[i][k, sl, :] = (mine[i][k, sl, :].astype(F32) + got[i][k, sl, :].astype(F32)).astype(BF16)
                return 0

            lax.fori_loop(0, rows // step, add, 0)
            st = pltpu.make_async_copy(mine[i], o_refs[i], osem.at[i])
            st.start()
            stores.append(st)
        for cp in sends:
            cp.wait_send()
        for st in stores:
            st.wait()

    hbm = pl.BlockSpec(memory_space=pl.ANY)
    blk = [(4,) + tuple(g.shape[2:]) for g in gs]
    bufs = [pltpu.VMEM(b, BF16) for b in blk]
    return pl.pallas_call(
        body, name=name,
        out_shape=[_sds(b, BF16) for b in blk],
        in_specs=[hbm] * n, out_specs=[hbm] * n,
        scratch_shapes=bufs * 3 + [pltpu.SemaphoreType.DMA((n, 4)), pltpu.SemaphoreType.DMA((n, 4)),
                                   pltpu.SemaphoreType.DMA((n,)), pltpu.SemaphoreType.DMA((n,)),
                                   pltpu.SemaphoreType.DMA((n,))],
        compiler_params=pltpu.CompilerParams(vmem_limit_bytes=VMEM_LIMIT),
    )(*gs)


def _all_gather(xs, name):
    by_chip = _chip_exchange(xs, name + "_chips", scatter=False)
    both = _core_gather(by_chip, name + "_cores")
    return [b.reshape((N_DEV,) + tuple(x.shape)) for b, x in zip(both, xs)]


def _ada_partial(c_all, w_ada_loc, b_loc):
    def body(c_ref, w_ref, b_ref, act_ref, mod_ref):
        c = c_ref[...]
        act = c * (1.0 / (1.0 + jnp.exp(-c)))
        act_ref[...] = act
        mod_ref[...] = _dot(act.astype(BF16), w_ref[...].astype(BF16)) + b_ref[...]

    nb, d = c_all.shape
    return pl.pallas_call(
        body, name="ada_partial",
        out_shape=(_sds((nb, d), F32), _sds((nb, w_ada_loc.shape[1]), F32)),
        compiler_params=pltpu.CompilerParams(vmem_limit_bytes=VMEM_LIMIT),
    )(c_all, w_ada_loc, b_loc)


_AFTER = pl.BlockSpec(memory_space=pl.ANY)


def _inproj_fwd(x, mod, ln_g, ln_b, w_in_p, w_uq_p, w_kv, gq, gkv, tc, ts1, ts2, dm, after):
    B, S, D = x.shape
    tm = dm["tm"]
    sbw, qr, kvr, nh = dm["sbw"], dm["qr"], dm["kvr"], dm["nh"]
    o_cq, o_ckv, o_kr = 3 * sbw, 3 * sbw + qr, 3 * sbw + qr + kvr
    qpw = nh * HEAD_PAD

    def body(x_ref, mod_ref, g_ref, b_ref, win_ref, wuq_ref, wkv_ref, gq_ref, gkv_ref, tc_ref, ts1_ref, ts2_ref, _,
             x0_ref, q_ref, k_ref, v_ref, qp_ref, kp_ref, mv_ref, cq_ref, ckv_ref, qn_ref, kvn_ref):
        x0, _, _ = _ln_fwd(x_ref[0], g_ref[...], b_ref[...])
        x0_ref[0] = x0
        mod = mod_ref[0]
        h = (x0 * (1.0 + mod[1:2]) + mod[0:1]).astype(BF16)
        proj = _dot(h, win_ref[...])
        q_ref[0] = (proj[:, 0:sbw] * SB_Q_SCALE).astype(BF16)
        k_ref[0] = proj[:, sbw:2 * sbw].astype(BF16)
        v_ref[0] = proj[:, 2 * sbw:3 * sbw].astype(BF16)
        cq = proj[:, o_cq:o_cq + qr]
        ckv = proj[:, o_ckv:o_ckv + kvr]
        cq_ref[0] = cq
        ckv_ref[0] = ckv
        qn = (cq * lax.rsqrt(jnp.mean(cq * cq, axis=-1, keepdims=True) + RMS_EPS) * gq_ref[...]).astype(BF16)
        kvn = (ckv * lax.rsqrt(jnp.mean(ckv * ckv, axis=-1, keepdims=True) + RMS_EPS) * gkv_ref[...]).astype(BF16)
        qn_ref[0] = qn
        kvn_ref[0] = kvn
        c1, s1, s2 = tc_ref[...], ts1_ref[...], ts2_ref[...]
        c8, s18, s28 = jnp.tile(c1, (1, nh)), jnp.tile(s1, (1, nh)), jnp.tile(s2, (1, nh))
        qp_ref[0] = (_rope(_dot(qn, wuq_ref[...]), c8, s18, s28) * MLA_Q_SCALE).astype(BF16)
        kvo = _dot(kvn, wkv_ref[...])
        kr = pltpu.roll(proj[:, o_kr:o_kr + LANES], 64, 1)
        kr = _rope(kr, c1, s1, s2)
        kp_ref[0] = (kvo[:, 0:qpw] + jnp.tile(kr, (1, nh))).astype(BF16)
        mv_ref[0] = kvo[:, qpw:].astype(BF16)

    tab = pl.BlockSpec((tm, LANES), lambda b, s: (s, 0))
    outs = [(D, F32), (sbw, BF16), (sbw, BF16), (sbw, BF16), (qpw, BF16), (qpw, BF16),
            (nh * MLA_V, BF16), (qr, F32), (kvr, F32), (qr, BF16), (kvr, BF16)]
    return pl.pallas_call(
        body, name="inproj_fwd", grid=(B, S // tm),
        in_specs=[_tok(tm, D), _perb(N_MOD, D), _full(ln_g), _full(ln_b), _full(w_in_p), _full(w_uq_p),
                  _full(w_kv), _full(gq), _full(gkv), tab, tab, tab, _AFTER],
        out_specs=[_tok(tm, w) for w, _ in outs],
        out_shape=[_sds((B, S, w), t) for w, t in outs],
        compiler_params=_cparams(("parallel", "parallel")),
    )(x, mod, ln_g, ln_b, w_in_p, w_uq_p, w_kv, gq, gkv, tc, ts1, ts2, after)


def _neg_abs(x):
    sign = jnp.uint32(0x80000000)
    return lax.bitcast_convert_type(lax.bitcast_convert_type(x, jnp.uint32) | sign, F32)


SB_Q_SCALE = -(SB_HD ** -0.5) * LOG2E
MLA_Q_SCALE = (MLA_NOPE + MLA_ROPE) ** -0.5 * LOG2E


def _log2_keep(zs):
    return jnp.minimum(zs, 0.0) - jnp.log2(1.0 + jnp.exp2(_neg_abs(zs)))


def _split_dot(a, u):
    hi = a.astype(BF16)
    lo = (a - hi.astype(F32)).astype(BF16)
    return _dot(jnp.concatenate([hi, lo], axis=1), jnp.concatenate([u, u], axis=0))


def _tri(n, rel):
    row = lax.broadcasted_iota(jnp.int32, (n, n), 0)
    col = lax.broadcasted_iota(jnp.int32, (n, n), 1)
    return rel(row, col).astype(BF16)


def _running_sum(a, tri, reverse, split, start):
    cs = tri.shape[0]
    n = a.shape[1] // cs
    out = [None] * n
    run = start
    for c in (reversed(range(n)) if reverse else range(n)):
        part = a[:, c * cs:(c + 1) * cs]
        out[c] = (_split_dot(part, tri) if split else _dot(part.astype(BF16), tri)) + run
        run = run + jnp.sum(part, axis=1, keepdims=True)
    return (out[0] if n == 1 else jnp.concatenate(out, axis=1)), run


def _transpose_bf16(a):
    return a.astype(F32).T.astype(BF16)


def _tile_mask(nr, nk, r0, c0, rel):
    row = lax.broadcasted_iota(jnp.int32, (nr, nk), 0) + r0
    col = lax.broadcasted_iota(jnp.int32, (nr, nk), 1) + c0
    return rel(row, col)


def _put_rows(whole, part, r0):
    return part if r0 == 0 else jnp.concatenate([whole[:r0], part], axis=0)


def _diag_tiles(tq, split):
    half = tq // 2
    return [(0, tq, 0, half), (half, half, half, half)] if split else [(0, tq, 0, tq)]


def _sb_fwd(q, k, v, dm):
    B, S, W = q.shape
    tq = dm["tq"]
    nq = S // tq

    def body(q_ref, k_ref, v_ref, y_ref, tot_ref):
        qi = pl.program_id(2)
        q2 = q_ref[0]
        lane = lax.broadcasted_iota(jnp.int32, (tq, LANES), 1)
        qs = jnp.concatenate([jnp.where(lane < SB_HD, q2, 0), jnp.where(lane >= SB_HD, q2, 0)], axis=0).astype(BF16)
        later = _tri(min(tq, CUMSUM_W), lambda a, b: a > b)
        assert tq & (tq - 1) == 0
        strict = _tile_mask(2 * tq, tq, 0, 0, lambda t, s: s < (t & (tq - 1)))

        def block(j, carry, masked):
            acc, run = carry
            off = pl.multiple_of(j * tq, tq)
            zs = _dot_nt(qs, k_ref[0, pl.ds(off, tq), :])
            a = _log2_keep(zs)
            if masked:
                a = jnp.where(strict, a, 0.0)
            a_later, run = _running_sum(a, later, reverse=True, split=True, start=run)
            w = jnp.exp2((a - zs) + a_later)
            if masked:
                w = jnp.where(strict, w, 0.0)
            return acc + _dot(w.astype(BF16), v_ref[0, pl.ds(off, tq), :]), run

        carry = block(qi, (jnp.zeros((2 * tq, LANES), F32), jnp.zeros((2 * tq, 1), F32)), True)
        acc, run = lax.fori_loop(0, qi, lambda jj, c: block(qi - 1 - jj, c, False), carry)
        y_ref[0] = jnp.where(lane < SB_HD, acc[:tq], acc[tq:]).astype(BF16)
        tot_ref[0] = jnp.where(lane < SB_HD, run[:tq], run[tq:])

    qspec = pl.BlockSpec((1, tq, LANES), lambda b, hp, i: (b, i, hp))
    kspec = pl.BlockSpec((1, S, LANES), lambda b, hp, i: (b, 0, hp))
    return pl.pallas_call(
        body, name="sb_fwd", grid=(B, W // LANES, nq),
        in_specs=[qspec, kspec, kspec],
        out_specs=[qspec, qspec],
        out_shape=[_sds((B, S, W), BF16), _sds((B, S, W), F32)],
        compiler_params=_cparams(("parallel", "parallel", "arbitrary")),
    )(q, k, v)


def _sb_bwd(q, k, v, tot, dy, dm, after):
    B, S, W = q.shape
    tq = dm["tq"]
    nq = S // tq

    def body(q_ref, k_ref, v_ref, tot_ref, dy_ref, _, dq_ref, dk_ref, dv_ref, dk_acc, dv_acc):
        qi = pl.program_id(2)

        @pl.when(qi == 0)
        def _():
            dk_acc[...] = jnp.zeros_like(dk_acc)
            dv_acc[...] = jnp.zeros_like(dv_acc)

        q2 = q_ref[0]
        dy2 = dy_ref[0]
        tot2 = tot_ref[0]
        lane = lax.broadcasted_iota(jnp.int32, (tq, LANES), 1)
        in_h = [lane < SB_HD, lane >= SB_HD]
        qh = [jnp.where(m, q2, 0).astype(BF16) for m in in_h]
        dyh = [jnp.where(m, dy2, 0).astype(BF16) for m in in_h]
        q_t = [_transpose_bf16(a) for a in qh]
        dy_t = [_transpose_bf16(a) for a in dyh]
        toth = [tot2[:, 0:1], tot2[:, SB_HD:SB_HD + 1]]

        def tile(j, carry, r0, nr, c0, nk, masked):
            off = pl.multiple_of(j * tq + c0, math.gcd(tq, c0))
            k2 = k_ref[0, pl.ds(off, nk), :]
            v2 = v_ref[0, pl.ds(off, nk), :]
            upto = _tri(min(nk, CUMSUM_W), lambda a, b: a <= b)
            before = _tri(min(nk, CUMSUM_W), lambda a, b: a < b)
            strict = _tile_mask(nr, nk, r0, c0, lambda t, s: s < t) if masked else None
            rows = slice(r0, r0 + nr)
            new = []
            dk_blk = jnp.zeros((LANES, nk), F32)
            dv_blk = jnp.zeros((LANES, nk), F32)
            for h in range(2):
                dq, pa, pg = carry[3 * h][rows], carry[3 * h + 1][rows], carry[3 * h + 2][rows]
                zs = _dot_nt(qh[h][rows], k2)
                a = _log2_keep(zs)
                if masked:
                    a = jnp.where(strict, a, 0.0)
                a_upto, pa = _running_sum(a, upto, reverse=False, split=True, start=pa)
                w = jnp.exp2((a - zs) - a_upto)
                if masked:
                    w = jnp.where(strict, w, 0.0)
                g = _dot_nt(dyh[h][rows], v2) * w
                g_before, pg = _running_sum(g, before, reverse=False, split=False, start=pg)
                dz = (g + g_before) * jnp.exp2(a) - g_before
                if masked:
                    dz = jnp.where(strict, dz, 0.0)
                dzb = dz.astype(BF16)
                dv_blk = dv_blk + _dot(dy_t[h][:, rows], w.astype(BF16))
                dk_blk = dk_blk + _dot(q_t[h][:, rows], dzb)
                new += [_put_rows(carry[3 * h], dq + _dot(dzb, k2), r0), _put_rows(carry[3 * h + 1], pa, r0),
                        _put_rows(carry[3 * h + 2], pg, r0)]
            dk_acc[j, :, c0:c0 + nk] += dk_blk
            dv_acc[j, :, c0:c0 + nk] += dv_blk
            return tuple(new)

        zero = jnp.zeros((tq, LANES), F32)
        zrun = jnp.zeros((tq, 1), F32)
        carry = lax.fori_loop(0, qi, lambda j, c: tile(j, c, 0, tq, 0, tq, False),
                              (zero, -toth[0], zrun, zero, -toth[1], zrun))
        for r0, nr, c0, nk in _diag_tiles(tq, False):
            carry = tile(qi, carry, r0, nr, c0, nk, True)
        dq_ref[0] = (jnp.where(in_h[0], carry[0], carry[3]) * (SB_HD ** -0.5)).astype(BF16)

        @pl.when(qi == nq - 1)
        def _():
            for jb in range(nq):
                dk_ref[0, jb * tq:(jb + 1) * tq, :] = (dk_acc[jb].T * (-1.0 / LOG2E)).astype(BF16)
                dv_ref[0, jb * tq:(jb + 1) * tq, :] = dv_acc[jb].T.astype(BF16)

    qspec = pl.BlockSpec((1, tq, LANES), lambda b, hp, i: (b, i, hp))
    kspec = pl.BlockSpec((1, S, LANES), lambda b, hp, i: (b, 0, hp))
    return pl.pallas_call(
        body, name="sb_bwd", grid=(B, W // LANES, nq),
        in_specs=[qspec, kspec, kspec, qspec, qspec, _AFTER],
        out_specs=[qspec, kspec, kspec],
        out_shape=[_sds((B, S, W), BF16)] * 3,
        scratch_shapes=[pltpu.VMEM((nq, LANES, tq), F32), pltpu.VMEM((nq, LANES, tq), F32)],
        compiler_params=_cparams(("parallel", "parallel", "arbitrary")),
    )(q, k, v, tot, dy, after)


def _same_or_earlier_chunk(row, col):
    return lax.shift_right_logical(col, 6) <= lax.shift_right_logical(row, 6)


def _mla_fwd(qp, kp, mv, dm, after):
    B, S, QW = qp.shape
    VW = mv.shape[2]
    tq = dm["tq"]
    nq = S // tq
    assert CHUNK == 64

    def body(q_ref, k_ref, v_ref, _, y_ref, lse_ref):
        qi = pl.program_id(2)
        q2 = q_ref[0]
        lane = lax.broadcasted_iota(jnp.int32, (tq, LANES), 1)

        def tile(j, carry, r0, nr, c0, nk, masked):
            off = pl.multiple_of(j * tq + c0, math.gcd(tq, c0))
            v2 = v_ref[0, pl.ds(off, nk), :]
            allowed = _tile_mask(nr, nk, r0, c0, _same_or_earlier_chunk) if masked else None
            rows = slice(r0, r0 + nr)
            heads = range(2)
            sl = [slice(h * HEAD_PAD, (h + 1) * HEAD_PAD) for h in heads]
            m_old = [carry[3 * h + 1][rows] for h in heads]
            s = [_dot_nt(q2[rows, sl[h]], k_ref[0, pl.ds(off, nk), sl[h]]) for h in heads]
            if masked:
                s = [jnp.where(allowed, s[h], -1e30) for h in heads]
            m_new = [jnp.maximum(m_old[h], jnp.max(s[h], axis=1, keepdims=True)) for h in heads]
            alpha = [jnp.exp2(m_old[h] - m_new[h]) for h in heads]
            p = [jnp.exp2(s[h] - m_new[h]) for h in heads]
            acc = [alpha[h] * carry[3 * h][rows] + _dot(p[h].astype(BF16), v2) for h in heads]
            l = [alpha[h] * carry[3 * h + 2][rows] + jnp.sum(p[h], axis=1, keepdims=True) for h in heads]
            out = []
            for h in heads:
                out += [_put_rows(carry[3 * h], acc[h], r0), _put_rows(carry[3 * h + 1], m_new[h], r0),
                        _put_rows(carry[3 * h + 2], l[h], r0)]
            return tuple(out)

        zero = jnp.zeros((tq, LANES), F32)
        m0 = jnp.full((tq, 1), -1e30, F32)
        l0 = jnp.zeros((tq, 1), F32)
        carry = (zero, m0, l0, zero, m0, l0)
        for r0, nr, c0, nk in _diag_tiles(tq, False):
            carry = tile(qi, carry, r0, nr, c0, nk, True)
        carry = lax.fori_loop(0, qi, lambda j, c: tile(j, c, 0, tq, 0, tq, False), carry)
        y0 = carry[0] / carry[2]
        y1 = carry[3] / carry[5]
        y_ref[0] = jnp.where(lane < MLA_V, y0, y1).astype(BF16)
        lse_ref[0] = jnp.where(lane < MLA_V, carry[1] + jnp.log2(carry[2]), carry[4] + jnp.log2(carry[5]))

    qspec = pl.BlockSpec((1, tq, 2 * HEAD_PAD), lambda b, hp, i: (b, i, hp))
    kspec = pl.BlockSpec((1, S, 2 * HEAD_PAD), lambda b, hp, i: (b, 0, hp))
    vspec = pl.BlockSpec((1, S, LANES), lambda b, hp, i: (b, 0, hp))
    yspec = pl.BlockSpec((1, tq, LANES), lambda b, hp, i: (b, i, hp))
    return pl.pallas_call(
        body, name="mla_fwd", grid=(B, VW // LANES, nq),
        in_specs=[qspec, kspec, vspec, _AFTER],
        out_specs=[yspec, yspec],
        out_shape=[_sds((B, S, VW), BF16), _sds((B, S, VW), F32)],
        compiler_params=_cparams(("parallel", "parallel", "arbitrary")),
    )(qp, kp, mv, after)


def _mla_bwd(qp, kp, mv, y, lse, dy, dm, after):
    B, S, QW = qp.shape
    VW = mv.shape[2]
    tq = dm["tq"]
    nq = S // tq
    scale = (MLA_NOPE + MLA_ROPE) ** -0.5

    def body(q_ref, k_ref, v_ref, y_ref, lse_ref, dy_ref, _, dq_ref, dk_ref, dv_ref, dk_acc, dv_acc):
        qi = pl.program_id(2)

        @pl.when(qi == 0)
        def _():
            dk_acc[...] = jnp.zeros_like(dk_acc)
            dv_acc[...] = jnp.zeros_like(dv_acc)

        q2 = q_ref[0]
        dy2 = dy_ref[0]
        lse2 = lse_ref[0]
        lane = lax.broadcasted_iota(jnp.int32, (tq, LANES), 1)
        in_h = [lane < MLA_V, lane >= MLA_V]
        prod = dy2.astype(F32) * y_ref[0].astype(F32)
        delta = [jnp.sum(jnp.where(m, prod, 0.0), axis=1, keepdims=True) for m in in_h]
        dyh = [jnp.where(m, dy2, 0).astype(BF16) for m in in_h]
        lseh = [lse2[:, 0:1], lse2[:, MLA_V:MLA_V + 1]]
        q_t = _transpose_bf16(q2)
        dy_t = [_transpose_bf16(a) for a in dyh]

        def tile(j, carry, r0, nr, c0, nk, masked):
            off = pl.multiple_of(j * tq + c0, math.gcd(tq, c0))
            v2 = v_ref[0, pl.ds(off, nk), :]
            allowed = _tile_mask(nr, nk, r0, c0, _same_or_earlier_chunk) if masked else None
            rows = slice(r0, r0 + nr)
            keys = slice(c0, c0 + nk)
            heads = range(2)
            sl = [slice(h * HEAD_PAD, (h + 1) * HEAD_PAD) for h in heads]
            qhh = [q2[rows, sl[h]] for h in heads]
            dyr = [dyh[h][rows] for h in heads]
            kh = [k_ref[0, pl.ds(off, nk), sl[h]] for h in heads]
            s = [_dot_nt(qhh[h], kh[h]) for h in heads]
            dp = [_dot_nt(dyr[h], v2) for h in heads]
            if masked:
                s = [jnp.where(allowed, s[h], -1e30) for h in heads]
            p = [jnp.exp2(s[h] - lseh[h][rows]) for h in heads]
            dv_acc[j, :, keys] += (_dot(dy_t[0][:, rows], p[0].astype(BF16))
                                   + _dot(dy_t[1][:, rows], p[1].astype(BF16)))
            ds = [(p[h] * (dp[h] - delta[h][rows])).astype(BF16) for h in heads]
            for h in heads:
                dk_acc[j, sl[h], keys] += _dot(q_t[sl[h], rows], ds[h])
            return tuple(_put_rows(carry[h], carry[h][rows] + _dot(ds[h], kh[h]), r0) for h in heads)

        zero = jnp.zeros((tq, HEAD_PAD), F32)
        carry = lax.fori_loop(0, qi, lambda j, c: tile(j, c, 0, tq, 0, tq, False), (zero, zero))
        for r0, nr, c0, nk in _diag_tiles(tq, True):
            carry = tile(qi, carry, r0, nr, c0, nk, True)
        dq_ref[0] = (jnp.concatenate([carry[0], carry[1]], axis=1) * scale).astype(BF16)

        @pl.when(qi == nq - 1)
        def _():
            for jb in range(nq):
                dk_ref[0, jb * tq:(jb + 1) * tq, :] = (dk_acc[jb].T * (1.0 / LOG2E)).astype(BF16)
                dv_ref[0, jb * tq:(jb + 1) * tq, :] = dv_acc[jb].T.astype(BF16)

    qspec = pl.BlockSpec((1, tq, 2 * HEAD_PAD), lambda b, hp, i: (b, i, hp))
    kspec = pl.BlockSpec((1, S, 2 * HEAD_PAD), lambda b, hp, i: (b, 0, hp))
    vspec = pl.BlockSpec((1, S, LANES), lambda b, hp, i: (b, 0, hp))
    yspec = pl.BlockSpec((1, tq, LANES), lambda b, hp, i: (b, i, hp))
    return pl.pallas_call(
        body, name="mla_bwd", grid=(B, VW // LANES, nq),
        in_specs=[qspec, kspec, vspec, yspec, yspec, yspec, _AFTER],
        out_specs=[qspec, kspec, vspec],
        out_shape=[_sds((B, S, QW), BF16), _sds((B, S, QW), BF16), _sds((B, S, VW), BF16)],
        scratch_shapes=[pltpu.VMEM((nq, 2 * HEAD_PAD, tq), F32), pltpu.VMEM((nq, LANES, tq), F32)],
        compiler_params=_cparams(("parallel", "parallel", "arbitrary")),
    )(qp, kp, mv, y, lse, dy, after)


def _outproj_fwd(sb_y, mla_y, x0, mod, w_o, ln_g, ln_b, dm):
    B, S, D = x0.shape
    tm = dm["tm"]
    sbw = sb_y.shape[2]

    def body(ya_ref, yb_ref, x0_ref, mod_ref, wo_ref, g_ref, b_ref, mix_ref, x1_ref, h2_ref):
        mod = mod_ref[0]
        mix = _dot(ya_ref[0], wo_ref[0:sbw, :]) + _dot(yb_ref[0], wo_ref[sbw:, :])
        mix_ref[0] = mix
        x1, _, _ = _ln_fwd(ALPHA * x0_ref[0] + (1.0 + mod[2:3]) * mix, g_ref[...], b_ref[...])
        x1_ref[0] = x1
        h2_ref[0] = (x1 * (1.0 + mod[4:5]) + mod[3:4]).astype(BF16)

    return pl.pallas_call(
        body, name="outproj_fwd", grid=(B, S // tm),
        in_specs=[_tok(tm, sbw), _tok(tm, mla_y.shape[2]), _tok(tm, D), _perb(N_MOD, D),
                  _full(w_o), _full(ln_g), _full(ln_b)],
        out_specs=[_tok(tm, D)] * 3,
        out_shape=[_sds((B, S, D), F32), _sds((B, S, D), F32), _sds((B, S, D), BF16)],
        compiler_params=_cparams(("parallel", "parallel")),
    )(sb_y, mla_y, x0, mod, w_o, ln_g, ln_b)


def _stat_specs(B, D):
    specs = [pl.BlockSpec((1, 8, D), lambda b, s: (b, 0, 0)), pl.BlockSpec((8, D), lambda b, s: (0, 0))]
    shapes = [_sds((B, 8, D), F32), _sds((8, D), F32)]
    return specs, shapes


def _stat_init(bst_ref, wst_ref):
    @pl.when(pl.program_id(1) == 0)
    def _():
        bst_ref[...] = jnp.zeros_like(bst_ref)

    @pl.when((pl.program_id(0) == 0) & (pl.program_id(1) == 0))
    def _():
        wst_ref[...] = jnp.zeros_like(wst_ref)


def _mlp_fwd(h2, x1, mod, target, w_up, w_down, ln_g, ln_b, dm):
    B, S, D = x1.shape
    tm = dm["tm"]
    nck, _, ck = w_up.shape
    dff = nck * ck

    def body(h2_ref, x1_ref, mod_ref, t_ref, wu_ref, wd_ref, g_ref, b_ref, u_ref, dr_ref, bst_ref, wst_ref):
        _stat_init(bst_ref, wst_ref)
        mod = mod_ref[0]
        g = g_ref[...]
        h2 = h2_ref[0]
        ff = jnp.zeros((tm, D), F32)
        for c in range(nck):
            u = _dot(h2, wu_ref[c])
            u_ref[0, :, c * ck:(c + 1) * ck] = u.astype(BF16)
            act = jnp.square(jnp.maximum(u, 0.0)).astype(BF16)
            ff = ff + _dot(act, wd_ref[c])
        x2, xhat, rstd = _ln_fwd(ALPHA * x1_ref[0] + (1.0 + mod[5:6]) * ff, g, b_ref[...])
        err = x2 - t_ref[0]
        dy = err * (1.0 / D)
        dr = _ln_bwd(dy, xhat, rstd, g)
        dr_ref[0] = dr
        bst_ref[0, 0:1, :] += _colsum(dr * ff)
        wst_ref[0:1, :] += _colsum(dy * xhat)
        wst_ref[1:2, :] += _colsum(dy)
        wst_ref[2:3, :] += _colsum(err * err) * (0.5 / D)

    sspecs, sshapes = _stat_specs(B, D)
    return pl.pallas_call(
        body, name="mlp_fwd", grid=(B, S // tm),
        in_specs=[_tok(tm, D), _tok(tm, D), _perb(N_MOD, D), _tok(tm, D), _full(w_up), _full(w_down),
                  _full(ln_g), _full(ln_b)],
        out_specs=[_tok(tm, dff), _tok(tm, D)] + sspecs,
        out_shape=[_sds((B, S, dff), BF16), _sds((B, S, D), F32)] + sshapes,
        compiler_params=_cparams(("arbitrary", "arbitrary")),
    )(h2, x1, mod, target, w_up, w_down, ln_g, ln_b)


def _mlp_bwd(dr2, u, x1, x0, mix, mod, w_up, w_down, w_o, ln_g, dm):
    B, S, D = x1.shape
    tm = dm["tm_small"]
    sbw = dm["sbw"]
    nck, _, ck = w_up.shape
    dff = nck * ck

    def body(dr_ref, u_ref, x1_ref, x0_ref, mix_ref, mod_ref, wu_ref, wd_ref, wo_ref, g_ref,
             du_ref, dff_ref, dmix_ref, dx0_ref, dya_ref, dyb_ref, bst_ref, wst_ref):
        _stat_init(bst_ref, wst_ref)
        mod = mod_ref[0]
        dr2 = dr_ref[0]
        dffv = ((1.0 + mod[5:6]) * dr2).astype(BF16)
        dff_ref[0] = dffv
        dh2 = jnp.zeros((tm, D), F32)
        for c in range(nck):
            sl = slice(c * ck, (c + 1) * ck)
            da = _dot_nt(dffv, wd_ref[c])
            du = (da * (2.0 * jnp.maximum(u_ref[0, :, sl].astype(F32), 0.0))).astype(BF16)
            du_ref[0, :, sl] = du
            dh2 = dh2 + _dot_nt(du, wu_ref[c])
        x1 = x1_ref[0]
        dx1 = ALPHA * dr2 + dh2 * (1.0 + mod[4:5])
        bst_ref[0, 0:1, :] += _colsum(dh2 * x1)
        bst_ref[0, 1:2, :] += _colsum(dh2)
        mix = mix_ref[0]
        g = g_ref[...]
        _, xhat, rstd = _ln_fwd(ALPHA * x0_ref[0] + (1.0 + mod[2:3]) * mix, g, 0.0)
        dr1 = _ln_bwd(dx1, xhat, rstd, g)
        wst_ref[0:1, :] += _colsum(dx1 * xhat)
        wst_ref[1:2, :] += _colsum(dx1)
        bst_ref[0, 2:3, :] += _colsum(dr1 * mix)
        dx0_ref[0] = ALPHA * dr1
        dmix = ((1.0 + mod[2:3]) * dr1).astype(BF16)
        dmix_ref[0] = dmix
        dya_ref[0] = _dot_nt(dmix, wo_ref[0:sbw, :]).astype(BF16)
        dyb_ref[0] = _dot_nt(dmix, wo_ref[sbw:, :]).astype(BF16)

    sspecs, sshapes = _stat_specs(B, D)
    wa, wb = sbw, w_o.shape[0] - sbw
    return pl.pallas_call(
        body, name="mlp_bwd", grid=(B, S // tm),
        in_specs=[_tok(tm, D), _tok(tm, dff), _tok(tm, D), _tok(tm, D), _tok(tm, D), _perb(N_MOD, D),
                  _full(w_up), _full(w_down), _full(w_o), _full(ln_g)],
        out_specs=[_tok(tm, dff), _tok(tm, D), _tok(tm, D), _tok(tm, D), _tok(tm, wa), _tok(tm, wb)] + sspecs,
        out_shape=[_sds((B, S, dff), BF16), _sds((B, S, D), BF16), _sds((B, S, D), BF16), _sds((B, S, D), F32),
                   _sds((B, S, wa), BF16), _sds((B, S, wb), BF16)] + sshapes,
        compiler_params=_cparams(("arbitrary", "arbitrary")),
    )(dr2, u, x1, x0, mix, mod, w_up, w_down, w_o, ln_g)


def _inproj_bwd(x, x0, dx0a, mod, ln_g, dq, dk, dv, dqp, dkp, dmv, cq, ckv, qn, kvn, w_in_p, w_uq_p, w_kv, gq, gkv,
                tc, ts1, ts2, dm):
    B, S, D = x.shape
    tm = dm["tm"]
    sbw, qr, kvr, nh = dm["sbw"], dm["qr"], dm["kvr"], dm["nh"]
    qpw = nh * HEAD_PAD
    dinp = w_in_p.shape[1]
    kvw = w_kv.shape[1]

    def body(x_ref, x0_ref, dx0a_ref, mod_ref, g_ref, dq_ref, dk_ref, dv_ref, dqp_ref, dkp_ref, dmv_ref,
             cq_ref, ckv_ref, qn_ref, kvn_ref, win_ref, wuq_ref, wkv_ref, gq_ref, gkv_ref, tc_ref, ts1_ref, ts2_ref,
             gx_ref, gin_ref, guq_ref, gwkv_ref, bst_ref, wst_ref):
        _stat_init(bst_ref, wst_ref)

        @pl.when((pl.program_id(0) == 0) & (pl.program_id(1) == 0))
        def _():
            gin_ref[...] = jnp.zeros_like(gin_ref)
            guq_ref[...] = jnp.zeros_like(guq_ref)
            gwkv_ref[...] = jnp.zeros_like(gwkv_ref)

        mod = mod_ref[0]
        c1, s1, s2 = tc_ref[...], ts1_ref[...], ts2_ref[...]
        c8, s18, s28 = jnp.tile(c1, (1, nh)), jnp.tile(s1, (1, nh)), jnp.tile(s2, (1, nh))
        dqpre = _rope_t(dqp_ref[0].astype(F32), c8, s18, s28).astype(BF16)
        guq_ref[...] += _dot_tn(qn_ref[0], dqpre)
        gq = gq_ref[...]
        cq = cq_ref[0]
        rq = lax.rsqrt(jnp.mean(cq * cq, axis=-1, keepdims=True) + RMS_EPS)
        dqn = _dot_nt(dqpre, wuq_ref[...])
        wst_ref[4:5, 0:qr] += _colsum(dqn * cq * rq)
        dqg = dqn * gq
        dcq = rq * dqg - cq * (rq * rq * rq) * jnp.mean(dqg * cq, axis=-1, keepdims=True)

        dkpre = _rope_t(dkp_ref[0].astype(F32), c8, s18, s28)
        dkr = dkpre[:, 0:HEAD_PAD]
        for h in range(1, nh):
            dkr = dkr + dkpre[:, h * HEAD_PAD:(h + 1) * HEAD_PAD]
        lane = lax.broadcasted_iota(jnp.int32, (tm, LANES), 1)
        dkr = jnp.where((lane >= MLA_NOPE) & (lane < MLA_NOPE + MLA_ROPE), dkr, 0.0)
        dkr = pltpu.roll(dkr, LANES - MLA_NOPE, 1)
        dkvo = jnp.concatenate([dkpre.astype(BF16), dmv_ref[0]], axis=1)
        gwkv_ref[...] += _dot_tn(kvn_ref[0], dkvo)
        gkv = gkv_ref[...]
        ckv = ckv_ref[0]
        rkv = lax.rsqrt(jnp.mean(ckv * ckv, axis=-1, keepdims=True) + RMS_EPS)
        dkvn = _dot_nt(dkvo, wkv_ref[...])
        wst_ref[5:6, 0:kvr] += _colsum(dkvn * ckv * rkv)
        dkg = dkvn * gkv
        dckv = rkv * dkg - ckv * (rkv * rkv * rkv) * jnp.mean(dkg * ckv, axis=-1, keepdims=True)

        dproj = jnp.concatenate([dq_ref[0], dk_ref[0], dv_ref[0], dcq.astype(BF16), dckv.astype(BF16),
                                 dkr.astype(BF16)], axis=1)
        dh = _dot_nt(dproj, win_ref[...])
        x0 = x0_ref[0]
        gin_ref[...] += _dot_tn((x0 * (1.0 + mod[1:2]) + mod[0:1]).astype(BF16), dproj)
        dx0 = dx0a_ref[0] + dh * (1.0 + mod[1:2])
        bst_ref[0, 0:1, :] += _colsum(dh * x0)
        bst_ref[0, 1:2, :] += _colsum(dh)
        g = g_ref[...]
        _, xhat, rstd = _ln_fwd(x_ref[0], g, 0.0)
        gx_ref[0] = _ln_bwd(dx0, xhat, rstd, g)
        wst_ref[0:1, :] += _colsum(dx0 * xhat)
        wst_ref[1:2, :] += _colsum(dx0)

    tab = pl.BlockSpec((tm, LANES), lambda b, s: (s, 0))
    sspecs, sshapes = _stat_specs(B, D)
    return pl.pallas_call(
        body, name="inproj_bwd", grid=(B, S // tm),
        in_specs=[_tok(tm, D), _tok(tm, D), _tok(tm, D), _perb(N_MOD, D), _full(ln_g),
                  _tok(tm, sbw), _tok(tm, sbw), _tok(tm, sbw), _tok(tm, qpw), _tok(tm, qpw), _tok(tm, nh * MLA_V),
                  _tok(tm, qr), _tok(tm, kvr), _tok(tm, qr), _tok(tm, kvr),
                  _full(w_in_p), _full(w_uq_p), _full(w_kv), _full(gq), _full(gkv), tab, tab, tab],
        out_specs=[_tok(tm, D), pl.BlockSpec((D, dinp), lambda b, s: (0, 0)),
                   pl.BlockSpec((qr, qpw), lambda b, s: (0, 0)), pl.BlockSpec((kvr, kvw), lambda b, s: (0, 0))] + sspecs,
        out_shape=[_sds((B, S, D), F32), _sds((D, dinp), F32), _sds((qr, qpw), F32),
                   _sds((kvr, kvw), F32)] + sshapes,
        compiler_params=_cparams(("arbitrary", "arbitrary")),
    )(x, x0, dx0a, mod, ln_g, dq, dk, dv, dqp, dkp, dmv, cq, ckv, qn, kvn, w_in_p, w_uq_p, w_kv, gq, gkv,
      tc, ts1, ts2)


def _tile_of(n, cap):
    if n <= cap:
        return n
    best = n
    for t in range(LANES, cap + 1, LANES):
        if n % t == 0:
            best = t
    return best


def _mm_tn(a, g, name, after, relu_sq=False, out_dtype=F32, col_blocks=None):
    T, K = a.shape
    N = g.shape[1]
    tt = 1024 if T % 1024 == 0 else (512 if T % 512 == 0 else T)
    tk = _tile_of(K, 1024)
    tn = _tile_of(N, 1280)
    nt = T // tt
    bw = N // col_blocks if col_blocks else tn
    assert tn % bw == 0

    def body(a_ref, g_ref, _, o_ref, acc_ref):
        @pl.when(pl.program_id(2) == 0)
        def _():
            acc_ref[...] = jnp.zeros_like(acc_ref)

        av = a_ref[...]
        if relu_sq:
            av = jnp.square(jnp.maximum(av.astype(F32), 0.0)).astype(BF16)
        acc_ref[...] += _dot_tn(av, g_ref[...])

        @pl.when(pl.program_id(2) == nt - 1)
        def _():
            if col_blocks:
                for c in range(tn // bw):
                    o_ref[c] = acc_ref[:, c * bw:(c + 1) * bw].astype(out_dtype)
            else:
                o_ref[...] = acc_ref[...].astype(out_dtype)

    if col_blocks:
        out_spec = pl.BlockSpec((tn // bw, tk, bw), lambda i, j, t: (j, i, 0))
        out_shape = _sds((col_blocks, K, bw), out_dtype)
    else:
        out_spec = pl.BlockSpec((tk, tn), lambda i, j, t: (i, j))
        out_shape = _sds((K, N), out_dtype)
    return pl.pallas_call(
        body, name=name, grid=(K // tk, N // tn, nt),
        in_specs=[pl.BlockSpec((tt, tk), lambda i, j, t: (t, i)), pl.BlockSpec((tt, tn), lambda i, j, t: (t, j)),
                  _AFTER],
        out_specs=out_spec, out_shape=out_shape,
        scratch_shapes=[pltpu.VMEM((tk, tn), F32)],
        compiler_params=_cparams(("parallel", "parallel", "arbitrary")),
    )(a, g, after)


def _reduce_adamw(parts, w, m, v, name):
    P, K, N = parts.shape
    tr = 256 if K % 256 == 0 else K

    def body(p_ref, w_ref, m_ref, v_ref, g_ref, d_ref, nm_ref, nv_ref):
        g = p_ref[0].astype(F32)
        for k in range(1, P):
            g = g + p_ref[k].astype(F32)
        g_ref[0] = g
        d_ref[0], nm_ref[0], nv_ref[0] = _adamw(w_ref[0], g, m_ref[0], v_ref[0])

    spec = pl.BlockSpec((1, tr, N), lambda r: (0, r, 0))
    return pl.pallas_call(
        body, name=name, grid=(K // tr,),
        in_specs=[pl.BlockSpec((P, tr, N), lambda r: (0, r, 0)), spec, spec, spec],
        out_specs=[spec] * 4, out_shape=[_sds((1, K, N), F32)] * 4,
        compiler_params=_cparams(("parallel",)),
    )(parts, w, m, v)


def _finish(sm, dmod_all, dmod_my, cact_all, p_small, m_small, v_small, b_ada, m_b, v_b, w_ada, m_w, v_w):
    n0 = p_small.shape[1]
    n1 = sm.shape[1]
    d = cact_all.shape[1]

    def body(sm_ref, dma_ref, dmm_ref, ca_ref, p_ref, pm_ref, pv_ref, b_ref, bm_ref, bv_ref, w_ref, wm_ref, wv_ref,
             gs_ref, ds_ref, ms_ref, vs_ref, gb_ref, db_ref, mb_ref, vb_ref, gw_ref, dw_ref, mw_ref, vw_ref,
             loss_ref):
        gs = sm_ref[0:1, :]
        for k in range(1, N_DEV):
            gs = gs + sm_ref[k:k + 1, :]
        gs_ref[...] = gs
        ds_ref[...], ms_ref[...], vs_ref[...] = _adamw(p_ref[...], gs[:, 0:n0], pm_ref[...], pv_ref[...])
        loss_ref[...] = jnp.zeros((1, LANES), F32) + jnp.sum(gs[:, n1 - d:n1])
        gb = jnp.sum(dma_ref[...], axis=0, keepdims=True)
        gb_ref[...] = gb
        db_ref[...], mb_ref[...], vb_ref[...] = _adamw(b_ref[...], gb, bm_ref[...], bv_ref[...])
        gw = _dot_tn(ca_ref[...].astype(BF16), dmm_ref[...].astype(BF16))
        gw_ref[...] = gw
        dw_ref[...], mw_ref[...], vw_ref[...] = _adamw(w_ref[...], gw, wm_ref[...], wv_ref[...])

    s0 = _sds(p_small.shape, F32)
    sb = _sds(b_ada.shape, F32)
    sw = _sds(w_ada.shape, F32)
    return pl.pallas_call(
        body, name="finish_small",
        out_shape=[_sds((1, n1), F32), s0, s0, s0, sb, sb, sb, sb, sw, sw, sw, sw,
                   _sds((1, LANES), F32)],
        compiler_params=pltpu.CompilerParams(vmem_limit_bytes=VMEM_LIMIT),
    )(sm, dmod_all, dmod_my, cact_all, p_small, m_small, v_small, b_ada, m_b, v_b, w_ada, m_w, v_w)


def _pack(arrs, dtype, width):
    flat = jnp.concatenate([a.astype(dtype).reshape(-1) for a in arrs])
    rows = -(-flat.shape[0] // (256 * width)) * 256
    return jnp.pad(flat, (0, rows * width - flat.shape[0])).reshape(rows, width)


def _unpack(slab, shapes):
    flat = slab.reshape(-1)
    out, o = [], 0
    for s in shapes:
        n = math.prod(s)
        out.append(flat[o:o + n].reshape(s))
        o += n
    return out


def _rope_tables(S):
    inv_freq = 1.0 / (ROPE_BASE ** (jnp.arange(0, MLA_ROPE, 2, dtype=F32) / MLA_ROPE))
    ang = jnp.arange(S, dtype=F32)[:, None] * inv_freq[None, :]
    cos, sin = jnp.cos(ang), jnp.sin(ang)
    one = jnp.ones((S, MLA_NOPE), F32)
    z16 = jnp.zeros((S, 16), F32)
    z32 = jnp.zeros((S, 32), F32)
    z64 = jnp.zeros((S, MLA_NOPE), F32)
    tc = jnp.concatenate([one, cos, cos, jnp.ones((S, 32), F32)], axis=1)
    ts1 = jnp.concatenate([z64, -sin, z16, z32], axis=1)
    ts2 = jnp.concatenate([z64, z16, sin, z32], axis=1)
    return tc, ts1, ts2


def kernel(x, c, ln_in_g, ln_in_b, w_ada, b_ada, w_in, q_norm_g, kv_norm_g, w_uq, w_ukv, w_o, ln1_g, ln1_b, w_up, w_down, ln2_g, ln2_b, loss_target, m_ln_in_g, m_ln_in_b, m_w_ada, m_b_ada, m_w_in, m_q_norm_g, m_kv_norm_g, m_w_uq, m_w_ukv, m_w_o, m_ln1_g, m_ln1_b, m_w_up, m_w_down, m_ln2_g, m_ln2_b, v_ln_in_g, v_ln_in_b, v_w_ada, v_b_ada, v_w_in, v_q_norm_g, v_kv_norm_g, v_w_uq, v_w_ukv, v_w_o, v_ln1_g, v_ln1_b, v_w_up, v_w_down, v_ln2_g, v_ln2_b):
    B, S, D = x.shape
    sbw = D // 2
    mlw = D - sbw
    nh = mlw // MLA_V
    qr = w_uq.shape[1]
    kvr = w_ukv.shape[1]
    qk = MLA_NOPE + MLA_ROPE
    dff = w_up.shape[2] * N_DEV
    din = w_in.shape[2] * N_DEV
    tm = 512 if S % 512 == 0 else S
    tq = min(512, S // 2)
    dm = dict(tm=tm, tm_small=min(tm, 256), tq=tq, sbw=sbw, qr=qr, kvr=kvr, nh=nh)
    dev =4 * lax.axis_index("x") + 2 * lax.axis_index("y") + lax.axis_index("c")

    big = [w_in, w_uq, w_ukv, w_o, w_up, w_down]
    first_w, first_token = _chip_exchange_start([a[0].astype(BF16) for a in big[:3]], "gather_w_first_start",
                                                scatter=False, after=c)

    nada = w_ada.shape[2]
    c_all = _all_gather([c + first_token[0, 0]], "gather_c")[0].reshape(N_DEV * B, D)
    b_loc = lax.dynamic_slice(b_ada, (0, dev * nada), (1, nada))
    cact_all, mod_part = _ada_partial(c_all, w_ada[0], b_loc)
    mod_all = _all_gather([mod_part], "gather_mod")[0]
    mod = lax.dynamic_slice(mod_all, (0, dev * B, 0), (N_DEV, B, nada))
    mod = jnp.swapaxes(mod, 0, 1).reshape(B, N_MOD, D)

    first_by_chip = _chip_exchange_wait(first_w, mod_all, "gather_w_first_wait")
    w_in8, w_uq8, w_ukv8 = [b.reshape((N_DEV,) + b.shape[2:]) for b in _core_gather(first_by_chip, "gather_w_first_cores")]
    late_w, late_token = _chip_exchange_start([a[0].astype(BF16) for a in big[3:]], "gather_w_late_start",
                                              scatter=False, after=w_in8, everyone=True)
    cols = lambda a8: jnp.swapaxes(a8, 0, 1).reshape(a8.shape[1], N_DEV * a8.shape[2])
    w_in_p = jnp.pad(cols(w_in8), ((0, 0), (0, LANES - MLA_ROPE)))
    zpad = jnp.zeros((qr, nh, HEAD_PAD - qk), BF16)
    w_uq_p = jnp.concatenate([cols(w_uq8).reshape(qr, nh, qk), zpad], axis=2).reshape(qr, nh * HEAD_PAD)
    w_ukv_f = cols(w_ukv8)
    w_uk = w_ukv_f[:, :nh * MLA_NOPE].reshape(kvr, nh, MLA_NOPE)
    w_uk_p = jnp.concatenate([w_uk, jnp.zeros((kvr, nh, HEAD_PAD - MLA_NOPE), BF16)], axis=2)
    w_kv = jnp.concatenate([w_uk_p.reshape(kvr, nh * HEAD_PAD), w_ukv_f[:, nh * MLA_NOPE:]], axis=1)

    tc, ts1, ts2 = _rope_tables(S)
    g_in, b_in = ln_in_g.reshape(1, D), ln_in_b.reshape(1, D)
    (x0, sq, sk, sv, qp, kp, mv, cq, ckv, qn, kvn) = _inproj_fwd(
        x, mod, g_in, b_in, w_in_p, w_uq_p, w_kv, q_norm_g, kv_norm_g, tc, ts1, ts2, dm, late_token)
    sb_y, sb_tot = _sb_fwd(sq, sk, sv, dm)
    mla_y, mla_lse = _mla_fwd(qp, kp, mv, dm, sb_tot)
    w_o8, w_up8, w_down8 = _chip_exchange_wait(late_w, mla_lse, "gather_w_late_wait")
    w_o_f = w_o8.reshape(D, D)
    mix, x1, h2 = _outproj_fwd(sb_y, mla_y, x0, mod, w_o_f, ln1_g, ln1_b, dm)
    u, dr2, bst_c, wst_c = _mlp_fwd(h2, x1, mod, loss_target, w_up8, w_down8, ln2_g, ln2_b, dm)

    du, dffb, dmixb, dx0a, dsb_y, dmla_y, bst_b, wst_b = _mlp_bwd(
        dr2, u, x1, x0, mix, mod, w_up8, w_down8, w_o_f, ln1_g, dm)
    T = B * S
    r2 = lambda a: a.reshape(T, a.shape[2])
    by_core = lambda a: a.reshape((4, 2) + a.shape[1:])
    g_o = jnp.concatenate([_mm_tn(r2(sb_y), r2(dmixb), "grad_w_o_sb", dr2, out_dtype=BF16),
                           _mm_tn(r2(mla_y), r2(dmixb), "grad_w_o_mla", dr2, out_dtype=BF16)], axis=0)
    g_up8 = _mm_tn(r2(h2), r2(du), "grad_w_up", dr2, out_dtype=BF16, col_blocks=N_DEV)
    g_down = _mm_tn(r2(u), r2(dffb), "grad_w_down", dr2, relu_sq=True, out_dtype=BF16)
    early = [g_o.reshape(N_DEV, D // N_DEV, D), g_up8, g_down.reshape(N_DEV, dff // N_DEV, D)]
    early_g, early_token = _chip_exchange_start(early, "scatter_g_early_start", scatter=True, after=dr2,
                                                everyone=True)

    dsq, dsk, dsv = _sb_bwd(sq, sk, sv, sb_tot, dsb_y, dm, early_token)
    dqp, dkp, dmv = _mla_bwd(qp, kp, mv, mla_y, mla_lse, dmla_y, dm, dsq)
    grad_x, g_in_p, g_uq_p, g_kv, bst_a, wst_a = _inproj_bwd(
        x, x0, dx0a, mod, g_in, dsq, dsk, dsv, dqp, dkp, dmv, cq, ckv, qn, kvn, w_in_p, w_uq_p, w_kv,
        q_norm_g, kv_norm_g, tc, ts1, ts2, dm)

    dmod = jnp.concatenate([bst_a[:, 1], bst_a[:, 0], bst_b[:, 2], bst_b[:, 1], bst_b[:, 0], bst_c[:, 0]], axis=1)
    small = jnp.concatenate([wst_a[0], wst_a[1], wst_a[4, :qr], wst_a[5, :kvr], wst_b[0], wst_b[1],
                             wst_c[0], wst_c[1], wst_c[2]])
    n1 = small.shape[0]
    small_g, small_token = _chip_exchange_start([_pack([dmod, small], F32, LANES)], "gather_small_start",
                                                scatter=False, after=grad_x)
    g_uq_f = g_uq_p.reshape(qr, nh, HEAD_PAD)[:, :, :qk].reshape(qr, nh * qk)
    g_uk = g_kv[:, :nh * HEAD_PAD].reshape(kvr, nh, HEAD_PAD)[:, :, :MLA_NOPE].reshape(kvr, nh * MLA_NOPE)
    g_ukv_f = jnp.concatenate([g_uk, g_kv[:, nh * HEAD_PAD:]], axis=1)
    early_quarter = _chip_exchange_wait(early_g, g_in_p, "scatter_g_early_wait")

    def by_dest_cols(a):
        k, n = a.shape[0], a.shape[1] // N_DEV
        return jnp.swapaxes(a.reshape(k, N_DEV, n), 0, 1).astype(BF16)

    last = [by_dest_cols(g_in_p[:, :din] + small_token[0, 0]), by_dest_cols(g_uq_f), by_dest_cols(g_ukv_f)]
    last_sum = _core_scatter_sum([by_core(a) for a in last], "scatter_g_last_cores")
    small_by_chip = _chip_exchange_wait(small_g, last_sum[0], "gather_small_wait")
    both = _core_gather(small_by_chip, "gather_small_cores")[0].reshape(N_DEV, -1)
    last_g, last_token = _chip_exchange_start(last_sum, "scatter_g_last_start", scatter=True, after=grad_x)
    names = ["w_in", "w_uq", "w_ukv", "w_o", "w_up", "w_down"]
    moms = [m_w_in, m_w_uq, m_w_ukv, m_w_o, m_w_up, m_w_down]
    vars_ = [v_w_in, v_w_uq, v_w_ukv, v_w_o, v_w_up, v_w_down]
    res_early = [_reduce_adamw(p, w, m, v, "adamw_" + n)
                 for p, w, m, v, n in zip(early_quarter, big[3:], moms[3:], vars_[3:], names[3:])]

    dmod_all = both[:, :B * N_MOD * D].reshape(N_DEV * B, N_MOD * D)
    sm = both[:, B * N_MOD * D:B * N_MOD * D + n1] + last_token[0, 0]
    dmod_my = lax.dynamic_slice(dmod_all, (0, dev * nada), (N_DEV * B, nada))
    row = lambda arrs: jnp.concatenate([a.reshape(1, -1) for a in arrs], axis=1)
    smalls = [ln_in_g, ln_in_b, q_norm_g, kv_norm_g, ln1_g, ln1_b, ln2_g, ln2_b]
    small_shapes = [a.shape for a in smalls]
    (gs, ds, nms, nvs, g_b, d_b, nm_b, nv_b, g_w, d_w, nm_w, nv_w, loss_v) = _finish(
        sm, dmod_all, dmod_my, cact_all, row(smalls),
        row([m_ln_in_g, m_ln_in_b, m_q_norm_g, m_kv_norm_g, m_ln1_g, m_ln1_b, m_ln2_g, m_ln2_b]),
        row([v_ln_in_g, v_ln_in_b, v_q_norm_g, v_kv_norm_g, v_ln1_g, v_ln1_b, v_ln2_g, v_ln2_b]),
        b_ada, m_b_ada, v_b_ada, w_ada[0], m_w_ada[0], v_w_ada[0])
    gsm, dsm, nmsm, nvsm = (_unpack(s, small_shapes) for s in (gs, ds, nms, nvs))
    last_quarter = _chip_exchange_wait(last_g, loss_v, "scatter_g_last_wait")
    res_last = [_reduce_adamw(p, w, m, v, "adamw_" + n)
                for p, w, m, v, n in zip(last_quarter, big[:3], moms[:3], vars_[:3], names[:3])]
    gb, db, nmb, nvb = ([r[i] for r in res_last + res_early] for i in range(4))

    def ordered(sm_l, w_l, ada_w, ada_b):
        return [sm_l[0], sm_l[1], ada_w[None], ada_b, w_l[0], sm_l[2], sm_l[3], w_l[1], w_l[2], w_l[3],
                sm_l[4], sm_l[5], w_l[4], w_l[5], sm_l[6], sm_l[7]]

    loss = loss_v[0, 0]
    return (loss, grad_x, *ordered(gsm, gb, g_w, g_b), *ordered(dsm, db, d_w, d_b),
            *ordered(nmsm, nmb, nm_w, nm_b), *ordered(nvsm, nvb, nv_w, nv_b))
```

```python
import math

import jax
import jax.numpy as jnp
from jax import lax
from jax.experimental import pallas as pl
from jax.experimental.pallas import tpu as pltpu

F32 = jnp.float32
BF16 = jnp.bfloat16

SB_HD = 64
MLA_V = 64
MLA_NOPE = 64
MLA_ROPE = 32
HEAD_PAD = 128
CHUNK = 64
ROPE_BASE = 10000.0
LN_EPS = 1e-5
RMS_EPS = 1e-6
DEPTH = 1
ALPHA = (2.0 * DEPTH) ** 0.25
N_MOD = 6
ADAM_LR = 0.001
ADAM_B1 = 0.9
ADAM_B2 = 0.999
ADAM_EPS = 1e-08
ADAM_WD = 0.01
ADAM_STEP = 10
N_DEV = 8
LANES = 128
LOG2E = 1.4426950408889634
CUMSUM_W = 256
VMEM_LIMIT = 56 * 1024 * 1024
MESH = pl.DeviceIdType.MESH


def _dot(a, b):
    return jnp.dot(a, b, preferred_element_type=F32)


def _dot_nt(a, b):
    return lax.dot_general(a, b, (((1,), (1,)), ((), ())), preferred_element_type=F32)


def _dot_tn(a, b):
    return lax.dot_general(a, b, (((0,), (0,)), ((), ())), preferred_element_type=F32)


def _cparams(sem):
    return pltpu.CompilerParams(dimension_semantics=sem, vmem_limit_bytes=VMEM_LIMIT)


def _full(a):
    nd = a.ndim
    return pl.BlockSpec(a.shape, lambda *_: (0,) * nd, pipeline_mode=pl.Buffered(1))


def _tok(tm, w):
    return pl.BlockSpec((1, tm, w), lambda b, s: (b, s, 0))


def _perb(rows, w):
    return pl.BlockSpec((1, rows, w), lambda b, s: (b, 0, 0))


def _sds(shape, dtype):
    return jax.ShapeDtypeStruct(shape, dtype)


def _ln_fwd(x, g, b):
    mu = jnp.mean(x, axis=-1, keepdims=True)
    xc = x - mu
    var = jnp.mean(xc * xc, axis=-1, keepdims=True)
    rstd = lax.rsqrt(var + LN_EPS)
    xhat = xc * rstd
    return xhat * g + b, xhat, rstd


def _ln_bwd(dy, xhat, rstd, g):
    dxh = dy * g
    m1 = jnp.mean(dxh, axis=-1, keepdims=True)
    m2 = jnp.mean(dxh * xhat, axis=-1, keepdims=True)
    return rstd * (dxh - m1 - xhat * m2)


def _colsum(a):
    return jnp.sum(a, axis=0, keepdims=True)


def _rope(x, c, s1, s2):
    w = x.shape[-1]
    return x * c + pltpu.roll(x, w - 16, 1) * s1 + pltpu.roll(x, 16, 1) * s2


def _rope_t(x, c, s1, s2):
    w = x.shape[-1]
    return x * c - pltpu.roll(x, w - 16, 1) * s1 - pltpu.roll(x, 16, 1) * s2


def _adamw(w, g, m, v):
    m = ADAM_B1 * m + (1.0 - ADAM_B1) * g
    v = ADAM_B2 * v + (1.0 - ADAM_B2) * (g * g)
    m_hat = m / (1.0 - ADAM_B1 ** ADAM_STEP)
    v_hat = v / (1.0 - ADAM_B2 ** ADAM_STEP)
    delta = -ADAM_LR * (m_hat / (jnp.sqrt(v_hat) + ADAM_EPS) + ADAM_WD * w)
    return delta, m, v


def _my_place():
    return lax.axis_index("x"), lax.axis_index("y"), lax.axis_index("c")


def _chip_peers(mx, my):
    out = []
    for j in (1, 2, 3):
        px = 1 - mx if (j >> 1) else mx
        py = 1 - my if (j & 1) else my
        out.append((px, py, 2 * px + py))
    return out


def _split_peers(everyone):
    mx, my, mc = _my_place()
    if not everyone:
        return [(px, py, mc, pk) for px, py, pk in _chip_peers(mx, my)], 2 * mx + my
    peers = []
    for j in range(1, N_DEV):
        px = 1 - mx if (j >> 2) & 1 else mx
        py = 1 - my if (j >> 1) & 1 else my
        pc = 1 - mc if j & 1 else mc
        peers.append((px, py, pc, 4 * px + 2 * py + pc))
    return peers, 4 * mx + 2 * my + mc


def _hbm_call(body, name, n_in, out_shape, sems):
    hbm = pl.BlockSpec(memory_space=pl.ANY)
    return pl.pallas_call(
        body, name=name, out_shape=out_shape,
        in_specs=[hbm] * n_in, out_specs=[hbm] * len(out_shape),
        scratch_shapes=[pltpu.SemaphoreType.DMA(s) for s in sems])


def _chip_exchange(xs, name, scatter):
    n = len(xs)

    def body(*refs):
        x_refs, o_refs = refs[:n], refs[n:2 * n]
        ssem, rsem, lsem = refs[2 * n:]
        mx, my, mc = _my_place()
        me = 2 * mx + my
        peers = _chip_peers(mx, my)

        def copy(i, j, src_slot, dst_slot):
            px, py, _ = peers[j]
            return pltpu.make_async_remote_copy(
                src_ref=x_refs[i].at[src_slot] if scatter else x_refs[i], dst_ref=o_refs[i].at[dst_slot],
                send_sem=ssem.at[i, j], recv_sem=rsem.at[i, j], device_id=(px, py, mc), device_id_type=MESH)

        local = [pltpu.make_async_copy(x_refs[i].at[me] if scatter else x_refs[i], o_refs[i].at[me], lsem.at[i])
                 for i in range(n)]
        sends = [copy(i, j, peers[j][2], me) for i in range(n) for j in range(3)]
        for cp in local + sends:
            cp.start()
        for i in range(n):
            for j in range(3):
                copy(i, j, peers[j][2], peers[j][2]).wait_recv()
        for cp in sends:
            cp.wait_send()
        for cp in local:
            cp.wait()

    out_shape = [_sds((4,) + tuple(x.shape[1:] if scatter else x.shape), x.dtype) for x in xs]
    return _hbm_call(body, name, n, out_shape, [(n, 3), (n, 3), (n,)])(*xs)


def _chip_exchange_start(xs, name, scatter, after, everyone=False):
    n = len(xs)
    npeer = N_DEV - 1 if everyone else 3
    blks = [tuple(x.shape[1:] if scatter else x.shape) for x in xs]

    def body(*refs):
        x_refs, land_refs = refs[:n], refs[n:2 * n]
        ssem, rsem = refs[2 * n + 1], refs[2 * n + 2]
        token = refs[-1]
        peers, me = _split_peers(everyone)
        for i in range(n):
            for j, (px, py, pc, slot) in enumerate(peers):
                pltpu.make_async_remote_copy(
                    src_ref=x_refs[i].at[slot] if scatter else x_refs[i], dst_ref=land_refs[i].at[me],
                    send_sem=ssem.at[npeer * i + j], recv_sem=rsem.at[npeer * i + j], device_id=(px, py, pc),
                    device_id_type=MESH).start()
        token[...] = jnp.zeros_like(token)

    hbm = pl.BlockSpec(memory_space=pltpu.HBM)
    sem = pl.BlockSpec(memory_space=pltpu.SEMAPHORE)
    lands = [lax.empty((npeer + 1,) + b, x.dtype) for b, x in zip(blks, xs)]
    res = pl.pallas_call(
        body, name=name,
        out_shape=[pltpu.SemaphoreType.DMA((npeer * n,)), pltpu.SemaphoreType.DMA((npeer * n,))]
        + [pltpu.HBM(x.shape, x.dtype) for x in xs] + [pltpu.HBM(l.shape, l.dtype) for l in lands]
        + [_sds((8, LANES), F32)],
        in_specs=[hbm] * (2 * n) + [_AFTER],
        out_specs=[sem, sem] + [hbm] * (2 * n) + [pl.BlockSpec(memory_space=pltpu.VMEM)],
        input_output_aliases={i: 2 + i for i in range(2 * n)},
        compiler_params=pltpu.CompilerParams(has_side_effects=pltpu.SideEffectType.DATAFLOW_SIDE_EFFECTING),
    )(*[pltpu.with_memory_space_constraint(a, pltpu.HBM) for a in list(xs) + lands], after)
    return dict(ssem=res[0], rsem=res[1], xs=res[2:2 + n], lands=res[2 + n:2 + 2 * n], n=n, scatter=scatter,
                everyone=everyone), res[-1]


def _chip_exchange_wait(handle, after, name):
    n, scatter, everyone = handle["n"], handle["scatter"], handle["everyone"]
    npeer = N_DEV - 1 if everyone else 3

    def body(*refs):
        x_refs, land_refs = refs[:n], refs[n:2 * n]
        ssem, rsem = refs[2 * n], refs[2 * n + 1]
        peers, _ = _split_peers(everyone)
        for i in range(n):
            for j, (px, py, pc, slot) in enumerate(peers):
                cp = pltpu.make_async_remote_copy(
                    src_ref=x_refs[i].at[slot] if scatter else x_refs[i], dst_ref=land_refs[i].at[slot],
                    send_sem=ssem.at[npeer * i + j], recv_sem=rsem.at[npeer * i + j], device_id=(px, py, pc),
                    device_id_type=MESH)
                cp.wait_send()
                cp.wait_recv()

    hbm = pl.BlockSpec(memory_space=pltpu.HBM)
    sem = pl.BlockSpec(memory_space=pltpu.SEMAPHORE)
    ops = list(handle["xs"]) + list(handle["lands"])
    res = pl.pallas_call(
        body, name=name,
        out_shape=[pltpu.HBM(a.shape, a.dtype) for a in ops],
        in_specs=[hbm] * (2 * n) + [sem, sem, pl.BlockSpec(memory_space=pl.ANY)],
        out_specs=[hbm] * (2 * n),
        input_output_aliases={i: i for i in range(2 * n)},
        compiler_params=pltpu.CompilerParams(has_side_effects=pltpu.SideEffectType.DATAFLOW_SIDE_EFFECTING),
    )(*ops, handle["ssem"], handle["rsem"], after)
    me = 2 * lax.axis_index("x") + lax.axis_index("y")
    if everyone:
        me = 2 * me + lax.axis_index("c")
    out = []
    for x, land in zip(res[:n], res[n:]):
        own = lax.dynamic_index_in_dim(x, me, 0, keepdims=False) if scatter else x
        out.append(lax.dynamic_update_index_in_dim(land, own, me, 0))
    return out


def _core_gather(xs, name):
    n = len(xs)

    def body(*refs):
        x_refs, o_refs, mine, got = refs[:n], refs[n:2 * n], refs[2 * n:3 * n], refs[3 * n:4 * n]
        lsem, ssem, rsem, osem = refs[4 * n:]
        mx, my, mc = _my_place()
        loads = [pltpu.make_async_copy(x_refs[i], mine[i], lsem.at[i]) for i in range(n)]
        for cp in loads:
            cp.start()
        sends, stores = [], []
        for i in range(n):
            loads[i].wait()
            cp = pltpu.make_async_remote_copy(
                src_ref=mine[i], dst_ref=got[i], send_sem=ssem.at[i], recv_sem=rsem.at[i],
                device_id=(mx, my, 1 - mc), device_id_type=MESH)
            cp.start()
            sends.append(cp)
            for k in range(4):
                st = pltpu.make_async_copy(mine[i].at[k], o_refs[i].at[k, mc], osem.at[i, k])
                st.start()
                stores.append(st)
        for i in range(n):
            sends[i].wait_recv()
            for k in range(4):
                st = pltpu.make_async_copy(got[i].at[k], o_refs[i].at[k, 1 - mc], osem.at[n + i, k])
                st.start()
                stores.append(st)
        for cp in sends:
            cp.wait_send()
        for st in stores:
            st.wait()

    hbm = pl.BlockSpec(memory_space=pl.ANY)
    bufs = [pltpu.VMEM(x.shape, x.dtype) for x in xs]
    return pl.pallas_call(
        body, name=name,
        out_shape=[_sds((4, 2) + tuple(x.shape[1:]), x.dtype) for x in xs],
        in_specs=[hbm] * n, out_specs=[hbm] * n,
        scratch_shapes=bufs + bufs + [pltpu.SemaphoreType.DMA((n,)), pltpu.SemaphoreType.DMA((n,)),
                                      pltpu.SemaphoreType.DMA((n,)), pltpu.SemaphoreType.DMA((2 * n, 4))],
        compiler_params=pltpu.CompilerParams(vmem_limit_bytes=VMEM_LIMIT),
    )(*xs)


def _rows_step(k):
    for r in (256, 128, 64, 32, 16, 8):
        if k % r == 0:
            return r
    return k


def _core_scatter_sum(gs, name):
    n = len(gs)

    def body(*refs):
        g_refs, o_refs = refs[:n], refs[n:2 * n]
        send, got, mine = refs[2 * n:3 * n], refs[3 * n:4 * n], refs[4 * n:5 * n]
        lsem, msem, ssem, rsem, osem = refs[5 * n:]
        mx, my, mc = _my_place()
        pairs = [(i, k) for i in range(n) for k in range(4)]
        out_loads = {(i, k): pltpu.make_async_copy(g_refs[i].at[k, 1 - mc], send[i].at[k], lsem.at[i, k])
                     for i, k in pairs}
        own_loads = {(i, k): pltpu.make_async_copy(g_refs[i].at[k, mc], mine[i].at[k], msem.at[i, k])
                     for i, k in pairs}
        for p in pairs:
            out_loads[p].start()
        for p in pairs:
            own_loads[p].start()
        sends = []
        for i in range(n):
            for k in range(4):
                out_loads[i, k].wait()
            cp = pltpu.make_async_remote_copy(
                src_ref=send[i], dst_ref=got[i], send_sem=ssem.at[i], recv_sem=rsem.at[i],
                device_id=(mx, my, 1 - mc), device_id_type=MESH)
            cp.start()
            sends.append(cp)
        stores = []
        for i in range(n):
            for k in range(4):
                own_loads[i, k].wait()
            sends[i].wait_recv()
            rows = g_refs[i].shape[2]
            step = _rows_step(rows)

            def add(r, _, i=i, step=step):
                sl = pl.ds(pl.multiple_of(r * step, step), step)
                for k in range(4):
                    mine[i][k, sl, :] = (mine[i][k, sl, :].astype(F32) + got[i][k, sl, :].astype(F32)).astype(BF16)
                return 0

            lax.fori_loop(0, rows // step, add, 0)
            st = pltpu.make_async_copy(mine[i], o_refs[i], osem.at[i])
            st.start()
            stores.append(st)
        for cp in sends:
            cp.wait_send()
        for st in stores:
            st.wait()

    hbm = pl.BlockSpec(memory_space=pl.ANY)
    blk = [(4,) + tuple(g.shape[2:]) for g in gs]
    bufs = [pltpu.VMEM(b, BF16) for b in blk]
    return pl.pallas_call(
        body, name=name,
        out_shape=[_sds(b, BF16) for b in blk],
        in_specs=[hbm] * n, out_specs=[hbm] * n,
        scratch_shapes=bufs * 3 + [pltpu.SemaphoreType.DMA((n, 4)), pltpu.SemaphoreType.DMA((n, 4)),
                                   pltpu.SemaphoreType.DMA((n,)), pltpu.SemaphoreType.DMA((n,)),
                                   pltpu.SemaphoreType.DMA((n,))],
        compiler_params=pltpu.CompilerParams(vmem_limit_bytes=VMEM_LIMIT),
    )(*gs)


def _all_gather(xs, name):
    by_chip = _chip_exchange(xs, name + "_chips", scatter=False)
    both = _core_gather(by_chip, name + "_cores")
    return [b.reshape((N_DEV,) + tuple(x.shape)) for b, x in zip(both, xs)]


def _ada_partial(c_all, w_ada_loc, b_loc):
    def body(c_ref, w_ref, b_ref, act_ref, mod_ref):
        c = c_ref[...]
        act = c * (1.0 / (1.0 + jnp.exp(-c)))
        act_ref[...] = act
        mod_ref[...] = _dot(act.astype(BF16), w_ref[...].astype(BF16)) + b_ref[...]

    nb, d = c_all.shape
    return pl.pallas_call(
        body, name="ada_partial",
        out_shape=(_sds((nb, d), F32), _sds((nb, w_ada_loc.shape[1]), F32)),
        compiler_params=pltpu.CompilerParams(vmem_limit_bytes=VMEM_LIMIT),
    )(c_all, w_ada_loc, b_loc)


_AFTER = pl.BlockSpec(memory_space=pl.ANY)


def _inproj_fwd(x, mod, ln_g, ln_b, w_in_p, w_uq_p, w_kv, gq, gkv, tc, ts1, ts2, dm, after):
    B, S, D = x.shape
    tm = dm["tm"]
    sbw, qr, kvr, nh = dm["sbw"], dm["qr"], dm["kvr"], dm["nh"]
    o_cq, o_ckv, o_kr = 3 * sbw, 3 * sbw + qr, 3 * sbw + qr + kvr
    qpw = nh * HEAD_PAD

    def body(x_ref, mod_ref, g_ref, b_ref, win_ref, wuq_ref, wkv_ref, gq_ref, gkv_ref, tc_ref, ts1_ref, ts2_ref, _,
             x0_ref, q_ref, k_ref, v_ref, qp_ref, kp_ref, mv_ref, cq_ref, ckv_ref, qn_ref, kvn_ref):
        x0, _, _ = _ln_fwd(x_ref[0], g_ref[...], b_ref[...])
        x0_ref[0] = x0
        mod = mod_ref[0]
        h = (x0 * (1.0 + mod[1:2]) + mod[0:1]).astype(BF16)
        proj = _dot(h, win_ref[...])
        q_ref[0] = (proj[:, 0:sbw] * SB_Q_SCALE).astype(BF16)
        k_ref[0] = proj[:, sbw:2 * sbw].astype(BF16)
        v_ref[0] = proj[:, 2 * sbw:3 * sbw].astype(BF16)
        cq = proj[:, o_cq:o_cq + qr]
        ckv = proj[:, o_ckv:o_ckv + kvr]
        cq_ref[0] = cq
        ckv_ref[0] = ckv
        qn = (cq * lax.rsqrt(jnp.mean(cq * cq, axis=-1, keepdims=True) + RMS_EPS) * gq_ref[...]).astype(BF16)
        kvn = (ckv * lax.rsqrt(jnp.mean(ckv * ckv, axis=-1, keepdims=True) + RMS_EPS) * gkv_ref[...]).astype(BF16)
        qn_ref[0] = qn
        kvn_ref[0] = kvn
        c1, s1, s2 = tc_ref[...], ts1_ref[...], ts2_ref[...]
        c8, s18, s28 = jnp.tile(c1, (1, nh)), jnp.tile(s1, (1, nh)), jnp.tile(s2, (1, nh))
        qp_ref[0] = (_rope(_dot(qn, wuq_ref[...]), c8, s18, s28) * MLA_Q_SCALE).astype(BF16)
        kvo = _dot(kvn, wkv_ref[...])
        kr = pltpu.roll(proj[:, o_kr:o_kr + LANES], 64, 1)
        kr = _rope(kr, c1, s1, s2)
        kp_ref[0] = (kvo[:, 0:qpw] + jnp.tile(kr, (1, nh))).astype(BF16)
        mv_ref[0] = kvo[:, qpw:].astype(BF16)

    tab = pl.BlockSpec((tm, LANES), lambda b, s: (s, 0))
    outs = [(D, F32), (sbw, BF16), (sbw, BF16), (sbw, BF16), (qpw, BF16), (qpw, BF16),
            (nh * MLA_V, BF16), (qr, F32), (kvr, F32), (qr, BF16), (kvr, BF16)]
    return pl.pallas_call(
        body, name="inproj_fwd", grid=(B, S // tm),
        in_specs=[_tok(tm, D), _perb(N_MOD, D), _full(ln_g), _full(ln_b), _full(w_in_p), _full(w_uq_p),
                  _full(w_kv), _full(gq), _full(gkv), tab, tab, tab, _AFTER],
        out_specs=[_tok(tm, w) for w, _ in outs],
        out_shape=[_sds((B, S, w), t) for w, t in outs],
        compiler_params=_cparams(("parallel", "parallel")),
    )(x, mod, ln_g, ln_b, w_in_p, w_uq_p, w_kv, gq, gkv, tc, ts1, ts2, after)


def _neg_abs(x):
    sign = jnp.uint32(0x80000000)
    return lax.bitcast_convert_type(lax.bitcast_convert_type(x, jnp.uint32) | sign, F32)


SB_Q_SCALE = -(SB_HD ** -0.5) * LOG2E
MLA_Q_SCALE = (MLA_NOPE + MLA_ROPE) ** -0.5 * LOG2E


def _log2_keep(zs):
    return jnp.minimum(zs, 0.0) - jnp.log2(1.0 + jnp.exp2(_neg_abs(zs)))


def _split_dot(a, u):
    hi = a.astype(BF16)
    lo = (a - hi.astype(F32)).astype(BF16)
    return _dot(jnp.concatenate([hi, lo], axis=1), jnp.concatenate([u, u], axis=0))


def _tri(n, rel):
    row = lax.broadcasted_iota(jnp.int32, (n, n), 0)
    col = lax.broadcasted_iota(jnp.int32, (n, n), 1)
    return rel(row, col).astype(BF16)


def _running_sum(a, tri, reverse, split, start):
    cs = tri.shape[0]
    n = a.shape[1] // cs
    out = [None] * n
    run = start
    for c in (reversed(range(n)) if reverse else range(n)):
        part = a[:, c * cs:(c + 1) * cs]
        out[c] = (_split_dot(part, tri) if split else _dot(part.astype(BF16), tri)) + run
        run = run + jnp.sum(part, axis=1, keepdims=True)
    return (out[0] if n == 1 else jnp.concatenate(out, axis=1)), run


def _transpose_bf16(a):
    return a.astype(F32).T.astype(BF16)


def _tile_mask(nr, nk, r0, c0, rel):
    row = lax.broadcasted_iota(jnp.int32, (nr, nk), 0) + r0
    col = lax.broadcasted_iota(jnp.int32, (nr, nk), 1) + c0
    return rel(row, col)


def _put_rows(whole, part, r0):
    return part if r0 == 0 else jnp.concatenate([whole[:r0], part], axis=0)


def _diag_tiles(tq, split):
    half = tq // 2
    return [(0, tq, 0, half), (half, half, half, half)] if split else [(0, tq, 0, tq)]


def _sb_fwd(q, k, v, dm):
    B, S, W = q.shape
    tq = dm["tq"]
    nq = S // tq

    def body(q_ref, k_ref, v_ref, y_ref, tot_ref):
        qi = pl.program_id(2)
        q2 = q_ref[0]
        lane = lax.broadcasted_iota(jnp.int32, (tq, LANES), 1)
        qs = jnp.concatenate([jnp.where(lane < SB_HD, q2, 0), jnp.where(lane >= SB_HD, q2, 0)], axis=0).astype(BF16)
        later = _tri(min(tq, CUMSUM_W), lambda a, b: a > b)
        assert tq & (tq - 1) == 0
        strict = _tile_mask(2 * tq, tq, 0, 0, lambda t, s: s < (t & (tq - 1)))

        def block(j, carry, masked):
            acc, run = carry
            off = pl.multiple_of(j * tq, tq)
            zs = _dot_nt(qs, k_ref[0, pl.ds(off, tq), :])
            a = _log2_keep(zs)
            if masked:
                a = jnp.where(strict, a, 0.0)
            a_later, run = _running_sum(a, later, reverse=True, split=True, start=run)
            w = jnp.exp2((a - zs) + a_later)
            if masked:
                w = jnp.where(strict, w, 0.0)
            return acc + _dot(w.astype(BF16), v_ref[0, pl.ds(off, tq), :]), run

        carry = block(qi, (jnp.zeros((2 * tq, LANES), F32), jnp.zeros((2 * tq, 1), F32)), True)
        acc, run = lax.fori_loop(0, qi, lambda jj, c: block(qi - 1 - jj, c, False), carry)
        y_ref[0] = jnp.where(lane < SB_HD, acc[:tq], acc[tq:]).astype(BF16)
        tot_ref[0] = jnp.where(lane < SB_HD, run[:tq], run[tq:])

    qspec = pl.BlockSpec((1, tq, LANES), lambda b, hp, i: (b, i, hp))
    kspec = pl.BlockSpec((1, S, LANES), lambda b, hp, i: (b, 0, hp))
    return pl.pallas_call(
        body, name="sb_fwd", grid=(B, W // LANES, nq),
        in_specs=[qspec, kspec, kspec],
        out_specs=[qspec, qspec],
        out_shape=[_sds((B, S, W), BF16), _sds((B, S, W), F32)],
        compiler_params=_cparams(("parallel", "parallel", "arbitrary")),
    )(q, k, v)


def _sb_bwd(q, k, v, tot, dy, dm, after):
    B, S, W = q.shape
    tq = dm["tq"]
    nq = S // tq

    def body(q_ref, k_ref, v_ref, tot_ref, dy_ref, _, dq_ref, dk_ref, dv_ref, dk_acc, dv_acc):
        qi = pl.program_id(2)

        @pl.when(qi == 0)
        def _():
            dk_acc[...] = jnp.zeros_like(dk_acc)
            dv_acc[...] = jnp.zeros_like(dv_acc)

        q2 = q_ref[0]
        dy2 = dy_ref[0]
        tot2 = tot_ref[0]
        lane = lax.broadcasted_iota(jnp.int32, (tq, LANES), 1)
        in_h = [lane < SB_HD, lane >= SB_HD]
        qh = [jnp.where(m, q2, 0).astype(BF16) for m in in_h]
        dyh = [jnp.where(m, dy2, 0).astype(BF16) for m in in_h]
        q_t = [_transpose_bf16(a) for a in qh]
        dy_t = [_transpose_bf16(a) for a in dyh]
        toth = [tot2[:, 0:1], tot2[:, SB_HD:SB_HD + 1]]

        def tile(j, carry, r0, nr, c0, nk, masked):
            off = pl.multiple_of(j * tq + c0, math.gcd(tq, c0))
            k2 = k_ref[0, pl.ds(off, nk), :]
            v2 = v_ref[0, pl.ds(off, nk), :]
            upto = _tri(min(nk, CUMSUM_W), lambda a, b: a <= b)
            before = _tri(min(nk, CUMSUM_W), lambda a, b: a < b)
            strict = _tile_mask(nr, nk, r0, c0, lambda t, s: s < t) if masked else None
            rows = slice(r0, r0 + nr)
            new = []
            dk_blk = jnp.zeros((LANES, nk), F32)
            dv_blk = jnp.zeros((LANES, nk), F32)
            for h in range(2):
                dq, pa, pg = carry[3 * h][rows], carry[3 * h + 1][rows], carry[3 * h + 2][rows]
                zs = _dot_nt(qh[h][rows], k2)
                a = _log2_keep(zs)
                if masked:
                    a = jnp.where(strict, a, 0.0)
                a_upto, pa = _running_sum(a, upto, reverse=False, split=True, start=pa)
                w = jnp.exp2((a - zs) - a_upto)
                if masked:
                    w = jnp.where(strict, w, 0.0)
                g = _dot_nt(dyh[h][rows], v2) * w
                g_before, pg = _running_sum(g, before, reverse=False, split=False, start=pg)
                dz = (g + g_before) * jnp.exp2(a) - g_before
                if masked:
                    dz = jnp.where(strict, dz, 0.0)
                dzb = dz.astype(BF16)
                dv_blk = dv_blk + _dot(dy_t[h][:, rows], w.astype(BF16))
                dk_blk = dk_blk + _dot(q_t[h][:, rows], dzb)
                new += [_put_rows(carry[3 * h], dq + _dot(dzb, k2), r0), _put_rows(carry[3 * h + 1], pa, r0),
                        _put_rows(carry[3 * h + 2], pg, r0)]
            dk_acc[j, :, c0:c0 + nk] += dk_blk
            dv_acc[j, :, c0:c0 + nk] += dv_blk
            return tuple(new)

        zero = jnp.zeros((tq, LANES), F32)
        zrun = jnp.zeros((tq, 1), F32)
        carry = lax.fori_loop(0, qi, lambda j, c: tile(j, c, 0, tq, 0, tq, False),
                              (zero, -toth[0], zrun, zero, -toth[1], zrun))
        for r0, nr, c0, nk in _diag_tiles(tq, False):
            carry = tile(qi, carry, r0, nr, c0, nk, True)
        dq_ref[0] = (jnp.where(in_h[0], carry[0], carry[3]) * (SB_HD ** -0.5)).astype(BF16)

        @pl.when(qi == nq - 1)
        def _():
            for jb in range(nq):
                dk_ref[0, jb * tq:(jb + 1) * tq, :] = (dk_acc[jb].T * (-1.0 / LOG2E)).astype(BF16)
                dv_ref[0, jb * tq:(jb + 1) * tq, :] = dv_acc[jb].T.astype(BF16)

    qspec = pl.BlockSpec((1, tq, LANES), lambda b, hp, i: (b, i, hp))
    kspec = pl.BlockSpec((1, S, LANES), lambda b, hp, i: (b, 0, hp))
    return pl.pallas_call(
        body, name="sb_bwd", grid=(B, W // LANES, nq),
        in_specs=[qspec, kspec, kspec, qspec, qspec, _AFTER],
        out_specs=[qspec, kspec, kspec],
        out_shape=[_sds((B, S, W), BF16)] * 3,
        scratch_shapes=[pltpu.VMEM((nq, LANES, tq), F32), pltpu.VMEM((nq, LANES, tq), F32)],
        compiler_params=_cparams(("parallel", "parallel", "arbitrary")),
    )(q, k, v, tot, dy, after)


def _same_or_earlier_chunk(row, col):
    return lax.shift_right_logical(col, 6) <= lax.shift_right_logical(row, 6)


def _mla_fwd(qp, kp, mv, dm, after):
    B, S, QW = qp.shape
    VW = mv.shape[2]
    tq = dm["tq"]
    nq = S // tq
    assert CHUNK == 64

    def body(q_ref, k_ref, v_ref, _, y_ref, lse_ref):
        qi = pl.program_id(2)
        q2 = q_ref[0]
        lane = lax.broadcasted_iota(jnp.int32, (tq, LANES), 1)

        def tile(j, carry, r0, nr, c0, nk, masked):
            off = pl.multiple_of(j * tq + c0, math.gcd(tq, c0))
            v2 = v_ref[0, pl.ds(off, nk), :]
            allowed = _tile_mask(nr, nk, r0, c0, _same_or_earlier_chunk) if masked else None
            rows = slice(r0, r0 + nr)
            heads = range(2)
            sl = [slice(h * HEAD_PAD, (h + 1) * HEAD_PAD) for h in heads]
            m_old = [carry[3 * h + 1][rows] for h in heads]
            s = [_dot_nt(q2[rows, sl[h]], k_ref[0, pl.ds(off, nk), sl[h]]) for h in heads]
            if masked:
                s = [jnp.where(allowed, s[h], -1e30) for h in heads]
            m_new = [jnp.maximum(m_old[h], jnp.max(s[h], axis=1, keepdims=True)) for h in heads]
            alpha = [jnp.exp2(m_old[h] - m_new[h]) for h in heads]
            p = [jnp.exp2(s[h] - m_new[h]) for h in heads]
            acc = [alpha[h] * carry[3 * h][rows] + _dot(p[h].astype(BF16), v2) for h in heads]
            l = [alpha[h] * carry[3 * h + 2][rows] + jnp.sum(p[h], axis=1, keepdims=True) for h in heads]
            out = []
            for h in heads:
                out += [_put_rows(carry[3 * h], acc[h], r0), _put_rows(carry[3 * h + 1], m_new[h], r0),
                        _put_rows(carry[3 * h + 2], l[h], r0)]
            return tuple(out)

        zero = jnp.zeros((tq, LANES), F32)
        m0 = jnp.full((tq, 1), -1e30, F32)
        l0 = jnp.zeros((tq, 1), F32)
        carry = (zero, m0, l0, zero, m0, l0)
        for r0, nr, c0, nk in _diag_tiles(tq, False):
            carry = tile(qi, carry, r0, nr, c0, nk, True)
        carry = lax.fori_loop(0, qi, lambda j, c: tile(j, c, 0, tq, 0, tq, False), carry)
        y0 = carry[0] / carry[2]
        y1 = carry[3] / carry[5]
        y_ref[0] = jnp.where(lane < MLA_V, y0, y1).astype(BF16)
        lse_ref[0] = jnp.where(lane < MLA_V, carry[1] + jnp.log2(carry[2]), carry[4] + jnp.log2(carry[5]))

    qspec = pl.BlockSpec((1, tq, 2 * HEAD_PAD), lambda b, hp, i: (b, i, hp))
    kspec = pl.BlockSpec((1, S, 2 * HEAD_PAD), lambda b, hp, i: (b, 0, hp))
    vspec = pl.BlockSpec((1, S, LANES), lambda b, hp, i: (b, 0, hp))
    yspec = pl.BlockSpec((1, tq, LANES), lambda b, hp, i: (b, i, hp))
    return pl.pallas_call(
        body, name="mla_fwd", grid=(B, VW // LANES, nq),
        in_specs=[qspec, kspec, vspec, _AFTER],
        out_specs=[yspec, yspec],
        out_shape=[_sds((B, S, VW), BF16), _sds((B, S, VW), F32)],
        compiler_params=_cparams(("parallel", "parallel", "arbitrary")),
    )(qp, kp, mv, after)


def _mla_bwd(qp, kp, mv, y, lse, dy, dm, after):
    B, S, QW = qp.shape
    VW = mv.shape[2]
    tq = dm["tq"]
    nq = S // tq
    scale = (MLA_NOPE + MLA_ROPE) ** -0.5

    def body(q_ref, k_ref, v_ref, y_ref, lse_ref, dy_ref, _, dq_ref, dk_ref, dv_ref, dk_acc, dv_acc):
        qi = pl.program_id(2)

        @pl.when(qi == 0)
        def _():
            dk_acc[...] = jnp.zeros_like(dk_acc)
            dv_acc[...] = jnp.zeros_like(dv_acc)

        q2 = q_ref[0]
        dy2 = dy_ref[0]
        lse2 = lse_ref[0]
        lane = lax.broadcasted_iota(jnp.int32, (tq, LANES), 1)
        in_h = [lane < MLA_V, lane >= MLA_V]
        prod = dy2.astype(F32) * y_ref[0].astype(F32)
        delta = [jnp.sum(jnp.where(m, prod, 0.0), axis=1, keepdims=True) for m in in_h]
        dyh = [jnp.where(m, dy2, 0).astype(BF16) for m in in_h]
        lseh = [lse2[:, 0:1], lse2[:, MLA_V:MLA_V + 1]]
        q_t = _transpose_bf16(q2)
        dy_t = [_transpose_bf16(a) for a in dyh]

        def tile(j, carry, r0, nr, c0, nk, masked):
            off = pl.multiple_of(j * tq + c0, math.gcd(tq, c0))
            v2 = v_ref[0, pl.ds(off, nk), :]
            allowed = _tile_mask(nr, nk, r0, c0, _same_or_earlier_chunk) if masked else None
            rows = slice(r0, r0 + nr)
            keys = slice(c0, c0 + nk)
            heads = range(2)
            sl = [slice(h * HEAD_PAD, (h + 1) * HEAD_PAD) for h in heads]
            qhh = [q2[rows, sl[h]] for h in heads]
            dyr = [dyh[h][rows] for h in heads]
            kh = [k_ref[0, pl.ds(off, nk), sl[h]] for h in heads]
            s = [_dot_nt(qhh[h], kh[h]) for h in heads]
            dp = [_dot_nt(dyr[h], v2) for h in heads]
            if masked:
                s = [jnp.where(allowed, s[h], -1e30) for h in heads]
            p = [jnp.exp2(s[h] - lseh[h][rows]) for h in heads]
            dv_acc[j, :, keys] += (_dot(dy_t[0][:, rows], p[0].astype(BF16))
                                   + _dot(dy_t[1][:, rows], p[1].astype(BF16)))
            ds = [(p[h] * (dp[h] - delta[h][rows])).astype(BF16) for h in heads]
            for h in heads:
                dk_acc[j, sl[h], keys] += _dot(q_t[sl[h], rows], ds[h])
            return tuple(_put_rows(carry[h], carry[h][rows] + _dot(ds[h], kh[h]), r0) for h in heads)

        zero = jnp.zeros((tq, HEAD_PAD), F32)
        carry = lax.fori_loop(0, qi, lambda j, c: tile(j, c, 0, tq, 0, tq, False), (zero, zero))
        for r0, nr, c0, nk in _diag_tiles(tq, True):
            carry = tile(qi, carry, r0, nr, c0, nk, True)
        dq_ref[0] = (jnp.concatenate([carry[0], carry[1]], axis=1) * scale).astype(BF16)

        @pl.when(qi == nq - 1)
        def _():
            for jb in range(nq):
                dk_ref[0, jb * tq:(jb + 1) * tq, :] = (dk_acc[jb].T * (1.0 / LOG2E)).astype(BF16)
                dv_ref[0, jb * tq:(jb + 1) * tq, :] = dv_acc[jb].T.astype(BF16)

    qspec = pl.BlockSpec((1, tq, 2 * HEAD_PAD), lambda b, hp, i: (b, i, hp))
    kspec = pl.BlockSpec((1, S, 2 * HEAD_PAD), lambda b, hp, i: (b, 0, hp))
    vspec = pl.BlockSpec((1, S, LANES), lambda b, hp, i: (b, 0, hp))
    yspec = pl.BlockSpec((1, tq, LANES), lambda b, hp, i: (b, i, hp))
    return pl.pallas_call(
        body, name="mla_bwd", grid=(B, VW // LANES, nq),
        in_specs=[qspec, kspec, vspec, yspec, yspec, yspec, _AFTER],
        out_specs=[qspec, kspec, vspec],
        out_shape=[_sds((B, S, QW), BF16), _sds((B, S, QW), BF16), _sds((B, S, VW), BF16)],
        scratch_shapes=[pltpu.VMEM((nq, 2 * HEAD_PAD, tq), F32), pltpu.VMEM((nq, LANES, tq), F32)],
        compiler_params=_cparams(("parallel", "parallel", "arbitrary")),
    )(qp, kp, mv, y, lse, dy, after)


def _outproj_fwd(sb_y, mla_y, x0, mod, w_o, ln_g, ln_b, dm):
    B, S, D = x0.shape
    tm = dm["tm"]
    sbw = sb_y.shape[2]

    def body(ya_ref, yb_ref, x0_ref, mod_ref, wo_ref, g_ref, b_ref, mix_ref, x1_ref, h2_ref):
        mod = mod_ref[0]
        mix = _dot(ya_ref[0], wo_ref[0:sbw, :]) + _dot(yb_ref[0], wo_ref[sbw:, :])
        mix_ref[0] = mix
        x1, _, _ = _ln_fwd(ALPHA * x0_ref[0] + (1.0 + mod[2:3]) * mix, g_ref[...], b_ref[...])
        x1_ref[0] = x1
        h2_ref[0] = (x1 * (1.0 + mod[4:5]) + mod[3:4]).astype(BF16)

    return pl.pallas_call(
        body, name="outproj_fwd", grid=(B, S // tm),
        in_specs=[_tok(tm, sbw), _tok(tm, mla_y.shape[2]), _tok(tm, D), _perb(N_MOD, D),
                  _full(w_o), _full(ln_g), _full(ln_b)],
        out_specs=[_tok(tm, D)] * 3,
        out_shape=[_sds((B, S, D), F32), _sds((B, S, D), F32), _sds((B, S, D), BF16)],
        compiler_params=_cparams(("parallel", "parallel")),
    )(sb_y, mla_y, x0, mod, w_o, ln_g, ln_b)


def _stat_specs(B, D):
    specs = [pl.BlockSpec((1, 8, D), lambda b, s: (b, 0, 0)), pl.BlockSpec((8, D), lambda b, s: (0, 0))]
    shapes = [_sds((B, 8, D), F32), _sds((8, D), F32)]
    return specs, shapes


def _stat_init(bst_ref, wst_ref):
    @pl.when(pl.program_id(1) == 0)
    def _():
        bst_ref[...] = jnp.zeros_like(bst_ref)

    @pl.when((pl.program_id(0) == 0) & (pl.program_id(1) == 0))
    def _():
        wst_ref[...] = jnp.zeros_like(wst_ref)


def _mlp_fwd(h2, x1, mod, target, w_up, w_down, ln_g, ln_b, dm):
    B, S, D = x1.shape
    tm = dm["tm"]
    nck, _, ck = w_up.shape
    dff = nck * ck

    def body(h2_ref, x1_ref, mod_ref, t_ref, wu_ref, wd_ref, g_ref, b_ref, u_ref, dr_ref, bst_ref, wst_ref):
        _stat_init(bst_ref, wst_ref)
        mod = mod_ref[0]
        g = g_ref[...]
        h2 = h2_ref[0]
        ff = jnp.zeros((tm, D), F32)
        for c in range(nck):
            u = _dot(h2, wu_ref[c])
            u_ref[0, :, c * ck:(c + 1) * ck] = u.astype(BF16)
            act = jnp.square(jnp.maximum(u, 0.0)).astype(BF16)
            ff = ff + _dot(act, wd_ref[c])
        x2, xhat, rstd = _ln_fwd(ALPHA * x1_ref[0] + (1.0 + mod[5:6]) * ff, g, b_ref[...])
        err = x2 - t_ref[0]
        dy = err * (1.0 / D)
        dr = _ln_bwd(dy, xhat, rstd, g)
        dr_ref[0] = dr
        bst_ref[0, 0:1, :] += _colsum(dr * ff)
        wst_ref[0:1, :] += _colsum(dy * xhat)
        wst_ref[1:2, :] += _colsum(dy)
        wst_ref[2:3, :] += _colsum(err * err) * (0.5 / D)

    sspecs, sshapes = _stat_specs(B, D)
    return pl.pallas_call(
        body, name="mlp_fwd", grid=(B, S // tm),
        in_specs=[_tok(tm, D), _tok(tm, D), _perb(N_MOD, D), _tok(tm, D), _full(w_up), _full(w_down),
                  _full(ln_g), _full(ln_b)],
        out_specs=[_tok(tm, dff), _tok(tm, D)] + sspecs,
        out_shape=[_sds((B, S, dff), BF16), _sds((B, S, D), F32)] + sshapes,
        compiler_params=_cparams(("arbitrary", "arbitrary")),
    )(h2, x1, mod, target, w_up, w_down, ln_g, ln_b)


def _mlp_bwd(dr2, u, x1, x0, mix, sb_y, mla_y, mod, w_up, w_down, w_o, ln_g, dm):
    B, S, D = x1.shape
    tm = dm["tm_small"]
    sbw = dm["sbw"]
    nck, _, ck = w_up.shape
    dff = nck * ck

    def body(dr_ref, u_ref, x1_ref, x0_ref, mix_ref, ya_ref, yb_ref, mod_ref, wu_ref, wd_ref, wo_ref, g_ref,
             du_ref, dff_ref, dx0_ref, dya_ref, dyb_ref, go_ref, bst_ref, wst_ref):
        _stat_init(bst_ref, wst_ref)

        @pl.when((pl.program_id(0) == 0) & (pl.program_id(1) == 0))
        def _():
            go_ref[...] = jnp.zeros_like(go_ref)

        mod = mod_ref[0]
        dr2 = dr_ref[0]
        dffv = ((1.0 + mod[5:6]) * dr2).astype(BF16)
        dff_ref[0] = dffv
        dh2 = jnp.zeros((tm, D), F32)
        for c in range(nck):
            sl = slice(c * ck, (c + 1) * ck)
            da = _dot_nt(dffv, wd_ref[c])
            du = (da * (2.0 * jnp.maximum(u_ref[0, :, sl].astype(F32), 0.0))).astype(BF16)
            du_ref[0, :, sl] = du
            dh2 = dh2 + _dot_nt(du, wu_ref[c])
        x1 = x1_ref[0]
        dx1 = ALPHA * dr2 + dh2 * (1.0 + mod[4:5])
        bst_ref[0, 0:1, :] += _colsum(dh2 * x1)
        bst_ref[0, 1:2, :] += _colsum(dh2)
        mix = mix_ref[0]
        g = g_ref[...]
        _, xhat, rstd = _ln_fwd(ALPHA * x0_ref[0] + (1.0 + mod[2:3]) * mix, g, 0.0)
        dr1 = _ln_bwd(dx1, xhat, rstd, g)
        wst_ref[0:1, :] += _colsum(dx1 * xhat)
        wst_ref[1:2, :] += _colsum(dx1)
        bst_ref[0, 2:3, :] += _colsum(dr1 * mix)
        dx0_ref[0] = ALPHA * dr1
        dmix = ((1.0 + mod[2:3]) * dr1).astype(BF16)
        dya_ref[0] = _dot_nt(dmix, wo_ref[0:sbw, :]).astype(BF16)
        dyb_ref[0] = _dot_nt(dmix, wo_ref[sbw:, :]).astype(BF16)
        go_ref[0:sbw, :] += _dot_tn(ya_ref[0], dmix)
        go_ref[sbw:, :] += _dot_tn(yb_ref[0], dmix)

    sspecs, sshapes = _stat_specs(B, D)
    wa, wb = sbw, w_o.shape[0] - sbw
    return pl.pallas_call(
        body, name="mlp_bwd", grid=(B, S // tm),
        in_specs=[_tok(tm, D), _tok(tm, dff), _tok(tm, D), _tok(tm, D), _tok(tm, D), _tok(tm, wa), _tok(tm, wb),
                  _perb(N_MOD, D), _full(w_up), _full(w_down), _full(w_o), _full(ln_g)],
        out_specs=[_tok(tm, dff), _tok(tm, D), _tok(tm, D), _tok(tm, wa), _tok(tm, wb),
                   pl.BlockSpec(w_o.shape, lambda b, s: (0, 0))] + sspecs,
        out_shape=[_sds((B, S, dff), BF16), _sds((B, S, D), BF16), _sds((B, S, D), F32),
                   _sds((B, S, wa), BF16), _sds((B, S, wb), BF16), _sds(w_o.shape, F32)] + sshapes,
        compiler_params=_cparams(("arbitrary", "arbitrary")),
    )(dr2, u, x1, x0, mix, sb_y, mla_y, mod, w_up, w_down, w_o, ln_g)


def _inproj_bwd(x, x0, dx0a, mod, ln_g, dq, dk, dv, dqp, dkp, dmv, cq, ckv, qn, kvn, w_in_p, w_uq_p, w_kv, gq, gkv,
                tc, ts1, ts2, dm):
    B, S, D = x.shape
    tm = dm["tm"]
    sbw, qr, kvr, nh = dm["sbw"], dm["qr"], dm["kvr"], dm["nh"]
    qpw = nh * HEAD_PAD
    dinp = w_in_p.shape[1]
    kvw = w_kv.shape[1]

    def body(x_ref, x0_ref, dx0a_ref, mod_ref, g_ref, dq_ref, dk_ref, dv_ref, dqp_ref, dkp_ref, dmv_ref,
             cq_ref, ckv_ref, qn_ref, kvn_ref, win_ref, wuq_ref, wkv_ref, gq_ref, gkv_ref, tc_ref, ts1_ref, ts2_ref,
             gx_ref, gin_ref, guq_ref, gwkv_ref, bst_ref, wst_ref):
        _stat_init(bst_ref, wst_ref)

        @pl.when((pl.program_id(0) == 0) & (pl.program_id(1) == 0))
        def _():
            gin_ref[...] = jnp.zeros_like(gin_ref)
            guq_ref[...] = jnp.zeros_like(guq_ref)
            gwkv_ref[...] = jnp.zeros_like(gwkv_ref)

        mod = mod_ref[0]
        c1, s1, s2 = tc_ref[...], ts1_ref[...], ts2_ref[...]
        c8, s18, s28 = jnp.tile(c1, (1, nh)), jnp.tile(s1, (1, nh)), jnp.tile(s2, (1, nh))
        dqpre = _rope_t(dqp_ref[0].astype(F32), c8, s18, s28).astype(BF16)
        guq_ref[...] += _dot_tn(qn_ref[0], dqpre)
        gq = gq_ref[...]
        cq = cq_ref[0]
        rq = lax.rsqrt(jnp.mean(cq * cq, axis=-1, keepdims=True) + RMS_EPS)
        dqn = _dot_nt(dqpre, wuq_ref[...])
        wst_ref[4:5, 0:qr] += _colsum(dqn * cq * rq)
        dqg = dqn * gq
        dcq = rq * dqg - cq * (rq * rq * rq) * jnp.mean(dqg * cq, axis=-1, keepdims=True)

        dkpre = _rope_t(dkp_ref[0].astype(F32), c8, s18, s28)
        dkr = dkpre[:, 0:HEAD_PAD]
        for h in range(1, nh):
            dkr = dkr + dkpre[:, h * HEAD_PAD:(h + 1) * HEAD_PAD]
        lane = lax.broadcasted_iota(jnp.int32, (tm, LANES), 1)
        dkr = jnp.where((lane >= MLA_NOPE) & (lane < MLA_NOPE + MLA_ROPE), dkr, 0.0)
        dkr = pltpu.roll(dkr, LANES - MLA_NOPE, 1)
        dkvo = jnp.concatenate([dkpre.astype(BF16), dmv_ref[0]], axis=1)
        gwkv_ref[...] += _dot_tn(kvn_ref[0], dkvo)
        gkv = gkv_ref[...]
        ckv = ckv_ref[0]
        rkv = lax.rsqrt(jnp.mean(ckv * ckv, axis=-1, keepdims=True) + RMS_EPS)
        dkvn = _dot_nt(dkvo, wkv_ref[...])
        wst_ref[5:6, 0:kvr] += _colsum(dkvn * ckv * rkv)
        dkg = dkvn * gkv
        dckv = rkv * dkg - ckv * (rkv * rkv * rkv) * jnp.mean(dkg * ckv, axis=-1, keepdims=True)

        dproj = jnp.concatenate([dq_ref[0], dk_ref[0], dv_ref[0], dcq.astype(BF16), dckv.astype(BF16),
                                 dkr.astype(BF16)], axis=1)
        dh = _dot_nt(dproj, win_ref[...])
        x0 = x0_ref[0]
        gin_ref[...] += _dot_tn((x0 * (1.0 + mod[1:2]) + mod[0:1]).astype(BF16), dproj)
        dx0 = dx0a_ref[0] + dh * (1.0 + mod[1:2])
        bst_ref[0, 0:1, :] += _colsum(dh * x0)
        bst_ref[0, 1:2, :] += _colsum(dh)
        g = g_ref[...]
        _, xhat, rstd = _ln_fwd(x_ref[0], g, 0.0)
        gx_ref[0] = _ln_bwd(dx0, xhat, rstd, g)
        wst_ref[0:1, :] += _colsum(dx0 * xhat)
        wst_ref[1:2, :] += _colsum(dx0)

    tab = pl.BlockSpec((tm, LANES), lambda b, s: (s, 0))
    sspecs, sshapes = _stat_specs(B, D)
    return pl.pallas_call(
        body, name="inproj_bwd", grid=(B, S // tm),
        in_specs=[_tok(tm, D), _tok(tm, D), _tok(tm, D), _perb(N_MOD, D), _full(ln_g),
                  _tok(tm, sbw), _tok(tm, sbw), _tok(tm, sbw), _tok(tm, qpw), _tok(tm, qpw), _tok(tm, nh * MLA_V),
                  _tok(tm, qr), _tok(tm, kvr), _tok(tm, qr), _tok(tm, kvr),
                  _full(w_in_p), _full(w_uq_p), _full(w_kv), _full(gq), _full(gkv), tab, tab, tab],
        out_specs=[_tok(tm, D), pl.BlockSpec((D, dinp), lambda b, s: (0, 0)),
                   pl.BlockSpec((qr, qpw), lambda b, s: (0, 0)), pl.BlockSpec((kvr, kvw), lambda b, s: (0, 0))] + sspecs,
        out_shape=[_sds((B, S, D), F32), _sds((D, dinp), F32), _sds((qr, qpw), F32),
                   _sds((kvr, kvw), F32)] + sshapes,
        compiler_params=_cparams(("arbitrary", "arbitrary")),
    )(x, x0, dx0a, mod, ln_g, dq, dk, dv, dqp, dkp, dmv, cq, ckv, qn, kvn, w_in_p, w_uq_p, w_kv, gq, gkv,
      tc, ts1, ts2)


def _tile_of(n, cap):
    if n <= cap:
        return n
    best = n
    for t in range(LANES, cap + 1, LANES):
        if n % t == 0:
            best = t
    return best


def _mm_tn(a, g, name, after, relu_sq=False, out_dtype=F32, col_blocks=None):
    T, K = a.shape
    N = g.shape[1]
    tt = 1024 if T % 1024 == 0 else (512 if T % 512 == 0 else T)
    tk = _tile_of(K, 1024)
    tn = _tile_of(N, 1280)
    nt = T // tt
    bw = N // col_blocks if col_blocks else tn
    assert tn % bw == 0

    def body(a_ref, g_ref, _, o_ref, acc_ref):
        @pl.when(pl.program_id(2) == 0)
        def _():
            acc_ref[...] = jnp.zeros_like(acc_ref)

        av = a_ref[...]
        if relu_sq:
            av = jnp.square(jnp.maximum(av.astype(F32), 0.0)).astype(BF16)
        acc_ref[...] += _dot_tn(av, g_ref[...])

        @pl.when(pl.program_id(2) == nt - 1)
        def _():
            if col_blocks:
                for c in range(tn // bw):
                    o_ref[c] = acc_ref[:, c * bw:(c + 1) * bw].astype(out_dtype)
            else:
                o_ref[...] = acc_ref[...].astype(out_dtype)

    if col_blocks:
        out_spec = pl.BlockSpec((tn // bw, tk, bw), lambda i, j, t: (j, i, 0))
        out_shape = _sds((col_blocks, K, bw), out_dtype)
    else:
        out_spec = pl.BlockSpec((tk, tn), lambda i, j, t: (i, j))
        out_shape = _sds((K, N), out_dtype)
    return pl.pallas_call(
        body, name=name, grid=(K // tk, N // tn, nt),
        in_specs=[pl.BlockSpec((tt, tk), lambda i, j, t: (t, i)), pl.BlockSpec((tt, tn), lambda i, j, t: (t, j)),
                  _AFTER],
        out_specs=out_spec, out_shape=out_shape,
        scratch_shapes=[pltpu.VMEM((tk, tn), F32)],
        compiler_params=_cparams(("parallel", "parallel", "arbitrary")),
    )(a, g, after)


def _reduce_adamw(parts, w, m, v, name):
    P, K, N = parts.shape
    tr = 256 if K % 256 == 0 else K

    def body(p_ref, w_ref, m_ref, v_ref, g_ref, d_ref, nm_ref, nv_ref):
        g = p_ref[0].astype(F32)
        for k in range(1, P):
            g = g + p_ref[k].astype(F32)
        g_ref[0] = g
        d_ref[0], nm_ref[0], nv_ref[0] = _adamw(w_ref[0], g, m_ref[0], v_ref[0])

    spec = pl.BlockSpec((1, tr, N), lambda r: (0, r, 0))
    return pl.pallas_call(
        body, name=name, grid=(K // tr,),
        in_specs=[pl.BlockSpec((P, tr, N), lambda r: (0, r, 0)), spec, spec, spec],
        out_specs=[spec] * 4, out_shape=[_sds((1, K, N), F32)] * 4,
        compiler_params=_cparams(("parallel",)),
    )(parts, w, m, v)


def _finish(sm, dmod_all, dmod_my, cact_all, p_small, m_small, v_small, b_ada, m_b, v_b, w_ada, m_w, v_w):
    n0 = p_small.shape[1]
    n1 = sm.shape[1]
    d = cact_all.shape[1]

    def body(sm_ref, dma_ref, dmm_ref, ca_ref, p_ref, pm_ref, pv_ref, b_ref, bm_ref, bv_ref, w_ref, wm_ref, wv_ref,
             gs_ref, ds_ref, ms_ref, vs_ref, gb_ref, db_ref, mb_ref, vb_ref, gw_ref, dw_ref, mw_ref, vw_ref,
             loss_ref):
        gs = sm_ref[0:1, :]
        for k in range(1, N_DEV):
            gs = gs + sm_ref[k:k + 1, :]
        gs_ref[...] = gs
        ds_ref[...], ms_ref[...], vs_ref[...] = _adamw(p_ref[...], gs[:, 0:n0], pm_ref[...], pv_ref[...])
        loss_ref[...] = jnp.zeros((1, LANES), F32) + jnp.sum(gs[:, n1 - d:n1])
        gb = jnp.sum(dma_ref[...], axis=0, keepdims=True)
        gb_ref[...] = gb
        db_ref[...], mb_ref[...], vb_ref[...] = _adamw(b_ref[...], gb, bm_ref[...], bv_ref[...])
        gw = _dot_tn(ca_ref[...].astype(BF16), dmm_ref[...].astype(BF16))
        gw_ref[...] = gw
        dw_ref[...], mw_ref[...], vw_ref[...] = _adamw(w_ref[...], gw, wm_ref[...], wv_ref[...])

    s0 = _sds(p_small.shape, F32)
    sb = _sds(b_ada.shape, F32)
    sw = _sds(w_ada.shape, F32)
    return pl.pallas_call(
        body, name="finish_small",
        out_shape=[_sds((1, n1), F32), s0, s0, s0, sb, sb, sb, sb, sw, sw, sw, sw,
                   _sds((1, LANES), F32)],
        compiler_params=pltpu.CompilerParams(vmem_limit_bytes=VMEM_LIMIT),
    )(sm, dmod_all, dmod_my, cact_all, p_small, m_small, v_small, b_ada, m_b, v_b, w_ada, m_w, v_w)


def _pack(arrs, dtype, width):
    flat = jnp.concatenate([a.astype(dtype).reshape(-1) for a in arrs])
    rows = -(-flat.shape[0] // (256 * width)) * 256
    return jnp.pad(flat, (0, rows * width - flat.shape[0])).reshape(rows, width)


def _unpack(slab, shapes):
    flat = slab.reshape(-1)
    out, o = [], 0
    for s in shapes:
        n = math.prod(s)
        out.append(flat[o:o + n].reshape(s))
        o += n
    return out


def _rope_tables(S):
    inv_freq = 1.0 / (ROPE_BASE ** (jnp.arange(0, MLA_ROPE, 2, dtype=F32) / MLA_ROPE))
    ang = jnp.arange(S, dtype=F32)[:, None] * inv_freq[None, :]
    cos, sin = jnp.cos(ang), jnp.sin(ang)
    one = jnp.ones((S, MLA_NOPE), F32)
    z16 = jnp.zeros((S, 16), F32)
    z32 = jnp.zeros((S, 32), F32)
    z64 = jnp.zeros((S, MLA_NOPE), F32)
    tc = jnp.concatenate([one, cos, cos, jnp.ones((S, 32), F32)], axis=1)
    ts1 = jnp.concatenate([z64, -sin, z16, z32], axis=1)
    ts2 = jnp.concatenate([z64, z16, sin, z32], axis=1)
    return tc, ts1, ts2


def kernel(x, c, ln_in_g, ln_in_b, w_ada, b_ada, w_in, q_norm_g, kv_norm_g, w_uq, w_ukv, w_o, ln1_g, ln1_b, w_up, w_down, ln2_g, ln2_b, loss_target, m_ln_in_g, m_ln_in_b, m_w_ada, m_b_ada, m_w_in, m_q_norm_g, m_kv_norm_g, m_w_uq, m_w_ukv, m_w_o, m_ln1_g, m_ln1_b, m_w_up, m_w_down, m_ln2_g, m_ln2_b, v_ln_in_g, v_ln_in_b, v_w_ada, v_b_ada, v_w_in, v_q_norm_g, v_kv_norm_g, v_w_uq, v_w_ukv, v_w_o, v_ln1_g, v_ln1_b, v_w_up, v_w_down, v_ln2_g, v_ln2_b):
    B, S, D = x.shape
    sbw = D // 2
    mlw = D - sbw
    nh = mlw // MLA_V
    qr = w_uq.shape[1]
    kvr = w_ukv.shape[1]
    qk = MLA_NOPE + MLA_ROPE
    dff = w_up.shape[2] * N_DEV
    din = w_in.shape[2] * N_DEV
    tm = 512 if S % 512 == 0 else S
    tq = min(512, S // 2)
    dm = dict(tm=tm, tm_small=min(tm, 256), tq=tq, sbw=sbw, qr=qr, kvr=kvr, nh=nh)
    dev =4 * lax.axis_index("x") + 2 * lax.axis_index("y") + lax.axis_index("c")

    big = [w_in, w_uq, w_ukv, w_o, w_up, w_down]
    first_w, first_token = _chip_exchange_start([a[0].astype(BF16) for a in big[:3]], "gather_w_first_start",
                                                scatter=False, after=c)

    nada = w_ada.shape[2]
    c_all = _all_gather([c + first_token[0, 0]], "gather_c")[0].reshape(N_DEV * B, D)
    b_loc = lax.dynamic_slice(b_ada, (0, dev * nada), (1, nada))
    cact_all, mod_part = _ada_partial(c_all, w_ada[0], b_loc)
    mod_all = _all_gather([mod_part], "gather_mod")[0]
    mod = lax.dynamic_slice(mod_all, (0, dev * B, 0), (N_DEV, B, nada))
    mod = jnp.swapaxes(mod, 0, 1).reshape(B, N_MOD, D)

    first_by_chip = _chip_exchange_wait(first_w, mod_all, "gather_w_first_wait")
    w_in8, w_uq8, w_ukv8 = [b.reshape((N_DEV,) + b.shape[2:]) for b in _core_gather(first_by_chip, "gather_w_first_cores")]
    late_w, late_token = _chip_exchange_start([a[0].astype(BF16) for a in big[3:]], "gather_w_late_start",
                                              scatter=False, after=w_in8, everyone=True)
    cols = lambda a8: jnp.swapaxes(a8, 0, 1).reshape(a8.shape[1], N_DEV * a8.shape[2])
    w_in_p = jnp.pad(cols(w_in8), ((0, 0), (0, LANES - MLA_ROPE)))
    zpad = jnp.zeros((qr, nh, HEAD_PAD - qk), BF16)
    w_uq_p = jnp.concatenate([cols(w_uq8).reshape(qr, nh, qk), zpad], axis=2).reshape(qr, nh * HEAD_PAD)
    w_ukv_f = cols(w_ukv8)
    w_uk = w_ukv_f[:, :nh * MLA_NOPE].reshape(kvr, nh, MLA_NOPE)
    w_uk_p = jnp.concatenate([w_uk, jnp.zeros((kvr, nh, HEAD_PAD - MLA_NOPE), BF16)], axis=2)
    w_kv = jnp.concatenate([w_uk_p.reshape(kvr, nh * HEAD_PAD), w_ukv_f[:, nh * MLA_NOPE:]], axis=1)

    tc, ts1, ts2 = _rope_tables(S)
    g_in, b_in = ln_in_g.reshape(1, D), ln_in_b.reshape(1, D)
    (x0, sq, sk, sv, qp, kp, mv, cq, ckv, qn, kvn) = _inproj_fwd(
        x, mod, g_in, b_in, w_in_p, w_uq_p, w_kv, q_norm_g, kv_norm_g, tc, ts1, ts2, dm, late_token)
    sb_y, sb_tot = _sb_fwd(sq, sk, sv, dm)
    mla_y, mla_lse = _mla_fwd(qp, kp, mv, dm, sb_tot)
    w_o8, w_up8, w_down8 = _chip_exchange_wait(late_w, mla_lse, "gather_w_late_wait")
    w_o_f = w_o8.reshape(D, D)
    mix, x1, h2 = _outproj_fwd(sb_y, mla_y, x0, mod, w_o_f, ln1_g, ln1_b, dm)
    u, dr2, bst_c, wst_c = _mlp_fwd(h2, x1, mod, loss_target, w_up8, w_down8, ln2_g, ln2_b, dm)

    du, dffb, dx0a, dsb_y, dmla_y, g_o, bst_b, wst_b = _mlp_bwd(
        dr2, u, x1, x0, mix, sb_y, mla_y, mod, w_up8, w_down8, w_o_f, ln1_g, dm)
    T = B * S
    r2 = lambda a: a.reshape(T, a.shape[2])
    by_core = lambda a: a.reshape((4, 2) + a.shape[1:])
    g_o = g_o.astype(BF16)
    g_up8 = _mm_tn(r2(h2), r2(du), "grad_w_up", dr2, out_dtype=BF16, col_blocks=N_DEV)
    g_down = _mm_tn(r2(u), r2(dffb), "grad_w_down", dr2, relu_sq=True, out_dtype=BF16)
    early = [g_o.reshape(N_DEV, D // N_DEV, D), g_up8, g_down.reshape(N_DEV, dff // N_DEV, D)]
    early_g, early_token = _chip_exchange_start(early, "scatter_g_early_start", scatter=True, after=dr2,
                                                everyone=True)

    dsq, dsk, dsv = _sb_bwd(sq, sk, sv, sb_tot, dsb_y, dm, early_token)
    dqp, dkp, dmv = _mla_bwd(qp, kp, mv, mla_y, mla_lse, dmla_y, dm, dsq)
    grad_x, g_in_p, g_uq_p, g_kv, bst_a, wst_a = _inproj_bwd(
        x, x0, dx0a, mod, g_in, dsq, dsk, dsv, dqp, dkp, dmv, cq, ckv, qn, kvn, w_in_p, w_uq_p, w_kv,
        q_norm_g, kv_norm_g, tc, ts1, ts2, dm)

    dmod = jnp.concatenate([bst_a[:, 1], bst_a[:, 0], bst_b[:, 2], bst_b[:, 1], bst_b[:, 0], bst_c[:, 0]], axis=1)
    small = jnp.concatenate([wst_a[0], wst_a[1], wst_a[4, :qr], wst_a[5, :kvr], wst_b[0], wst_b[1],
                             wst_c[0], wst_c[1], wst_c[2]])
    n1 = small.shape[0]
    small_g, small_token = _chip_exchange_start([_pack([dmod, small], F32, LANES)], "gather_small_start",
                                                scatter=False, after=grad_x)
    g_uq_f = g_uq_p.reshape(qr, nh, HEAD_PAD)[:, :, :qk].reshape(qr, nh * qk)
    g_uk = g_kv[:, :nh * HEAD_PAD].reshape(kvr, nh, HEAD_PAD)[:, :, :MLA_NOPE].reshape(kvr, nh * MLA_NOPE)
    g_ukv_f = jnp.concatenate([g_uk, g_kv[:, nh * HEAD_PAD:]], axis=1)
    early_quarter = _chip_exchange_wait(early_g, g_in_p, "scatter_g_early_wait")

    def by_dest_cols(a):
        k, n = a.shape[0], a.shape[1] // N_DEV
        return jnp.swapaxes(a.reshape(k, N_DEV, n), 0, 1).astype(BF16)

    last = [by_dest_cols(g_in_p[:, :din] + small_token[0, 0]), by_dest_cols(g_uq_f), by_dest_cols(g_ukv_f)]
    last_sum = _core_scatter_sum([by_core(a) for a in last], "scatter_g_last_cores")
    small_by_chip = _chip_exchange_wait(small_g, last_sum[0], "gather_small_wait")
    both = _core_gather(small_by_chip, "gather_small_cores")[0].reshape(N_DEV, -1)
    last_g, last_token = _chip_exchange_start(last_sum, "scatter_g_last_start", scatter=True, after=grad_x)
    names = ["w_in", "w_uq", "w_ukv", "w_o", "w_up", "w_down"]
    moms = [m_w_in, m_w_uq, m_w_ukv, m_w_o, m_w_up, m_w_down]
    vars_ = [v_w_in, v_w_uq, v_w_ukv, v_w_o, v_w_up, v_w_down]
    res_early = [_reduce_adamw(p, w, m, v, "adamw_" + n)
                 for p, w, m, v, n in zip(early_quarter, big[3:], moms[3:], vars_[3:], names[3:])]

    dmod_all = both[:, :B * N_MOD * D].reshape(N_DEV * B, N_MOD * D)
    sm = both[:, B * N_MOD * D:B * N_MOD * D + n1] + last_token[0, 0]
    dmod_my = lax.dynamic_slice(dmod_all, (0, dev * nada), (N_DEV * B, nada))
    row = lambda arrs: jnp.concatenate([a.reshape(1, -1) for a in arrs], axis=1)
    smalls = [ln_in_g, ln_in_b, q_norm_g, kv_norm_g, ln1_g, ln1_b, ln2_g, ln2_b]
    small_shapes = [a.shape for a in smalls]
    (gs, ds, nms, nvs, g_b, d_b, nm_b, nv_b, g_w, d_w, nm_w, nv_w, loss_v) = _finish(
        sm, dmod_all, dmod_my, cact_all, row(smalls),
        row([m_ln_in_g, m_ln_in_b, m_q_norm_g, m_kv_norm_g, m_ln1_g, m_ln1_b, m_ln2_g, m_ln2_b]),
        row([v_ln_in_g, v_ln_in_b, v_q_norm_g, v_kv_norm_g, v_ln1_g, v_ln1_b, v_ln2_g, v_ln2_b]),
        b_ada, m_b_ada, v_b_ada, w_ada[0], m_w_ada[0], v_w_ada[0])
    gsm, dsm, nmsm, nvsm = (_unpack(s, small_shapes) for s in (gs, ds, nms, nvs))
    last_quarter = _chip_exchange_wait(last_g, loss_v, "scatter_g_last_wait")
    res_last = [_reduce_adamw(p, w, m, v, "adamw_" + n)
                for p, w, m, v, n in zip(last_quarter, big[:3], moms[:3], vars_[:3], names[:3])]
    gb, db, nmb, nvb = ([r[i] for r in res_last + res_early] for i in range(4))

    def ordered(sm_l, w_l, ada_w, ada_b):
        return [sm_l[0], sm_l[1], ada_w[None], ada_b, w_l[0], sm_l[2], sm_l[3], w_l[1], w_l[2], w_l[3],
                sm_l[4], sm_l[5], w_l[4], w_l[5], sm_l[6], sm_l[7]]

    loss = loss_v[0, 0]
    return (loss, grad_x, *ordered(gsm, gb, g_w, g_b), *ordered(dsm, db, d_w, d_b),
            *ordered(nmsm, nmb, nm_w, nm_b), *ordered(nvsm, nvb, nv_w, nv_b))
```

```python
import math

import jax
import jax.numpy as jnp
from jax import lax
from jax.experimental import pallas as pl
from jax.experimental.pallas import tpu as pltpu

F32 = jnp.float32
BF16 = jnp.bfloat16

SB_HD = 64
MLA_V = 64
MLA_NOPE = 64
MLA_ROPE = 32
HEAD_PAD = 128
CHUNK = 64
ROPE_BASE = 10000.0
LN_EPS = 1e-5
RMS_EPS = 1e-6
DEPTH = 1
ALPHA = (2.0 * DEPTH) ** 0.25
N_MOD = 6
ADAM_LR = 0.001
ADAM_B1 = 0.9
ADAM_B2 = 0.999
ADAM_EPS = 1e-08
ADAM_WD = 0.01
ADAM_STEP = 10
N_DEV = 8
LANES = 128
LOG2E = 1.4426950408889634
CUMSUM_W = 256
VMEM_LIMIT = 56 * 1024 * 1024
MESH = pl.DeviceIdType.MESH


def _dot(a, b):
    return jnp.dot(a, b, preferred_element_type=F32)


def _dot_nt(a, b):
    return lax.dot_general(a, b, (((1,), (1,)), ((), ())), preferred_element_type=F32)


def _dot_tn(a, b):
    return lax.dot_general(a, b, (((0,), (0,)), ((), ())), preferred_element_type=F32)


def _cparams(sem):
    return pltpu.CompilerParams(dimension_semantics=sem, vmem_limit_bytes=VMEM_LIMIT)


def _full(a):
    nd = a.ndim
    return pl.BlockSpec(a.shape, lambda *_: (0,) * nd, pipeline_mode=pl.Buffered(1))


def _tok(tm, w):
    return pl.BlockSpec((1, tm, w), lambda b, s: (b, s, 0))


def _perb(rows, w):
    return pl.BlockSpec((1, rows, w), lambda b, s: (b, 0, 0))


def _sds(shape, dtype):
    return jax.ShapeDtypeStruct(shape, dtype)


def _ln_fwd(x, g, b):
    mu = jnp.mean(x, axis=-1, keepdims=True)
    xc = x - mu
    var = jnp.mean(xc * xc, axis=-1, keepdims=True)
    rstd = lax.rsqrt(var + LN_EPS)
    xhat = xc * rstd
    return xhat * g + b, xhat, rstd


def _ln_bwd(dy, xhat, rstd, g):
    dxh = dy * g
    m1 = jnp.mean(dxh, axis=-1, keepdims=True)
    m2 = jnp.mean(dxh * xhat, axis=-1, keepdims=True)
    return rstd * (dxh - m1 - xhat * m2)


def _colsum(a):
    return jnp.sum(a, axis=0, keepdims=True)


def _rope(x, c, s1, s2):
    w = x.shape[-1]
    return x * c + pltpu.roll(x, w - 16, 1) * s1 + pltpu.roll(x, 16, 1) * s2


def _rope_t(x, c, s1, s2):
    w = x.shape[-1]
    return x * c - pltpu.roll(x, w - 16, 1) * s1 - pltpu.roll(x, 16, 1) * s2


def _adamw(w, g, m, v):
    m = ADAM_B1 * m + (1.0 - ADAM_B1) * g
    v = ADAM_B2 * v + (1.0 - ADAM_B2) * (g * g)
    m_hat = m / (1.0 - ADAM_B1 ** ADAM_STEP)
    v_hat = v / (1.0 - ADAM_B2 ** ADAM_STEP)
    delta = -ADAM_LR * (m_hat / (jnp.sqrt(v_hat) + ADAM_EPS) + ADAM_WD * w)
    return delta, m, v


def _my_place():
    return lax.axis_index("x"), lax.axis_index("y"), lax.axis_index("c")


def _chip_peers(mx, my):
    out = []
    for j in (1, 2, 3):
        px = 1 - mx if (j >> 1) else mx
        py = 1 - my if (j & 1) else my
        out.append((px, py, 2 * px + py))
    return out


def _split_peers(everyone):
    mx, my, mc = _my_place()
    if not everyone:
        return [(px, py, mc, pk) for px, py, pk in _chip_peers(mx, my)], 2 * mx + my
    peers = []
    for j in range(1, N_DEV):
        px = 1 - mx if (j >> 2) & 1 else mx
        py = 1 - my if (j >> 1) & 1 else my
        pc = 1 - mc if j & 1 else mc
        peers.append((px, py, pc, 4 * px + 2 * py + pc))
    return peers, 4 * mx + 2 * my + mc


def _hbm_call(body, name, n_in, out_shape, sems):
    hbm = pl.BlockSpec(memory_space=pl.ANY)
    return pl.pallas_call(
        body, name=name, out_shape=out_shape,
        in_specs=[hbm] * n_in, out_specs=[hbm] * len(out_shape),
        scratch_shapes=[pltpu.SemaphoreType.DMA(s) for s in sems])


def _chip_exchange(xs, name, scatter):
    n = len(xs)

    def body(*refs):
        x_refs, o_refs = refs[:n], refs[n:2 * n]
        ssem, rsem, lsem = refs[2 * n:]
        mx, my, mc = _my_place()
        me = 2 * mx + my
        peers = _chip_peers(mx, my)

        def copy(i, j, src_slot, dst_slot):
            px, py, _ = peers[j]
            return pltpu.make_async_remote_copy(
                src_ref=x_refs[i].at[src_slot] if scatter else x_refs[i], dst_ref=o_refs[i].at[dst_slot],
                send_sem=ssem.at[i, j], recv_sem=rsem.at[i, j], device_id=(px, py, mc), device_id_type=MESH)

        local = [pltpu.make_async_copy(x_refs[i].at[me] if scatter else x_refs[i], o_refs[i].at[me], lsem.at[i])
                 for i in range(n)]
        sends = [copy(i, j, peers[j][2], me) for i in range(n) for j in range(3)]
        for cp in local + sends:
            cp.start()
        for i in range(n):
            for j in range(3):
                copy(i, j, peers[j][2], peers[j][2]).wait_recv()
        for cp in sends:
            cp.wait_send()
        for cp in local:
            cp.wait()

    out_shape = [_sds((4,) + tuple(x.shape[1:] if scatter else x.shape), x.dtype) for x in xs]
    return _hbm_call(body, name, n, out_shape, [(n, 3), (n, 3), (n,)])(*xs)


def _chip_exchange_start(xs, name, scatter, after, everyone=False):
    n = len(xs)
    npeer = N_DEV - 1 if everyone else 3
    blks = [tuple(x.shape[1:] if scatter else x.shape) for x in xs]

    def body(*refs):
        x_refs, land_refs = refs[:n], refs[n:2 * n]
        ssem, rsem = refs[2 * n + 1], refs[2 * n + 2]
        token = refs[-1]
        peers, me = _split_peers(everyone)
        for i in range(n):
            for j, (px, py, pc, slot) in enumerate(peers):
                pltpu.make_async_remote_copy(
                    src_ref=x_refs[i].at[slot] if scatter else x_refs[i], dst_ref=land_refs[i].at[me],
                    send_sem=ssem.at[npeer * i + j], recv_sem=rsem.at[npeer * i + j], device_id=(px, py, pc),
                    device_id_type=MESH).start()
        token[...] = jnp.zeros_like(token)

    hbm = pl.BlockSpec(memory_space=pltpu.HBM)
    sem = pl.BlockSpec(memory_space=pltpu.SEMAPHORE)
    lands = [lax.empty((npeer + 1,) + b, x.dtype) for b, x in zip(blks, xs)]
    res = pl.pallas_call(
        body, name=name,
        out_shape=[pltpu.SemaphoreType.DMA((npeer * n,)), pltpu.SemaphoreType.DMA((npeer * n,))]
        + [pltpu.HBM(x.shape, x.dtype) for x in xs] + [pltpu.HBM(l.shape, l.dtype) for l in lands]
        + [_sds((8, LANES), F32)],
        in_specs=[hbm] * (2 * n) + [_AFTER],
        out_specs=[sem, sem] + [hbm] * (2 * n) + [pl.BlockSpec(memory_space=pltpu.VMEM)],
        input_output_aliases={i: 2 + i for i in range(2 * n)},
        compiler_params=pltpu.CompilerParams(has_side_effects=pltpu.SideEffectType.DATAFLOW_SIDE_EFFECTING),
    )(*[pltpu.with_memory_space_constraint(a, pltpu.HBM) for a in list(xs) + lands], after)
    return dict(ssem=res[0], rsem=res[1], xs=res[2:2 + n], lands=res[2 + n:2 + 2 * n], n=n, scatter=scatter,
                everyone=everyone), res[-1]


def _chip_exchange_wait(handle, after, name):
    n, scatter, everyone = handle["n"], handle["scatter"], handle["everyone"]
    npeer = N_DEV - 1 if everyone else 3

    def body(*refs):
        x_refs, land_refs = refs[:n], refs[n:2 * n]
        ssem, rsem = refs[2 * n], refs[2 * n + 1]
        peers, _ = _split_peers(everyone)
        for i in range(n):
            for j, (px, py, pc, slot) in enumerate(peers):
                cp = pltpu.make_async_remote_copy(
                    src_ref=x_refs[i].at[slot] if scatter else x_refs[i], dst_ref=land_refs[i].at[slot],
                    send_sem=ssem.at[npeer * i + j], recv_sem=rsem.at[npeer * i + j], device_id=(px, py, pc),
                    device_id_type=MESH)
                cp.wait_send()
                cp.wait_recv()

    hbm = pl.BlockSpec(memory_space=pltpu.HBM)
    sem = pl.BlockSpec(memory_space=pltpu.SEMAPHORE)
    ops = list(handle["xs"]) + list(handle["lands"])
    res = pl.pallas_call(
        body, name=name,
        out_shape=[pltpu.HBM(a.shape, a.dtype) for a in ops],
        in_specs=[hbm] * (2 * n) + [sem, sem, pl.BlockSpec(memory_space=pl.ANY)],
        out_specs=[hbm] * (2 * n),
        input_output_aliases={i: i for i in range(2 * n)},
        compiler_params=pltpu.CompilerParams(has_side_effects=pltpu.SideEffectType.DATAFLOW_SIDE_EFFECTING),
    )(*ops, handle["ssem"], handle["rsem"], after)
    me = 2 * lax.axis_index("x") + lax.axis_index("y")
    if everyone:
        me = 2 * me + lax.axis_index("c")
    out = []
    for x, land in zip(res[:n], res[n:]):
        own = lax.dynamic_index_in_dim(x, me, 0, keepdims=False) if scatter else x
        out.append(lax.dynamic_update_index_in_dim(land, own, me, 0))
    return out


def _core_gather(xs, name):
    n = len(xs)

    def body(*refs):
        x_refs, o_refs, mine, got = refs[:n], refs[n:2 * n], refs[2 * n:3 * n], refs[3 * n:4 * n]
        lsem, ssem, rsem, osem = refs[4 * n:]
        mx, my, mc = _my_place()
        loads = [pltpu.make_async_copy(x_refs[i], mine[i], lsem.at[i]) for i in range(n)]
        for cp in loads:
            cp.start()
        sends, stores = [], []
        for i in range(n):
            loads[i].wait()
            cp = pltpu.make_async_remote_copy(
                src_ref=mine[i], dst_ref=got[i], send_sem=ssem.at[i], recv_sem=rsem.at[i],
                device_id=(mx, my, 1 - mc), device_id_type=MESH)
            cp.start()
            sends.append(cp)
            for k in range(4):
                st = pltpu.make_async_copy(mine[i].at[k], o_refs[i].at[k, mc], osem.at[i, k])
                st.start()
                stores.append(st)
        for i in range(n):
            sends[i].wait_recv()
            for k in range(4):
                st = pltpu.make_async_copy(got[i].at[k], o_refs[i].at[k, 1 - mc], osem.at[n + i, k])
                st.start()
                stores.append(st)
        for cp in sends:
            cp.wait_send()
        for st in stores:
            st.wait()

    hbm = pl.BlockSpec(memory_space=pl.ANY)
    bufs = [pltpu.VMEM(x.shape, x.dtype) for x in xs]
    return pl.pallas_call(
        body, name=name,
        out_shape=[_sds((4, 2) + tuple(x.shape[1:]), x.dtype) for x in xs],
        in_specs=[hbm] * n, out_specs=[hbm] * n,
        scratch_shapes=bufs + bufs + [pltpu.SemaphoreType.DMA((n,)), pltpu.SemaphoreType.DMA((n,)),
                                      pltpu.SemaphoreType.DMA((n,)), pltpu.SemaphoreType.DMA((2 * n, 4))],
        compiler_params=pltpu.CompilerParams(vmem_limit_bytes=VMEM_LIMIT),
    )(*xs)


def _rows_step(k):
    for r in (256, 128, 64, 32, 16, 8):
        if k % r == 0:
            return r
    return k


def _core_scatter_sum(gs, name):
    n = len(gs)

    def body(*refs):
        g_refs, o_refs = refs[:n], refs[n:2 * n]
        send, got, mine = refs[2 * n:3 * n], refs[3 * n:4 * n], refs[4 * n:5 * n]
        lsem, msem, ssem, rsem, osem = refs[5 * n:]
        mx, my, mc = _my_place()
        pairs = [(i, k) for i in range(n) for k in range(4)]
        out_loads = {(i, k): pltpu.make_async_copy(g_refs[i].at[k, 1 - mc], send[i].at[k], lsem.at[i, k])
                     for i, k in pairs}
        own_loads = {(i, k): pltpu.make_async_copy(g_refs[i].at[k, mc], mine[i].at[k], msem.at[i, k])
                     for i, k in pairs}
        for p in pairs:
            out_loads[p].start()
        for p in pairs:
            own_loads[p].start()
        sends = []
        for i in range(n):
            for k in range(4):
                out_loads[i, k].wait()
            cp = pltpu.make_async_remote_copy(
                src_ref=send[i], dst_ref=got[i], send_sem=ssem.at[i], recv_sem=rsem.at[i],
                device_id=(mx, my, 1 - mc), device_id_type=MESH)
            cp.start()
            sends.append(cp)
        stores = []
        for i in range(n):
            for k in range(4):
                own_loads[i, k].wait()
            sends[i].wait_recv()
            rows = g_refs[i].shape[2]
            step = _rows_step(rows)

            def add(r, _, i=i, step=step):
                sl = pl.ds(pl.multiple_of(r * step, step), step)
                for k in range(4):
                    mine[i][k, sl, :] = (mine[i][k, sl, :].astype(F32) + got[i][k, sl, :].astype(F32)).astype(BF16)
                return 0

            lax.fori_loop(0, rows // step, add, 0)
            st = pltpu.make_async_copy(mine[i], o_refs[i], osem.at[i])
            st.start()
            stores.append(st)
        for cp in sends:
            cp.wait_send()
        for st in stores:
            st.wait()

    hbm = pl.BlockSpec(memory_space=pl.ANY)
    blk = [(4,) + tuple(g.shape[2:]) for g in gs]
    bufs = [pltpu.VMEM(b, BF16) for b in blk]
    return pl.pallas_call(
        body, name=name,
        out_shape=[_sds(b, BF16) for b in blk],
        in_specs=[hbm] * n, out_specs=[hbm] * n,
        scratch_shapes=bufs * 3 + [pltpu.SemaphoreType.DMA((n, 4)), pltpu.SemaphoreType.DMA((n, 4)),
                                   pltpu.SemaphoreType.DMA((n,)), pltpu.SemaphoreType.DMA((n,)),
                                   pltpu.SemaphoreType.DMA((n,))],
        compiler_params=pltpu.CompilerParams(vmem_limit_bytes=VMEM_LIMIT),
    )(*gs)


def _all_gather(xs, name):
    by_chip = _chip_exchange(xs, name + "_chips", scatter=False)
    both = _core_gather(by_chip, name + "_cores")
    return [b.reshape((N_DEV,) + tuple(x.shape)) for b, x in zip(both, xs)]


def _ada_partial(c_all, w_ada_loc, b_loc):
    def body(c_ref, w_ref, b_ref, act_ref, mod_ref):
        c = c_ref[...]
        act = c * (1.0 / (1.0 + jnp.exp(-c)))
        act_ref[...] = act
        mod_ref[...] = _dot(act.astype(BF16), w_ref[...].astype(BF16)) + b_ref[...]

    nb, d = c_all.shape
    return pl.pallas_call(
        body, name="ada_partial",
        out_shape=(_sds((nb, d), F32), _sds((nb, w_ada_loc.shape[1]), F32)),
        compiler_params=pltpu.CompilerParams(vmem_limit_bytes=VMEM_LIMIT),
    )(c_all, w_ada_loc, b_loc)


_AFTER = pl.BlockSpec(memory_space=pl.ANY)


def _inproj_fwd(x, mod, ln_g, ln_b, w_in_p, w_uq_p, w_kv, gq, gkv, tc, ts1, ts2, dm, after):
    B, S, D = x.shape
    tm = dm["tm"]
    sbw, qr, kvr, nh = dm["sbw"], dm["qr"], dm["kvr"], dm["nh"]
    o_cq, o_ckv, o_kr = 3 * sbw, 3 * sbw + qr, 3 * sbw + qr + kvr
    qpw = nh * HEAD_PAD

    def body(x_ref, mod_ref, g_ref, b_ref, win_ref, wuq_ref, wkv_ref, gq_ref, gkv_ref, tc_ref, ts1_ref, ts2_ref, _,
             x0_ref, q_ref, k_ref, v_ref, qp_ref, kp_ref, mv_ref, cq_ref, ckv_ref, qn_ref, kvn_ref):
        x0, _, _ = _ln_fwd(x_ref[0], g_ref[...], b_ref[...])
        x0_ref[0] = x0
        mod = mod_ref[0]
        h = (x0 * (1.0 + mod[1:2]) + mod[0:1]).astype(BF16)
        proj = _dot(h, win_ref[...])
        q_ref[0] = (proj[:, 0:sbw] * SB_Q_SCALE).astype(BF16)
        k_ref[0] = proj[:, sbw:2 * sbw].astype(BF16)
        v_ref[0] = proj[:, 2 * sbw:3 * sbw].astype(BF16)
        cq = proj[:, o_cq:o_cq + qr]
        ckv = proj[:, o_ckv:o_ckv + kvr]
        cq_ref[0] = cq
        ckv_ref[0] = ckv
        qn = (cq * lax.rsqrt(jnp.mean(cq * cq, axis=-1, keepdims=True) + RMS_EPS) * gq_ref[...]).astype(BF16)
        kvn = (ckv * lax.rsqrt(jnp.mean(ckv * ckv, axis=-1, keepdims=True) + RMS_EPS) * gkv_ref[...]).astype(BF16)
        qn_ref[0] = qn
        kvn_ref[0] = kvn
        c1, s1, s2 = tc_ref[...], ts1_ref[...], ts2_ref[...]
        c8, s18, s28 = jnp.tile(c1, (1, nh)), jnp.tile(s1, (1, nh)), jnp.tile(s2, (1, nh))
        qp_ref[0] = (_rope(_dot(qn, wuq_ref[...]), c8, s18, s28) * MLA_Q_SCALE).astype(BF16)
        kvo = _dot(kvn, wkv_ref[...])
        kr = pltpu.roll(proj[:, o_kr:o_kr + LANES], 64, 1)
        kr = _rope(kr, c1, s1, s2)
        kp_ref[0] = (kvo[:, 0:qpw] + jnp.tile(kr, (1, nh))).astype(BF16)
        mv_ref[0] = kvo[:, qpw:].astype(BF16)

    tab = pl.BlockSpec((tm, LANES), lambda b, s: (s, 0))
    outs = [(D, F32), (sbw, BF16), (sbw, BF16), (sbw, BF16), (qpw, BF16), (qpw, BF16),
            (nh * MLA_V, BF16), (qr, F32), (kvr, F32), (qr, BF16), (kvr, BF16)]
    return pl.pallas_call(
        body, name="inproj_fwd", grid=(B, S // tm),
        in_specs=[_tok(tm, D), _perb(N_MOD, D), _full(ln_g), _full(ln_b), _full(w_in_p), _full(w_uq_p),
                  _full(w_kv), _full(gq), _full(gkv), tab, tab, tab, _AFTER],
        out_specs=[_tok(tm, w) for w, _ in outs],
        out_shape=[_sds((B, S, w), t) for w, t in outs],
        compiler_params=_cparams(("parallel", "parallel")),
    )(x, mod, ln_g, ln_b, w_in_p, w_uq_p, w_kv, gq, gkv, tc, ts1, ts2, after)


def _neg_abs(x):
    sign = jnp.uint32(0x80000000)
    return lax.bitcast_convert_type(lax.bitcast_convert_type(x, jnp.uint32) | sign, F32)


SB_Q_SCALE = -(SB_HD ** -0.5) * LOG2E
MLA_Q_SCALE = (MLA_NOPE + MLA_ROPE) ** -0.5 * LOG2E


def _log2_keep(zs):
    return jnp.minimum(zs, 0.0) - jnp.log2(1.0 + jnp.exp2(_neg_abs(zs)))


def _split_dot(a, u):
    hi = a.astype(BF16)
    lo = (a - hi.astype(F32)).astype(BF16)
    return _dot(jnp.concatenate([hi, lo], axis=1), jnp.concatenate([u, u], axis=0))


def _tri(n, rel):
    row = lax.broadcasted_iota(jnp.int32, (n, n), 0)
    col = lax.broadcasted_iota(jnp.int32, (n, n), 1)
    return rel(row, col).astype(BF16)


def _running_sum(a, tri, reverse, split, start):
    cs = tri.shape[0]
    n = a.shape[1] // cs
    out = [None] * n
    run = start
    for c in (reversed(range(n)) if reverse else range(n)):
        part = a[:, c * cs:(c + 1) * cs]
        out[c] = (_split_dot(part, tri) if split else _dot(part.astype(BF16), tri)) + run
        run = run + jnp.sum(part, axis=1, keepdims=True)
    return (out[0] if n == 1 else jnp.concatenate(out, axis=1)), run


def _transpose_bf16(a):
    return a.astype(F32).T.astype(BF16)


def _tile_mask(nr, nk, r0, c0, rel):
    row = lax.broadcasted_iota(jnp.int32, (nr, nk), 0) + r0
    col = lax.broadcasted_iota(jnp.int32, (nr, nk), 1) + c0
    return rel(row, col)


def _put_rows(whole, part, r0):
    return part if r0 == 0 else jnp.concatenate([whole[:r0], part], axis=0)


def _diag_tiles(tq, split):
    half = tq // 2
    return [(0, tq, 0, half), (half, half, half, half)] if split else [(0, tq, 0, tq)]


def _sb_fwd(q, k, v, dm):
    B, S, W = q.shape
    tq = dm["tq"]
    nq = S // tq

    def body(q_ref, k_ref, v_ref, y_ref, tot_ref):
        qi = pl.program_id(2)
        q2 = q_ref[0]
        lane = lax.broadcasted_iota(jnp.int32, (tq, LANES), 1)
        qs = jnp.concatenate([jnp.where(lane < SB_HD, q2, 0), jnp.where(lane >= SB_HD, q2, 0)], axis=0).astype(BF16)
        later = _tri(min(tq, CUMSUM_W), lambda a, b: a > b)
        assert tq & (tq - 1) == 0
        strict = _tile_mask(2 * tq, tq, 0, 0, lambda t, s: s < (t & (tq - 1)))

        def block(j, carry, masked):
            acc, run = carry
            off = pl.multiple_of(j * tq, tq)
            zs = _dot_nt(qs, k_ref[0, pl.ds(off, tq), :])
            a = _log2_keep(zs)
            if masked:
                a = jnp.where(strict, a, 0.0)
            a_later, run = _running_sum(a, later, reverse=True, split=True, start=run)
            w = jnp.exp2((a - zs) + a_later)
            if masked:
                w = jnp.where(strict, w, 0.0)
            return acc + _dot(w.astype(BF16), v_ref[0, pl.ds(off, tq), :]), run

        carry = block(qi, (jnp.zeros((2 * tq, LANES), F32), jnp.zeros((2 * tq, 1), F32)), True)
        acc, run = lax.fori_loop(0, qi, lambda jj, c: block(qi - 1 - jj, c, False), carry)
        y_ref[0] = jnp.where(lane < SB_HD, acc[:tq], acc[tq:]).astype(BF16)
        tot_ref[0] = jnp.where(lane < SB_HD, run[:tq], run[tq:])

    qspec = pl.BlockSpec((1, tq, LANES), lambda b, hp, i: (b, i, hp))
    kspec = pl.BlockSpec((1, S, LANES), lambda b, hp, i: (b, 0, hp))
    return pl.pallas_call(
        body, name="sb_fwd", grid=(B, W // LANES, nq),
        in_specs=[qspec, kspec, kspec],
        out_specs=[qspec, qspec],
        out_shape=[_sds((B, S, W), BF16), _sds((B, S, W), F32)],
        compiler_params=_cparams(("parallel", "parallel", "arbitrary")),
    )(q, k, v)


def _sb_bwd(q, k, v, tot, dy, dm, after):
    B, S, W = q.shape
    tq = dm["tq"]
    nq = S // tq

    def body(q_ref, k_ref, v_ref, tot_ref, dy_ref, _, dq_ref, dk_ref, dv_ref, dk_acc, dv_acc):
        qi = pl.program_id(2)

        @pl.when(qi == 0)
        def _():
            dk_acc[...] = jnp.zeros_like(dk_acc)
            dv_acc[...] = jnp.zeros_like(dv_acc)

        q2 = q_ref[0]
        dy2 = dy_ref[0]
        tot2 = tot_ref[0]
        lane = lax.broadcasted_iota(jnp.int32, (tq, LANES), 1)
        in_h = [lane < SB_HD, lane >= SB_HD]
        qh = [jnp.where(m, q2, 0).astype(BF16) for m in in_h]
        dyh = [jnp.where(m, dy2, 0).astype(BF16) for m in in_h]
        q_t = [_transpose_bf16(a) for a in qh]
        dy_t = [_transpose_bf16(a) for a in dyh]
        toth = [tot2[:, 0:1], tot2[:, SB_HD:SB_HD + 1]]

        def tile(j, carry, r0, nr, c0, nk, masked):
            off = pl.multiple_of(j * tq + c0, math.gcd(tq, c0))
            k2 = k_ref[0, pl.ds(off, nk), :]
            v2 = v_ref[0, pl.ds(off, nk), :]
            upto = _tri(min(nk, CUMSUM_W), lambda a, b: a <= b)
            before = _tri(min(nk, CUMSUM_W), lambda a, b: a < b)
            strict = _tile_mask(nr, nk, r0, c0, lambda t, s: s < t) if masked else None
            rows = slice(r0, r0 + nr)
            new = []
            dk_blk = jnp.zeros((LANES, nk), F32)
            dv_blk = jnp.zeros((LANES, nk), F32)
            for h in range(2):
                dq, pa, pg = carry[3 * h][rows], carry[3 * h + 1][rows], carry[3 * h + 2][rows]
                zs = _dot_nt(qh[h][rows], k2)
                a = _log2_keep(zs)
                if masked:
                    a = jnp.where(strict, a, 0.0)
                a_upto, pa = _running_sum(a, upto, reverse=False, split=True, start=pa)
                w = jnp.exp2((a - zs) - a_upto)
                if masked:
                    w = jnp.where(strict, w, 0.0)
                g = _dot_nt(dyh[h][rows], v2) * w
                g_before, pg = _running_sum(g, before, reverse=False, split=False, start=pg)
                dz = (g + g_before) * jnp.exp2(a) - g_before
                if masked:
                    dz = jnp.where(strict, dz, 0.0)
                dzb = dz.astype(BF16)
                dv_blk = dv_blk + _dot(dy_t[h][:, rows], w.astype(BF16))
                dk_blk = dk_blk + _dot(q_t[h][:, rows], dzb)
                new += [_put_rows(carry[3 * h], dq + _dot(dzb, k2), r0), _put_rows(carry[3 * h + 1], pa, r0),
                        _put_rows(carry[3 * h + 2], pg, r0)]
            dk_acc[j, :, c0:c0 + nk] += dk_blk
            dv_acc[j, :, c0:c0 + nk] += dv_blk
            return tuple(new)

        zero = jnp.zeros((tq, LANES), F32)
        zrun = jnp.zeros((tq, 1), F32)
        carry = lax.fori_loop(0, qi, lambda j, c: tile(j, c, 0, tq, 0, tq, False),
                              (zero, -toth[0], zrun, zero, -toth[1], zrun))
        for r0, nr, c0, nk in _diag_tiles(tq, False):
            carry = tile(qi, carry, r0, nr, c0, nk, True)
        dq_ref[0] = (jnp.where(in_h[0], carry[0], carry[3]) * (SB_HD ** -0.5)).astype(BF16)

        @pl.when(qi == nq - 1)
        def _():
            for jb in range(nq):
                dk_ref[0, jb * tq:(jb + 1) * tq, :] = (dk_acc[jb].T * (-1.0 / LOG2E)).astype(BF16)
                dv_ref[0, jb * tq:(jb + 1) * tq, :] = dv_acc[jb].T.astype(BF16)

    qspec = pl.BlockSpec((1, tq, LANES), lambda b, hp, i: (b, i, hp))
    kspec = pl.BlockSpec((1, S, LANES), lambda b, hp, i: (b, 0, hp))
    return pl.pallas_call(
        body, name="sb_bwd", grid=(B, W // LANES, nq),
        in_specs=[qspec, kspec, kspec, qspec, qspec, _AFTER],
        out_specs=[qspec, kspec, kspec],
        out_shape=[_sds((B, S, W), BF16)] * 3,
        scratch_shapes=[pltpu.VMEM((nq, LANES, tq), F32), pltpu.VMEM((nq, LANES, tq), F32)],
        compiler_params=_cparams(("parallel", "parallel", "arbitrary")),
    )(q, k, v, tot, dy, after)


def _same_or_earlier_chunk(row, col):
    return lax.shift_right_logical(col, 6) <= lax.shift_right_logical(row, 6)


def _mla_fwd(qp, kp, mv, dm, after):
    B, S, QW = qp.shape
    VW = mv.shape[2]
    tq = dm["tq"]
    nq = S // tq
    assert CHUNK == 64

    def body(q_ref, k_ref, v_ref, _, y_ref, lse_ref):
        qi = pl.program_id(2)
        q2 = q_ref[0]
        lane = lax.broadcasted_iota(jnp.int32, (tq, LANES), 1)

        def tile(j, carry, r0, nr, c0, nk, masked):
            off = pl.multiple_of(j * tq + c0, math.gcd(tq, c0))
            v2 = v_ref[0, pl.ds(off, nk), :]
            allowed = _tile_mask(nr, nk, r0, c0, _same_or_earlier_chunk) if masked else None
            rows = slice(r0, r0 + nr)
            heads = range(2)
            sl = [slice(h * HEAD_PAD, (h + 1) * HEAD_PAD) for h in heads]
            m_old = [carry[3 * h + 1][rows] for h in heads]
            s = [_dot_nt(q2[rows, sl[h]], k_ref[0, pl.ds(off, nk), sl[h]]) for h in heads]
            if masked:
                s = [jnp.where(allowed, s[h], -1e30) for h in heads]
            m_new = [jnp.maximum(m_old[h], jnp.max(s[h], axis=1, keepdims=True)) for h in heads]
            alpha = [jnp.exp2(m_old[h] - m_new[h]) for h in heads]
            p = [jnp.exp2(s[h] - m_new[h]) for h in heads]
            acc = [alpha[h] * carry[3 * h][rows] + _dot(p[h].astype(BF16), v2) for h in heads]
            l = [alpha[h] * carry[3 * h + 2][rows] + jnp.sum(p[h], axis=1, keepdims=True) for h in heads]
            out = []
            for h in heads:
                out += [_put_rows(carry[3 * h], acc[h], r0), _put_rows(carry[3 * h + 1], m_new[h], r0),
                        _put_rows(carry[3 * h + 2], l[h], r0)]
            return tuple(out)

        zero = jnp.zeros((tq, LANES), F32)
        m0 = jnp.full((tq, 1), -1e30, F32)
        l0 = jnp.zeros((tq, 1), F32)
        carry = (zero, m0, l0, zero, m0, l0)
        for r0, nr, c0, nk in _diag_tiles(tq, False):
            carry = tile(qi, carry, r0, nr, c0, nk, True)
        carry = lax.fori_loop(0, qi, lambda j, c: tile(j, c, 0, tq, 0, tq, False), carry)
        y0 = carry[0] / carry[2]
        y1 = carry[3] / carry[5]
        y_ref[0] = jnp.where(lane < MLA_V, y0, y1).astype(BF16)
        lse_ref[0] = jnp.where(lane < MLA_V, carry[1] + jnp.log2(carry[2]), carry[4] + jnp.log2(carry[5]))

    qspec = pl.BlockSpec((1, tq, 2 * HEAD_PAD), lambda b, hp, i: (b, i, hp))
    kspec = pl.BlockSpec((1, S, 2 * HEAD_PAD), lambda b, hp, i: (b, 0, hp))
    vspec = pl.BlockSpec((1, S, LANES), lambda b, hp, i: (b, 0, hp))
    yspec = pl.BlockSpec((1, tq, LANES), lambda b, hp, i: (b, i, hp))
    return pl.pallas_call(
        body, name="mla_fwd", grid=(B, VW // LANES, nq),
        in_specs=[qspec, kspec, vspec, _AFTER],
        out_specs=[yspec, yspec],
        out_shape=[_sds((B, S, VW), BF16), _sds((B, S, VW), F32)],
        compiler_params=_cparams(("parallel", "parallel", "arbitrary")),
    )(qp, kp, mv, after)


def _mla_bwd(qp, kp, mv, y, lse, dy, dm, after):
    B, S, QW = qp.shape
    VW = mv.shape[2]
    tq = dm["tq"]
    nq = S // tq
    scale = (MLA_NOPE + MLA_ROPE) ** -0.5

    def body(q_ref, k_ref, v_ref, y_ref, lse_ref, dy_ref, _, dq_ref, dk_ref, dv_ref, dk_acc, dv_acc):
        qi = pl.program_id(2)

        @pl.when(qi == 0)
        def _():
            dk_acc[...] = jnp.zeros_like(dk_acc)
            dv_acc[...] = jnp.zeros_like(dv_acc)

        q2 = q_ref[0]
        dy2 = dy_ref[0]
        lse2 = lse_ref[0]
        lane = lax.broadcasted_iota(jnp.int32, (tq, LANES), 1)
        in_h = [lane < MLA_V, lane >= MLA_V]
        prod = dy2.astype(F32) * y_ref[0].astype(F32)
        delta = [jnp.sum(jnp.where(m, prod, 0.0), axis=1, keepdims=True) for m in in_h]
        dyh = [jnp.where(m, dy2, 0).astype(BF16) for m in in_h]
        lseh = [lse2[:, 0:1], lse2[:, MLA_V:MLA_V + 1]]
        q_t = _transpose_bf16(q2)
        dy_t = [_transpose_bf16(a) for a in dyh]

        def tile(j, carry, r0, nr, c0, nk, masked):
            off = pl.multiple_of(j * tq + c0, math.gcd(tq, c0))
            v2 = v_ref[0, pl.ds(off, nk), :]
            allowed = _tile_mask(nr, nk, r0, c0, _same_or_earlier_chunk) if masked else None
            rows = slice(r0, r0 + nr)
            keys = slice(c0, c0 + nk)
            heads = range(2)
            sl = [slice(h * HEAD_PAD, (h + 1) * HEAD_PAD) for h in heads]
            qhh = [q2[rows, sl[h]] for h in heads]
            dyr = [dyh[h][rows] for h in heads]
            kh = [k_ref[0, pl.ds(off, nk), sl[h]] for h in heads]
            s = [_dot_nt(qhh[h], kh[h]) for h in heads]
            dp = [_dot_nt(dyr[h], v2) for h in heads]
            if masked:
                s = [jnp.where(allowed, s[h], -1e30) for h in heads]
            p = [jnp.exp2(s[h] - lseh[h][rows]) for h in heads]
            dv_acc[j, :, keys] += (_dot(dy_t[0][:, rows], p[0].astype(BF16))
                                   + _dot(dy_t[1][:, rows], p[1].astype(BF16)))
            ds = [(p[h] * (dp[h] - delta[h][rows])).astype(BF16) for h in heads]
            for h in heads:
                dk_acc[j, sl[h], keys] += _dot(q_t[sl[h], rows], ds[h])
            return tuple(_put_rows(carry[h], carry[h][rows] + _dot(ds[h], kh[h]), r0) for h in heads)

        zero = jnp.zeros((tq, HEAD_PAD), F32)
        carry = lax.fori_loop(0, qi, lambda j, c: tile(j, c, 0, tq, 0, tq, False), (zero, zero))
        for r0, nr, c0, nk in _diag_tiles(tq, True):
            carry = tile(qi, carry, r0, nr, c0, nk, True)
        dq_ref[0] = (jnp.concatenate([carry[0], carry[1]], axis=1) * scale).astype(BF16)

        @pl.when(qi == nq - 1)
        def _():
            for jb in range(nq):
                dk_ref[0, jb * tq:(jb + 1) * tq, :] = (dk_acc[jb].T * (1.0 / LOG2E)).astype(BF16)
                dv_ref[0, jb * tq:(jb + 1) * tq, :] = dv_acc[jb].T.astype(BF16)

    qspec = pl.BlockSpec((1, tq, 2 * HEAD_PAD), lambda b, hp, i: (b, i, hp))
    kspec = pl.BlockSpec((1, S, 2 * HEAD_PAD), lambda b, hp, i: (b, 0, hp))
    vspec = pl.BlockSpec((1, S, LANES), lambda b, hp, i: (b, 0, hp))
    yspec = pl.BlockSpec((1, tq, LANES), lambda b, hp, i: (b, i, hp))
    return pl.pallas_call(
        body, name="mla_bwd", grid=(B, VW // LANES, nq),
        in_specs=[qspec, kspec, vspec, yspec, yspec, yspec, _AFTER],
        out_specs=[qspec, kspec, vspec],
        out_shape=[_sds((B, S, QW), BF16), _sds((B, S, QW), BF16), _sds((B, S, VW), BF16)],
        scratch_shapes=[pltpu.VMEM((nq, 2 * HEAD_PAD, tq), F32), pltpu.VMEM((nq, LANES, tq), F32)],
        compiler_params=_cparams(("parallel", "parallel", "arbitrary")),
    )(qp, kp, mv, y, lse, dy, after)


def _stat_specs(B, D):
    specs = [pl.BlockSpec((1, 8, D), lambda b, s: (b, 0, 0)), pl.BlockSpec((8, D), lambda b, s: (0, 0))]
    shapes = [_sds((B, 8, D), F32), _sds((8, D), F32)]
    return specs, shapes


def _stat_init(bst_ref, wst_ref):
    @pl.when(pl.program_id(1) == 0)
    def _():
        bst_ref[...] = jnp.zeros_like(bst_ref)

    @pl.when((pl.program_id(0) == 0) & (pl.program_id(1) == 0))
    def _():
        wst_ref[...] = jnp.zeros_like(wst_ref)


def _mlp_fwd(sb_y, mla_y, x0, mod, target, w_o, w_up, w_down, ln1_g, ln1_b, ln_g, ln_b, dm):
    B, S, D = x0.shape
    tm = dm["tm"]
    sbw = sb_y.shape[2]
    nck, _, ck = w_up.shape
    dff = nck * ck

    def body(ya_ref, yb_ref, x0_ref, mod_ref, t_ref, wo_ref, wu_ref, wd_ref, g1_ref, b1_ref, g_ref, b_ref,
             mix_ref, x1_ref, h2_ref, u_ref, dr_ref, bst_ref, wst_ref):
        _stat_init(bst_ref, wst_ref)
        mod = mod_ref[0]
        mix = _dot(ya_ref[0], wo_ref[0:sbw, :]) + _dot(yb_ref[0], wo_ref[sbw:, :])
        mix_ref[0] = mix
        x1, _, _ = _ln_fwd(ALPHA * x0_ref[0] + (1.0 + mod[2:3]) * mix, g1_ref[...], b1_ref[...])
        x1_ref[0] = x1
        h2 = (x1 * (1.0 + mod[4:5]) + mod[3:4]).astype(BF16)
        h2_ref[0] = h2
        g = g_ref[...]
        ff = jnp.zeros((tm, D), F32)
        for c in range(nck):
            u = _dot(h2, wu_ref[c])
            u_ref[0, :, c * ck:(c + 1) * ck] = u.astype(BF16)
            act = jnp.square(jnp.maximum(u, 0.0)).astype(BF16)
            ff = ff + _dot(act, wd_ref[c])
        x2, xhat, rstd = _ln_fwd(ALPHA * x1 + (1.0 + mod[5:6]) * ff, g, b_ref[...])
        err = x2 - t_ref[0]
        dy = err * (1.0 / D)
        dr = _ln_bwd(dy, xhat, rstd, g)
        dr_ref[0] = dr
        bst_ref[0, 0:1, :] += _colsum(dr * ff)
        wst_ref[0:1, :] += _colsum(dy * xhat)
        wst_ref[1:2, :] += _colsum(dy)
        wst_ref[2:3, :] += _colsum(err * err) * (0.5 / D)

    sspecs, sshapes = _stat_specs(B, D)
    return pl.pallas_call(
        body, name="mlp_fwd", grid=(B, S // tm),
        in_specs=[_tok(tm, sbw), _tok(tm, mla_y.shape[2]), _tok(tm, D), _perb(N_MOD, D), _tok(tm, D),
                  _full(w_o), _full(w_up), _full(w_down), _full(ln1_g), _full(ln1_b), _full(ln_g), _full(ln_b)],
        out_specs=[_tok(tm, D), _tok(tm, D), _tok(tm, D), _tok(tm, dff), _tok(tm, D)] + sspecs,
        out_shape=[_sds((B, S, D), F32), _sds((B, S, D), F32), _sds((B, S, D), BF16), _sds((B, S, dff), BF16),
                   _sds((B, S, D), F32)] + sshapes,
        compiler_params=_cparams(("arbitrary", "arbitrary")),
    )(sb_y, mla_y, x0, mod, target, w_o, w_up, w_down, ln1_g, ln1_b, ln_g, ln_b)


def _mlp_bwd(dr2, u, x1, x0, mix, sb_y, mla_y, mod, w_up, w_down, w_o, ln_g, dm):
    B, S, D = x1.shape
    tm = dm["tm_small"]
    sbw = dm["sbw"]
    nck, _, ck = w_up.shape
    dff = nck * ck

    def body(dr_ref, u_ref, x1_ref, x0_ref, mix_ref, ya_ref, yb_ref, mod_ref, wu_ref, wd_ref, wo_ref, g_ref,
             du_ref, dff_ref, dx0_ref, dya_ref, dyb_ref, go_ref, bst_ref, wst_ref):
        _stat_init(bst_ref, wst_ref)

        @pl.when((pl.program_id(0) == 0) & (pl.program_id(1) == 0))
        def _():
            go_ref[...] = jnp.zeros_like(go_ref)

        mod = mod_ref[0]
        dr2 = dr_ref[0]
        dffv = ((1.0 + mod[5:6]) * dr2).astype(BF16)
        dff_ref[0] = dffv
        dh2 = jnp.zeros((tm, D), F32)
        for c in range(nck):
            sl = slice(c * ck, (c + 1) * ck)
            da = _dot_nt(dffv, wd_ref[c])
            du = (da * (2.0 * jnp.maximum(u_ref[0, :, sl].astype(F32), 0.0))).astype(BF16)
            du_ref[0, :, sl] = du
            dh2 = dh2 + _dot_nt(du, wu_ref[c])
        x1 = x1_ref[0]
        dx1 = ALPHA * dr2 + dh2 * (1.0 + mod[4:5])
        bst_ref[0, 0:1, :] += _colsum(dh2 * x1)
        bst_ref[0, 1:2, :] += _colsum(dh2)
        mix = mix_ref[0]
        g = g_ref[...]
        _, xhat, rstd = _ln_fwd(ALPHA * x0_ref[0] + (1.0 + mod[2:3]) * mix, g, 0.0)
        dr1 = _ln_bwd(dx1, xhat, rstd, g)
        wst_ref[0:1, :] += _colsum(dx1 * xhat)
        wst_ref[1:2, :] += _colsum(dx1)
        bst_ref[0, 2:3, :] += _colsum(dr1 * mix)
        dx0_ref[0] = ALPHA * dr1
        dmix = ((1.0 + mod[2:3]) * dr1).astype(BF16)
        dya_ref[0] = _dot_nt(dmix, wo_ref[0:sbw, :]).astype(BF16)
        dyb_ref[0] = _dot_nt(dmix, wo_ref[sbw:, :]).astype(BF16)
        go_ref[0:sbw, :] += _dot_tn(ya_ref[0], dmix)
        go_ref[sbw:, :] += _dot_tn(yb_ref[0], dmix)

    sspecs, sshapes = _stat_specs(B, D)
    wa, wb = sbw, w_o.shape[0] - sbw
    return pl.pallas_call(
        body, name="mlp_bwd", grid=(B, S // tm),
        in_specs=[_tok(tm, D), _tok(tm, dff), _tok(tm, D), _tok(tm, D), _tok(tm, D), _tok(tm, wa), _tok(tm, wb),
                  _perb(N_MOD, D), _full(w_up), _full(w_down), _full(w_o), _full(ln_g)],
        out_specs=[_tok(tm, dff), _tok(tm, D), _tok(tm, D), _tok(tm, wa), _tok(tm, wb),
                   pl.BlockSpec(w_o.shape, lambda b, s: (0, 0))] + sspecs,
        out_shape=[_sds((B, S, dff), BF16), _sds((B, S, D), BF16), _sds((B, S, D), F32),
                   _sds((B, S, wa), BF16), _sds((B, S, wb), BF16), _sds(w_o.shape, F32)] + sshapes,
        compiler_params=_cparams(("arbitrary", "arbitrary")),
    )(dr2, u, x1, x0, mix, sb_y, mla_y, mod, w_up, w_down, w_o, ln_g)


def _inproj_bwd(x, x0, dx0a, mod, ln_g, dq, dk, dv, dqp, dkp, dmv, cq, ckv, qn, kvn, w_in_p, w_uq_p, w_kv, gq, gkv,
                tc, ts1, ts2, dm):
    B, S, D = x.shape
    tm = dm["tm"]
    sbw, qr, kvr, nh = dm["sbw"], dm["qr"], dm["kvr"], dm["nh"]
    qpw = nh * HEAD_PAD
    dinp = w_in_p.shape[1]
    kvw = w_kv.shape[1]

    def body(x_ref, x0_ref, dx0a_ref, mod_ref, g_ref, dq_ref, dk_ref, dv_ref, dqp_ref, dkp_ref, dmv_ref,
             cq_ref, ckv_ref, qn_ref, kvn_ref, win_ref, wuq_ref, wkv_ref, gq_ref, gkv_ref, tc_ref, ts1_ref, ts2_ref,
             gx_ref, gin_ref, guq_ref, gwkv_ref, bst_ref, wst_ref):
        _stat_init(bst_ref, wst_ref)

        @pl.when((pl.program_id(0) == 0) & (pl.program_id(1) == 0))
        def _():
            gin_ref[...] = jnp.zeros_like(gin_ref)
            guq_ref[...] = jnp.zeros_like(guq_ref)
            gwkv_ref[...] = jnp.zeros_like(gwkv_ref)

        mod = mod_ref[0]
        c1, s1, s2 = tc_ref[...], ts1_ref[...], ts2_ref[...]
        c8, s18, s28 = jnp.tile(c1, (1, nh)), jnp.tile(s1, (1, nh)), jnp.tile(s2, (1, nh))
        dqpre = _rope_t(dqp_ref[0].astype(F32), c8, s18, s28).astype(BF16)
        guq_ref[...] += _dot_tn(qn_ref[0], dqpre)
        gq = gq_ref[...]
        cq = cq_ref[0]
        rq = lax.rsqrt(jnp.mean(cq * cq, axis=-1, keepdims=True) + RMS_EPS)
        dqn = _dot_nt(dqpre, wuq_ref[...])
        wst_ref[4:5, 0:qr] += _colsum(dqn * cq * rq)
        dqg = dqn * gq
        dcq = rq * dqg - cq * (rq * rq * rq) * jnp.mean(dqg * cq, axis=-1, keepdims=True)

        dkpre = _rope_t(dkp_ref[0].astype(F32), c8, s18, s28)
        dkr = dkpre[:, 0:HEAD_PAD]
        for h in range(1, nh):
            dkr = dkr + dkpre[:, h * HEAD_PAD:(h + 1) * HEAD_PAD]
        lane = lax.broadcasted_iota(jnp.int32, (tm, LANES), 1)
        dkr = jnp.where((lane >= MLA_NOPE) & (lane < MLA_NOPE + MLA_ROPE), dkr, 0.0)
        dkr = pltpu.roll(dkr, LANES - MLA_NOPE, 1)
        dkvo = jnp.concatenate([dkpre.astype(BF16), dmv_ref[0]], axis=1)
        gwkv_ref[...] += _dot_tn(kvn_ref[0], dkvo)
        gkv = gkv_ref[...]
        ckv = ckv_ref[0]
        rkv = lax.rsqrt(jnp.mean(ckv * ckv, axis=-1, keepdims=True) + RMS_EPS)
        dkvn = _dot_nt(dkvo, wkv_ref[...])
        wst_ref[5:6, 0:kvr] += _colsum(dkvn * ckv * rkv)
        dkg = dkvn * gkv
        dckv = rkv * dkg - ckv * (rkv * rkv * rkv) * jnp.mean(dkg * ckv, axis=-1, keepdims=True)

        dproj = jnp.concatenate([dq_ref[0], dk_ref[0], dv_ref[0], dcq.astype(BF16), dckv.astype(BF16),
                                 dkr.astype(BF16)], axis=1)
        dh = _dot_nt(dproj, win_ref[...])
        x0 = x0_ref[0]
        gin_ref[...] += _dot_tn((x0 * (1.0 + mod[1:2]) + mod[0:1]).astype(BF16), dproj)
        dx0 = dx0a_ref[0] + dh * (1.0 + mod[1:2])
        bst_ref[0, 0:1, :] += _colsum(dh * x0)
        bst_ref[0, 1:2, :] += _colsum(dh)
        g = g_ref[...]
        _, xhat, rstd = _ln_fwd(x_ref[0], g, 0.0)
        gx_ref[0] = _ln_bwd(dx0, xhat, rstd, g)
        wst_ref[0:1, :] += _colsum(dx0 * xhat)
        wst_ref[1:2, :] += _colsum(dx0)

    tab = pl.BlockSpec((tm, LANES), lambda b, s: (s, 0))
    sspecs, sshapes = _stat_specs(B, D)
    return pl.pallas_call(
        body, name="inproj_bwd", grid=(B, S // tm),
        in_specs=[_tok(tm, D), _tok(tm, D), _tok(tm, D), _perb(N_MOD, D), _full(ln_g),
                  _tok(tm, sbw), _tok(tm, sbw), _tok(tm, sbw), _tok(tm, qpw), _tok(tm, qpw), _tok(tm, nh * MLA_V),
                  _tok(tm, qr), _tok(tm, kvr), _tok(tm, qr), _tok(tm, kvr),
                  _full(w_in_p), _full(w_uq_p), _full(w_kv), _full(gq), _full(gkv), tab, tab, tab],
        out_specs=[_tok(tm, D), pl.BlockSpec((D, dinp), lambda b, s: (0, 0)),
                   pl.BlockSpec((qr, qpw), lambda b, s: (0, 0)), pl.BlockSpec((kvr, kvw), lambda b, s: (0, 0))] + sspecs,
        out_shape=[_sds((B, S, D), F32), _sds((D, dinp), F32), _sds((qr, qpw), F32),
                   _sds((kvr, kvw), F32)] + sshapes,
        compiler_params=_cparams(("arbitrary", "arbitrary")),
    )(x, x0, dx0a, mod, ln_g, dq, dk, dv, dqp, dkp, dmv, cq, ckv, qn, kvn, w_in_p, w_uq_p, w_kv, gq, gkv,
      tc, ts1, ts2)


def _tile_of(n, cap):
    if n <= cap:
        return n
    best = n
    for t in range(LANES, cap + 1, LANES):
        if n % t == 0:
            best = t
    return best


def _mm_tn(a, g, name, after, relu_sq=False, out_dtype=F32, col_blocks=None):
    T, K = a.shape
    N = g.shape[1]
    tt = 1024 if T % 1024 == 0 else (512 if T % 512 == 0 else T)
    tk = _tile_of(K, 1024)
    tn = _tile_of(N, 1280)
    nt = T // tt
    bw = N // col_blocks if col_blocks else tn
    assert tn % bw == 0

    def body(a_ref, g_ref, _, o_ref, acc_ref):
        @pl.when(pl.program_id(2) == 0)
        def _():
            acc_ref[...] = jnp.zeros_like(acc_ref)

        av = a_ref[...]
        if relu_sq:
            av = jnp.square(jnp.maximum(av.astype(F32), 0.0)).astype(BF16)
        acc_ref[...] += _dot_tn(av, g_ref[...])

        @pl.when(pl.program_id(2) == nt - 1)
        def _():
            if col_blocks:
                for c in range(tn // bw):
                    o_ref[c] = acc_ref[:, c * bw:(c + 1) * bw].astype(out_dtype)
            else:
                o_ref[...] = acc_ref[...].astype(out_dtype)

    if col_blocks:
        out_spec = pl.BlockSpec((tn // bw, tk, bw), lambda i, j, t: (j, i, 0))
        out_shape = _sds((col_blocks, K, bw), out_dtype)
    else:
        out_spec = pl.BlockSpec((tk, tn), lambda i, j, t: (i, j))
        out_shape = _sds((K, N), out_dtype)
    return pl.pallas_call(
        body, name=name, grid=(K // tk, N // tn, nt),
        in_specs=[pl.BlockSpec((tt, tk), lambda i, j, t: (t, i)), pl.BlockSpec((tt, tn), lambda i, j, t: (t, j)),
                  _AFTER],
        out_specs=out_spec, out_shape=out_shape,
        scratch_shapes=[pltpu.VMEM((tk, tn), F32)],
        compiler_params=_cparams(("parallel", "parallel", "arbitrary")),
    )(a, g, after)


def _reduce_adamw(parts, w, m, v, name):
    P, K, N = parts.shape
    tr = 256 if K % 256 == 0 else K

    def body(p_ref, w_ref, m_ref, v_ref, g_ref, d_ref, nm_ref, nv_ref):
        g = p_ref[0].astype(F32)
        for k in range(1, P):
            g = g + p_ref[k].astype(F32)
        g_ref[0] = g
        d_ref[0], nm_ref[0], nv_ref[0] = _adamw(w_ref[0], g, m_ref[0], v_ref[0])

    spec = pl.BlockSpec((1, tr, N), lambda r: (0, r, 0))
    return pl.pallas_call(
        body, name=name, grid=(K // tr,),
        in_specs=[pl.BlockSpec((P, tr, N), lambda r: (0, r, 0)), spec, spec, spec],
        out_specs=[spec] * 4, out_shape=[_sds((1, K, N), F32)] * 4,
        compiler_params=_cparams(("parallel",)),
    )(parts, w, m, v)


def _finish(sm, dmod_all, dmod_my, cact_all, p_small, m_small, v_small, b_ada, m_b, v_b, w_ada, m_w, v_w):
    n0 = p_small.shape[1]
    n1 = sm.shape[1]
    d = cact_all.shape[1]

    def body(sm_ref, dma_ref, dmm_ref, ca_ref, p_ref, pm_ref, pv_ref, b_ref, bm_ref, bv_ref, w_ref, wm_ref, wv_ref,
             gs_ref, ds_ref, ms_ref, vs_ref, gb_ref, db_ref, mb_ref, vb_ref, gw_ref, dw_ref, mw_ref, vw_ref,
             loss_ref):
        gs = sm_ref[0:1, :]
        for k in range(1, N_DEV):
            gs = gs + sm_ref[k:k + 1, :]
        gs_ref[...] = gs
        ds_ref[...], ms_ref[...], vs_ref[...] = _adamw(p_ref[...], gs[:, 0:n0], pm_ref[...], pv_ref[...])
        loss_ref[...] = jnp.zeros((1, LANES), F32) + jnp.sum(gs[:, n1 - d:n1])
        gb = jnp.sum(dma_ref[...], axis=0, keepdims=True)
        gb_ref[...] = gb
        db_ref[...], mb_ref[...], vb_ref[...] = _adamw(b_ref[...], gb, bm_ref[...], bv_ref[...])
        gw = _dot_tn(ca_ref[...].astype(BF16), dmm_ref[...].astype(BF16))
        gw_ref[...] = gw
        dw_ref[...], mw_ref[...], vw_ref[...] = _adamw(w_ref[...], gw, wm_ref[...], wv_ref[...])

    s0 = _sds(p_small.shape, F32)
    sb = _sds(b_ada.shape, F32)
    sw = _sds(w_ada.shape, F32)
    return pl.pallas_call(
        body, name="finish_small",
        out_shape=[_sds((1, n1), F32), s0, s0, s0, sb, sb, sb, sb, sw, sw, sw, sw,
                   _sds((1, LANES), F32)],
        compiler_params=pltpu.CompilerParams(vmem_limit_bytes=VMEM_LIMIT),
    )(sm, dmod_all, dmod_my, cact_all, p_small, m_small, v_small, b_ada, m_b, v_b, w_ada, m_w, v_w)


def _pack(arrs, dtype, width):
    flat = jnp.concatenate([a.astype(dtype).reshape(-1) for a in arrs])
    rows = -(-flat.shape[0] // (256 * width)) * 256
    return jnp.pad(flat, (0, rows * width - flat.shape[0])).reshape(rows, width)


def _unpack(slab, shapes):
    flat = slab.reshape(-1)
    out, o = [], 0
    for s in shapes:
        n = math.prod(s)
        out.append(flat[o:o + n].reshape(s))
        o += n
    return out


def _rope_tables(S):
    inv_freq = 1.0 / (ROPE_BASE ** (jnp.arange(0, MLA_ROPE, 2, dtype=F32) / MLA_ROPE))
    ang = jnp.arange(S, dtype=F32)[:, None] * inv_freq[None, :]
    cos, sin = jnp.cos(ang), jnp.sin(ang)
    one = jnp.ones((S, MLA_NOPE), F32)
    z16 = jnp.zeros((S, 16), F32)
    z32 = jnp.zeros((S, 32), F32)
    z64 = jnp.zeros((S, MLA_NOPE), F32)
    tc = jnp.concatenate([one, cos, cos, jnp.ones((S, 32), F32)], axis=1)
    ts1 = jnp.concatenate([z64, -sin, z16, z32], axis=1)
    ts2 = jnp.concatenate([z64, z16, sin, z32], axis=1)
    return tc, ts1, ts2


def kernel(x, c, ln_in_g, ln_in_b, w_ada, b_ada, w_in, q_norm_g, kv_norm_g, w_uq, w_ukv, w_o, ln1_g, ln1_b, w_up, w_down, ln2_g, ln2_b, loss_target, m_ln_in_g, m_ln_in_b, m_w_ada, m_b_ada, m_w_in, m_q_norm_g, m_kv_norm_g, m_w_uq, m_w_ukv, m_w_o, m_ln1_g, m_ln1_b, m_w_up, m_w_down, m_ln2_g, m_ln2_b, v_ln_in_g, v_ln_in_b, v_w_ada, v_b_ada, v_w_in, v_q_norm_g, v_kv_norm_g, v_w_uq, v_w_ukv, v_w_o, v_ln1_g, v_ln1_b, v_w_up, v_w_down, v_ln2_g, v_ln2_b):
    B, S, D = x.shape
    sbw = D // 2
    mlw = D - sbw
    nh = mlw // MLA_V
    qr = w_uq.shape[1]
    kvr = w_ukv.shape[1]
    qk = MLA_NOPE + MLA_ROPE
    dff = w_up.shape[2] * N_DEV
    din = w_in.shape[2] * N_DEV
    tm = 512 if S % 512 == 0 else S
    tq = min(512, S // 2)
    dm = dict(tm=tm, tm_small=min(tm, 256), tq=tq, sbw=sbw, qr=qr, kvr=kvr, nh=nh)
    dev =4 * lax.axis_index("x") + 2 * lax.axis_index("y") + lax.axis_index("c")

    big = [w_in, w_uq, w_ukv, w_o, w_up, w_down]
    first_w, first_token = _chip_exchange_start([a[0].astype(BF16) for a in big[:3]], "gather_w_first_start",
                                                scatter=False, after=c)

    nada = w_ada.shape[2]
    c_all = _all_gather([c + first_token[0, 0]], "gather_c")[0].reshape(N_DEV * B, D)
    b_loc = lax.dynamic_slice(b_ada, (0, dev * nada), (1, nada))
    cact_all, mod_part = _ada_partial(c_all, w_ada[0], b_loc)
    mod_all = _all_gather([mod_part], "gather_mod")[0]
    mod = lax.dynamic_slice(mod_all, (0, dev * B, 0), (N_DEV, B, nada))
    mod = jnp.swapaxes(mod, 0, 1).reshape(B, N_MOD, D)

    first_by_chip = _chip_exchange_wait(first_w, mod_all, "gather_w_first_wait")
    w_in8, w_uq8, w_ukv8 = [b.reshape((N_DEV,) + b.shape[2:]) for b in _core_gather(first_by_chip, "gather_w_first_cores")]
    late_w, late_token = _chip_exchange_start([a[0].astype(BF16) for a in big[3:]], "gather_w_late_start",
                                              scatter=False, after=w_in8, everyone=True)
    cols = lambda a8: jnp.swapaxes(a8, 0, 1).reshape(a8.shape[1], N_DEV * a8.shape[2])
    w_in_p = jnp.pad(cols(w_in8), ((0, 0), (0, LANES - MLA_ROPE)))
    zpad = jnp.zeros((qr, nh, HEAD_PAD - qk), BF16)
    w_uq_p = jnp.concatenate([cols(w_uq8).reshape(qr, nh, qk), zpad], axis=2).reshape(qr, nh * HEAD_PAD)
    w_ukv_f = cols(w_ukv8)
    w_uk = w_ukv_f[:, :nh * MLA_NOPE].reshape(kvr, nh, MLA_NOPE)
    w_uk_p = jnp.concatenate([w_uk, jnp.zeros((kvr, nh, HEAD_PAD - MLA_NOPE), BF16)], axis=2)
    w_kv = jnp.concatenate([w_uk_p.reshape(kvr, nh * HEAD_PAD), w_ukv_f[:, nh * MLA_NOPE:]], axis=1)

    tc, ts1, ts2 = _rope_tables(S)
    g_in, b_in = ln_in_g.reshape(1, D), ln_in_b.reshape(1, D)
    (x0, sq, sk, sv, qp, kp, mv, cq, ckv, qn, kvn) = _inproj_fwd(
        x, mod, g_in, b_in, w_in_p, w_uq_p, w_kv, q_norm_g, kv_norm_g, tc, ts1, ts2, dm, late_token)
    sb_y, sb_tot = _sb_fwd(sq, sk, sv, dm)
    mla_y, mla_lse = _mla_fwd(qp, kp, mv, dm, sb_tot)
    w_o8, w_up8, w_down8 = _chip_exchange_wait(late_w, mla_lse, "gather_w_late_wait")
    w_o_f = w_o8.reshape(D, D)
    mix, x1, h2, u, dr2, bst_c, wst_c = _mlp_fwd(sb_y, mla_y, x0, mod, loss_target, w_o_f, w_up8, w_down8,
                                                  ln1_g, ln1_b, ln2_g, ln2_b, dm)

    du, dffb, dx0a, dsb_y, dmla_y, g_o, bst_b, wst_b = _mlp_bwd(
        dr2, u, x1, x0, mix, sb_y, mla_y, mod, w_up8, w_down8, w_o_f, ln1_g, dm)
    T = B * S
    r2 = lambda a: a.reshape(T, a.shape[2])
    by_core = lambda a: a.reshape((4, 2) + a.shape[1:])
    g_o = g_o.astype(BF16)
    g_up8 = _mm_tn(r2(h2), r2(du), "grad_w_up", dr2, out_dtype=BF16, col_blocks=N_DEV)
    g_down = _mm_tn(r2(u), r2(dffb), "grad_w_down", dr2, relu_sq=True, out_dtype=BF16)
    early = [g_o.reshape(N_DEV, D // N_DEV, D), g_up8, g_down.reshape(N_DEV, dff // N_DEV, D)]
    early_g, early_token = _chip_exchange_start(early, "scatter_g_early_start", scatter=True, after=dr2,
                                                everyone=True)

    dsq, dsk, dsv = _sb_bwd(sq, sk, sv, sb_tot, dsb_y, dm, early_token)
    dqp, dkp, dmv = _mla_bwd(qp, kp, mv, mla_y, mla_lse, dmla_y, dm, dsq)
    grad_x, g_in_p, g_uq_p, g_kv, bst_a, wst_a = _inproj_bwd(
        x, x0, dx0a, mod, g_in, dsq, dsk, dsv, dqp, dkp, dmv, cq, ckv, qn, kvn, w_in_p, w_uq_p, w_kv,
        q_norm_g, kv_norm_g, tc, ts1, ts2, dm)

    dmod = jnp.concatenate([bst_a[:, 1], bst_a[:, 0], bst_b[:, 2], bst_b[:, 1], bst_b[:, 0], bst_c[:, 0]], axis=1)
    small = jnp.concatenate([wst_a[0], wst_a[1], wst_a[4, :qr], wst_a[5, :kvr], wst_b[0], wst_b[1],
                             wst_c[0], wst_c[1], wst_c[2]])
    n1 = small.shape[0]
    small_g, small_token = _chip_exchange_start([_pack([dmod, small], F32, LANES)], "gather_small_start",
                                                scatter=False, after=grad_x)
    g_uq_f = g_uq_p.reshape(qr, nh, HEAD_PAD)[:, :, :qk].reshape(qr, nh * qk)
    g_uk = g_kv[:, :nh * HEAD_PAD].reshape(kvr, nh, HEAD_PAD)[:, :, :MLA_NOPE].reshape(kvr, nh * MLA_NOPE)
    g_ukv_f = jnp.concatenate([g_uk, g_kv[:, nh * HEAD_PAD:]], axis=1)
    early_quarter = _chip_exchange_wait(early_g, g_in_p, "scatter_g_early_wait")

    def by_dest_cols(a):
        k, n = a.shape[0], a.shape[1] // N_DEV
        return jnp.swapaxes(a.reshape(k, N_DEV, n), 0, 1).astype(BF16)

    last = [by_dest_cols(g_in_p[:, :din] + small_token[0, 0]), by_dest_cols(g_uq_f), by_dest_cols(g_ukv_f)]
    last_sum = _core_scatter_sum([by_core(a) for a in last], "scatter_g_last_cores")
    small_by_chip = _chip_exchange_wait(small_g, last_sum[0], "gather_small_wait")
    both = _core_gather(small_by_chip, "gather_small_cores")[0].reshape(N_DEV, -1)
    last_g, last_token = _chip_exchange_start(last_sum, "scatter_g_last_start", scatter=True, after=grad_x)
    names = ["w_in", "w_uq", "w_ukv", "w_o", "w_up", "w_down"]
    moms = [m_w_in, m_w_uq, m_w_ukv, m_w_o, m_w_up, m_w_down]
    vars_ = [v_w_in, v_w_uq, v_w_ukv, v_w_o, v_w_up, v_w_down]
    res_early = [_reduce_adamw(p, w, m, v, "adamw_" + n)
                 for p, w, m, v, n in zip(early_quarter, big[3:], moms[3:], vars_[3:], names[3:])]

    dmod_all = both[:, :B * N_MOD * D].reshape(N_DEV * B, N_MOD * D)
    sm = both[:, B * N_MOD * D:B * N_MOD * D + n1] + last_token[0, 0]
    dmod_my = lax.dynamic_slice(dmod_all, (0, dev * nada), (N_DEV * B, nada))
    row = lambda arrs: jnp.concatenate([a.reshape(1, -1) for a in arrs], axis=1)
    smalls = [ln_in_g, ln_in_b, q_norm_g, kv_norm_g, ln1_g, ln1_b, ln2_g, ln2_b]
    small_shapes = [a.shape for a in smalls]
    (gs, ds, nms, nvs, g_b, d_b, nm_b, nv_b, g_w, d_w, nm_w, nv_w, loss_v) = _finish(
        sm, dmod_all, dmod_my, cact_all, row(smalls),
        row([m_ln_in_g, m_ln_in_b, m_q_norm_g, m_kv_norm_g, m_ln1_g, m_ln1_b, m_ln2_g, m_ln2_b]),
        row([v_ln_in_g, v_ln_in_b, v_q_norm_g, v_kv_norm_g, v_ln1_g, v_ln1_b, v_ln2_g, v_ln2_b]),
        b_ada, m_b_ada, v_b_ada, w_ada[0], m_w_ada[0], v_w_ada[0])
    gsm, dsm, nmsm, nvsm = (_unpack(s, small_shapes) for s in (gs, ds, nms, nvs))
    last_quarter = _chip_exchange_wait(last_g, loss_v, "scatter_g_last_wait")
    res_last = [_reduce_adamw(p, w, m, v, "adamw_" + n)
                for p, w, m, v, n in zip(last_quarter, big[:3], moms[:3], vars_[:3], names[:3])]
    gb, db, nmb, nvb = ([r[i] for r in res_last + res_early] for i in range(4))

    def ordered(sm_l, w_l, ada_w, ada_b):
        return [sm_l[0], sm_l[1], ada_w[None], ada_b, w_l[0], sm_l[2], sm_l[3], w_l[1], w_l[2], w_l[3],
                sm_l[4], sm_l[5], w_l[4], w_l[5], sm_l[6], sm_l[7]]

    loss = loss_v[0, 0]
    return (loss, grad_x, *ordered(gsm, gb, g_w, g_b), *ordered(dsm, db, d_w, d_b),
            *ordered(nmsm, nmb, nm_w, nm_b), *ordered(nvsm, nvb, nv_w, nv_b))
```

```python
import math

import jax
import jax.numpy as jnp
from jax import lax
from jax.experimental import pallas as pl
from jax.experimental.pallas import tpu as pltpu

F32 = jnp.float32
BF16 = jnp.bfloat16

SB_HD = 64
MLA_V = 64
MLA_NOPE = 64
MLA_ROPE = 32
HEAD_PAD = 128
CHUNK = 64
ROPE_BASE = 10000.0
LN_EPS = 1e-5
RMS_EPS = 1e-6
DEPTH = 1
ALPHA = (2.0 * DEPTH) ** 0.25
N_MOD = 6
ADAM_LR = 0.001
ADAM_B1 = 0.9
ADAM_B2 = 0.999
ADAM_EPS = 1e-08
ADAM_WD = 0.01
ADAM_STEP = 10
N_DEV = 8
LANES = 128
LOG2E = 1.4426950408889634
CUMSUM_W = 256
VMEM_LIMIT = 56 * 1024 * 1024
MESH = pl.DeviceIdType.MESH


def _dot(a, b):
    return jnp.dot(a, b, preferred_element_type=F32)


def _dot_nt(a, b):
    return lax.dot_general(a, b, (((1,), (1,)), ((), ())), preferred_element_type=F32)


def _dot_tn(a, b):
    return lax.dot_general(a, b, (((0,), (0,)), ((), ())), preferred_element_type=F32)


def _cparams(sem):
    return pltpu.CompilerParams(dimension_semantics=sem, vmem_limit_bytes=VMEM_LIMIT)


def _full(a):
    nd = a.ndim
    return pl.BlockSpec(a.shape, lambda *_: (0,) * nd, pipeline_mode=pl.Buffered(1))


def _tok(tm, w):
    return pl.BlockSpec((1, tm, w), lambda b, s: (b, s, 0))


def _perb(rows, w):
    return pl.BlockSpec((1, rows, w), lambda b, s: (b, 0, 0))


def _sds(shape, dtype):
    return jax.ShapeDtypeStruct(shape, dtype)


def _ln_fwd(x, g, b):
    mu = jnp.mean(x, axis=-1, keepdims=True)
    xc = x - mu
    var = jnp.mean(xc * xc, axis=-1, keepdims=True)
    rstd = lax.rsqrt(var + LN_EPS)
    xhat = xc * rstd
    return xhat * g + b, xhat, rstd


def _ln_bwd(dy, xhat, rstd, g):
    dxh = dy * g
    m1 = jnp.mean(dxh, axis=-1, keepdims=True)
    m2 = jnp.mean(dxh * xhat, axis=-1, keepdims=True)
    return rstd * (dxh - m1 - xhat * m2)


def _colsum(a):
    return jnp.sum(a, axis=0, keepdims=True)


def _rope(x, c, s1, s2):
    w = x.shape[-1]
    return x * c + pltpu.roll(x, w - 16, 1) * s1 + pltpu.roll(x, 16, 1) * s2


def _rope_t(x, c, s1, s2):
    w = x.shape[-1]
    return x * c - pltpu.roll(x, w - 16, 1) * s1 - pltpu.roll(x, 16, 1) * s2


def _adamw(w, g, m, v):
    m = ADAM_B1 * m + (1.0 - ADAM_B1) * g
    v = ADAM_B2 * v + (1.0 - ADAM_B2) * (g * g)
    m_hat = m / (1.0 - ADAM_B1 ** ADAM_STEP)
    v_hat = v / (1.0 - ADAM_B2 ** ADAM_STEP)
    delta = -ADAM_LR * (m_hat / (jnp.sqrt(v_hat) + ADAM_EPS) + ADAM_WD * w)
    return delta, m, v


def _my_place():
    return lax.axis_index("x"), lax.axis_index("y"), lax.axis_index("c")


def _chip_peers(mx, my):
    out = []
    for j in (1, 2, 3):
        px = 1 - mx if (j >> 1) else mx
        py = 1 - my if (j & 1) else my
        out.append((px, py, 2 * px + py))
    return out


def _split_peers(everyone):
    mx, my, mc = _my_place()
    if not everyone:
        return [(px, py, mc, pk) for px, py, pk in _chip_peers(mx, my)], 2 * mx + my
    peers = []
    for j in range(1, N_DEV):
        px = 1 - mx if (j >> 2) & 1 else mx
        py = 1 - my if (j >> 1) & 1 else my
        pc = 1 - mc if j & 1 else mc
        peers.append((px, py, pc, 4 * px + 2 * py + pc))
    return peers, 4 * mx + 2 * my + mc


def _hbm_call(body, name, n_in, out_shape, sems):
    hbm = pl.BlockSpec(memory_space=pl.ANY)
    return pl.pallas_call(
        body, name=name, out_shape=out_shape,
        in_specs=[hbm] * n_in, out_specs=[hbm] * len(out_shape),
        scratch_shapes=[pltpu.SemaphoreType.DMA(s) for s in sems])


def _chip_exchange(xs, name, scatter):
    n = len(xs)

    def body(*refs):
        x_refs, o_refs = refs[:n], refs[n:2 * n]
        ssem, rsem, lsem = refs[2 * n:]
        mx, my, mc = _my_place()
        me = 2 * mx + my
        peers = _chip_peers(mx, my)

        def copy(i, j, src_slot, dst_slot):
            px, py, _ = peers[j]
            return pltpu.make_async_remote_copy(
                src_ref=x_refs[i].at[src_slot] if scatter else x_refs[i], dst_ref=o_refs[i].at[dst_slot],
                send_sem=ssem.at[i, j], recv_sem=rsem.at[i, j], device_id=(px, py, mc), device_id_type=MESH)

        local = [pltpu.make_async_copy(x_refs[i].at[me] if scatter else x_refs[i], o_refs[i].at[me], lsem.at[i])
                 for i in range(n)]
        sends = [copy(i, j, peers[j][2], me) for i in range(n) for j in range(3)]
        for cp in local + sends:
            cp.start()
        for i in range(n):
            for j in range(3):
                copy(i, j, peers[j][2], peers[j][2]).wait_recv()
        for cp in sends:
            cp.wait_send()
        for cp in local:
            cp.wait()

    out_shape = [_sds((4,) + tuple(x.shape[1:] if scatter else x.shape), x.dtype) for x in xs]
    return _hbm_call(body, name, n, out_shape, [(n, 3), (n, 3), (n,)])(*xs)


def _chip_exchange_start(xs, name, scatter, after, everyone=False):
    n = len(xs)
    npeer = N_DEV - 1 if everyone else 3
    blks = [tuple(x.shape[1:] if scatter else x.shape) for x in xs]

    def body(*refs):
        x_refs, land_refs = refs[:n], refs[n:2 * n]
        ssem, rsem = refs[2 * n + 1], refs[2 * n + 2]
        token = refs[-1]
        peers, me = _split_peers(everyone)
        for i in range(n):
            for j, (px, py, pc, slot) in enumerate(peers):
                pltpu.make_async_remote_copy(
                    src_ref=x_refs[i].at[slot] if scatter else x_refs[i], dst_ref=land_refs[i].at[me],
                    send_sem=ssem.at[npeer * i + j], recv_sem=rsem.at[npeer * i + j], device_id=(px, py, pc),
                    device_id_type=MESH).start()
        token[...] = jnp.zeros_like(token)

    hbm = pl.BlockSpec(memory_space=pltpu.HBM)
    sem = pl.BlockSpec(memory_space=pltpu.SEMAPHORE)
    lands = [lax.empty((npeer + 1,) + b, x.dtype) for b, x in zip(blks, xs)]
    res = pl.pallas_call(
        body, name=name,
        out_shape=[pltpu.SemaphoreType.DMA((npeer * n,)), pltpu.SemaphoreType.DMA((npeer * n,))]
        + [pltpu.HBM(x.shape, x.dtype) for x in xs] + [pltpu.HBM(l.shape, l.dtype) for l in lands]
        + [_sds((8, LANES), F32)],
        in_specs=[hbm] * (2 * n) + [_AFTER],
        out_specs=[sem, sem] + [hbm] * (2 * n) + [pl.BlockSpec(memory_space=pltpu.VMEM)],
        input_output_aliases={i: 2 + i for i in range(2 * n)},
        compiler_params=pltpu.CompilerParams(has_side_effects=pltpu.SideEffectType.DATAFLOW_SIDE_EFFECTING),
    )(*[pltpu.with_memory_space_constraint(a, pltpu.HBM) for a in list(xs) + lands], after)
    return dict(ssem=res[0], rsem=res[1], xs=res[2:2 + n], lands=res[2 + n:2 + 2 * n], n=n, scatter=scatter,
                everyone=everyone), res[-1]


def _chip_exchange_wait(handle, after, name):
    n, scatter, everyone = handle["n"], handle["scatter"], handle["everyone"]
    npeer = N_DEV - 1 if everyone else 3

    def body(*refs):
        x_refs, land_refs = refs[:n], refs[n:2 * n]
        ssem, rsem = refs[2 * n], refs[2 * n + 1]
        peers, _ = _split_peers(everyone)
        for i in range(n):
            for j, (px, py, pc, slot) in enumerate(peers):
                cp = pltpu.make_async_remote_copy(
                    src_ref=x_refs[i].at[slot] if scatter else x_refs[i], dst_ref=land_refs[i].at[slot],
                    send_sem=ssem.at[npeer * i + j], recv_sem=rsem.at[npeer * i + j], device_id=(px, py, pc),
                    device_id_type=MESH)
                cp.wait_send()
                cp.wait_recv()

    hbm = pl.BlockSpec(memory_space=pltpu.HBM)
    sem = pl.BlockSpec(memory_space=pltpu.SEMAPHORE)
    ops = list(handle["xs"]) + list(handle["lands"])
    res = pl.pallas_call(
        body, name=name,
        out_shape=[pltpu.HBM(a.shape, a.dtype) for a in ops],
        in_specs=[hbm] * (2 * n) + [sem, sem, pl.BlockSpec(memory_space=pl.ANY)],
        out_specs=[hbm] * (2 * n),
        input_output_aliases={i: i for i in range(2 * n)},
        compiler_params=pltpu.CompilerParams(has_side_effects=pltpu.SideEffectType.DATAFLOW_SIDE_EFFECTING),
    )(*ops, handle["ssem"], handle["rsem"], after)
    me = 2 * lax.axis_index("x") + lax.axis_index("y")
    if everyone:
        me = 2 * me + lax.axis_index("c")
    out = []
    for x, land in zip(res[:n], res[n:]):
        own = lax.dynamic_index_in_dim(x, me, 0, keepdims=False) if scatter else x
        out.append(lax.dynamic_update_index_in_dim(land, own, me, 0))
    return out


def _core_gather(xs, name):
    n = len(xs)

    def body(*refs):
        x_refs, o_refs, mine, got = refs[:n], refs[n:2 * n], refs[2 * n:3 * n], refs[3 * n:4 * n]
        lsem, ssem, rsem, osem = refs[4 * n:]
        mx, my, mc = _my_place()
        loads = [pltpu.make_async_copy(x_refs[i], mine[i], lsem.at[i]) for i in range(n)]
        for cp in loads:
            cp.start()
        sends, stores = [], []
        for i in range(n):
            loads[i].wait()
            cp = pltpu.make_async_remote_copy(
                src_ref=mine[i], dst_ref=got[i], send_sem=ssem.at[i], recv_sem=rsem.at[i],
                device_id=(mx, my, 1 - mc), device_id_type=MESH)
            cp.start()
            sends.append(cp)
            for k in range(4):
                st = pltpu.make_async_copy(mine[i].at[k], o_refs[i].at[k, mc], osem.at[i, k])
                st.start()
                stores.append(st)
        for i in range(n):
            sends[i].wait_recv()
            for k in range(4):
                st = pltpu.make_async_copy(got[i].at[k], o_refs[i].at[k, 1 - mc], osem.at[n + i, k])
                st.start()
                stores.append(st)
        for cp in sends:
            cp.wait_send()
        for st in stores:
            st.wait()

    hbm = pl.BlockSpec(memory_space=pl.ANY)
    bufs = [pltpu.VMEM(x.shape, x.dtype) for x in xs]
    return pl.pallas_call(
        body, name=name,
        out_shape=[_sds((4, 2) + tuple(x.shape[1:]), x.dtype) for x in xs],
        in_specs=[hbm] * n, out_specs=[hbm] * n,
        scratch_shapes=bufs + bufs + [pltpu.SemaphoreType.DMA((n,)), pltpu.SemaphoreType.DMA((n,)),
                                      pltpu.SemaphoreType.DMA((n,)), pltpu.SemaphoreType.DMA((2 * n, 4))],
        compiler_params=pltpu.CompilerParams(vmem_limit_bytes=VMEM_LIMIT),
    )(*xs)


def _rows_step(k):
    for r in (256, 128, 64, 32, 16, 8):
        if k % r == 0:
            return r
    return k


def _core_scatter_sum(gs, name):
    n = len(gs)

    def body(*refs):
        g_refs, o_refs = refs[:n], refs[n:2 * n]
        send, got, mine = refs[2 * n:3 * n], refs[3 * n:4 * n], refs[4 * n:5 * n]
        lsem, msem, ssem, rsem, osem = refs[5 * n:]
        mx, my, mc = _my_place()
        pairs = [(i, k) for i in range(n) for k in range(4)]
        out_loads = {(i, k): pltpu.make_async_copy(g_refs[i].at[k, 1 - mc], send[i].at[k], lsem.at[i, k])
                     for i, k in pairs}
        own_loads = {(i, k): pltpu.make_async_copy(g_refs[i].at[k, mc], mine[i].at[k], msem.at[i, k])
                     for i, k in pairs}
        for p in pairs:
            out_loads[p].start()
        for p in pairs:
            own_loads[p].start()
        sends = []
        for i in range(n):
            for k in range(4):
                out_loads[i, k].wait()
            cp = pltpu.make_async_remote_copy(
                src_ref=send[i], dst_ref=got[i], send_sem=ssem.at[i], recv_sem=rsem.at[i],
                device_id=(mx, my, 1 - mc), device_id_type=MESH)
            cp.start()
            sends.append(cp)
        stores = []
        for i in range(n):
            for k in range(4):
                own_loads[i, k].wait()
            sends[i].wait_recv()
            rows = g_refs[i].shape[2]
            step = _rows_step(rows)

            def add(r, _, i=i, step=step):
                sl = pl.ds(pl.multiple_of(r * step, step), step)
                for k in range(4):
                    mine[i][k, sl, :] = (mine[i][k, sl, :].astype(F32) + got[i][k, sl, :].astype(F32)).astype(BF16)
                return 0

            lax.fori_loop(0, rows // step, add, 0)
            st = pltpu.make_async_copy(mine[i], o_refs[i], osem.at[i])
            st.start()
            stores.append(st)
        for cp in sends:
            cp.wait_send()
        for st in stores:
            st.wait()

    hbm = pl.BlockSpec(memory_space=pl.ANY)
    blk = [(4,) + tuple(g.shape[2:]) for g in gs]
    bufs = [pltpu.VMEM(b, BF16) for b in blk]
    return pl.pallas_call(
        body, name=name,
        out_shape=[_sds(b, BF16) for b in blk],
        in_specs=[hbm] * n, out_specs=[hbm] * n,
        scratch_shapes=bufs * 3 + [pltpu.SemaphoreType.DMA((n, 4)), pltpu.SemaphoreType.DMA((n, 4)),
                                   pltpu.SemaphoreType.DMA((n,)), pltpu.SemaphoreType.DMA((n,)),
                                   pltpu.SemaphoreType.DMA((n,))],
        compiler_params=pltpu.CompilerParams(vmem_limit_bytes=VMEM_LIMIT),
    )(*gs)


def _all_gather(xs, name):
    by_chip = _chip_exchange(xs, name + "_chips", scatter=False)
    both = _core_gather(by_chip, name + "_cores")
    return [b.reshape((N_DEV,) + tuple(x.shape)) for b, x in zip(both, xs)]


def _ada_partial(c_all, w_ada_loc, b_loc):
    def body(c_ref, w_ref, b_ref, act_ref, mod_ref):
        c = c_ref[...]
        act = c * (1.0 / (1.0 + jnp.exp(-c)))
        act_ref[...] = act
        mod_ref[...] = _dot(act.astype(BF16), w_ref[...].astype(BF16)) + b_ref[...]

    nb, d = c_all.shape
    return pl.pallas_call(
        body, name="ada_partial",
        out_shape=(_sds((nb, d), F32), _sds((nb, w_ada_loc.shape[1]), F32)),
        compiler_params=pltpu.CompilerParams(vmem_limit_bytes=VMEM_LIMIT),
    )(c_all, w_ada_loc, b_loc)


_AFTER = pl.BlockSpec(memory_space=pl.ANY)


def _inproj_fwd(x, mod, ln_g, ln_b, w_in_p, w_uq_p, w_kv, gq, gkv, tc, ts1, ts2, dm, after):
    B, S, D = x.shape
    tm = dm["tm"]
    sbw, qr, kvr, nh = dm["sbw"], dm["qr"], dm["kvr"], dm["nh"]
    o_cq, o_ckv, o_kr = 3 * sbw, 3 * sbw + qr, 3 * sbw + qr + kvr
    qpw = nh * HEAD_PAD

    def body(x_ref, mod_ref, g_ref, b_ref, win_ref, wuq_ref, wkv_ref, gq_ref, gkv_ref, tc_ref, ts1_ref, ts2_ref, _,
             x0_ref, q_ref, k_ref, v_ref, qp_ref, kp_ref, mv_ref, cq_ref, ckv_ref, qn_ref, kvn_ref):
        x0, _, _ = _ln_fwd(x_ref[0], g_ref[...], b_ref[...])
        x0_ref[0] = x0
        mod = mod_ref[0]
        h = (x0 * (1.0 + mod[1:2]) + mod[0:1]).astype(BF16)
        proj = _dot_nt(h, win_ref[...])
        q_ref[0] = (proj[:, 0:sbw] * SB_Q_SCALE).astype(BF16)
        k_ref[0] = proj[:, sbw:2 * sbw].astype(BF16)
        v_ref[0] = proj[:, 2 * sbw:3 * sbw].astype(BF16)
        cq = proj[:, o_cq:o_cq + qr]
        ckv = proj[:, o_ckv:o_ckv + kvr]
        cq_ref[0] = cq
        ckv_ref[0] = ckv
        qn = (cq * lax.rsqrt(jnp.mean(cq * cq, axis=-1, keepdims=True) + RMS_EPS) * gq_ref[...]).astype(BF16)
        kvn = (ckv * lax.rsqrt(jnp.mean(ckv * ckv, axis=-1, keepdims=True) + RMS_EPS) * gkv_ref[...]).astype(BF16)
        qn_ref[0] = qn
        kvn_ref[0] = kvn
        c1, s1, s2 = tc_ref[...], ts1_ref[...], ts2_ref[...]
        c8, s18, s28 = jnp.tile(c1, (1, nh)), jnp.tile(s1, (1, nh)), jnp.tile(s2, (1, nh))
        qp_ref[0] = (_rope(_dot(qn, wuq_ref[...]), c8, s18, s28) * MLA_Q_SCALE).astype(BF16)
        kvo = _dot(kvn, wkv_ref[...])
        kr = pltpu.roll(proj[:, o_kr:o_kr + LANES], 64, 1)
        kr = _rope(kr, c1, s1, s2)
        kp_ref[0] = (kvo[:, 0:qpw] + jnp.tile(kr, (1, nh))).astype(BF16)
        mv_ref[0] = kvo[:, qpw:].astype(BF16)

    tab = pl.BlockSpec((tm, LANES), lambda b, s: (s, 0))
    outs = [(D, F32), (sbw, BF16), (sbw, BF16), (sbw, BF16), (qpw, BF16), (qpw, BF16),
            (nh * MLA_V, BF16), (qr, F32), (kvr, F32), (qr, BF16), (kvr, BF16)]
    return pl.pallas_call(
        body, name="inproj_fwd", grid=(B, S // tm),
        in_specs=[_tok(tm, D), _perb(N_MOD, D), _full(ln_g), _full(ln_b), _full(w_in_p), _full(w_uq_p),
                  _full(w_kv), _full(gq), _full(gkv), tab, tab, tab, _AFTER],
        out_specs=[_tok(tm, w) for w, _ in outs],
        out_shape=[_sds((B, S, w), t) for w, t in outs],
        compiler_params=_cparams(("parallel", "parallel")),
    )(x, mod, ln_g, ln_b, w_in_p, w_uq_p, w_kv, gq, gkv, tc, ts1, ts2, after)


def _neg_abs(x):
    sign = jnp.uint32(0x80000000)
    return lax.bitcast_convert_type(lax.bitcast_convert_type(x, jnp.uint32) | sign, F32)


SB_Q_SCALE = -(SB_HD ** -0.5) * LOG2E
MLA_Q_SCALE = (MLA_NOPE + MLA_ROPE) ** -0.5 * LOG2E


def _log2_keep(zs):
    return jnp.minimum(zs, 0.0) - jnp.log2(1.0 + jnp.exp2(_neg_abs(zs)))


def _split_dot(a, u):
    hi = a.astype(BF16)
    lo = (a - hi.astype(F32)).astype(BF16)
    return _dot(jnp.concatenate([hi, lo], axis=1), jnp.concatenate([u, u], axis=0))


def _tri(n, rel):
    row = lax.broadcasted_iota(jnp.int32, (n, n), 0)
    col = lax.broadcasted_iota(jnp.int32, (n, n), 1)
    return rel(row, col).astype(BF16)


def _running_sum(a, tri, reverse, split, start):
    cs = tri.shape[0]
    n = a.shape[1] // cs
    out = [None] * n
    run = start
    for c in (reversed(range(n)) if reverse else range(n)):
        part = a[:, c * cs:(c + 1) * cs]
        out[c] = (_split_dot(part, tri) if split else _dot(part.astype(BF16), tri)) + run
        run = run + jnp.sum(part, axis=1, keepdims=True)
    return (out[0] if n == 1 else jnp.concatenate(out, axis=1)), run


def _transpose_bf16(a):
    return a.astype(F32).T.astype(BF16)


def _tile_mask(nr, nk, r0, c0, rel):
    row = lax.broadcasted_iota(jnp.int32, (nr, nk), 0) + r0
    col = lax.broadcasted_iota(jnp.int32, (nr, nk), 1) + c0
    return rel(row, col)


def _put_rows(whole, part, r0):
    return part if r0 == 0 else jnp.concatenate([whole[:r0], part], axis=0)


def _diag_tiles(tq, split):
    half = tq // 2
    return [(0, tq, 0, half), (half, half, half, half)] if split else [(0, tq, 0, tq)]


def _sb_fwd(q, k, v, dm):
    B, S, W = q.shape
    tq = dm["tq"]
    nq = S // tq

    def body(q_ref, k_ref, v_ref, y_ref, tot_ref):
        qi = pl.program_id(2)
        q2 = q_ref[0]
        lane = lax.broadcasted_iota(jnp.int32, (tq, LANES), 1)
        qs = jnp.concatenate([jnp.where(lane < SB_HD, q2, 0), jnp.where(lane >= SB_HD, q2, 0)], axis=0).astype(BF16)
        later = _tri(min(tq, CUMSUM_W), lambda a, b: a > b)
        assert tq & (tq - 1) == 0
        strict = _tile_mask(2 * tq, tq, 0, 0, lambda t, s: s < (t & (tq - 1)))

        def block(j, carry, masked):
            acc, run = carry
            off = pl.multiple_of(j * tq, tq)
            zs = _dot_nt(qs, k_ref[0, pl.ds(off, tq), :])
            a = _log2_keep(zs)
            if masked:
                a = jnp.where(strict, a, 0.0)
            a_later, run = _running_sum(a, later, reverse=True, split=True, start=run)
            w = jnp.exp2((a - zs) + a_later)
            if masked:
                w = jnp.where(strict, w, 0.0)
            return acc + _dot(w.astype(BF16), v_ref[0, pl.ds(off, tq), :]), run

        carry = block(qi, (jnp.zeros((2 * tq, LANES), F32), jnp.zeros((2 * tq, 1), F32)), True)
        acc, run = lax.fori_loop(0, qi, lambda jj, c: block(qi - 1 - jj, c, False), carry)
        y_ref[0] = jnp.where(lane < SB_HD, acc[:tq], acc[tq:]).astype(BF16)
        tot_ref[0] = jnp.where(lane < SB_HD, run[:tq], run[tq:])

    qspec = pl.BlockSpec((1, tq, LANES), lambda b, hp, i: (b, i, hp))
    kspec = pl.BlockSpec((1, S, LANES), lambda b, hp, i: (b, 0, hp))
    return pl.pallas_call(
        body, name="sb_fwd", grid=(B, W // LANES, nq),
        in_specs=[qspec, kspec, kspec],
        out_specs=[qspec, qspec],
        out_shape=[_sds((B, S, W), BF16), _sds((B, S, W), F32)],
        compiler_params=_cparams(("parallel", "parallel", "arbitrary")),
    )(q, k, v)


def _sb_bwd(q, k, v, tot, dy, dm, after):
    B, S, W = q.shape
    tq = dm["tq"]
    nq = S // tq

    def body(q_ref, k_ref, v_ref, tot_ref, dy_ref, _, dq_ref, dk_ref, dv_ref, dk_acc, dv_acc):
        qi = pl.program_id(2)

        @pl.when(qi == 0)
        def _():
            dk_acc[...] = jnp.zeros_like(dk_acc)
            dv_acc[...] = jnp.zeros_like(dv_acc)

        q2 = q_ref[0]
        dy2 = dy_ref[0]
        tot2 = tot_ref[0]
        lane = lax.broadcasted_iota(jnp.int32, (tq, LANES), 1)
        in_h = [lane < SB_HD, lane >= SB_HD]
        qh = [jnp.where(m, q2, 0).astype(BF16) for m in in_h]
        dyh = [jnp.where(m, dy2, 0).astype(BF16) for m in in_h]
        q_t = [_transpose_bf16(a) for a in qh]
        dy_t = [_transpose_bf16(a) for a in dyh]
        toth = [tot2[:, 0:1], tot2[:, SB_HD:SB_HD + 1]]

        def tile(j, carry, r0, nr, c0, nk, masked):
            off = pl.multiple_of(j * tq + c0, math.gcd(tq, c0))
            k2 = k_ref[0, pl.ds(off, nk), :]
            v2 = v_ref[0, pl.ds(off, nk), :]
            upto = _tri(min(nk, CUMSUM_W), lambda a, b: a <= b)
            before = _tri(min(nk, CUMSUM_W), lambda a, b: a < b)
            strict = _tile_mask(nr, nk, r0, c0, lambda t, s: s < t) if masked else None
            rows = slice(r0, r0 + nr)
            new = []
            dk_blk = jnp.zeros((LANES, nk), F32)
            dv_blk = jnp.zeros((LANES, nk), F32)
            for h in range(2):
                dq, pa, pg = carry[3 * h][rows], carry[3 * h + 1][rows], carry[3 * h + 2][rows]
                zs = _dot_nt(qh[h][rows], k2)
                a = _log2_keep(zs)
                if masked:
                    a = jnp.where(strict, a, 0.0)
                a_upto, pa = _running_sum(a, upto, reverse=False, split=True, start=pa)
                w = jnp.exp2((a - zs) - a_upto)
                if masked:
                    w = jnp.where(strict, w, 0.0)
                g = _dot_nt(dyh[h][rows], v2) * w
                g_before, pg = _running_sum(g, before, reverse=False, split=False, start=pg)
                dz = (g + g_before) * jnp.exp2(a) - g_before
                if masked:
                    dz = jnp.where(strict, dz, 0.0)
                dzb = dz.astype(BF16)
                dv_blk = dv_blk + _dot(dy_t[h][:, rows], w.astype(BF16))
                dk_blk = dk_blk + _dot(q_t[h][:, rows], dzb)
                new += [_put_rows(carry[3 * h], dq + _dot(dzb, k2), r0), _put_rows(carry[3 * h + 1], pa, r0),
                        _put_rows(carry[3 * h + 2], pg, r0)]
            dk_acc[j, :, c0:c0 + nk] += dk_blk
            dv_acc[j, :, c0:c0 + nk] += dv_blk
            return tuple(new)

        zero = jnp.zeros((tq, LANES), F32)
        zrun = jnp.zeros((tq, 1), F32)
        carry = lax.fori_loop(0, qi, lambda j, c: tile(j, c, 0, tq, 0, tq, False),
                              (zero, -toth[0], zrun, zero, -toth[1], zrun))
        for r0, nr, c0, nk in _diag_tiles(tq, False):
            carry = tile(qi, carry, r0, nr, c0, nk, True)
        dq_ref[0] = (jnp.where(in_h[0], carry[0], carry[3]) * (SB_HD ** -0.5)).astype(BF16)

        @pl.when(qi == nq - 1)
        def _():
            for jb in range(nq):
                dk_ref[0, jb * tq:(jb + 1) * tq, :] = (dk_acc[jb].T * (-1.0 / LOG2E)).astype(BF16)
                dv_ref[0, jb * tq:(jb + 1) * tq, :] = dv_acc[jb].T.astype(BF16)

    qspec = pl.BlockSpec((1, tq, LANES), lambda b, hp, i: (b, i, hp))
    kspec = pl.BlockSpec((1, S, LANES), lambda b, hp, i: (b, 0, hp))
    return pl.pallas_call(
        body, name="sb_bwd", grid=(B, W // LANES, nq),
        in_specs=[qspec, kspec, kspec, qspec, qspec, _AFTER],
        out_specs=[qspec, kspec, kspec],
        out_shape=[_sds((B, S, W), BF16)] * 3,
        scratch_shapes=[pltpu.VMEM((nq, LANES, tq), F32), pltpu.VMEM((nq, LANES, tq), F32)],
        compiler_params=_cparams(("parallel", "parallel", "arbitrary")),
    )(q, k, v, tot, dy, after)


def _same_or_earlier_chunk(row, col):
    return lax.shift_right_logical(col, 6) <= lax.shift_right_logical(row, 6)


def _mla_fwd(qp, kp, mv, dm, after):
    B, S, QW = qp.shape
    VW = mv.shape[2]
    tq = dm["tq"]
    nq = S // tq
    assert CHUNK == 64

    def body(q_ref, k_ref, v_ref, _, y_ref, lse_ref):
        qi = pl.program_id(2)
        q2 = q_ref[0]
        lane = lax.broadcasted_iota(jnp.int32, (tq, LANES), 1)

        def tile(j, carry, r0, nr, c0, nk, masked):
            off = pl.multiple_of(j * tq + c0, math.gcd(tq, c0))
            v2 = v_ref[0, pl.ds(off, nk), :]
            allowed = _tile_mask(nr, nk, r0, c0, _same_or_earlier_chunk) if masked else None
            rows = slice(r0, r0 + nr)
            heads = range(2)
            sl = [slice(h * HEAD_PAD, (h + 1) * HEAD_PAD) for h in heads]
            m_old = [carry[3 * h + 1][rows] for h in heads]
            s = [_dot_nt(q2[rows, sl[h]], k_ref[0, pl.ds(off, nk), sl[h]]) for h in heads]
            if masked:
                s = [jnp.where(allowed, s[h], -1e30) for h in heads]
            m_new = [jnp.maximum(m_old[h], jnp.max(s[h], axis=1, keepdims=True)) for h in heads]
            alpha = [jnp.exp2(m_old[h] - m_new[h]) for h in heads]
            p = [jnp.exp2(s[h] - m_new[h]) for h in heads]
            acc = [alpha[h] * carry[3 * h][rows] + _dot(p[h].astype(BF16), v2) for h in heads]
            l = [alpha[h] * carry[3 * h + 2][rows] + jnp.sum(p[h], axis=1, keepdims=True) for h in heads]
            out = []
            for h in heads:
                out += [_put_rows(carry[3 * h], acc[h], r0), _put_rows(carry[3 * h + 1], m_new[h], r0),
                        _put_rows(carry[3 * h + 2], l[h], r0)]
            return tuple(out)

        zero = jnp.zeros((tq, LANES), F32)
        m0 = jnp.full((tq, 1), -1e30, F32)
        l0 = jnp.zeros((tq, 1), F32)
        carry = (zero, m0, l0, zero, m0, l0)
        for r0, nr, c0, nk in _diag_tiles(tq, False):
            carry = tile(qi, carry, r0, nr, c0, nk, True)
        carry = lax.fori_loop(0, qi, lambda j, c: tile(j, c, 0, tq, 0, tq, False), carry)
        y0 = carry[0] / carry[2]
        y1 = carry[3] / carry[5]
        y_ref[0] = jnp.where(lane < MLA_V, y0, y1).astype(BF16)
        lse_ref[0] = jnp.where(lane < MLA_V, carry[1] + jnp.log2(carry[2]), carry[4] + jnp.log2(carry[5]))

    qspec = pl.BlockSpec((1, tq, 2 * HEAD_PAD), lambda b, hp, i: (b, i, hp))
    kspec = pl.BlockSpec((1, S, 2 * HEAD_PAD), lambda b, hp, i: (b, 0, hp))
    vspec = pl.BlockSpec((1, S, LANES), lambda b, hp, i: (b, 0, hp))
    yspec = pl.BlockSpec((1, tq, LANES), lambda b, hp, i: (b, i, hp))
    return pl.pallas_call(
        body, name="mla_fwd", grid=(B, VW // LANES, nq),
        in_specs=[qspec, kspec, vspec, _AFTER],
        out_specs=[yspec, yspec],
        out_shape=[_sds((B, S, VW), BF16), _sds((B, S, VW), F32)],
        compiler_params=_cparams(("parallel", "parallel", "arbitrary")),
    )(qp, kp, mv, after)


def _mla_bwd(qp, kp, mv, y, lse, dy, dm, after):
    B, S, QW = qp.shape
    VW = mv.shape[2]
    tq = dm["tq"]
    nq = S // tq
    scale = (MLA_NOPE + MLA_ROPE) ** -0.5

    def body(q_ref, k_ref, v_ref, y_ref, lse_ref, dy_ref, _, dq_ref, dk_ref, dv_ref, dk_acc, dv_acc):
        qi = pl.program_id(2)

        @pl.when(qi == 0)
        def _():
            dk_acc[...] = jnp.zeros_like(dk_acc)
            dv_acc[...] = jnp.zeros_like(dv_acc)

        q2 = q_ref[0]
        dy2 = dy_ref[0]
        lse2 = lse_ref[0]
        lane = lax.broadcasted_iota(jnp.int32, (tq, LANES), 1)
        in_h = [lane < MLA_V, lane >= MLA_V]
        prod = dy2.astype(F32) * y_ref[0].astype(F32)
        delta = [jnp.sum(jnp.where(m, prod, 0.0), axis=1, keepdims=True) for m in in_h]
        dyh = [jnp.where(m, dy2, 0).astype(BF16) for m in in_h]
        lseh = [lse2[:, 0:1], lse2[:, MLA_V:MLA_V + 1]]
        q_t = _transpose_bf16(q2)
        dy_t = [_transpose_bf16(a) for a in dyh]

        def tile(j, carry, r0, nr, c0, nk, masked):
            off = pl.multiple_of(j * tq + c0, math.gcd(tq, c0))
            v2 = v_ref[0, pl.ds(off, nk), :]
            allowed = _tile_mask(nr, nk, r0, c0, _same_or_earlier_chunk) if masked else None
            rows = slice(r0, r0 + nr)
            keys = slice(c0, c0 + nk)
            heads = range(2)
            sl = [slice(h * HEAD_PAD, (h + 1) * HEAD_PAD) for h in heads]
            qhh = [q2[rows, sl[h]] for h in heads]
            dyr = [dyh[h][rows] for h in heads]
            kh = [k_ref[0, pl.ds(off, nk), sl[h]] for h in heads]
            s = [_dot_nt(qhh[h], kh[h]) for h in heads]
            dp = [_dot_nt(dyr[h], v2) for h in heads]
            if masked:
                s = [jnp.where(allowed, s[h], -1e30) for h in heads]
            p = [jnp.exp2(s[h] - lseh[h][rows]) for h in heads]
            dv_acc[j, :, keys] += (_dot(dy_t[0][:, rows], p[0].astype(BF16))
                                   + _dot(dy_t[1][:, rows], p[1].astype(BF16)))
            ds = [(p[h] * (dp[h] - delta[h][rows])).astype(BF16) for h in heads]
            for h in heads:
                dk_acc[j, sl[h], keys] += _dot(q_t[sl[h], rows], ds[h])
            return tuple(_put_rows(carry[h], carry[h][rows] + _dot(ds[h], kh[h]), r0) for h in heads)

        zero = jnp.zeros((tq, HEAD_PAD), F32)
        carry = lax.fori_loop(0, qi, lambda j, c: tile(j, c, 0, tq, 0, tq, False), (zero, zero))
        for r0, nr, c0, nk in _diag_tiles(tq, True):
            carry = tile(qi, carry, r0, nr, c0, nk, True)
        dq_ref[0] = (jnp.concatenate([carry[0], carry[1]], axis=1) * scale).astype(BF16)

        @pl.when(qi == nq - 1)
        def _():
            for jb in range(nq):
                dk_ref[0, jb * tq:(jb + 1) * tq, :] = (dk_acc[jb].T * (1.0 / LOG2E)).astype(BF16)
                dv_ref[0, jb * tq:(jb + 1) * tq, :] = dv_acc[jb].T.astype(BF16)

    qspec = pl.BlockSpec((1, tq, 2 * HEAD_PAD), lambda b, hp, i: (b, i, hp))
    kspec = pl.BlockSpec((1, S, 2 * HEAD_PAD), lambda b, hp, i: (b, 0, hp))
    vspec = pl.BlockSpec((1, S, LANES), lambda b, hp, i: (b, 0, hp))
    yspec = pl.BlockSpec((1, tq, LANES), lambda b, hp, i: (b, i, hp))
    return pl.pallas_call(
        body, name="mla_bwd", grid=(B, VW // LANES, nq),
        in_specs=[qspec, kspec, vspec, yspec, yspec, yspec, _AFTER],
        out_specs=[qspec, kspec, vspec],
        out_shape=[_sds((B, S, QW), BF16), _sds((B, S, QW), BF16), _sds((B, S, VW), BF16)],
        scratch_shapes=[pltpu.VMEM((nq, 2 * HEAD_PAD, tq), F32), pltpu.VMEM((nq, LANES, tq), F32)],
        compiler_params=_cparams(("parallel", "parallel", "arbitrary")),
    )(qp, kp, mv, y, lse, dy, after)


def _stat_specs(B, D):
    specs = [pl.BlockSpec((1, 8, D), lambda b, s: (b, 0, 0)), pl.BlockSpec((8, D), lambda b, s: (0, 0))]
    shapes = [_sds((B, 8, D), F32), _sds((8, D), F32)]
    return specs, shapes


def _stat_init(bst_ref, wst_ref):
    @pl.when(pl.program_id(1) == 0)
    def _():
        bst_ref[...] = jnp.zeros_like(bst_ref)

    @pl.when((pl.program_id(0) == 0) & (pl.program_id(1) == 0))
    def _():
        wst_ref[...] = jnp.zeros_like(wst_ref)


def _mlp_fwd(sb_y, mla_y, x0, mod, target, w_o, w_up, w_down, ln1_g, ln1_b, ln_g, ln_b, dm):
    B, S, D = x0.shape
    tm = dm["tm"]
    sbw = sb_y.shape[2]
    nck, _, ck = w_up.shape
    dff = nck * ck

    def body(ya_ref, yb_ref, x0_ref, mod_ref, t_ref, wo_ref, wu_ref, wd_ref, g1_ref, b1_ref, g_ref, b_ref,
             mix_ref, x1_ref, h2_ref, u_ref, dr_ref, bst_ref, wst_ref):
        _stat_init(bst_ref, wst_ref)
        mod = mod_ref[0]
        mix = _dot(ya_ref[0], wo_ref[0:sbw, :]) + _dot(yb_ref[0], wo_ref[sbw:, :])
        mix_ref[0] = mix
        x1, _, _ = _ln_fwd(ALPHA * x0_ref[0] + (1.0 + mod[2:3]) * mix, g1_ref[...], b1_ref[...])
        x1_ref[0] = x1
        h2 = (x1 * (1.0 + mod[4:5]) + mod[3:4]).astype(BF16)
        h2_ref[0] = h2
        g = g_ref[...]
        ff = jnp.zeros((tm, D), F32)
        for c in range(nck):
            u = _dot(h2, wu_ref[c])
            u_ref[0, :, c * ck:(c + 1) * ck] = u.astype(BF16)
            act = jnp.square(jnp.maximum(u, 0.0)).astype(BF16)
            ff = ff + _dot(act, wd_ref[c])
        x2, xhat, rstd = _ln_fwd(ALPHA * x1 + (1.0 + mod[5:6]) * ff, g, b_ref[...])
        err = x2 - t_ref[0]
        dy = err * (1.0 / D)
        dr = _ln_bwd(dy, xhat, rstd, g)
        dr_ref[0] = dr
        bst_ref[0, 0:1, :] += _colsum(dr * ff)
        wst_ref[0:1, :] += _colsum(dy * xhat)
        wst_ref[1:2, :] += _colsum(dy)
        wst_ref[2:3, :] += _colsum(err * err) * (0.5 / D)

    sspecs, sshapes = _stat_specs(B, D)
    return pl.pallas_call(
        body, name="mlp_fwd", grid=(B, S // tm),
        in_specs=[_tok(tm, sbw), _tok(tm, mla_y.shape[2]), _tok(tm, D), _perb(N_MOD, D), _tok(tm, D),
                  _full(w_o), _full(w_up), _full(w_down), _full(ln1_g), _full(ln1_b), _full(ln_g), _full(ln_b)],
        out_specs=[_tok(tm, D), _tok(tm, D), _tok(tm, D), _tok(tm, dff), _tok(tm, D)] + sspecs,
        out_shape=[_sds((B, S, D), F32), _sds((B, S, D), F32), _sds((B, S, D), BF16), _sds((B, S, dff), BF16),
                   _sds((B, S, D), F32)] + sshapes,
        compiler_params=_cparams(("arbitrary", "arbitrary")),
    )(sb_y, mla_y, x0, mod, target, w_o, w_up, w_down, ln1_g, ln1_b, ln_g, ln_b)


def _mlp_bwd(dr2, u, x1, x0, mix, sb_y, mla_y, mod, w_up, w_down, w_o, ln_g, dm):
    B, S, D = x1.shape
    tm = dm["tm_small"]
    sbw = dm["sbw"]
    nck, _, ck = w_up.shape
    dff = nck * ck

    def body(dr_ref, u_ref, x1_ref, x0_ref, mix_ref, ya_ref, yb_ref, mod_ref, wu_ref, wd_ref, wo_ref, g_ref,
             du_ref, dff_ref, dx0_ref, dya_ref, dyb_ref, go_ref, bst_ref, wst_ref):
        _stat_init(bst_ref, wst_ref)

        @pl.when((pl.program_id(0) == 0) & (pl.program_id(1) == 0))
        def _():
            go_ref[...] = jnp.zeros_like(go_ref)

        mod = mod_ref[0]
        dr2 = dr_ref[0]
        dffv = ((1.0 + mod[5:6]) * dr2).astype(BF16)
        dff_ref[0] = dffv
        dh2 = jnp.zeros((tm, D), F32)
        for c in range(nck):
            sl = slice(c * ck, (c + 1) * ck)
            da = _dot_nt(dffv, wd_ref[c])
            du = (da * (2.0 * jnp.maximum(u_ref[0, :, sl].astype(F32), 0.0))).astype(BF16)
            du_ref[0, :, sl] = du
            dh2 = dh2 + _dot_nt(du, wu_ref[c])
        x1 = x1_ref[0]
        dx1 = ALPHA * dr2 + dh2 * (1.0 + mod[4:5])
        bst_ref[0, 0:1, :] += _colsum(dh2 * x1)
        bst_ref[0, 1:2, :] += _colsum(dh2)
        mix = mix_ref[0]
        g = g_ref[...]
        _, xhat, rstd = _ln_fwd(ALPHA * x0_ref[0] + (1.0 + mod[2:3]) * mix, g, 0.0)
        dr1 = _ln_bwd(dx1, xhat, rstd, g)
        wst_ref[0:1, :] += _colsum(dx1 * xhat)
        wst_ref[1:2, :] += _colsum(dx1)
        bst_ref[0, 2:3, :] += _colsum(dr1 * mix)
        dx0_ref[0] = ALPHA * dr1
        dmix = ((1.0 + mod[2:3]) * dr1).astype(BF16)
        dya_ref[0] = _dot_nt(dmix, wo_ref[0:sbw, :]).astype(BF16)
        dyb_ref[0] = _dot_nt(dmix, wo_ref[sbw:, :]).astype(BF16)
        go_ref[0:sbw, :] += _dot_tn(ya_ref[0], dmix)
        go_ref[sbw:, :] += _dot_tn(yb_ref[0], dmix)

    sspecs, sshapes = _stat_specs(B, D)
    wa, wb = sbw, w_o.shape[0] - sbw
    return pl.pallas_call(
        body, name="mlp_bwd", grid=(B, S // tm),
        in_specs=[_tok(tm, D), _tok(tm, dff), _tok(tm, D), _tok(tm, D), _tok(tm, D), _tok(tm, wa), _tok(tm, wb),
                  _perb(N_MOD, D), _full(w_up), _full(w_down), _full(w_o), _full(ln_g)],
        out_specs=[_tok(tm, dff), _tok(tm, D), _tok(tm, D), _tok(tm, wa), _tok(tm, wb),
                   pl.BlockSpec(w_o.shape, lambda b, s: (0, 0))] + sspecs,
        out_shape=[_sds((B, S, dff), BF16), _sds((B, S, D), BF16), _sds((B, S, D), F32),
                   _sds((B, S, wa), BF16), _sds((B, S, wb), BF16), _sds(w_o.shape, F32)] + sshapes,
        compiler_params=_cparams(("arbitrary", "arbitrary")),
    )(dr2, u, x1, x0, mix, sb_y, mla_y, mod, w_up, w_down, w_o, ln_g)


def _inproj_bwd(x, x0, dx0a, mod, ln_g, dq, dk, dv, dqp, dkp, dmv, cq, ckv, qn, kvn, w_in_p, w_uq_p, w_kv, gq, gkv,
                tc, ts1, ts2, dm):
    B, S, D = x.shape
    tm = dm["tm"]
    sbw, qr, kvr, nh = dm["sbw"], dm["qr"], dm["kvr"], dm["nh"]
    qpw = nh * HEAD_PAD
    dinp = w_in_p.shape[0]
    kvw = w_kv.shape[1]

    def body(x_ref, x0_ref, dx0a_ref, mod_ref, g_ref, dq_ref, dk_ref, dv_ref, dqp_ref, dkp_ref, dmv_ref,
             cq_ref, ckv_ref, qn_ref, kvn_ref, win_ref, wuq_ref, wkv_ref, gq_ref, gkv_ref, tc_ref, ts1_ref, ts2_ref,
             gx_ref, gin_ref, guq_ref, gwkv_ref, bst_ref, wst_ref):
        _stat_init(bst_ref, wst_ref)

        @pl.when((pl.program_id(0) == 0) & (pl.program_id(1) == 0))
        def _():
            gin_ref[...] = jnp.zeros_like(gin_ref)
            guq_ref[...] = jnp.zeros_like(guq_ref)
            gwkv_ref[...] = jnp.zeros_like(gwkv_ref)

        mod = mod_ref[0]
        c1, s1, s2 = tc_ref[...], ts1_ref[...], ts2_ref[...]
        c8, s18, s28 = jnp.tile(c1, (1, nh)), jnp.tile(s1, (1, nh)), jnp.tile(s2, (1, nh))
        dqpre = _rope_t(dqp_ref[0].astype(F32), c8, s18, s28).astype(BF16)
        guq_ref[...] += _dot_tn(qn_ref[0], dqpre)
        gq = gq_ref[...]
        cq = cq_ref[0]
        rq = lax.rsqrt(jnp.mean(cq * cq, axis=-1, keepdims=True) + RMS_EPS)
        dqn = _dot_nt(dqpre, wuq_ref[...])
        wst_ref[4:5, 0:qr] += _colsum(dqn * cq * rq)
        dqg = dqn * gq
        dcq = rq * dqg - cq * (rq * rq * rq) * jnp.mean(dqg * cq, axis=-1, keepdims=True)

        dkpre = _rope_t(dkp_ref[0].astype(F32), c8, s18, s28)
        dkr = dkpre[:, 0:HEAD_PAD]
        for h in range(1, nh):
            dkr = dkr + dkpre[:, h * HEAD_PAD:(h + 1) * HEAD_PAD]
        lane = lax.broadcasted_iota(jnp.int32, (tm, LANES), 1)
        dkr = jnp.where((lane >= MLA_NOPE) & (lane < MLA_NOPE + MLA_ROPE), dkr, 0.0)
        dkr = pltpu.roll(dkr, LANES - MLA_NOPE, 1)
        dkvo = jnp.concatenate([dkpre.astype(BF16), dmv_ref[0]], axis=1)
        gwkv_ref[...] += _dot_tn(kvn_ref[0], dkvo)
        gkv = gkv_ref[...]
        ckv = ckv_ref[0]
        rkv = lax.rsqrt(jnp.mean(ckv * ckv, axis=-1, keepdims=True) + RMS_EPS)
        dkvn = _dot_nt(dkvo, wkv_ref[...])
        wst_ref[5:6, 0:kvr] += _colsum(dkvn * ckv * rkv)
        dkg = dkvn * gkv
        dckv = rkv * dkg - ckv * (rkv * rkv * rkv) * jnp.mean(dkg * ckv, axis=-1, keepdims=True)

        dproj = jnp.concatenate([dq_ref[0], dk_ref[0], dv_ref[0], dcq.astype(BF16), dckv.astype(BF16),
                                 dkr.astype(BF16)], axis=1)
        dh = _dot(dproj, win_ref[...])
        x0 = x0_ref[0]
        gin_ref[...] += _dot_tn(dproj, (x0 * (1.0 + mod[1:2]) + mod[0:1]).astype(BF16))
        dx0 = dx0a_ref[0] + dh * (1.0 + mod[1:2])
        bst_ref[0, 0:1, :] += _colsum(dh * x0)
        bst_ref[0, 1:2, :] += _colsum(dh)
        g = g_ref[...]
        _, xhat, rstd = _ln_fwd(x_ref[0], g, 0.0)
        gx_ref[0] = _ln_bwd(dx0, xhat, rstd, g)
        wst_ref[0:1, :] += _colsum(dx0 * xhat)
        wst_ref[1:2, :] += _colsum(dx0)

    tab = pl.BlockSpec((tm, LANES), lambda b, s: (s, 0))
    sspecs, sshapes = _stat_specs(B, D)
    return pl.pallas_call(
        body, name="inproj_bwd", grid=(B, S // tm),
        in_specs=[_tok(tm, D), _tok(tm, D), _tok(tm, D), _perb(N_MOD, D), _full(ln_g),
                  _tok(tm, sbw), _tok(tm, sbw), _tok(tm, sbw), _tok(tm, qpw), _tok(tm, qpw), _tok(tm, nh * MLA_V),
                  _tok(tm, qr), _tok(tm, kvr), _tok(tm, qr), _tok(tm, kvr),
                  _full(w_in_p), _full(w_uq_p), _full(w_kv), _full(gq), _full(gkv), tab, tab, tab],
        out_specs=[_tok(tm, D), pl.BlockSpec((dinp, D), lambda b, s: (0, 0)),
                   pl.BlockSpec((qr, qpw), lambda b, s: (0, 0)), pl.BlockSpec((kvr, kvw), lambda b, s: (0, 0))] + sspecs,
        out_shape=[_sds((B, S, D), F32), _sds((dinp, D), F32), _sds((qr, qpw), F32),
                   _sds((kvr, kvw), F32)] + sshapes,
        compiler_params=_cparams(("arbitrary", "arbitrary")),
    )(x, x0, dx0a, mod, ln_g, dq, dk, dv, dqp, dkp, dmv, cq, ckv, qn, kvn, w_in_p, w_uq_p, w_kv, gq, gkv,
      tc, ts1, ts2)


def _tile_of(n, cap):
    if n <= cap:
        return n
    best = n
    for t in range(LANES, cap + 1, LANES):
        if n % t == 0:
            best = t
    return best


def _mm_tn(a, g, name, after, relu_sq=False, out_dtype=F32, col_blocks=None):
    T, K = a.shape
    N = g.shape[1]
    tt = 1024 if T % 1024 == 0 else (512 if T % 512 == 0 else T)
    tk = _tile_of(K, 1024)
    tn = _tile_of(N, 1280)
    nt = T // tt
    bw = N // col_blocks if col_blocks else tn
    assert tn % bw == 0

    def body(a_ref, g_ref, _, o_ref, acc_ref):
        @pl.when(pl.program_id(2) == 0)
        def _():
            acc_ref[...] = jnp.zeros_like(acc_ref)

        av = a_ref[...]
        if relu_sq:
            av = jnp.square(jnp.maximum(av.astype(F32), 0.0)).astype(BF16)
        acc_ref[...] += _dot_tn(av, g_ref[...])

        @pl.when(pl.program_id(2) == nt - 1)
        def _():
            if col_blocks:
                for c in range(tn // bw):
                    o_ref[c] = acc_ref[:, c * bw:(c + 1) * bw].astype(out_dtype)
            else:
                o_ref[...] = acc_ref[...].astype(out_dtype)

    if col_blocks:
        out_spec = pl.BlockSpec((tn // bw, tk, bw), lambda i, j, t: (j, i, 0))
        out_shape = _sds((col_blocks, K, bw), out_dtype)
    else:
        out_spec = pl.BlockSpec((tk, tn), lambda i, j, t: (i, j))
        out_shape = _sds((K, N), out_dtype)
    return pl.pallas_call(
        body, name=name, grid=(K // tk, N // tn, nt),
        in_specs=[pl.BlockSpec((tt, tk), lambda i, j, t: (t, i)), pl.BlockSpec((tt, tn), lambda i, j, t: (t, j)),
                  _AFTER],
        out_specs=out_spec, out_shape=out_shape,
        scratch_shapes=[pltpu.VMEM((tk, tn), F32)],
        compiler_params=_cparams(("parallel", "parallel", "arbitrary")),
    )(a, g, after)


def _reduce_adamw(parts, w, m, v, name):
    P, K, N = parts.shape
    tr = 256 if K % 256 == 0 else K

    def body(p_ref, w_ref, m_ref, v_ref, g_ref, d_ref, nm_ref, nv_ref):
        g = p_ref[0].astype(F32)
        for k in range(1, P):
            g = g + p_ref[k].astype(F32)
        g_ref[0] = g
        d_ref[0], nm_ref[0], nv_ref[0] = _adamw(w_ref[0], g, m_ref[0], v_ref[0])

    spec = pl.BlockSpec((1, tr, N), lambda r: (0, r, 0))
    return pl.pallas_call(
        body, name=name, grid=(K // tr,),
        in_specs=[pl.BlockSpec((P, tr, N), lambda r: (0, r, 0)), spec, spec, spec],
        out_specs=[spec] * 4, out_shape=[_sds((1, K, N), F32)] * 4,
        compiler_params=_cparams(("parallel",)),
    )(parts, w, m, v)


def _finish(sm, dmod_all, dmod_my, cact_all, p_small, m_small, v_small, b_ada, m_b, v_b, w_ada, m_w, v_w):
    n0 = p_small.shape[1]
    n1 = sm.shape[1]
    d = cact_all.shape[1]

    def body(sm_ref, dma_ref, dmm_ref, ca_ref, p_ref, pm_ref, pv_ref, b_ref, bm_ref, bv_ref, w_ref, wm_ref, wv_ref,
             gs_ref, ds_ref, ms_ref, vs_ref, gb_ref, db_ref, mb_ref, vb_ref, gw_ref, dw_ref, mw_ref, vw_ref,
             loss_ref):
        gs = sm_ref[0:1, :]
        for k in range(1, N_DEV):
            gs = gs + sm_ref[k:k + 1, :]
        gs_ref[...] = gs
        ds_ref[...], ms_ref[...], vs_ref[...] = _adamw(p_ref[...], gs[:, 0:n0], pm_ref[...], pv_ref[...])
        loss_ref[...] = jnp.zeros((1, LANES), F32) + jnp.sum(gs[:, n1 - d:n1])
        gb = jnp.sum(dma_ref[...], axis=0, keepdims=True)
        gb_ref[...] = gb
        db_ref[...], mb_ref[...], vb_ref[...] = _adamw(b_ref[...], gb, bm_ref[...], bv_ref[...])
        gw = _dot_tn(ca_ref[...].astype(BF16), dmm_ref[...].astype(BF16))
        gw_ref[...] = gw
        dw_ref[...], mw_ref[...], vw_ref[...] = _adamw(w_ref[...], gw, wm_ref[...], wv_ref[...])

    s0 = _sds(p_small.shape, F32)
    sb = _sds(b_ada.shape, F32)
    sw = _sds(w_ada.shape, F32)
    return pl.pallas_call(
        body, name="finish_small",
        out_shape=[_sds((1, n1), F32), s0, s0, s0, sb, sb, sb, sb, sw, sw, sw, sw,
                   _sds((1, LANES), F32)],
        compiler_params=pltpu.CompilerParams(vmem_limit_bytes=VMEM_LIMIT),
    )(sm, dmod_all, dmod_my, cact_all, p_small, m_small, v_small, b_ada, m_b, v_b, w_ada, m_w, v_w)


def _pack(arrs, dtype, width):
    flat = jnp.concatenate([a.astype(dtype).reshape(-1) for a in arrs])
    rows = -(-flat.shape[0] // (256 * width)) * 256
    return jnp.pad(flat, (0, rows * width - flat.shape[0])).reshape(rows, width)


def _unpack(slab, shapes):
    flat = slab.reshape(-1)
    out, o = [], 0
    for s in shapes:
        n = math.prod(s)
        out.append(flat[o:o + n].reshape(s))
        o += n
    return out


def _rope_tables(S):
    inv_freq = 1.0 / (ROPE_BASE ** (jnp.arange(0, MLA_ROPE, 2, dtype=F32) / MLA_ROPE))
    ang = jnp.arange(S, dtype=F32)[:, None] * inv_freq[None, :]
    cos, sin = jnp.cos(ang), jnp.sin(ang)
    one = jnp.ones((S, MLA_NOPE), F32)
    z16 = jnp.zeros((S, 16), F32)
    z32 = jnp.zeros((S, 32), F32)
    z64 = jnp.zeros((S, MLA_NOPE), F32)
    tc = jnp.concatenate([one, cos, cos, jnp.ones((S, 32), F32)], axis=1)
    ts1 = jnp.concatenate([z64, -sin, z16, z32], axis=1)
    ts2 = jnp.concatenate([z64, z16, sin, z32], axis=1)
    return tc, ts1, ts2


def kernel(x, c, ln_in_g, ln_in_b, w_ada, b_ada, w_in, q_norm_g, kv_norm_g, w_uq, w_ukv, w_o, ln1_g, ln1_b, w_up, w_down, ln2_g, ln2_b, loss_target, m_ln_in_g, m_ln_in_b, m_w_ada, m_b_ada, m_w_in, m_q_norm_g, m_kv_norm_g, m_w_uq, m_w_ukv, m_w_o, m_ln1_g, m_ln1_b, m_w_up, m_w_down, m_ln2_g, m_ln2_b, v_ln_in_g, v_ln_in_b, v_w_ada, v_b_ada, v_w_in, v_q_norm_g, v_kv_norm_g, v_w_uq, v_w_ukv, v_w_o, v_ln1_g, v_ln1_b, v_w_up, v_w_down, v_ln2_g, v_ln2_b):
    B, S, D = x.shape
    sbw = D // 2
    mlw = D - sbw
    nh = mlw // MLA_V
    qr = w_uq.shape[1]
    kvr = w_ukv.shape[1]
    qk = MLA_NOPE + MLA_ROPE
    dff = w_up.shape[2] * N_DEV
    din = w_in.shape[2] * N_DEV
    tm = 512 if S % 512 == 0 else S
    tq = min(512, S // 2)
    dm = dict(tm=tm, tm_small=min(tm, 256), tq=tq, sbw=sbw, qr=qr, kvr=kvr, nh=nh)
    dev =4 * lax.axis_index("x") + 2 * lax.axis_index("y") + lax.axis_index("c")

    big = [w_in, w_uq, w_ukv, w_o, w_up, w_down]
    first = [w_in[0].T.astype(BF16), w_uq[0].astype(BF16), w_ukv[0].astype(BF16)]
    first_w, first_token = _chip_exchange_start(first, "gather_w_first_start", scatter=False, after=c)

    nada = w_ada.shape[2]
    c_all = _all_gather([c + first_token[0, 0]], "gather_c")[0].reshape(N_DEV * B, D)
    b_loc = lax.dynamic_slice(b_ada, (0, dev * nada), (1, nada))
    cact_all, mod_part = _ada_partial(c_all, w_ada[0], b_loc)
    mod_all = _all_gather([mod_part], "gather_mod")[0]
    mod = lax.dynamic_slice(mod_all, (0, dev * B, 0), (N_DEV, B, nada))
    mod = jnp.swapaxes(mod, 0, 1).reshape(B, N_MOD, D)

    first_by_chip = _chip_exchange_wait(first_w, mod_all, "gather_w_first_wait")
    w_in8, w_uq8, w_ukv8 = [b.reshape((N_DEV,) + b.shape[2:]) for b in _core_gather(first_by_chip, "gather_w_first_cores")]
    late_w, late_token = _chip_exchange_start([a[0].astype(BF16) for a in big[3:]], "gather_w_late_start",
                                              scatter=False, after=w_in8, everyone=True)
    cols = lambda a8: jnp.swapaxes(a8, 0, 1).reshape(a8.shape[1], N_DEV * a8.shape[2])
    w_in_p = jnp.pad(w_in8.reshape(din, D), ((0, LANES - MLA_ROPE), (0, 0)))
    zpad = jnp.zeros((qr, nh, HEAD_PAD - qk), BF16)
    w_uq_p = jnp.concatenate([cols(w_uq8).reshape(qr, nh, qk), zpad], axis=2).reshape(qr, nh * HEAD_PAD)
    w_ukv_f = cols(w_ukv8)
    w_uk = w_ukv_f[:, :nh * MLA_NOPE].reshape(kvr, nh, MLA_NOPE)
    w_uk_p = jnp.concatenate([w_uk, jnp.zeros((kvr, nh, HEAD_PAD - MLA_NOPE), BF16)], axis=2)
    w_kv = jnp.concatenate([w_uk_p.reshape(kvr, nh * HEAD_PAD), w_ukv_f[:, nh * MLA_NOPE:]], axis=1)

    tc, ts1, ts2 = _rope_tables(S)
    g_in, b_in = ln_in_g.reshape(1, D), ln_in_b.reshape(1, D)
    (x0, sq, sk, sv, qp, kp, mv, cq, ckv, qn, kvn) = _inproj_fwd(
        x, mod, g_in, b_in, w_in_p, w_uq_p, w_kv, q_norm_g, kv_norm_g, tc, ts1, ts2, dm, late_token)
    sb_y, sb_tot = _sb_fwd(sq, sk, sv, dm)
    mla_y, mla_lse = _mla_fwd(qp, kp, mv, dm, sb_tot)
    w_o8, w_up8, w_down8 = _chip_exchange_wait(late_w, mla_lse, "gather_w_late_wait")
    w_o_f = w_o8.reshape(D, D)
    mix, x1, h2, u, dr2, bst_c, wst_c = _mlp_fwd(sb_y, mla_y, x0, mod, loss_target, w_o_f, w_up8, w_down8,
                                                  ln1_g, ln1_b, ln2_g, ln2_b, dm)

    du, dffb, dx0a, dsb_y, dmla_y, g_o, bst_b, wst_b = _mlp_bwd(
        dr2, u, x1, x0, mix, sb_y, mla_y, mod, w_up8, w_down8, w_o_f, ln1_g, dm)
    T = B * S
    r2 = lambda a: a.reshape(T, a.shape[2])
    by_core = lambda a: a.reshape((4, 2) + a.shape[1:])
    g_o = g_o.astype(BF16)
    g_up8 = _mm_tn(r2(h2), r2(du), "grad_w_up", dr2, out_dtype=BF16, col_blocks=N_DEV)
    g_down = _mm_tn(r2(u), r2(dffb), "grad_w_down", dr2, relu_sq=True, out_dtype=BF16)
    early = [g_o.reshape(N_DEV, D // N_DEV, D), g_up8, g_down.reshape(N_DEV, dff // N_DEV, D)]
    early_g, early_token = _chip_exchange_start(early, "scatter_g_early_start", scatter=True, after=dr2,
                                                everyone=True)

    dsq, dsk, dsv = _sb_bwd(sq, sk, sv, sb_tot, dsb_y, dm, early_token)
    dqp, dkp, dmv = _mla_bwd(qp, kp, mv, mla_y, mla_lse, dmla_y, dm, dsq)
    grad_x, g_in_p, g_uq_p, g_kv, bst_a, wst_a = _inproj_bwd(
        x, x0, dx0a, mod, g_in, dsq, dsk, dsv, dqp, dkp, dmv, cq, ckv, qn, kvn, w_in_p, w_uq_p, w_kv,
        q_norm_g, kv_norm_g, tc, ts1, ts2, dm)

    dmod = jnp.concatenate([bst_a[:, 1], bst_a[:, 0], bst_b[:, 2], bst_b[:, 1], bst_b[:, 0], bst_c[:, 0]], axis=1)
    small = jnp.concatenate([wst_a[0], wst_a[1], wst_a[4, :qr], wst_a[5, :kvr], wst_b[0], wst_b[1],
                             wst_c[0], wst_c[1], wst_c[2]])
    n1 = small.shape[0]
    small_g, small_token = _chip_exchange_start([_pack([dmod, small], F32, LANES)], "gather_small_start",
                                                scatter=False, after=grad_x)
    g_uq_f = g_uq_p.reshape(qr, nh, HEAD_PAD)[:, :, :qk].reshape(qr, nh * qk)
    g_uk = g_kv[:, :nh * HEAD_PAD].reshape(kvr, nh, HEAD_PAD)[:, :, :MLA_NOPE].reshape(kvr, nh * MLA_NOPE)
    g_ukv_f = jnp.concatenate([g_uk, g_kv[:, nh * HEAD_PAD:]], axis=1)
    early_quarter = _chip_exchange_wait(early_g, g_in_p, "scatter_g_early_wait")

    def by_dest_cols(a):
        k, n = a.shape[0], a.shape[1] // N_DEV
        return jnp.swapaxes(a.reshape(k, N_DEV, n), 0, 1).astype(BF16)

    g_in8 = (g_in_p[:din] + small_token[0, 0]).astype(BF16).reshape(N_DEV, din // N_DEV, D)
    last = [g_in8, by_dest_cols(g_uq_f), by_dest_cols(g_ukv_f)]
    last_sum = _core_scatter_sum([by_core(a) for a in last], "scatter_g_last_cores")
    small_by_chip = _chip_exchange_wait(small_g, last_sum[0], "gather_small_wait")
    both = _core_gather(small_by_chip, "gather_small_cores")[0].reshape(N_DEV, -1)
    last_g, last_token = _chip_exchange_start(last_sum, "scatter_g_last_start", scatter=True, after=grad_x)
    names = ["w_in", "w_uq", "w_ukv", "w_o", "w_up", "w_down"]
    moms = [m_w_in, m_w_uq, m_w_ukv, m_w_o, m_w_up, m_w_down]
    vars_ = [v_w_in, v_w_uq, v_w_ukv, v_w_o, v_w_up, v_w_down]
    res_early = [_reduce_adamw(p, w, m, v, "adamw_" + n)
                 for p, w, m, v, n in zip(early_quarter, big[3:], moms[3:], vars_[3:], names[3:])]

    dmod_all = both[:, :B * N_MOD * D].reshape(N_DEV * B, N_MOD * D)
    sm = both[:, B * N_MOD * D:B * N_MOD * D + n1] + last_token[0, 0]
    dmod_my = lax.dynamic_slice(dmod_all, (0, dev * nada), (N_DEV * B, nada))
    row = lambda arrs: jnp.concatenate([a.reshape(1, -1) for a in arrs], axis=1)
    smalls = [ln_in_g, ln_in_b, q_norm_g, kv_norm_g, ln1_g, ln1_b, ln2_g, ln2_b]
    small_shapes = [a.shape for a in smalls]
    (gs, ds, nms, nvs, g_b, d_b, nm_b, nv_b, g_w, d_w, nm_w, nv_w, loss_v) = _finish(
        sm, dmod_all, dmod_my, cact_all, row(smalls),
        row([m_ln_in_g, m_ln_in_b, m_q_norm_g, m_kv_norm_g, m_ln1_g, m_ln1_b, m_ln2_g, m_ln2_b]),
        row([v_ln_in_g, v_ln_in_b, v_q_norm_g, v_kv_norm_g, v_ln1_g, v_ln1_b, v_ln2_g, v_ln2_b]),
        b_ada, m_b_ada, v_b_ada, w_ada[0], m_w_ada[0], v_w_ada[0])
    gsm, dsm, nmsm, nvsm = (_unpack(s, small_shapes) for s in (gs, ds, nms, nvs))
    last_quarter = list(_chip_exchange_wait(last_g, loss_v, "scatter_g_last_wait"))
    last_quarter[0] = jnp.swapaxes(last_quarter[0], 1, 2)
    res_last =[_reduce_adamw(p, w, m, v, "adamw_" + n)
                for p, w, m, v, n in zip(last_quarter, big[:3], moms[:3], vars_[:3], names[:3])]
    gb, db, nmb, nvb = ([r[i] for r in res_last + res_early] for i in range(4))

    def ordered(sm_l, w_l, ada_w, ada_b):
        return [sm_l[0], sm_l[1], ada_w[None], ada_b, w_l[0], sm_l[2], sm_l[3], w_l[1], w_l[2], w_l[3],
                sm_l[4], sm_l[5], w_l[4], w_l[5], sm_l[6], sm_l[7]]

    loss = loss_v[0, 0]
    return (loss, grad_x, *ordered(gsm, gb, g_w, g_b), *ordered(dsm, db, d_w, d_b),
            *ordered(nmsm, nmb, nm_w, nm_b), *ordered(nvsm, nvb, nv_w, nv_b))
```

```python
import math

import jax
import jax.numpy as jnp
from jax import lax
from jax.experimental import pallas as pl
from jax.experimental.pallas import tpu as pltpu

F32 = jnp.float32
BF16 = jnp.bfloat16

SB_HD = 64
MLA_V = 64
MLA_NOPE = 64
MLA_ROPE = 32
HEAD_PAD = 128
CHUNK = 64
ROPE_BASE = 10000.0
LN_EPS = 1e-5
RMS_EPS = 1e-6
DEPTH = 1
ALPHA = (2.0 * DEPTH) ** 0.25
N_MOD = 6
ADAM_LR = 0.001
ADAM_B1 = 0.9
ADAM_B2 = 0.999
ADAM_EPS = 1e-08
ADAM_WD = 0.01
ADAM_STEP = 10
N_DEV = 8
LANES = 128
LOG2E = 1.4426950408889634
CUMSUM_W = 256
VMEM_LIMIT = 56 * 1024 * 1024
MESH = pl.DeviceIdType.MESH


def _dot(a, b):
    return jnp.dot(a, b, preferred_element_type=F32)


def _dot_nt(a, b):
    return lax.dot_general(a, b, (((1,), (1,)), ((), ())), preferred_element_type=F32)


def _dot_tn(a, b):
    return lax.dot_general(a, b, (((0,), (0,)), ((), ())), preferred_element_type=F32)


def _cparams(sem):
    return pltpu.CompilerParams(dimension_semantics=sem, vmem_limit_bytes=VMEM_LIMIT)


def _full(a):
    nd = a.ndim
    return pl.BlockSpec(a.shape, lambda *_: (0,) * nd, pipeline_mode=pl.Buffered(1))


def _tok(tm, w):
    return pl.BlockSpec((1, tm, w), lambda b, s: (b, s, 0))


def _perb(rows, w):
    return pl.BlockSpec((1, rows, w), lambda b, s: (b, 0, 0))


def _sds(shape, dtype):
    return jax.ShapeDtypeStruct(shape, dtype)


def _ln_fwd(x, g, b):
    mu = jnp.mean(x, axis=-1, keepdims=True)
    xc = x - mu
    var = jnp.mean(xc * xc, axis=-1, keepdims=True)
    rstd = lax.rsqrt(var + LN_EPS)
    xhat = xc * rstd
    return xhat * g + b, xhat, rstd


def _ln_bwd(dy, xhat, rstd, g):
    dxh = dy * g
    m1 = jnp.mean(dxh, axis=-1, keepdims=True)
    m2 = jnp.mean(dxh * xhat, axis=-1, keepdims=True)
    return rstd * (dxh - m1 - xhat * m2)


def _colsum(a):
    return jnp.sum(a, axis=0, keepdims=True)


def _rope(x, c, s1, s2):
    w = x.shape[-1]
    return x * c + pltpu.roll(x, w - 16, 1) * s1 + pltpu.roll(x, 16, 1) * s2


def _rope_t(x, c, s1, s2):
    w = x.shape[-1]
    return x * c - pltpu.roll(x, w - 16, 1) * s1 - pltpu.roll(x, 16, 1) * s2


def _adamw(w, g, m, v):
    m = ADAM_B1 * m + (1.0 - ADAM_B1) * g
    v = ADAM_B2 * v + (1.0 - ADAM_B2) * (g * g)
    m_hat = m / (1.0 - ADAM_B1 ** ADAM_STEP)
    v_hat = v / (1.0 - ADAM_B2 ** ADAM_STEP)
    delta = -ADAM_LR * (m_hat / (jnp.sqrt(v_hat) + ADAM_EPS) + ADAM_WD * w)
    return delta, m, v


def _my_place():
    return lax.axis_index("x"), lax.axis_index("y"), lax.axis_index("c")


def _chip_peers(mx, my):
    out = []
    for j in (1, 2, 3):
        px = 1 - mx if (j >> 1) else mx
        py = 1 - my if (j & 1) else my
        out.append((px, py, 2 * px + py))
    return out


def _split_peers(everyone):
    mx, my, mc = _my_place()
    if not everyone:
        return [(px, py, mc, pk) for px, py, pk in _chip_peers(mx, my)], 2 * mx + my
    peers = []
    for j in range(1, N_DEV):
        px = 1 - mx if (j >> 2) & 1 else mx
        py = 1 - my if (j >> 1) & 1 else my
        pc = 1 - mc if j & 1 else mc
        peers.append((px, py, pc, 4 * px + 2 * py + pc))
    return peers, 4 * mx + 2 * my + mc


def _hbm_call(body, name, n_in, out_shape, sems):
    hbm = pl.BlockSpec(memory_space=pl.ANY)
    return pl.pallas_call(
        body, name=name, out_shape=out_shape,
        in_specs=[hbm] * n_in, out_specs=[hbm] * len(out_shape),
        scratch_shapes=[pltpu.SemaphoreType.DMA(s) for s in sems])


def _chip_exchange(xs, name, scatter):
    n = len(xs)

    def body(*refs):
        x_refs, o_refs = refs[:n], refs[n:2 * n]
        ssem, rsem, lsem = refs[2 * n:]
        mx, my, mc = _my_place()
        me = 2 * mx + my
        peers = _chip_peers(mx, my)

        def copy(i, j, src_slot, dst_slot):
            px, py, _ = peers[j]
            return pltpu.make_async_remote_copy(
                src_ref=x_refs[i].at[src_slot] if scatter else x_refs[i], dst_ref=o_refs[i].at[dst_slot],
                send_sem=ssem.at[i, j], recv_sem=rsem.at[i, j], device_id=(px, py, mc), device_id_type=MESH)

        local = [pltpu.make_async_copy(x_refs[i].at[me] if scatter else x_refs[i], o_refs[i].at[me], lsem.at[i])
                 for i in range(n)]
        sends = [copy(i, j, peers[j][2], me) for i in range(n) for j in range(3)]
        for cp in local + sends:
            cp.start()
        for i in range(n):
            for j in range(3):
                copy(i, j, peers[j][2], peers[j][2]).wait_recv()
        for cp in sends:
            cp.wait_send()
        for cp in local:
            cp.wait()

    out_shape = [_sds((4,) + tuple(x.shape[1:] if scatter else x.shape), x.dtype) for x in xs]
    return _hbm_call(body, name, n, out_shape, [(n, 3), (n, 3), (n,)])(*xs)


def _chip_exchange_start(xs, name, scatter, after, everyone=False):
    n = len(xs)
    npeer = N_DEV - 1 if everyone else 3
    blks = [tuple(x.shape[1:] if scatter else x.shape) for x in xs]

    def body(*refs):
        x_refs, land_refs = refs[:n], refs[n:2 * n]
        ssem, rsem = refs[2 * n + 1], refs[2 * n + 2]
        token = refs[-1]
        peers, me = _split_peers(everyone)
        for i in range(n):
            for j, (px, py, pc, slot) in enumerate(peers):
                pltpu.make_async_remote_copy(
                    src_ref=x_refs[i].at[slot] if scatter else x_refs[i], dst_ref=land_refs[i].at[me],
                    send_sem=ssem.at[npeer * i + j], recv_sem=rsem.at[npeer * i + j], device_id=(px, py, pc),
                    device_id_type=MESH).start()
        token[...] = jnp.zeros_like(token)

    hbm = pl.BlockSpec(memory_space=pltpu.HBM)
    sem = pl.BlockSpec(memory_space=pltpu.SEMAPHORE)
    lands = [lax.empty((npeer + 1,) + b, x.dtype) for b, x in zip(blks, xs)]
    res = pl.pallas_call(
        body, name=name,
        out_shape=[pltpu.SemaphoreType.DMA((npeer * n,)), pltpu.SemaphoreType.DMA((npeer * n,))]
        + [pltpu.HBM(x.shape, x.dtype) for x in xs] + [pltpu.HBM(l.shape, l.dtype) for l in lands]
        + [_sds((8, LANES), F32)],
        in_specs=[hbm] * (2 * n) + [_AFTER],
        out_specs=[sem, sem] + [hbm] * (2 * n) + [pl.BlockSpec(memory_space=pltpu.VMEM)],
        input_output_aliases={i: 2 + i for i in range(2 * n)},
        compiler_params=pltpu.CompilerParams(has_side_effects=pltpu.SideEffectType.DATAFLOW_SIDE_EFFECTING),
    )(*[pltpu.with_memory_space_constraint(a, pltpu.HBM) for a in list(xs) + lands], after)
    return dict(ssem=res[0], rsem=res[1], xs=res[2:2 + n], lands=res[2 + n:2 + 2 * n], n=n, scatter=scatter,
                everyone=everyone), res[-1]


def _chip_exchange_wait(handle, after, name):
    n, scatter, everyone = handle["n"], handle["scatter"], handle["everyone"]
    npeer = N_DEV - 1 if everyone else 3

    def body(*refs):
        x_refs, land_refs = refs[:n], refs[n:2 * n]
        ssem, rsem = refs[2 * n], refs[2 * n + 1]
        peers, _ = _split_peers(everyone)
        for i in range(n):
            for j, (px, py, pc, slot) in enumerate(peers):
                cp = pltpu.make_async_remote_copy(
                    src_ref=x_refs[i].at[slot] if scatter else x_refs[i], dst_ref=land_refs[i].at[slot],
                    send_sem=ssem.at[npeer * i + j], recv_sem=rsem.at[npeer * i + j], device_id=(px, py, pc),
                    device_id_type=MESH)
                cp.wait_send()
                cp.wait_recv()

    hbm = pl.BlockSpec(memory_space=pltpu.HBM)
    sem = pl.BlockSpec(memory_space=pltpu.SEMAPHORE)
    ops = list(handle["xs"]) + list(handle["lands"])
    res = pl.pallas_call(
        body, name=name,
        out_shape=[pltpu.HBM(a.shape, a.dtype) for a in ops],
        in_specs=[hbm] * (2 * n) + [sem, sem, pl.BlockSpec(memory_space=pl.ANY)],
        out_specs=[hbm] * (2 * n),
        input_output_aliases={i: i for i in range(2 * n)},
        compiler_params=pltpu.CompilerParams(has_side_effects=pltpu.SideEffectType.DATAFLOW_SIDE_EFFECTING),
    )(*ops, handle["ssem"], handle["rsem"], after)
    me = 2 * lax.axis_index("x") + lax.axis_index("y")
    if everyone:
        me = 2 * me + lax.axis_index("c")
    out = []
    for x, land in zip(res[:n], res[n:]):
        own = lax.dynamic_index_in_dim(x, me, 0, keepdims=False) if scatter else x
        out.append(lax.dynamic_update_index_in_dim(land, own, me, 0))
    return out


def _core_gather(xs, name):
    n = len(xs)

    def body(*refs):
        x_refs, o_refs, mine, got = refs[:n], refs[n:2 * n], refs[2 * n:3 * n], refs[3 * n:4 * n]
        lsem, ssem, rsem, osem = refs[4 * n:]
        mx, my, mc = _my_place()
        loads = [pltpu.make_async_copy(x_refs[i], mine[i], lsem.at[i]) for i in range(n)]
        for cp in loads:
            cp.start()
        sends, stores = [], []
        for i in range(n):
            loads[i].wait()
            cp = pltpu.make_async_remote_copy(
                src_ref=mine[i], dst_ref=got[i], send_sem=ssem.at[i], recv_sem=rsem.at[i],
                device_id=(mx, my, 1 - mc), device_id_type=MESH)
            cp.start()
            sends.append(cp)
            for k in range(4):
                st = pltpu.make_async_copy(mine[i].at[k], o_refs[i].at[k, mc], osem.at[i, k])
                st.start()
                stores.append(st)
        for i in range(n):
            sends[i].wait_recv()
            for k in range(4):
                st = pltpu.make_async_copy(got[i].at[k], o_refs[i].at[k, 1 - mc], osem.at[n + i, k])
                st.start()
                stores.append(st)
        for cp in sends:
            cp.wait_send()
        for st in stores:
            st.wait()

    hbm = pl.BlockSpec(memory_space=pl.ANY)
    bufs = [pltpu.VMEM(x.shape, x.dtype) for x in xs]
    return pl.pallas_call(
        body, name=name,
        out_shape=[_sds((4, 2) + tuple(x.shape[1:]), x.dtype) for x in xs],
        in_specs=[hbm] * n, out_specs=[hbm] * n,
        scratch_shapes=bufs + bufs + [pltpu.SemaphoreType.DMA((n,)), pltpu.SemaphoreType.DMA((n,)),
                                      pltpu.SemaphoreType.DMA((n,)), pltpu.SemaphoreType.DMA((2 * n, 4))],
        compiler_params=pltpu.CompilerParams(vmem_limit_bytes=VMEM_LIMIT),
    )(*xs)


def _rows_step(k):
    for r in (256, 128, 64, 32, 16, 8):
        if k % r == 0:
            return r
    return k


def _core_scatter_sum(gs, name):
    n = len(gs)

    def body(*refs):
        g_refs, o_refs = refs[:n], refs[n:2 * n]
        send, got, mine = refs[2 * n:3 * n], refs[3 * n:4 * n], refs[4 * n:5 * n]
        lsem, msem, ssem, rsem, osem = refs[5 * n:]
        mx, my, mc = _my_place()
        pairs = [(i, k) for i in range(n) for k in range(4)]
        out_loads = {(i, k): pltpu.make_async_copy(g_refs[i].at[k, 1 - mc], send[i].at[k], lsem.at[i, k])
                     for i, k in pairs}
        own_loads = {(i, k): pltpu.make_async_copy(g_refs[i].at[k, mc], mine[i].at[k], msem.at[i, k])
                     for i, k in pairs}
        for p in pairs:
            out_loads[p].start()
        for p in pairs:
            own_loads[p].start()
        sends = []
        for i in range(n):
            for k in range(4):
                out_loads[i, k].wait()
            cp = pltpu.make_async_remote_copy(
                src_ref=send[i], dst_ref=got[i], send_sem=ssem.at[i], recv_sem=rsem.at[i],
                device_id=(mx, my, 1 - mc), device_id_type=MESH)
            cp.start()
            sends.append(cp)
        stores = []
        for i in range(n):
            for k in range(4):
                own_loads[i, k].wait()
            sends[i].wait_recv()
            rows = g_refs[i].shape[2]
            step = _rows_step(rows)

            def add(r, _, i=i, step=step):
                sl = pl.ds(pl.multiple_of(r * step, step), step)
                for k in range(4):
                    mine[i][k, sl, :] = (mine[i][k, sl, :].astype(F32) + got[i][k, sl, :].astype(F32)).astype(BF16)
                return 0

            lax.fori_loop(0, rows // step, add, 0)
            st = pltpu.make_async_copy(mine[i], o_refs[i], osem.at[i])
            st.start()
            stores.append(st)
        for cp in sends:
            cp.wait_send()
        for st in stores:
            st.wait()

    hbm = pl.BlockSpec(memory_space=pl.ANY)
    blk = [(4,) + tuple(g.shape[2:]) for g in gs]
    bufs = [pltpu.VMEM(b, BF16) for b in blk]
    return pl.pallas_call(
        body, name=name,
        out_shape=[_sds(b, BF16) for b in blk],
        in_specs=[hbm] * n, out_specs=[hbm] * n,
        scratch_shapes=bufs * 3 + [pltpu.SemaphoreType.DMA((n, 4)), pltpu.SemaphoreType.DMA((n, 4)),
                                   pltpu.SemaphoreType.DMA((n,)), pltpu.SemaphoreType.DMA((n,)),
                                   pltpu.SemaphoreType.DMA((n,))],
        compiler_params=pltpu.CompilerParams(vmem_limit_bytes=VMEM_LIMIT),
    )(*gs)


def _all_gather(xs, name):
    by_chip = _chip_exchange(xs, name + "_chips", scatter=False)
    both = _core_gather(by_chip, name + "_cores")
    return [b.reshape((N_DEV,) + tuple(x.shape)) for b, x in zip(both, xs)]


def _ada_partial(c_all, w_ada_loc, b_loc):
    def body(c_ref, w_ref, b_ref, act_ref, mod_ref):
        c = c_ref[...]
        act = c * (1.0 / (1.0 + jnp.exp(-c)))
        act_ref[...] = act
        mod_ref[...] = _dot(act.astype(BF16), w_ref[...].astype(BF16)) + b_ref[...]

    nb, d = c_all.shape
    return pl.pallas_call(
        body, name="ada_partial",
        out_shape=(_sds((nb, d), F32), _sds((nb, w_ada_loc.shape[1]), F32)),
        compiler_params=pltpu.CompilerParams(vmem_limit_bytes=VMEM_LIMIT),
    )(c_all, w_ada_loc, b_loc)


_AFTER = pl.BlockSpec(memory_space=pl.ANY)


def _inproj_fwd(x, mod, ln_g, ln_b, w_in_p, w_uq_p, w_kv, gq, gkv, tc, ts1, ts2, dm, after):
    B, S, D = x.shape
    tm = dm["tm"]
    sbw, qr, kvr, nh = dm["sbw"], dm["qr"], dm["kvr"], dm["nh"]
    o_cq, o_ckv, o_kr = 3 * sbw, 3 * sbw + qr, 3 * sbw + qr + kvr
    qpw = nh * HEAD_PAD

    def body(x_ref, mod_ref, g_ref, b_ref, win_ref, wuq_ref, wkv_ref, gq_ref, gkv_ref, tc_ref, ts1_ref, ts2_ref, _,
             x0_ref, q_ref, k_ref, v_ref, qp_ref, kp_ref, mv_ref, cq_ref, ckv_ref, qn_ref, kvn_ref):
        x0, _, _ = _ln_fwd(x_ref[0], g_ref[...], b_ref[...])
        x0_ref[0] = x0
        mod = mod_ref[0]
        h = (x0 * (1.0 + mod[1:2]) + mod[0:1]).astype(BF16)
        proj = _dot_nt(h, win_ref[...])
        q_ref[0] = (proj[:, 0:sbw] * SB_Q_SCALE).astype(BF16)
        k_ref[0] = proj[:, sbw:2 * sbw].astype(BF16)
        v_ref[0] = proj[:, 2 * sbw:3 * sbw].astype(BF16)
        cq = proj[:, o_cq:o_cq + qr]
        ckv = proj[:, o_ckv:o_ckv + kvr]
        cq_ref[0] = cq
        ckv_ref[0] = ckv
        qn = (cq * lax.rsqrt(jnp.mean(cq * cq, axis=-1, keepdims=True) + RMS_EPS) * gq_ref[...]).astype(BF16)
        kvn = (ckv * lax.rsqrt(jnp.mean(ckv * ckv, axis=-1, keepdims=True) + RMS_EPS) * gkv_ref[...]).astype(BF16)
        qn_ref[0] = qn
        kvn_ref[0] = kvn
        c1, s1, s2 = tc_ref[...], ts1_ref[...], ts2_ref[...]
        c8, s18, s28 = jnp.tile(c1, (1, nh)), jnp.tile(s1, (1, nh)), jnp.tile(s2, (1, nh))
        qp_ref[0] = (_rope(_dot(qn, wuq_ref[...]), c8, s18, s28) * MLA_Q_SCALE).astype(BF16)
        kvo = _dot(kvn, wkv_ref[...])
        kr = pltpu.roll(proj[:, o_kr:o_kr + LANES], 64, 1)
        kr = _rope(kr, c1, s1, s2)
        kp_ref[0] = (kvo[:, 0:qpw] + jnp.tile(kr, (1, nh))).astype(BF16)
        mv_ref[0] = kvo[:, qpw:].astype(BF16)

    tab = pl.BlockSpec((tm, LANES), lambda b, s: (s, 0))
    outs = [(D, F32), (sbw, BF16), (sbw, BF16), (sbw, BF16), (qpw, BF16), (qpw, BF16),
            (nh * MLA_V, BF16), (qr, F32), (kvr, F32), (qr, BF16), (kvr, BF16)]
    return pl.pallas_call(
        body, name="inproj_fwd", grid=(B, S // tm),
        in_specs=[_tok(tm, D), _perb(N_MOD, D), _full(ln_g), _full(ln_b), _full(w_in_p), _full(w_uq_p),
                  _full(w_kv), _full(gq), _full(gkv), tab, tab, tab, _AFTER],
        out_specs=[_tok(tm, w) for w, _ in outs],
        out_shape=[_sds((B, S, w), t) for w, t in outs],
        compiler_params=_cparams(("parallel", "parallel")),
    )(x, mod, ln_g, ln_b, w_in_p, w_uq_p, w_kv, gq, gkv, tc, ts1, ts2, after)


def _neg_abs(x):
    sign = jnp.uint32(0x80000000)
    return lax.bitcast_convert_type(lax.bitcast_convert_type(x, jnp.uint32) | sign, F32)


SB_Q_SCALE = -(SB_HD ** -0.5) * LOG2E
MLA_Q_SCALE = (MLA_NOPE + MLA_ROPE) ** -0.5 * LOG2E


def _log2_keep(zs):
    return jnp.minimum(zs, 0.0) - jnp.log2(1.0 + jnp.exp2(_neg_abs(zs)))


def _split_dot(a, u):
    hi = a.astype(BF16)
    lo = (a - hi.astype(F32)).astype(BF16)
    return _dot(jnp.concatenate([hi, lo], axis=1), jnp.concatenate([u, u], axis=0))


def _tri(n, rel):
    row = lax.broadcasted_iota(jnp.int32, (n, n), 0)
    col = lax.broadcasted_iota(jnp.int32, (n, n), 1)
    return rel(row, col).astype(BF16)


def _running_sum(a, tri, reverse, split, start):
    cs = tri.shape[0]
    n = a.shape[1] // cs
    out = [None] * n
    run = start
    for c in (reversed(range(n)) if reverse else range(n)):
        part = a[:, c * cs:(c + 1) * cs]
        out[c] = (_split_dot(part, tri) if split else _dot(part.astype(BF16), tri)) + run
        run = run + jnp.sum(part, axis=1, keepdims=True)
    return (out[0] if n == 1 else jnp.concatenate(out, axis=1)), run


def _transpose_bf16(a):
    return a.astype(F32).T.astype(BF16)


def _tile_mask(nr, nk, r0, c0, rel):
    row = lax.broadcasted_iota(jnp.int32, (nr, nk), 0) + r0
    col = lax.broadcasted_iota(jnp.int32, (nr, nk), 1) + c0
    return rel(row, col)


def _put_rows(whole, part, r0):
    return part if r0 == 0 else jnp.concatenate([whole[:r0], part], axis=0)


def _diag_tiles(tq, split):
    half = tq // 2
    return [(0, tq, 0, half), (half, half, half, half)] if split else [(0, tq, 0, tq)]


def _sb_fwd(q, k, v, dm):
    B, S, W = q.shape
    tq = dm["tq"]
    nq = S // tq

    def body(q_ref, k_ref, v_ref, y_ref, tot_ref):
        qi = pl.program_id(2)
        q2 = q_ref[0]
        lane = lax.broadcasted_iota(jnp.int32, (tq, LANES), 1)
        qs = jnp.concatenate([jnp.where(lane < SB_HD, q2, 0), jnp.where(lane >= SB_HD, q2, 0)], axis=0).astype(BF16)
        later = _tri(min(tq, CUMSUM_W), lambda a, b: a > b)
        assert tq & (tq - 1) == 0
        strict = _tile_mask(2 * tq, tq, 0, 0, lambda t, s: s < (t & (tq - 1)))

        def block(j, carry, masked):
            acc, run = carry
            off = pl.multiple_of(j * tq, tq)
            zs = _dot_nt(qs, k_ref[0, pl.ds(off, tq), :])
            a = _log2_keep(zs)
            if masked:
                a = jnp.where(strict, a, 0.0)
            a_later, run = _running_sum(a, later, reverse=True, split=True, start=run)
            w = jnp.exp2((a - zs) + a_later)
            if masked:
                w = jnp.where(strict, w, 0.0)
            return acc + _dot(w.astype(BF16), v_ref[0, pl.ds(off, tq), :]), run

        carry = block(qi, (jnp.zeros((2 * tq, LANES), F32), jnp.zeros((2 * tq, 1), F32)), True)
        acc, run = lax.fori_loop(0, qi, lambda jj, c: block(qi - 1 - jj, c, False), carry)
        y_ref[0] = jnp.where(lane < SB_HD, acc[:tq], acc[tq:]).astype(BF16)
        tot_ref[0] = jnp.where(lane < SB_HD, run[:tq], run[tq:])

    qspec = pl.BlockSpec((1, tq, LANES), lambda b, hp, i: (b, i, hp))
    kspec = pl.BlockSpec((1, S, LANES), lambda b, hp, i: (b, 0, hp))
    return pl.pallas_call(
        body, name="sb_fwd", grid=(B, W // LANES, nq),
        in_specs=[qspec, kspec, kspec],
        out_specs=[qspec, qspec],
        out_shape=[_sds((B, S, W), BF16), _sds((B, S, W), F32)],
        compiler_params=_cparams(("parallel", "parallel", "arbitrary")),
    )(q, k, v)


def _sb_bwd(q, k, v, tot, dy, dm, after):
    B, S, W = q.shape
    tq = dm["tq"]
    nq = S // tq

    def body(q_ref, k_ref, v_ref, tot_ref, dy_ref, _, dq_ref, dk_ref, dv_ref, dk_acc, dv_acc):
        qi = pl.program_id(2)

        @pl.when(qi == 0)
        def _():
            dk_acc[...] = jnp.zeros_like(dk_acc)
            dv_acc[...] = jnp.zeros_like(dv_acc)

        q2 = q_ref[0]
        dy2 = dy_ref[0]
        tot2 = tot_ref[0]
        lane = lax.broadcasted_iota(jnp.int32, (tq, LANES), 1)
        in_h = [lane < SB_HD, lane >= SB_HD]
        qh = [jnp.where(m, q2, 0).astype(BF16) for m in in_h]
        dyh = [jnp.where(m, dy2, 0).astype(BF16) for m in in_h]
        q_t = [_transpose_bf16(a) for a in qh]
        dy_t = [_transpose_bf16(a) for a in dyh]
        toth = [tot2[:, 0:1], tot2[:, SB_HD:SB_HD + 1]]

        def tile(j, carry, r0, nr, c0, nk, masked):
            off = pl.multiple_of(j * tq + c0, math.gcd(tq, c0))
            k2 = k_ref[0, pl.ds(off, nk), :]
            v2 = v_ref[0, pl.ds(off, nk), :]
            upto = _tri(min(nk, CUMSUM_W), lambda a, b: a <= b)
            before = _tri(min(nk, CUMSUM_W), lambda a, b: a < b)
            strict = _tile_mask(nr, nk, r0, c0, lambda t, s: s < t) if masked else None
            rows = slice(r0, r0 + nr)
            new = []
            dk_blk = jnp.zeros((LANES, nk), F32)
            dv_blk = jnp.zeros((LANES, nk), F32)
            for h in range(2):
                dq, pa, pg = carry[3 * h][rows], carry[3 * h + 1][rows], carry[3 * h + 2][rows]
                zs = _dot_nt(qh[h][rows], k2)
                a = _log2_keep(zs)
                if masked:
                    a = jnp.where(strict, a, 0.0)
                a_upto, pa = _running_sum(a, upto, reverse=False, split=True, start=pa)
                w = jnp.exp2((a - zs) - a_upto)
                if masked:
                    w = jnp.where(strict, w, 0.0)
                g = _dot_nt(dyh[h][rows], v2) * w
                g_before, pg = _running_sum(g, before, reverse=False, split=False, start=pg)
                dz = (g + g_before) * jnp.exp2(a) - g_before
                if masked:
                    dz = jnp.where(strict, dz, 0.0)
                dzb = dz.astype(BF16)
                dv_blk = dv_blk + _dot(dy_t[h][:, rows], w.astype(BF16))
                dk_blk = dk_blk + _dot(q_t[h][:, rows], dzb)
                new += [_put_rows(carry[3 * h], dq + _dot(dzb, k2), r0), _put_rows(carry[3 * h + 1], pa, r0),
                        _put_rows(carry[3 * h + 2], pg, r0)]
            dk_acc[j, :, c0:c0 + nk] += dk_blk
            dv_acc[j, :, c0:c0 + nk] += dv_blk
            return tuple(new)

        zero = jnp.zeros((tq, LANES), F32)
        zrun = jnp.zeros((tq, 1), F32)
        carry = lax.fori_loop(0, qi, lambda j, c: tile(j, c, 0, tq, 0, tq, False),
                              (zero, -toth[0], zrun, zero, -toth[1], zrun))
        for r0, nr, c0, nk in _diag_tiles(tq, False):
            carry = tile(qi, carry, r0, nr, c0, nk, True)
        dq_ref[0] = (jnp.where(in_h[0], carry[0], carry[3]) * (SB_HD ** -0.5)).astype(BF16)

        @pl.when(qi == nq - 1)
        def _():
            for jb in range(nq):
                dk_ref[0, jb * tq:(jb + 1) * tq, :] = (dk_acc[jb].T * (-1.0 / LOG2E)).astype(BF16)
                dv_ref[0, jb * tq:(jb + 1) * tq, :] = dv_acc[jb].T.astype(BF16)

    qspec = pl.BlockSpec((1, tq, LANES), lambda b, hp, i: (b, i, hp))
    kspec = pl.BlockSpec((1, S, LANES), lambda b, hp, i: (b, 0, hp))
    return pl.pallas_call(
        body, name="sb_bwd", grid=(B, W // LANES, nq),
        in_specs=[qspec, kspec, kspec, qspec, qspec, _AFTER],
        out_specs=[qspec, kspec, kspec],
        out_shape=[_sds((B, S, W), BF16)] * 3,
        scratch_shapes=[pltpu.VMEM((nq, LANES, tq), F32), pltpu.VMEM((nq, LANES, tq), F32)],
        compiler_params=_cparams(("parallel", "parallel", "arbitrary")),
    )(q, k, v, tot, dy, after)


def _same_or_earlier_chunk(row, col):
    return lax.shift_right_logical(col, 6) <= lax.shift_right_logical(row, 6)


def _mla_fwd(qp, kp, mv, dm, after):
    B, S, QW = qp.shape
    VW = mv.shape[2]
    tq = dm["tq"]
    nq = S // tq
    assert CHUNK == 64

    def body(q_ref, k_ref, v_ref, _, y_ref, lse_ref):
        qi = pl.program_id(2)
        q2 = q_ref[0]
        lane = lax.broadcasted_iota(jnp.int32, (tq, LANES), 1)

        def tile(j, carry, r0, nr, c0, nk, masked):
            off = pl.multiple_of(j * tq + c0, math.gcd(tq, c0))
            v2 = v_ref[0, pl.ds(off, nk), :]
            allowed = _tile_mask(nr, nk, r0, c0, _same_or_earlier_chunk) if masked else None
            rows = slice(r0, r0 + nr)
            heads = range(2)
            sl = [slice(h * HEAD_PAD, (h + 1) * HEAD_PAD) for h in heads]
            m_old = [carry[3 * h + 1][rows] for h in heads]
            s = [_dot_nt(q2[rows, sl[h]], k_ref[0, pl.ds(off, nk), sl[h]]) for h in heads]
            if masked:
                s = [jnp.where(allowed, s[h], -1e30) for h in heads]
            m_new = [jnp.maximum(m_old[h], jnp.max(s[h], axis=1, keepdims=True)) for h in heads]
            alpha = [jnp.exp2(m_old[h] - m_new[h]) for h in heads]
            p = [jnp.exp2(s[h] - m_new[h]) for h in heads]
            acc = [alpha[h] * carry[3 * h][rows] + _dot(p[h].astype(BF16), v2) for h in heads]
            l = [alpha[h] * carry[3 * h + 2][rows] + jnp.sum(p[h], axis=1, keepdims=True) for h in heads]
            out = []
            for h in heads:
                out += [_put_rows(carry[3 * h], acc[h], r0), _put_rows(carry[3 * h + 1], m_new[h], r0),
                        _put_rows(carry[3 * h + 2], l[h], r0)]
            return tuple(out)

        zero = jnp.zeros((tq, LANES), F32)
        m0 = jnp.full((tq, 1), -1e30, F32)
        l0 = jnp.zeros((tq, 1), F32)
        carry = (zero, m0, l0, zero, m0, l0)
        for r0, nr, c0, nk in _diag_tiles(tq, False):
            carry = tile(qi, carry, r0, nr, c0, nk, True)
        carry = lax.fori_loop(0, qi, lambda j, c: tile(j, c, 0, tq, 0, tq, False), carry)
        y0 = carry[0] / carry[2]
        y1 = carry[3] / carry[5]
        y_ref[0] = jnp.where(lane < MLA_V, y0, y1).astype(BF16)
        lse_ref[0] = jnp.where(lane < MLA_V, carry[1] + jnp.log2(carry[2]), carry[4] + jnp.log2(carry[5]))

    qspec = pl.BlockSpec((1, tq, 2 * HEAD_PAD), lambda b, hp, i: (b, i, hp))
    kspec = pl.BlockSpec((1, S, 2 * HEAD_PAD), lambda b, hp, i: (b, 0, hp))
    vspec = pl.BlockSpec((1, S, LANES), lambda b, hp, i: (b, 0, hp))
    yspec = pl.BlockSpec((1, tq, LANES), lambda b, hp, i: (b, i, hp))
    return pl.pallas_call(
        body, name="mla_fwd", grid=(B, VW // LANES, nq),
        in_specs=[qspec, kspec, vspec, _AFTER],
        out_specs=[yspec, yspec],
        out_shape=[_sds((B, S, VW), BF16), _sds((B, S, VW), F32)],
        compiler_params=_cparams(("parallel", "parallel", "arbitrary")),
    )(qp, kp, mv, after)


def _mla_bwd(qp, kp, mv, y, lse, dy, dm, after):
    B, S, QW = qp.shape
    VW = mv.shape[2]
    tq = dm["tq"]
    nq = S // tq
    scale = (MLA_NOPE + MLA_ROPE) ** -0.5

    def body(q_ref, k_ref, v_ref, y_ref, lse_ref, dy_ref, _, dq_ref, dk_ref, dv_ref, dk_acc, dv_acc):
        qi = pl.program_id(2)

        @pl.when(qi == 0)
        def _():
            dk_acc[...] = jnp.zeros_like(dk_acc)
            dv_acc[...] = jnp.zeros_like(dv_acc)

        q2 = q_ref[0]
        dy2 = dy_ref[0]
        lse2 = lse_ref[0]
        lane = lax.broadcasted_iota(jnp.int32, (tq, LANES), 1)
        in_h = [lane < MLA_V, lane >= MLA_V]
        prod = dy2.astype(F32) * y_ref[0].astype(F32)
        delta = [jnp.sum(jnp.where(m, prod, 0.0), axis=1, keepdims=True) for m in in_h]
        dyh = [jnp.where(m, dy2, 0).astype(BF16) for m in in_h]
        lseh = [lse2[:, 0:1], lse2[:, MLA_V:MLA_V + 1]]
        q_t = _transpose_bf16(q2)
        dy_t = [_transpose_bf16(a) for a in dyh]

        def tile(j, carry, r0, nr, c0, nk, masked):
            off = pl.multiple_of(j * tq + c0, math.gcd(tq, c0))
            v2 = v_ref[0, pl.ds(off, nk), :]
            allowed = _tile_mask(nr, nk, r0, c0, _same_or_earlier_chunk) if masked else None
            rows = slice(r0, r0 + nr)
            keys = slice(c0, c0 + nk)
            heads = range(2)
            sl = [slice(h * HEAD_PAD, (h + 1) * HEAD_PAD) for h in heads]
            qhh = [q2[rows, sl[h]] for h in heads]
            dyr = [dyh[h][rows] for h in heads]
            kh = [k_ref[0, pl.ds(off, nk), sl[h]] for h in heads]
            s = [_dot_nt(qhh[h], kh[h]) for h in heads]
            dp = [_dot_nt(dyr[h], v2) for h in heads]
            if masked:
                s = [jnp.where(allowed, s[h], -1e30) for h in heads]
            p = [jnp.exp2(s[h] - lseh[h][rows]) for h in heads]
            dv_acc[j, :, keys] += (_dot(dy_t[0][:, rows], p[0].astype(BF16))
                                   + _dot(dy_t[1][:, rows], p[1].astype(BF16)))
            ds = [(p[h] * (dp[h] - delta[h][rows])).astype(BF16) for h in heads]
            for h in heads:
                dk_acc[j, sl[h], keys] += _dot(q_t[sl[h], rows], ds[h])
            return tuple(_put_rows(carry[h], carry[h][rows] + _dot(ds[h], kh[h]), r0) for h in heads)

        zero = jnp.zeros((tq, HEAD_PAD), F32)
        carry = lax.fori_loop(0, qi, lambda j, c: tile(j, c, 0, tq, 0, tq, False), (zero, zero))
        for r0, nr, c0, nk in _diag_tiles(tq, True):
            carry = tile(qi, carry, r0, nr, c0, nk, True)
        dq_ref[0] = (jnp.concatenate([carry[0], carry[1]], axis=1) * scale).astype(BF16)

        @pl.when(qi == nq - 1)
        def _():
            for jb in range(nq):
                dk_ref[0, jb * tq:(jb + 1) * tq, :] = (dk_acc[jb].T * (1.0 / LOG2E)).astype(BF16)
                dv_ref[0, jb * tq:(jb + 1) * tq, :] = dv_acc[jb].T.astype(BF16)

    qspec = pl.BlockSpec((1, tq, 2 * HEAD_PAD), lambda b, hp, i: (b, i, hp))
    kspec = pl.BlockSpec((1, S, 2 * HEAD_PAD), lambda b, hp, i: (b, 0, hp))
    vspec = pl.BlockSpec((1, S, LANES), lambda b, hp, i: (b, 0, hp))
    yspec = pl.BlockSpec((1, tq, LANES), lambda b, hp, i: (b, i, hp))
    return pl.pallas_call(
        body, name="mla_bwd", grid=(B, VW // LANES, nq),
        in_specs=[qspec, kspec, vspec, yspec, yspec, yspec, _AFTER],
        out_specs=[qspec, kspec, vspec],
        out_shape=[_sds((B, S, QW), BF16), _sds((B, S, QW), BF16), _sds((B, S, VW), BF16)],
        scratch_shapes=[pltpu.VMEM((nq, 2 * HEAD_PAD, tq), F32), pltpu.VMEM((nq, LANES, tq), F32)],
        compiler_params=_cparams(("parallel", "parallel", "arbitrary")),
    )(qp, kp, mv, y, lse, dy, after)


def _stat_specs(B, D):
    specs = [pl.BlockSpec((1, 8, D), lambda b, s: (b, 0, 0)), pl.BlockSpec((8, D), lambda b, s: (0, 0))]
    shapes = [_sds((B, 8, D), F32), _sds((8, D), F32)]
    return specs, shapes


def _stat_init(bst_ref, wst_ref):
    @pl.when(pl.program_id(1) == 0)
    def _():
        bst_ref[...] = jnp.zeros_like(bst_ref)

    @pl.when((pl.program_id(0) == 0) & (pl.program_id(1) == 0))
    def _():
        wst_ref[...] = jnp.zeros_like(wst_ref)


def _mlp_fwd(sb_y, mla_y, x0, mod, target, w_o, w_up, w_down, ln1_g, ln1_b, ln_g, ln_b, dm):
    B, S, D = x0.shape
    tm = dm["tm"]
    sbw = sb_y.shape[2]
    nck, _, ck = w_up.shape
    dff = nck * ck

    def body(ya_ref, yb_ref, x0_ref, mod_ref, t_ref, wo_ref, wu_ref, wd_ref, g1_ref, b1_ref, g_ref, b_ref,
             mix_ref, x1_ref, h2_ref, u_ref, dr_ref, bst_ref, wst_ref):
        _stat_init(bst_ref, wst_ref)
        mod = mod_ref[0]
        mix = _dot(ya_ref[0], wo_ref[0:sbw, :]) + _dot(yb_ref[0], wo_ref[sbw:, :])
        mix_ref[0] = mix
        x1, _, _ = _ln_fwd(ALPHA * x0_ref[0] + (1.0 + mod[2:3]) * mix, g1_ref[...], b1_ref[...])
        x1_ref[0] = x1
        h2 = (x1 * (1.0 + mod[4:5]) + mod[3:4]).astype(BF16)
        h2_ref[0] = h2
        g = g_ref[...]
        ff = jnp.zeros((tm, D), F32)
        for c in range(nck):
            u = _dot(h2, wu_ref[c])
            u_ref[0, :, c * ck:(c + 1) * ck] = u.astype(BF16)
            act = jnp.square(jnp.maximum(u, 0.0)).astype(BF16)
            ff = ff + _dot(act, wd_ref[c])
        x2, xhat, rstd = _ln_fwd(ALPHA * x1 + (1.0 + mod[5:6]) * ff, g, b_ref[...])
        err = x2 - t_ref[0]
        dy = err * (1.0 / D)
        dr = _ln_bwd(dy, xhat, rstd, g)
        dr_ref[0] = dr
        bst_ref[0, 0:1, :] += _colsum(dr * ff)
        wst_ref[0:1, :] += _colsum(dy * xhat)
        wst_ref[1:2, :] += _colsum(dy)
        wst_ref[2:3, :] += _colsum(err * err) * (0.5 / D)

    sspecs, sshapes = _stat_specs(B, D)
    return pl.pallas_call(
        body, name="mlp_fwd", grid=(B, S // tm),
        in_specs=[_tok(tm, sbw), _tok(tm, mla_y.shape[2]), _tok(tm, D), _perb(N_MOD, D), _tok(tm, D),
                  _full(w_o), _full(w_up), _full(w_down), _full(ln1_g), _full(ln1_b), _full(ln_g), _full(ln_b)],
        out_specs=[_tok(tm, D), _tok(tm, D), _tok(tm, D), _tok(tm, dff), _tok(tm, D)] + sspecs,
        out_shape=[_sds((B, S, D), F32), _sds((B, S, D), F32), _sds((B, S, D), BF16), _sds((B, S, dff), BF16),
                   _sds((B, S, D), F32)] + sshapes,
        compiler_params=_cparams(("arbitrary", "arbitrary")),
    )(sb_y, mla_y, x0, mod, target, w_o, w_up, w_down, ln1_g, ln1_b, ln_g, ln_b)


def _mlp_bwd(dr2, u, x1, x0, mix, sb_y, mla_y, mod, w_up, w_down, w_o, ln_g, dm):
    B, S, D = x1.shape
    tm = dm["tm_small"]
    sbw = dm["sbw"]
    nck, _, ck = w_up.shape
    dff = nck * ck

    def body(dr_ref, u_ref, x1_ref, x0_ref, mix_ref, ya_ref, yb_ref, mod_ref, wu_ref, wd_ref, wo_ref, g_ref,
             du_ref, dff_ref, dx0_ref, dya_ref, dyb_ref, go_ref, bst_ref, wst_ref):
        _stat_init(bst_ref, wst_ref)

        @pl.when((pl.program_id(0) == 0) & (pl.program_id(1) == 0))
        def _():
            go_ref[...] = jnp.zeros_like(go_ref)

        mod = mod_ref[0]
        dr2 = dr_ref[0]
        dffv = ((1.0 + mod[5:6]) * dr2).astype(BF16)
        dff_ref[0] = dffv
        dh2 = jnp.zeros((tm, D), F32)
        for c in range(nck):
            sl = slice(c * ck, (c + 1) * ck)
            da = _dot_nt(dffv, wd_ref[c])
            du = (da * (2.0 * jnp.maximum(u_ref[0, :, sl].astype(F32), 0.0))).astype(BF16)
            du_ref[0, :, sl] = du
            dh2 = dh2 + _dot_nt(du, wu_ref[c])
        x1 = x1_ref[0]
        dx1 = ALPHA * dr2 + dh2 * (1.0 + mod[4:5])
        bst_ref[0, 0:1, :] += _colsum(dh2 * x1)
        bst_ref[0, 1:2, :] += _colsum(dh2)
        mix = mix_ref[0]
        g = g_ref[...]
        _, xhat, rstd = _ln_fwd(ALPHA * x0_ref[0] + (1.0 + mod[2:3]) * mix, g, 0.0)
        dr1 = _ln_bwd(dx1, xhat, rstd, g)
        wst_ref[0:1, :] += _colsum(dx1 * xhat)
        wst_ref[1:2, :] += _colsum(dx1)
        bst_ref[0, 2:3, :] += _colsum(dr1 * mix)
        dx0_ref[0] = ALPHA * dr1
        dmix = ((1.0 + mod[2:3]) * dr1).astype(BF16)
        dya_ref[0] = _dot_nt(dmix, wo_ref[0:sbw, :]).astype(BF16)
        dyb_ref[0] = _dot_nt(dmix, wo_ref[sbw:, :]).astype(BF16)
        go_ref[0:sbw, :] += _dot_tn(ya_ref[0], dmix)
        go_ref[sbw:, :] += _dot_tn(yb_ref[0], dmix)

    sspecs, sshapes = _stat_specs(B, D)
    wa, wb = sbw, w_o.shape[0] - sbw
    return pl.pallas_call(
        body, name="mlp_bwd", grid=(B, S // tm),
        in_specs=[_tok(tm, D), _tok(tm, dff), _tok(tm, D), _tok(tm, D), _tok(tm, D), _tok(tm, wa), _tok(tm, wb),
                  _perb(N_MOD, D), _full(w_up), _full(w_down), _full(w_o), _full(ln_g)],
        out_specs=[_tok(tm, dff), _tok(tm, D), _tok(tm, D), _tok(tm, wa), _tok(tm, wb),
                   pl.BlockSpec(w_o.shape, lambda b, s: (0, 0))] + sspecs,
        out_shape=[_sds((B, S, dff), BF16), _sds((B, S, D), BF16), _sds((B, S, D), F32),
                   _sds((B, S, wa), BF16), _sds((B, S, wb), BF16), _sds(w_o.shape, F32)] + sshapes,
        compiler_params=_cparams(("arbitrary", "arbitrary")),
    )(dr2, u, x1, x0, mix, sb_y, mla_y, mod, w_up, w_down, w_o, ln_g)


def _inproj_bwd(x, x0, dx0a, mod, ln_g, dq, dk, dv, dqp, dkp, dmv, cq, ckv, qn, kvn, w_in_p, w_uq_p, w_kv, gq, gkv,
                tc, ts1, ts2, dm):
    B, S, D = x.shape
    tm = dm["tm"]
    sbw, qr, kvr, nh = dm["sbw"], dm["qr"], dm["kvr"], dm["nh"]
    qpw = nh * HEAD_PAD
    dinp = w_in_p.shape[0]
    kvw = w_kv.shape[1]

    def body(x_ref, x0_ref, dx0a_ref, mod_ref, g_ref, dq_ref, dk_ref, dv_ref, dqp_ref, dkp_ref, dmv_ref,
             cq_ref, ckv_ref, qn_ref, kvn_ref, win_ref, wuq_ref, wkv_ref, gq_ref, gkv_ref, tc_ref, ts1_ref, ts2_ref,
             gx_ref, gin_ref, guq_ref, gwkv_ref, bst_ref, wst_ref):
        _stat_init(bst_ref, wst_ref)

        @pl.when((pl.program_id(0) == 0) & (pl.program_id(1) == 0))
        def _():
            gin_ref[...] = jnp.zeros_like(gin_ref)
            guq_ref[...] = jnp.zeros_like(guq_ref)
            gwkv_ref[...] = jnp.zeros_like(gwkv_ref)

        mod = mod_ref[0]
        c1, s1, s2 = tc_ref[...], ts1_ref[...], ts2_ref[...]
        c8, s18, s28 = jnp.tile(c1, (1, nh)), jnp.tile(s1, (1, nh)), jnp.tile(s2, (1, nh))
        dqpre = _rope_t(dqp_ref[0].astype(F32), c8, s18, s28).astype(BF16)
        guq_ref[...] += _dot_tn(qn_ref[0], dqpre)
        gq = gq_ref[...]
        cq = cq_ref[0]
        rq = lax.rsqrt(jnp.mean(cq * cq, axis=-1, keepdims=True) + RMS_EPS)
        dqn = _dot_nt(dqpre, wuq_ref[...])
        wst_ref[4:5, 0:qr] += _colsum(dqn * cq * rq)
        dqg = dqn * gq
        dcq = rq * dqg - cq * (rq * rq * rq) * jnp.mean(dqg * cq, axis=-1, keepdims=True)

        dkpre = _rope_t(dkp_ref[0].astype(F32), c8, s18, s28)
        dkr = dkpre[:, 0:HEAD_PAD]
        for h in range(1, nh):
            dkr = dkr + dkpre[:, h * HEAD_PAD:(h + 1) * HEAD_PAD]
        lane = lax.broadcasted_iota(jnp.int32, (tm, LANES), 1)
        dkr = jnp.where((lane >= MLA_NOPE) & (lane < MLA_NOPE + MLA_ROPE), dkr, 0.0)
        dkr = pltpu.roll(dkr, LANES - MLA_NOPE, 1)
        dkvo = jnp.concatenate([dkpre.astype(BF16), dmv_ref[0]], axis=1)
        gwkv_ref[...] += _dot_tn(kvn_ref[0], dkvo)
        gkv = gkv_ref[...]
        ckv = ckv_ref[0]
        rkv = lax.rsqrt(jnp.mean(ckv * ckv, axis=-1, keepdims=True) + RMS_EPS)
        dkvn = _dot_nt(dkvo, wkv_ref[...])
        wst_ref[5:6, 0:kvr] += _colsum(dkvn * ckv * rkv)
        dkg = dkvn * gkv
        dckv = rkv * dkg - ckv * (rkv * rkv * rkv) * jnp.mean(dkg * ckv, axis=-1, keepdims=True)

        dproj = jnp.concatenate([dq_ref[0], dk_ref[0], dv_ref[0], dcq.astype(BF16), dckv.astype(BF16),
                                 dkr.astype(BF16)], axis=1)
        dh = _dot(dproj, win_ref[...])
        x0 = x0_ref[0]
        gin_ref[...] += _dot_tn(dproj, (x0 * (1.0 + mod[1:2]) + mod[0:1]).astype(BF16))
        dx0 = dx0a_ref[0] + dh * (1.0 + mod[1:2])
        bst_ref[0, 0:1, :] += _colsum(dh * x0)
        bst_ref[0, 1:2, :] += _colsum(dh)
        g = g_ref[...]
        _, xhat, rstd = _ln_fwd(x_ref[0], g, 0.0)
        gx_ref[0] = _ln_bwd(dx0, xhat, rstd, g)
        wst_ref[0:1, :] += _colsum(dx0 * xhat)
        wst_ref[1:2, :] += _colsum(dx0)

    tab = pl.BlockSpec((tm, LANES), lambda b, s: (s, 0))
    sspecs, sshapes = _stat_specs(B, D)
    return pl.pallas_call(
        body, name="inproj_bwd", grid=(B, S // tm),
        in_specs=[_tok(tm, D), _tok(tm, D), _tok(tm, D), _perb(N_MOD, D), _full(ln_g),
                  _tok(tm, sbw), _tok(tm, sbw), _tok(tm, sbw), _tok(tm, qpw), _tok(tm, qpw), _tok(tm, nh * MLA_V),
                  _tok(tm, qr), _tok(tm, kvr), _tok(tm, qr), _tok(tm, kvr),
                  _full(w_in_p), _full(w_uq_p), _full(w_kv), _full(gq), _full(gkv), tab, tab, tab],
        out_specs=[_tok(tm, D), pl.BlockSpec((dinp, D), lambda b, s: (0, 0)),
                   pl.BlockSpec((qr, qpw), lambda b, s: (0, 0)), pl.BlockSpec((kvr, kvw), lambda b, s: (0, 0))] + sspecs,
        out_shape=[_sds((B, S, D), F32), _sds((dinp, D), F32), _sds((qr, qpw), F32),
                   _sds((kvr, kvw), F32)] + sshapes,
        compiler_params=_cparams(("arbitrary", "arbitrary")),
    )(x, x0, dx0a, mod, ln_g, dq, dk, dv, dqp, dkp, dmv, cq, ckv, qn, kvn, w_in_p, w_uq_p, w_kv, gq, gkv,
      tc, ts1, ts2)


def _tile_of(n, cap):
    if n <= cap:
        return n
    best = n
    for t in range(LANES, cap + 1, LANES):
        if n % t == 0:
            best = t
    return best


def _mm_tn(a, g, name, after, relu_sq=False, out_dtype=F32, col_blocks=None):
    T, K = a.shape
    N = g.shape[1]
    tt = 2048 if T % 2048 == 0 else (512 if T % 512 == 0 else T)
    tk = _tile_of(K, 1024)
    tn = _tile_of(N, 1280)
    nt = T // tt
    bw = N // col_blocks if col_blocks else tn
    assert tn % bw == 0

    def body(a_ref, g_ref, _, o_ref, acc_ref):
        @pl.when(pl.program_id(2) == 0)
        def _():
            acc_ref[...] = jnp.zeros_like(acc_ref)

        av = a_ref[...]
        if relu_sq:
            av = jnp.square(jnp.maximum(av.astype(F32), 0.0)).astype(BF16)
        acc_ref[...] += _dot_tn(av, g_ref[...])

        @pl.when(pl.program_id(2) == nt - 1)
        def _():
            if col_blocks:
                for c in range(tn // bw):
                    o_ref[c] = acc_ref[:, c * bw:(c + 1) * bw].astype(out_dtype)
            else:
                o_ref[...] = acc_ref[...].astype(out_dtype)

    if col_blocks:
        out_spec = pl.BlockSpec((tn // bw, tk, bw), lambda i, j, t: (j, i, 0))
        out_shape = _sds((col_blocks, K, bw), out_dtype)
    else:
        out_spec = pl.BlockSpec((tk, tn), lambda i, j, t: (i, j))
        out_shape = _sds((K, N), out_dtype)
    return pl.pallas_call(
        body, name=name, grid=(K // tk, N // tn, nt),
        in_specs=[pl.BlockSpec((tt, tk), lambda i, j, t: (t, i)), pl.BlockSpec((tt, tn), lambda i, j, t: (t, j)),
                  _AFTER],
        out_specs=out_spec, out_shape=out_shape,
        scratch_shapes=[pltpu.VMEM((tk, tn), F32)],
        compiler_params=_cparams(("parallel", "parallel", "arbitrary")),
    )(a, g, after)


def _reduce_adamw(parts, w, m, v, name):
    P, K, N = parts.shape
    tr = 256 if K % 256 == 0 else K

    def body(p_ref, w_ref, m_ref, v_ref, g_ref, d_ref, nm_ref, nv_ref):
        g = p_ref[0].astype(F32)
        for k in range(1, P):
            g = g + p_ref[k].astype(F32)
        g_ref[0] = g
        d_ref[0], nm_ref[0], nv_ref[0] = _adamw(w_ref[0], g, m_ref[0], v_ref[0])

    spec = pl.BlockSpec((1, tr, N), lambda r: (0, r, 0))
    return pl.pallas_call(
        body, name=name, grid=(K // tr,),
        in_specs=[pl.BlockSpec((P, tr, N), lambda r: (0, r, 0)), spec, spec, spec],
        out_specs=[spec] * 4, out_shape=[_sds((1, K, N), F32)] * 4,
        compiler_params=_cparams(("parallel",)),
    )(parts, w, m, v)


def _finish(sm, dmod_all, dmod_my, cact_all, p_small, m_small, v_small, b_ada, m_b, v_b, w_ada, m_w, v_w):
    n0 = p_small.shape[1]
    n1 = sm.shape[1]
    d = cact_all.shape[1]

    def body(sm_ref, dma_ref, dmm_ref, ca_ref, p_ref, pm_ref, pv_ref, b_ref, bm_ref, bv_ref, w_ref, wm_ref, wv_ref,
             gs_ref, ds_ref, ms_ref, vs_ref, gb_ref, db_ref, mb_ref, vb_ref, gw_ref, dw_ref, mw_ref, vw_ref,
             loss_ref):
        gs = sm_ref[0:1, :]
        for k in range(1, N_DEV):
            gs = gs + sm_ref[k:k + 1, :]
        gs_ref[...] = gs
        ds_ref[...], ms_ref[...], vs_ref[...] = _adamw(p_ref[...], gs[:, 0:n0], pm_ref[...], pv_ref[...])
        loss_ref[...] = jnp.zeros((1, LANES), F32) + jnp.sum(gs[:, n1 - d:n1])
        gb = jnp.sum(dma_ref[...], axis=0, keepdims=True)
        gb_ref[...] = gb
        db_ref[...], mb_ref[...], vb_ref[...] = _adamw(b_ref[...], gb, bm_ref[...], bv_ref[...])
        gw = _dot_tn(ca_ref[...].astype(BF16), dmm_ref[...].astype(BF16))
        gw_ref[...] = gw
        dw_ref[...], mw_ref[...], vw_ref[...] = _adamw(w_ref[...], gw, wm_ref[...], wv_ref[...])

    s0 = _sds(p_small.shape, F32)
    sb = _sds(b_ada.shape, F32)
    sw = _sds(w_ada.shape, F32)
    return pl.pallas_call(
        body, name="finish_small",
        out_shape=[_sds((1, n1), F32), s0, s0, s0, sb, sb, sb, sb, sw, sw, sw, sw,
                   _sds((1, LANES), F32)],
        compiler_params=pltpu.CompilerParams(vmem_limit_bytes=VMEM_LIMIT),
    )(sm, dmod_all, dmod_my, cact_all, p_small, m_small, v_small, b_ada, m_b, v_b, w_ada, m_w, v_w)


def _pack(arrs, dtype, width):
    flat = jnp.concatenate([a.astype(dtype).reshape(-1) for a in arrs])
    rows = -(-flat.shape[0] // (256 * width)) * 256
    return jnp.pad(flat, (0, rows * width - flat.shape[0])).reshape(rows, width)


def _unpack(slab, shapes):
    flat = slab.reshape(-1)
    out, o = [], 0
    for s in shapes:
        n = math.prod(s)
        out.append(flat[o:o + n].reshape(s))
        o += n
    return out


def _rope_tables(S):
    inv_freq = 1.0 / (ROPE_BASE ** (jnp.arange(0, MLA_ROPE, 2, dtype=F32) / MLA_ROPE))
    ang = jnp.arange(S, dtype=F32)[:, None] * inv_freq[None, :]
    cos, sin = jnp.cos(ang), jnp.sin(ang)
    one = jnp.ones((S, MLA_NOPE), F32)
    z16 = jnp.zeros((S, 16), F32)
    z32 = jnp.zeros((S, 32), F32)
    z64 = jnp.zeros((S, MLA_NOPE), F32)
    tc = jnp.concatenate([one, cos, cos, jnp.ones((S, 32), F32)], axis=1)
    ts1 = jnp.concatenate([z64, -sin, z16, z32], axis=1)
    ts2 = jnp.concatenate([z64, z16, sin, z32], axis=1)
    return tc, ts1, ts2


def kernel(x, c, ln_in_g, ln_in_b, w_ada, b_ada, w_in, q_norm_g, kv_norm_g, w_uq, w_ukv, w_o, ln1_g, ln1_b, w_up, w_down, ln2_g, ln2_b, loss_target, m_ln_in_g, m_ln_in_b, m_w_ada, m_b_ada, m_w_in, m_q_norm_g, m_kv_norm_g, m_w_uq, m_w_ukv, m_w_o, m_ln1_g, m_ln1_b, m_w_up, m_w_down, m_ln2_g, m_ln2_b, v_ln_in_g, v_ln_in_b, v_w_ada, v_b_ada, v_w_in, v_q_norm_g, v_kv_norm_g, v_w_uq, v_w_ukv, v_w_o, v_ln1_g, v_ln1_b, v_w_up, v_w_down, v_ln2_g, v_ln2_b):
    B, S, D = x.shape
    sbw = D // 2
    mlw = D - sbw
    nh = mlw // MLA_V
    qr = w_uq.shape[1]
    kvr = w_ukv.shape[1]
    qk = MLA_NOPE + MLA_ROPE
    dff = w_up.shape[2] * N_DEV
    din = w_in.shape[2] * N_DEV
    tm = 512 if S % 512 == 0 else S
    tq = min(512, S // 2)
    dm = dict(tm=tm, tm_small=min(tm, 256), tq=tq, sbw=sbw, qr=qr, kvr=kvr, nh=nh)
    dev =4 * lax.axis_index("x") + 2 * lax.axis_index("y") + lax.axis_index("c")

    big = [w_in, w_uq, w_ukv, w_o, w_up, w_down]
    first = [w_in[0].T.astype(BF16), w_uq[0].astype(BF16), w_ukv[0].astype(BF16)]
    first_w, first_token = _chip_exchange_start(first, "gather_w_first_start", scatter=False, after=c)

    nada = w_ada.shape[2]
    c_all = _all_gather([c + first_token[0, 0]], "gather_c")[0].reshape(N_DEV * B, D)
    b_loc = lax.dynamic_slice(b_ada, (0, dev * nada), (1, nada))
    cact_all, mod_part = _ada_partial(c_all, w_ada[0], b_loc)
    mod_all = _all_gather([mod_part], "gather_mod")[0]
    mod = lax.dynamic_slice(mod_all, (0, dev * B, 0), (N_DEV, B, nada))
    mod = jnp.swapaxes(mod, 0, 1).reshape(B, N_MOD, D)

    first_by_chip = _chip_exchange_wait(first_w, mod_all, "gather_w_first_wait")
    w_in8, w_uq8, w_ukv8 = [b.reshape((N_DEV,) + b.shape[2:]) for b in _core_gather(first_by_chip, "gather_w_first_cores")]
    late_w, late_token = _chip_exchange_start([a[0].astype(BF16) for a in big[3:]], "gather_w_late_start",
                                              scatter=False, after=w_in8, everyone=True)
    cols = lambda a8: jnp.swapaxes(a8, 0, 1).reshape(a8.shape[1], N_DEV * a8.shape[2])
    w_in_p = jnp.pad(w_in8.reshape(din, D), ((0, LANES - MLA_ROPE), (0, 0)))
    zpad = jnp.zeros((qr, nh, HEAD_PAD - qk), BF16)
    w_uq_p = jnp.concatenate([cols(w_uq8).reshape(qr, nh, qk), zpad], axis=2).reshape(qr, nh * HEAD_PAD)
    w_ukv_f = cols(w_ukv8)
    w_uk = w_ukv_f[:, :nh * MLA_NOPE].reshape(kvr, nh, MLA_NOPE)
    w_uk_p = jnp.concatenate([w_uk, jnp.zeros((kvr, nh, HEAD_PAD - MLA_NOPE), BF16)], axis=2)
    w_kv = jnp.concatenate([w_uk_p.reshape(kvr, nh * HEAD_PAD), w_ukv_f[:, nh * MLA_NOPE:]], axis=1)

    tc, ts1, ts2 = _rope_tables(S)
    g_in, b_in = ln_in_g.reshape(1, D), ln_in_b.reshape(1, D)
    (x0, sq, sk, sv, qp, kp, mv, cq, ckv, qn, kvn) = _inproj_fwd(
        x, mod, g_in, b_in, w_in_p, w_uq_p, w_kv, q_norm_g, kv_norm_g, tc, ts1, ts2, dm, late_token)
    sb_y, sb_tot = _sb_fwd(sq, sk, sv, dm)
    mla_y, mla_lse = _mla_fwd(qp, kp, mv, dm, sb_tot)
    w_o8, w_up8, w_down8 = _chip_exchange_wait(late_w, mla_lse, "gather_w_late_wait")
    w_o_f = w_o8.reshape(D, D)
    mix, x1, h2, u, dr2, bst_c, wst_c = _mlp_fwd(sb_y, mla_y, x0, mod, loss_target, w_o_f, w_up8, w_down8,
                                                  ln1_g, ln1_b, ln2_g, ln2_b, dm)

    du, dffb, dx0a, dsb_y, dmla_y, g_o, bst_b, wst_b = _mlp_bwd(
        dr2, u, x1, x0, mix, sb_y, mla_y, mod, w_up8, w_down8, w_o_f, ln1_g, dm)
    T = B * S
    r2 = lambda a: a.reshape(T, a.shape[2])
    by_core = lambda a: a.reshape((4, 2) + a.shape[1:])
    g_o = g_o.astype(BF16)
    g_up8 = _mm_tn(r2(h2), r2(du), "grad_w_up", dr2, out_dtype=BF16, col_blocks=N_DEV)
    g_down = _mm_tn(r2(u), r2(dffb), "grad_w_down", dr2, relu_sq=True, out_dtype=BF16)
    early = [g_o.reshape(N_DEV, D // N_DEV, D), g_up8, g_down.reshape(N_DEV, dff // N_DEV, D)]
    early_g, early_token = _chip_exchange_start(early, "scatter_g_early_start", scatter=True, after=dr2,
                                                everyone=True)

    dsq, dsk, dsv = _sb_bwd(sq, sk, sv, sb_tot, dsb_y, dm, early_token)
    dqp, dkp, dmv = _mla_bwd(qp, kp, mv, mla_y, mla_lse, dmla_y, dm, dsq)
    grad_x, g_in_p, g_uq_p, g_kv, bst_a, wst_a = _inproj_bwd(
        x, x0, dx0a, mod, g_in, dsq, dsk, dsv, dqp, dkp, dmv, cq, ckv, qn, kvn, w_in_p, w_uq_p, w_kv,
        q_norm_g, kv_norm_g, tc, ts1, ts2, dm)

    dmod = jnp.concatenate([bst_a[:, 1], bst_a[:, 0], bst_b[:, 2], bst_b[:, 1], bst_b[:, 0], bst_c[:, 0]], axis=1)
    small = jnp.concatenate([wst_a[0], wst_a[1], wst_a[4, :qr], wst_a[5, :kvr], wst_b[0], wst_b[1],
                             wst_c[0], wst_c[1], wst_c[2]])
    n1 = small.shape[0]
    small_g, small_token = _chip_exchange_start([_pack([dmod, small], F32, LANES)], "gather_small_start",
                                                scatter=False, after=grad_x)
    g_uq_f = g_uq_p.reshape(qr, nh, HEAD_PAD)[:, :, :qk].reshape(qr, nh * qk)
    g_uk = g_kv[:, :nh * HEAD_PAD].reshape(kvr, nh, HEAD_PAD)[:, :, :MLA_NOPE].reshape(kvr, nh * MLA_NOPE)
    g_ukv_f = jnp.concatenate([g_uk, g_kv[:, nh * HEAD_PAD:]], axis=1)
    early_quarter = _chip_exchange_wait(early_g, g_in_p, "scatter_g_early_wait")

    def by_dest_cols(a):
        k, n = a.shape[0], a.shape[1] // N_DEV
        return jnp.swapaxes(a.reshape(k, N_DEV, n), 0, 1).astype(BF16)

    g_in8 = (g_in_p[:din] + small_token[0, 0]).astype(BF16).reshape(N_DEV, din // N_DEV, D)
    last = [g_in8, by_dest_cols(g_uq_f), by_dest_cols(g_ukv_f)]
    last_sum = _core_scatter_sum([by_core(a) for a in last], "scatter_g_last_cores")
    small_by_chip = _chip_exchange_wait(small_g, last_sum[0], "gather_small_wait")
    both = _core_gather(small_by_chip, "gather_small_cores")[0].reshape(N_DEV, -1)
    last_g, last_token = _chip_exchange_start(last_sum, "scatter_g_last_start", scatter=True, after=grad_x)
    names = ["w_in", "w_uq", "w_ukv", "w_o", "w_up", "w_down"]
    moms = [m_w_in, m_w_uq, m_w_ukv, m_w_o, m_w_up, m_w_down]
    vars_ = [v_w_in, v_w_uq, v_w_ukv, v_w_o, v_w_up, v_w_down]
    res_early = [_reduce_adamw(p, w, m, v, "adamw_" + n)
                 for p, w, m, v, n in zip(early_quarter, big[3:], moms[3:], vars_[3:], names[3:])]

    dmod_all = both[:, :B * N_MOD * D].reshape(N_DEV * B, N_MOD * D)
    sm = both[:, B * N_MOD * D:B * N_MOD * D + n1] + last_token[0, 0]
    dmod_my = lax.dynamic_slice(dmod_all, (0, dev * nada), (N_DEV * B, nada))
    row = lambda arrs: jnp.concatenate([a.reshape(1, -1) for a in arrs], axis=1)
    smalls = [ln_in_g, ln_in_b, q_norm_g, kv_norm_g, ln1_g, ln1_b, ln2_g, ln2_b]
    small_shapes = [a.shape for a in smalls]
    (gs, ds, nms, nvs, g_b, d_b, nm_b, nv_b, g_w, d_w, nm_w, nv_w, loss_v) = _finish(
        sm, dmod_all, dmod_my, cact_all, row(smalls),
        row([m_ln_in_g, m_ln_in_b, m_q_norm_g, m_kv_norm_g, m_ln1_g, m_ln1_b, m_ln2_g, m_ln2_b]),
        row([v_ln_in_g, v_ln_in_b, v_q_norm_g, v_kv_norm_g, v_ln1_g, v_ln1_b, v_ln2_g, v_ln2_b]),
        b_ada, m_b_ada, v_b_ada, w_ada[0], m_w_ada[0], v_w_ada[0])
    gsm, dsm, nmsm, nvsm = (_unpack(s, small_shapes) for s in (gs, ds, nms, nvs))
    last_quarter = list(_chip_exchange_wait(last_g, loss_v, "scatter_g_last_wait"))
    last_quarter[0] = jnp.swapaxes(last_quarter[0], 1, 2)
    res_last =[_reduce_adamw(p, w, m, v, "adamw_" + n)
                for p, w, m, v, n in zip(last_quarter, big[:3], moms[:3], vars_[:3], names[:3])]
    gb, db, nmb, nvb = ([r[i] for r in res_last + res_early] for i in range(4))

    def ordered(sm_l, w_l, ada_w, ada_b):
        return [sm_l[0], sm_l[1], ada_w[None], ada_b, w_l[0], sm_l[2], sm_l[3], w_l[1], w_l[2], w_l[3],
                sm_l[4], sm_l[5], w_l[4], w_l[5], sm_l[6], sm_l[7]]

    loss = loss_v[0, 0]
    return (loss, grad_x, *ordered(gsm, gb, g_w, g_b), *ordered(dsm, db, d_w, d_b),
            *ordered(nmsm, nmb, nm_w, nm_b), *ordered(nvsm, nvb, nv_w, nv_b))
```

```python
import math

import jax
import jax.numpy as jnp
from jax import lax
from jax.experimental import pallas as pl
from jax.experimental.pallas import tpu as pltpu

F32 = jnp.float32
BF16 = jnp.bfloat16

SB_HD = 64
MLA_V = 64
MLA_NOPE = 64
MLA_ROPE = 32
HEAD_PAD = 128
CHUNK = 64
ROPE_BASE = 10000.0
LN_EPS = 1e-5
RMS_EPS = 1e-6
DEPTH = 1
ALPHA = (2.0 * DEPTH) ** 0.25
N_MOD = 6
ADAM_LR = 0.001
ADAM_B1 = 0.9
ADAM_B2 = 0.999
ADAM_EPS = 1e-08
ADAM_WD = 0.01
ADAM_STEP = 10
N_DEV = 8
LANES = 128
LOG2E = 1.4426950408889634
CUMSUM_W = 256
VMEM_LIMIT = 56 * 1024 * 1024
MESH = pl.DeviceIdType.MESH


def _dot(a, b):
    return jnp.dot(a, b, preferred_element_type=F32)


def _dot_nt(a, b):
    return lax.dot_general(a, b, (((1,), (1,)), ((), ())), preferred_element_type=F32)


def _dot_tn(a, b):
    return lax.dot_general(a, b, (((0,), (0,)), ((), ())), preferred_element_type=F32)


def _cparams(sem):
    return pltpu.CompilerParams(dimension_semantics=sem, vmem_limit_bytes=VMEM_LIMIT)


def _full(a):
    nd = a.ndim
    return pl.BlockSpec(a.shape, lambda *_: (0,) * nd, pipeline_mode=pl.Buffered(1))


def _tok(tm, w):
    return pl.BlockSpec((1, tm, w), lambda b, s: (b, s, 0))


def _perb(rows, w):
    return pl.BlockSpec((1, rows, w), lambda b, s: (b, 0, 0))


def _sds(shape, dtype):
    return jax.ShapeDtypeStruct(shape, dtype)


def _ln_fwd(x, g, b):
    mu = jnp.mean(x, axis=-1, keepdims=True)
    xc = x - mu
    var = jnp.mean(xc * xc, axis=-1, keepdims=True)
    rstd = lax.rsqrt(var + LN_EPS)
    xhat = xc * rstd
    return xhat * g + b, xhat, rstd


def _ln_bwd(dy, xhat, rstd, g):
    dxh = dy * g
    m1 = jnp.mean(dxh, axis=-1, keepdims=True)
    m2 = jnp.mean(dxh * xhat, axis=-1, keepdims=True)
    return rstd * (dxh - m1 - xhat * m2)


def _colsum(a):
    return jnp.sum(a, axis=0, keepdims=True)


def _rope(x, c, s1, s2):
    w = x.shape[-1]
    return x * c + pltpu.roll(x, w - 16, 1) * s1 + pltpu.roll(x, 16, 1) * s2


def _rope_t(x, c, s1, s2):
    w = x.shape[-1]
    return x * c - pltpu.roll(x, w - 16, 1) * s1 - pltpu.roll(x, 16, 1) * s2


def _adamw(w, g, m, v):
    m = ADAM_B1 * m + (1.0 - ADAM_B1) * g
    v = ADAM_B2 * v + (1.0 - ADAM_B2) * (g * g)
    m_hat = m / (1.0 - ADAM_B1 ** ADAM_STEP)
    v_hat = v / (1.0 - ADAM_B2 ** ADAM_STEP)
    delta = -ADAM_LR * (m_hat / (jnp.sqrt(v_hat) + ADAM_EPS) + ADAM_WD * w)
    return delta, m, v


def _my_place():
    return lax.axis_index("x"), lax.axis_index("y"), lax.axis_index("c")


def _chip_peers(mx, my):
    out = []
    for j in (1, 2, 3):
        px = 1 - mx if (j >> 1) else mx
        py = 1 - my if (j & 1) else my
        out.append((px, py, 2 * px + py))
    return out


def _split_peers(everyone):
    mx, my, mc = _my_place()
    if not everyone:
        return [(px, py, mc, pk) for px, py, pk in _chip_peers(mx, my)], 2 * mx + my
    peers = []
    for j in range(1, N_DEV):
        px = 1 - mx if (j >> 2) & 1 else mx
        py = 1 - my if (j >> 1) & 1 else my
        pc = 1 - mc if j & 1 else mc
        peers.append((px, py, pc, 4 * px + 2 * py + pc))
    return peers, 4 * mx + 2 * my + mc


def _hbm_call(body, name, n_in, out_shape, sems):
    hbm = pl.BlockSpec(memory_space=pl.ANY)
    return pl.pallas_call(
        body, name=name, out_shape=out_shape,
        in_specs=[hbm] * n_in, out_specs=[hbm] * len(out_shape),
        scratch_shapes=[pltpu.SemaphoreType.DMA(s) for s in sems])


def _chip_exchange(xs, name, scatter):
    n = len(xs)

    def body(*refs):
        x_refs, o_refs = refs[:n], refs[n:2 * n]
        ssem, rsem, lsem = refs[2 * n:]
        mx, my, mc = _my_place()
        me = 2 * mx + my
        peers = _chip_peers(mx, my)

        def copy(i, j, src_slot, dst_slot):
            px, py, _ = peers[j]
            return pltpu.make_async_remote_copy(
                src_ref=x_refs[i].at[src_slot] if scatter else x_refs[i], dst_ref=o_refs[i].at[dst_slot],
                send_sem=ssem.at[i, j], recv_sem=rsem.at[i, j], device_id=(px, py, mc), device_id_type=MESH)

        local = [pltpu.make_async_copy(x_refs[i].at[me] if scatter else x_refs[i], o_refs[i].at[me], lsem.at[i])
                 for i in range(n)]
        sends = [copy(i, j, peers[j][2], me) for i in range(n) for j in range(3)]
        for cp in local + sends:
            cp.start()
        for i in range(n):
            for j in range(3):
                copy(i, j, peers[j][2], peers[j][2]).wait_recv()
        for cp in sends:
            cp.wait_send()
        for cp in local:
            cp.wait()

    out_shape = [_sds((4,) + tuple(x.shape[1:] if scatter else x.shape), x.dtype) for x in xs]
    return _hbm_call(body, name, n, out_shape, [(n, 3), (n, 3), (n,)])(*xs)


def _chip_exchange_start(xs, name, scatter, after, everyone=False):
    n = len(xs)
    npeer = N_DEV - 1 if everyone else 3
    blks = [tuple(x.shape[1:] if scatter else x.shape) for x in xs]

    def body(*refs):
        x_refs, land_refs = refs[:n], refs[n:2 * n]
        ssem, rsem = refs[2 * n + 1], refs[2 * n + 2]
        token = refs[-1]
        peers, me = _split_peers(everyone)
        for i in range(n):
            for j, (px, py, pc, slot) in enumerate(peers):
                pltpu.make_async_remote_copy(
                    src_ref=x_refs[i].at[slot] if scatter else x_refs[i], dst_ref=land_refs[i].at[me],
                    send_sem=ssem.at[npeer * i + j], recv_sem=rsem.at[npeer * i + j], device_id=(px, py, pc),
                    device_id_type=MESH).start()
        token[...] = jnp.zeros_like(token)

    hbm = pl.BlockSpec(memory_space=pltpu.HBM)
    sem = pl.BlockSpec(memory_space=pltpu.SEMAPHORE)
    lands = [lax.empty((npeer + 1,) + b, x.dtype) for b, x in zip(blks, xs)]
    res = pl.pallas_call(
        body, name=name,
        out_shape=[pltpu.SemaphoreType.DMA((npeer * n,)), pltpu.SemaphoreType.DMA((npeer * n,))]
        + [pltpu.HBM(x.shape, x.dtype) for x in xs] + [pltpu.HBM(l.shape, l.dtype) for l in lands]
        + [_sds((8, LANES), F32)],
        in_specs=[hbm] * (2 * n) + [_AFTER],
        out_specs=[sem, sem] + [hbm] * (2 * n) + [pl.BlockSpec(memory_space=pltpu.VMEM)],
        input_output_aliases={i: 2 + i for i in range(2 * n)},
        compiler_params=pltpu.CompilerParams(has_side_effects=pltpu.SideEffectType.DATAFLOW_SIDE_EFFECTING),
    )(*[pltpu.with_memory_space_constraint(a, pltpu.HBM) for a in list(xs) + lands], after)
    return dict(ssem=res[0], rsem=res[1], xs=res[2:2 + n], lands=res[2 + n:2 + 2 * n], n=n, scatter=scatter,
                everyone=everyone), res[-1]


def _chip_exchange_wait(handle, after, name):
    n, scatter, everyone = handle["n"], handle["scatter"], handle["everyone"]
    npeer = N_DEV - 1 if everyone else 3

    def body(*refs):
        x_refs, land_refs = refs[:n], refs[n:2 * n]
        ssem, rsem = refs[2 * n], refs[2 * n + 1]
        peers, _ = _split_peers(everyone)
        for i in range(n):
            for j, (px, py, pc, slot) in enumerate(peers):
                cp = pltpu.make_async_remote_copy(
                    src_ref=x_refs[i].at[slot] if scatter else x_refs[i], dst_ref=land_refs[i].at[slot],
                    send_sem=ssem.at[npeer * i + j], recv_sem=rsem.at[npeer * i + j], device_id=(px, py, pc),
                    device_id_type=MESH)
                cp.wait_send()
                cp.wait_recv()

    hbm = pl.BlockSpec(memory_space=pltpu.HBM)
    sem = pl.BlockSpec(memory_space=pltpu.SEMAPHORE)
    ops = list(handle["xs"]) + list(handle["lands"])
    res = pl.pallas_call(
        body, name=name,
        out_shape=[pltpu.HBM(a.shape, a.dtype) for a in ops],
        in_specs=[hbm] * (2 * n) + [sem, sem, pl.BlockSpec(memory_space=pl.ANY)],
        out_specs=[hbm] * (2 * n),
        input_output_aliases={i: i for i in range(2 * n)},
        compiler_params=pltpu.CompilerParams(has_side_effects=pltpu.SideEffectType.DATAFLOW_SIDE_EFFECTING),
    )(*ops, handle["ssem"], handle["rsem"], after)
    me = 2 * lax.axis_index("x") + lax.axis_index("y")
    if everyone:
        me = 2 * me + lax.axis_index("c")
    out = []
    for x, land in zip(res[:n], res[n:]):
        own = lax.dynamic_index_in_dim(x, me, 0, keepdims=False) if scatter else x
        out.append(lax.dynamic_update_index_in_dim(land, own, me, 0))
    return out


def _core_gather(xs, name):
    n = len(xs)

    def body(*refs):
        x_refs, o_refs, mine, got = refs[:n], refs[n:2 * n], refs[2 * n:3 * n], refs[3 * n:4 * n]
        lsem, ssem, rsem, osem = refs[4 * n:]
        mx, my, mc = _my_place()
        loads = [pltpu.make_async_copy(x_refs[i], mine[i], lsem.at[i]) for i in range(n)]
        for cp in loads:
            cp.start()
        sends, stores = [], []
        for i in range(n):
            loads[i].wait()
            cp = pltpu.make_async_remote_copy(
                src_ref=mine[i], dst_ref=got[i], send_sem=ssem.at[i], recv_sem=rsem.at[i],
                device_id=(mx, my, 1 - mc), device_id_type=MESH)
            cp.start()
            sends.append(cp)
            for k in range(4):
                st = pltpu.make_async_copy(mine[i].at[k], o_refs[i].at[k, mc], osem.at[i, k])
                st.start()
                stores.append(st)
        for i in range(n):
            sends[i].wait_recv()
            for k in range(4):
                st = pltpu.make_async_copy(got[i].at[k], o_refs[i].at[k, 1 - mc], osem.at[n + i, k])
                st.start()
                stores.append(st)
        for cp in sends:
            cp.wait_send()
        for st in stores:
            st.wait()

    hbm = pl.BlockSpec(memory_space=pl.ANY)
    bufs = [pltpu.VMEM(x.shape, x.dtype) for x in xs]
    return pl.pallas_call(
        body, name=name,
        out_shape=[_sds((4, 2) + tuple(x.shape[1:]), x.dtype) for x in xs],
        in_specs=[hbm] * n, out_specs=[hbm] * n,
        scratch_shapes=bufs + bufs + [pltpu.SemaphoreType.DMA((n,)), pltpu.SemaphoreType.DMA((n,)),
                                      pltpu.SemaphoreType.DMA((n,)), pltpu.SemaphoreType.DMA((2 * n, 4))],
        compiler_params=pltpu.CompilerParams(vmem_limit_bytes=VMEM_LIMIT),
    )(*xs)


def _rows_step(k):
    for r in (256, 128, 64, 32, 16, 8):
        if k % r == 0:
            return r
    return k


def _core_scatter_sum(gs, name):
    n = len(gs)

    def body(*refs):
        g_refs, o_refs = refs[:n], refs[n:2 * n]
        send, got, mine = refs[2 * n:3 * n], refs[3 * n:4 * n], refs[4 * n:5 * n]
        lsem, msem, ssem, rsem, osem = refs[5 * n:]
        mx, my, mc = _my_place()
        pairs = [(i, k) for i in range(n) for k in range(4)]
        out_loads = {(i, k): pltpu.make_async_copy(g_refs[i].at[k, 1 - mc], send[i].at[k], lsem.at[i, k])
                     for i, k in pairs}
        own_loads = {(i, k): pltpu.make_async_copy(g_refs[i].at[k, mc], mine[i].at[k], msem.at[i, k])
                     for i, k in pairs}
        for p in pairs:
            out_loads[p].start()
        for p in pairs:
            own_loads[p].start()
        sends = []
        for i in range(n):
            for k in range(4):
                out_loads[i, k].wait()
            cp = pltpu.make_async_remote_copy(
                src_ref=send[i], dst_ref=got[i], send_sem=ssem.at[i], recv_sem=rsem.at[i],
                device_id=(mx, my, 1 - mc), device_id_type=MESH)
            cp.start()
            sends.append(cp)
        stores = []
        for i in range(n):
            for k in range(4):
                own_loads[i, k].wait()
            sends[i].wait_recv()
            rows = g_refs[i].shape[2]
            step = _rows_step(rows)

            def add(r, _, i=i, step=step):
                sl = pl.ds(pl.multiple_of(r * step, step), step)
                for k in range(4):
                    mine[i][k, sl, :] = (mine[i][k, sl, :].astype(F32) + got[i][k, sl, :].astype(F32)).astype(BF16)
                return 0

            lax.fori_loop(0, rows // step, add, 0)
            st = pltpu.make_async_copy(mine[i], o_refs[i], osem.at[i])
            st.start()
            stores.append(st)
        for cp in sends:
            cp.wait_send()
        for st in stores:
            st.wait()

    hbm = pl.BlockSpec(memory_space=pl.ANY)
    blk = [(4,) + tuple(g.shape[2:]) for g in gs]
    bufs = [pltpu.VMEM(b, BF16) for b in blk]
    return pl.pallas_call(
        body, name=name,
        out_shape=[_sds(b, BF16) for b in blk],
        in_specs=[hbm] * n, out_specs=[hbm] * n,
        scratch_shapes=bufs * 3 + [pltpu.SemaphoreType.DMA((n, 4)), pltpu.SemaphoreType.DMA((n, 4)),
                                   pltpu.SemaphoreType.DMA((n,)), pltpu.SemaphoreType.DMA((n,)),
                                   pltpu.SemaphoreType.DMA((n,))],
        compiler_params=pltpu.CompilerParams(vmem_limit_bytes=VMEM_LIMIT),
    )(*gs)


def _all_gather(xs, name):
    by_chip = _chip_exchange(xs, name + "_chips", scatter=False)
    both = _core_gather(by_chip, name + "_cores")
    return [b.reshape((N_DEV,) + tuple(x.shape)) for b, x in zip(both, xs)]


def _ada_partial(c_all, w_ada_loc, b_loc):
    def body(c_ref, w_ref, b_ref, act_ref, mod_ref):
        c = c_ref[...]
        act = c * (1.0 / (1.0 + jnp.exp(-c)))
        act_ref[...] = act
        mod_ref[...] = _dot(act.astype(BF16), w_ref[...].astype(BF16)) + b_ref[...]

    nb, d = c_all.shape
    return pl.pallas_call(
        body, name="ada_partial",
        out_shape=(_sds((nb, d), F32), _sds((nb, w_ada_loc.shape[1]), F32)),
        compiler_params=pltpu.CompilerParams(vmem_limit_bytes=VMEM_LIMIT),
    )(c_all, w_ada_loc, b_loc)


_AFTER = pl.BlockSpec(memory_space=pl.ANY)


def _inproj_fwd(x, mod, ln_g, ln_b, w_in_p, w_uq_p, w_kv, gq, gkv, tc, ts1, ts2, dm, after):
    B, S, D = x.shape
    tm = dm["tm"]
    sbw, qr, kvr, nh = dm["sbw"], dm["qr"], dm["kvr"], dm["nh"]
    o_cq, o_ckv, o_kr = 3 * sbw, 3 * sbw + qr, 3 * sbw + qr + kvr
    qpw = nh * HEAD_PAD

    def body(x_ref, mod_ref, g_ref, b_ref, win_ref, wuq_ref, wkv_ref, gq_ref, gkv_ref, tc_ref, ts1_ref, ts2_ref, _,
             x0_ref, q_ref, k_ref, v_ref, qp_ref, kp_ref, mv_ref, cq_ref, ckv_ref, qn_ref, kvn_ref):
        x0, _, _ = _ln_fwd(x_ref[0], g_ref[...], b_ref[...])
        x0_ref[0] = x0
        mod = mod_ref[0]
        h = (x0 * (1.0 + mod[1:2]) + mod[0:1]).astype(BF16)
        proj = _dot_nt(h, win_ref[...])
        q_ref[0] = (proj[:, 0:sbw] * SB_Q_SCALE).astype(BF16)
        k_ref[0] = proj[:, sbw:2 * sbw].astype(BF16)
        v_ref[0] = proj[:, 2 * sbw:3 * sbw].astype(BF16)
        cq = proj[:, o_cq:o_cq + qr]
        ckv = proj[:, o_ckv:o_ckv + kvr]
        cq_ref[0] = cq
        ckv_ref[0] = ckv
        qn = (cq * lax.rsqrt(jnp.mean(cq * cq, axis=-1, keepdims=True) + RMS_EPS) * gq_ref[...]).astype(BF16)
        kvn = (ckv * lax.rsqrt(jnp.mean(ckv * ckv, axis=-1, keepdims=True) + RMS_EPS) * gkv_ref[...]).astype(BF16)
        qn_ref[0] = qn
        kvn_ref[0] = kvn
        c1, s1, s2 = tc_ref[...], ts1_ref[...], ts2_ref[...]
        c8, s18, s28 = jnp.tile(c1, (1, nh)), jnp.tile(s1, (1, nh)), jnp.tile(s2, (1, nh))
        qp_ref[0] = (_rope(_dot(qn, wuq_ref[...]), c8, s18, s28) * MLA_Q_SCALE).astype(BF16)
        kvo = _dot(kvn, wkv_ref[...])
        kr = pltpu.roll(proj[:, o_kr:o_kr + LANES], 64, 1)
        kr = _rope(kr, c1, s1, s2)
        kp_ref[0] = (kvo[:, 0:qpw] + jnp.tile(kr, (1, nh))).astype(BF16)
        mv_ref[0] = kvo[:, qpw:].astype(BF16)

    tab = pl.BlockSpec((tm, LANES), lambda b, s: (s, 0))
    outs = [(D, F32), (sbw, BF16), (sbw, BF16), (sbw, BF16), (qpw, BF16), (qpw, BF16),
            (nh * MLA_V, BF16), (qr, F32), (kvr, F32), (qr, BF16), (kvr, BF16)]
    return pl.pallas_call(
        body, name="inproj_fwd", grid=(B, S // tm),
        in_specs=[_tok(tm, D), _perb(N_MOD, D), _full(ln_g), _full(ln_b), _full(w_in_p), _full(w_uq_p),
                  _full(w_kv), _full(gq), _full(gkv), tab, tab, tab, _AFTER],
        out_specs=[_tok(tm, w) for w, _ in outs],
        out_shape=[_sds((B, S, w), t) for w, t in outs],
        compiler_params=_cparams(("parallel", "parallel")),
    )(x, mod, ln_g, ln_b, w_in_p, w_uq_p, w_kv, gq, gkv, tc, ts1, ts2, after)


def _neg_abs(x):
    sign = jnp.uint32(0x80000000)
    return lax.bitcast_convert_type(lax.bitcast_convert_type(x, jnp.uint32) | sign, F32)


SB_Q_SCALE = -(SB_HD ** -0.5) * LOG2E
MLA_Q_SCALE = (MLA_NOPE + MLA_ROPE) ** -0.5 * LOG2E


def _log2_keep(zs):
    return jnp.minimum(zs, 0.0) - jnp.log2(1.0 + jnp.exp2(_neg_abs(zs)))


def _split_dot(a, u):
    hi = a.astype(BF16)
    lo = (a - hi.astype(F32)).astype(BF16)
    return _dot(jnp.concatenate([hi, lo], axis=1), jnp.concatenate([u, u], axis=0))


def _tri(n, rel):
    row = lax.broadcasted_iota(jnp.int32, (n, n), 0)
    col = lax.broadcasted_iota(jnp.int32, (n, n), 1)
    return rel(row, col).astype(BF16)


def _running_sum(a, tri, reverse, split, start):
    cs = tri.shape[0]
    n = a.shape[1] // cs
    out = [None] * n
    run = start
    for c in (reversed(range(n)) if reverse else range(n)):
        part = a[:, c * cs:(c + 1) * cs]
        out[c] = (_split_dot(part, tri) if split else _dot(part.astype(BF16), tri)) + run
        run = run + jnp.sum(part, axis=1, keepdims=True)
    return (out[0] if n == 1 else jnp.concatenate(out, axis=1)), run


def _transpose_bf16(a):
    return a.astype(F32).T.astype(BF16)


def _tile_mask(nr, nk, r0, c0, rel):
    row = lax.broadcasted_iota(jnp.int32, (nr, nk), 0) + r0
    col = lax.broadcasted_iota(jnp.int32, (nr, nk), 1) + c0
    return rel(row, col)


def _put_rows(whole, part, r0):
    return part if r0 == 0 else jnp.concatenate([whole[:r0], part], axis=0)


def _diag_tiles(tq, split):
    half = tq // 2
    return [(0, tq, 0, half), (half, half, half, half)] if split else [(0, tq, 0, tq)]


def _sb_fwd(q, k, v, dm):
    B, S, W = q.shape
    tq = dm["tq"]
    nq = S // tq

    def body(q_ref, k_ref, v_ref, y_ref, tot_ref):
        qi = pl.program_id(2)
        q2 = q_ref[0]
        lane = lax.broadcasted_iota(jnp.int32, (tq, LANES), 1)
        qs = jnp.concatenate([jnp.where(lane < SB_HD, q2, 0), jnp.where(lane >= SB_HD, q2, 0)], axis=0).astype(BF16)
        later = _tri(min(tq, CUMSUM_W), lambda a, b: a > b)
        assert tq & (tq - 1) == 0
        strict = _tile_mask(2 * tq, tq, 0, 0, lambda t, s: s < (t & (tq - 1)))

        def block(j, carry, masked):
            acc, run = carry
            off = pl.multiple_of(j * tq, tq)
            zs = _dot_nt(qs, k_ref[0, pl.ds(off, tq), :])
            a = _log2_keep(zs)
            if masked:
                a = jnp.where(strict, a, 0.0)
            a_later, run = _running_sum(a, later, reverse=True, split=True, start=run)
            w = jnp.exp2((a - zs) + a_later)
            if masked:
                w = jnp.where(strict, w, 0.0)
            return acc + _dot(w.astype(BF16), v_ref[0, pl.ds(off, tq), :]), run

        carry = block(qi, (jnp.zeros((2 * tq, LANES), F32), jnp.zeros((2 * tq, 1), F32)), True)
        acc, run = lax.fori_loop(0, qi, lambda jj, c: block(qi - 1 - jj, c, False), carry)
        y_ref[0] = jnp.where(lane < SB_HD, acc[:tq], acc[tq:]).astype(BF16)
        tot_ref[0] = jnp.where(lane < SB_HD, run[:tq], run[tq:])

    qspec = pl.BlockSpec((1, tq, LANES), lambda b, hp, i: (b, i, hp))
    kspec = pl.BlockSpec((1, S, LANES), lambda b, hp, i: (b, 0, hp))
    return pl.pallas_call(
        body, name="sb_fwd", grid=(B, W // LANES, nq),
        in_specs=[qspec, kspec, kspec],
        out_specs=[qspec, qspec],
        out_shape=[_sds((B, S, W), BF16), _sds((B, S, W), F32)],
        compiler_params=_cparams(("parallel", "parallel", "arbitrary")),
    )(q, k, v)


def _sb_bwd(q, k, v, tot, dy, dm, after):
    B, S, W = q.shape
    tq = dm["tq"]
    nq = S // tq

    def body(q_ref, k_ref, v_ref, tot_ref, dy_ref, _, dq_ref, dk_ref, dv_ref, dk_acc, dv_acc):
        qi = pl.program_id(2)

        @pl.when(qi == 0)
        def _():
            dk_acc[...] = jnp.zeros_like(dk_acc)
            dv_acc[...] = jnp.zeros_like(dv_acc)

        q2 = q_ref[0]
        dy2 = dy_ref[0]
        tot2 = tot_ref[0]
        lane = lax.broadcasted_iota(jnp.int32, (tq, LANES), 1)
        in_h = [lane < SB_HD, lane >= SB_HD]
        qh = [jnp.where(m, q2, 0).astype(BF16) for m in in_h]
        dyh = [jnp.where(m, dy2, 0).astype(BF16) for m in in_h]
        q_t = [_transpose_bf16(a) for a in qh]
        dy_t = [_transpose_bf16(a) for a in dyh]
        toth = [tot2[:, 0:1], tot2[:, SB_HD:SB_HD + 1]]

        def tile(j, carry, r0, nr, c0, nk, masked):
            off = pl.multiple_of(j * tq + c0, math.gcd(tq, c0))
            k2 = k_ref[0, pl.ds(off, nk), :]
            v2 = v_ref[0, pl.ds(off, nk), :]
            upto = _tri(min(nk, CUMSUM_W), lambda a, b: a <= b)
            before = _tri(min(nk, CUMSUM_W), lambda a, b: a < b)
            strict = _tile_mask(nr, nk, r0, c0, lambda t, s: s < t) if masked else None
            rows = slice(r0, r0 + nr)
            new = []
            dk_blk = jnp.zeros((LANES, nk), F32)
            dv_blk = jnp.zeros((LANES, nk), F32)
            for h in range(2):
                dq, pa, pg = carry[3 * h][rows], carry[3 * h + 1][rows], carry[3 * h + 2][rows]
                zs = _dot_nt(qh[h][rows], k2)
                a = _log2_keep(zs)
                if masked:
                    a = jnp.where(strict, a, 0.0)
                a_upto, pa = _running_sum(a, upto, reverse=False, split=True, start=pa)
                w = jnp.exp2((a - zs) - a_upto)
                if masked:
                    w = jnp.where(strict, w, 0.0)
                g = _dot_nt(dyh[h][rows], v2) * w
                g_before, pg = _running_sum(g, before, reverse=False, split=False, start=pg)
                dz = (g + g_before) * jnp.exp2(a) - g_before
                if masked:
                    dz = jnp.where(strict, dz, 0.0)
                dzb = dz.astype(BF16)
                dv_blk = dv_blk + _dot(dy_t[h][:, rows], w.astype(BF16))
                dk_blk = dk_blk + _dot(q_t[h][:, rows], dzb)
                new += [_put_rows(carry[3 * h], dq + _dot(dzb, k2), r0), _put_rows(carry[3 * h + 1], pa, r0),
                        _put_rows(carry[3 * h + 2], pg, r0)]
            dk_acc[j, :, c0:c0 + nk] += dk_blk
            dv_acc[j, :, c0:c0 + nk] += dv_blk
            return tuple(new)

        zero = jnp.zeros((tq, LANES), F32)
        zrun = jnp.zeros((tq, 1), F32)
        half = tq // 2
        carry = lax.fori_loop(
            0, qi, lambda j, c: tile(j, tile(j, c, 0, tq, 0, half, False), 0, tq, half, half, False),
            (zero, -toth[0], zrun, zero, -toth[1], zrun))
        for r0, nr, c0, nk in _diag_tiles(tq, False):
            carry = tile(qi, carry, r0, nr, c0, nk, True)
        dq_ref[0] = (jnp.where(in_h[0], carry[0], carry[3]) * (SB_HD ** -0.5)).astype(BF16)

        @pl.when(qi == nq - 1)
        def _():
            for jb in range(nq):
                dk_ref[0, jb * tq:(jb + 1) * tq, :] = (dk_acc[jb].T * (-1.0 / LOG2E)).astype(BF16)
                dv_ref[0, jb * tq:(jb + 1) * tq, :] = dv_acc[jb].T.astype(BF16)

    qspec = pl.BlockSpec((1, tq, LANES), lambda b, hp, i: (b, i, hp))
    kspec = pl.BlockSpec((1, S, LANES), lambda b, hp, i: (b, 0, hp))
    return pl.pallas_call(
        body, name="sb_bwd", grid=(B, W // LANES, nq),
        in_specs=[qspec, kspec, kspec, qspec, qspec, _AFTER],
        out_specs=[qspec, kspec, kspec],
        out_shape=[_sds((B, S, W), BF16)] * 3,
        scratch_shapes=[pltpu.VMEM((nq, LANES, tq), F32), pltpu.VMEM((nq, LANES, tq), F32)],
        compiler_params=_cparams(("parallel", "parallel", "arbitrary")),
    )(q, k, v, tot, dy, after)


def _same_or_earlier_chunk(row, col):
    return lax.shift_right_logical(col, 6) <= lax.shift_right_logical(row, 6)


def _mla_fwd(qp, kp, mv, dm, after):
    B, S, QW = qp.shape
    VW = mv.shape[2]
    tq = dm["tq"]
    nq = S // tq
    assert CHUNK == 64

    def body(q_ref, k_ref, v_ref, _, y_ref, lse_ref):
        qi = pl.program_id(2)
        q2 = q_ref[0]
        lane = lax.broadcasted_iota(jnp.int32, (tq, LANES), 1)

        def tile(j, carry, r0, nr, c0, nk, masked):
            off = pl.multiple_of(j * tq + c0, math.gcd(tq, c0))
            v2 = v_ref[0, pl.ds(off, nk), :]
            allowed = _tile_mask(nr, nk, r0, c0, _same_or_earlier_chunk) if masked else None
            rows = slice(r0, r0 + nr)
            heads = range(2)
            sl = [slice(h * HEAD_PAD, (h + 1) * HEAD_PAD) for h in heads]
            m_old = [carry[3 * h + 1][rows] for h in heads]
            s = [_dot_nt(q2[rows, sl[h]], k_ref[0, pl.ds(off, nk), sl[h]]) for h in heads]
            if masked:
                s = [jnp.where(allowed, s[h], -1e30) for h in heads]
            m_new = [jnp.maximum(m_old[h], jnp.max(s[h], axis=1, keepdims=True)) for h in heads]
            alpha = [jnp.exp2(m_old[h] - m_new[h]) for h in heads]
            p = [jnp.exp2(s[h] - m_new[h]) for h in heads]
            acc = [alpha[h] * carry[3 * h][rows] + _dot(p[h].astype(BF16), v2) for h in heads]
            l = [alpha[h] * carry[3 * h + 2][rows] + jnp.sum(p[h], axis=1, keepdims=True) for h in heads]
            out = []
            for h in heads:
                out += [_put_rows(carry[3 * h], acc[h], r0), _put_rows(carry[3 * h + 1], m_new[h], r0),
                        _put_rows(carry[3 * h + 2], l[h], r0)]
            return tuple(out)

        zero = jnp.zeros((tq, LANES), F32)
        m0 = jnp.full((tq, 1), -1e30, F32)
        l0 = jnp.zeros((tq, 1), F32)
        carry = (zero, m0, l0, zero, m0, l0)
        for r0, nr, c0, nk in _diag_tiles(tq, False):
            carry = tile(qi, carry, r0, nr, c0, nk, True)
        carry = lax.fori_loop(0, qi, lambda j, c: tile(j, c, 0, tq, 0, tq, False), carry)
        y0 = carry[0] / carry[2]
        y1 = carry[3] / carry[5]
        y_ref[0] = jnp.where(lane < MLA_V, y0, y1).astype(BF16)
        lse_ref[0] = jnp.where(lane < MLA_V, carry[1] + jnp.log2(carry[2]), carry[4] + jnp.log2(carry[5]))

    qspec = pl.BlockSpec((1, tq, 2 * HEAD_PAD), lambda b, hp, i: (b, i, hp))
    kspec = pl.BlockSpec((1, S, 2 * HEAD_PAD), lambda b, hp, i: (b, 0, hp))
    vspec = pl.BlockSpec((1, S, LANES), lambda b, hp, i: (b, 0, hp))
    yspec = pl.BlockSpec((1, tq, LANES), lambda b, hp, i: (b, i, hp))
    return pl.pallas_call(
        body, name="mla_fwd", grid=(B, VW // LANES, nq),
        in_specs=[qspec, kspec, vspec, _AFTER],
        out_specs=[yspec, yspec],
        out_shape=[_sds((B, S, VW), BF16), _sds((B, S, VW), F32)],
        compiler_params=_cparams(("parallel", "parallel", "arbitrary")),
    )(qp, kp, mv, after)


def _mla_bwd(qp, kp, mv, y, lse, dy, dm, after):
    B, S, QW = qp.shape
    VW = mv.shape[2]
    tq = dm["tq"]
    nq = S // tq
    scale = (MLA_NOPE + MLA_ROPE) ** -0.5

    def body(q_ref, k_ref, v_ref, y_ref, lse_ref, dy_ref, _, dq_ref, dk_ref, dv_ref, dk_acc, dv_acc):
        qi = pl.program_id(2)

        @pl.when(qi == 0)
        def _():
            dk_acc[...] = jnp.zeros_like(dk_acc)
            dv_acc[...] = jnp.zeros_like(dv_acc)

        q2 = q_ref[0]
        dy2 = dy_ref[0]
        lse2 = lse_ref[0]
        lane = lax.broadcasted_iota(jnp.int32, (tq, LANES), 1)
        in_h = [lane < MLA_V, lane >= MLA_V]
        prod = dy2.astype(F32) * y_ref[0].astype(F32)
        delta = [jnp.sum(jnp.where(m, prod, 0.0), axis=1, keepdims=True) for m in in_h]
        dyh = [jnp.where(m, dy2, 0).astype(BF16) for m in in_h]
        lseh = [lse2[:, 0:1], lse2[:, MLA_V:MLA_V + 1]]
        q_t = _transpose_bf16(q2)
        dy_t = [_transpose_bf16(a) for a in dyh]

        def tile(j, carry, r0, nr, c0, nk, masked):
            off = pl.multiple_of(j * tq + c0, math.gcd(tq, c0))
            v2 = v_ref[0, pl.ds(off, nk), :]
            allowed = _tile_mask(nr, nk, r0, c0, _same_or_earlier_chunk) if masked else None
            rows = slice(r0, r0 + nr)
            keys = slice(c0, c0 + nk)
            heads = range(2)
            sl = [slice(h * HEAD_PAD, (h + 1) * HEAD_PAD) for h in heads]
            qhh = [q2[rows, sl[h]] for h in heads]
            dyr = [dyh[h][rows] for h in heads]
            kh = [k_ref[0, pl.ds(off, nk), sl[h]] for h in heads]
            s = [_dot_nt(qhh[h], kh[h]) for h in heads]
            dp = [_dot_nt(dyr[h], v2) for h in heads]
            if masked:
                s = [jnp.where(allowed, s[h], -1e30) for h in heads]
            p = [jnp.exp2(s[h] - lseh[h][rows]) for h in heads]
            dv_acc[j, :, keys] += (_dot(dy_t[0][:, rows], p[0].astype(BF16))
                                   + _dot(dy_t[1][:, rows], p[1].astype(BF16)))
            ds = [(p[h] * (dp[h] - delta[h][rows])).astype(BF16) for h in heads]
            for h in heads:
                dk_acc[j, sl[h], keys] += _dot(q_t[sl[h], rows], ds[h])
            return tuple(_put_rows(carry[h], carry[h][rows] + _dot(ds[h], kh[h]), r0) for h in heads)

        zero = jnp.zeros((tq, HEAD_PAD), F32)
        carry = lax.fori_loop(0, qi, lambda j, c: tile(j, c, 0, tq, 0, tq, False), (zero, zero))
        for r0, nr, c0, nk in _diag_tiles(tq, True):
            carry = tile(qi, carry, r0, nr, c0, nk, True)
        dq_ref[0] = (jnp.concatenate([carry[0], carry[1]], axis=1) * scale).astype(BF16)

        @pl.when(qi == nq - 1)
        def _():
            for jb in range(nq):
                dk_ref[0, jb * tq:(jb + 1) * tq, :] = (dk_acc[jb].T * (1.0 / LOG2E)).astype(BF16)
                dv_ref[0, jb * tq:(jb + 1) * tq, :] = dv_acc[jb].T.astype(BF16)

    qspec = pl.BlockSpec((1, tq, 2 * HEAD_PAD), lambda b, hp, i: (b, i, hp))
    kspec = pl.BlockSpec((1, S, 2 * HEAD_PAD), lambda b, hp, i: (b, 0, hp))
    vspec = pl.BlockSpec((1, S, LANES), lambda b, hp, i: (b, 0, hp))
    yspec = pl.BlockSpec((1, tq, LANES), lambda b, hp, i: (b, i, hp))
    return pl.pallas_call(
        body, name="mla_bwd", grid=(B, VW // LANES, nq),
        in_specs=[qspec, kspec, vspec, yspec, yspec, yspec, _AFTER],
        out_specs=[qspec, kspec, vspec],
        out_shape=[_sds((B, S, QW), BF16), _sds((B, S, QW), BF16), _sds((B, S, VW), BF16)],
        scratch_shapes=[pltpu.VMEM((nq, 2 * HEAD_PAD, tq), F32), pltpu.VMEM((nq, LANES, tq), F32)],
        compiler_params=_cparams(("parallel", "parallel", "arbitrary")),
    )(qp, kp, mv, y, lse, dy, after)


def _stat_specs(B, D):
    specs = [pl.BlockSpec((1, 8, D), lambda b, s: (b, 0, 0)), pl.BlockSpec((8, D), lambda b, s: (0, 0))]
    shapes = [_sds((B, 8, D), F32), _sds((8, D), F32)]
    return specs, shapes


def _stat_init(bst_ref, wst_ref):
    @pl.when(pl.program_id(1) == 0)
    def _():
        bst_ref[...] = jnp.zeros_like(bst_ref)

    @pl.when((pl.program_id(0) == 0) & (pl.program_id(1) == 0))
    def _():
        wst_ref[...] = jnp.zeros_like(wst_ref)


def _mlp_fwd(sb_y, mla_y, x0, mod, target, w_o, w_up, w_down, ln1_g, ln1_b, ln_g, ln_b, dm):
    B, S, D = x0.shape
    tm = dm["tm"]
    sbw = sb_y.shape[2]
    nck, _, ck = w_up.shape
    dff = nck * ck

    def body(ya_ref, yb_ref, x0_ref, mod_ref, t_ref, wo_ref, wu_ref, wd_ref, g1_ref, b1_ref, g_ref, b_ref,
             mix_ref, x1_ref, h2_ref, u_ref, dr_ref, bst_ref, wst_ref):
        _stat_init(bst_ref, wst_ref)
        mod = mod_ref[0]
        mix = _dot(ya_ref[0], wo_ref[0:sbw, :]) + _dot(yb_ref[0], wo_ref[sbw:, :])
        mix_ref[0] = mix
        x1, _, _ = _ln_fwd(ALPHA * x0_ref[0] + (1.0 + mod[2:3]) * mix, g1_ref[...], b1_ref[...])
        x1_ref[0] = x1
        h2 = (x1 * (1.0 + mod[4:5]) + mod[3:4]).astype(BF16)
        h2_ref[0] = h2
        g = g_ref[...]
        ff = jnp.zeros((tm, D), F32)
        for c in range(nck):
            u = _dot(h2, wu_ref[c])
            u_ref[0, :, c * ck:(c + 1) * ck] = u.astype(BF16)
            act = jnp.square(jnp.maximum(u, 0.0)).astype(BF16)
            ff = ff + _dot(act, wd_ref[c])
        x2, xhat, rstd = _ln_fwd(ALPHA * x1 + (1.0 + mod[5:6]) * ff, g, b_ref[...])
        err = x2 - t_ref[0]
        dy = err * (1.0 / D)
        dr = _ln_bwd(dy, xhat, rstd, g)
        dr_ref[0] = dr
        bst_ref[0, 0:1, :] += _colsum(dr * ff)
        wst_ref[0:1, :] += _colsum(dy * xhat)
        wst_ref[1:2, :] += _colsum(dy)
        wst_ref[2:3, :] += _colsum(err * err) * (0.5 / D)

    sspecs, sshapes = _stat_specs(B, D)
    return pl.pallas_call(
        body, name="mlp_fwd", grid=(B, S // tm),
        in_specs=[_tok(tm, sbw), _tok(tm, mla_y.shape[2]), _tok(tm, D), _perb(N_MOD, D), _tok(tm, D),
                  _full(w_o), _full(w_up), _full(w_down), _full(ln1_g), _full(ln1_b), _full(ln_g), _full(ln_b)],
        out_specs=[_tok(tm, D), _tok(tm, D), _tok(tm, D), _tok(tm, dff), _tok(tm, D)] + sspecs,
        out_shape=[_sds((B, S, D), F32), _sds((B, S, D), F32), _sds((B, S, D), BF16), _sds((B, S, dff), BF16),
                   _sds((B, S, D), F32)] + sshapes,
        compiler_params=_cparams(("arbitrary", "arbitrary")),
    )(sb_y, mla_y, x0, mod, target, w_o, w_up, w_down, ln1_g, ln1_b, ln_g, ln_b)


def _mlp_bwd(dr2, u, x1, x0, mix, sb_y, mla_y, mod, w_up, w_down, w_o, ln_g, dm):
    B, S, D = x1.shape
    tm = dm["tm_small"]
    sbw = dm["sbw"]
    nck, _, ck = w_up.shape
    dff = nck * ck

    def body(dr_ref, u_ref, x1_ref, x0_ref, mix_ref, ya_ref, yb_ref, mod_ref, wu_ref, wd_ref, wo_ref, g_ref,
             du_ref, dff_ref, dx0_ref, dya_ref, dyb_ref, go_ref, bst_ref, wst_ref):
        _stat_init(bst_ref, wst_ref)

        @pl.when((pl.program_id(0) == 0) & (pl.program_id(1) == 0))
        def _():
            go_ref[...] = jnp.zeros_like(go_ref)

        mod = mod_ref[0]
        dr2 = dr_ref[0]
        dffv = ((1.0 + mod[5:6]) * dr2).astype(BF16)
        dff_ref[0] = dffv
        dh2 = jnp.zeros((tm, D), F32)
        for c in range(nck):
            sl = slice(c * ck, (c + 1) * ck)
            da = _dot_nt(dffv, wd_ref[c])
            du = (da * (2.0 * jnp.maximum(u_ref[0, :, sl].astype(F32), 0.0))).astype(BF16)
            du_ref[0, :, sl] = du
            dh2 = dh2 + _dot_nt(du, wu_ref[c])
        x1 = x1_ref[0]
        dx1 = ALPHA * dr2 + dh2 * (1.0 + mod[4:5])
        bst_ref[0, 0:1, :] += _colsum(dh2 * x1)
        bst_ref[0, 1:2, :] += _colsum(dh2)
        mix = mix_ref[0]
        g = g_ref[...]
        _, xhat, rstd = _ln_fwd(ALPHA * x0_ref[0] + (1.0 + mod[2:3]) * mix, g, 0.0)
        dr1 = _ln_bwd(dx1, xhat, rstd, g)
        wst_ref[0:1, :] += _colsum(dx1 * xhat)
        wst_ref[1:2, :] += _colsum(dx1)
        bst_ref[0, 2:3, :] += _colsum(dr1 * mix)
        dx0_ref[0] = ALPHA * dr1
        dmix = ((1.0 + mod[2:3]) * dr1).astype(BF16)
        dya_ref[0] = _dot_nt(dmix, wo_ref[0:sbw, :]).astype(BF16)
        dyb_ref[0] = _dot_nt(dmix, wo_ref[sbw:, :]).astype(BF16)
        go_ref[0:sbw, :] += _dot_tn(ya_ref[0], dmix)
        go_ref[sbw:, :] += _dot_tn(yb_ref[0], dmix)

    sspecs, sshapes = _stat_specs(B, D)
    wa, wb = sbw, w_o.shape[0] - sbw
    return pl.pallas_call(
        body, name="mlp_bwd", grid=(B, S // tm),
        in_specs=[_tok(tm, D), _tok(tm, dff), _tok(tm, D), _tok(tm, D), _tok(tm, D), _tok(tm, wa), _tok(tm, wb),
                  _perb(N_MOD, D), _full(w_up), _full(w_down), _full(w_o), _full(ln_g)],
        out_specs=[_tok(tm, dff), _tok(tm, D), _tok(tm, D), _tok(tm, wa), _tok(tm, wb),
                   pl.BlockSpec(w_o.shape, lambda b, s: (0, 0))] + sspecs,
        out_shape=[_sds((B, S, dff), BF16), _sds((B, S, D), BF16), _sds((B, S, D), F32),
                   _sds((B, S, wa), BF16), _sds((B, S, wb), BF16), _sds(w_o.shape, F32)] + sshapes,
        compiler_params=_cparams(("arbitrary", "arbitrary")),
    )(dr2, u, x1, x0, mix, sb_y, mla_y, mod, w_up, w_down, w_o, ln_g)


def _inproj_bwd(x, x0, dx0a, mod, ln_g, dq, dk, dv, dqp, dkp, dmv, cq, ckv, qn, kvn, w_in_p, w_uq_p, w_kv, gq, gkv,
                tc, ts1, ts2, dm):
    B, S, D = x.shape
    tm = dm["tm"]
    sbw, qr, kvr, nh = dm["sbw"], dm["qr"], dm["kvr"], dm["nh"]
    qpw = nh * HEAD_PAD
    dinp = w_in_p.shape[0]
    kvw = w_kv.shape[1]

    def body(x_ref, x0_ref, dx0a_ref, mod_ref, g_ref, dq_ref, dk_ref, dv_ref, dqp_ref, dkp_ref, dmv_ref,
             cq_ref, ckv_ref, qn_ref, kvn_ref, win_ref, wuq_ref, wkv_ref, gq_ref, gkv_ref, tc_ref, ts1_ref, ts2_ref,
             gx_ref, gin_ref, guq_ref, gwkv_ref, bst_ref, wst_ref):
        _stat_init(bst_ref, wst_ref)

        @pl.when((pl.program_id(0) == 0) & (pl.program_id(1) == 0))
        def _():
            gin_ref[...] = jnp.zeros_like(gin_ref)
            guq_ref[...] = jnp.zeros_like(guq_ref)
            gwkv_ref[...] = jnp.zeros_like(gwkv_ref)

        mod = mod_ref[0]
        c1, s1, s2 = tc_ref[...], ts1_ref[...], ts2_ref[...]
        c8, s18, s28 = jnp.tile(c1, (1, nh)), jnp.tile(s1, (1, nh)), jnp.tile(s2, (1, nh))
        dqpre = _rope_t(dqp_ref[0].astype(F32), c8, s18, s28).astype(BF16)
        guq_ref[...] += _dot_tn(qn_ref[0], dqpre)
        gq = gq_ref[...]
        cq = cq_ref[0]
        rq = lax.rsqrt(jnp.mean(cq * cq, axis=-1, keepdims=True) + RMS_EPS)
        dqn = _dot_nt(dqpre, wuq_ref[...])
        wst_ref[4:5, 0:qr] += _colsum(dqn * cq * rq)
        dqg = dqn * gq
        dcq = rq * dqg - cq * (rq * rq * rq) * jnp.mean(dqg * cq, axis=-1, keepdims=True)

        dkpre = _rope_t(dkp_ref[0].astype(F32), c8, s18, s28)
        dkr = dkpre[:, 0:HEAD_PAD]
        for h in range(1, nh):
            dkr = dkr + dkpre[:, h * HEAD_PAD:(h + 1) * HEAD_PAD]
        lane = lax.broadcasted_iota(jnp.int32, (tm, LANES), 1)
        dkr = jnp.where((lane >= MLA_NOPE) & (lane < MLA_NOPE + MLA_ROPE), dkr, 0.0)
        dkr = pltpu.roll(dkr, LANES - MLA_NOPE, 1)
        dkvo = jnp.concatenate([dkpre.astype(BF16), dmv_ref[0]], axis=1)
        gwkv_ref[...] += _dot_tn(kvn_ref[0], dkvo)
        gkv = gkv_ref[...]
        ckv = ckv_ref[0]
        rkv = lax.rsqrt(jnp.mean(ckv * ckv, axis=-1, keepdims=True) + RMS_EPS)
        dkvn = _dot_nt(dkvo, wkv_ref[...])
        wst_ref[5:6, 0:kvr] += _colsum(dkvn * ckv * rkv)
        dkg = dkvn * gkv
        dckv = rkv * dkg - ckv * (rkv * rkv * rkv) * jnp.mean(dkg * ckv, axis=-1, keepdims=True)

        dproj = jnp.concatenate([dq_ref[0], dk_ref[0], dv_ref[0], dcq.astype(BF16), dckv.astype(BF16),
                                 dkr.astype(BF16)], axis=1)
        dh = _dot(dproj, win_ref[...])
        x0 = x0_ref[0]
        gin_ref[...] += _dot_tn(dproj, (x0 * (1.0 + mod[1:2]) + mod[0:1]).astype(BF16))
        dx0 = dx0a_ref[0] + dh * (1.0 + mod[1:2])
        bst_ref[0, 0:1, :] += _colsum(dh * x0)
        bst_ref[0, 1:2, :] += _colsum(dh)
        g = g_ref[...]
        _, xhat, rstd = _ln_fwd(x_ref[0], g, 0.0)
        gx_ref[0] = _ln_bwd(dx0, xhat, rstd, g)
        wst_ref[0:1, :] += _colsum(dx0 * xhat)
        wst_ref[1:2, :] += _colsum(dx0)

    tab = pl.BlockSpec((tm, LANES), lambda b, s: (s, 0))
    sspecs, sshapes = _stat_specs(B, D)
    return pl.pallas_call(
        body, name="inproj_bwd", grid=(B, S // tm),
        in_specs=[_tok(tm, D), _tok(tm, D), _tok(tm, D), _perb(N_MOD, D), _full(ln_g),
                  _tok(tm, sbw), _tok(tm, sbw), _tok(tm, sbw), _tok(tm, qpw), _tok(tm, qpw), _tok(tm, nh * MLA_V),
                  _tok(tm, qr), _tok(tm, kvr), _tok(tm, qr), _tok(tm, kvr),
                  _full(w_in_p), _full(w_uq_p), _full(w_kv), _full(gq), _full(gkv), tab, tab, tab],
        out_specs=[_tok(tm, D), pl.BlockSpec((dinp, D), lambda b, s: (0, 0)),
                   pl.BlockSpec((qr, qpw), lambda b, s: (0, 0)), pl.BlockSpec((kvr, kvw), lambda b, s: (0, 0))] + sspecs,
        out_shape=[_sds((B, S, D), F32), _sds((dinp, D), F32), _sds((qr, qpw), F32),
                   _sds((kvr, kvw), F32)] + sshapes,
        compiler_params=_cparams(("arbitrary", "arbitrary")),
    )(x, x0, dx0a, mod, ln_g, dq, dk, dv, dqp, dkp, dmv, cq, ckv, qn, kvn, w_in_p, w_uq_p, w_kv, gq, gkv,
      tc, ts1, ts2)


def _tile_of(n, cap):
    if n <= cap:
        return n
    best = n
    for t in range(LANES, cap + 1, LANES):
        if n % t == 0:
            best = t
    return best


def _mm_tn(a, g, name, after, relu_sq=False, out_dtype=F32, col_blocks=None):
    T, K = a.shape
    N = g.shape[1]
    tt = 2048 if T % 2048 == 0 else (512 if T % 512 == 0 else T)
    tk = _tile_of(K, 1024)
    tn = _tile_of(N, 1280)
    nt = T // tt
    bw = N // col_blocks if col_blocks else tn
    assert tn % bw == 0

    def body(a_ref, g_ref, _, o_ref, acc_ref):
        @pl.when(pl.program_id(2) == 0)
        def _():
            acc_ref[...] = jnp.zeros_like(acc_ref)

        av = a_ref[...]
        if relu_sq:
            av = jnp.square(jnp.maximum(av.astype(F32), 0.0)).astype(BF16)
        acc_ref[...] += _dot_tn(av, g_ref[...])

        @pl.when(pl.program_id(2) == nt - 1)
        def _():
            if col_blocks:
                for c in range(tn // bw):
                    o_ref[c] = acc_ref[:, c * bw:(c + 1) * bw].astype(out_dtype)
            else:
                o_ref[...] = acc_ref[...].astype(out_dtype)

    if col_blocks:
        out_spec = pl.BlockSpec((tn // bw, tk, bw), lambda i, j, t: (j, i, 0))
        out_shape = _sds((col_blocks, K, bw), out_dtype)
    else:
        out_spec = pl.BlockSpec((tk, tn), lambda i, j, t: (i, j))
        out_shape = _sds((K, N), out_dtype)
    return pl.pallas_call(
        body, name=name, grid=(K // tk, N // tn, nt),
        in_specs=[pl.BlockSpec((tt, tk), lambda i, j, t: (t, i)), pl.BlockSpec((tt, tn), lambda i, j, t: (t, j)),
                  _AFTER],
        out_specs=out_spec, out_shape=out_shape,
        scratch_shapes=[pltpu.VMEM((tk, tn), F32)],
        compiler_params=_cparams(("parallel", "parallel", "arbitrary")),
    )(a, g, after)


def _reduce_adamw(parts, w, m, v, name):
    P, K, N = parts.shape
    tr = 256 if K % 256 == 0 else K

    def body(p_ref, w_ref, m_ref, v_ref, g_ref, d_ref, nm_ref, nv_ref):
        g = p_ref[0].astype(F32)
        for k in range(1, P):
            g = g + p_ref[k].astype(F32)
        g_ref[0] = g
        d_ref[0], nm_ref[0], nv_ref[0] = _adamw(w_ref[0], g, m_ref[0], v_ref[0])

    spec = pl.BlockSpec((1, tr, N), lambda r: (0, r, 0))
    return pl.pallas_call(
        body, name=name, grid=(K // tr,),
        in_specs=[pl.BlockSpec((P, tr, N), lambda r: (0, r, 0)), spec, spec, spec],
        out_specs=[spec] * 4, out_shape=[_sds((1, K, N), F32)] * 4,
        compiler_params=_cparams(("parallel",)),
    )(parts, w, m, v)


def _finish(sm, dmod_all, dmod_my, cact_all, p_small, m_small, v_small, b_ada, m_b, v_b, w_ada, m_w, v_w):
    n0 = p_small.shape[1]
    n1 = sm.shape[1]
    d = cact_all.shape[1]

    def body(sm_ref, dma_ref, dmm_ref, ca_ref, p_ref, pm_ref, pv_ref, b_ref, bm_ref, bv_ref, w_ref, wm_ref, wv_ref,
             gs_ref, ds_ref, ms_ref, vs_ref, gb_ref, db_ref, mb_ref, vb_ref, gw_ref, dw_ref, mw_ref, vw_ref,
             loss_ref):
        gs = sm_ref[0:1, :]
        for k in range(1, N_DEV):
            gs = gs + sm_ref[k:k + 1, :]
        gs_ref[...] = gs
        ds_ref[...], ms_ref[...], vs_ref[...] = _adamw(p_ref[...], gs[:, 0:n0], pm_ref[...], pv_ref[...])
        loss_ref[...] = jnp.zeros((1, LANES), F32) + jnp.sum(gs[:, n1 - d:n1])
        gb = jnp.sum(dma_ref[...], axis=0, keepdims=True)
        gb_ref[...] = gb
        db_ref[...], mb_ref[...], vb_ref[...] = _adamw(b_ref[...], gb, bm_ref[...], bv_ref[...])
        gw = _dot_tn(ca_ref[...].astype(BF16), dmm_ref[...].astype(BF16))
        gw_ref[...] = gw
        dw_ref[...], mw_ref[...], vw_ref[...] = _adamw(w_ref[...], gw, wm_ref[...], wv_ref[...])

    s0 = _sds(p_small.shape, F32)
    sb = _sds(b_ada.shape, F32)
    sw = _sds(w_ada.shape, F32)
    return pl.pallas_call(
        body, name="finish_small",
        out_shape=[_sds((1, n1), F32), s0, s0, s0, sb, sb, sb, sb, sw, sw, sw, sw,
                   _sds((1, LANES), F32)],
        compiler_params=pltpu.CompilerParams(vmem_limit_bytes=VMEM_LIMIT),
    )(sm, dmod_all, dmod_my, cact_all, p_small, m_small, v_small, b_ada, m_b, v_b, w_ada, m_w, v_w)


def _pack(arrs, dtype, width):
    flat = jnp.concatenate([a.astype(dtype).reshape(-1) for a in arrs])
    rows = -(-flat.shape[0] // (256 * width)) * 256
    return jnp.pad(flat, (0, rows * width - flat.shape[0])).reshape(rows, width)


def _unpack(slab, shapes):
    flat = slab.reshape(-1)
    out, o = [], 0
    for s in shapes:
        n = math.prod(s)
        out.append(flat[o:o + n].reshape(s))
        o += n
    return out


def _rope_tables(S):
    inv_freq = 1.0 / (ROPE_BASE ** (jnp.arange(0, MLA_ROPE, 2, dtype=F32) / MLA_ROPE))
    ang = jnp.arange(S, dtype=F32)[:, None] * inv_freq[None, :]
    cos, sin = jnp.cos(ang), jnp.sin(ang)
    one = jnp.ones((S, MLA_NOPE), F32)
    z16 = jnp.zeros((S, 16), F32)
    z32 = jnp.zeros((S, 32), F32)
    z64 = jnp.zeros((S, MLA_NOPE), F32)
    tc = jnp.concatenate([one, cos, cos, jnp.ones((S, 32), F32)], axis=1)
    ts1 = jnp.concatenate([z64, -sin, z16, z32], axis=1)
    ts2 = jnp.concatenate([z64, z16, sin, z32], axis=1)
    return tc, ts1, ts2


def kernel(x, c, ln_in_g, ln_in_b, w_ada, b_ada, w_in, q_norm_g, kv_norm_g, w_uq, w_ukv, w_o, ln1_g, ln1_b, w_up, w_down, ln2_g, ln2_b, loss_target, m_ln_in_g, m_ln_in_b, m_w_ada, m_b_ada, m_w_in, m_q_norm_g, m_kv_norm_g, m_w_uq, m_w_ukv, m_w_o, m_ln1_g, m_ln1_b, m_w_up, m_w_down, m_ln2_g, m_ln2_b, v_ln_in_g, v_ln_in_b, v_w_ada, v_b_ada, v_w_in, v_q_norm_g, v_kv_norm_g, v_w_uq, v_w_ukv, v_w_o, v_ln1_g, v_ln1_b, v_w_up, v_w_down, v_ln2_g, v_ln2_b):
    B, S, D = x.shape
    sbw = D // 2
    mlw = D - sbw
    nh = mlw // MLA_V
    qr = w_uq.shape[1]
    kvr = w_ukv.shape[1]
    qk = MLA_NOPE + MLA_ROPE
    dff = w_up.shape[2] * N_DEV
    din = w_in.shape[2] * N_DEV
    tm = 512 if S % 512 == 0 else S
    tq = min(512, S // 2)
    dm = dict(tm=tm, tm_small=min(tm, 256), tq=tq, sbw=sbw, qr=qr, kvr=kvr, nh=nh)
    dev =4 * lax.axis_index("x") + 2 * lax.axis_index("y") + lax.axis_index("c")

    big = [w_in, w_uq, w_ukv, w_o, w_up, w_down]
    first = [w_in[0].T.astype(BF16), w_uq[0].astype(BF16), w_ukv[0].astype(BF16)]
    first_w, first_token = _chip_exchange_start(first, "gather_w_first_start", scatter=False, after=c)

    nada = w_ada.shape[2]
    c_all = _all_gather([c + first_token[0, 0]], "gather_c")[0].reshape(N_DEV * B, D)
    b_loc = lax.dynamic_slice(b_ada, (0, dev * nada), (1, nada))
    cact_all, mod_part = _ada_partial(c_all, w_ada[0], b_loc)
    mod_all = _all_gather([mod_part], "gather_mod")[0]
    mod = lax.dynamic_slice(mod_all, (0, dev * B, 0), (N_DEV, B, nada))
    mod = jnp.swapaxes(mod, 0, 1).reshape(B, N_MOD, D)

    first_by_chip = _chip_exchange_wait(first_w, mod_all, "gather_w_first_wait")
    w_in8, w_uq8, w_ukv8 = [b.reshape((N_DEV,) + b.shape[2:]) for b in _core_gather(first_by_chip, "gather_w_first_cores")]
    late_w, late_token = _chip_exchange_start([a[0].astype(BF16) for a in big[3:]], "gather_w_late_start",
                                              scatter=False, after=w_in8, everyone=True)
    cols = lambda a8: jnp.swapaxes(a8, 0, 1).reshape(a8.shape[1], N_DEV * a8.shape[2])
    w_in_p = jnp.pad(w_in8.reshape(din, D), ((0, LANES - MLA_ROPE), (0, 0)))
    zpad = jnp.zeros((qr, nh, HEAD_PAD - qk), BF16)
    w_uq_p = jnp.concatenate([cols(w_uq8).reshape(qr, nh, qk), zpad], axis=2).reshape(qr, nh * HEAD_PAD)
    w_ukv_f = cols(w_ukv8)
    w_uk = w_ukv_f[:, :nh * MLA_NOPE].reshape(kvr, nh, MLA_NOPE)
    w_uk_p = jnp.concatenate([w_uk, jnp.zeros((kvr, nh, HEAD_PAD - MLA_NOPE), BF16)], axis=2)
    w_kv = jnp.concatenate([w_uk_p.reshape(kvr, nh * HEAD_PAD), w_ukv_f[:, nh * MLA_NOPE:]], axis=1)

    tc, ts1, ts2 = _rope_tables(S)
    g_in, b_in = ln_in_g.reshape(1, D), ln_in_b.reshape(1, D)
    (x0, sq, sk, sv, qp, kp, mv, cq, ckv, qn, kvn) = _inproj_fwd(
        x, mod, g_in, b_in, w_in_p, w_uq_p, w_kv, q_norm_g, kv_norm_g, tc, ts1, ts2, dm, late_token)
    sb_y, sb_tot = _sb_fwd(sq, sk, sv, dm)
    mla_y, mla_lse = _mla_fwd(qp, kp, mv, dm, sb_tot)
    w_o8, w_up8, w_down8 = _chip_exchange_wait(late_w, mla_lse, "gather_w_late_wait")
    w_o_f = w_o8.reshape(D, D)
    mix, x1, h2, u, dr2, bst_c, wst_c = _mlp_fwd(sb_y, mla_y, x0, mod, loss_target, w_o_f, w_up8, w_down8,
                                                  ln1_g, ln1_b, ln2_g, ln2_b, dm)

    du, dffb, dx0a, dsb_y, dmla_y, g_o, bst_b, wst_b = _mlp_bwd(
        dr2, u, x1, x0, mix, sb_y, mla_y, mod, w_up8, w_down8, w_o_f, ln1_g, dm)
    T = B * S
    r2 = lambda a: a.reshape(T, a.shape[2])
    by_core = lambda a: a.reshape((4, 2) + a.shape[1:])
    g_o = g_o.astype(BF16)
    g_up8 = _mm_tn(r2(h2), r2(du), "grad_w_up", dr2, out_dtype=BF16, col_blocks=N_DEV)
    g_down = _mm_tn(r2(u), r2(dffb), "grad_w_down", dr2, relu_sq=True, out_dtype=BF16)
    early = [g_o.reshape(N_DEV, D // N_DEV, D), g_up8, g_down.reshape(N_DEV, dff // N_DEV, D)]
    early_g, early_token = _chip_exchange_start(early, "scatter_g_early_start", scatter=True, after=dr2,
                                                everyone=True)

    dsq, dsk, dsv = _sb_bwd(sq, sk, sv, sb_tot, dsb_y, dm, early_token)
    dqp, dkp, dmv = _mla_bwd(qp, kp, mv, mla_y, mla_lse, dmla_y, dm, dsq)
    grad_x, g_in_p, g_uq_p, g_kv, bst_a, wst_a = _inproj_bwd(
        x, x0, dx0a, mod, g_in, dsq, dsk, dsv, dqp, dkp, dmv, cq, ckv, qn, kvn, w_in_p, w_uq_p, w_kv,
        q_norm_g, kv_norm_g, tc, ts1, ts2, dm)

    dmod = jnp.concatenate([bst_a[:, 1], bst_a[:, 0], bst_b[:, 2], bst_b[:, 1], bst_b[:, 0], bst_c[:, 0]], axis=1)
    small = jnp.concatenate([wst_a[0], wst_a[1], wst_a[4, :qr], wst_a[5, :kvr], wst_b[0], wst_b[1],
                             wst_c[0], wst_c[1], wst_c[2]])
    n1 = small.shape[0]
    small_g, small_token = _chip_exchange_start([_pack([dmod, small], F32, LANES)], "gather_small_start",
                                                scatter=False, after=grad_x)
    g_uq_f = g_uq_p.reshape(qr, nh, HEAD_PAD)[:, :, :qk].reshape(qr, nh * qk)
    g_uk = g_kv[:, :nh * HEAD_PAD].reshape(kvr, nh, HEAD_PAD)[:, :, :MLA_NOPE].reshape(kvr, nh * MLA_NOPE)
    g_ukv_f = jnp.concatenate([g_uk, g_kv[:, nh * HEAD_PAD:]], axis=1)
    early_quarter = _chip_exchange_wait(early_g, g_in_p, "scatter_g_early_wait")

    def by_dest_cols(a):
        k, n = a.shape[0], a.shape[1] // N_DEV
        return jnp.swapaxes(a.reshape(k, N_DEV, n), 0, 1).astype(BF16)

    g_in8 = (g_in_p[:din] + small_token[0, 0]).astype(BF16).reshape(N_DEV, din // N_DEV, D)
    last = [g_in8, by_dest_cols(g_uq_f), by_dest_cols(g_ukv_f)]
    last_sum = _core_scatter_sum([by_core(a) for a in last], "scatter_g_last_cores")
    small_by_chip = _chip_exchange_wait(small_g, last_sum[0], "gather_small_wait")
    both = _core_gather(small_by_chip, "gather_small_cores")[0].reshape(N_DEV, -1)
    last_g, last_token = _chip_exchange_start(last_sum, "scatter_g_last_start", scatter=True, after=grad_x)
    names = ["w_in", "w_uq", "w_ukv", "w_o", "w_up", "w_down"]
    moms = [m_w_in, m_w_uq, m_w_ukv, m_w_o, m_w_up, m_w_down]
    vars_ = [v_w_in, v_w_uq, v_w_ukv, v_w_o, v_w_up, v_w_down]
    res_early = [_reduce_adamw(p, w, m, v, "adamw_" + n)
                 for p, w, m, v, n in zip(early_quarter, big[3:], moms[3:], vars_[3:], names[3:])]

    dmod_all = both[:, :B * N_MOD * D].reshape(N_DEV * B, N_MOD * D)
    sm = both[:, B * N_MOD * D:B * N_MOD * D + n1] + last_token[0, 0]
    dmod_my = lax.dynamic_slice(dmod_all, (0, dev * nada), (N_DEV * B, nada))
    row = lambda arrs: jnp.concatenate([a.reshape(1, -1) for a in arrs], axis=1)
    smalls = [ln_in_g, ln_in_b, q_norm_g, kv_norm_g, ln1_g, ln1_b, ln2_g, ln2_b]
    small_shapes = [a.shape for a in smalls]
    (gs, ds, nms, nvs, g_b, d_b, nm_b, nv_b, g_w, d_w, nm_w, nv_w, loss_v) = _finish(
        sm, dmod_all, dmod_my, cact_all, row(smalls),
        row([m_ln_in_g, m_ln_in_b, m_q_norm_g, m_kv_norm_g, m_ln1_g, m_ln1_b, m_ln2_g, m_ln2_b]),
        row([v_ln_in_g, v_ln_in_b, v_q_norm_g, v_kv_norm_g, v_ln1_g, v_ln1_b, v_ln2_g, v_ln2_b]),
        b_ada, m_b_ada, v_b_ada, w_ada[0], m_w_ada[0], v_w_ada[0])
    gsm, dsm, nmsm, nvsm = (_unpack(s, small_shapes) for s in (gs, ds, nms, nvs))
    last_quarter = list(_chip_exchange_wait(last_g, loss_v, "scatter_g_last_wait"))
    last_quarter[0] = jnp.swapaxes(last_quarter[0], 1, 2)
    res_last =[_reduce_adamw(p, w, m, v, "adamw_" + n)
                for p, w, m, v, n in zip(last_quarter, big[:3], moms[:3], vars_[:3], names[:3])]
    gb, db, nmb, nvb = ([r[i] for r in res_last + res_early] for i in range(4))

    def ordered(sm_l, w_l, ada_w, ada_b):
        return [sm_l[0], sm_l[1], ada_w[None], ada_b, w_l[0], sm_l[2], sm_l[3], w_l[1], w_l[2], w_l[3],
                sm_l[4], sm_l[5], w_l[4], w_l[5], sm_l[6], sm_l[7]]

    loss = loss_v[0, 0]
    return (loss, grad_x, *ordered(gsm, gb, g_w, g_b), *ordered(dsm, db, d_w, d_b),
            *ordered(nmsm, nmb, nm_w, nm_b), *ordered(nvsm, nvb, nv_w, nv_b))
```

```python
import math

import jax
import jax.numpy as jnp
from jax import lax
from jax.experimental import pallas as pl
from jax.experimental.pallas import tpu as pltpu

F32 = jnp.float32
BF16 = jnp.bfloat16

SB_HD = 64
MLA_V = 64
MLA_NOPE = 64
MLA_ROPE = 32
HEAD_PAD = 128
CHUNK = 64
ROPE_BASE = 10000.0
LN_EPS = 1e-5
RMS_EPS = 1e-6
DEPTH = 1
ALPHA = (2.0 * DEPTH) ** 0.25
N_MOD = 6
ADAM_LR = 0.001
ADAM_B1 = 0.9
ADAM_B2 = 0.999
ADAM_EPS = 1e-08
ADAM_WD = 0.01
ADAM_STEP = 10
N_DEV = 8
LANES = 128
LOG2E = 1.4426950408889634
CUMSUM_W = 256
VMEM_LIMIT = 56 * 1024 * 1024
MESH = pl.DeviceIdType.MESH


def _dot(a, b):
    return jnp.dot(a, b, preferred_element_type=F32)


def _dot_nt(a, b):
    return lax.dot_general(a, b, (((1,), (1,)), ((), ())), preferred_element_type=F32)


def _dot_tn(a, b):
    return lax.dot_general(a, b, (((0,), (0,)), ((), ())), preferred_element_type=F32)


def _cparams(sem):
    return pltpu.CompilerParams(dimension_semantics=sem, vmem_limit_bytes=VMEM_LIMIT)


def _full(a):
    nd = a.ndim
    return pl.BlockSpec(a.shape, lambda *_: (0,) * nd, pipeline_mode=pl.Buffered(1))


def _tok(tm, w):
    return pl.BlockSpec((1, tm, w), lambda b, s: (b, s, 0))


def _perb(rows, w):
    return pl.BlockSpec((1, rows, w), lambda b, s: (b, 0, 0))


def _sds(shape, dtype):
    return jax.ShapeDtypeStruct(shape, dtype)


def _ln_fwd(x, g, b):
    mu = jnp.mean(x, axis=-1, keepdims=True)
    xc = x - mu
    var = jnp.mean(xc * xc, axis=-1, keepdims=True)
    rstd = lax.rsqrt(var + LN_EPS)
    xhat = xc * rstd
    return xhat * g + b, xhat, rstd


def _ln_bwd(dy, xhat, rstd, g):
    dxh = dy * g
    m1 = jnp.mean(dxh, axis=-1, keepdims=True)
    m2 = jnp.mean(dxh * xhat, axis=-1, keepdims=True)
    return rstd * (dxh - m1 - xhat * m2)


def _colsum(a):
    return jnp.sum(a, axis=0, keepdims=True)


def _rope(x, c, s1, s2):
    w = x.shape[-1]
    return x * c + pltpu.roll(x, w - 16, 1) * s1 + pltpu.roll(x, 16, 1) * s2


def _rope_t(x, c, s1, s2):
    w = x.shape[-1]
    return x * c - pltpu.roll(x, w - 16, 1) * s1 - pltpu.roll(x, 16, 1) * s2


def _adamw(w, g, m, v):
    m = ADAM_B1 * m + (1.0 - ADAM_B1) * g
    v = ADAM_B2 * v + (1.0 - ADAM_B2) * (g * g)
    m_hat = m / (1.0 - ADAM_B1 ** ADAM_STEP)
    v_hat = v / (1.0 - ADAM_B2 ** ADAM_STEP)
    delta = -ADAM_LR * (m_hat / (jnp.sqrt(v_hat) + ADAM_EPS) + ADAM_WD * w)
    return delta, m, v


def _my_place():
    return lax.axis_index("x"), lax.axis_index("y"), lax.axis_index("c")


def _chip_peers(mx, my):
    out = []
    for j in (1, 2, 3):
        px = 1 - mx if (j >> 1) else mx
        py = 1 - my if (j & 1) else my
        out.append((px, py, 2 * px + py))
    return out


def _split_peers(everyone):
    mx, my, mc = _my_place()
    if not everyone:
        return [(px, py, mc, pk) for px, py, pk in _chip_peers(mx, my)], 2 * mx + my
    peers = []
    for j in range(1, N_DEV):
        px = 1 - mx if (j >> 2) & 1 else mx
        py = 1 - my if (j >> 1) & 1 else my
        pc = 1 - mc if j & 1 else mc
        peers.append((px, py, pc, 4 * px + 2 * py + pc))
    return peers, 4 * mx + 2 * my + mc


def _hbm_call(body, name, n_in, out_shape, sems):
    hbm = pl.BlockSpec(memory_space=pl.ANY)
    return pl.pallas_call(
        body, name=name, out_shape=out_shape,
        in_specs=[hbm] * n_in, out_specs=[hbm] * len(out_shape),
        scratch_shapes=[pltpu.SemaphoreType.DMA(s) for s in sems])


def _chip_exchange(xs, name, scatter):
    n = len(xs)

    def body(*refs):
        x_refs, o_refs = refs[:n], refs[n:2 * n]
        ssem, rsem, lsem = refs[2 * n:]
        mx, my, mc = _my_place()
        me = 2 * mx + my
        peers = _chip_peers(mx, my)

        def copy(i, j, src_slot, dst_slot):
            px, py, _ = peers[j]
            return pltpu.make_async_remote_copy(
                src_ref=x_refs[i].at[src_slot] if scatter else x_refs[i], dst_ref=o_refs[i].at[dst_slot],
                send_sem=ssem.at[i, j], recv_sem=rsem.at[i, j], device_id=(px, py, mc), device_id_type=MESH)

        local = [pltpu.make_async_copy(x_refs[i].at[me] if scatter else x_refs[i], o_refs[i].at[me], lsem.at[i])
                 for i in range(n)]
        sends = [copy(i, j, peers[j][2], me) for i in range(n) for j in range(3)]
        for cp in local + sends:
            cp.start()
        for i in range(n):
            for j in range(3):
                copy(i, j, peers[j][2], peers[j][2]).wait_recv()
        for cp in sends:
            cp.wait_send()
        for cp in local:
            cp.wait()

    out_shape = [_sds((4,) + tuple(x.shape[1:] if scatter else x.shape), x.dtype) for x in xs]
    return _hbm_call(body, name, n, out_shape, [(n, 3), (n, 3), (n,)])(*xs)


def _chip_exchange_start(xs, name, scatter, after, everyone=False):
    n = len(xs)
    npeer = N_DEV - 1 if everyone else 3
    blks = [tuple(x.shape[1:] if scatter else x.shape) for x in xs]

    def body(*refs):
        x_refs, land_refs = refs[:n], refs[n:2 * n]
        ssem, rsem = refs[2 * n + 1], refs[2 * n + 2]
        token = refs[-1]
        peers, me = _split_peers(everyone)
        for i in range(n):
            for j, (px, py, pc, slot) in enumerate(peers):
                pltpu.make_async_remote_copy(
                    src_ref=x_refs[i].at[slot] if scatter else x_refs[i], dst_ref=land_refs[i].at[me],
                    send_sem=ssem.at[npeer * i + j], recv_sem=rsem.at[npeer * i + j], device_id=(px, py, pc),
                    device_id_type=MESH).start()
        token[...] = jnp.zeros_like(token)

    hbm = pl.BlockSpec(memory_space=pltpu.HBM)
    sem = pl.BlockSpec(memory_space=pltpu.SEMAPHORE)
    lands = [lax.empty((npeer + 1,) + b, x.dtype) for b, x in zip(blks, xs)]
    res = pl.pallas_call(
        body, name=name,
        out_shape=[pltpu.SemaphoreType.DMA((npeer * n,)), pltpu.SemaphoreType.DMA((npeer * n,))]
        + [pltpu.HBM(x.shape, x.dtype) for x in xs] + [pltpu.HBM(l.shape, l.dtype) for l in lands]
        + [_sds((8, LANES), F32)],
        in_specs=[hbm] * (2 * n) + [_AFTER],
        out_specs=[sem, sem] + [hbm] * (2 * n) + [pl.BlockSpec(memory_space=pltpu.VMEM)],
        input_output_aliases={i: 2 + i for i in range(2 * n)},
        compiler_params=pltpu.CompilerParams(has_side_effects=pltpu.SideEffectType.DATAFLOW_SIDE_EFFECTING),
    )(*[pltpu.with_memory_space_constraint(a, pltpu.HBM) for a in list(xs) + lands], after)
    return dict(ssem=res[0], rsem=res[1], xs=res[2:2 + n], lands=res[2 + n:2 + 2 * n], n=n, scatter=scatter,
                everyone=everyone), res[-1]


def _chip_exchange_wait(handle, after, name):
    n, scatter, everyone = handle["n"], handle["scatter"], handle["everyone"]
    npeer = N_DEV - 1 if everyone else 3

    def body(*refs):
        x_refs, land_refs = refs[:n], refs[n:2 * n]
        ssem, rsem = refs[2 * n], refs[2 * n + 1]
        peers, _ = _split_peers(everyone)
        for i in range(n):
            for j, (px, py, pc, slot) in enumerate(peers):
                cp = pltpu.make_async_remote_copy(
                    src_ref=x_refs[i].at[slot] if scatter else x_refs[i], dst_ref=land_refs[i].at[slot],
                    send_sem=ssem.at[npeer * i + j], recv_sem=rsem.at[npeer * i + j], device_id=(px, py, pc),
                    device_id_type=MESH)
                cp.wait_send()
                cp.wait_recv()

    hbm = pl.BlockSpec(memory_space=pltpu.HBM)
    sem = pl.BlockSpec(memory_space=pltpu.SEMAPHORE)
    ops = list(handle["xs"]) + list(handle["lands"])
    res = pl.pallas_call(
        body, name=name,
        out_shape=[pltpu.HBM(a.shape, a.dtype) for a in ops],
        in_specs=[hbm] * (2 * n) + [sem, sem, pl.BlockSpec(memory_space=pl.ANY)],
        out_specs=[hbm] * (2 * n),
        input_output_aliases={i: i for i in range(2 * n)},
        compiler_params=pltpu.CompilerParams(has_side_effects=pltpu.SideEffectType.DATAFLOW_SIDE_EFFECTING),
    )(*ops, handle["ssem"], handle["rsem"], after)
    me = 2 * lax.axis_index("x") + lax.axis_index("y")
    if everyone:
        me = 2 * me + lax.axis_index("c")
    out = []
    for x, land in zip(res[:n], res[n:]):
        own = lax.dynamic_index_in_dim(x, me, 0, keepdims=False) if scatter else x
        out.append(lax.dynamic_update_index_in_dim(land, own, me, 0))
    return out


def _core_gather(xs, name):
    n = len(xs)

    def body(*refs):
        x_refs, o_refs, mine, got = refs[:n], refs[n:2 * n], refs[2 * n:3 * n], refs[3 * n:4 * n]
        lsem, ssem, rsem, osem = refs[4 * n:]
        mx, my, mc = _my_place()
        loads = [pltpu.make_async_copy(x_refs[i], mine[i], lsem.at[i]) for i in range(n)]
        for cp in loads:
            cp.start()
        sends, stores = [], []
        for i in range(n):
            loads[i].wait()
            cp = pltpu.make_async_remote_copy(
                src_ref=mine[i], dst_ref=got[i], send_sem=ssem.at[i], recv_sem=rsem.at[i],
                device_id=(mx, my, 1 - mc), device_id_type=MESH)
            cp.start()
            sends.append(cp)
            for k in range(4):
                st = pltpu.make_async_copy(mine[i].at[k], o_refs[i].at[k, mc], osem.at[i, k])
                st.start()
                stores.append(st)
        for i in range(n):
            sends[i].wait_recv()
            for k in range(4):
                st = pltpu.make_async_copy(got[i].at[k], o_refs[i].at[k, 1 - mc], osem.at[n + i, k])
                st.start()
                stores.append(st)
        for cp in sends:
            cp.wait_send()
        for st in stores:
            st.wait()

    hbm = pl.BlockSpec(memory_space=pl.ANY)
    bufs = [pltpu.VMEM(x.shape, x.dtype) for x in xs]
    return pl.pallas_call(
        body, name=name,
        out_shape=[_sds((4, 2) + tuple(x.shape[1:]), x.dtype) for x in xs],
        in_specs=[hbm] * n, out_specs=[hbm] * n,
        scratch_shapes=bufs + bufs + [pltpu.SemaphoreType.DMA((n,)), pltpu.SemaphoreType.DMA((n,)),
                                      pltpu.SemaphoreType.DMA((n,)), pltpu.SemaphoreType.DMA((2 * n, 4))],
        compiler_params=pltpu.CompilerParams(vmem_limit_bytes=VMEM_LIMIT),
    )(*xs)


def _rows_step(k):
    for r in (256, 128, 64, 32, 16, 8):
        if k % r == 0:
            return r
    return k


def _core_scatter_sum(gs, name):
    n = len(gs)

    def body(*refs):
        g_refs, o_refs = refs[:n], refs[n:2 * n]
        send, got, mine = refs[2 * n:3 * n], refs[3 * n:4 * n], refs[4 * n:5 * n]
        lsem, msem, ssem, rsem, osem = refs[5 * n:]
        mx, my, mc = _my_place()
        pairs = [(i, k) for i in range(n) for k in range(4)]
        out_loads = {(i, k): pltpu.make_async_copy(g_refs[i].at[k, 1 - mc], send[i].at[k], lsem.at[i, k])
                     for i, k in pairs}
        own_loads = {(i, k): pltpu.make_async_copy(g_refs[i].at[k, mc], mine[i].at[k], msem.at[i, k])
                     for i, k in pairs}
        for p in pairs:
            out_loads[p].start()
        for p in pairs:
            own_loads[p].start()
        sends = []
        for i in range(n):
            for k in range(4):
                out_loads[i, k].wait()
            cp = pltpu.make_async_remote_copy(
                src_ref=send[i], dst_ref=got[i], send_sem=ssem.at[i], recv_sem=rsem.at[i],
                device_id=(mx, my, 1 - mc), device_id_type=MESH)
            cp.start()
            sends.append(cp)
        stores = []
        for i in range(n):
            for k in range(4):
                own_loads[i, k].wait()
            sends[i].wait_recv()
            rows = g_refs[i].shape[2]
            step = _rows_step(rows)

            def add(r, _, i=i, step=step):
                sl = pl.ds(pl.multiple_of(r * step, step), step)
                for k in range(4):
                    mine[i][k, sl, :] = (mine[i][k, sl, :].astype(F32) + got[i][k, sl, :].astype(F32)).astype(BF16)
                return 0

            lax.fori_loop(0, rows // step, add, 0)
            st = pltpu.make_async_copy(mine[i], o_refs[i], osem.at[i])
            st.start()
            stores.append(st)
        for cp in sends:
            cp.wait_send()
        for st in stores:
            st.wait()

    hbm = pl.BlockSpec(memory_space=pl.ANY)
    blk = [(4,) + tuple(g.shape[2:]) for g in gs]
    bufs = [pltpu.VMEM(b, BF16) for b in blk]
    return pl.pallas_call(
        body, name=name,
        out_shape=[_sds(b, BF16) for b in blk],
        in_specs=[hbm] * n, out_specs=[hbm] * n,
        scratch_shapes=bufs * 3 + [pltpu.SemaphoreType.DMA((n, 4)), pltpu.SemaphoreType.DMA((n, 4)),
                                   pltpu.SemaphoreType.DMA((n,)), pltpu.SemaphoreType.DMA((n,)),
                                   pltpu.SemaphoreType.DMA((n,))],
        compiler_params=pltpu.CompilerParams(vmem_limit_bytes=VMEM_LIMIT),
    )(*gs)


def _all_gather(xs, name):
    by_chip = _chip_exchange(xs, name + "_chips", scatter=False)
    both = _core_gather(by_chip, name + "_cores")
    return [b.reshape((N_DEV,) + tuple(x.shape)) for b, x in zip(both, xs)]


def _ada_partial(c_all, w_ada_loc, b_loc):
    def body(c_ref, w_ref, b_ref, act_ref, mod_ref):
        c = c_ref[...]
        act = c * (1.0 / (1.0 + jnp.exp(-c)))
        act_ref[...] = act
        mod_ref[...] = _dot(act.astype(BF16), w_ref[...].astype(BF16)) + b_ref[...]

    nb, d = c_all.shape
    return pl.pallas_call(
        body, name="ada_partial",
        out_shape=(_sds((nb, d), F32), _sds((nb, w_ada_loc.shape[1]), F32)),
        compiler_params=pltpu.CompilerParams(vmem_limit_bytes=VMEM_LIMIT),
    )(c_all, w_ada_loc, b_loc)


_AFTER = pl.BlockSpec(memory_space=pl.ANY)


def _inproj_fwd(x, mod, ln_g, ln_b, w_in_p, w_uq_p, w_kv, gq, gkv, tc, ts1, ts2, dm, after):
    B, S, D = x.shape
    tm = dm["tm"]
    sbw, qr, kvr, nh = dm["sbw"], dm["qr"], dm["kvr"], dm["nh"]
    o_cq, o_ckv, o_kr = 3 * sbw, 3 * sbw + qr, 3 * sbw + qr + kvr
    qpw = nh * HEAD_PAD

    def body(x_ref, mod_ref, g_ref, b_ref, win_ref, wuq_ref, wkv_ref, gq_ref, gkv_ref, tc_ref, ts1_ref, ts2_ref, _,
             x0_ref, q_ref, k_ref, v_ref, qp_ref, kp_ref, mv_ref, cq_ref, ckv_ref, qn_ref, kvn_ref):
        x0, _, _ = _ln_fwd(x_ref[0], g_ref[...], b_ref[...])
        x0_ref[0] = x0
        mod = mod_ref[0]
        h = (x0 * (1.0 + mod[1:2]) + mod[0:1]).astype(BF16)
        proj = _dot_nt(h, win_ref[...])
        q_ref[0] = (proj[:, 0:sbw] * SB_Q_SCALE).astype(BF16)
        k_ref[0] = proj[:, sbw:2 * sbw].astype(BF16)
        v_ref[0] = proj[:, 2 * sbw:3 * sbw].astype(BF16)
        cq = proj[:, o_cq:o_cq + qr]
        ckv = proj[:, o_ckv:o_ckv + kvr]
        cq_ref[0] = cq
        ckv_ref[0] = ckv
        qn = (cq * lax.rsqrt(jnp.mean(cq * cq, axis=-1, keepdims=True) + RMS_EPS) * gq_ref[...]).astype(BF16)
        kvn = (ckv * lax.rsqrt(jnp.mean(ckv * ckv, axis=-1, keepdims=True) + RMS_EPS) * gkv_ref[...]).astype(BF16)
        qn_ref[0] = qn
        kvn_ref[0] = kvn
        c1, s1, s2 = tc_ref[...], ts1_ref[...], ts2_ref[...]
        c8, s18, s28 = jnp.tile(c1, (1, nh)), jnp.tile(s1, (1, nh)), jnp.tile(s2, (1, nh))
        qp_ref[0] = (_rope(_dot(qn, wuq_ref[...]), c8, s18, s28) * MLA_Q_SCALE).astype(BF16)
        kvo = _dot(kvn, wkv_ref[...])
        kr = pltpu.roll(proj[:, o_kr:o_kr + LANES], 64, 1)
        kr = _rope(kr, c1, s1, s2)
        kp_ref[0] = (kvo[:, 0:qpw] + jnp.tile(kr, (1, nh))).astype(BF16)
        mv_ref[0] = kvo[:, qpw:].astype(BF16)

    tab = pl.BlockSpec((tm, LANES), lambda b, s: (s, 0))
    outs = [(D, F32), (sbw, BF16), (sbw, BF16), (sbw, BF16), (qpw, BF16), (qpw, BF16),
            (nh * MLA_V, BF16), (qr, F32), (kvr, F32), (qr, BF16), (kvr, BF16)]
    return pl.pallas_call(
        body, name="inproj_fwd", grid=(B, S // tm),
        in_specs=[_tok(tm, D), _perb(N_MOD, D), _full(ln_g), _full(ln_b), _full(w_in_p), _full(w_uq_p),
                  _full(w_kv), _full(gq), _full(gkv), tab, tab, tab, _AFTER],
        out_specs=[_tok(tm, w) for w, _ in outs],
        out_shape=[_sds((B, S, w), t) for w, t in outs],
        compiler_params=_cparams(("parallel", "parallel")),
    )(x, mod, ln_g, ln_b, w_in_p, w_uq_p, w_kv, gq, gkv, tc, ts1, ts2, after)


def _neg_abs(x):
    sign = jnp.uint32(0x80000000)
    return lax.bitcast_convert_type(lax.bitcast_convert_type(x, jnp.uint32) | sign, F32)


SB_Q_SCALE = -(SB_HD ** -0.5) * LOG2E
MLA_Q_SCALE = (MLA_NOPE + MLA_ROPE) ** -0.5 * LOG2E


def _log2_keep(zs):
    return jnp.minimum(zs, 0.0) - jnp.log2(1.0 + jnp.exp2(_neg_abs(zs)))


def _split_dot(a, u):
    hi = a.astype(BF16)
    lo = (a - hi.astype(F32)).astype(BF16)
    return _dot(jnp.concatenate([hi, lo], axis=1), jnp.concatenate([u, u], axis=0))


def _tri(n, rel):
    row = lax.broadcasted_iota(jnp.int32, (n, n), 0)
    col = lax.broadcasted_iota(jnp.int32, (n, n), 1)
    return rel(row, col).astype(BF16)


def _running_sum(a, tri, reverse, split, start):
    cs = tri.shape[0]
    n = a.shape[1] // cs
    out = [None] * n
    run = start
    for c in (reversed(range(n)) if reverse else range(n)):
        part = a[:, c * cs:(c + 1) * cs]
        out[c] = (_split_dot(part, tri) if split else _dot(part.astype(BF16), tri)) + run
        run = run + jnp.sum(part, axis=1, keepdims=True)
    return (out[0] if n == 1 else jnp.concatenate(out, axis=1)), run


def _transpose_bf16(a):
    return a.astype(F32).T.astype(BF16)


def _tile_mask(nr, nk, r0, c0, rel):
    row = lax.broadcasted_iota(jnp.int32, (nr, nk), 0) + r0
    col = lax.broadcasted_iota(jnp.int32, (nr, nk), 1) + c0
    return rel(row, col)


def _put_rows(whole, part, r0):
    return part if r0 == 0 else jnp.concatenate([whole[:r0], part], axis=0)


def _diag_tiles(tq, split):
    half = tq // 2
    return [(0, tq, 0, half), (half, half, half, half)] if split else [(0, tq, 0, tq)]


def _sb_fwd(q, k, v, dm):
    B, S, W = q.shape
    tq = dm["tq"]
    nq = S // tq

    def body(q_ref, k_ref, v_ref, y_ref, tot_ref):
        qi = pl.program_id(2)
        q2 = q_ref[0]
        lane = lax.broadcasted_iota(jnp.int32, (tq, LANES), 1)
        qs = jnp.concatenate([jnp.where(lane < SB_HD, q2, 0), jnp.where(lane >= SB_HD, q2, 0)], axis=0).astype(BF16)
        later = _tri(min(tq, CUMSUM_W), lambda a, b: a > b)
        assert tq & (tq - 1) == 0
        strict = _tile_mask(2 * tq, tq, 0, 0, lambda t, s: s < (t & (tq - 1)))

        def block(j, carry, masked):
            acc, run = carry
            off = pl.multiple_of(j * tq, tq)
            zs = _dot_nt(qs, k_ref[0, pl.ds(off, tq), :])
            a = _log2_keep(zs)
            if masked:
                a = jnp.where(strict, a, 0.0)
            a_later, run = _running_sum(a, later, reverse=True, split=True, start=run)
            w = jnp.exp2((a - zs) + a_later)
            if masked:
                w = jnp.where(strict, w, 0.0)
            return acc + _dot(w.astype(BF16), v_ref[0, pl.ds(off, tq), :]), run

        carry = block(qi, (jnp.zeros((2 * tq, LANES), F32), jnp.zeros((2 * tq, 1), F32)), True)
        acc, run = lax.fori_loop(0, qi, lambda jj, c: block(qi - 1 - jj, c, False), carry)
        y_ref[0] = jnp.where(lane < SB_HD, acc[:tq], acc[tq:]).astype(BF16)
        tot_ref[0] = jnp.where(lane < SB_HD, run[:tq], run[tq:])

    qspec = pl.BlockSpec((1, tq, LANES), lambda b, hp, i: (b, i, hp))
    kspec = pl.BlockSpec((1, S, LANES), lambda b, hp, i: (b, 0, hp))
    return pl.pallas_call(
        body, name="sb_fwd", grid=(B, W // LANES, nq),
        in_specs=[qspec, kspec, kspec],
        out_specs=[qspec, qspec],
        out_shape=[_sds((B, S, W), BF16), _sds((B, S, W), F32)],
        compiler_params=_cparams(("parallel", "parallel", "arbitrary")),
    )(q, k, v)


def _sb_bwd(q, k, v, tot, dy, dm, after):
    B, S, W = q.shape
    tq = dm["tq"]
    nq = S // tq

    def body(q_ref, k_ref, v_ref, tot_ref, dy_ref, _, dq_ref, dk_ref, dv_ref, dk_acc, dv_acc):
        qi = pl.program_id(2)

        @pl.when(qi == 0)
        def _():
            dk_acc[...] = jnp.zeros_like(dk_acc)
            dv_acc[...] = jnp.zeros_like(dv_acc)

        q2 = q_ref[0]
        dy2 = dy_ref[0]
        tot2 = tot_ref[0]
        lane = lax.broadcasted_iota(jnp.int32, (tq, LANES), 1)
        in_h = [lane < SB_HD, lane >= SB_HD]
        qh = [jnp.where(m, q2, 0).astype(BF16) for m in in_h]
        dyh = [jnp.where(m, dy2, 0).astype(BF16) for m in in_h]
        q_t = [_transpose_bf16(a) for a in qh]
        dy_t = [_transpose_bf16(a) for a in dyh]
        toth = [tot2[:, 0:1], tot2[:, SB_HD:SB_HD + 1]]

        def tile(j, carry, r0, nr, c0, nk, masked):
            off = pl.multiple_of(j * tq + c0, math.gcd(tq, c0))
            k2 = k_ref[0, pl.ds(off, nk), :]
            v2 = v_ref[0, pl.ds(off, nk), :]
            upto = _tri(min(nk, CUMSUM_W), lambda a, b: a <= b)
            before = _tri(min(nk, CUMSUM_W), lambda a, b: a < b)
            strict = _tile_mask(nr, nk, r0, c0, lambda t, s: s < t) if masked else None
            rows = slice(r0, r0 + nr)
            new = []
            dk_blk = jnp.zeros((LANES, nk), F32)
            dv_blk = jnp.zeros((LANES, nk), F32)
            for h in range(2):
                dq, pa, pg = carry[3 * h][rows], carry[3 * h + 1][rows], carry[3 * h + 2][rows]
                zs = _dot_nt(qh[h][rows], k2)
                a = _log2_keep(zs)
                if masked:
                    a = jnp.where(strict, a, 0.0)
                a_upto, pa = _running_sum(a, upto, reverse=False, split=True, start=pa)
                w = jnp.exp2((a - zs) - a_upto)
                if masked:
                    w = jnp.where(strict, w, 0.0)
                g = _dot_nt(dyh[h][rows], v2) * w
                g_before, pg = _running_sum(g, before, reverse=False, split=False, start=pg)
                dz = (g + g_before) * jnp.exp2(a) - g_before
                if masked:
                    dz = jnp.where(strict, dz, 0.0)
                dzb = dz.astype(BF16)
                dv_blk = dv_blk + _dot(dy_t[h][:, rows], w.astype(BF16))
                dk_blk = dk_blk + _dot(q_t[h][:, rows], dzb)
                new += [_put_rows(carry[3 * h], dq + _dot(dzb, k2), r0), _put_rows(carry[3 * h + 1], pa, r0),
                        _put_rows(carry[3 * h + 2], pg, r0)]
            dk_acc[j, :, c0:c0 + nk] += dk_blk
            dv_acc[j, :, c0:c0 + nk] += dv_blk
            return tuple(new)

        zero = jnp.zeros((tq, LANES), F32)
        zrun = jnp.zeros((tq, 1), F32)
        half = tq // 2
        carry = lax.fori_loop(
            0, qi, lambda j, c: tile(j, tile(j, c, 0, tq, 0, half, False), 0, tq, half, half, False),
            (zero, -toth[0], zrun, zero, -toth[1], zrun))
        for r0, nr, c0, nk in _diag_tiles(tq, False):
            carry = tile(qi, carry, r0, nr, c0, nk, True)
        dq_ref[0] = (jnp.where(in_h[0], carry[0], carry[3]) * (SB_HD ** -0.5)).astype(BF16)

        @pl.when(qi == nq - 1)
        def _():
            for jb in range(nq):
                dk_ref[0, jb * tq:(jb + 1) * tq, :] = (dk_acc[jb].T * (-1.0 / LOG2E)).astype(BF16)
                dv_ref[0, jb * tq:(jb + 1) * tq, :] = dv_acc[jb].T.astype(BF16)

    qspec = pl.BlockSpec((1, tq, LANES), lambda b, hp, i: (b, i, hp))
    kspec = pl.BlockSpec((1, S, LANES), lambda b, hp, i: (b, 0, hp))
    return pl.pallas_call(
        body, name="sb_bwd", grid=(B, W // LANES, nq),
        in_specs=[qspec, kspec, kspec, qspec, qspec, _AFTER],
        out_specs=[qspec, kspec, kspec],
        out_shape=[_sds((B, S, W), BF16)] * 3,
        scratch_shapes=[pltpu.VMEM((nq, LANES, tq), F32), pltpu.VMEM((nq, LANES, tq), F32)],
        compiler_params=_cparams(("parallel", "parallel", "arbitrary")),
    )(q, k, v, tot, dy, after)


def _same_or_earlier_chunk(row, col):
    return lax.shift_right_logical(col, 6) <= lax.shift_right_logical(row, 6)


def _mla_fwd(qp, kp, mv, dm, after):
    B, S, QW = qp.shape
    VW = mv.shape[2]
    tq = dm["tq"]
    nq = S // tq
    assert CHUNK == 64

    def body(q_ref, k_ref, v_ref, _, y_ref, lse_ref):
        qi = pl.program_id(2)
        q2 = q_ref[0]
        lane = lax.broadcasted_iota(jnp.int32, (tq, LANES), 1)

        def tile(j, carry, r0, nr, c0, nk, masked):
            off = pl.multiple_of(j * tq + c0, math.gcd(tq, c0))
            v2 = v_ref[0, pl.ds(off, nk), :]
            allowed = _tile_mask(nr, nk, r0, c0, _same_or_earlier_chunk) if masked else None
            rows = slice(r0, r0 + nr)
            heads = range(2)
            sl = [slice(h * HEAD_PAD, (h + 1) * HEAD_PAD) for h in heads]
            m_old = [carry[3 * h + 1][rows] for h in heads]
            s = [_dot_nt(q2[rows, sl[h]], k_ref[0, pl.ds(off, nk), sl[h]]) for h in heads]
            if masked:
                s = [jnp.where(allowed, s[h], -1e30) for h in heads]
            m_new = [jnp.maximum(m_old[h], jnp.max(s[h], axis=1, keepdims=True)) for h in heads]
            alpha = [jnp.exp2(m_old[h] - m_new[h]) for h in heads]
            p = [jnp.exp2(s[h] - m_new[h]) for h in heads]
            acc = [alpha[h] * carry[3 * h][rows] + _dot(p[h].astype(BF16), v2) for h in heads]
            l = [alpha[h] * carry[3 * h + 2][rows] + jnp.sum(p[h], axis=1, keepdims=True) for h in heads]
            out = []
            for h in heads:
                out += [_put_rows(carry[3 * h], acc[h], r0), _put_rows(carry[3 * h + 1], m_new[h], r0),
                        _put_rows(carry[3 * h + 2], l[h], r0)]
            return tuple(out)

        zero = jnp.zeros((tq, LANES), F32)
        m0 = jnp.full((tq, 1), -1e30, F32)
        l0 = jnp.zeros((tq, 1), F32)
        carry = (zero, m0, l0, zero, m0, l0)
        for r0, nr, c0, nk in _diag_tiles(tq, False):
            carry = tile(qi, carry, r0, nr, c0, nk, True)
        carry = lax.fori_loop(0, qi, lambda j, c: tile(j, c, 0, tq, 0, tq, False), carry)
        y0 = carry[0] / carry[2]
        y1 = carry[3] / carry[5]
        y_ref[0] = jnp.where(lane < MLA_V, y0, y1).astype(BF16)
        lse_ref[0] = jnp.where(lane < MLA_V, carry[1] + jnp.log2(carry[2]), carry[4] + jnp.log2(carry[5]))

    qspec = pl.BlockSpec((1, tq, 2 * HEAD_PAD), lambda b, hp, i: (b, i, hp))
    kspec = pl.BlockSpec((1, S, 2 * HEAD_PAD), lambda b, hp, i: (b, 0, hp))
    vspec = pl.BlockSpec((1, S, LANES), lambda b, hp, i: (b, 0, hp))
    yspec = pl.BlockSpec((1, tq, LANES), lambda b, hp, i: (b, i, hp))
    return pl.pallas_call(
        body, name="mla_fwd", grid=(B, VW // LANES, nq),
        in_specs=[qspec, kspec, vspec, _AFTER],
        out_specs=[yspec, yspec],
        out_shape=[_sds((B, S, VW), BF16), _sds((B, S, VW), F32)],
        compiler_params=_cparams(("parallel", "parallel", "arbitrary")),
    )(qp, kp, mv, after)


def _mla_bwd(qp, kp, mv, y, lse, dy, dm, after):
    B, S, QW = qp.shape
    VW = mv.shape[2]
    tq = dm["tq"]
    nq = S // tq
    scale = (MLA_NOPE + MLA_ROPE) ** -0.5

    def body(q_ref, k_ref, v_ref, y_ref, lse_ref, dy_ref, _, dq_ref, dk_ref, dv_ref, dk_acc, dv_acc):
        qi = pl.program_id(2)

        @pl.when(qi == 0)
        def _():
            dk_acc[...] = jnp.zeros_like(dk_acc)
            dv_acc[...] = jnp.zeros_like(dv_acc)

        q2 = q_ref[0]
        dy2 = dy_ref[0]
        lse2 = lse_ref[0]
        lane = lax.broadcasted_iota(jnp.int32, (tq, LANES), 1)
        in_h = [lane < MLA_V, lane >= MLA_V]
        prod = dy2.astype(F32) * y_ref[0].astype(F32)
        delta = [jnp.sum(jnp.where(m, prod, 0.0), axis=1, keepdims=True) for m in in_h]
        dyh = [jnp.where(m, dy2, 0).astype(BF16) for m in in_h]
        lseh = [lse2[:, 0:1], lse2[:, MLA_V:MLA_V + 1]]
        q_t = _transpose_bf16(q2)
        dy_t = [_transpose_bf16(a) for a in dyh]

        def tile(j, carry, r0, nr, c0, nk, masked):
            off = pl.multiple_of(j * tq + c0, math.gcd(tq, c0))
            v2 = v_ref[0, pl.ds(off, nk), :]
            allowed = _tile_mask(nr, nk, r0, c0, _same_or_earlier_chunk) if masked else None
            rows = slice(r0, r0 + nr)
            keys = slice(c0, c0 + nk)
            heads = range(2)
            sl = [slice(h * HEAD_PAD, (h + 1) * HEAD_PAD) for h in heads]
            qhh = [q2[rows, sl[h]] for h in heads]
            dyr = [dyh[h][rows] for h in heads]
            kh = [k_ref[0, pl.ds(off, nk), sl[h]] for h in heads]
            s = [_dot_nt(qhh[h], kh[h]) for h in heads]
            dp = [_dot_nt(dyr[h], v2) for h in heads]
            if masked:
                s = [jnp.where(allowed, s[h], -1e30) for h in heads]
            p = [jnp.exp2(s[h] - lseh[h][rows]) for h in heads]
            dv_acc[j, :, keys] += (_dot(dy_t[0][:, rows], p[0].astype(BF16))
                                   + _dot(dy_t[1][:, rows], p[1].astype(BF16)))
            ds = [(p[h] * (dp[h] - delta[h][rows])).astype(BF16) for h in heads]
            for h in heads:
                dk_acc[j, sl[h], keys] += _dot(q_t[sl[h], rows], ds[h])
            return tuple(_put_rows(carry[h], carry[h][rows] + _dot(ds[h], kh[h]), r0) for h in heads)

        zero = jnp.zeros((tq, HEAD_PAD), F32)
        half = tq // 2
        carry = lax.fori_loop(
            0, qi, lambda j, c: tile(j, tile(j, c, 0, tq, 0, half, False), 0, tq, half, half, False), (zero, zero))
        for r0, nr, c0, nk in _diag_tiles(tq, True):
            carry = tile(qi, carry, r0, nr, c0, nk, True)
        dq_ref[0] = (jnp.concatenate([carry[0], carry[1]], axis=1) * scale).astype(BF16)

        @pl.when(qi == nq - 1)
        def _():
            for jb in range(nq):
                dk_ref[0, jb * tq:(jb + 1) * tq, :] = (dk_acc[jb].T * (1.0 / LOG2E)).astype(BF16)
                dv_ref[0, jb * tq:(jb + 1) * tq, :] = dv_acc[jb].T.astype(BF16)

    qspec = pl.BlockSpec((1, tq, 2 * HEAD_PAD), lambda b, hp, i: (b, i, hp))
    kspec = pl.BlockSpec((1, S, 2 * HEAD_PAD), lambda b, hp, i: (b, 0, hp))
    vspec = pl.BlockSpec((1, S, LANES), lambda b, hp, i: (b, 0, hp))
    yspec = pl.BlockSpec((1, tq, LANES), lambda b, hp, i: (b, i, hp))
    return pl.pallas_call(
        body, name="mla_bwd", grid=(B, VW // LANES, nq),
        in_specs=[qspec, kspec, vspec, yspec, yspec, yspec, _AFTER],
        out_specs=[qspec, kspec, vspec],
        out_shape=[_sds((B, S, QW), BF16), _sds((B, S, QW), BF16), _sds((B, S, VW), BF16)],
        scratch_shapes=[pltpu.VMEM((nq, 2 * HEAD_PAD, tq), F32), pltpu.VMEM((nq, LANES, tq), F32)],
        compiler_params=_cparams(("parallel", "parallel", "arbitrary")),
    )(qp, kp, mv, y, lse, dy, after)


def _stat_specs(B, D):
    specs = [pl.BlockSpec((1, 8, D), lambda b, s: (b, 0, 0)), pl.BlockSpec((8, D), lambda b, s: (0, 0))]
    shapes = [_sds((B, 8, D), F32), _sds((8, D), F32)]
    return specs, shapes


def _stat_init(bst_ref, wst_ref):
    @pl.when(pl.program_id(1) == 0)
    def _():
        bst_ref[...] = jnp.zeros_like(bst_ref)

    @pl.when((pl.program_id(0) == 0) & (pl.program_id(1) == 0))
    def _():
        wst_ref[...] = jnp.zeros_like(wst_ref)


def _mlp_fwd(sb_y, mla_y, x0, mod, target, w_o, w_up, w_down, ln1_g, ln1_b, ln_g, ln_b, dm):
    B, S, D = x0.shape
    tm = dm["tm"]
    sbw = sb_y.shape[2]
    nck, _, ck = w_up.shape
    dff = nck * ck

    def body(ya_ref, yb_ref, x0_ref, mod_ref, t_ref, wo_ref, wu_ref, wd_ref, g1_ref, b1_ref, g_ref, b_ref,
             mix_ref, x1_ref, h2_ref, u_ref, dr_ref, bst_ref, wst_ref):
        _stat_init(bst_ref, wst_ref)
        mod = mod_ref[0]
        mix = _dot(ya_ref[0], wo_ref[0:sbw, :]) + _dot(yb_ref[0], wo_ref[sbw:, :])
        mix_ref[0] = mix
        x1, _, _ = _ln_fwd(ALPHA * x0_ref[0] + (1.0 + mod[2:3]) * mix, g1_ref[...], b1_ref[...])
        x1_ref[0] = x1
        h2 = (x1 * (1.0 + mod[4:5]) + mod[3:4]).astype(BF16)
        h2_ref[0] = h2
        g = g_ref[...]
        ff = jnp.zeros((tm, D), F32)
        for c in range(nck):
            u = _dot(h2, wu_ref[c])
            u_ref[0, :, c * ck:(c + 1) * ck] = u.astype(BF16)
            act = jnp.square(jnp.maximum(u, 0.0)).astype(BF16)
            ff = ff + _dot(act, wd_ref[c])
        x2, xhat, rstd = _ln_fwd(ALPHA * x1 + (1.0 + mod[5:6]) * ff, g, b_ref[...])
        err = x2 - t_ref[0]
        dy = err * (1.0 / D)
        dr = _ln_bwd(dy, xhat, rstd, g)
        dr_ref[0] = dr
        bst_ref[0, 0:1, :] += _colsum(dr * ff)
        wst_ref[0:1, :] += _colsum(dy * xhat)
        wst_ref[1:2, :] += _colsum(dy)
        wst_ref[2:3, :] += _colsum(err * err) * (0.5 / D)

    sspecs, sshapes = _stat_specs(B, D)
    return pl.pallas_call(
        body, name="mlp_fwd", grid=(B, S // tm),
        in_specs=[_tok(tm, sbw), _tok(tm, mla_y.shape[2]), _tok(tm, D), _perb(N_MOD, D), _tok(tm, D),
                  _full(w_o), _full(w_up), _full(w_down), _full(ln1_g), _full(ln1_b), _full(ln_g), _full(ln_b)],
        out_specs=[_tok(tm, D), _tok(tm, D), _tok(tm, D), _tok(tm, dff), _tok(tm, D)] + sspecs,
        out_shape=[_sds((B, S, D), F32), _sds((B, S, D), F32), _sds((B, S, D), BF16), _sds((B, S, dff), BF16),
                   _sds((B, S, D), F32)] + sshapes,
        compiler_params=_cparams(("arbitrary", "arbitrary")),
    )(sb_y, mla_y, x0, mod, target, w_o, w_up, w_down, ln1_g, ln1_b, ln_g, ln_b)


def _mlp_bwd(dr2, u, x1, x0, mix, sb_y, mla_y, mod, w_up, w_down, w_o, ln_g, dm):
    B, S, D = x1.shape
    tm = dm["tm_small"]
    sbw = dm["sbw"]
    nck, _, ck = w_up.shape
    dff = nck * ck

    def body(dr_ref, u_ref, x1_ref, x0_ref, mix_ref, ya_ref, yb_ref, mod_ref, wu_ref, wd_ref, wo_ref, g_ref,
             du_ref, dff_ref, dx0_ref, dya_ref, dyb_ref, go_ref, bst_ref, wst_ref):
        _stat_init(bst_ref, wst_ref)

        @pl.when((pl.program_id(0) == 0) & (pl.program_id(1) == 0))
        def _():
            go_ref[...] = jnp.zeros_like(go_ref)

        mod = mod_ref[0]
        dr2 = dr_ref[0]
        dffv = ((1.0 + mod[5:6]) * dr2).astype(BF16)
        dff_ref[0] = dffv
        dh2 = jnp.zeros((tm, D), F32)
        for c in range(nck):
            sl = slice(c * ck, (c + 1) * ck)
            da = _dot_nt(dffv, wd_ref[c])
            du = (da * (2.0 * jnp.maximum(u_ref[0, :, sl].astype(F32), 0.0))).astype(BF16)
            du_ref[0, :, sl] = du
            dh2 = dh2 + _dot_nt(du, wu_ref[c])
        x1 = x1_ref[0]
        dx1 = ALPHA * dr2 + dh2 * (1.0 + mod[4:5])
        bst_ref[0, 0:1, :] += _colsum(dh2 * x1)
        bst_ref[0, 1:2, :] += _colsum(dh2)
        mix = mix_ref[0]
        g = g_ref[...]
        _, xhat, rstd = _ln_fwd(ALPHA * x0_ref[0] + (1.0 + mod[2:3]) * mix, g, 0.0)
        dr1 = _ln_bwd(dx1, xhat, rstd, g)
        wst_ref[0:1, :] += _colsum(dx1 * xhat)
        wst_ref[1:2, :] += _colsum(dx1)
        bst_ref[0, 2:3, :] += _colsum(dr1 * mix)
        dx0_ref[0] = ALPHA * dr1
        dmix = ((1.0 + mod[2:3]) * dr1).astype(BF16)
        dya_ref[0] = _dot_nt(dmix, wo_ref[0:sbw, :]).astype(BF16)
        dyb_ref[0] = _dot_nt(dmix, wo_ref[sbw:, :]).astype(BF16)
        go_ref[0:sbw, :] += _dot_tn(ya_ref[0], dmix)
        go_ref[sbw:, :] += _dot_tn(yb_ref[0], dmix)

    sspecs, sshapes = _stat_specs(B, D)
    wa, wb = sbw, w_o.shape[0] - sbw
    return pl.pallas_call(
        body, name="mlp_bwd", grid=(B, S // tm),
        in_specs=[_tok(tm, D), _tok(tm, dff), _tok(tm, D), _tok(tm, D), _tok(tm, D), _tok(tm, wa), _tok(tm, wb),
                  _perb(N_MOD, D), _full(w_up), _full(w_down), _full(w_o), _full(ln_g)],
        out_specs=[_tok(tm, dff), _tok(tm, D), _tok(tm, D), _tok(tm, wa), _tok(tm, wb),
                   pl.BlockSpec(w_o.shape, lambda b, s: (0, 0))] + sspecs,
        out_shape=[_sds((B, S, dff), BF16), _sds((B, S, D), BF16), _sds((B, S, D), F32),
                   _sds((B, S, wa), BF16), _sds((B, S, wb), BF16), _sds(w_o.shape, F32)] + sshapes,
        compiler_params=_cparams(("arbitrary", "arbitrary")),
    )(dr2, u, x1, x0, mix, sb_y, mla_y, mod, w_up, w_down, w_o, ln_g)


def _inproj_bwd(x, x0, dx0a, mod, ln_g, dq, dk, dv, dqp, dkp, dmv, cq, ckv, qn, kvn, w_in_p, w_uq_p, w_kv, gq, gkv,
                tc, ts1, ts2, dm):
    B, S, D = x.shape
    tm = dm["tm"]
    sbw, qr, kvr, nh = dm["sbw"], dm["qr"], dm["kvr"], dm["nh"]
    qpw = nh * HEAD_PAD
    dinp = w_in_p.shape[0]
    kvw = w_kv.shape[1]

    def body(x_ref, x0_ref, dx0a_ref, mod_ref, g_ref, dq_ref, dk_ref, dv_ref, dqp_ref, dkp_ref, dmv_ref,
             cq_ref, ckv_ref, qn_ref, kvn_ref, win_ref, wuq_ref, wkv_ref, gq_ref, gkv_ref, tc_ref, ts1_ref, ts2_ref,
             gx_ref, gin_ref, guq_ref, gwkv_ref, bst_ref, wst_ref):
        _stat_init(bst_ref, wst_ref)

        @pl.when((pl.program_id(0) == 0) & (pl.program_id(1) == 0))
        def _():
            gin_ref[...] = jnp.zeros_like(gin_ref)
            guq_ref[...] = jnp.zeros_like(guq_ref)
            gwkv_ref[...] = jnp.zeros_like(gwkv_ref)

        mod = mod_ref[0]
        c1, s1, s2 = tc_ref[...], ts1_ref[...], ts2_ref[...]
        c8, s18, s28 = jnp.tile(c1, (1, nh)), jnp.tile(s1, (1, nh)), jnp.tile(s2, (1, nh))
        dqpre = _rope_t(dqp_ref[0].astype(F32), c8, s18, s28).astype(BF16)
        guq_ref[...] += _dot_tn(qn_ref[0], dqpre)
        gq = gq_ref[...]
        cq = cq_ref[0]
        rq = lax.rsqrt(jnp.mean(cq * cq, axis=-1, keepdims=True) + RMS_EPS)
        dqn = _dot_nt(dqpre, wuq_ref[...])
        wst_ref[4:5, 0:qr] += _colsum(dqn * cq * rq)
        dqg = dqn * gq
        dcq = rq * dqg - cq * (rq * rq * rq) * jnp.mean(dqg * cq, axis=-1, keepdims=True)

        dkpre = _rope_t(dkp_ref[0].astype(F32), c8, s18, s28)
        dkr = dkpre[:, 0:HEAD_PAD]
        for h in range(1, nh):
            dkr = dkr + dkpre[:, h * HEAD_PAD:(h + 1) * HEAD_PAD]
        lane = lax.broadcasted_iota(jnp.int32, (tm, LANES), 1)
        dkr = jnp.where((lane >= MLA_NOPE) & (lane < MLA_NOPE + MLA_ROPE), dkr, 0.0)
        dkr = pltpu.roll(dkr, LANES - MLA_NOPE, 1)
        dkvo = jnp.concatenate([dkpre.astype(BF16), dmv_ref[0]], axis=1)
        gwkv_ref[...] += _dot_tn(kvn_ref[0], dkvo)
        gkv = gkv_ref[...]
        ckv = ckv_ref[0]
        rkv = lax.rsqrt(jnp.mean(ckv * ckv, axis=-1, keepdims=True) + RMS_EPS)
        dkvn = _dot_nt(dkvo, wkv_ref[...])
        wst_ref[5:6, 0:kvr] += _colsum(dkvn * ckv * rkv)
        dkg = dkvn * gkv
        dckv = rkv * dkg - ckv * (rkv * rkv * rkv) * jnp.mean(dkg * ckv, axis=-1, keepdims=True)

        dproj = jnp.concatenate([dq_ref[0], dk_ref[0], dv_ref[0], dcq.astype(BF16), dckv.astype(BF16),
                                 dkr.astype(BF16)], axis=1)
        dh = _dot(dproj, win_ref[...])
        x0 = x0_ref[0]
        gin_ref[...] += _dot_tn(dproj, (x0 * (1.0 + mod[1:2]) + mod[0:1]).astype(BF16))
        dx0 = dx0a_ref[0] + dh * (1.0 + mod[1:2])
        bst_ref[0, 0:1, :] += _colsum(dh * x0)
        bst_ref[0, 1:2, :] += _colsum(dh)
        g = g_ref[...]
        _, xhat, rstd = _ln_fwd(x_ref[0], g, 0.0)
        gx_ref[0] = _ln_bwd(dx0, xhat, rstd, g)
        wst_ref[0:1, :] += _colsum(dx0 * xhat)
        wst_ref[1:2, :] += _colsum(dx0)

    tab = pl.BlockSpec((tm, LANES), lambda b, s: (s, 0))
    sspecs, sshapes = _stat_specs(B, D)
    return pl.pallas_call(
        body, name="inproj_bwd", grid=(B, S // tm),
        in_specs=[_tok(tm, D), _tok(tm, D), _tok(tm, D), _perb(N_MOD, D), _full(ln_g),
                  _tok(tm, sbw), _tok(tm, sbw), _tok(tm, sbw), _tok(tm, qpw), _tok(tm, qpw), _tok(tm, nh * MLA_V),
                  _tok(tm, qr), _tok(tm, kvr), _tok(tm, qr), _tok(tm, kvr),
                  _full(w_in_p), _full(w_uq_p), _full(w_kv), _full(gq), _full(gkv), tab, tab, tab],
        out_specs=[_tok(tm, D), pl.BlockSpec((dinp, D), lambda b, s: (0, 0)),
                   pl.BlockSpec((qr, qpw), lambda b, s: (0, 0)), pl.BlockSpec((kvr, kvw), lambda b, s: (0, 0))] + sspecs,
        out_shape=[_sds((B, S, D), F32), _sds((dinp, D), F32), _sds((qr, qpw), F32),
                   _sds((kvr, kvw), F32)] + sshapes,
        compiler_params=_cparams(("arbitrary", "arbitrary")),
    )(x, x0, dx0a, mod, ln_g, dq, dk, dv, dqp, dkp, dmv, cq, ckv, qn, kvn, w_in_p, w_uq_p, w_kv, gq, gkv,
      tc, ts1, ts2)


def _tile_of(n, cap):
    if n <= cap:
        return n
    best = n
    for t in range(LANES, cap + 1, LANES):
        if n % t == 0:
            best = t
    return best


def _mm_tn(a, g, name, after, relu_sq=False, out_dtype=F32, col_blocks=None):
    T, K = a.shape
    N = g.shape[1]
    tt = 2048 if T % 2048 == 0 else (512 if T % 512 == 0 else T)
    tk = _tile_of(K, 1024)
    tn = _tile_of(N, 1280)
    nt = T // tt
    bw = N // col_blocks if col_blocks else tn
    assert tn % bw == 0

    def body(a_ref, g_ref, _, o_ref, acc_ref):
        @pl.when(pl.program_id(2) == 0)
        def _():
            acc_ref[...] = jnp.zeros_like(acc_ref)

        av = a_ref[...]
        if relu_sq:
            av = jnp.square(jnp.maximum(av.astype(F32), 0.0)).astype(BF16)
        acc_ref[...] += _dot_tn(av, g_ref[...])

        @pl.when(pl.program_id(2) == nt - 1)
        def _():
            if col_blocks:
                for c in range(tn // bw):
                    o_ref[c] = acc_ref[:, c * bw:(c + 1) * bw].astype(out_dtype)
            else:
                o_ref[...] = acc_ref[...].astype(out_dtype)

    if col_blocks:
        out_spec = pl.BlockSpec((tn // bw, tk, bw), lambda i, j, t: (j, i, 0))
        out_shape = _sds((col_blocks, K, bw), out_dtype)
    else:
        out_spec = pl.BlockSpec((tk, tn), lambda i, j, t: (i, j))
        out_shape = _sds((K, N), out_dtype)
    return pl.pallas_call(
        body, name=name, grid=(K // tk, N // tn, nt),
        in_specs=[pl.BlockSpec((tt, tk), lambda i, j, t: (t, i)), pl.BlockSpec((tt, tn), lambda i, j, t: (t, j)),
                  _AFTER],
        out_specs=out_spec, out_shape=out_shape,
        scratch_shapes=[pltpu.VMEM((tk, tn), F32)],
        compiler_params=_cparams(("parallel", "parallel", "arbitrary")),
    )(a, g, after)


def _reduce_adamw(parts, w, m, v, name):
    P, K, N = parts.shape
    tr = 256 if K % 256 == 0 else K

    def body(p_ref, w_ref, m_ref, v_ref, g_ref, d_ref, nm_ref, nv_ref):
        g = p_ref[0].astype(F32)
        for k in range(1, P):
            g = g + p_ref[k].astype(F32)
        g_ref[0] = g
        d_ref[0], nm_ref[0], nv_ref[0] = _adamw(w_ref[0], g, m_ref[0], v_ref[0])

    spec = pl.BlockSpec((1, tr, N), lambda r: (0, r, 0))
    return pl.pallas_call(
        body, name=name, grid=(K // tr,),
        in_specs=[pl.BlockSpec((P, tr, N), lambda r: (0, r, 0)), spec, spec, spec],
        out_specs=[spec] * 4, out_shape=[_sds((1, K, N), F32)] * 4,
        compiler_params=_cparams(("parallel",)),
    )(parts, w, m, v)


def _finish(sm, dmod_all, dmod_my, cact_all, p_small, m_small, v_small, b_ada, m_b, v_b, w_ada, m_w, v_w):
    n0 = p_small.shape[1]
    n1 = sm.shape[1]
    d = cact_all.shape[1]

    def body(sm_ref, dma_ref, dmm_ref, ca_ref, p_ref, pm_ref, pv_ref, b_ref, bm_ref, bv_ref, w_ref, wm_ref, wv_ref,
             gs_ref, ds_ref, ms_ref, vs_ref, gb_ref, db_ref, mb_ref, vb_ref, gw_ref, dw_ref, mw_ref, vw_ref,
             loss_ref):
        gs = sm_ref[0:1, :]
        for k in range(1, N_DEV):
            gs = gs + sm_ref[k:k + 1, :]
        gs_ref[...] = gs
        ds_ref[...], ms_ref[...], vs_ref[...] = _adamw(p_ref[...], gs[:, 0:n0], pm_ref[...], pv_ref[...])
        loss_ref[...] = jnp.zeros((1, LANES), F32) + jnp.sum(gs[:, n1 - d:n1])
        gb = jnp.sum(dma_ref[...], axis=0, keepdims=True)
        gb_ref[...] = gb
        db_ref[...], mb_ref[...], vb_ref[...] = _adamw(b_ref[...], gb, bm_ref[...], bv_ref[...])
        gw = _dot_tn(ca_ref[...].astype(BF16), dmm_ref[...].astype(BF16))
        gw_ref[...] = gw
        dw_ref[...], mw_ref[...], vw_ref[...] = _adamw(w_ref[...], gw, wm_ref[...], wv_ref[...])

    s0 = _sds(p_small.shape, F32)
    sb = _sds(b_ada.shape, F32)
    sw = _sds(w_ada.shape, F32)
    return pl.pallas_call(
        body, name="finish_small",
        out_shape=[_sds((1, n1), F32), s0, s0, s0, sb, sb, sb, sb, sw, sw, sw, sw,
                   _sds((1, LANES), F32)],
        compiler_params=pltpu.CompilerParams(vmem_limit_bytes=VMEM_LIMIT),
    )(sm, dmod_all, dmod_my, cact_all, p_small, m_small, v_small, b_ada, m_b, v_b, w_ada, m_w, v_w)


def _pack(arrs, dtype, width):
    flat = jnp.concatenate([a.astype(dtype).reshape(-1) for a in arrs])
    rows = -(-flat.shape[0] // (256 * width)) * 256
    return jnp.pad(flat, (0, rows * width - flat.shape[0])).reshape(rows, width)


def _unpack(slab, shapes):
    flat = slab.reshape(-1)
    out, o = [], 0
    for s in shapes:
        n = math.prod(s)
        out.append(flat[o:o + n].reshape(s))
        o += n
    return out


def _rope_tables(S):
    inv_freq = 1.0 / (ROPE_BASE ** (jnp.arange(0, MLA_ROPE, 2, dtype=F32) / MLA_ROPE))
    ang = jnp.arange(S, dtype=F32)[:, None] * inv_freq[None, :]
    cos, sin = jnp.cos(ang), jnp.sin(ang)
    one = jnp.ones((S, MLA_NOPE), F32)
    z16 = jnp.zeros((S, 16), F32)
    z32 = jnp.zeros((S, 32), F32)
    z64 = jnp.zeros((S, MLA_NOPE), F32)
    tc = jnp.concatenate([one, cos, cos, jnp.ones((S, 32), F32)], axis=1)
    ts1 = jnp.concatenate([z64, -sin, z16, z32], axis=1)
    ts2 = jnp.concatenate([z64, z16, sin, z32], axis=1)
    return tc, ts1, ts2


def kernel(x, c, ln_in_g, ln_in_b, w_ada, b_ada, w_in, q_norm_g, kv_norm_g, w_uq, w_ukv, w_o, ln1_g, ln1_b, w_up, w_down, ln2_g, ln2_b, loss_target, m_ln_in_g, m_ln_in_b, m_w_ada, m_b_ada, m_w_in, m_q_norm_g, m_kv_norm_g, m_w_uq, m_w_ukv, m_w_o, m_ln1_g, m_ln1_b, m_w_up, m_w_down, m_ln2_g, m_ln2_b, v_ln_in_g, v_ln_in_b, v_w_ada, v_b_ada, v_w_in, v_q_norm_g, v_kv_norm_g, v_w_uq, v_w_ukv, v_w_o, v_ln1_g, v_ln1_b, v_w_up, v_w_down, v_ln2_g, v_ln2_b):
    B, S, D = x.shape
    sbw = D // 2
    mlw = D - sbw
    nh = mlw // MLA_V
    qr = w_uq.shape[1]
    kvr = w_ukv.shape[1]
    qk = MLA_NOPE + MLA_ROPE
    dff = w_up.shape[2] * N_DEV
    din = w_in.shape[2] * N_DEV
    tm = 512 if S % 512 == 0 else S
    tq = min(512, S // 2)
    dm = dict(tm=tm, tm_small=min(tm, 256), tq=tq, sbw=sbw, qr=qr, kvr=kvr, nh=nh)
    dev =4 * lax.axis_index("x") + 2 * lax.axis_index("y") + lax.axis_index("c")

    big = [w_in, w_uq, w_ukv, w_o, w_up, w_down]
    first = [w_in[0].T.astype(BF16), w_uq[0].astype(BF16), w_ukv[0].astype(BF16)]
    first_w, first_token = _chip_exchange_start(first, "gather_w_first_start", scatter=False, after=c)

    nada = w_ada.shape[2]
    c_all = _all_gather([c + first_token[0, 0]], "gather_c")[0].reshape(N_DEV * B, D)
    b_loc = lax.dynamic_slice(b_ada, (0, dev * nada), (1, nada))
    cact_all, mod_part = _ada_partial(c_all, w_ada[0], b_loc)
    mod_all = _all_gather([mod_part], "gather_mod")[0]
    mod = lax.dynamic_slice(mod_all, (0, dev * B, 0), (N_DEV, B, nada))
    mod = jnp.swapaxes(mod, 0, 1).reshape(B, N_MOD, D)

    first_by_chip = _chip_exchange_wait(first_w, mod_all, "gather_w_first_wait")
    w_in8, w_uq8, w_ukv8 = [b.reshape((N_DEV,) + b.shape[2:]) for b in _core_gather(first_by_chip, "gather_w_first_cores")]
    late_w, late_token = _chip_exchange_start([a[0].astype(BF16) for a in big[3:]], "gather_w_late_start",
                                              scatter=False, after=w_in8, everyone=True)
    cols = lambda a8: jnp.swapaxes(a8, 0, 1).reshape(a8.shape[1], N_DEV * a8.shape[2])
    w_in_p = jnp.pad(w_in8.reshape(din, D), ((0, LANES - MLA_ROPE), (0, 0)))
    zpad = jnp.zeros((qr, nh, HEAD_PAD - qk), BF16)
    w_uq_p = jnp.concatenate([cols(w_uq8).reshape(qr, nh, qk), zpad], axis=2).reshape(qr, nh * HEAD_PAD)
    w_ukv_f = cols(w_ukv8)
    w_uk = w_ukv_f[:, :nh * MLA_NOPE].reshape(kvr, nh, MLA_NOPE)
    w_uk_p = jnp.concatenate([w_uk, jnp.zeros((kvr, nh, HEAD_PAD - MLA_NOPE), BF16)], axis=2)
    w_kv = jnp.concatenate([w_uk_p.reshape(kvr, nh * HEAD_PAD), w_ukv_f[:, nh * MLA_NOPE:]], axis=1)

    tc, ts1, ts2 = _rope_tables(S)
    g_in, b_in = ln_in_g.reshape(1, D), ln_in_b.reshape(1, D)
    (x0, sq, sk, sv, qp, kp, mv, cq, ckv, qn, kvn) = _inproj_fwd(
        x, mod, g_in, b_in, w_in_p, w_uq_p, w_kv, q_norm_g, kv_norm_g, tc, ts1, ts2, dm, late_token)
    sb_y, sb_tot = _sb_fwd(sq, sk, sv, dm)
    mla_y, mla_lse = _mla_fwd(qp, kp, mv, dm, sb_tot)
    w_o8, w_up8, w_down8 = _chip_exchange_wait(late_w, mla_lse, "gather_w_late_wait")
    w_o_f = w_o8.reshape(D, D)
    mix, x1, h2, u, dr2, bst_c, wst_c = _mlp_fwd(sb_y, mla_y, x0, mod, loss_target, w_o_f, w_up8, w_down8,
                                                  ln1_g, ln1_b, ln2_g, ln2_b, dm)

    du, dffb, dx0a, dsb_y, dmla_y, g_o, bst_b, wst_b = _mlp_bwd(
        dr2, u, x1, x0, mix, sb_y, mla_y, mod, w_up8, w_down8, w_o_f, ln1_g, dm)
    T = B * S
    r2 = lambda a: a.reshape(T, a.shape[2])
    by_core = lambda a: a.reshape((4, 2) + a.shape[1:])
    g_o = g_o.astype(BF16)
    g_up8 = _mm_tn(r2(h2), r2(du), "grad_w_up", dr2, out_dtype=BF16, col_blocks=N_DEV)
    g_down = _mm_tn(r2(u), r2(dffb), "grad_w_down", dr2, relu_sq=True, out_dtype=BF16)
    early = [g_o.reshape(N_DEV, D // N_DEV, D), g_up8, g_down.reshape(N_DEV, dff // N_DEV, D)]
    early_g, early_token = _chip_exchange_start(early, "scatter_g_early_start", scatter=True, after=dr2,
                                                everyone=True)

    dsq, dsk, dsv = _sb_bwd(sq, sk, sv, sb_tot, dsb_y, dm, early_token)
    dqp, dkp, dmv = _mla_bwd(qp, kp, mv, mla_y, mla_lse, dmla_y, dm, dsq)
    grad_x, g_in_p, g_uq_p, g_kv, bst_a, wst_a = _inproj_bwd(
        x, x0, dx0a, mod, g_in, dsq, dsk, dsv, dqp, dkp, dmv, cq, ckv, qn, kvn, w_in_p, w_uq_p, w_kv,
        q_norm_g, kv_norm_g, tc, ts1, ts2, dm)

    dmod = jnp.concatenate([bst_a[:, 1], bst_a[:, 0], bst_b[:, 2], bst_b[:, 1], bst_b[:, 0], bst_c[:, 0]], axis=1)
    small = jnp.concatenate([wst_a[0], wst_a[1], wst_a[4, :qr], wst_a[5, :kvr], wst_b[0], wst_b[1],
                             wst_c[0], wst_c[1], wst_c[2]])
    n1 = small.shape[0]
    small_g, small_token = _chip_exchange_start([_pack([dmod, small], F32, LANES)], "gather_small_start",
                                                scatter=False, after=grad_x)
    g_uq_f = g_uq_p.reshape(qr, nh, HEAD_PAD)[:, :, :qk].reshape(qr, nh * qk)
    g_uk = g_kv[:, :nh * HEAD_PAD].reshape(kvr, nh, HEAD_PAD)[:, :, :MLA_NOPE].reshape(kvr, nh * MLA_NOPE)
    g_ukv_f = jnp.concatenate([g_uk, g_kv[:, nh * HEAD_PAD:]], axis=1)
    early_quarter = _chip_exchange_wait(early_g, g_in_p, "scatter_g_early_wait")

    def by_dest_cols(a):
        k, n = a.shape[0], a.shape[1] // N_DEV
        return jnp.swapaxes(a.reshape(k, N_DEV, n), 0, 1).astype(BF16)

    g_in8 = (g_in_p[:din] + small_token[0, 0]).astype(BF16).reshape(N_DEV, din // N_DEV, D)
    last = [g_in8, by_dest_cols(g_uq_f), by_dest_cols(g_ukv_f)]
    last_sum = _core_scatter_sum([by_core(a) for a in last], "scatter_g_last_cores")
    small_by_chip = _chip_exchange_wait(small_g, last_sum[0], "gather_small_wait")
    both = _core_gather(small_by_chip, "gather_small_cores")[0].reshape(N_DEV, -1)
    last_g, last_token = _chip_exchange_start(last_sum, "scatter_g_last_start", scatter=True, after=grad_x)
    names = ["w_in", "w_uq", "w_ukv", "w_o", "w_up", "w_down"]
    moms = [m_w_in, m_w_uq, m_w_ukv, m_w_o, m_w_up, m_w_down]
    vars_ = [v_w_in, v_w_uq, v_w_ukv, v_w_o, v_w_up, v_w_down]
    res_early = [_reduce_adamw(p, w, m, v, "adamw_" + n)
                 for p, w, m, v, n in zip(early_quarter, big[3:], moms[3:], vars_[3:], names[3:])]

    dmod_all = both[:, :B * N_MOD * D].reshape(N_DEV * B, N_MOD * D)
    sm = both[:, B * N_MOD * D:B * N_MOD * D + n1] + last_token[0, 0]
    dmod_my = lax.dynamic_slice(dmod_all, (0, dev * nada), (N_DEV * B, nada))
    row = lambda arrs: jnp.concatenate([a.reshape(1, -1) for a in arrs], axis=1)
    smalls = [ln_in_g, ln_in_b, q_norm_g, kv_norm_g, ln1_g, ln1_b, ln2_g, ln2_b]
    small_shapes = [a.shape for a in smalls]
    (gs, ds, nms, nvs, g_b, d_b, nm_b, nv_b, g_w, d_w, nm_w, nv_w, loss_v) = _finish(
        sm, dmod_all, dmod_my, cact_all, row(smalls),
        row([m_ln_in_g, m_ln_in_b, m_q_norm_g, m_kv_norm_g, m_ln1_g, m_ln1_b, m_ln2_g, m_ln2_b]),
        row([v_ln_in_g, v_ln_in_b, v_q_norm_g, v_kv_norm_g, v_ln1_g, v_ln1_b, v_ln2_g, v_ln2_b]),
        b_ada, m_b_ada, v_b_ada, w_ada[0], m_w_ada[0], v_w_ada[0])
    gsm, dsm, nmsm, nvsm = (_unpack(s, small_shapes) for s in (gs, ds, nms, nvs))
    last_quarter = list(_chip_exchange_wait(last_g, loss_v, "scatter_g_last_wait"))
    last_quarter[0] = jnp.swapaxes(last_quarter[0], 1, 2)
    res_last =[_reduce_adamw(p, w, m, v, "adamw_" + n)
                for p, w, m, v, n in zip(last_quarter, big[:3], moms[:3], vars_[:3], names[:3])]
    gb, db, nmb, nvb = ([r[i] for r in res_last + res_early] for i in range(4))

    def ordered(sm_l, w_l, ada_w, ada_b):
        return [sm_l[0], sm_l[1], ada_w[None], ada_b, w_l[0], sm_l[2], sm_l[3], w_l[1], w_l[2], w_l[3],
                sm_l[4], sm_l[5], w_l[4], w_l[5], sm_l[6], sm_l[7]]

    loss = loss_v[0, 0]
    return (loss, grad_x, *ordered(gsm, gb, g_w, g_b), *ordered(dsm, db, d_w, d_b),
            *ordered(nmsm, nmb, nm_w, nm_b), *ordered(nvsm, nvb, nv_w, nv_b))
```
